```python
import math
import jax, jax.numpy as jnp
from jax import lax
import numpy as np

D_MODEL = 1024
BATCH = 8
SEQ = 8192
DEPTH = 1

EPS = 1e-6
Q_BLOCK = 128
MLA_HEADS = 8
MLA_NOPE_DIM = 64
MLA_ROPE_DIM = 32
MLA_V_DIM = 64
Q_LORA_RANK = 256
KV_LORA_RANK = 128
ROPE_THETA = 10000.0
MLA_QK_DIM = MLA_NOPE_DIM + MLA_ROPE_DIM
SB_HEADS = 8
SB_HEAD_DIM = 64
MLA_WIDTH = MLA_HEADS * MLA_V_DIM
SB_WIDTH = SB_HEADS * SB_HEAD_DIM
MIX_WIDTH = MLA_WIDTH + SB_WIDTH
IN_SPLITS = (Q_LORA_RANK, KV_LORA_RANK, MLA_ROPE_DIM, SB_WIDTH, SB_WIDTH, SB_WIDTH)
IN_PROJ_WIDTH = sum(IN_SPLITS)
IN_SPLIT_POINTS = tuple(int(v) for v in np.cumsum(IN_SPLITS)[:-1])
D_FF = ((8 * D_MODEL + 3 * 256 - 1) // (3 * 256)) * 256

kernel_name = "hymba_mla_stickbreaking_swiglu"


def rmsnorm(x, g):
    xf = x.astype(jnp.float32)
    y = xf * lax.rsqrt(jnp.mean(xf * xf, axis=-1, keepdims=True) + EPS)
    return (y * g.astype(jnp.float32)).astype(x.dtype)


def rope_tables(positions, dim):
    inv_freq = ROPE_THETA ** (-jnp.arange(0, dim, 2, dtype=jnp.float32) / dim)
    ang = positions.astype(jnp.float32)[:, :, None] * inv_freq[None, None, :]
    return jnp.cos(ang)[:, None], jnp.sin(ang)[:, None]


def apply_rope(x, cos, sin):
    xf = x.astype(jnp.float32)
    x1, x2 = jnp.split(xf, 2, axis=-1)
    out = jnp.concatenate([x1 * cos - x2 * sin, x2 * cos + x1 * sin], axis=-1)
    return out.astype(x.dtype)


def to_query_blocks(q):
    b, h, s, d = q.shape
    return q.reshape(b, h, s // Q_BLOCK, Q_BLOCK, d).transpose(2, 0, 1, 3, 4)


def from_query_blocks(o):
    nb, b, h, qb, d = o.shape
    return o.transpose(1, 2, 0, 3, 4).reshape(b, h, nb * qb, d)


def mla_causal_attention(q, k, v):
    seq = q.shape[2]
    scale = 1.0 / math.sqrt(q.shape[-1])
    kf = k.astype(jnp.float32)
    vf = v.astype(jnp.float32)
    k_pos = jnp.arange(seq)
    nb = seq // Q_BLOCK

    def block(args):
        i, qb = args
        s = jnp.einsum("bhqd,bhkd->bhqk", qb.astype(jnp.float32), kf) * scale
        q_pos = i * Q_BLOCK + jnp.arange(Q_BLOCK)
        causal = k_pos[None, :] <= q_pos[:, None]
        p = jax.nn.softmax(jnp.where(causal, s, -jnp.inf), axis=-1)
        return jnp.einsum("bhqk,bhkd->bhqd", p, vf)

    o = lax.map(block, (jnp.arange(nb), to_query_blocks(q)))
    return from_query_blocks(o).astype(q.dtype)


def stick_breaking_attention(q, k, v):
    seq = q.shape[2]
    scale = 1.0 / math.sqrt(q.shape[-1])
    kf = k.astype(jnp.float32)
    vf = v.astype(jnp.float32)
    k_pos = jnp.arange(seq)
    nb = seq // Q_BLOCK

    def block(args):
        i, qb = args
        z = jnp.einsum("bhqd,bhkd->bhqk", qb.astype(jnp.float32), kf) * scale
        q_pos = i * Q_BLOCK + jnp.arange(Q_BLOCK)
        strict = k_pos[None, :] < q_pos[:, None]
        log_beta = jax.nn.log_sigmoid(z)
        log_one_minus = jnp.where(strict, jax.nn.log_sigmoid(-z), 0.0)
        suffix = lax.cumsum(log_one_minus, axis=3, reverse=True) - log_one_minus
        a = jnp.where(strict, jnp.exp(log_beta + suffix), 0.0)
        return jnp.einsum("bhqk,bhkd->bhqd", a, vf)

    o = lax.map(block, (jnp.arange(nb), to_query_blocks(q)))
    return from_query_blocks(o).astype(q.dtype)


def split_heads(t, n_heads):
    b, s, _ = t.shape
    return t.reshape(b, s, n_heads, -1).transpose(0, 2, 1, 3)


def merge_heads(t):
    b, h, s, d = t.shape
    return t.transpose(0, 2, 1, 3).reshape(b, s, h * d)


def _fwd_setup_inputs(seed: int = 0) -> dict:
    key = jax.random.key(seed)
    ks = jax.random.split(key, 20)
    f32 = jnp.float32

    def w(k, shape, fan_in):
        return jax.random.normal(k, shape, f32) * (fan_in ** -0.5)

    def gain(k, shape):
        return 1.0 + 0.02 * jax.random.normal(k, shape, f32)

    x = jax.random.normal(ks[0], (BATCH, SEQ, D_MODEL), f32)
    positions = jnp.broadcast_to(jnp.arange(SEQ, dtype=jnp.int32), (BATCH, SEQ))
    return {
        "x": x,
        "positions": positions,
        "norm_mix": gain(ks[1], (DEPTH, D_MODEL)),
        "w_in": w(ks[2], (DEPTH, D_MODEL, IN_PROJ_WIDTH), D_MODEL),
        "q_latent_norm": gain(ks[3], (DEPTH, Q_LORA_RANK)),
        "w_uq": w(ks[4], (DEPTH, Q_LORA_RANK, MLA_HEADS * MLA_QK_DIM), Q_LORA_RANK),
        "kv_latent_norm": gain(ks[5], (DEPTH, KV_LORA_RANK)),
        "w_ukv": w(ks[6], (DEPTH, KV_LORA_RANK, MLA_HEADS * (MLA_NOPE_DIM + MLA_V_DIM)), KV_LORA_RANK),
        "out_norm_mla": gain(ks[7], (DEPTH, MLA_WIDTH)),
        "out_norm_sb": gain(ks[8], (DEPTH, SB_WIDTH)),
        "w_o": w(ks[9], (DEPTH, MIX_WIDTH, D_MODEL), MIX_WIDTH),
        "norm_ffn": gain(ks[10], (DEPTH, D_MODEL)),
        "w_gate": w(ks[11], (DEPTH, D_MODEL, D_FF), D_MODEL),
        "w_up": w(ks[12], (DEPTH, D_MODEL, D_FF), D_MODEL),
        "w_down": w(ks[13], (DEPTH, D_FF, D_MODEL), D_FF),
        "norm_final": gain(ks[14], (D_MODEL,)),
    }


def _fwd_reference(x, positions, norm_mix, w_in, q_latent_norm, w_uq, kv_latent_norm,
              w_ukv, out_norm_mla, out_norm_sb, w_o, norm_ffn, w_gate, w_up,
              w_down, norm_final):
    b, s, _ = x.shape
    cos, sin = rope_tables(positions, MLA_ROPE_DIM)
    h = x
    for l in range(DEPTH):
        u = rmsnorm(h, norm_mix[l])
        proj = jnp.einsum("bsd,de->bse", u, w_in[l])
        c_q, c_kv, k_r, q_sb, k_sb, v_sb = jnp.split(proj, IN_SPLIT_POINTS, axis=-1)

        q = split_heads(jnp.einsum("bsr,re->bse", rmsnorm(c_q, q_latent_norm[l]), w_uq[l]), MLA_HEADS)
        q_nope, q_rope = q[..., :MLA_NOPE_DIM], q[..., MLA_NOPE_DIM:]
        q_rope = apply_rope(q_rope, cos, sin)
        kv = split_heads(jnp.einsum("bsr,re->bse", rmsnorm(c_kv, kv_latent_norm[l]), w_ukv[l]), MLA_HEADS)
        k_nope, v_mla = kv[..., :MLA_NOPE_DIM], kv[..., MLA_NOPE_DIM:]
        k_rope = apply_rope(k_r[:, None], cos, sin)
        q_mla = jnp.concatenate([q_nope, q_rope], axis=-1)
        k_mla = jnp.concatenate([k_nope, jnp.broadcast_to(k_rope, (b, MLA_HEADS, s, MLA_ROPE_DIM))], axis=-1)
        o_mla = merge_heads(mla_causal_attention(q_mla, k_mla, v_mla))

        o_sb = merge_heads(stick_breaking_attention(
            split_heads(q_sb, SB_HEADS), split_heads(k_sb, SB_HEADS), split_heads(v_sb, SB_HEADS)))

        merged = jnp.concatenate([rmsnorm(o_mla, out_norm_mla[l]), rmsnorm(o_sb, out_norm_sb[l])], axis=-1)
        h = h + jnp.einsum("bse,ed->bsd", merged, w_o[l])

        f = rmsnorm(h, norm_ffn[l])
        gate = jnp.einsum("bsd,df->bsf", f, w_gate[l])
        up = jnp.einsum("bsd,df->bsf", f, w_up[l])
        h = h + jnp.einsum("bsf,fd->bsd", jax.nn.silu(gate) * up, w_down[l])
    return rmsnorm(h, norm_final)


import jax as _jax
import jax.numpy as _jnp

TWIN_FORMAT = 'train_step'
FWD_PARAMS = ['x', 'positions', 'norm_mix', 'w_in', 'q_latent_norm', 'w_uq', 'kv_latent_norm', 'w_ukv', 'out_norm_mla', 'out_norm_sb', 'w_o', 'norm_ffn', 'w_gate', 'w_up', 'w_down', 'norm_final']
TWIN_WEIGHTS = ['norm_mix', 'w_in', 'q_latent_norm', 'w_uq', 'kv_latent_norm', 'w_ukv', 'out_norm_mla', 'out_norm_sb', 'w_o', 'norm_ffn', 'w_gate', 'w_up', 'w_down', 'norm_final']
TWIN_DIFF_INPUT = 'x'
TWIN_INPUTS = ['x', 'positions', 'norm_mix', 'w_in', 'q_latent_norm', 'w_uq', 'kv_latent_norm', 'w_ukv', 'out_norm_mla', 'out_norm_sb', 'w_o', 'norm_ffn', 'w_gate', 'w_up', 'w_down', 'norm_final', 'loss_target', 'm_norm_mix', 'm_w_in', 'm_q_latent_norm', 'm_w_uq', 'm_kv_latent_norm', 'm_w_ukv', 'm_out_norm_mla', 'm_out_norm_sb', 'm_w_o', 'm_norm_ffn', 'm_w_gate', 'm_w_up', 'm_w_down', 'm_norm_final', 'v_norm_mix', 'v_w_in', 'v_q_latent_norm', 'v_w_uq', 'v_kv_latent_norm', 'v_w_ukv', 'v_out_norm_mla', 'v_out_norm_sb', 'v_w_o', 'v_norm_ffn', 'v_w_gate', 'v_w_up', 'v_w_down', 'v_norm_final']
TWIN_OUTPUTS = ['loss', 'grad_x', 'grad_norm_mix', 'grad_w_in', 'grad_q_latent_norm', 'grad_w_uq', 'grad_kv_latent_norm', 'grad_w_ukv', 'grad_out_norm_mla', 'grad_out_norm_sb', 'grad_w_o', 'grad_norm_ffn', 'grad_w_gate', 'grad_w_up', 'grad_w_down', 'grad_norm_final', 'delta_norm_mix', 'delta_w_in', 'delta_q_latent_norm', 'delta_w_uq', 'delta_kv_latent_norm', 'delta_w_ukv', 'delta_out_norm_mla', 'delta_out_norm_sb', 'delta_w_o', 'delta_norm_ffn', 'delta_w_gate', 'delta_w_up', 'delta_w_down', 'delta_norm_final', 'new_m_norm_mix', 'new_m_w_in', 'new_m_q_latent_norm', 'new_m_w_uq', 'new_m_kv_latent_norm', 'new_m_w_ukv', 'new_m_out_norm_mla', 'new_m_out_norm_sb', 'new_m_w_o', 'new_m_norm_ffn', 'new_m_w_gate', 'new_m_w_up', 'new_m_w_down', 'new_m_norm_final', 'new_v_norm_mix', 'new_v_w_in', 'new_v_q_latent_norm', 'new_v_w_uq', 'new_v_kv_latent_norm', 'new_v_w_ukv', 'new_v_out_norm_mla', 'new_v_out_norm_sb', 'new_v_w_o', 'new_v_norm_ffn', 'new_v_w_gate', 'new_v_w_up', 'new_v_w_down', 'new_v_norm_final']
TWIN_LEAF_KINDS = {'loss': 'loss', 'grad_x': 'grad_x', 'grad_norm_mix': 'grad_w', 'grad_w_in': 'grad_w', 'grad_q_latent_norm': 'grad_w', 'grad_w_uq': 'grad_w', 'grad_kv_latent_norm': 'grad_w', 'grad_w_ukv': 'grad_w', 'grad_out_norm_mla': 'grad_w', 'grad_out_norm_sb': 'grad_w', 'grad_w_o': 'grad_w', 'grad_norm_ffn': 'grad_w', 'grad_w_gate': 'grad_w', 'grad_w_up': 'grad_w', 'grad_w_down': 'grad_w', 'grad_norm_final': 'grad_w', 'delta_norm_mix': 'delta_w', 'delta_w_in': 'delta_w', 'delta_q_latent_norm': 'delta_w', 'delta_w_uq': 'delta_w', 'delta_kv_latent_norm': 'delta_w', 'delta_w_ukv': 'delta_w', 'delta_out_norm_mla': 'delta_w', 'delta_out_norm_sb': 'delta_w', 'delta_w_o': 'delta_w', 'delta_norm_ffn': 'delta_w', 'delta_w_gate': 'delta_w', 'delta_w_up': 'delta_w', 'delta_w_down': 'delta_w', 'delta_norm_final': 'delta_w', 'new_m_norm_mix': 'new_m', 'new_m_w_in': 'new_m', 'new_m_q_latent_norm': 'new_m', 'new_m_w_uq': 'new_m', 'new_m_kv_latent_norm': 'new_m', 'new_m_w_ukv': 'new_m', 'new_m_out_norm_mla': 'new_m', 'new_m_out_norm_sb': 'new_m', 'new_m_w_o': 'new_m', 'new_m_norm_ffn': 'new_m', 'new_m_w_gate': 'new_m', 'new_m_w_up': 'new_m', 'new_m_w_down': 'new_m', 'new_m_norm_final': 'new_m', 'new_v_norm_mix': 'new_v', 'new_v_w_in': 'new_v', 'new_v_q_latent_norm': 'new_v', 'new_v_w_uq': 'new_v', 'new_v_kv_latent_norm': 'new_v', 'new_v_w_ukv': 'new_v', 'new_v_out_norm_mla': 'new_v', 'new_v_out_norm_sb': 'new_v', 'new_v_w_o': 'new_v', 'new_v_norm_ffn': 'new_v', 'new_v_w_gate': 'new_v', 'new_v_w_up': 'new_v', 'new_v_w_down': 'new_v', 'new_v_norm_final': 'new_v'}


def _forward(args):
    return _fwd_reference(*[args[k] for k in FWD_PARAMS])


def _output_shape():
    def fwd():
        inp = _fwd_setup_inputs(0)
        return _fwd_reference(*[inp[k] for k in FWD_PARAMS])
    out = _jax.eval_shape(fwd)
    return out.shape, out.dtype

N_MICROBATCH = 1
ADAM_LR = 0.001
ADAM_B1 = 0.9
ADAM_B2 = 0.999
ADAM_EPS = 1e-08
ADAM_WD = 0.01
ADAM_STEP = 10
PER_EXAMPLE_BATCH_AXIS = {'x': 0, 'positions': 0, 'loss_target': 0}
SHARED_INPUTS = []
_WEIGHT_DTYPES = {'norm_mix': _jnp.float32, 'w_in': _jnp.float32, 'q_latent_norm': _jnp.float32, 'w_uq': _jnp.float32, 'kv_latent_norm': _jnp.float32, 'w_ukv': _jnp.float32, 'out_norm_mla': _jnp.float32, 'out_norm_sb': _jnp.float32, 'w_o': _jnp.float32, 'norm_ffn': _jnp.float32, 'w_gate': _jnp.float32, 'w_up': _jnp.float32, 'w_down': _jnp.float32, 'norm_final': _jnp.float32}
MOMENT_SCALE = {'norm_mix': 2.668472e-01, 'w_in': 1.920137e-01, 'q_latent_norm': 3.251807e-01, 'w_uq': 1.627584e-01, 'kv_latent_norm': 7.330428e-01, 'w_ukv': 1.966289e-01, 'out_norm_mla': 2.061234e-01, 'out_norm_sb': 2.050439e-01, 'w_o': 1.934607e-01, 'norm_ffn': 1.458198e-01, 'w_gate': 6.077784e-02, 'w_up': 5.891645e-02, 'w_down': 9.746403e-02, 'norm_final': 6.403854e+01}


def _to_microbatches(a, axis):
    t = _jnp.moveaxis(a, axis, 0)
    t = t.reshape((N_MICROBATCH, t.shape[0] // N_MICROBATCH) + t.shape[1:])
    return _jnp.moveaxis(t, 1, axis + 1)


def setup_inputs(seed: int = 0) -> dict:
    inp = _fwd_setup_inputs(seed)
    key = _jax.random.fold_in(_jax.random.key(seed), 7919)
    shape, _ = _output_shape()
    out = dict(inp)
    out["loss_target"] = _jax.random.normal(_jax.random.fold_in(key, 0), shape, _jnp.float32)
    for i, name in enumerate(TWIN_WEIGHTS):
        w = inp[name].astype(_jnp.float32)
        if MOMENT_SCALE is None:
            s = _jnp.sqrt(_jnp.mean(_jnp.square(w)) + 1e-30)
        else:
            s = MOMENT_SCALE[name]
        km, kv = _jax.random.split(_jax.random.fold_in(key, i + 1))
        out[name] = w
        out["m_" + name] = s * _jax.random.normal(km, w.shape, _jnp.float32)
        out["v_" + name] = (s * s) * _jax.random.uniform(kv, w.shape, _jnp.float32, 0.5, 1.5)
    if N_MICROBATCH > 1:
        for name, axis in PER_EXAMPLE_BATCH_AXIS.items():
            out[name] = _to_microbatches(out[name], axis)
    return {'x': out['x'], 'positions': out['positions'], 'norm_mix': out['norm_mix'], 'w_in': out['w_in'], 'q_latent_norm': out['q_latent_norm'], 'w_uq': out['w_uq'], 'kv_latent_norm': out['kv_latent_norm'], 'w_ukv': out['w_ukv'], 'out_norm_mla': out['out_norm_mla'], 'out_norm_sb': out['out_norm_sb'], 'w_o': out['w_o'], 'norm_ffn': out['norm_ffn'], 'w_gate': out['w_gate'], 'w_up': out['w_up'], 'w_down': out['w_down'], 'norm_final': out['norm_final'], 'loss_target': out['loss_target'], 'm_norm_mix': out['m_norm_mix'], 'm_w_in': out['m_w_in'], 'm_q_latent_norm': out['m_q_latent_norm'], 'm_w_uq': out['m_w_uq'], 'm_kv_latent_norm': out['m_kv_latent_norm'], 'm_w_ukv': out['m_w_ukv'], 'm_out_norm_mla': out['m_out_norm_mla'], 'm_out_norm_sb': out['m_out_norm_sb'], 'm_w_o': out['m_w_o'], 'm_norm_ffn': out['m_norm_ffn'], 'm_w_gate': out['m_w_gate'], 'm_w_up': out['m_w_up'], 'm_w_down': out['m_w_down'], 'm_norm_final': out['m_norm_final'], 'v_norm_mix': out['v_norm_mix'], 'v_w_in': out['v_w_in'], 'v_q_latent_norm': out['v_q_latent_norm'], 'v_w_uq': out['v_w_uq'], 'v_kv_latent_norm': out['v_kv_latent_norm'], 'v_w_ukv': out['v_w_ukv'], 'v_out_norm_mla': out['v_out_norm_mla'], 'v_out_norm_sb': out['v_out_norm_sb'], 'v_w_o': out['v_w_o'], 'v_norm_ffn': out['v_norm_ffn'], 'v_w_gate': out['v_w_gate'], 'v_w_up': out['v_w_up'], 'v_w_down': out['v_w_down'], 'v_norm_final': out['v_norm_final']}


def _loss(weights, diff, rest, loss_target):
    with _jax.named_scope("forward"):
        args = {**rest, TWIN_DIFF_INPUT: diff, **{k: w.astype(_WEIGHT_DTYPES[k]) for k, w in weights.items()}}
        y = _forward(args)
    with _jax.named_scope("loss_head"):
        err = _jnp.square(y.astype(_jnp.float32) - loss_target)
        return 0.5 * _jnp.sum(_jnp.mean(err, axis=-1)) if err.ndim else 0.5 * err


def _adamw(w, g, m, v):
    m = ADAM_B1 * m + (1.0 - ADAM_B1) * g
    v = ADAM_B2 * v + (1.0 - ADAM_B2) * _jnp.square(g)
    m_hat = m / (1.0 - ADAM_B1 ** ADAM_STEP)
    v_hat = v / (1.0 - ADAM_B2 ** ADAM_STEP)
    delta = -ADAM_LR * (m_hat / (_jnp.sqrt(v_hat) + ADAM_EPS) + ADAM_WD * w)
    return delta, m, v


def reference(x, positions, norm_mix, w_in, q_latent_norm, w_uq, kv_latent_norm, w_ukv, out_norm_mla, out_norm_sb, w_o, norm_ffn, w_gate, w_up, w_down, norm_final, loss_target, m_norm_mix, m_w_in, m_q_latent_norm, m_w_uq, m_kv_latent_norm, m_w_ukv, m_out_norm_mla, m_out_norm_sb, m_w_o, m_norm_ffn, m_w_gate, m_w_up, m_w_down, m_norm_final, v_norm_mix, v_w_in, v_q_latent_norm, v_w_uq, v_kv_latent_norm, v_w_ukv, v_out_norm_mla, v_out_norm_sb, v_w_o, v_norm_ffn, v_w_gate, v_w_up, v_w_down, v_norm_final):
    given = dict(x=x, positions=positions, norm_mix=norm_mix, w_in=w_in, q_latent_norm=q_latent_norm, w_uq=w_uq, kv_latent_norm=kv_latent_norm, w_ukv=w_ukv, out_norm_mla=out_norm_mla, out_norm_sb=out_norm_sb, w_o=w_o, norm_ffn=norm_ffn, w_gate=w_gate, w_up=w_up, w_down=w_down, norm_final=norm_final, loss_target=loss_target, m_norm_mix=m_norm_mix, m_w_in=m_w_in, m_q_latent_norm=m_q_latent_norm, m_w_uq=m_w_uq, m_kv_latent_norm=m_kv_latent_norm, m_w_ukv=m_w_ukv, m_out_norm_mla=m_out_norm_mla, m_out_norm_sb=m_out_norm_sb, m_w_o=m_w_o, m_norm_ffn=m_norm_ffn, m_w_gate=m_w_gate, m_w_up=m_w_up, m_w_down=m_w_down, m_norm_final=m_norm_final, v_norm_mix=v_norm_mix, v_w_in=v_w_in, v_q_latent_norm=v_q_latent_norm, v_w_uq=v_w_uq, v_kv_latent_norm=v_kv_latent_norm, v_w_ukv=v_w_ukv, v_out_norm_mla=v_out_norm_mla, v_out_norm_sb=v_out_norm_sb, v_w_o=v_w_o, v_norm_ffn=v_norm_ffn, v_w_gate=v_w_gate, v_w_up=v_w_up, v_w_down=v_w_down, v_norm_final=v_norm_final)
    weights = {n: given[n] for n in TWIN_WEIGHTS}
    shared = {n: given[n] for n in SHARED_INPUTS}
    per_example = {n: given[n] for n in ['x', 'positions']}
    grad_fn = _jax.value_and_grad(_loss, argnums=(0, 1))

    def one_microbatch(ex, loss_target):
        ex = dict(ex)
        diff = ex.pop(TWIN_DIFF_INPUT)
        return grad_fn(weights, diff, {**shared, **ex}, loss_target)

    if N_MICROBATCH == 1:
        loss, (grad_w, grad_x) = one_microbatch(per_example, given["loss_target"])
    else:
        def body(carry, xs):
            loss_sum, grad_sum = carry
            l_k, (gw_k, gx_k) = one_microbatch(xs[0], xs[1])
            with _jax.named_scope("update"):
                return (loss_sum + l_k, _jax.tree.map(_jnp.add, grad_sum, gw_k)), gx_k

        init = (_jnp.zeros((), _jnp.float32), _jax.tree.map(_jnp.zeros_like, weights))
        (loss, grad_w), grad_x = _jax.lax.scan(body, init, (per_example, given["loss_target"]))
    with _jax.named_scope("update"):
        delta_w, new_m, new_v = {}, {}, {}
        for n in TWIN_WEIGHTS:
            delta_w[n], new_m[n], new_v[n] = _adamw(weights[n], grad_w[n], given["m_" + n], given["v_" + n])
    return (loss, grad_x, *[grad_w[n] for n in TWIN_WEIGHTS], *[delta_w[n] for n in TWIN_WEIGHTS],
            *[new_m[n] for n in TWIN_WEIGHTS], *[new_v[n] for n in TWIN_WEIGHTS])
```

```python
import functools
import math

import jax
import jax.numpy as jnp
from jax import lax
from jax.experimental import pallas as pl
from jax.experimental.pallas import tpu as pltpu

F32 = jnp.float32
BF16 = jnp.bfloat16
MESH = pl.DeviceIdType.MESH

EPS = 1e-6
ROPE_THETA = 10000.0
MLA_HEADS = 8
MLA_NOPE = 64
MLA_ROPE = 32
SB_HEADS = 8
HEAD_DIM = 64
Q_LORA = 256
KV_LORA = 128
MLA_SCALE = 1.0 / math.sqrt(MLA_NOPE + MLA_ROPE)
SB_SCALE = 1.0 / math.sqrt(HEAD_DIM)
N_DEV = 8

ADAM_LR = 0.001
ADAM_B1 = 0.9
ADAM_B2 = 0.999
ADAM_EPS = 1e-08
ADAM_WD = 0.01
ADAM_STEP = 10

LANES = 128
ATT_TILE = 256
ROW_TILE = 512
FFN_BWD_ROW_TILE = 256
PROJ_BWD_ROW_TILE = 256
VMEM_LIMIT = 56 * 1024 * 1024
NEG = -1e30


def _cparams(*sem):
    return pltpu.CompilerParams(dimension_semantics=sem, vmem_limit_bytes=VMEM_LIMIT)


def _dot(a, b):
    return jnp.dot(a, b, preferred_element_type=F32)


def _dot_nt(a, b):
    return lax.dot_general(a, b, (((1,), (1,)), ((), ())), preferred_element_type=F32)


def _dot_tn(a, b):
    return lax.dot_general(a, b, (((0,), (0,)), ((), ())), preferred_element_type=F32)


def _rms(x, g):
    r = lax.rsqrt(jnp.mean(x * x, axis=-1, keepdims=True) + EPS)
    return x * r * g


def _rms_bwd(x, g, dy):
    r = lax.rsqrt(jnp.mean(x * x, axis=-1, keepdims=True) + EPS)
    n = x * r
    dn = dy * g
    dx = r * (dn - n * jnp.mean(dn * n, axis=-1, keepdims=True))
    return dx, jnp.sum(dy * n, axis=0, keepdims=True)


def _rope(x, cos, sin_a, sin_b):
    return x * cos + pltpu.roll(x, 112, 1) * sin_a + pltpu.roll(x, 16, 1) * sin_b


def _rope_t(g, cos, sin_a, sin_b):
    return g * cos + pltpu.roll(g * sin_a, 16, 1) + pltpu.roll(g * sin_b, 112, 1)


def _row_spec(tm, width):
    return pl.BlockSpec((tm, width), lambda r: (r, 0))


def _full_spec(shape):
    return pl.BlockSpec(shape, lambda *_: (0,) * len(shape))


def _accumulate(ref, val, first):
    @pl.when(first)
    def _():
        ref[...] = val

    @pl.when(jnp.logical_not(first))
    def _():
        ref[...] += val


def _proj_in_fwd(x, g_mix, w_a, g_q, w_uq, g_kv, w_ukv, cos, sin_a, sin_b):
    s, d = x.shape
    tm = min(ROW_TILE, s)

    def body(x_ref, gm_ref, wa_ref, gq_ref, wuq_ref, gkv_ref, wukv_ref, cos_ref, sa_ref, sb_ref,
             u_ref, cq_ref, ckv_ref, cqn_ref, ckvn_ref, qn_ref, qr_ref, kv_ref, kr_ref, sbq_ref):
        u = _rms(x_ref[...], gm_ref[...]).astype(BF16)
        u_ref[...] = u
        cq = _dot(u, wa_ref[:, 0:256])
        ckv = _dot(u, wa_ref[:, 256:384])
        kr = _dot(u, wa_ref[:, 384:512])
        cq_ref[...] = cq
        ckv_ref[...] = ckv
        cqn = _rms(cq, gq_ref[...]).astype(BF16)
        ckvn = _rms(ckv, gkv_ref[...]).astype(BF16)
        cqn_ref[...] = cqn
        ckvn_ref[...] = ckvn
        cos_t, sa_t, sb_t = cos_ref[...], sa_ref[...], sb_ref[...]
        qn_ref[...] = (_dot(cqn, wuq_ref[:, 0:512]) * MLA_SCALE).astype(BF16)
        for half in range(2):
            lo = 512 + half * LANES
            qr = _dot(cqn, wuq_ref[:, lo:lo + LANES])
            qr_ref[:, half * LANES:(half + 1) * LANES] = (_rope(qr, cos_t, sa_t, sb_t) * MLA_SCALE).astype(BF16)
        kv_ref[...] = _dot(ckvn, wukv_ref[...]).astype(BF16)
        krt = kr + pltpu.roll(kr, 32, 1) + pltpu.roll(kr, 64, 1) + pltpu.roll(kr, 96, 1)
        kr_ref[...] = _rope(krt, cos_t, sa_t, sb_t).astype(BF16)
        sbq_ref[:, 0:512] = (_dot(u, wa_ref[:, 512:1024]) * SB_SCALE).astype(BF16)
        sbq_ref[:, 512:1536] = _dot(u, wa_ref[:, 1024:2048]).astype(BF16)

    outs = [
        jax.ShapeDtypeStruct((s, d), BF16),
        jax.ShapeDtypeStruct((s, 256), F32),
        jax.ShapeDtypeStruct((s, 128), F32),
        jax.ShapeDtypeStruct((s, 256), BF16),
        jax.ShapeDtypeStruct((s, 128), BF16),
        jax.ShapeDtypeStruct((s, 512), BF16),
        jax.ShapeDtypeStruct((s, 256), BF16),
        jax.ShapeDtypeStruct((s, 1024), BF16),
        jax.ShapeDtypeStruct((s, 128), BF16),
        jax.ShapeDtypeStruct((s, 1536), BF16),
    ]
    return pl.pallas_call(
        body, name="proj_in_fwd", grid=(s // tm,), out_shape=outs,
        in_specs=[_row_spec(tm, d), _full_spec(g_mix.shape), _full_spec(w_a.shape), _full_spec(g_q.shape),
                  _full_spec(w_uq.shape), _full_spec(g_kv.shape), _full_spec(w_ukv.shape),
                  _row_spec(tm, LANES), _row_spec(tm, LANES), _row_spec(tm, LANES)],
        out_specs=[_row_spec(tm, o.shape[1]) for o in outs],
        compiler_params=_cparams("arbitrary"),
    )(x, g_mix, w_a, g_q, w_uq, g_kv, w_ukv, cos, sin_a, sin_b)


def _attn_out_fwd(o_mla, o_sb, g_mla, g_sb, w_o, x, g_ffn):
    s, d = x.shape
    tm = min(ROW_TILE, s)

    def body(oa_ref, ob_ref, ga_ref, gb_ref, wo_ref, x_ref, gf_ref, merged_ref, h1_ref, f_ref):
        na = _rms(oa_ref[...], ga_ref[...]).astype(BF16)
        nb = _rms(ob_ref[...], gb_ref[...]).astype(BF16)
        merged_ref[:, 0:512] = na
        merged_ref[:, 512:1024] = nb
        h1 = x_ref[...] + _dot(na, wo_ref[0:512, :]) + _dot(nb, wo_ref[512:1024, :])
        h1_ref[...] = h1
        f_ref[...] = _rms(h1, gf_ref[...]).astype(BF16)

    outs = [jax.ShapeDtypeStruct((s, d), BF16), jax.ShapeDtypeStruct((s, d), F32), jax.ShapeDtypeStruct((s, d), BF16)]
    return pl.pallas_call(
        body, name="attn_out_fwd", grid=(s // tm,), out_shape=outs,
        in_specs=[_row_spec(tm, 512), _row_spec(tm, 512), _full_spec(g_mla.shape), _full_spec(g_sb.shape),
                  _full_spec(w_o.shape), _row_spec(tm, d), _full_spec(g_ffn.shape)],
        out_specs=[_row_spec(tm, d)] * 3,
        compiler_params=_cparams("arbitrary"),
    )(o_mla, o_sb, g_mla, g_sb, w_o, x, g_ffn)


def _ffn_tile(d_ff):
    return d_ff // 2 if (d_ff // 2) % LANES == 0 else d_ff


def _ffn_fwd(f, h1, w_gate, w_up, w_down):
    s, d = h1.shape
    d_ff = w_gate.shape[1]
    tm = min(ROW_TILE, s)
    tf = _ffn_tile(d_ff)

    def body(f_ref, h1_ref, wg_ref, wu_ref, wd_ref, gate_ref, up_ref, h2_ref):
        j = pl.program_id(1)
        fb = f_ref[...]
        gate = _dot(fb, wg_ref[...])
        up = _dot(fb, wu_ref[...])
        gate_ref[...] = gate.astype(BF16)
        up_ref[...] = up.astype(BF16)
        act = (gate * jax.nn.sigmoid(gate) * up).astype(BF16)
        part = _dot(act, wd_ref[...])

        @pl.when(j == 0)
        def _():
            h2_ref[...] = h1_ref[...] + part

        @pl.when(j != 0)
        def _():
            h2_ref[...] += part

    outs = [jax.ShapeDtypeStruct((s, d_ff), BF16), jax.ShapeDtypeStruct((s, d_ff), BF16), jax.ShapeDtypeStruct((s, d), F32)]
    return pl.pallas_call(
        body, name="ffn_fwd", grid=(s // tm, d_ff // tf), out_shape=outs,
        in_specs=[pl.BlockSpec((tm, d), lambda r, j: (r, 0)), pl.BlockSpec((tm, d), lambda r, j: (r, 0)),
                  pl.BlockSpec((d, tf), lambda r, j: (0, j)), pl.BlockSpec((d, tf), lambda r, j: (0, j)),
                  pl.BlockSpec((tf, d), lambda r, j: (j, 0))],
        out_specs=[pl.BlockSpec((tm, tf), lambda r, j: (r, j)), pl.BlockSpec((tm, tf), lambda r, j: (r, j)),
                   pl.BlockSpec((tm, d), lambda r, j: (r, 0))],
        compiler_params=_cparams("arbitrary", "arbitrary"),
    )(f, h1, w_gate, w_up, w_down)


def _final_loss(h2, target, g_final):
    s, d = h2.shape
    tm = min(ROW_TILE, s)

    def body(h2_ref, t_ref, g_ref, loss_ref, dh2_ref, dg_ref):
        first = pl.program_id(0) == 0
        h2v = h2_ref[...]
        g = g_ref[...]
        diff = _rms(h2v, g) - t_ref[...]
        part = 0.5 * jnp.sum(jnp.mean(diff * diff, axis=-1, keepdims=True), axis=0, keepdims=True)
        _accumulate(loss_ref, jnp.broadcast_to(part, loss_ref.shape), first)
        dx, dg = _rms_bwd(h2v, g, diff * (1.0 / d))
        dh2_ref[...] = dx
        _accumulate(dg_ref, dg, first)

    outs = [jax.ShapeDtypeStruct((1, LANES), F32), jax.ShapeDtypeStruct((s, d), F32), jax.ShapeDtypeStruct((1, d), F32)]
    return pl.pallas_call(
        body, name="final_loss", grid=(s // tm,), out_shape=outs,
        in_specs=[_row_spec(tm, d), _row_spec(tm, d), _full_spec((1, d))],
        out_specs=[_full_spec((1, LANES)), _row_spec(tm, d), _full_spec((1, d))],
        compiler_params=_cparams("arbitrary"),
    )(h2, target, g_final)


def _tile_iotas(t):
    return lax.broadcasted_iota(jnp.int32, (t, t), 0), lax.broadcasted_iota(jnp.int32, (t, t), 1)


def _split_bf16(v):
    hi = v.astype(BF16)
    return hi, (v - hi.astype(F32)).astype(BF16)


def _mla_fwd(qn, qr, kv, kr):
    s = qn.shape[0]
    t = min(ATT_TILE, s)
    pairs = MLA_HEADS // 2

    def body(qn_ref, qr_ref, kn_ref, v_ref, kr_ref, o_ref, lse_ref):
        hp, i = pl.program_id(0), pl.program_id(1)
        lane = lax.broadcasted_iota(jnp.int32, (1, LANES), 1)
        row, col = _tile_iotas(t)
        causal = col <= row
        q_pair, q_quad = qn_ref[...], qr_ref[...]
        zero = jnp.zeros_like(q_pair)
        o_heads, lse_heads = [], []
        for hh in range(2):
            in_head = (lane // HEAD_DIM) == hh
            in_rope = (lane // MLA_ROPE) == (hp % 2) * 2 + hh
            qcat = jnp.concatenate([jnp.where(in_head, q_pair, zero), jnp.where(in_rope, q_quad, zero)], axis=1)

            def tile(j, carry, masked):
                m, l, acc = carry
                rows = pl.ds(pl.multiple_of(j * t, t), t)
                kcat = jnp.concatenate([kn_ref[rows, :], kr_ref[rows, :]], axis=1)
                sc = _dot_nt(qcat, kcat)
                if masked:
                    sc = jnp.where(causal, sc, NEG)
                m_new = jnp.maximum(m, jnp.max(sc, axis=-1, keepdims=True))
                alpha = jnp.exp(m - m_new)
                p = jnp.exp(sc - m_new)
                l = alpha * l + jnp.sum(p, axis=-1, keepdims=True)
                acc = alpha * acc + _dot(p.astype(BF16), v_ref[rows, :])
                return m_new, l, acc

            init = (jnp.full((t, 1), NEG, F32), jnp.zeros((t, 1), F32), jnp.zeros((t, LANES), F32))
            carry = tile(i, init, True)
            m, l, acc = lax.fori_loop(0, i, lambda j, c: tile(j, c, False), carry)
            o_heads.append(acc / l)
            lse_heads.append(m + jnp.log(l))
        first = (lane // HEAD_DIM) == 0
        o_ref[...] = jnp.where(first, o_heads[0], o_heads[1])
        lse_ref[...] = jnp.where(first, lse_heads[0], lse_heads[1])

    outs = [jax.ShapeDtypeStruct((s, 512), F32), jax.ShapeDtypeStruct((pairs, s, LANES), F32)]
    return pl.pallas_call(
        body, name="mla_fwd", grid=(pairs, s // t), out_shape=outs,
        in_specs=[pl.BlockSpec((t, LANES), lambda hp, i: (i, hp)), pl.BlockSpec((t, LANES), lambda hp, i: (i, hp // 2)),
                  pl.BlockSpec((s, LANES), lambda hp, i: (0, hp)), pl.BlockSpec((s, LANES), lambda hp, i: (0, 4 + hp)),
                  pl.BlockSpec((s, LANES), lambda hp, i: (0, 0))],
        out_specs=[pl.BlockSpec((t, LANES), lambda hp, i: (i, hp)), pl.BlockSpec((None, t, LANES), lambda hp, i: (hp, i, 0))],
        compiler_params=_cparams("arbitrary", "arbitrary"),
    )(qn, qr, kv, kv, kr)


def _sb_scores(qm, k, strict, masked):
    z = _dot_nt(qm, k)
    sp = jnp.log(1.0 + jnp.exp(-jnp.abs(z)))
    log_1m = -(jnp.maximum(z, 0.0) + sp)
    log_b = jnp.minimum(z, 0.0) - sp
    if masked:
        log_1m = jnp.where(strict, log_1m, 0.0)
    return log_1m, log_b


def _sb_fwd(qkv):
    s = qkv.shape[0]
    t = min(ATT_TILE, s)
    pairs = SB_HEADS // 2

    def body(q_ref, k_ref, v_ref, o_ref, tot_ref):
        i = pl.program_id(1)
        lane = lax.broadcasted_iota(jnp.int32, (1, LANES), 1)
        row, col = _tile_iotas(t)
        strict = col < row
        t_suffix = (row > col).astype(BF16)
        q_pair = q_ref[...]
        o_heads, tot_heads = [], []
        for hh in range(2):
            qm = jnp.where((lane // HEAD_DIM) == hh, q_pair, jnp.zeros_like(q_pair))

            def tile(j, carry, masked):
                right, acc = carry
                rows = pl.ds(pl.multiple_of(j * t, t), t)
                log_1m, log_b = _sb_scores(qm, k_ref[rows, :], strict, masked)
                hi, lo = _split_bf16(log_1m)
                inside = _dot(hi, t_suffix) + _dot(lo, t_suffix)
                a = jnp.exp(log_b + inside + right)
                if masked:
                    a = jnp.where(strict, a, 0.0)
                acc = acc + _dot(a.astype(BF16), v_ref[rows, :])
                right = right + jnp.sum(log_1m, axis=-1, keepdims=True)
                return right, acc

            carry = tile(i, (jnp.zeros((t, 1), F32), jnp.zeros((t, LANES), F32)), True)
            right, acc = lax.fori_loop(0, i, lambda n, c: tile(i - 1 - n, c, False), carry)
            o_heads.append(acc)
            tot_heads.append(right)
        first = (lane // HEAD_DIM) == 0
        o_ref[...] = jnp.where(first, o_heads[0], o_heads[1])
        tot_ref[...] = jnp.where(first, tot_heads[0], tot_heads[1])

    outs = [jax.ShapeDtypeStruct((s, 512), F32), jax.ShapeDtypeStruct((pairs, s, LANES), F32)]
    return pl.pallas_call(
        body, name="sb_fwd", grid=(pairs, s // t), out_shape=outs,
        in_specs=[pl.BlockSpec((t, LANES), lambda hp, i: (i, hp)), pl.BlockSpec((s, LANES), lambda hp, i: (0, 4 + hp)),
                  pl.BlockSpec((s, LANES), lambda hp, i: (0, 8 + hp))],
        out_specs=[pl.BlockSpec((t, LANES), lambda hp, i: (i, hp)), pl.BlockSpec((None, t, LANES), lambda hp, i: (hp, i, 0))],
        compiler_params=_cparams("arbitrary", "arbitrary"),
    )(qkv, qkv, qkv)


def _sb_bwd(qkv, do, tot):
    s = qkv.shape[0]
    t = min(ATT_TILE, s)
    pairs = SB_HEADS // 2

    def body(q_ref, k_ref, v_ref, do_ref, tot_ref, dq_ref, dk_ref, dv_ref):
        i = pl.program_id(1)

        @pl.when(i == 0)
        def _():
            dk_ref[...] = jnp.zeros_like(dk_ref)
            dv_ref[...] = jnp.zeros_like(dv_ref)

        lane = lax.broadcasted_iota(jnp.int32, (1, LANES), 1)
        row, col = _tile_iotas(t)
        strict = col < row
        t_incl = (row <= col).astype(BF16)
        t_excl = (row < col).astype(BF16)
        q_pair, do_pair, tot_pair = q_ref[...], do_ref[...], tot_ref[...]
        dq_heads = []
        for hh in range(2):
            in_head = (lane // HEAD_DIM) == hh
            qm = jnp.where(in_head, q_pair, jnp.zeros_like(q_pair))
            dob = jnp.where(in_head, do_pair, 0.0).astype(BF16)
            total = jnp.sum(jnp.where(lane == hh * HEAD_DIM, tot_pair, 0.0), axis=-1, keepdims=True)

            def tile(j, carry, masked):
                left_l, left_g, dq = carry
                rows = pl.ds(pl.multiple_of(j * t, t), t)
                k, v = k_ref[rows, :], v_ref[rows, :]
                log_1m, log_b = _sb_scores(qm, k, strict, masked)
                hi, lo = _split_bf16(log_1m)
                prefix = _dot(hi, t_incl) + _dot(lo, t_incl)
                a = jnp.exp(log_b + (total - left_l - prefix))
                if masked:
                    a = jnp.where(strict, a, 0.0)
                g = a * _dot_nt(dob, v)
                p = left_g + _dot(g.astype(BF16), t_excl)
                dz = g - jnp.exp(log_b) * (g + p)
                if masked:
                    dz = jnp.where(strict, dz, 0.0)
                dzb = dz.astype(BF16)
                dq = dq + _dot(dzb, k)
                dk_ref[rows, :] += _dot_tn(dzb, qm)
                dv_ref[rows, :] += _dot_tn(a.astype(BF16), dob)
                left_l = left_l + jnp.sum(log_1m, axis=-1, keepdims=True)
                left_g = left_g + jnp.sum(g, axis=-1, keepdims=True)
                return left_l, left_g, dq

            init = (jnp.zeros((t, 1), F32), jnp.zeros((t, 1), F32), jnp.zeros((t, LANES), F32))
            carry = lax.fori_loop(0, i, lambda j, c: tile(j, c, False), init)
            dq_heads.append(tile(i, carry, True)[2])
        dq_ref[...] = jnp.where((lane // HEAD_DIM) == 0, dq_heads[0], dq_heads[1])

    outs = [jax.ShapeDtypeStruct((s, 512), F32)] * 3
    return pl.pallas_call(
        body, name="sb_bwd", grid=(pairs, s // t), out_shape=outs,
        in_specs=[pl.BlockSpec((t, LANES), lambda hp, i: (i, hp)), pl.BlockSpec((s, LANES), lambda hp, i: (0, 4 + hp)),
                  pl.BlockSpec((s, LANES), lambda hp, i: (0, 8 + hp)), pl.BlockSpec((t, LANES), lambda hp, i: (i, hp)),
                  pl.BlockSpec((None, t, LANES), lambda hp, i: (hp, i, 0))],
        out_specs=[pl.BlockSpec((t, LANES), lambda hp, i: (i, hp)), pl.BlockSpec((s, LANES), lambda hp, i: (0, hp)),
                   pl.BlockSpec((s, LANES), lambda hp, i: (0, hp))],
        compiler_params=_cparams("arbitrary", "arbitrary"),
    )(qkv, qkv, qkv, do, tot)


def _mla_bwd(qn, qr, kv, kr, do, o, lse):
    s = qn.shape[0]
    t = min(ATT_TILE, s)
    pairs = MLA_HEADS // 2

    def body(qn_ref, qr_ref, kn_ref, v_ref, kr_ref, do_ref, o_ref, lse_ref, dqn_ref, dqr_ref, dkn_ref, dv_ref, dkr_ref):
        hp, i = pl.program_id(0), pl.program_id(1)

        @pl.when(i == 0)
        def _():
            dkn_ref[...] = jnp.zeros_like(dkn_ref)
            dv_ref[...] = jnp.zeros_like(dv_ref)
            dkr_ref[...] = jnp.zeros_like(dkr_ref)

        lane = lax.broadcasted_iota(jnp.int32, (1, LANES), 1)
        row, col = _tile_iotas(t)
        causal = col <= row
        q_pair, q_quad, do_pair, lse_pair = qn_ref[...], qr_ref[...], do_ref[...], lse_ref[...]
        do_o = do_pair * o_ref[...]
        zero = jnp.zeros_like(q_pair)
        dq_heads, ropes = [], []
        for hh in range(2):
            in_head = (lane // HEAD_DIM) == hh
            in_rope = (lane // MLA_ROPE) == (hp % 2) * 2 + hh
            ropes.append(in_rope)
            qcat = jnp.concatenate([jnp.where(in_head, q_pair, zero), jnp.where(in_rope, q_quad, zero)], axis=1)
            dob = jnp.where(in_head, do_pair, 0.0).astype(BF16)
            delta = jnp.sum(jnp.where(in_head, do_o, 0.0), axis=-1, keepdims=True)
            lse_h = jnp.sum(jnp.where(lane == hh * HEAD_DIM, lse_pair, 0.0), axis=-1, keepdims=True)

            def tile(j, dq, masked):
                rows = pl.ds(pl.multiple_of(j * t, t), t)
                kcat = jnp.concatenate([kn_ref[rows, :], kr_ref[rows, :]], axis=1)
                v = v_ref[rows, :]
                p = jnp.exp(_dot_nt(qcat, kcat) - lse_h)
                if masked:
                    p = jnp.where(causal, p, 0.0)
                ds = (p * (_dot_nt(dob, v) - delta)).astype(BF16)
                dq = dq + _dot(ds, kcat)
                dkcat = _dot_tn(ds, qcat)
                dkn_ref[rows, :] += dkcat[:, 0:LANES]
                dkr_ref[rows, :] += dkcat[:, LANES:2 * LANES]
                dv_ref[rows, :] += _dot_tn(p.astype(BF16), dob)
                return dq

            dq = lax.fori_loop(0, i, lambda j, c: tile(j, c, False), jnp.zeros((t, 2 * LANES), F32))
            dq_heads.append(tile(i, dq, True))
        dqn_ref[...] = jnp.where((lane // HEAD_DIM) == 0, dq_heads[0][:, 0:LANES], dq_heads[1][:, 0:LANES])
        dqr_ref[...] = (jnp.where(ropes[0], dq_heads[0][:, LANES:], 0.0) + jnp.where(ropes[1], dq_heads[1][:, LANES:], 0.0))

    pair_block = pl.BlockSpec((t, LANES), lambda hp, i: (i, hp))
    outs = [jax.ShapeDtypeStruct((s, 512), F32), jax.ShapeDtypeStruct((pairs, s, LANES), F32),
            jax.ShapeDtypeStruct((s, 512), F32), jax.ShapeDtypeStruct((s, 512), F32), jax.ShapeDtypeStruct((pairs, s, LANES), F32)]
    return pl.pallas_call(
        body, name="mla_bwd", grid=(pairs, s // t), out_shape=outs,
        in_specs=[pair_block, pl.BlockSpec((t, LANES), lambda hp, i: (i, hp // 2)),
                  pl.BlockSpec((s, LANES), lambda hp, i: (0, hp)), pl.BlockSpec((s, LANES), lambda hp, i: (0, 4 + hp)),
                  pl.BlockSpec((s, LANES), lambda hp, i: (0, 0)), pair_block, pair_block,
                  pl.BlockSpec((None, t, LANES), lambda hp, i: (hp, i, 0))],
        out_specs=[pair_block, pl.BlockSpec((None, t, LANES), lambda hp, i: (hp, i, 0)),
                   pl.BlockSpec((s, LANES), lambda hp, i: (0, hp)), pl.BlockSpec((s, LANES), lambda hp, i: (0, hp)),
                   pl.BlockSpec((None, s, LANES), lambda hp, i: (hp, 0, 0))],
        compiler_params=_cparams("arbitrary", "arbitrary"),
    )(qn, qr, kv, kv, kr, do, o, lse)


def _ffn_bwd(dh2, gate, up, h1, g_ffn, w_down_t, w_gate_t, w_up_t):
    s, d = h1.shape
    d_ff = gate.shape[1]
    tm = min(FFN_BWD_ROW_TILE, s)
    tf = _ffn_tile(d_ff)
    nf = d_ff // tf

    def body(dh2_ref, gate_ref, up_ref, h1_ref, g_ref, wdt_ref, wgt_ref, wut_ref,
             dgate_ref, dup_ref, act_ref, dh1_ref, dg_ref, df_ref):
        r, j = pl.program_id(0), pl.program_id(1)
        dact = _dot(dh2_ref[...].astype(BF16), wdt_ref[...])
        gate_v = gate_ref[...].astype(F32)
        up_v = up_ref[...].astype(F32)
        sig = jax.nn.sigmoid(gate_v)
        silu = gate_v * sig
        dup = (dact * silu).astype(BF16)
        dgate = (dact * up_v * (sig * (1.0 + gate_v * (1.0 - sig)))).astype(BF16)
        dgate_ref[...] = dgate
        dup_ref[...] = dup
        act_ref[...] = (silu * up_v).astype(BF16)
        part = _dot(dgate, wgt_ref[...]) + _dot(dup, wut_ref[...])
        _accumulate(df_ref, part, j == 0)

        @pl.when(j == nf - 1)
        def _():
            dx, dg = _rms_bwd(h1_ref[...], g_ref[...], df_ref[...])
            dh1_ref[...] = dh2_ref[...] + dx
            _accumulate(dg_ref, dg, r == 0)

    outs = [jax.ShapeDtypeStruct((s, d_ff), BF16)] * 3 + [jax.ShapeDtypeStruct((s, d), F32), jax.ShapeDtypeStruct((1, d), F32)]
    rows = pl.BlockSpec((tm, d), lambda r, j: (r, 0))
    ff = pl.BlockSpec((tm, tf), lambda r, j: (r, j))
    return pl.pallas_call(
        body, name="ffn_bwd", grid=(s // tm, nf), out_shape=outs,
        in_specs=[rows, ff, ff, rows, pl.BlockSpec((1, d), lambda r, j: (0, 0)),
                  pl.BlockSpec((d, tf), lambda r, j: (0, j)), pl.BlockSpec((tf, d), lambda r, j: (j, 0)),
                  pl.BlockSpec((tf, d), lambda r, j: (j, 0))],
        out_specs=[ff, ff, ff, rows, pl.BlockSpec((1, d), lambda r, j: (0, 0))],
        scratch_shapes=[pltpu.VMEM((tm, d), F32)],
        compiler_params=_cparams("arbitrary", "arbitrary"),
    )(dh2, gate, up, h1, g_ffn, w_down_t, w_gate_t, w_up_t)


def _largest_tile(n, cap):
    for cand in range(cap, 0, -LANES):
        if n % cand == 0:
            return cand
    return n


def _tn_matmul(a, b, name):
    s, m = a.shape
    n = b.shape[1]
    ts = min(ROW_TILE, s)
    tm = _largest_tile(m, 512)
    tn = _largest_tile(n, 512)

    def body(a_ref, b_ref, o_ref):
        part = _dot_tn(a_ref[...].astype(BF16), b_ref[...].astype(BF16))
        _accumulate(o_ref, part, pl.program_id(2) == 0)

    return pl.pallas_call(
        body, name=name, grid=(m // tm, n // tn, s // ts), out_shape=jax.ShapeDtypeStruct((m, n), F32),
        in_specs=[pl.BlockSpec((ts, tm), lambda i, j, k: (k, i)), pl.BlockSpec((ts, tn), lambda i, j, k: (k, j))],
        out_specs=pl.BlockSpec((tm, tn), lambda i, j, k: (i, j)),
        compiler_params=_cparams("arbitrary", "arbitrary", "arbitrary"),
    )(a, b)


def _attn_out_bwd(dh1, w_o_t, o_mla, o_sb, g_mla, g_sb):
    s, d = dh1.shape
    tm = min(ROW_TILE, s)

    def body(dh1_ref, wot_ref, oa_ref, ob_ref, ga_ref, gb_ref, doa_ref, dob_ref, dga_ref, dgb_ref):
        first = pl.program_id(0) == 0
        dh1b = dh1_ref[...].astype(BF16)
        dxa, dga = _rms_bwd(oa_ref[...], ga_ref[...], _dot(dh1b, wot_ref[:, 0:512]))
        dxb, dgb = _rms_bwd(ob_ref[...], gb_ref[...], _dot(dh1b, wot_ref[:, 512:1024]))
        doa_ref[...] = dxa
        dob_ref[...] = dxb
        _accumulate(dga_ref, dga, first)
        _accumulate(dgb_ref, dgb, first)

    outs = [jax.ShapeDtypeStruct((s, 512), F32)] * 2 + [jax.ShapeDtypeStruct((1, 512), F32)] * 2
    return pl.pallas_call(
        body, name="attn_out_bwd", grid=(s // tm,), out_shape=outs,
        in_specs=[_row_spec(tm, d), _full_spec(w_o_t.shape), _row_spec(tm, 512), _row_spec(tm, 512),
                  _full_spec((1, 512)), _full_spec((1, 512))],
        out_specs=[_row_spec(tm, 512), _row_spec(tm, 512), _full_spec((1, 512)), _full_spec((1, 512))],
        compiler_params=_cparams("arbitrary"),
    )(dh1, w_o_t, o_mla, o_sb, g_mla, g_sb)


def _proj_in_bwd(dqn, dqr, dkn, dv, dkr, dq_sb, dk_sb, dv_sb, cq, ckv, x, dh1, cos, sin_a, sin_b,
                 g_q, g_kv, g_mix, w_uq_t, w_ukv_t, w_a_t):
    s, d = x.shape
    tm = min(PROJ_BWD_ROW_TILE, s)

    def body(dqn_ref, dqr_ref, dkn_ref, dv_ref, dkr_ref, dqs_ref, dks_ref, dvs_ref, cq_ref, ckv_ref, x_ref, dh1_ref,
             cos_ref, sa_ref, sb_ref, gq_ref, gkv_ref, gm_ref, wuqt_ref, wukvt_ref, wat_ref,
             dx_ref, dproj_ref, dq_ref, dkv_ref, dgq_ref, dgkv_ref, dgm_ref):
        first = pl.program_id(0) == 0
        lane = lax.broadcasted_iota(jnp.int32, (1, LANES), 1)
        cos_t, sa_t, sb_t = cos_ref[...], sa_ref[...], sb_ref[...]
        dq_ref[:, 0:512] = (dqn_ref[...] * MLA_SCALE).astype(BF16)
        for half in range(2):
            quad = (dqr_ref[2 * half] + dqr_ref[2 * half + 1]) * MLA_SCALE
            dq_ref[:, 512 + half * LANES:512 + (half + 1) * LANES] = _rope_t(quad, cos_t, sa_t, sb_t).astype(BF16)
        dcq, dgq = _rms_bwd(cq_ref[...], gq_ref[...], _dot(dq_ref[...], wuqt_ref[...]))
        _accumulate(dgq_ref, dgq, first)
        dkv_ref[:, 0:512] = dkn_ref[...].astype(BF16)
        dkv_ref[:, 512:1024] = dv_ref[...].astype(BF16)
        dckv, dgkv = _rms_bwd(ckv_ref[...], gkv_ref[...], _dot(dkv_ref[...], wukvt_ref[...]))
        _accumulate(dgkv_ref, dgkv, first)
        g = _rope_t(dkr_ref[0] + dkr_ref[1] + dkr_ref[2] + dkr_ref[3], cos_t, sa_t, sb_t)
        g = g + pltpu.roll(g, 96, 1) + pltpu.roll(g, 64, 1) + pltpu.roll(g, 32, 1)
        dproj_ref[:, 0:256] = dcq.astype(BF16)
        dproj_ref[:, 256:384] = dckv.astype(BF16)
        dproj_ref[:, 384:512] = jnp.where(lane < MLA_ROPE, g, 0.0).astype(BF16)
        dproj_ref[:, 512:1024] = (dqs_ref[...] * SB_SCALE).astype(BF16)
        dproj_ref[:, 1024:1536] = dks_ref[...].astype(BF16)
        dproj_ref[:, 1536:2048] = dvs_ref[...].astype(BF16)
        dxn, dgm = _rms_bwd(x_ref[...], gm_ref[...], _dot(dproj_ref[...], wat_ref[...]))
        dx_ref[...] = dh1_ref[...] + dxn
        _accumulate(dgm_ref, dgm, first)

    quad_spec = pl.BlockSpec((4, tm, LANES), lambda r: (0, r, 0))
    outs = [jax.ShapeDtypeStruct((s, d), F32), jax.ShapeDtypeStruct((s, 2048), BF16), jax.ShapeDtypeStruct((s, 768), BF16),
            jax.ShapeDtypeStruct((s, 1024), BF16), jax.ShapeDtypeStruct((1, 256), F32), jax.ShapeDtypeStruct((1, 128), F32),
            jax.ShapeDtypeStruct((1, d), F32)]
    return pl.pallas_call(
        body, name="proj_in_bwd", grid=(s // tm,), out_shape=outs,
        in_specs=[_row_spec(tm, 512), quad_spec, _row_spec(tm, 512), _row_spec(tm, 512), quad_spec,
                  _row_spec(tm, 512), _row_spec(tm, 512), _row_spec(tm, 512), _row_spec(tm, 256), _row_spec(tm, 128),
                  _row_spec(tm, d), _row_spec(tm, d), _row_spec(tm, LANES), _row_spec(tm, LANES), _row_spec(tm, LANES),
                  _full_spec((1, 256)), _full_spec((1, 128)), _full_spec((1, d)),
                  _full_spec(w_uq_t.shape), _full_spec(w_ukv_t.shape), _full_spec(w_a_t.shape)],
        out_specs=[_row_spec(tm, d), _row_spec(tm, 2048), _row_spec(tm, 768), _row_spec(tm, 1024),
                   _full_spec((1, 256)), _full_spec((1, 128)), _full_spec((1, d))],
        compiler_params=_cparams("arbitrary"),
    )(dqn, dqr, dkn, dv, dkr, dq_sb, dk_sb, dv_sb, cq, ckv, x, dh1, cos, sin_a, sin_b, g_q, g_kv, g_mix,
      w_uq_t, w_ukv_t, w_a_t)


ANY = pl.BlockSpec(memory_space=pl.ANY)


def _place():
    return lax.axis_index("x"), lax.axis_index("y"), lax.axis_index("c")


def _all_gather(shards, name):
    n = len(shards)

    def body(*refs):
        ins, outs = refs[:n], refs[n:2 * n]
        send_sems, recv_sems, local_sems = refs[2 * n:]
        x, y, c = _place()
        me, sibling = (x, y, c), (x, y, 1 - c)
        chips = [(1 - x, y), (x, 1 - y), (1 - x, 1 - y)]

        def slot(a, px, py, pc):
            return outs[a].at[4 * px + 2 * py + pc]

        def copy(a, k, block, to, src=None):
            return pltpu.make_async_remote_copy(
                src_ref=slot(a, *block) if src is None else src, dst_ref=slot(a, *block),
                send_sem=send_sems.at[a, k], recv_sem=recv_sems.at[a, k], device_id=to, device_id_type=MESH)

        mine, first, passed = [], [], []
        for a in range(n):
            own = pltpu.make_async_copy(ins[a], slot(a, *me), local_sems.at[a])
            own.start()
            mine.append(own)
            cps = [copy(a, 0, me, sibling, src=ins[a])]
            cps += [copy(a, 1 + j, me, (*chip, c), src=ins[a]) for j, chip in enumerate(chips)]
            for cp in cps:
                cp.start()
            first += cps
        for a in range(n):
            for j, chip in enumerate(chips):
                copy(a, 1 + j, (*chip, c), me).wait_recv()
                fwd = copy(a, 4 + j, (*chip, c), sibling)
                fwd.start()
                passed.append(fwd)
        for a in range(n):
            copy(a, 0, sibling, me).wait_recv()
            for j, chip in enumerate(chips):
                copy(a, 4 + j, (*chip, 1 - c), me).wait_recv()
        for cp in first + passed:
            cp.wait_send()
        for own in mine:
            own.wait()

    return pl.pallas_call(
        body, name=name,
        out_shape=[jax.ShapeDtypeStruct((N_DEV,) + v.shape, v.dtype) for v in shards],
        in_specs=[ANY] * n, out_specs=[ANY] * n,
        scratch_shapes=[pltpu.SemaphoreType.DMA((n, 7)), pltpu.SemaphoreType.DMA((n, 7)), pltpu.SemaphoreType.DMA((n,))],
    )(*shards)


def _sibling_exchange(parts):
    n = len(parts)

    def body(*refs):
        ins, outs = refs[:n], refs[n:2 * n]
        send_sems, recv_sems = refs[2 * n:]
        x, y, c = _place()
        copies = []
        for a in range(n):
            for q in range(4):
                cp = pltpu.make_async_remote_copy(
                    src_ref=ins[a].at[2 * q + 1 - c], dst_ref=outs[a].at[q],
                    send_sem=send_sems.at[a, q], recv_sem=recv_sems.at[a, q], device_id=(x, y, 1 - c), device_id_type=MESH)
                cp.start()
                copies.append(cp)
        for cp in copies:
            cp.wait()

    return pl.pallas_call(
        body, name="grad_sibling_exchange",
        out_shape=[jax.ShapeDtypeStruct((4,) + v.shape[1:], v.dtype) for v in parts],
        in_specs=[ANY] * n, out_specs=[ANY] * n,
        scratch_shapes=[pltpu.SemaphoreType.DMA((n, 4)), pltpu.SemaphoreType.DMA((n, 4))],
    )(*parts)


def _grad_row_tile(rows):
    return _largest_tile_rows(rows, 256)


def _largest_tile_rows(rows, cap):
    for cand in range(cap, 0, -8):
        if rows % cand == 0:
            return cand
    return rows


def _chip_sum(part, recv, name):
    _, r, cdim = part.shape
    part4 = part.reshape(4, 2, r, cdim)
    tr = _grad_row_tile(r)

    def body(p_ref, s_ref, o_ref):
        c = lax.axis_index("c")
        own = jnp.where(c == 1, p_ref[1], p_ref[0])
        o_ref[...] = own + s_ref[...]

    return pl.pallas_call(
        body, name=name, grid=(4, r // tr), out_shape=jax.ShapeDtypeStruct((4, r, cdim), F32),
        in_specs=[pl.BlockSpec((None, 2, tr, cdim), lambda q, i: (q, 0, i, 0)),
                  pl.BlockSpec((None, tr, cdim), lambda q, i: (q, i, 0))],
        out_specs=pl.BlockSpec((None, tr, cdim), lambda q, i: (q, i, 0)),
        compiler_params=_cparams("arbitrary", "arbitrary"),
    )(part4, recv)


def _chip_exchange(sums):
    n = len(sums)

    def body(*refs):
        ins, outs = refs[:n], refs[n:2 * n]
        send_sems, recv_sems, local_sems = refs[2 * n:]
        x, y, c = _place()
        my_chip = 2 * x + y
        peers = [(1 - x, y), (x, 1 - y), (1 - x, 1 - y)]
        copies, local = [], []
        for a in range(n):
            own = pltpu.make_async_copy(ins[a].at[my_chip], outs[a].at[my_chip], local_sems.at[a])
            own.start()
            local.append(own)
            for k, (tx, ty) in enumerate(peers):
                cp = pltpu.make_async_remote_copy(
                    src_ref=ins[a].at[2 * tx + ty], dst_ref=outs[a].at[my_chip],
                    send_sem=send_sems.at[a, k], recv_sem=recv_sems.at[a, k], device_id=(tx, ty, c), device_id_type=MESH)
                cp.start()
                copies.append(cp)
        for cp in copies:
            cp.wait()
        for own in local:
            own.wait()

    return pl.pallas_call(
        body, name="grad_chip_exchange",
        out_shape=[jax.ShapeDtypeStruct(v.shape, v.dtype) for v in sums],
        in_specs=[ANY] * n, out_specs=[ANY] * n,
        scratch_shapes=[pltpu.SemaphoreType.DMA((n, 3)), pltpu.SemaphoreType.DMA((n, 3)), pltpu.SemaphoreType.DMA((n,))],
    )(*sums)


def _adamw_math(w, g, m, v):
    m_new = ADAM_B1 * m + (1.0 - ADAM_B1) * g
    v_new = ADAM_B2 * v + (1.0 - ADAM_B2) * (g * g)
    m_hat = m_new / (1.0 - ADAM_B1 ** ADAM_STEP)
    v_hat = v_new / (1.0 - ADAM_B2 ** ADAM_STEP)
    delta = -ADAM_LR * (m_hat / (jnp.sqrt(v_hat) + ADAM_EPS) + ADAM_WD * w)
    return delta, m_new, v_new


def _adamw(slots, w, m, v, name):
    k, r, cdim = slots.shape
    tr = _grad_row_tile(r)

    def body(s_ref, w_ref, m_ref, v_ref, g_ref, d_ref, mo_ref, vo_ref):
        g = s_ref[0]
        for q in range(1, k):
            g = g + s_ref[q]
        g_ref[...] = g
        d_ref[...], mo_ref[...], vo_ref[...] = _adamw_math(w_ref[...], g, m_ref[...], v_ref[...])

    blk = pl.BlockSpec((tr, cdim), lambda i: (i, 0))
    return pl.pallas_call(
        body, name=name, grid=(r // tr,), out_shape=[jax.ShapeDtypeStruct((r, cdim), F32)] * 4,
        in_specs=[pl.BlockSpec((k, tr, cdim), lambda i: (0, i, 0)), blk, blk, blk], out_specs=[blk] * 4,
        compiler_params=_cparams("arbitrary"),
    )(slots, w, m, v)


def _stack_cols(g):
    n, r, c = g.shape
    return g.transpose(1, 0, 2).reshape(r, n * c)


def _split_cols(w):
    r, nc = w.shape
    return w.reshape(r, N_DEV, nc // N_DEV).transpose(1, 0, 2)


def _rope_tables(positions):
    inv_freq = ROPE_THETA ** (-jnp.arange(0, MLA_ROPE, 2, dtype=F32) / MLA_ROPE)
    ang = positions.astype(F32).reshape(-1, 1) * inv_freq[None, :]
    cos, sin, zero = jnp.cos(ang), jnp.sin(ang), jnp.zeros_like(ang)
    reps = LANES // MLA_ROPE
    return (jnp.tile(jnp.concatenate([cos, cos], axis=1), (1, reps)),
            jnp.tile(jnp.concatenate([-sin, zero], axis=1), (1, reps)),
            jnp.tile(jnp.concatenate([zero, sin], axis=1), (1, reps)))


def _local_step(x, positions, loss_target, gains, g_in, g_uq, g_ukv, g_o, g_gate, g_up, g_down):
    norm_mix, q_norm, kv_norm, out_mla, out_sb, norm_ffn, norm_final = gains
    d = x.shape[1]
    w_in = _stack_cols(g_in)
    w_a = jnp.concatenate([w_in[:, :416], jnp.zeros((d, 96), BF16), w_in[:, 416:]], axis=1)
    w_uq = jnp.concatenate([g_uq[:, :, :MLA_NOPE].transpose(1, 0, 2).reshape(Q_LORA, -1),
                            g_uq[:, :, MLA_NOPE:].transpose(1, 0, 2).reshape(Q_LORA, -1)], axis=1)
    w_ukv = jnp.concatenate([g_ukv[:, :, :MLA_NOPE].transpose(1, 0, 2).reshape(KV_LORA, -1),
                             g_ukv[:, :, MLA_NOPE:].transpose(1, 0, 2).reshape(KV_LORA, -1)], axis=1)
    w_o = g_o.reshape(-1, d)
    w_gate, w_up = _stack_cols(g_gate), _stack_cols(g_up)
    w_down = g_down.reshape(-1, d)
    cos, sin_a, sin_b = _rope_tables(positions)

    u, cq, ckv, cqn, ckvn, qn, qr, kv, kr, qkv_sb = _proj_in_fwd(x, norm_mix, w_a, q_norm, w_uq, kv_norm, w_ukv, cos, sin_a, sin_b)
    o_mla, lse = _mla_fwd(qn, qr, kv, kr)
    o_sb, tot = _sb_fwd(qkv_sb)
    merged, h1, f = _attn_out_fwd(o_mla, o_sb, out_mla, out_sb, w_o, x, norm_ffn)
    gate, up, h2 = _ffn_fwd(f, h1, w_gate, w_up, w_down)
    loss, dh2, dg_final = _final_loss(h2, loss_target, norm_final.reshape(1, d))

    dgate, dup, act, dh1, dg_ffn = _ffn_bwd(dh2, gate, up, h1, norm_ffn, w_down.T, w_gate.T, w_up.T)
    dw_down = _tn_matmul(act, dh2, "dw_down")
    dw_gate = _tn_matmul(f, dgate, "dw_gate")
    dw_up = _tn_matmul(f, dup, "dw_up")
    do_mla, do_sb, dg_mla, dg_sb = _attn_out_bwd(dh1, w_o.T, o_mla, o_sb, out_mla, out_sb)
    dw_o = _tn_matmul(merged, dh1, "dw_o")
    dq_sb, dk_sb, dv_sb = _sb_bwd(qkv_sb, do_sb, tot)
    dqn, dqr, dkn, dv, dkr = _mla_bwd(qn, qr, kv, kr, do_mla, o_mla, lse)
    dx, dproj, dq, dkv, dg_q, dg_kv, dg_mix = _proj_in_bwd(
        dqn, dqr, dkn, dv, dkr, dq_sb, dk_sb, dv_sb, cq, ckv, x, dh1, cos, sin_a, sin_b,
        q_norm, kv_norm, norm_mix, w_uq.T, w_ukv.T, w_a.T)
    dw_a = _tn_matmul(u, dproj, "dw_in")
    dw_uq = _tn_matmul(cqn, dq, "dw_uq")
    dw_ukv = _tn_matmul(ckvn, dkv, "dw_ukv")

    p_in = _split_cols(jnp.concatenate([dw_a[:, :416], dw_a[:, 512:]], axis=1))
    p_uq = jnp.concatenate([dw_uq[:, :512].reshape(Q_LORA, MLA_HEADS, MLA_NOPE),
                            dw_uq[:, 512:].reshape(Q_LORA, MLA_HEADS, MLA_ROPE)], axis=2).transpose(1, 0, 2)
    p_ukv = jnp.concatenate([dw_ukv[:, :512].reshape(KV_LORA, MLA_HEADS, MLA_NOPE),
                             dw_ukv[:, 512:].reshape(KV_LORA, MLA_HEADS, HEAD_DIM)], axis=2).transpose(1, 0, 2)
    p_o = dw_o.reshape(N_DEV, -1, d)
    p_down = dw_down.reshape(N_DEV, -1, d)
    parts = [p_in, p_uq, p_ukv, p_o, _split_cols(dw_gate), _split_cols(dw_up), p_down]
    gain_grads = [dg_mix, dg_q, dg_kv, dg_mla, dg_sb, dg_ffn, dg_final]
    return loss, dx, parts, gain_grads


def kernel(x, positions, norm_mix, w_in, q_latent_norm, w_uq, kv_latent_norm, w_ukv, out_norm_mla, out_norm_sb, w_o, norm_ffn, w_gate, w_up, w_down, norm_final, loss_target, m_norm_mix, m_w_in, m_q_latent_norm, m_w_uq, m_kv_latent_norm, m_w_ukv, m_out_norm_mla, m_out_norm_sb, m_w_o, m_norm_ffn, m_w_gate, m_w_up, m_w_down, m_norm_final, v_norm_mix, v_w_in, v_q_latent_norm, v_w_uq, v_kv_latent_norm, v_w_ukv, v_out_norm_mla, v_out_norm_sb, v_w_o, v_norm_ffn, v_w_gate, v_w_up, v_w_down, v_norm_final):
    mats = [w_in, w_uq, w_ukv, w_o, w_gate, w_up, w_down]
    mat_m = [m_w_in, m_w_uq, m_w_ukv, m_w_o, m_w_gate, m_w_up, m_w_down]
    mat_v = [v_w_in, v_w_uq, v_w_ukv, v_w_o, v_w_gate, v_w_up, v_w_down]
    mat_names = ["w_in", "w_uq", "w_ukv", "w_o", "w_gate", "w_up", "w_down"]
    gains = [norm_mix, q_latent_norm, kv_latent_norm, out_norm_mla, out_norm_sb, norm_ffn, norm_final]
    gain_m = [m_norm_mix, m_q_latent_norm, m_kv_latent_norm, m_out_norm_mla, m_out_norm_sb, m_norm_ffn, m_norm_final]
    gain_v = [v_norm_mix, v_q_latent_norm, v_kv_latent_norm, v_out_norm_mla, v_out_norm_sb, v_norm_ffn, v_norm_final]

    gathered = _all_gather([w[0].astype(BF16) for w in mats], "weight_all_gather")

    gains2d = [g.reshape(1, -1) for g in gains]
    loss_part, dx, parts, gain_grads = _local_step(x[0], positions[0], loss_target[0], gains2d, *gathered)

    from_sibling = _sibling_exchange(parts)
    chip_sums = [_chip_sum(p, r, "chip_sum_" + nm) for p, r, nm in zip(parts, from_sibling, mat_names)]
    slots = _chip_exchange(chip_sums)
    mat_out = [_adamw(sl, w[0], m[0], v[0], "adamw_" + nm)
               for sl, w, m, v, nm in zip(slots, mats, mat_m, mat_v, mat_names)]

    sizes = [g.size for g in gains]
    used = sum(sizes) + LANES
    rows = -(-used // (8 * LANES)) * 8

    def pack(vals, tail):
        flat = jnp.concatenate([v.reshape(-1) for v in vals] + [tail])
        return jnp.pad(flat, (0, rows * LANES - flat.size)).reshape(rows, LANES)

    zeros_tail = jnp.zeros((LANES,), F32)
    small = _all_gather([pack(gain_grads, loss_part.reshape(-1))], "gain_all_gather")[0]
    g_s, d_s, m_s, v_s = _adamw(small, pack(gains, zeros_tail), pack(gain_m, zeros_tail), pack(gain_v, zeros_tail), "adamw_gains")

    def unpack(packed):
        flat = packed.reshape(-1)
        outs, off = [], 0
        for g, n in zip(gains, sizes):
            outs.append(flat[off:off + n].reshape(g.shape))
            off += n
        return outs

    loss = g_s.reshape(-1)[sum(sizes)]

    order = ["norm_mix", "w_in", "q_latent_norm", "w_uq", "kv_latent_norm", "w_ukv", "out_norm_mla", "out_norm_sb",
             "w_o", "norm_ffn", "w_gate", "w_up", "w_down", "norm_final"]
    gain_names = ["norm_mix", "q_latent_norm", "kv_latent_norm", "out_norm_mla", "out_norm_sb", "norm_ffn", "norm_final"]
    result = [loss, dx[None]]
    for kind in range(4):
        small_parts = dict(zip(gain_names, unpack([g_s, d_s, m_s, v_s][kind])))
        mat_parts = {nm: out[kind][None] for nm, out in zip(mat_names, mat_out)}
        result += [small_parts[nm] if nm in small_parts else mat_parts[nm] for nm in order]
    return tuple(result)
```

```python
import functools
import math

import jax
import jax.numpy as jnp
from jax import lax
from jax.experimental import pallas as pl
from jax.experimental.pallas import tpu as pltpu

F32 = jnp.float32
BF16 = jnp.bfloat16
MESH = pl.DeviceIdType.MESH

EPS = 1e-6
ROPE_THETA = 10000.0
MLA_HEADS = 8
MLA_NOPE = 64
MLA_ROPE = 32
SB_HEADS = 8
HEAD_DIM = 64
Q_LORA = 256
KV_LORA = 128
MLA_SCALE = 1.0 / math.sqrt(MLA_NOPE + MLA_ROPE)
SB_SCALE = 1.0 / math.sqrt(HEAD_DIM)
N_DEV = 8

ADAM_LR = 0.001
ADAM_B1 = 0.9
ADAM_B2 = 0.999
ADAM_EPS = 1e-08
ADAM_WD = 0.01
ADAM_STEP = 10

LANES = 128
ATT_TILE = 512
TRI = 256
ROW_TILE = 512
FFN_BWD_ROW_TILE = 256
PROJ_BWD_ROW_TILE = 256
VMEM_LIMIT = 56 * 1024 * 1024
NEG = -1e30


def _cparams(*sem):
    return pltpu.CompilerParams(dimension_semantics=sem, vmem_limit_bytes=VMEM_LIMIT)


def _dot(a, b):
    return jnp.dot(a, b, preferred_element_type=F32)


def _dot_nt(a, b):
    return lax.dot_general(a, b, (((1,), (1,)), ((), ())), preferred_element_type=F32)


def _dot_tn(a, b):
    return lax.dot_general(a, b, (((0,), (0,)), ((), ())), preferred_element_type=F32)


def _rms(x, g):
    r = lax.rsqrt(jnp.mean(x * x, axis=-1, keepdims=True) + EPS)
    return x * r * g


def _rms_bwd(x, g, dy):
    r = lax.rsqrt(jnp.mean(x * x, axis=-1, keepdims=True) + EPS)
    n = x * r
    dn = dy * g
    dx = r * (dn - n * jnp.mean(dn * n, axis=-1, keepdims=True))
    return dx, jnp.sum(dy * n, axis=0, keepdims=True)


def _rope(x, cos, sin_a, sin_b):
    return x * cos + pltpu.roll(x, 112, 1) * sin_a + pltpu.roll(x, 16, 1) * sin_b


def _rope_t(g, cos, sin_a, sin_b):
    return g * cos + pltpu.roll(g * sin_a, 16, 1) + pltpu.roll(g * sin_b, 112, 1)


def _row_spec(tm, width):
    return pl.BlockSpec((tm, width), lambda r: (r, 0))


def _full_spec(shape):
    return pl.BlockSpec(shape, lambda *_: (0,) * len(shape))


def _accumulate(ref, val, first):
    @pl.when(first)
    def _():
        ref[...] = val

    @pl.when(jnp.logical_not(first))
    def _():
        ref[...] += val


def _proj_in_fwd(x, g_mix, w_a, g_q, w_uq, g_kv, w_ukv, cos, sin_a, sin_b):
    s, d = x.shape
    tm = min(ROW_TILE, s)

    def body(x_ref, gm_ref, wa_ref, gq_ref, wuq_ref, gkv_ref, wukv_ref, cos_ref, sa_ref, sb_ref,
             u_ref, cq_ref, ckv_ref, cqn_ref, ckvn_ref, qn_ref, qr_ref, kv_ref, kr_ref, sbq_ref):
        u = _rms(x_ref[...], gm_ref[...]).astype(BF16)
        u_ref[...] = u
        cq = _dot(u, wa_ref[:, 0:256])
        ckv = _dot(u, wa_ref[:, 256:384])
        kr = _dot(u, wa_ref[:, 384:512])
        cq_ref[...] = cq
        ckv_ref[...] = ckv
        cqn = _rms(cq, gq_ref[...]).astype(BF16)
        ckvn = _rms(ckv, gkv_ref[...]).astype(BF16)
        cqn_ref[...] = cqn
        ckvn_ref[...] = ckvn
        cos_t, sa_t, sb_t = cos_ref[...], sa_ref[...], sb_ref[...]
        qn_ref[...] = (_dot(cqn, wuq_ref[:, 0:512]) * MLA_SCALE).astype(BF16)
        for half in range(2):
            lo = 512 + half * LANES
            qr = _dot(cqn, wuq_ref[:, lo:lo + LANES])
            qr_ref[:, half * LANES:(half + 1) * LANES] = (_rope(qr, cos_t, sa_t, sb_t) * MLA_SCALE).astype(BF16)
        kv_ref[...] = _dot(ckvn, wukv_ref[...]).astype(BF16)
        krt = kr + pltpu.roll(kr, 32, 1) + pltpu.roll(kr, 64, 1) + pltpu.roll(kr, 96, 1)
        kr_ref[...] = _rope(krt, cos_t, sa_t, sb_t).astype(BF16)
        sbq_ref[:, 0:512] = (_dot(u, wa_ref[:, 512:1024]) * SB_SCALE).astype(BF16)
        sbq_ref[:, 512:1536] = _dot(u, wa_ref[:, 1024:2048]).astype(BF16)

    outs = [
        jax.ShapeDtypeStruct((s, d), BF16),
        jax.ShapeDtypeStruct((s, 256), F32),
        jax.ShapeDtypeStruct((s, 128), F32),
        jax.ShapeDtypeStruct((s, 256), BF16),
        jax.ShapeDtypeStruct((s, 128), BF16),
        jax.ShapeDtypeStruct((s, 512), BF16),
        jax.ShapeDtypeStruct((s, 256), BF16),
        jax.ShapeDtypeStruct((s, 1024), BF16),
        jax.ShapeDtypeStruct((s, 128), BF16),
        jax.ShapeDtypeStruct((s, 1536), BF16),
    ]
    return pl.pallas_call(
        body, name="proj_in_fwd", grid=(s // tm,), out_shape=outs,
        in_specs=[_row_spec(tm, d), _full_spec(g_mix.shape), _full_spec(w_a.shape), _full_spec(g_q.shape),
                  _full_spec(w_uq.shape), _full_spec(g_kv.shape), _full_spec(w_ukv.shape),
                  _row_spec(tm, LANES), _row_spec(tm, LANES), _row_spec(tm, LANES)],
        out_specs=[_row_spec(tm, o.shape[1]) for o in outs],
        compiler_params=_cparams("arbitrary"),
    )(x, g_mix, w_a, g_q, w_uq, g_kv, w_ukv, cos, sin_a, sin_b)


def _attn_out_fwd(o_mla, o_sb, g_mla, g_sb, w_o, x, g_ffn):
    s, d = x.shape
    tm = min(ROW_TILE, s)

    def body(oa_ref, ob_ref, ga_ref, gb_ref, wo_ref, x_ref, gf_ref, merged_ref, h1_ref, f_ref):
        na = _rms(oa_ref[...], ga_ref[...]).astype(BF16)
        nb = _rms(ob_ref[...], gb_ref[...]).astype(BF16)
        merged_ref[:, 0:512] = na
        merged_ref[:, 512:1024] = nb
        h1 = x_ref[...] + _dot(na, wo_ref[0:512, :]) + _dot(nb, wo_ref[512:1024, :])
        h1_ref[...] = h1
        f_ref[...] = _rms(h1, gf_ref[...]).astype(BF16)

    outs = [jax.ShapeDtypeStruct((s, d), BF16), jax.ShapeDtypeStruct((s, d), F32), jax.ShapeDtypeStruct((s, d), BF16)]
    return pl.pallas_call(
        body, name="attn_out_fwd", grid=(s // tm,), out_shape=outs,
        in_specs=[_row_spec(tm, 512), _row_spec(tm, 512), _full_spec(g_mla.shape), _full_spec(g_sb.shape),
                  _full_spec(w_o.shape), _row_spec(tm, d), _full_spec(g_ffn.shape)],
        out_specs=[_row_spec(tm, d)] * 3,
        compiler_params=_cparams("arbitrary"),
    )(o_mla, o_sb, g_mla, g_sb, w_o, x, g_ffn)


def _ffn_tile(d_ff):
    return d_ff // 2 if (d_ff // 2) % LANES == 0 else d_ff


def _ffn_fwd(f, h1, w_gate, w_up, w_down):
    s, d = h1.shape
    d_ff = w_gate.shape[1]
    tm = min(ROW_TILE, s)
    tf = _ffn_tile(d_ff)

    def body(f_ref, h1_ref, wg_ref, wu_ref, wd_ref, gate_ref, up_ref, h2_ref):
        j = pl.program_id(1)
        fb = f_ref[...]
        gate = _dot(fb, wg_ref[...])
        up = _dot(fb, wu_ref[...])
        gate_ref[...] = gate.astype(BF16)
        up_ref[...] = up.astype(BF16)
        act = (gate * jax.nn.sigmoid(gate) * up).astype(BF16)
        part = _dot(act, wd_ref[...])

        @pl.when(j == 0)
        def _():
            h2_ref[...] = h1_ref[...] + part

        @pl.when(j != 0)
        def _():
            h2_ref[...] += part

    outs = [jax.ShapeDtypeStruct((s, d_ff), BF16), jax.ShapeDtypeStruct((s, d_ff), BF16), jax.ShapeDtypeStruct((s, d), F32)]
    return pl.pallas_call(
        body, name="ffn_fwd", grid=(s // tm, d_ff // tf), out_shape=outs,
        in_specs=[pl.BlockSpec((tm, d), lambda r, j: (r, 0)), pl.BlockSpec((tm, d), lambda r, j: (r, 0)),
                  pl.BlockSpec((d, tf), lambda r, j: (0, j)), pl.BlockSpec((d, tf), lambda r, j: (0, j)),
                  pl.BlockSpec((tf, d), lambda r, j: (j, 0))],
        out_specs=[pl.BlockSpec((tm, tf), lambda r, j: (r, j)), pl.BlockSpec((tm, tf), lambda r, j: (r, j)),
                   pl.BlockSpec((tm, d), lambda r, j: (r, 0))],
        compiler_params=_cparams("arbitrary", "arbitrary"),
    )(f, h1, w_gate, w_up, w_down)


def _final_loss(h2, target, g_final):
    s, d = h2.shape
    tm = min(ROW_TILE, s)

    def body(h2_ref, t_ref, g_ref, loss_ref, dh2_ref, dg_ref):
        first = pl.program_id(0) == 0
        h2v = h2_ref[...]
        g = g_ref[...]
        diff = _rms(h2v, g) - t_ref[...]
        part = 0.5 * jnp.sum(jnp.mean(diff * diff, axis=-1, keepdims=True), axis=0, keepdims=True)
        _accumulate(loss_ref, jnp.broadcast_to(part, loss_ref.shape), first)
        dx, dg = _rms_bwd(h2v, g, diff * (1.0 / d))
        dh2_ref[...] = dx
        _accumulate(dg_ref, dg, first)

    outs = [jax.ShapeDtypeStruct((1, LANES), F32), jax.ShapeDtypeStruct((s, d), F32), jax.ShapeDtypeStruct((1, d), F32)]
    return pl.pallas_call(
        body, name="final_loss", grid=(s // tm,), out_shape=outs,
        in_specs=[_row_spec(tm, d), _row_spec(tm, d), _full_spec((1, d))],
        out_specs=[_full_spec((1, LANES)), _row_spec(tm, d), _full_spec((1, d))],
        compiler_params=_cparams("arbitrary"),
    )(h2, target, g_final)


def _tile_iotas(t):
    return lax.broadcasted_iota(jnp.int32, (t, t), 0), lax.broadcasted_iota(jnp.int32, (t, t), 1)


def _split_bf16(v):
    hi = v.astype(BF16)
    return hi, (v - hi.astype(F32)).astype(BF16)


def _mla_fwd(qn, qr, kv, kr):
    s = qn.shape[0]
    t = min(ATT_TILE, s)
    pairs = MLA_HEADS // 2

    def body(qn_ref, qr_ref, kn_ref, v_ref, kr_ref, o_ref, lse_ref, qcat_ref, m_ref, l_ref, acc_ref):
        hp, i = pl.program_id(0), pl.program_id(1)
        lane = lax.broadcasted_iota(jnp.int32, (1, LANES), 1)
        row, col = _tile_iotas(t)
        causal = col <= row
        q_pair, q_quad = qn_ref[...], qr_ref[...]
        zero = jnp.zeros_like(q_pair)
        for hh in range(2):
            in_head = (lane // HEAD_DIM) == hh
            in_rope = (lane // MLA_ROPE) == (hp % 2) * 2 + hh
            qcat_ref[hh, :, 0:LANES] = jnp.where(in_head, q_pair, zero)
            qcat_ref[hh, :, LANES:2 * LANES] = jnp.where(in_rope, q_quad, zero)
        m_ref[...] = jnp.full_like(m_ref, NEG)
        l_ref[...] = jnp.zeros_like(l_ref)
        acc_ref[...] = jnp.zeros_like(acc_ref)

        def tile(j, masked):
            rows = pl.ds(pl.multiple_of(j * t, t), t)
            kcat = jnp.concatenate([kn_ref[rows, :], kr_ref[rows, :]], axis=1)
            v_ones = jnp.concatenate([v_ref[rows, :], jnp.ones((t, LANES), BF16)], axis=1)
            for hh in range(2):
                sc = _dot_nt(qcat_ref[hh], kcat)
                if masked:
                    sc = jnp.where(causal, sc, NEG)
                m = m_ref[hh]
                m_new = jnp.maximum(m, jnp.max(sc, axis=-1, keepdims=True))
                alpha = jnp.exp(m - m_new)
                p = jnp.exp(sc - jnp.concatenate([m_new] * (t // LANES), axis=1))
                pv = _dot(p.astype(BF16), v_ones)
                l_ref[hh] = alpha * l_ref[hh] + pv[:, LANES:]
                acc_ref[hh] = alpha * acc_ref[hh] + pv[:, :LANES]
                m_ref[hh] = m_new

        tile(i, True)

        def step(j, carry):
            tile(j, False)
            return carry

        lax.fori_loop(0, i, step, 0)
        first = (lane // HEAD_DIM) == 0
        o_ref[...] = jnp.where(first, acc_ref[0] / l_ref[0], acc_ref[1] / l_ref[1])
        lse_ref[...] = jnp.where(first, m_ref[0] + jnp.log(l_ref[0]), m_ref[1] + jnp.log(l_ref[1]))

    outs = [jax.ShapeDtypeStruct((s, 512), F32), jax.ShapeDtypeStruct((pairs, s, LANES), F32)]
    return pl.pallas_call(
        body, name="mla_fwd", grid=(pairs, s // t), out_shape=outs,
        in_specs=[pl.BlockSpec((t, LANES), lambda hp, i: (i, hp)), pl.BlockSpec((t, LANES), lambda hp, i: (i, hp // 2)),
                  pl.BlockSpec((s, LANES), lambda hp, i: (0, hp)), pl.BlockSpec((s, LANES), lambda hp, i: (0, 4 + hp)),
                  pl.BlockSpec((s, LANES), lambda hp, i: (0, 0))],
        out_specs=[pl.BlockSpec((t, LANES), lambda hp, i: (i, hp)), pl.BlockSpec((None, t, LANES), lambda hp, i: (hp, i, 0))],
        scratch_shapes=[pltpu.VMEM((2, t, 2 * LANES), BF16), pltpu.VMEM((2, t, LANES), F32), pltpu.VMEM((2, t, LANES), F32),
                        pltpu.VMEM((2, t, LANES), F32)],
        compiler_params=_cparams("arbitrary", "arbitrary"),
    )(qn, qr, kv, kv, kr)


def _sb_scores(qm, k, strict, masked):
    z = _dot_nt(qm, k)
    log_b = jnp.minimum(z, 0.0) - jnp.log(1.0 + jnp.exp(-jnp.abs(z)))
    log_1m = log_b - z
    if masked:
        log_1m = jnp.where(strict, log_1m, 0.0)
    return log_1m, log_b


def _running_sums(x, tri, carry, suffix, split):
    t, w = x.shape
    nb = max(w // TRI, 1)
    bw = w // nb
    blocks = [x[:, b * bw:(b + 1) * bw] for b in range(nb)]
    totals = [jnp.broadcast_to(jnp.sum(blk, axis=-1, keepdims=True), (t, LANES)) for blk in blocks]
    outs = [None] * nb
    run = carry
    for b in (range(nb - 1, -1, -1) if suffix else range(nb)):
        if split:
            hi, lo = _split_bf16(blocks[b])
            ins = _dot(hi, tri) + _dot(lo, tri)
        else:
            ins = _dot(blocks[b].astype(BF16), tri)
        outs[b] = ins + jnp.concatenate([run] * (bw // LANES), axis=1)
        run = run + totals[b]
    return (outs[0] if nb == 1 else jnp.concatenate(outs, axis=1)), run


def _tri(t, rel):
    n = min(TRI, t)
    row, col = _tile_iotas(n)
    return rel(row, col).astype(BF16)


def _sb_fwd(qkv):
    s = qkv.shape[0]
    t = min(ATT_TILE, s)
    pairs = SB_HEADS // 2

    def body(q_ref, k_ref, v_ref, o_ref, tot_ref, qm_ref, right_ref, acc_ref):
        i = pl.program_id(1)
        lane = lax.broadcasted_iota(jnp.int32, (1, LANES), 1)
        row, col = _tile_iotas(t)
        strict = col < row
        t_suffix = _tri(t, lambda r, c: r > c)
        q_pair = q_ref[...]
        for hh in range(2):
            qm_ref[hh] = jnp.where((lane // HEAD_DIM) == hh, q_pair, jnp.zeros_like(q_pair))
        right_ref[...] = jnp.zeros_like(right_ref)
        acc_ref[...] = jnp.zeros_like(acc_ref)

        def tile(j, masked):
            rows = pl.ds(pl.multiple_of(j * t, t), t)
            k, v = k_ref[rows, :], v_ref[rows, :]
            for hh in range(2):
                log_1m, log_b = _sb_scores(qm_ref[hh], k, strict, masked)
                later, right_ref[hh] = _running_sums(log_1m, t_suffix, right_ref[hh], True, True)
                a = jnp.exp(log_b + later)
                if masked:
                    a = jnp.where(strict, a, 0.0)
                acc_ref[hh] += _dot(a.astype(BF16), v)

        tile(i, True)

        def step(n, carry):
            tile(i - 1 - n, False)
            return carry

        lax.fori_loop(0, i, step, 0)
        first = (lane // HEAD_DIM) == 0
        o_ref[...] = jnp.where(first, acc_ref[0], acc_ref[1])
        tot_ref[...] = jnp.where(first, right_ref[0], right_ref[1])

    outs = [jax.ShapeDtypeStruct((s, 512), F32), jax.ShapeDtypeStruct((pairs, s, LANES), F32)]
    return pl.pallas_call(
        body, name="sb_fwd", grid=(pairs, s // t), out_shape=outs,
        in_specs=[pl.BlockSpec((t, LANES), lambda hp, i: (i, hp)), pl.BlockSpec((s, LANES), lambda hp, i: (0, 4 + hp)),
                  pl.BlockSpec((s, LANES), lambda hp, i: (0, 8 + hp))],
        out_specs=[pl.BlockSpec((t, LANES), lambda hp, i: (i, hp)), pl.BlockSpec((None, t, LANES), lambda hp, i: (hp, i, 0))],
        scratch_shapes=[pltpu.VMEM((2, t, LANES), BF16), pltpu.VMEM((2, t, LANES), F32), pltpu.VMEM((2, t, LANES), F32)],
        compiler_params=_cparams("arbitrary", "arbitrary"),
    )(qkv, qkv, qkv)


def _sb_bwd(qkv, do, tot):
    s = qkv.shape[0]
    t = min(ATT_TILE, s)
    pairs = SB_HEADS // 2

    def body(q_ref, k_ref, v_ref, do_ref, tot_ref, dq_ref, dk_ref, dv_ref,
             qm_ref, dob_ref, total_s, left_l, left_g, dq_s):
        i = pl.program_id(1)

        @pl.when(i == 0)
        def _():
            dk_ref[...] = jnp.zeros_like(dk_ref)
            dv_ref[...] = jnp.zeros_like(dv_ref)

        lane = lax.broadcasted_iota(jnp.int32, (1, LANES), 1)
        row, col = _tile_iotas(t)
        strict = col < row
        t_incl = _tri(t, lambda r, c: r <= c)
        t_excl = _tri(t, lambda r, c: r < c)
        q_pair, do_pair, tot_pair = q_ref[...], do_ref[...], tot_ref[...]
        for hh in range(2):
            in_head = (lane // HEAD_DIM) == hh
            qm_ref[hh] = jnp.where(in_head, q_pair, jnp.zeros_like(q_pair))
            dob_ref[hh] = jnp.where(in_head, do_pair, 0.0).astype(BF16)
            total_s[hh] = jnp.broadcast_to(
                jnp.sum(jnp.where(lane == hh * HEAD_DIM, tot_pair, 0.0), axis=-1, keepdims=True), (t, LANES))
        left_l[...] = jnp.zeros_like(left_l)
        left_g[...] = jnp.zeros_like(left_g)
        dq_s[...] = jnp.zeros_like(dq_s)
        reps = t // LANES

        def tile(j, masked):
            rows = pl.ds(pl.multiple_of(j * t, t), t)
            k, v = k_ref[rows, :], v_ref[rows, :]
            for hh in range(2):
                qm, dob = qm_ref[hh], dob_ref[hh]
                log_1m, log_b = _sb_scores(qm, k, strict, masked)
                upto, left_l[hh] = _running_sums(log_1m, t_incl, left_l[hh], False, True)
                a = jnp.exp(log_b + (jnp.concatenate([total_s[hh]] * reps, axis=1) - upto))
                if masked:
                    a = jnp.where(strict, a, 0.0)
                g = a * _dot_nt(dob, v)
                before, left_g[hh] = _running_sums(g, t_excl, left_g[hh], False, False)
                dz = g - jnp.exp(log_b) * (g + before)
                if masked:
                    dz = jnp.where(strict, dz, 0.0)
                dzb = dz.astype(BF16)
                dq_s[hh] += _dot(dzb, k)
                dk_ref[rows, :] += _dot_tn(dzb, qm)
                dv_ref[rows, :] += _dot_tn(a.astype(BF16), dob)

        def step(j, carry):
            tile(j, False)
            return carry

        lax.fori_loop(0, i, step, 0)
        tile(i, True)
        dq_ref[...] = jnp.where((lane // HEAD_DIM) == 0, dq_s[0], dq_s[1])

    outs = [jax.ShapeDtypeStruct((s, 512), F32)] * 3
    return pl.pallas_call(
        body, name="sb_bwd", grid=(pairs, s // t), out_shape=outs,
        in_specs=[pl.BlockSpec((t, LANES), lambda hp, i: (i, hp)), pl.BlockSpec((s, LANES), lambda hp, i: (0, 4 + hp)),
                  pl.BlockSpec((s, LANES), lambda hp, i: (0, 8 + hp)), pl.BlockSpec((t, LANES), lambda hp, i: (i, hp)),
                  pl.BlockSpec((None, t, LANES), lambda hp, i: (hp, i, 0))],
        out_specs=[pl.BlockSpec((t, LANES), lambda hp, i: (i, hp)), pl.BlockSpec((s, LANES), lambda hp, i: (0, hp)),
                   pl.BlockSpec((s, LANES), lambda hp, i: (0, hp))],
        scratch_shapes=[pltpu.VMEM((2, t, LANES), BF16), pltpu.VMEM((2, t, LANES), BF16)]
        + [pltpu.VMEM((2, t, LANES), F32)] * 4,
        compiler_params=_cparams("arbitrary", "arbitrary"),
    )(qkv, qkv, qkv, do, tot)


def _mla_bwd(qn, qr, kv, kr, do, o, lse):
    s = qn.shape[0]
    t = min(ATT_TILE, s)
    pairs = MLA_HEADS // 2

    def body(qn_ref, qr_ref, kn_ref, v_ref, kr_ref, do_ref, o_ref, lse_ref, dqn_ref, dqr_ref, dkn_ref, dv_ref, dkr_ref,
             qcat_ref, dob_ref, lse_s, delta_s, dq_s):
        hp, i = pl.program_id(0), pl.program_id(1)

        @pl.when(i == 0)
        def _():
            dkn_ref[...] = jnp.zeros_like(dkn_ref)
            dv_ref[...] = jnp.zeros_like(dv_ref)
            dkr_ref[...] = jnp.zeros_like(dkr_ref)

        lane = lax.broadcasted_iota(jnp.int32, (1, LANES), 1)
        row, col = _tile_iotas(t)
        causal = col <= row
        q_pair, q_quad, do_pair, lse_pair = qn_ref[...], qr_ref[...], do_ref[...], lse_ref[...]
        do_o = do_pair * o_ref[...]
        zero = jnp.zeros_like(q_pair)
        ropes = []
        for hh in range(2):
            in_head = (lane // HEAD_DIM) == hh
            in_rope = (lane // MLA_ROPE) == (hp % 2) * 2 + hh
            ropes.append(in_rope)
            qcat_ref[hh, :, 0:LANES] = jnp.where(in_head, q_pair, zero)
            qcat_ref[hh, :, LANES:2 * LANES] = jnp.where(in_rope, q_quad, zero)
            dob_ref[hh] = jnp.where(in_head, do_pair, 0.0).astype(BF16)
            delta_s[hh] = jnp.broadcast_to(jnp.sum(jnp.where(in_head, do_o, 0.0), axis=-1, keepdims=True), (t, LANES))
            lse_s[hh] = jnp.broadcast_to(
                jnp.sum(jnp.where(lane == hh * HEAD_DIM, lse_pair, 0.0), axis=-1, keepdims=True), (t, LANES))
        dq_s[...] = jnp.zeros_like(dq_s)
        reps = t // LANES

        def tile(j, masked):
            rows = pl.ds(pl.multiple_of(j * t, t), t)
            kcat = jnp.concatenate([kn_ref[rows, :], kr_ref[rows, :]], axis=1)
            v = v_ref[rows, :]
            for hh in range(2):
                qcat, dob = qcat_ref[hh], dob_ref[hh]
                p = jnp.exp(_dot_nt(qcat, kcat) - jnp.concatenate([lse_s[hh]] * reps, axis=1))
                if masked:
                    p = jnp.where(causal, p, 0.0)
                ds = (p * (_dot_nt(dob, v) - jnp.concatenate([delta_s[hh]] * reps, axis=1))).astype(BF16)
                dq_s[hh] += _dot(ds, kcat)
                dkcat = _dot_tn(ds, qcat)
                dkn_ref[rows, :] += dkcat[:, 0:LANES]
                dkr_ref[rows, :] += dkcat[:, LANES:2 * LANES]
                dv_ref[rows, :] += _dot_tn(p.astype(BF16), dob)

        def step(j, carry):
            tile(j, False)
            return carry

        lax.fori_loop(0, i, step, 0)
        tile(i, True)
        dqn_ref[...] = jnp.where((lane // HEAD_DIM) == 0, dq_s[0, :, 0:LANES], dq_s[1, :, 0:LANES])
        dqr_ref[...] = (jnp.where(ropes[0], dq_s[0, :, LANES:2 * LANES], 0.0)
                        + jnp.where(ropes[1], dq_s[1, :, LANES:2 * LANES], 0.0))

    pair_block = pl.BlockSpec((t, LANES), lambda hp, i: (i, hp))
    outs = [jax.ShapeDtypeStruct((s, 512), F32), jax.ShapeDtypeStruct((pairs, s, LANES), F32),
            jax.ShapeDtypeStruct((s, 512), F32), jax.ShapeDtypeStruct((s, 512), F32), jax.ShapeDtypeStruct((pairs, s, LANES), F32)]
    return pl.pallas_call(
        body, name="mla_bwd", grid=(pairs, s // t), out_shape=outs,
        in_specs=[pair_block, pl.BlockSpec((t, LANES), lambda hp, i: (i, hp // 2)),
                  pl.BlockSpec((s, LANES), lambda hp, i: (0, hp)), pl.BlockSpec((s, LANES), lambda hp, i: (0, 4 + hp)),
                  pl.BlockSpec((s, LANES), lambda hp, i: (0, 0)), pair_block, pair_block,
                  pl.BlockSpec((None, t, LANES), lambda hp, i: (hp, i, 0))],
        out_specs=[pair_block, pl.BlockSpec((None, t, LANES), lambda hp, i: (hp, i, 0)),
                   pl.BlockSpec((s, LANES), lambda hp, i: (0, hp)), pl.BlockSpec((s, LANES), lambda hp, i: (0, hp)),
                   pl.BlockSpec((None, s, LANES), lambda hp, i: (hp, 0, 0))],
        scratch_shapes=[pltpu.VMEM((2, t, 2 * LANES), BF16), pltpu.VMEM((2, t, LANES), BF16), pltpu.VMEM((2, t, LANES), F32),
                        pltpu.VMEM((2, t, LANES), F32), pltpu.VMEM((2, t, 2 * LANES), F32)],
        compiler_params=_cparams("arbitrary", "arbitrary"),
    )(qn, qr, kv, kv, kr, do, o, lse)


def _ffn_bwd(dh2, gate, up, h1, g_ffn, w_down_t, w_gate_t, w_up_t):
    s, d = h1.shape
    d_ff = gate.shape[1]
    tm = min(FFN_BWD_ROW_TILE, s)
    tf = _ffn_tile(d_ff)
    nf = d_ff // tf

    def body(dh2_ref, gate_ref, up_ref, h1_ref, g_ref, wdt_ref, wgt_ref, wut_ref,
             dgate_ref, dup_ref, act_ref, dh1_ref, dg_ref, df_ref):
        r, j = pl.program_id(0), pl.program_id(1)
        dact = _dot(dh2_ref[...].astype(BF16), wdt_ref[...])
        gate_v = gate_ref[...].astype(F32)
        up_v = up_ref[...].astype(F32)
        sig = jax.nn.sigmoid(gate_v)
        silu = gate_v * sig
        dup = (dact * silu).astype(BF16)
        dgate = (dact * up_v * (sig * (1.0 + gate_v * (1.0 - sig)))).astype(BF16)
        dgate_ref[...] = dgate
        dup_ref[...] = dup
        act_ref[...] = (silu * up_v).astype(BF16)
        part = _dot(dgate, wgt_ref[...]) + _dot(dup, wut_ref[...])
        _accumulate(df_ref, part, j == 0)

        @pl.when(j == nf - 1)
        def _():
            dx, dg = _rms_bwd(h1_ref[...], g_ref[...], df_ref[...])
            dh1_ref[...] = dh2_ref[...] + dx
            _accumulate(dg_ref, dg, r == 0)

    outs = [jax.ShapeDtypeStruct((s, d_ff), BF16)] * 3 + [jax.ShapeDtypeStruct((s, d), F32), jax.ShapeDtypeStruct((1, d), F32)]
    rows = pl.BlockSpec((tm, d), lambda r, j: (r, 0))
    ff = pl.BlockSpec((tm, tf), lambda r, j: (r, j))
    return pl.pallas_call(
        body, name="ffn_bwd", grid=(s // tm, nf), out_shape=outs,
        in_specs=[rows, ff, ff, rows, pl.BlockSpec((1, d), lambda r, j: (0, 0)),
                  pl.BlockSpec((d, tf), lambda r, j: (0, j)), pl.BlockSpec((tf, d), lambda r, j: (j, 0)),
                  pl.BlockSpec((tf, d), lambda r, j: (j, 0))],
        out_specs=[ff, ff, ff, rows, pl.BlockSpec((1, d), lambda r, j: (0, 0))],
        scratch_shapes=[pltpu.VMEM((tm, d), F32)],
        compiler_params=_cparams("arbitrary", "arbitrary"),
    )(dh2, gate, up, h1, g_ffn, w_down_t, w_gate_t, w_up_t)


def _largest_tile(n, cap):
    for cand in range(cap, 0, -LANES):
        if n % cand == 0:
            return cand
    return n


def _tn_matmul(a, b, name):
    s, m = a.shape
    n = b.shape[1]
    ts = min(ROW_TILE, s)
    tm = _largest_tile(m, 512)
    tn = _largest_tile(n, 512)

    def body(a_ref, b_ref, o_ref):
        part = _dot_tn(a_ref[...].astype(BF16), b_ref[...].astype(BF16))
        _accumulate(o_ref, part, pl.program_id(2) == 0)

    return pl.pallas_call(
        body, name=name, grid=(m // tm, n // tn, s // ts), out_shape=jax.ShapeDtypeStruct((m, n), F32),
        in_specs=[pl.BlockSpec((ts, tm), lambda i, j, k: (k, i)), pl.BlockSpec((ts, tn), lambda i, j, k: (k, j))],
        out_specs=pl.BlockSpec((tm, tn), lambda i, j, k: (i, j)),
        compiler_params=_cparams("arbitrary", "arbitrary", "arbitrary"),
    )(a, b)


def _attn_out_bwd(dh1, w_o_t, o_mla, o_sb, g_mla, g_sb):
    s, d = dh1.shape
    tm = min(ROW_TILE, s)

    def body(dh1_ref, wot_ref, oa_ref, ob_ref, ga_ref, gb_ref, doa_ref, dob_ref, dga_ref, dgb_ref):
        first = pl.program_id(0) == 0
        dh1b = dh1_ref[...].astype(BF16)
        dxa, dga = _rms_bwd(oa_ref[...], ga_ref[...], _dot(dh1b, wot_ref[:, 0:512]))
        dxb, dgb = _rms_bwd(ob_ref[...], gb_ref[...], _dot(dh1b, wot_ref[:, 512:1024]))
        doa_ref[...] = dxa
        dob_ref[...] = dxb
        _accumulate(dga_ref, dga, first)
        _accumulate(dgb_ref, dgb, first)

    outs = [jax.ShapeDtypeStruct((s, 512), F32)] * 2 + [jax.ShapeDtypeStruct((1, 512), F32)] * 2
    return pl.pallas_call(
        body, name="attn_out_bwd", grid=(s // tm,), out_shape=outs,
        in_specs=[_row_spec(tm, d), _full_spec(w_o_t.shape), _row_spec(tm, 512), _row_spec(tm, 512),
                  _full_spec((1, 512)), _full_spec((1, 512))],
        out_specs=[_row_spec(tm, 512), _row_spec(tm, 512), _full_spec((1, 512)), _full_spec((1, 512))],
        compiler_params=_cparams("arbitrary"),
    )(dh1, w_o_t, o_mla, o_sb, g_mla, g_sb)


def _proj_in_bwd(dqn, dqr, dkn, dv, dkr, dq_sb, dk_sb, dv_sb, cq, ckv, x, dh1, cos, sin_a, sin_b,
                 g_q, g_kv, g_mix, w_uq_t, w_ukv_t, w_a_t):
    s, d = x.shape
    tm = min(PROJ_BWD_ROW_TILE, s)

    def body(dqn_ref, dqr_ref, dkn_ref, dv_ref, dkr_ref, dqs_ref, dks_ref, dvs_ref, cq_ref, ckv_ref, x_ref, dh1_ref,
             cos_ref, sa_ref, sb_ref, gq_ref, gkv_ref, gm_ref, wuqt_ref, wukvt_ref, wat_ref,
             dx_ref, dproj_ref, dq_ref, dkv_ref, dgq_ref, dgkv_ref, dgm_ref):
        first = pl.program_id(0) == 0
        lane = lax.broadcasted_iota(jnp.int32, (1, LANES), 1)
        cos_t, sa_t, sb_t = cos_ref[...], sa_ref[...], sb_ref[...]
        dq_ref[:, 0:512] = (dqn_ref[...] * MLA_SCALE).astype(BF16)
        for half in range(2):
            quad = (dqr_ref[2 * half] + dqr_ref[2 * half + 1]) * MLA_SCALE
            dq_ref[:, 512 + half * LANES:512 + (half + 1) * LANES] = _rope_t(quad, cos_t, sa_t, sb_t).astype(BF16)
        dcq, dgq = _rms_bwd(cq_ref[...], gq_ref[...], _dot(dq_ref[...], wuqt_ref[...]))
        _accumulate(dgq_ref, dgq, first)
        dkv_ref[:, 0:512] = dkn_ref[...].astype(BF16)
        dkv_ref[:, 512:1024] = dv_ref[...].astype(BF16)
        dckv, dgkv = _rms_bwd(ckv_ref[...], gkv_ref[...], _dot(dkv_ref[...], wukvt_ref[...]))
        _accumulate(dgkv_ref, dgkv, first)
        g = _rope_t(dkr_ref[0] + dkr_ref[1] + dkr_ref[2] + dkr_ref[3], cos_t, sa_t, sb_t)
        g = g + pltpu.roll(g, 96, 1) + pltpu.roll(g, 64, 1) + pltpu.roll(g, 32, 1)
        dproj_ref[:, 0:256] = dcq.astype(BF16)
        dproj_ref[:, 256:384] = dckv.astype(BF16)
        dproj_ref[:, 384:512] = jnp.where(lane < MLA_ROPE, g, 0.0).astype(BF16)
        dproj_ref[:, 512:1024] = (dqs_ref[...] * SB_SCALE).astype(BF16)
        dproj_ref[:, 1024:1536] = dks_ref[...].astype(BF16)
        dproj_ref[:, 1536:2048] = dvs_ref[...].astype(BF16)
        dxn, dgm = _rms_bwd(x_ref[...], gm_ref[...], _dot(dproj_ref[...], wat_ref[...]))
        dx_ref[...] = dh1_ref[...] + dxn
        _accumulate(dgm_ref, dgm, first)

    quad_spec = pl.BlockSpec((4, tm, LANES), lambda r: (0, r, 0))
    outs = [jax.ShapeDtypeStruct((s, d), F32), jax.ShapeDtypeStruct((s, 2048), BF16), jax.ShapeDtypeStruct((s, 768), BF16),
            jax.ShapeDtypeStruct((s, 1024), BF16), jax.ShapeDtypeStruct((1, 256), F32), jax.ShapeDtypeStruct((1, 128), F32),
            jax.ShapeDtypeStruct((1, d), F32)]
    return pl.pallas_call(
        body, name="proj_in_bwd", grid=(s // tm,), out_shape=outs,
        in_specs=[_row_spec(tm, 512), quad_spec, _row_spec(tm, 512), _row_spec(tm, 512), quad_spec,
                  _row_spec(tm, 512), _row_spec(tm, 512), _row_spec(tm, 512), _row_spec(tm, 256), _row_spec(tm, 128),
                  _row_spec(tm, d), _row_spec(tm, d), _row_spec(tm, LANES), _row_spec(tm, LANES), _row_spec(tm, LANES),
                  _full_spec((1, 256)), _full_spec((1, 128)), _full_spec((1, d)),
                  _full_spec(w_uq_t.shape), _full_spec(w_ukv_t.shape), _full_spec(w_a_t.shape)],
        out_specs=[_row_spec(tm, d), _row_spec(tm, 2048), _row_spec(tm, 768), _row_spec(tm, 1024),
                   _full_spec((1, 256)), _full_spec((1, 128)), _full_spec((1, d))],
        compiler_params=_cparams("arbitrary"),
    )(dqn, dqr, dkn, dv, dkr, dq_sb, dk_sb, dv_sb, cq, ckv, x, dh1, cos, sin_a, sin_b, g_q, g_kv, g_mix,
      w_uq_t, w_ukv_t, w_a_t)


ANY = pl.BlockSpec(memory_space=pl.ANY)


def _place():
    return lax.axis_index("x"), lax.axis_index("y"), lax.axis_index("c")


def _all_gather(shards, name):
    n = len(shards)

    def body(*refs):
        ins, outs = refs[:n], refs[n:2 * n]
        send_sems, recv_sems, local_sems = refs[2 * n:]
        x, y, c = _place()
        me, sibling = (x, y, c), (x, y, 1 - c)
        chips = [(1 - x, y), (x, 1 - y), (1 - x, 1 - y)]

        def slot(a, px, py, pc):
            return outs[a].at[4 * px + 2 * py + pc]

        def copy(a, k, block, to, src=None):
            return pltpu.make_async_remote_copy(
                src_ref=slot(a, *block) if src is None else src, dst_ref=slot(a, *block),
                send_sem=send_sems.at[a, k], recv_sem=recv_sems.at[a, k], device_id=to, device_id_type=MESH)

        mine, first, passed = [], [], []
        for a in range(n):
            own = pltpu.make_async_copy(ins[a], slot(a, *me), local_sems.at[a])
            own.start()
            mine.append(own)
            cps = [copy(a, 0, me, sibling, src=ins[a])]
            cps += [copy(a, 1 + j, me, (*chip, c), src=ins[a]) for j, chip in enumerate(chips)]
            for cp in cps:
                cp.start()
            first += cps
        for a in range(n):
            for j, chip in enumerate(chips):
                copy(a, 1 + j, (*chip, c), me).wait_recv()
                fwd = copy(a, 4 + j, (*chip, c), sibling)
                fwd.start()
                passed.append(fwd)
        for a in range(n):
            copy(a, 0, sibling, me).wait_recv()
            for j, chip in enumerate(chips):
                copy(a, 4 + j, (*chip, 1 - c), me).wait_recv()
        for cp in first + passed:
            cp.wait_send()
        for own in mine:
            own.wait()

    return pl.pallas_call(
        body, name=name,
        out_shape=[jax.ShapeDtypeStruct((N_DEV,) + v.shape, v.dtype) for v in shards],
        in_specs=[ANY] * n, out_specs=[ANY] * n,
        scratch_shapes=[pltpu.SemaphoreType.DMA((n, 7)), pltpu.SemaphoreType.DMA((n, 7)), pltpu.SemaphoreType.DMA((n,))],
    )(*shards)


def _sibling_exchange(parts):
    n = len(parts)

    def body(*refs):
        ins, outs = refs[:n], refs[n:2 * n]
        send_sems, recv_sems = refs[2 * n:]
        x, y, c = _place()
        copies = []
        for a in range(n):
            for q in range(4):
                cp = pltpu.make_async_remote_copy(
                    src_ref=ins[a].at[2 * q + 1 - c], dst_ref=outs[a].at[q],
                    send_sem=send_sems.at[a, q], recv_sem=recv_sems.at[a, q], device_id=(x, y, 1 - c), device_id_type=MESH)
                cp.start()
                copies.append(cp)
        for cp in copies:
            cp.wait()

    return pl.pallas_call(
        body, name="grad_sibling_exchange",
        out_shape=[jax.ShapeDtypeStruct((4,) + v.shape[1:], v.dtype) for v in parts],
        in_specs=[ANY] * n, out_specs=[ANY] * n,
        scratch_shapes=[pltpu.SemaphoreType.DMA((n, 4)), pltpu.SemaphoreType.DMA((n, 4))],
    )(*parts)


def _grad_row_tile(rows):
    return _largest_tile_rows(rows, 256)


def _largest_tile_rows(rows, cap):
    for cand in range(cap, 0, -8):
        if rows % cand == 0:
            return cand
    return rows


def _chip_sum(part, recv, name):
    _, r, cdim = part.shape
    part4 = part.reshape(4, 2, r, cdim)
    tr = _grad_row_tile(r)

    def body(p_ref, s_ref, o_ref):
        c = lax.axis_index("c")
        own = jnp.where(c == 1, p_ref[1], p_ref[0])
        o_ref[...] = own + s_ref[...]

    return pl.pallas_call(
        body, name=name, grid=(4, r // tr), out_shape=jax.ShapeDtypeStruct((4, r, cdim), F32),
        in_specs=[pl.BlockSpec((None, 2, tr, cdim), lambda q, i: (q, 0, i, 0)),
                  pl.BlockSpec((None, tr, cdim), lambda q, i: (q, i, 0))],
        out_specs=pl.BlockSpec((None, tr, cdim), lambda q, i: (q, i, 0)),
        compiler_params=_cparams("arbitrary", "arbitrary"),
    )(part4, recv)


def _chip_exchange(sums):
    n = len(sums)

    def body(*refs):
        ins, outs = refs[:n], refs[n:2 * n]
        send_sems, recv_sems, local_sems = refs[2 * n:]
        x, y, c = _place()
        my_chip = 2 * x + y
        peers = [(1 - x, y), (x, 1 - y), (1 - x, 1 - y)]
        copies, local = [], []
        for a in range(n):
            own = pltpu.make_async_copy(ins[a].at[my_chip], outs[a].at[my_chip], local_sems.at[a])
            own.start()
            local.append(own)
            for k, (tx, ty) in enumerate(peers):
                cp = pltpu.make_async_remote_copy(
                    src_ref=ins[a].at[2 * tx + ty], dst_ref=outs[a].at[my_chip],
                    send_sem=send_sems.at[a, k], recv_sem=recv_sems.at[a, k], device_id=(tx, ty, c), device_id_type=MESH)
                cp.start()
                copies.append(cp)
        for cp in copies:
            cp.wait()
        for own in local:
            own.wait()

    return pl.pallas_call(
        body, name="grad_chip_exchange",
        out_shape=[jax.ShapeDtypeStruct(v.shape, v.dtype) for v in sums],
        in_specs=[ANY] * n, out_specs=[ANY] * n,
        scratch_shapes=[pltpu.SemaphoreType.DMA((n, 3)), pltpu.SemaphoreType.DMA((n, 3)), pltpu.SemaphoreType.DMA((n,))],
    )(*sums)


def _adamw_math(w, g, m, v):
    m_new = ADAM_B1 * m + (1.0 - ADAM_B1) * g
    v_new = ADAM_B2 * v + (1.0 - ADAM_B2) * (g * g)
    m_hat = m_new / (1.0 - ADAM_B1 ** ADAM_STEP)
    v_hat = v_new / (1.0 - ADAM_B2 ** ADAM_STEP)
    delta = -ADAM_LR * (m_hat / (jnp.sqrt(v_hat) + ADAM_EPS) + ADAM_WD * w)
    return delta, m_new, v_new


def _adamw(slots, w, m, v, name):
    k, r, cdim = slots.shape
    tr = _grad_row_tile(r)

    def body(s_ref, w_ref, m_ref, v_ref, g_ref, d_ref, mo_ref, vo_ref):
        g = s_ref[0]
        for q in range(1, k):
            g = g + s_ref[q]
        g_ref[...] = g
        d_ref[...], mo_ref[...], vo_ref[...] = _adamw_math(w_ref[...], g, m_ref[...], v_ref[...])

    blk = pl.BlockSpec((tr, cdim), lambda i: (i, 0))
    return pl.pallas_call(
        body, name=name, grid=(r // tr,), out_shape=[jax.ShapeDtypeStruct((r, cdim), F32)] * 4,
        in_specs=[pl.BlockSpec((k, tr, cdim), lambda i: (0, i, 0)), blk, blk, blk], out_specs=[blk] * 4,
        compiler_params=_cparams("arbitrary"),
    )(slots, w, m, v)


def _stack_cols(g):
    n, r, c = g.shape
    return g.transpose(1, 0, 2).reshape(r, n * c)


def _split_cols(w):
    r, nc = w.shape
    return w.reshape(r, N_DEV, nc // N_DEV).transpose(1, 0, 2)


def _rope_tables(positions):
    inv_freq = ROPE_THETA ** (-jnp.arange(0, MLA_ROPE, 2, dtype=F32) / MLA_ROPE)
    ang = positions.astype(F32).reshape(-1, 1) * inv_freq[None, :]
    cos, sin, zero = jnp.cos(ang), jnp.sin(ang), jnp.zeros_like(ang)
    reps = LANES // MLA_ROPE
    return (jnp.tile(jnp.concatenate([cos, cos], axis=1), (1, reps)),
            jnp.tile(jnp.concatenate([-sin, zero], axis=1), (1, reps)),
            jnp.tile(jnp.concatenate([zero, sin], axis=1), (1, reps)))


def _local_step(x, positions, loss_target, gains, g_in, g_uq, g_ukv, g_o, g_gate, g_up, g_down):
    norm_mix, q_norm, kv_norm, out_mla, out_sb, norm_ffn, norm_final = gains
    d = x.shape[1]
    w_in = _stack_cols(g_in)
    w_a = jnp.concatenate([w_in[:, :416], jnp.zeros((d, 96), BF16), w_in[:, 416:]], axis=1)
    w_uq = jnp.concatenate([g_uq[:, :, :MLA_NOPE].transpose(1, 0, 2).reshape(Q_LORA, -1),
                            g_uq[:, :, MLA_NOPE:].transpose(1, 0, 2).reshape(Q_LORA, -1)], axis=1)
    w_ukv = jnp.concatenate([g_ukv[:, :, :MLA_NOPE].transpose(1, 0, 2).reshape(KV_LORA, -1),
                             g_ukv[:, :, MLA_NOPE:].transpose(1, 0, 2).reshape(KV_LORA, -1)], axis=1)
    w_o = g_o.reshape(-1, d)
    w_gate, w_up = _stack_cols(g_gate), _stack_cols(g_up)
    w_down = g_down.reshape(-1, d)
    cos, sin_a, sin_b = _rope_tables(positions)

    u, cq, ckv, cqn, ckvn, qn, qr, kv, kr, qkv_sb = _proj_in_fwd(x, norm_mix, w_a, q_norm, w_uq, kv_norm, w_ukv, cos, sin_a, sin_b)
    o_mla, lse = _mla_fwd(qn, qr, kv, kr)
    o_sb, tot = _sb_fwd(qkv_sb)
    merged, h1, f = _attn_out_fwd(o_mla, o_sb, out_mla, out_sb, w_o, x, norm_ffn)
    gate, up, h2 = _ffn_fwd(f, h1, w_gate, w_up, w_down)
    loss, dh2, dg_final = _final_loss(h2, loss_target, norm_final.reshape(1, d))

    dgate, dup, act, dh1, dg_ffn = _ffn_bwd(dh2, gate, up, h1, norm_ffn, w_down.T, w_gate.T, w_up.T)
    dw_down = _tn_matmul(act, dh2, "dw_down")
    dw_gate = _tn_matmul(f, dgate, "dw_gate")
    dw_up = _tn_matmul(f, dup, "dw_up")
    do_mla, do_sb, dg_mla, dg_sb = _attn_out_bwd(dh1, w_o.T, o_mla, o_sb, out_mla, out_sb)
    dw_o = _tn_matmul(merged, dh1, "dw_o")
    dq_sb, dk_sb, dv_sb = _sb_bwd(qkv_sb, do_sb, tot)
    dqn, dqr, dkn, dv, dkr = _mla_bwd(qn, qr, kv, kr, do_mla, o_mla, lse)
    dx, dproj, dq, dkv, dg_q, dg_kv, dg_mix = _proj_in_bwd(
        dqn, dqr, dkn, dv, dkr, dq_sb, dk_sb, dv_sb, cq, ckv, x, dh1, cos, sin_a, sin_b,
        q_norm, kv_norm, norm_mix, w_uq.T, w_ukv.T, w_a.T)
    dw_a = _tn_matmul(u, dproj, "dw_in")
    dw_uq = _tn_matmul(cqn, dq, "dw_uq")
    dw_ukv = _tn_matmul(ckvn, dkv, "dw_ukv")

    p_in = _split_cols(jnp.concatenate([dw_a[:, :416], dw_a[:, 512:]], axis=1))
    p_uq = jnp.concatenate([dw_uq[:, :512].reshape(Q_LORA, MLA_HEADS, MLA_NOPE),
                            dw_uq[:, 512:].reshape(Q_LORA, MLA_HEADS, MLA_ROPE)], axis=2).transpose(1, 0, 2)
    p_ukv = jnp.concatenate([dw_ukv[:, :512].reshape(KV_LORA, MLA_HEADS, MLA_NOPE),
                             dw_ukv[:, 512:].reshape(KV_LORA, MLA_HEADS, HEAD_DIM)], axis=2).transpose(1, 0, 2)
    p_o = dw_o.reshape(N_DEV, -1, d)
    p_down = dw_down.reshape(N_DEV, -1, d)
    parts = [p_in, p_uq, p_ukv, p_o, _split_cols(dw_gate), _split_cols(dw_up), p_down]
    gain_grads = [dg_mix, dg_q, dg_kv, dg_mla, dg_sb, dg_ffn, dg_final]
    return loss, dx, parts, gain_grads


def kernel(x, positions, norm_mix, w_in, q_latent_norm, w_uq, kv_latent_norm, w_ukv, out_norm_mla, out_norm_sb, w_o, norm_ffn, w_gate, w_up, w_down, norm_final, loss_target, m_norm_mix, m_w_in, m_q_latent_norm, m_w_uq, m_kv_latent_norm, m_w_ukv, m_out_norm_mla, m_out_norm_sb, m_w_o, m_norm_ffn, m_w_gate, m_w_up, m_w_down, m_norm_final, v_norm_mix, v_w_in, v_q_latent_norm, v_w_uq, v_kv_latent_norm, v_w_ukv, v_out_norm_mla, v_out_norm_sb, v_w_o, v_norm_ffn, v_w_gate, v_w_up, v_w_down, v_norm_final):
    mats = [w_in, w_uq, w_ukv, w_o, w_gate, w_up, w_down]
    mat_m = [m_w_in, m_w_uq, m_w_ukv, m_w_o, m_w_gate, m_w_up, m_w_down]
    mat_v = [v_w_in, v_w_uq, v_w_ukv, v_w_o, v_w_gate, v_w_up, v_w_down]
    mat_names = ["w_in", "w_uq", "w_ukv", "w_o", "w_gate", "w_up", "w_down"]
    gains = [norm_mix, q_latent_norm, kv_latent_norm, out_norm_mla, out_norm_sb, norm_ffn, norm_final]
    gain_m = [m_norm_mix, m_q_latent_norm, m_kv_latent_norm, m_out_norm_mla, m_out_norm_sb, m_norm_ffn, m_norm_final]
    gain_v = [v_norm_mix, v_q_latent_norm, v_kv_latent_norm, v_out_norm_mla, v_out_norm_sb, v_norm_ffn, v_norm_final]

    gathered = _all_gather([w[0].astype(BF16) for w in mats], "weight_all_gather")

    gains2d = [g.reshape(1, -1) for g in gains]
    loss_part, dx, parts, gain_grads = _local_step(x[0], positions[0], loss_target[0], gains2d, *gathered)

    from_sibling = _sibling_exchange(parts)
    chip_sums = [_chip_sum(p, r, "chip_sum_" + nm) for p, r, nm in zip(parts, from_sibling, mat_names)]
    slots = _chip_exchange(chip_sums)
    mat_out = [_adamw(sl, w[0], m[0], v[0], "adamw_" + nm)
               for sl, w, m, v, nm in zip(slots, mats, mat_m, mat_v, mat_names)]

    sizes = [g.size for g in gains]
    used = sum(sizes) + LANES
    rows = -(-used // (8 * LANES)) * 8

    def pack(vals, tail):
        flat = jnp.concatenate([v.reshape(-1) for v in vals] + [tail])
        return jnp.pad(flat, (0, rows * LANES - flat.size)).reshape(rows, LANES)

    zeros_tail = jnp.zeros((LANES,), F32)
    small = _all_gather([pack(gain_grads, loss_part.reshape(-1))], "gain_all_gather")[0]
    g_s, d_s, m_s, v_s = _adamw(small, pack(gains, zeros_tail), pack(gain_m, zeros_tail), pack(gain_v, zeros_tail), "adamw_gains")

    def unpack(packed):
        flat = packed.reshape(-1)
        outs, off = [], 0
        for g, n in zip(gains, sizes):
            outs.append(flat[off:off + n].reshape(g.shape))
            off += n
        return outs

    loss = g_s.reshape(-1)[sum(sizes)]

    order = ["norm_mix", "w_in", "q_latent_norm", "w_uq", "kv_latent_norm", "w_ukv", "out_norm_mla", "out_norm_sb",
             "w_o", "norm_ffn", "w_gate", "w_up", "w_down", "norm_final"]
    gain_names = ["norm_mix", "q_latent_norm", "kv_latent_norm", "out_norm_mla", "out_norm_sb", "norm_ffn", "norm_final"]
    result = [loss, dx[None]]
    for kind in range(4):
        small_parts = dict(zip(gain_names, unpack([g_s, d_s, m_s, v_s][kind])))
        mat_parts = {nm: out[kind][None] for nm, out in zip(mat_names, mat_out)}
        result += [small_parts[nm] if nm in small_parts else mat_parts[nm] for nm in order]
    return tuple(result)
```

```python
import functools
import math

import jax
import jax.numpy as jnp
from jax import lax
from jax.experimental import pallas as pl
from jax.experimental.pallas import tpu as pltpu

F32 = jnp.float32
BF16 = jnp.bfloat16
MESH = pl.DeviceIdType.MESH

EPS = 1e-6
ROPE_THETA = 10000.0
MLA_HEADS = 8
MLA_NOPE = 64
MLA_ROPE = 32
SB_HEADS = 8
HEAD_DIM = 64
Q_LORA = 256
KV_LORA = 128
MLA_SCALE = 1.0 / math.sqrt(MLA_NOPE + MLA_ROPE)
SB_SCALE = 1.0 / math.sqrt(HEAD_DIM)
LOG2E = math.log2(math.e)
N_DEV = 8

ADAM_LR = 0.001
ADAM_B1 = 0.9
ADAM_B2 = 0.999
ADAM_EPS = 1e-08
ADAM_WD = 0.01
ADAM_STEP = 10

LANES = 128
ATT_TILE = 512
TRI = 256
ROW_TILE = 512
FFN_BWD_ROW_TILE = 256
PROJ_BWD_ROW_TILE = 256
TN_BLOCK = 256
TN_RESIDENT_BYTES = 16 * 1024 * 1024
VMEM_LIMIT = 56 * 1024 * 1024
NEG = -1e30


def _cparams(*sem):
    return pltpu.CompilerParams(dimension_semantics=sem, vmem_limit_bytes=VMEM_LIMIT)


def _dot(a, b):
    return jnp.dot(a, b, preferred_element_type=F32)


def _dot_nt(a, b):
    return lax.dot_general(a, b, (((1,), (1,)), ((), ())), preferred_element_type=F32)


def _dot_tn(a, b):
    return lax.dot_general(a, b, (((0,), (0,)), ((), ())), preferred_element_type=F32)


def _rms(x, g):
    r = lax.rsqrt(jnp.mean(x * x, axis=-1, keepdims=True) + EPS)
    return x * r * g


def _rms_bwd(x, g, dy):
    r = lax.rsqrt(jnp.mean(x * x, axis=-1, keepdims=True) + EPS)
    n = x * r
    dn = dy * g
    dx = r * (dn - n * jnp.mean(dn * n, axis=-1, keepdims=True))
    return dx, jnp.sum(dy * n, axis=0, keepdims=True)


def _rope(x, cos, sin_a, sin_b):
    return x * cos + pltpu.roll(x, 112, 1) * sin_a + pltpu.roll(x, 16, 1) * sin_b


def _rope_t(g, cos, sin_a, sin_b):
    return g * cos + pltpu.roll(g * sin_a, 16, 1) + pltpu.roll(g * sin_b, 112, 1)


def _row_spec(tm, width):
    return pl.BlockSpec((tm, width), lambda r: (r, 0))


def _full_spec(shape):
    return pl.BlockSpec(shape, lambda *_: (0,) * len(shape))


def _accumulate(ref, val, first):
    @pl.when(first)
    def _():
        ref[...] = val

    @pl.when(jnp.logical_not(first))
    def _():
        ref[...] += val


def _proj_in_fwd(x, g_mix, w_a, g_q, w_uq, g_kv, w_ukv, cos, sin_a, sin_b):
    s, d = x.shape
    tm = min(ROW_TILE, s)

    def body(x_ref, gm_ref, wa_ref, gq_ref, wuq_ref, gkv_ref, wukv_ref, cos_ref, sa_ref, sb_ref,
             u_ref, cq_ref, ckv_ref, cqn_ref, ckvn_ref, qn_ref, qr_ref, kv_ref, kr_ref, sbq_ref):
        u = _rms(x_ref[...], gm_ref[...]).astype(BF16)
        u_ref[...] = u
        cq = _dot(u, wa_ref[:, 0:256])
        ckv = _dot(u, wa_ref[:, 256:384])
        kr = _dot(u, wa_ref[:, 384:512])
        cq_ref[...] = cq
        ckv_ref[...] = ckv
        cqn = _rms(cq, gq_ref[...]).astype(BF16)
        ckvn = _rms(ckv, gkv_ref[...]).astype(BF16)
        cqn_ref[...] = cqn
        ckvn_ref[...] = ckvn
        cos_t, sa_t, sb_t = cos_ref[...], sa_ref[...], sb_ref[...]
        qn_ref[...] = (_dot(cqn, wuq_ref[:, 0:512]) * MLA_SCALE).astype(BF16)
        for half in range(2):
            lo = 512 + half * LANES
            qr = _dot(cqn, wuq_ref[:, lo:lo + LANES])
            qr_ref[:, half * LANES:(half + 1) * LANES] = (_rope(qr, cos_t, sa_t, sb_t) * MLA_SCALE).astype(BF16)
        kv_ref[...] = _dot(ckvn, wukv_ref[...]).astype(BF16)
        krt = kr + pltpu.roll(kr, 32, 1) + pltpu.roll(kr, 64, 1) + pltpu.roll(kr, 96, 1)
        kr_ref[...] = _rope(krt, cos_t, sa_t, sb_t).astype(BF16)
        sbq_ref[:, 0:512] = (_dot(u, wa_ref[:, 512:1024]) * (SB_SCALE * LOG2E)).astype(BF16)
        sbq_ref[:, 512:1536] = _dot(u, wa_ref[:, 1024:2048]).astype(BF16)

    outs = [
        jax.ShapeDtypeStruct((s, d), BF16),
        jax.ShapeDtypeStruct((s, 256), F32),
        jax.ShapeDtypeStruct((s, 128), F32),
        jax.ShapeDtypeStruct((s, 256), BF16),
        jax.ShapeDtypeStruct((s, 128), BF16),
        jax.ShapeDtypeStruct((s, 512), BF16),
        jax.ShapeDtypeStruct((s, 256), BF16),
        jax.ShapeDtypeStruct((s, 1024), BF16),
        jax.ShapeDtypeStruct((s, 128), BF16),
        jax.ShapeDtypeStruct((s, 1536), BF16),
    ]
    return pl.pallas_call(
        body, name="proj_in_fwd", grid=(s // tm,), out_shape=outs,
        in_specs=[_row_spec(tm, d), _full_spec(g_mix.shape), _full_spec(w_a.shape), _full_spec(g_q.shape),
                  _full_spec(w_uq.shape), _full_spec(g_kv.shape), _full_spec(w_ukv.shape),
                  _row_spec(tm, LANES), _row_spec(tm, LANES), _row_spec(tm, LANES)],
        out_specs=[_row_spec(tm, o.shape[1]) for o in outs],
        compiler_params=_cparams("arbitrary"),
    )(x, g_mix, w_a, g_q, w_uq, g_kv, w_ukv, cos, sin_a, sin_b)


def _attn_out_fwd(o_mla, o_sb, g_mla, g_sb, w_o, x, g_ffn):
    s, d = x.shape
    tm = min(ROW_TILE, s)

    def body(oa_ref, ob_ref, ga_ref, gb_ref, wo_ref, x_ref, gf_ref, merged_ref, h1_ref, f_ref):
        na = _rms(oa_ref[...], ga_ref[...]).astype(BF16)
        nb = _rms(ob_ref[...], gb_ref[...]).astype(BF16)
        merged_ref[:, 0:512] = na
        merged_ref[:, 512:1024] = nb
        h1 = x_ref[...] + _dot(na, wo_ref[0:512, :]) + _dot(nb, wo_ref[512:1024, :])
        h1_ref[...] = h1
        f_ref[...] = _rms(h1, gf_ref[...]).astype(BF16)

    outs = [jax.ShapeDtypeStruct((s, d), BF16), jax.ShapeDtypeStruct((s, d), F32), jax.ShapeDtypeStruct((s, d), BF16)]
    return pl.pallas_call(
        body, name="attn_out_fwd", grid=(s // tm,), out_shape=outs,
        in_specs=[_row_spec(tm, 512), _row_spec(tm, 512), _full_spec(g_mla.shape), _full_spec(g_sb.shape),
                  _full_spec(w_o.shape), _row_spec(tm, d), _full_spec(g_ffn.shape)],
        out_specs=[_row_spec(tm, d)] * 3,
        compiler_params=_cparams("arbitrary"),
    )(o_mla, o_sb, g_mla, g_sb, w_o, x, g_ffn)


def _ffn_tile(d_ff):
    return d_ff // 2 if (d_ff // 2) % LANES == 0 else d_ff


def _ffn_fwd(f, h1, w_gate, w_up, w_down):
    s, d = h1.shape
    d_ff = w_gate.shape[1]
    tm = min(ROW_TILE, s)
    tf = _ffn_tile(d_ff)

    def body(f_ref, h1_ref, wg_ref, wu_ref, wd_ref, gate_ref, up_ref, h2_ref):
        j = pl.program_id(1)
        fb = f_ref[...]
        gate = _dot(fb, wg_ref[...])
        up = _dot(fb, wu_ref[...])
        gate_ref[...] = gate.astype(BF16)
        up_ref[...] = up.astype(BF16)
        act = (gate * jax.nn.sigmoid(gate) * up).astype(BF16)
        part = _dot(act, wd_ref[...])

        @pl.when(j == 0)
        def _():
            h2_ref[...] = h1_ref[...] + part

        @pl.when(j != 0)
        def _():
            h2_ref[...] += part

    outs = [jax.ShapeDtypeStruct((s, d_ff), BF16), jax.ShapeDtypeStruct((s, d_ff), BF16), jax.ShapeDtypeStruct((s, d), F32)]
    return pl.pallas_call(
        body, name="ffn_fwd", grid=(s // tm, d_ff // tf), out_shape=outs,
        in_specs=[pl.BlockSpec((tm, d), lambda r, j: (r, 0)), pl.BlockSpec((tm, d), lambda r, j: (r, 0)),
                  pl.BlockSpec((d, tf), lambda r, j: (0, j)), pl.BlockSpec((d, tf), lambda r, j: (0, j)),
                  pl.BlockSpec((tf, d), lambda r, j: (j, 0))],
        out_specs=[pl.BlockSpec((tm, tf), lambda r, j: (r, j)), pl.BlockSpec((tm, tf), lambda r, j: (r, j)),
                   pl.BlockSpec((tm, d), lambda r, j: (r, 0))],
        compiler_params=_cparams("arbitrary", "arbitrary"),
    )(f, h1, w_gate, w_up, w_down)


def _final_loss(h2, target, g_final):
    s, d = h2.shape
    tm = min(ROW_TILE, s)

    def body(h2_ref, t_ref, g_ref, loss_ref, dh2_ref, dh2b_ref, dg_ref):
        first = pl.program_id(0) == 0
        h2v = h2_ref[...]
        g = g_ref[...]
        diff = _rms(h2v, g) - t_ref[...]
        part = 0.5 * jnp.sum(jnp.mean(diff * diff, axis=-1, keepdims=True), axis=0, keepdims=True)
        _accumulate(loss_ref, jnp.broadcast_to(part, loss_ref.shape), first)
        dx, dg = _rms_bwd(h2v, g, diff * (1.0 / d))
        dh2_ref[...] = dx
        dh2b_ref[...] = dx.astype(BF16)
        _accumulate(dg_ref, dg, first)

    outs = [jax.ShapeDtypeStruct((1, LANES), F32), jax.ShapeDtypeStruct((s, d), F32), jax.ShapeDtypeStruct((s, d), BF16),
            jax.ShapeDtypeStruct((1, d), F32)]
    return pl.pallas_call(
        body, name="final_loss", grid=(s // tm,), out_shape=outs,
        in_specs=[_row_spec(tm, d), _row_spec(tm, d), _full_spec((1, d))],
        out_specs=[_full_spec((1, LANES)), _row_spec(tm, d), _row_spec(tm, d), _full_spec((1, d))],
        compiler_params=_cparams("arbitrary"),
    )(h2, target, g_final)


def _tile_iotas(t):
    return lax.broadcasted_iota(jnp.int32, (t, t), 0), lax.broadcasted_iota(jnp.int32, (t, t), 1)


def _mla_fwd(qn, qr, kv, kr):
    s = qn.shape[0]
    t = min(ATT_TILE, s)
    pairs = MLA_HEADS // 2

    def body(qn_ref, qr_ref, kn_ref, v_ref, kr_ref, o_ref, lse_ref, qcat_ref, m_ref, l_ref, acc_ref):
        hp, i = pl.program_id(0), pl.program_id(1)
        lane = lax.broadcasted_iota(jnp.int32, (1, LANES), 1)
        row, col = _tile_iotas(t)
        causal = col <= row
        q_pair, q_quad = qn_ref[...], qr_ref[...]
        zero = jnp.zeros_like(q_pair)
        for hh in range(2):
            in_head = (lane // HEAD_DIM) == hh
            in_rope = (lane // MLA_ROPE) == (hp % 2) * 2 + hh
            qcat_ref[hh, :, 0:LANES] = jnp.where(in_head, q_pair, zero)
            qcat_ref[hh, :, LANES:2 * LANES] = jnp.where(in_rope, q_quad, zero)
        m_ref[...] = jnp.full_like(m_ref, NEG)
        l_ref[...] = jnp.zeros_like(l_ref)
        acc_ref[...] = jnp.zeros_like(acc_ref)

        def tile(j, masked):
            rows = pl.ds(pl.multiple_of(j * t, t), t)
            kcat = jnp.concatenate([kn_ref[rows, :], kr_ref[rows, :]], axis=1)
            v_ones = jnp.concatenate([v_ref[rows, :], jnp.ones((t, LANES), BF16)], axis=1)
            for hh in range(2):
                sc = _dot_nt(qcat_ref[hh], kcat)
                if masked:
                    sc = jnp.where(causal, sc, NEG)
                m = m_ref[hh]
                m_new = jnp.maximum(m, jnp.max(sc, axis=-1, keepdims=True))
                alpha = jnp.exp(m - m_new)
                p = jnp.exp(sc - jnp.concatenate([m_new] * (t // LANES), axis=1))
                pv = _dot(p.astype(BF16), v_ones)
                l_ref[hh] = alpha * l_ref[hh] + pv[:, LANES:]
                acc_ref[hh] = alpha * acc_ref[hh] + pv[:, :LANES]
                m_ref[hh] = m_new

        tile(i, True)

        def step(j, carry):
            tile(j, False)
            return carry

        lax.fori_loop(0, i, step, 0)
        first = (lane // HEAD_DIM) == 0
        o_ref[...] = jnp.where(first, acc_ref[0] / l_ref[0], acc_ref[1] / l_ref[1])
        lse_ref[...] = jnp.where(first, m_ref[0] + jnp.log(l_ref[0]), m_ref[1] + jnp.log(l_ref[1]))

    outs = [jax.ShapeDtypeStruct((s, 512), F32), jax.ShapeDtypeStruct((pairs, s, LANES), F32)]
    return pl.pallas_call(
        body, name="mla_fwd", grid=(pairs, s // t), out_shape=outs,
        in_specs=[pl.BlockSpec((t, LANES), lambda hp, i: (i, hp)), pl.BlockSpec((t, LANES), lambda hp, i: (i, hp // 2)),
                  pl.BlockSpec((s, LANES), lambda hp, i: (0, hp)), pl.BlockSpec((s, LANES), lambda hp, i: (0, 4 + hp)),
                  pl.BlockSpec((s, LANES), lambda hp, i: (0, 0))],
        out_specs=[pl.BlockSpec((t, LANES), lambda hp, i: (i, hp)), pl.BlockSpec((None, t, LANES), lambda hp, i: (hp, i, 0))],
        scratch_shapes=[pltpu.VMEM((2, t, 2 * LANES), BF16), pltpu.VMEM((2, t, LANES), F32), pltpu.VMEM((2, t, LANES), F32),
                        pltpu.VMEM((2, t, LANES), F32)],
        compiler_params=_cparams("arbitrary", "arbitrary"),
    )(qn, qr, kv, kv, kr)


def _sb_scores(qm, k, strict, masked):
    z2 = _dot_nt(qm, k)
    log_b = jnp.minimum(z2, 0.0) - jnp.log2(1.0 + jnp.exp2(-jnp.abs(z2)))
    log_1m = log_b - z2
    if masked:
        log_1m = jnp.where(strict, log_1m, 0.0)
    return log_1m, log_b


def _block_totals(x):
    t, w = x.shape
    nb = max(w // TRI, 1)
    bw = w // nb
    blocks = [x[:, b * bw:(b + 1) * bw] for b in range(nb)]
    totals = [jnp.broadcast_to(jnp.sum(blk, axis=-1, keepdims=True), (t, LANES)) for blk in blocks]
    whole = totals[0]
    for tot in totals[1:]:
        whole = whole + tot
    return blocks, totals, whole


def _running_sums(blocks, totals, tri, carry, suffix):
    nb = len(blocks)
    reps = blocks[0].shape[1] // LANES
    outs = [None] * nb
    run = carry
    for b in (range(nb - 1, -1, -1) if suffix else range(nb)):
        outs[b] = _dot(blocks[b].astype(BF16), tri) + jnp.concatenate([run] * reps, axis=1)
        run = run + totals[b]
    return outs[0] if nb == 1 else jnp.concatenate(outs, axis=1)


def _tri(t, rel):
    n = min(TRI, t)
    row, col = _tile_iotas(n)
    return rel(row, col).astype(BF16)


def _sb_fwd(qkv):
    s = qkv.shape[0]
    t = min(ATT_TILE, s)
    pairs = SB_HEADS // 2

    def body(q_ref, k_ref, v_ref, o_ref, tot_ref, qm_ref, right_ref, acc_ref):
        i = pl.program_id(1)
        lane = lax.broadcasted_iota(jnp.int32, (1, LANES), 1)
        row, col = _tile_iotas(t)
        strict = col < row
        t_suffix = _tri(t, lambda r, c: r > c)
        q_pair = q_ref[...]
        for hh in range(2):
            qm_ref[hh] = jnp.where((lane // HEAD_DIM) == hh, q_pair, jnp.zeros_like(q_pair))
        right_ref[...] = jnp.zeros_like(right_ref)
        acc_ref[...] = jnp.zeros_like(acc_ref)

        def tile(j, masked):
            rows = pl.ds(pl.multiple_of(j * t, t), t)
            k, v = k_ref[rows, :], v_ref[rows, :]
            for hh in range(2):
                log_1m, log_b = _sb_scores(qm_ref[hh], k, strict, masked)
                blocks, totals, whole = _block_totals(log_1m)
                right = right_ref[hh]
                a = jnp.exp2(log_b + _running_sums(blocks, totals, t_suffix, right, True))
                right_ref[hh] = right + whole
                if masked:
                    a = jnp.where(strict, a, 0.0)
                acc_ref[hh] += _dot(a.astype(BF16), v)

        tile(i, True)

        def step(n, carry):
            tile(i - 1 - n, False)
            return carry

        lax.fori_loop(0, i, step, 0)
        first = (lane // HEAD_DIM) == 0
        o_ref[...] = jnp.where(first, acc_ref[0], acc_ref[1])
        tot_ref[...] = jnp.where(first, right_ref[0], right_ref[1])

    outs = [jax.ShapeDtypeStruct((s, 512), F32), jax.ShapeDtypeStruct((pairs, s, LANES), F32)]
    return pl.pallas_call(
        body, name="sb_fwd", grid=(pairs, s // t), out_shape=outs,
        in_specs=[pl.BlockSpec((t, LANES), lambda hp, i: (i, hp)), pl.BlockSpec((s, LANES), lambda hp, i: (0, 4 + hp)),
                  pl.BlockSpec((s, LANES), lambda hp, i: (0, 8 + hp))],
        out_specs=[pl.BlockSpec((t, LANES), lambda hp, i: (i, hp)), pl.BlockSpec((None, t, LANES), lambda hp, i: (hp, i, 0))],
        scratch_shapes=[pltpu.VMEM((2, t, LANES), BF16), pltpu.VMEM((2, t, LANES), F32), pltpu.VMEM((2, t, LANES), F32)],
        compiler_params=_cparams("arbitrary", "arbitrary"),
    )(qkv, qkv, qkv)


def _sb_bwd(qkv, do, tot):
    s = qkv.shape[0]
    t = min(ATT_TILE, s)
    pairs = SB_HEADS // 2

    def body(q_ref, k_ref, v_ref, do_ref, tot_ref, dq_ref, dk_ref, dv_ref,
             qm_ref, dob_ref, total_s, left_l, left_g, dq_s):
        i = pl.program_id(1)

        @pl.when(i == 0)
        def _():
            dk_ref[...] = jnp.zeros_like(dk_ref)
            dv_ref[...] = jnp.zeros_like(dv_ref)

        lane = lax.broadcasted_iota(jnp.int32, (1, LANES), 1)
        row, col = _tile_iotas(t)
        strict = col < row
        t_suffix = _tri(t, lambda r, c: r > c)
        t_excl = _tri(t, lambda r, c: r < c)
        q_pair, do_pair, tot_pair = q_ref[...], do_ref[...], tot_ref[...]
        for hh in range(2):
            in_head = (lane // HEAD_DIM) == hh
            qm_ref[hh] = jnp.where(in_head, q_pair, jnp.zeros_like(q_pair))
            dob_ref[hh] = jnp.where(in_head, do_pair, 0.0).astype(BF16)
            total_s[hh] = jnp.broadcast_to(
                jnp.sum(jnp.where(lane == hh * HEAD_DIM, tot_pair, 0.0), axis=-1, keepdims=True), (t, LANES))
        left_l[...] = jnp.zeros_like(left_l)
        left_g[...] = jnp.zeros_like(left_g)
        dq_s[...] = jnp.zeros_like(dq_s)
        reps = t // LANES

        def tile(j, masked):
            rows = pl.ds(pl.multiple_of(j * t, t), t)
            k, v = k_ref[rows, :], v_ref[rows, :]
            for hh in range(2):
                qm, dob = qm_ref[hh], dob_ref[hh]
                log_1m, log_b = _sb_scores(qm, k, strict, masked)
                blocks, totals, whole = _block_totals(log_1m)
                done = left_l[hh] + whole
                left_l[hh] = done
                a = jnp.exp2(log_b + _running_sums(blocks, totals, t_suffix, total_s[hh] - done, True))
                if masked:
                    a = jnp.where(strict, a, 0.0)
                g = a * _dot_nt(dob, v)
                blocks, totals, whole = _block_totals(g)
                before = _running_sums(blocks, totals, t_excl, left_g[hh], False)
                left_g[hh] += whole
                dz = g - jnp.exp2(log_b) * (g + before)
                if masked:
                    dz = jnp.where(strict, dz, 0.0)
                dzb = dz.astype(BF16)
                dq_s[hh] += _dot(dzb, k)
                dk_ref[rows, :] += _dot_tn(dzb, qm)
                dv_ref[rows, :] += _dot_tn(a.astype(BF16), dob)

        def step(j, carry):
            tile(j, False)
            return carry

        lax.fori_loop(0, i, step, 0)
        tile(i, True)
        dq_ref[...] = jnp.where((lane // HEAD_DIM) == 0, dq_s[0], dq_s[1])

        @pl.when(i == s // t - 1)
        def _():
            dk_ref[...] *= 1.0 / LOG2E

    outs = [jax.ShapeDtypeStruct((s, 512), F32)] * 3
    return pl.pallas_call(
        body, name="sb_bwd", grid=(pairs, s // t), out_shape=outs,
        in_specs=[pl.BlockSpec((t, LANES), lambda hp, i: (i, hp)), pl.BlockSpec((s, LANES), lambda hp, i: (0, 4 + hp)),
                  pl.BlockSpec((s, LANES), lambda hp, i: (0, 8 + hp)), pl.BlockSpec((t, LANES), lambda hp, i: (i, hp)),
                  pl.BlockSpec((None, t, LANES), lambda hp, i: (hp, i, 0))],
        out_specs=[pl.BlockSpec((t, LANES), lambda hp, i: (i, hp)), pl.BlockSpec((s, LANES), lambda hp, i: (0, hp)),
                   pl.BlockSpec((s, LANES), lambda hp, i: (0, hp))],
        scratch_shapes=[pltpu.VMEM((2, t, LANES), BF16), pltpu.VMEM((2, t, LANES), BF16)]
        + [pltpu.VMEM((2, t, LANES), F32)] * 4,
        compiler_params=_cparams("arbitrary", "arbitrary"),
    )(qkv, qkv, qkv, do, tot)


def _mla_bwd(qn, qr, kv, kr, do, o, lse):
    s = qn.shape[0]
    t = min(ATT_TILE, s)
    pairs = MLA_HEADS // 2

    def body(qn_ref, qr_ref, kn_ref, v_ref, kr_ref, do_ref, o_ref, lse_ref, dqn_ref, dqr_ref, dkn_ref, dv_ref, dkr_ref,
             qcat_ref, dob_ref, lse_s, delta_s, dq_s):
        hp, i = pl.program_id(0), pl.program_id(1)

        @pl.when(i == 0)
        def _():
            dkn_ref[...] = jnp.zeros_like(dkn_ref)
            dv_ref[...] = jnp.zeros_like(dv_ref)
            dkr_ref[...] = jnp.zeros_like(dkr_ref)

        lane = lax.broadcasted_iota(jnp.int32, (1, LANES), 1)
        row, col = _tile_iotas(t)
        causal = col <= row
        q_pair, q_quad, do_pair, lse_pair = qn_ref[...], qr_ref[...], do_ref[...], lse_ref[...]
        do_o = do_pair * o_ref[...]
        zero = jnp.zeros_like(q_pair)
        ropes = []
        for hh in range(2):
            in_head = (lane // HEAD_DIM) == hh
            in_rope = (lane // MLA_ROPE) == (hp % 2) * 2 + hh
            ropes.append(in_rope)
            qcat_ref[hh, :, 0:LANES] = jnp.where(in_head, q_pair, zero)
            qcat_ref[hh, :, LANES:2 * LANES] = jnp.where(in_rope, q_quad, zero)
            dob_ref[hh] = jnp.where(in_head, do_pair, 0.0).astype(BF16)
            delta_s[hh] = jnp.broadcast_to(jnp.sum(jnp.where(in_head, do_o, 0.0), axis=-1, keepdims=True), (t, LANES))
            lse_s[hh] = jnp.broadcast_to(
                jnp.sum(jnp.where(lane == hh * HEAD_DIM, lse_pair, 0.0), axis=-1, keepdims=True), (t, LANES))
        dq_s[...] = jnp.zeros_like(dq_s)
        reps = t // LANES

        def tile(j, masked):
            rows = pl.ds(pl.multiple_of(j * t, t), t)
            kcat = jnp.concatenate([kn_ref[rows, :], kr_ref[rows, :]], axis=1)
            v = v_ref[rows, :]
            for hh in range(2):
                qcat, dob = qcat_ref[hh], dob_ref[hh]
                p = jnp.exp(_dot_nt(qcat, kcat) - jnp.concatenate([lse_s[hh]] * reps, axis=1))
                if masked:
                    p = jnp.where(causal, p, 0.0)
                ds = (p * (_dot_nt(dob, v) - jnp.concatenate([delta_s[hh]] * reps, axis=1))).astype(BF16)
                dq_s[hh] += _dot(ds, kcat)
                dkcat = _dot_tn(ds, qcat)
                dkn_ref[rows, :] += dkcat[:, 0:LANES]
                dkr_ref[rows, :] += dkcat[:, LANES:2 * LANES]
                dv_ref[rows, :] += _dot_tn(p.astype(BF16), dob)

        def step(j, carry):
            tile(j, False)
            return carry

        lax.fori_loop(0, i, step, 0)
        tile(i, True)
        dqn_ref[...] = jnp.where((lane // HEAD_DIM) == 0, dq_s[0, :, 0:LANES], dq_s[1, :, 0:LANES])
        dqr_ref[...] = (jnp.where(ropes[0], dq_s[0, :, LANES:2 * LANES], 0.0)
                        + jnp.where(ropes[1], dq_s[1, :, LANES:2 * LANES], 0.0))

    pair_block = pl.BlockSpec((t, LANES), lambda hp, i: (i, hp))
    outs = [jax.ShapeDtypeStruct((s, 512), F32), jax.ShapeDtypeStruct((pairs, s, LANES), F32),
            jax.ShapeDtypeStruct((s, 512), F32), jax.ShapeDtypeStruct((s, 512), F32), jax.ShapeDtypeStruct((pairs, s, LANES), F32)]
    return pl.pallas_call(
        body, name="mla_bwd", grid=(pairs, s // t), out_shape=outs,
        in_specs=[pair_block, pl.BlockSpec((t, LANES), lambda hp, i: (i, hp // 2)),
                  pl.BlockSpec((s, LANES), lambda hp, i: (0, hp)), pl.BlockSpec((s, LANES), lambda hp, i: (0, 4 + hp)),
                  pl.BlockSpec((s, LANES), lambda hp, i: (0, 0)), pair_block, pair_block,
                  pl.BlockSpec((None, t, LANES), lambda hp, i: (hp, i, 0))],
        out_specs=[pair_block, pl.BlockSpec((None, t, LANES), lambda hp, i: (hp, i, 0)),
                   pl.BlockSpec((s, LANES), lambda hp, i: (0, hp)), pl.BlockSpec((s, LANES), lambda hp, i: (0, hp)),
                   pl.BlockSpec((None, s, LANES), lambda hp, i: (hp, 0, 0))],
        scratch_shapes=[pltpu.VMEM((2, t, 2 * LANES), BF16), pltpu.VMEM((2, t, LANES), BF16), pltpu.VMEM((2, t, LANES), F32),
                        pltpu.VMEM((2, t, LANES), F32), pltpu.VMEM((2, t, 2 * LANES), F32)],
        compiler_params=_cparams("arbitrary", "arbitrary"),
    )(qn, qr, kv, kv, kr, do, o, lse)


def _ffn_bwd(dh2, dh2b, gate, up, h1, g_ffn, w_down_t, w_gate_t, w_up_t):
    s, d = h1.shape
    d_ff = gate.shape[1]
    tm = min(FFN_BWD_ROW_TILE, s)
    tf = _ffn_tile(d_ff)
    nf = d_ff // tf

    def body(dh2_ref, dh2b_ref, gate_ref, up_ref, h1_ref, g_ref, wdt_ref, wgt_ref, wut_ref,
             dgate_ref, dup_ref, act_ref, dh1_ref, dh1b_ref, dg_ref, df_ref):
        r, j = pl.program_id(0), pl.program_id(1)
        dact = _dot(dh2b_ref[...], wdt_ref[...])
        gate_v = gate_ref[...].astype(F32)
        up_v = up_ref[...].astype(F32)
        sig = jax.nn.sigmoid(gate_v)
        silu = gate_v * sig
        dup = (dact * silu).astype(BF16)
        dgate = (dact * up_v * (sig * (1.0 + gate_v * (1.0 - sig)))).astype(BF16)
        dgate_ref[...] = dgate
        dup_ref[...] = dup
        act_ref[...] = (silu * up_v).astype(BF16)
        part = _dot(dgate, wgt_ref[...]) + _dot(dup, wut_ref[...])
        _accumulate(df_ref, part, j == 0)

        @pl.when(j == nf - 1)
        def _():
            dx, dg = _rms_bwd(h1_ref[...], g_ref[...], df_ref[...])
            dh1 = dh2_ref[...] + dx
            dh1_ref[...] = dh1
            dh1b_ref[...] = dh1.astype(BF16)
            _accumulate(dg_ref, dg, r == 0)

    outs = [jax.ShapeDtypeStruct((s, d_ff), BF16)] * 3 + [jax.ShapeDtypeStruct((s, d), F32), jax.ShapeDtypeStruct((s, d), BF16),
                                                           jax.ShapeDtypeStruct((1, d), F32)]
    rows = pl.BlockSpec((tm, d), lambda r, j: (r, 0))
    ff = pl.BlockSpec((tm, tf), lambda r, j: (r, j))
    return pl.pallas_call(
        body, name="ffn_bwd", grid=(s // tm, nf), out_shape=outs,
        in_specs=[rows, rows, ff, ff, rows, pl.BlockSpec((1, d), lambda r, j: (0, 0)),
                  pl.BlockSpec((d, tf), lambda r, j: (0, j)), pl.BlockSpec((tf, d), lambda r, j: (j, 0)),
                  pl.BlockSpec((tf, d), lambda r, j: (j, 0))],
        out_specs=[ff, ff, ff, rows, rows, pl.BlockSpec((1, d), lambda r, j: (0, 0))],
        scratch_shapes=[pltpu.VMEM((tm, d), F32)],
        compiler_params=_cparams("arbitrary", "arbitrary"),
    )(dh2, dh2b, gate, up, h1, g_ffn, w_down_t, w_gate_t, w_up_t)


def _largest_tile(n, cap):
    for cand in range(cap, 0, -LANES):
        if n % cand == 0:
            return cand
    return n


def _tn_matmul(a, b, name):
    assert a.dtype == BF16 and b.dtype == BF16
    s, m = a.shape
    n = b.shape[1]
    if s * m * 2 <= TN_RESIDENT_BYTES:
        tm, tn = m, min(n, TN_BLOCK)
    else:
        tm, tn = TN_BLOCK, n

    def body(a_ref, b_ref, o_ref):
        o_ref[...] = _dot_tn(a_ref[...], b_ref[...])

    return pl.pallas_call(
        body, name=name, grid=(m // tm, n // tn), out_shape=jax.ShapeDtypeStruct((m, n), F32),
        in_specs=[pl.BlockSpec((s, tm), lambda i, j: (0, i)), pl.BlockSpec((s, tn), lambda i, j: (0, j))],
        out_specs=pl.BlockSpec((tm, tn), lambda i, j: (i, j)),
        compiler_params=_cparams("arbitrary", "arbitrary"),
    )(a, b)


def _attn_out_bwd(dh1, w_o_t, o_mla, o_sb, g_mla, g_sb):
    s, d = dh1.shape
    tm = min(ROW_TILE, s)

    def body(dh1_ref, wot_ref, oa_ref, ob_ref, ga_ref, gb_ref, doa_ref, dob_ref, dga_ref, dgb_ref):
        first = pl.program_id(0) == 0
        dh1b = dh1_ref[...]
        dxa, dga = _rms_bwd(oa_ref[...], ga_ref[...], _dot(dh1b, wot_ref[:, 0:512]))
        dxb, dgb = _rms_bwd(ob_ref[...], gb_ref[...], _dot(dh1b, wot_ref[:, 512:1024]))
        doa_ref[...] = dxa
        dob_ref[...] = dxb
        _accumulate(dga_ref, dga, first)
        _accumulate(dgb_ref, dgb, first)

    outs = [jax.ShapeDtypeStruct((s, 512), F32)] * 2 + [jax.ShapeDtypeStruct((1, 512), F32)] * 2
    return pl.pallas_call(
        body, name="attn_out_bwd", grid=(s // tm,), out_shape=outs,
        in_specs=[_row_spec(tm, d), _full_spec(w_o_t.shape), _row_spec(tm, 512), _row_spec(tm, 512),
                  _full_spec((1, 512)), _full_spec((1, 512))],
        out_specs=[_row_spec(tm, 512), _row_spec(tm, 512), _full_spec((1, 512)), _full_spec((1, 512))],
        compiler_params=_cparams("arbitrary"),
    )(dh1, w_o_t, o_mla, o_sb, g_mla, g_sb)


def _proj_in_bwd(dqn, dqr, dkn, dv, dkr, dq_sb, dk_sb, dv_sb, cq, ckv, x, dh1, cos, sin_a, sin_b,
                 g_q, g_kv, g_mix, w_uq_t, w_ukv_t, w_a_t):
    s, d = x.shape
    tm = min(PROJ_BWD_ROW_TILE, s)

    def body(dqn_ref, dqr_ref, dkn_ref, dv_ref, dkr_ref, dqs_ref, dks_ref, dvs_ref, cq_ref, ckv_ref, x_ref, dh1_ref,
             cos_ref, sa_ref, sb_ref, gq_ref, gkv_ref, gm_ref, wuqt_ref, wukvt_ref, wat_ref,
             dx_ref, dproj_ref, dq_ref, dkv_ref, dgq_ref, dgkv_ref, dgm_ref):
        first = pl.program_id(0) == 0
        lane = lax.broadcasted_iota(jnp.int32, (1, LANES), 1)
        cos_t, sa_t, sb_t = cos_ref[...], sa_ref[...], sb_ref[...]
        dq_ref[:, 0:512] = (dqn_ref[...] * MLA_SCALE).astype(BF16)
        for half in range(2):
            quad = (dqr_ref[2 * half] + dqr_ref[2 * half + 1]) * MLA_SCALE
            dq_ref[:, 512 + half * LANES:512 + (half + 1) * LANES] = _rope_t(quad, cos_t, sa_t, sb_t).astype(BF16)
        dcq, dgq = _rms_bwd(cq_ref[...], gq_ref[...], _dot(dq_ref[...], wuqt_ref[...]))
        _accumulate(dgq_ref, dgq, first)
        dkv_ref[:, 0:512] = dkn_ref[...].astype(BF16)
        dkv_ref[:, 512:1024] = dv_ref[...].astype(BF16)
        dckv, dgkv = _rms_bwd(ckv_ref[...], gkv_ref[...], _dot(dkv_ref[...], wukvt_ref[...]))
        _accumulate(dgkv_ref, dgkv, first)
        g = _rope_t(dkr_ref[0] + dkr_ref[1] + dkr_ref[2] + dkr_ref[3], cos_t, sa_t, sb_t)
        g = g + pltpu.roll(g, 96, 1) + pltpu.roll(g, 64, 1) + pltpu.roll(g, 32, 1)
        dproj_ref[:, 0:256] = dcq.astype(BF16)
        dproj_ref[:, 256:384] = dckv.astype(BF16)
        dproj_ref[:, 384:512] = jnp.where(lane < MLA_ROPE, g, 0.0).astype(BF16)
        dproj_ref[:, 512:1024] = (dqs_ref[...] * SB_SCALE).astype(BF16)
        dproj_ref[:, 1024:1536] = dks_ref[...].astype(BF16)
        dproj_ref[:, 1536:2048] = dvs_ref[...].astype(BF16)
        dxn, dgm = _rms_bwd(x_ref[...], gm_ref[...], _dot(dproj_ref[...], wat_ref[...]))
        dx_ref[...] = dh1_ref[...] + dxn
        _accumulate(dgm_ref, dgm, first)

    quad_spec = pl.BlockSpec((4, tm, LANES), lambda r: (0, r, 0))
    outs = [jax.ShapeDtypeStruct((s, d), F32), jax.ShapeDtypeStruct((s, 2048), BF16), jax.ShapeDtypeStruct((s, 768), BF16),
            jax.ShapeDtypeStruct((s, 1024), BF16), jax.ShapeDtypeStruct((1, 256), F32), jax.ShapeDtypeStruct((1, 128), F32),
            jax.ShapeDtypeStruct((1, d), F32)]
    return pl.pallas_call(
        body, name="proj_in_bwd", grid=(s // tm,), out_shape=outs,
        in_specs=[_row_spec(tm, 512), quad_spec, _row_spec(tm, 512), _row_spec(tm, 512), quad_spec,
                  _row_spec(tm, 512), _row_spec(tm, 512), _row_spec(tm, 512), _row_spec(tm, 256), _row_spec(tm, 128),
                  _row_spec(tm, d), _row_spec(tm, d), _row_spec(tm, LANES), _row_spec(tm, LANES), _row_spec(tm, LANES),
                  _full_spec((1, 256)), _full_spec((1, 128)), _full_spec((1, d)),
                  _full_spec(w_uq_t.shape), _full_spec(w_ukv_t.shape), _full_spec(w_a_t.shape)],
        out_specs=[_row_spec(tm, d), _row_spec(tm, 2048), _row_spec(tm, 768), _row_spec(tm, 1024),
                   _full_spec((1, 256)), _full_spec((1, 128)), _full_spec((1, d))],
        compiler_params=_cparams("arbitrary"),
    )(dqn, dqr, dkn, dv, dkr, dq_sb, dk_sb, dv_sb, cq, ckv, x, dh1, cos, sin_a, sin_b, g_q, g_kv, g_mix,
      w_uq_t, w_ukv_t, w_a_t)


ANY = pl.BlockSpec(memory_space=pl.ANY)


def _place():
    return lax.axis_index("x"), lax.axis_index("y"), lax.axis_index("c")


def _all_gather(shards, name):
    n = len(shards)

    def body(*refs):
        ins, outs = refs[:n], refs[n:2 * n]
        send_sems, recv_sems, local_sems = refs[2 * n:]
        x, y, c = _place()
        me, sibling = (x, y, c), (x, y, 1 - c)
        chips = [(1 - x, y), (x, 1 - y), (1 - x, 1 - y)]

        def slot(a, px, py, pc):
            return outs[a].at[4 * px + 2 * py + pc]

        def copy(a, k, block, to, src=None):
            return pltpu.make_async_remote_copy(
                src_ref=slot(a, *block) if src is None else src, dst_ref=slot(a, *block),
                send_sem=send_sems.at[a, k], recv_sem=recv_sems.at[a, k], device_id=to, device_id_type=MESH)

        mine, first, passed = [], [], []
        for a in range(n):
            own = pltpu.make_async_copy(ins[a], slot(a, *me), local_sems.at[a])
            own.start()
            mine.append(own)
            cps = [copy(a, 0, me, sibling, src=ins[a])]
            cps += [copy(a, 1 + j, me, (*chip, c), src=ins[a]) for j, chip in enumerate(chips)]
            for cp in cps:
                cp.start()
            first += cps
        for a in range(n):
            for j, chip in enumerate(chips):
                copy(a, 1 + j, (*chip, c), me).wait_recv()
                fwd = copy(a, 4 + j, (*chip, c), sibling)
                fwd.start()
                passed.append(fwd)
        for a in range(n):
            copy(a, 0, sibling, me).wait_recv()
            for j, chip in enumerate(chips):
                copy(a, 4 + j, (*chip, 1 - c), me).wait_recv()
        for cp in first + passed:
            cp.wait_send()
        for own in mine:
            own.wait()

    return pl.pallas_call(
        body, name=name,
        out_shape=[jax.ShapeDtypeStruct((N_DEV,) + v.shape, v.dtype) for v in shards],
        in_specs=[ANY] * n, out_specs=[ANY] * n,
        scratch_shapes=[pltpu.SemaphoreType.DMA((n, 7)), pltpu.SemaphoreType.DMA((n, 7)), pltpu.SemaphoreType.DMA((n,))],
    )(*shards)


def _sibling_exchange(parts):
    n = len(parts)

    def body(*refs):
        ins, outs = refs[:n], refs[n:2 * n]
        send_sems, recv_sems = refs[2 * n:]
        x, y, c = _place()
        copies = []
        for a in range(n):
            for q in range(4):
                cp = pltpu.make_async_remote_copy(
                    src_ref=ins[a].at[2 * q + 1 - c], dst_ref=outs[a].at[q],
                    send_sem=send_sems.at[a, q], recv_sem=recv_sems.at[a, q], device_id=(x, y, 1 - c), device_id_type=MESH)
                cp.start()
                copies.append(cp)
        for cp in copies:
            cp.wait()

    return pl.pallas_call(
        body, name="grad_sibling_exchange",
        out_shape=[jax.ShapeDtypeStruct((4,) + v.shape[1:], v.dtype) for v in parts],
        in_specs=[ANY] * n, out_specs=[ANY] * n,
        scratch_shapes=[pltpu.SemaphoreType.DMA((n, 4)), pltpu.SemaphoreType.DMA((n, 4))],
    )(*parts)


def _grad_row_tile(rows):
    return _largest_tile_rows(rows, 256)


def _largest_tile_rows(rows, cap):
    for cand in range(cap, 0, -8):
        if rows % cand == 0:
            return cand
    return rows


def _chip_sum(part, recv, name):
    _, r, cdim = part.shape
    part4 = part.reshape(4, 2, r, cdim)
    tr = _grad_row_tile(r)

    def body(p_ref, s_ref, o_ref):
        c = lax.axis_index("c")
        own = jnp.where(c == 1, p_ref[1], p_ref[0])
        o_ref[...] = own + s_ref[...]

    return pl.pallas_call(
        body, name=name, grid=(4, r // tr), out_shape=jax.ShapeDtypeStruct((4, r, cdim), F32),
        in_specs=[pl.BlockSpec((None, 2, tr, cdim), lambda q, i: (q, 0, i, 0)),
                  pl.BlockSpec((None, tr, cdim), lambda q, i: (q, i, 0))],
        out_specs=pl.BlockSpec((None, tr, cdim), lambda q, i: (q, i, 0)),
        compiler_params=_cparams("arbitrary", "arbitrary"),
    )(part4, recv)


def _chip_exchange(sums):
    n = len(sums)

    def body(*refs):
        ins, outs = refs[:n], refs[n:2 * n]
        send_sems, recv_sems, local_sems = refs[2 * n:]
        x, y, c = _place()
        my_chip = 2 * x + y
        peers = [(1 - x, y), (x, 1 - y), (1 - x, 1 - y)]
        copies, local = [], []
        for a in range(n):
            own = pltpu.make_async_copy(ins[a].at[my_chip], outs[a].at[my_chip], local_sems.at[a])
            own.start()
            local.append(own)
            for k, (tx, ty) in enumerate(peers):
                cp = pltpu.make_async_remote_copy(
                    src_ref=ins[a].at[2 * tx + ty], dst_ref=outs[a].at[my_chip],
                    send_sem=send_sems.at[a, k], recv_sem=recv_sems.at[a, k], device_id=(tx, ty, c), device_id_type=MESH)
                cp.start()
                copies.append(cp)
        for cp in copies:
            cp.wait()
        for own in local:
            own.wait()

    return pl.pallas_call(
        body, name="grad_chip_exchange",
        out_shape=[jax.ShapeDtypeStruct(v.shape, v.dtype) for v in sums],
        in_specs=[ANY] * n, out_specs=[ANY] * n,
        scratch_shapes=[pltpu.SemaphoreType.DMA((n, 3)), pltpu.SemaphoreType.DMA((n, 3)), pltpu.SemaphoreType.DMA((n,))],
    )(*sums)


def _adamw_math(w, g, m, v):
    m_new = ADAM_B1 * m + (1.0 - ADAM_B1) * g
    v_new = ADAM_B2 * v + (1.0 - ADAM_B2) * (g * g)
    m_hat = m_new / (1.0 - ADAM_B1 ** ADAM_STEP)
    v_hat = v_new / (1.0 - ADAM_B2 ** ADAM_STEP)
    delta = -ADAM_LR * (m_hat / (jnp.sqrt(v_hat) + ADAM_EPS) + ADAM_WD * w)
    return delta, m_new, v_new


def _adamw(slots, w, m, v, name):
    k, r, cdim = slots.shape
    tr = _grad_row_tile(r)

    def body(s_ref, w_ref, m_ref, v_ref, g_ref, d_ref, mo_ref, vo_ref):
        g = s_ref[0]
        for q in range(1, k):
            g = g + s_ref[q]
        g_ref[...] = g
        d_ref[...], mo_ref[...], vo_ref[...] = _adamw_math(w_ref[...], g, m_ref[...], v_ref[...])

    blk = pl.BlockSpec((tr, cdim), lambda i: (i, 0))
    return pl.pallas_call(
        body, name=name, grid=(r // tr,), out_shape=[jax.ShapeDtypeStruct((r, cdim), F32)] * 4,
        in_specs=[pl.BlockSpec((k, tr, cdim), lambda i: (0, i, 0)), blk, blk, blk], out_specs=[blk] * 4,
        compiler_params=_cparams("arbitrary"),
    )(slots, w, m, v)


def _stack_cols(g):
    n, r, c = g.shape
    return g.transpose(1, 0, 2).reshape(r, n * c)


def _split_cols(w):
    r, nc = w.shape
    return w.reshape(r, N_DEV, nc // N_DEV).transpose(1, 0, 2)


def _rope_tables(positions):
    inv_freq = ROPE_THETA ** (-jnp.arange(0, MLA_ROPE, 2, dtype=F32) / MLA_ROPE)
    ang = positions.astype(F32).reshape(-1, 1) * inv_freq[None, :]
    cos, sin, zero = jnp.cos(ang), jnp.sin(ang), jnp.zeros_like(ang)
    reps = LANES // MLA_ROPE
    return (jnp.tile(jnp.concatenate([cos, cos], axis=1), (1, reps)),
            jnp.tile(jnp.concatenate([-sin, zero], axis=1), (1, reps)),
            jnp.tile(jnp.concatenate([zero, sin], axis=1), (1, reps)))


def _local_step(x, positions, loss_target, gains, g_in, g_uq, g_ukv, g_o, g_gate, g_up, g_down):
    norm_mix, q_norm, kv_norm, out_mla, out_sb, norm_ffn, norm_final = gains
    d = x.shape[1]
    w_in = _stack_cols(g_in)
    w_a = jnp.concatenate([w_in[:, :416], jnp.zeros((d, 96), BF16), w_in[:, 416:]], axis=1)
    w_uq = jnp.concatenate([g_uq[:, :, :MLA_NOPE].transpose(1, 0, 2).reshape(Q_LORA, -1),
                            g_uq[:, :, MLA_NOPE:].transpose(1, 0, 2).reshape(Q_LORA, -1)], axis=1)
    w_ukv = jnp.concatenate([g_ukv[:, :, :MLA_NOPE].transpose(1, 0, 2).reshape(KV_LORA, -1),
                             g_ukv[:, :, MLA_NOPE:].transpose(1, 0, 2).reshape(KV_LORA, -1)], axis=1)
    w_o = g_o.reshape(-1, d)
    w_gate, w_up = _stack_cols(g_gate), _stack_cols(g_up)
    w_down = g_down.reshape(-1, d)
    cos, sin_a, sin_b = _rope_tables(positions)

    u, cq, ckv, cqn, ckvn, qn, qr, kv, kr, qkv_sb = _proj_in_fwd(x, norm_mix, w_a, q_norm, w_uq, kv_norm, w_ukv, cos, sin_a, sin_b)
    o_mla, lse = _mla_fwd(qn, qr, kv, kr)
    o_sb, tot = _sb_fwd(qkv_sb)
    merged, h1, f = _attn_out_fwd(o_mla, o_sb, out_mla, out_sb, w_o, x, norm_ffn)
    gate, up, h2 = _ffn_fwd(f, h1, w_gate, w_up, w_down)
    loss, dh2, dh2b, dg_final = _final_loss(h2, loss_target, norm_final.reshape(1, d))

    dgate, dup, act, dh1, dh1b, dg_ffn = _ffn_bwd(dh2, dh2b, gate, up, h1, norm_ffn, w_down.T, w_gate.T, w_up.T)
    dw_down = _tn_matmul(act, dh2b, "dw_down")
    dw_gate = _tn_matmul(f, dgate, "dw_gate")
    dw_up = _tn_matmul(f, dup, "dw_up")
    do_mla, do_sb, dg_mla, dg_sb = _attn_out_bwd(dh1b, w_o.T, o_mla, o_sb, out_mla, out_sb)
    dw_o = _tn_matmul(merged, dh1b, "dw_o")
    dq_sb, dk_sb, dv_sb = _sb_bwd(qkv_sb, do_sb, tot)
    dqn, dqr, dkn, dv, dkr = _mla_bwd(qn, qr, kv, kr, do_mla, o_mla, lse)
    dx, dproj, dq, dkv, dg_q, dg_kv, dg_mix = _proj_in_bwd(
        dqn, dqr, dkn, dv, dkr, dq_sb, dk_sb, dv_sb, cq, ckv, x, dh1, cos, sin_a, sin_b,
        q_norm, kv_norm, norm_mix, w_uq.T, w_ukv.T, w_a.T)
    dw_a = _tn_matmul(u, dproj, "dw_in")
    dw_uq = _tn_matmul(cqn, dq, "dw_uq")
    dw_ukv = _tn_matmul(ckvn, dkv, "dw_ukv")

    p_in = _split_cols(jnp.concatenate([dw_a[:, :416], dw_a[:, 512:]], axis=1))
    p_uq = jnp.concatenate([dw_uq[:, :512].reshape(Q_LORA, MLA_HEADS, MLA_NOPE),
                            dw_uq[:, 512:].reshape(Q_LORA, MLA_HEADS, MLA_ROPE)], axis=2).transpose(1, 0, 2)
    p_ukv = jnp.concatenate([dw_ukv[:, :512].reshape(KV_LORA, MLA_HEADS, MLA_NOPE),
                             dw_ukv[:, 512:].reshape(KV_LORA, MLA_HEADS, HEAD_DIM)], axis=2).transpose(1, 0, 2)
    p_o = dw_o.reshape(N_DEV, -1, d)
    p_down = dw_down.reshape(N_DEV, -1, d)
    parts = [p_in, p_uq, p_ukv, p_o, _split_cols(dw_gate), _split_cols(dw_up), p_down]
    gain_grads = [dg_mix, dg_q, dg_kv, dg_mla, dg_sb, dg_ffn, dg_final]
    return loss, dx, parts, gain_grads


def kernel(x, positions, norm_mix, w_in, q_latent_norm, w_uq, kv_latent_norm, w_ukv, out_norm_mla, out_norm_sb, w_o, norm_ffn, w_gate, w_up, w_down, norm_final, loss_target, m_norm_mix, m_w_in, m_q_latent_norm, m_w_uq, m_kv_latent_norm, m_w_ukv, m_out_norm_mla, m_out_norm_sb, m_w_o, m_norm_ffn, m_w_gate, m_w_up, m_w_down, m_norm_final, v_norm_mix, v_w_in, v_q_latent_norm, v_w_uq, v_kv_latent_norm, v_w_ukv, v_out_norm_mla, v_out_norm_sb, v_w_o, v_norm_ffn, v_w_gate, v_w_up, v_w_down, v_norm_final):
    mats = [w_in, w_uq, w_ukv, w_o, w_gate, w_up, w_down]
    mat_m = [m_w_in, m_w_uq, m_w_ukv, m_w_o, m_w_gate, m_w_up, m_w_down]
    mat_v = [v_w_in, v_w_uq, v_w_ukv, v_w_o, v_w_gate, v_w_up, v_w_down]
    mat_names = ["w_in", "w_uq", "w_ukv", "w_o", "w_gate", "w_up", "w_down"]
    gains = [norm_mix, q_latent_norm, kv_latent_norm, out_norm_mla, out_norm_sb, norm_ffn, norm_final]
    gain_m = [m_norm_mix, m_q_latent_norm, m_kv_latent_norm, m_out_norm_mla, m_out_norm_sb, m_norm_ffn, m_norm_final]
    gain_v = [v_norm_mix, v_q_latent_norm, v_kv_latent_norm, v_out_norm_mla, v_out_norm_sb, v_norm_ffn, v_norm_final]

    gathered = _all_gather([w[0].astype(BF16) for w in mats], "weight_all_gather")

    gains2d = [g.reshape(1, -1) for g in gains]
    loss_part, dx, parts, gain_grads = _local_step(x[0], positions[0], loss_target[0], gains2d, *gathered)

    from_sibling = _sibling_exchange(parts)
    chip_sums = [_chip_sum(p, r, "chip_sum_" + nm) for p, r, nm in zip(parts, from_sibling, mat_names)]
    slots = _chip_exchange(chip_sums)
    mat_out = [_adamw(sl, w[0], m[0], v[0], "adamw_" + nm)
               for sl, w, m, v, nm in zip(slots, mats, mat_m, mat_v, mat_names)]

    sizes = [g.size for g in gains]
    used = sum(sizes) + LANES
    rows = -(-used // (8 * LANES)) * 8

    def pack(vals, tail):
        flat = jnp.concatenate([v.reshape(-1) for v in vals] + [tail])
        return jnp.pad(flat, (0, rows * LANES - flat.size)).reshape(rows, LANES)

    zeros_tail = jnp.zeros((LANES,), F32)
    small = _all_gather([pack(gain_grads, loss_part.reshape(-1))], "gain_all_gather")[0]
    g_s, d_s, m_s, v_s = _adamw(small, pack(gains, zeros_tail), pack(gain_m, zeros_tail), pack(gain_v, zeros_tail), "adamw_gains")

    def unpack(packed):
        flat = packed.reshape(-1)
        outs, off = [], 0
        for g, n in zip(gains, sizes):
            outs.append(flat[off:off + n].reshape(g.shape))
            off += n
        return outs

    loss = g_s.reshape(-1)[sum(sizes)]

    order = ["norm_mix", "w_in", "q_latent_norm", "w_uq", "kv_latent_norm", "w_ukv", "out_norm_mla", "out_norm_sb",
             "w_o", "norm_ffn", "w_gate", "w_up", "w_down", "norm_final"]
    gain_names = ["norm_mix", "q_latent_norm", "kv_latent_norm", "out_norm_mla", "out_norm_sb", "norm_ffn", "norm_final"]
    result = [loss, dx[None]]
    for kind in range(4):
        small_parts = dict(zip(gain_names, unpack([g_s, d_s, m_s, v_s][kind])))
        mat_parts = {nm: out[kind][None] for nm, out in zip(mat_names, mat_out)}
        result += [small_parts[nm] if nm in small_parts else mat_parts[nm] for nm in order]
    return tuple(result)
```

```python
import functools
import math

import jax
import jax.numpy as jnp
from jax import lax
from jax.experimental import pallas as pl
from jax.experimental.pallas import tpu as pltpu

F32 = jnp.float32
BF16 = jnp.bfloat16
MESH = pl.DeviceIdType.MESH

EPS = 1e-6
ROPE_THETA = 10000.0
MLA_HEADS = 8
MLA_NOPE = 64
MLA_ROPE = 32
SB_HEADS = 8
HEAD_DIM = 64
Q_LORA = 256
KV_LORA = 128
MLA_SCALE = 1.0 / math.sqrt(MLA_NOPE + MLA_ROPE)
SB_SCALE = 1.0 / math.sqrt(HEAD_DIM)
LOG2E = math.log2(math.e)
N_DEV = 8

ADAM_LR = 0.001
ADAM_B1 = 0.9
ADAM_B2 = 0.999
ADAM_EPS = 1e-08
ADAM_WD = 0.01
ADAM_STEP = 10

LANES = 128
ATT_TILE = 512
TRI = 256
ROW_TILE = 512
FFN_BWD_ROW_TILE = 256
PROJ_BWD_ROW_TILE = 256
TN_BLOCK = 256
TN_RESIDENT_BYTES = 16 * 1024 * 1024
VMEM_LIMIT = 56 * 1024 * 1024
NEG = -1e30


def _cparams(*sem):
    return pltpu.CompilerParams(dimension_semantics=sem, vmem_limit_bytes=VMEM_LIMIT)


def _dot(a, b):
    return jnp.dot(a, b, preferred_element_type=F32)


def _dot_nt(a, b):
    return lax.dot_general(a, b, (((1,), (1,)), ((), ())), preferred_element_type=F32)


def _dot_tn(a, b):
    return lax.dot_general(a, b, (((0,), (0,)), ((), ())), preferred_element_type=F32)


def _rms(x, g):
    r = lax.rsqrt(jnp.mean(x * x, axis=-1, keepdims=True) + EPS)
    return x * r * g


def _rms_bwd(x, g, dy):
    r = lax.rsqrt(jnp.mean(x * x, axis=-1, keepdims=True) + EPS)
    n = x * r
    dn = dy * g
    dx = r * (dn - n * jnp.mean(dn * n, axis=-1, keepdims=True))
    return dx, jnp.sum(dy * n, axis=0, keepdims=True)


def _rope(x, cos, sin_a, sin_b):
    return x * cos + pltpu.roll(x, 112, 1) * sin_a + pltpu.roll(x, 16, 1) * sin_b


def _rope_t(g, cos, sin_a, sin_b):
    return g * cos + pltpu.roll(g * sin_a, 16, 1) + pltpu.roll(g * sin_b, 112, 1)


def _row_spec(tm, width):
    return pl.BlockSpec((tm, width), lambda r: (r, 0))


def _full_spec(shape):
    return pl.BlockSpec(shape, lambda *_: (0,) * len(shape))


def _accumulate(ref, val, first):
    @pl.when(first)
    def _():
        ref[...] = val

    @pl.when(jnp.logical_not(first))
    def _():
        ref[...] += val


def _proj_in_fwd(x, g_mix, w_a, g_q, w_uq, g_kv, w_ukv, cos, sin_a, sin_b):
    s, d = x.shape
    tm = min(ROW_TILE, s)

    def body(x_ref, gm_ref, wa_ref, gq_ref, wuq_ref, gkv_ref, wukv_ref, cos_ref, sa_ref, sb_ref,
             u_ref, cq_ref, ckv_ref, cqn_ref, ckvn_ref, qn_ref, qr_ref, kv_ref, kr_ref, sbq_ref):
        u = _rms(x_ref[...], gm_ref[...]).astype(BF16)
        u_ref[...] = u
        cq = _dot(u, wa_ref[:, 0:256])
        ckv = _dot(u, wa_ref[:, 256:384])
        kr = _dot(u, wa_ref[:, 384:512])
        cq_ref[...] = cq
        ckv_ref[...] = ckv
        cqn = _rms(cq, gq_ref[...]).astype(BF16)
        ckvn = _rms(ckv, gkv_ref[...]).astype(BF16)
        cqn_ref[...] = cqn
        ckvn_ref[...] = ckvn
        cos_t, sa_t, sb_t = cos_ref[...], sa_ref[...], sb_ref[...]
        qn_ref[...] = (_dot(cqn, wuq_ref[:, 0:512]) * MLA_SCALE).astype(BF16)
        for half in range(2):
            lo = 512 + half * LANES
            qr = _dot(cqn, wuq_ref[:, lo:lo + LANES])
            qr_ref[:, half * LANES:(half + 1) * LANES] = (_rope(qr, cos_t, sa_t, sb_t) * MLA_SCALE).astype(BF16)
        kv_ref[...] = _dot(ckvn, wukv_ref[...]).astype(BF16)
        krt = kr + pltpu.roll(kr, 32, 1) + pltpu.roll(kr, 64, 1) + pltpu.roll(kr, 96, 1)
        kr_ref[...] = _rope(krt, cos_t, sa_t, sb_t).astype(BF16)
        sbq_ref[:, 0:512] = (_dot(u, wa_ref[:, 512:1024]) * (SB_SCALE * LOG2E)).astype(BF16)
        sbq_ref[:, 512:1536] = _dot(u, wa_ref[:, 1024:2048]).astype(BF16)

    outs = [
        jax.ShapeDtypeStruct((s, d), BF16),
        jax.ShapeDtypeStruct((s, 256), F32),
        jax.ShapeDtypeStruct((s, 128), F32),
        jax.ShapeDtypeStruct((s, 256), BF16),
        jax.ShapeDtypeStruct((s, 128), BF16),
        jax.ShapeDtypeStruct((s, 512), BF16),
        jax.ShapeDtypeStruct((s, 256), BF16),
        jax.ShapeDtypeStruct((s, 1024), BF16),
        jax.ShapeDtypeStruct((s, 128), BF16),
        jax.ShapeDtypeStruct((s, 1536), BF16),
    ]
    return pl.pallas_call(
        body, name="proj_in_fwd", grid=(s // tm,), out_shape=outs,
        in_specs=[_row_spec(tm, d), _full_spec(g_mix.shape), _full_spec(w_a.shape), _full_spec(g_q.shape),
                  _full_spec(w_uq.shape), _full_spec(g_kv.shape), _full_spec(w_ukv.shape),
                  _row_spec(tm, LANES), _row_spec(tm, LANES), _row_spec(tm, LANES)],
        out_specs=[_row_spec(tm, o.shape[1]) for o in outs],
        compiler_params=_cparams("arbitrary"),
    )(x, g_mix, w_a, g_q, w_uq, g_kv, w_ukv, cos, sin_a, sin_b)


def _attn_out_fwd(o_mla, o_sb, g_mla, g_sb, w_o, x, g_ffn):
    s, d = x.shape
    tm = min(ROW_TILE, s)

    def body(oa_ref, ob_ref, ga_ref, gb_ref, wo_ref, x_ref, gf_ref, merged_ref, h1_ref, f_ref):
        na = _rms(oa_ref[...], ga_ref[...]).astype(BF16)
        nb = _rms(ob_ref[...], gb_ref[...]).astype(BF16)
        merged_ref[:, 0:512] = na
        merged_ref[:, 512:1024] = nb
        h1 = x_ref[...] + _dot(na, wo_ref[0:512, :]) + _dot(nb, wo_ref[512:1024, :])
        h1_ref[...] = h1
        f_ref[...] = _rms(h1, gf_ref[...]).astype(BF16)

    outs = [jax.ShapeDtypeStruct((s, d), BF16), jax.ShapeDtypeStruct((s, d), F32), jax.ShapeDtypeStruct((s, d), BF16)]
    return pl.pallas_call(
        body, name="attn_out_fwd", grid=(s // tm,), out_shape=outs,
        in_specs=[_row_spec(tm, 512), _row_spec(tm, 512), _full_spec(g_mla.shape), _full_spec(g_sb.shape),
                  _full_spec(w_o.shape), _row_spec(tm, d), _full_spec(g_ffn.shape)],
        out_specs=[_row_spec(tm, d)] * 3,
        compiler_params=_cparams("arbitrary"),
    )(o_mla, o_sb, g_mla, g_sb, w_o, x, g_ffn)


def _ffn_tile(d_ff):
    return d_ff // 2 if (d_ff // 2) % LANES == 0 else d_ff


def _ffn_fwd(f, h1, w_gate, w_up, w_down):
    s, d = h1.shape
    d_ff = w_gate.shape[1]
    tm = min(ROW_TILE, s)
    tf = _ffn_tile(d_ff)

    def body(f_ref, h1_ref, wg_ref, wu_ref, wd_ref, gate_ref, up_ref, h2_ref):
        j = pl.program_id(1)
        fb = f_ref[...]
        gate = _dot(fb, wg_ref[...])
        up = _dot(fb, wu_ref[...])
        gate_ref[...] = gate.astype(BF16)
        up_ref[...] = up.astype(BF16)
        act = (gate * jax.nn.sigmoid(gate) * up).astype(BF16)
        part = _dot(act, wd_ref[...])

        @pl.when(j == 0)
        def _():
            h2_ref[...] = h1_ref[...] + part

        @pl.when(j != 0)
        def _():
            h2_ref[...] += part

    outs = [jax.ShapeDtypeStruct((s, d_ff), BF16), jax.ShapeDtypeStruct((s, d_ff), BF16), jax.ShapeDtypeStruct((s, d), F32)]
    return pl.pallas_call(
        body, name="ffn_fwd", grid=(s // tm, d_ff // tf), out_shape=outs,
        in_specs=[pl.BlockSpec((tm, d), lambda r, j: (r, 0)), pl.BlockSpec((tm, d), lambda r, j: (r, 0)),
                  pl.BlockSpec((d, tf), lambda r, j: (0, j)), pl.BlockSpec((d, tf), lambda r, j: (0, j)),
                  pl.BlockSpec((tf, d), lambda r, j: (j, 0))],
        out_specs=[pl.BlockSpec((tm, tf), lambda r, j: (r, j)), pl.BlockSpec((tm, tf), lambda r, j: (r, j)),
                   pl.BlockSpec((tm, d), lambda r, j: (r, 0))],
        compiler_params=_cparams("arbitrary", "arbitrary"),
    )(f, h1, w_gate, w_up, w_down)


def _final_loss(h2, target, g_final):
    s, d = h2.shape
    tm = min(ROW_TILE, s)

    def body(h2_ref, t_ref, g_ref, loss_ref, dh2_ref, dh2b_ref, dg_ref):
        first = pl.program_id(0) == 0
        h2v = h2_ref[...]
        g = g_ref[...]
        diff = _rms(h2v, g) - t_ref[...]
        part = 0.5 * jnp.sum(jnp.mean(diff * diff, axis=-1, keepdims=True), axis=0, keepdims=True)
        _accumulate(loss_ref, jnp.broadcast_to(part, loss_ref.shape), first)
        dx, dg = _rms_bwd(h2v, g, diff * (1.0 / d))
        dh2_ref[...] = dx
        dh2b_ref[...] = dx.astype(BF16)
        _accumulate(dg_ref, dg, first)

    outs = [jax.ShapeDtypeStruct((1, LANES), F32), jax.ShapeDtypeStruct((s, d), F32), jax.ShapeDtypeStruct((s, d), BF16),
            jax.ShapeDtypeStruct((1, d), F32)]
    return pl.pallas_call(
        body, name="final_loss", grid=(s // tm,), out_shape=outs,
        in_specs=[_row_spec(tm, d), _row_spec(tm, d), _full_spec((1, d))],
        out_specs=[_full_spec((1, LANES)), _row_spec(tm, d), _row_spec(tm, d), _full_spec((1, d))],
        compiler_params=_cparams("arbitrary"),
    )(h2, target, g_final)


def _tile_iotas(t):
    return lax.broadcasted_iota(jnp.int32, (t, t), 0), lax.broadcasted_iota(jnp.int32, (t, t), 1)


def _stacked_mask(t, strict):
    row = lax.broadcasted_iota(jnp.int32, (2 * t, t), 0)
    col = lax.broadcasted_iota(jnp.int32, (2 * t, t), 1)
    row = jnp.where(row >= t, row - t, row)
    return col < row if strict else col <= row


def _mla_fwd(qn, qr, kv, kr):
    s = qn.shape[0]
    t = min(ATT_TILE, s)
    pairs = MLA_HEADS // 2

    def body(qn_ref, qr_ref, kn_ref, v_ref, kr_ref, o_ref, lse_ref, qcat_ref, m_ref, l_ref, acc_ref):
        hp, i = pl.program_id(0), pl.program_id(1)
        lane = lax.broadcasted_iota(jnp.int32, (1, LANES), 1)
        row, col = _tile_iotas(t)
        causal = col <= row
        q_pair, q_quad = qn_ref[...], qr_ref[...]
        zero = jnp.zeros_like(q_pair)
        for hh in range(2):
            in_head = (lane // HEAD_DIM) == hh
            in_rope = (lane // MLA_ROPE) == (hp % 2) * 2 + hh
            qcat_ref[hh * t:(hh + 1) * t, 0:LANES] = jnp.where(in_head, q_pair, zero)
            qcat_ref[hh * t:(hh + 1) * t, LANES:2 * LANES] = jnp.where(in_rope, q_quad, zero)
        m_ref[...] = jnp.full_like(m_ref, NEG)
        l_ref[...] = jnp.zeros_like(l_ref)
        acc_ref[...] = jnp.zeros_like(acc_ref)

        def tile(j, masked):
            rows = pl.ds(pl.multiple_of(j * t, t), t)
            kcat = jnp.concatenate([kn_ref[rows, :], kr_ref[rows, :]], axis=1)
            v_ones = jnp.concatenate([v_ref[rows, :], jnp.ones((t, LANES), BF16)], axis=1)
            scores = [_dot_nt(qcat_ref[hh * t:(hh + 1) * t, :], kcat) for hh in range(2)]
            for hh in range(2):
                half = slice(hh * t, (hh + 1) * t)
                sc = jnp.where(causal, scores[hh], NEG) if masked else scores[hh]
                m = m_ref[half, :]
                m_new = jnp.maximum(m, jnp.max(sc, axis=-1, keepdims=True))
                alpha = jnp.exp(m - m_new)
                p = jnp.exp(sc - jnp.concatenate([m_new] * (t // LANES), axis=1))
                pv = _dot(p.astype(BF16), v_ones)
                l_ref[half, :] = alpha * l_ref[half, :] + pv[:, LANES:]
                acc_ref[half, :] = alpha * acc_ref[half, :] + pv[:, :LANES]
                m_ref[half, :] = m_new

        tile(i, True)

        def step(j, carry):
            tile(j, False)
            return carry

        lax.fori_loop(0, i, step, 0)
        first = (lane // HEAD_DIM) == 0
        o = acc_ref[...] / l_ref[...]
        lse = m_ref[...] + jnp.log(l_ref[...])
        o_ref[...] = jnp.where(first, o[0:t], o[t:2 * t])
        lse_ref[...] = jnp.where(first, lse[0:t], lse[t:2 * t])

    outs = [jax.ShapeDtypeStruct((s, 512), F32), jax.ShapeDtypeStruct((pairs, s, LANES), F32)]
    return pl.pallas_call(
        body, name="mla_fwd", grid=(pairs, s // t), out_shape=outs,
        in_specs=[pl.BlockSpec((t, LANES), lambda hp, i: (i, hp)), pl.BlockSpec((t, LANES), lambda hp, i: (i, hp // 2)),
                  pl.BlockSpec((s, LANES), lambda hp, i: (0, hp)), pl.BlockSpec((s, LANES), lambda hp, i: (0, 4 + hp)),
                  pl.BlockSpec((s, LANES), lambda hp, i: (0, 0))],
        out_specs=[pl.BlockSpec((t, LANES), lambda hp, i: (i, hp)), pl.BlockSpec((None, t, LANES), lambda hp, i: (hp, i, 0))],
        scratch_shapes=[pltpu.VMEM((2 * t, 2 * LANES), BF16), pltpu.VMEM((2 * t, LANES), F32), pltpu.VMEM((2 * t, LANES), F32),
                        pltpu.VMEM((2 * t, LANES), F32)],
        compiler_params=_cparams("arbitrary", "arbitrary"),
    )(qn, qr, kv, kv, kr)


HEADS = (0, 1)


def _sb_logs(z2, strict, masked):
    log_b = jnp.minimum(z2, 0.0) - jnp.log2(1.0 + jnp.exp2(-jnp.abs(z2)))
    log_1m = log_b - z2
    if masked:
        log_1m = jnp.where(strict, log_1m, 0.0)
    return log_1m, log_b


def _block_totals(x):
    t, w = x.shape
    nb = max(w // TRI, 1)
    bw = w // nb
    blocks = [x[:, b * bw:(b + 1) * bw] for b in range(nb)]
    totals = [jnp.broadcast_to(jnp.sum(blk, axis=-1, keepdims=True), (t, LANES)) for blk in blocks]
    whole = totals[0]
    for tot in totals[1:]:
        whole = whole + tot
    return blocks, totals, whole


def _running_sums(blocks, totals, tri, carry, suffix):
    nb = len(blocks)
    reps = blocks[0].shape[1] // LANES
    outs = [None] * nb
    run = carry
    for b in (range(nb - 1, -1, -1) if suffix else range(nb)):
        outs[b] = _dot(blocks[b].astype(BF16), tri) + jnp.concatenate([run] * reps, axis=1)
        run = run + totals[b]
    return outs[0] if nb == 1 else jnp.concatenate(outs, axis=1)


def _tri(t, rel):
    n = min(TRI, t)
    row, col = _tile_iotas(n)
    return rel(row, col).astype(BF16)


def _sb_fwd(qkv):
    s = qkv.shape[0]
    t = min(ATT_TILE, s)
    pairs = SB_HEADS // 2

    def body(q_ref, k_ref, v_ref, o_ref, tot_ref, qm_ref, right_ref, acc_ref):
        i = pl.program_id(1)
        lane = lax.broadcasted_iota(jnp.int32, (1, LANES), 1)
        row, col = _tile_iotas(t)
        strict = col < row
        t_suffix = _tri(t, lambda r, c: r > c)
        q_pair = q_ref[...]
        for hh in range(2):
            qm_ref[hh] = jnp.where((lane // HEAD_DIM) == hh, q_pair, jnp.zeros_like(q_pair))
        right_ref[...] = jnp.zeros_like(right_ref)
        acc_ref[...] = jnp.zeros_like(acc_ref)

        def tile(j, masked):
            rows = pl.ds(pl.multiple_of(j * t, t), t)
            k, v = k_ref[rows, :], v_ref[rows, :]
            for hh in HEADS:
                log_1m, log_b = _sb_logs(_dot_nt(qm_ref[hh], k), strict, masked)
                blocks, totals, whole = _block_totals(log_1m)
                a = jnp.exp2(log_b + _running_sums(blocks, totals, t_suffix, right_ref[hh], True))
                if masked:
                    a = jnp.where(strict, a, 0.0)
                right_ref[hh] += whole
                acc_ref[hh] += _dot(a.astype(BF16), v)

        tile(i, True)

        def step(n, carry):
            tile(i - 1 - n, False)
            return carry

        lax.fori_loop(0, i, step, 0)
        first = (lane // HEAD_DIM) == 0
        o_ref[...] = jnp.where(first, acc_ref[0], acc_ref[1])
        tot_ref[...] = jnp.where(first, right_ref[0], right_ref[1])

    outs = [jax.ShapeDtypeStruct((s, 512), F32), jax.ShapeDtypeStruct((pairs, s, LANES), F32)]
    return pl.pallas_call(
        body, name="sb_fwd", grid=(pairs, s // t), out_shape=outs,
        in_specs=[pl.BlockSpec((t, LANES), lambda hp, i: (i, hp)), pl.BlockSpec((s, LANES), lambda hp, i: (0, 4 + hp)),
                  pl.BlockSpec((s, LANES), lambda hp, i: (0, 8 + hp))],
        out_specs=[pl.BlockSpec((t, LANES), lambda hp, i: (i, hp)), pl.BlockSpec((None, t, LANES), lambda hp, i: (hp, i, 0))],
        scratch_shapes=[pltpu.VMEM((2, t, LANES), BF16), pltpu.VMEM((2, t, LANES), F32), pltpu.VMEM((2, t, LANES), F32)],
        compiler_params=_cparams("arbitrary", "arbitrary"),
    )(qkv, qkv, qkv)


def _sb_bwd(qkv, do, tot):
    s = qkv.shape[0]
    t = min(ATT_TILE, s)
    pairs = SB_HEADS // 2

    def body(q_ref, k_ref, v_ref, do_ref, tot_ref, dq_ref, dk_ref, dv_ref,
             qm_ref, dob_ref, total_s, left_l, left_g, dq_s):
        i = pl.program_id(1)

        @pl.when(i == 0)
        def _():
            dk_ref[...] = jnp.zeros_like(dk_ref)
            dv_ref[...] = jnp.zeros_like(dv_ref)

        lane = lax.broadcasted_iota(jnp.int32, (1, LANES), 1)
        row, col = _tile_iotas(t)
        strict = col < row
        t_suffix = _tri(t, lambda r, c: r > c)
        t_excl = _tri(t, lambda r, c: r < c)
        q_pair, do_pair, tot_pair = q_ref[...], do_ref[...], tot_ref[...]
        for hh in range(2):
            in_head = (lane // HEAD_DIM) == hh
            qm_ref[hh] = jnp.where(in_head, q_pair, jnp.zeros_like(q_pair))
            dob_ref[hh] = jnp.where(in_head, do_pair, 0.0).astype(BF16)
            total_s[hh] = jnp.broadcast_to(
                jnp.sum(jnp.where(lane == hh * HEAD_DIM, tot_pair, 0.0), axis=-1, keepdims=True), (t, LANES))
        left_l[...] = jnp.zeros_like(left_l)
        left_g[...] = jnp.zeros_like(left_g)
        dq_s[...] = jnp.zeros_like(dq_s)
        reps = t // LANES

        def tile(j, masked):
            rows = pl.ds(pl.multiple_of(j * t, t), t)
            k, v = k_ref[rows, :], v_ref[rows, :]
            z2 = [_dot_nt(qm_ref[hh], k) for hh in HEADS]
            d_a = [_dot_nt(dob_ref[hh], v) for hh in HEADS]
            for hh in HEADS:
                qm, dob = qm_ref[hh], dob_ref[hh]
                log_1m, log_b = _sb_logs(z2[hh], strict, masked)
                blocks, totals, whole = _block_totals(log_1m)
                done = left_l[hh] + whole
                left_l[hh] = done
                a = jnp.exp2(log_b + _running_sums(blocks, totals, t_suffix, total_s[hh] - done, True))
                if masked:
                    a = jnp.where(strict, a, 0.0)
                g = a * d_a[hh]
                blocks, totals, whole = _block_totals(g)
                before = _running_sums(blocks, totals, t_excl, left_g[hh], False)
                left_g[hh] += whole
                dz = g - jnp.exp2(log_b) * (g + before)
                if masked:
                    dz = jnp.where(strict, dz, 0.0)
                dzb = dz.astype(BF16)
                dq_s[hh] += _dot(dzb, k)
                dk_ref[rows, :] += _dot_tn(dzb, qm)
                dv_ref[rows, :] += _dot_tn(a.astype(BF16), dob)

        def step(j, carry):
            tile(j, False)
            return carry

        lax.fori_loop(0, i, step, 0)
        tile(i, True)
        dq_ref[...] = jnp.where((lane // HEAD_DIM) == 0, dq_s[0], dq_s[1])

        @pl.when(i == s // t - 1)
        def _():
            dk_ref[...] *= 1.0 / LOG2E

    outs = [jax.ShapeDtypeStruct((s, 512), F32)] * 3
    return pl.pallas_call(
        body, name="sb_bwd", grid=(pairs, s // t), out_shape=outs,
        in_specs=[pl.BlockSpec((t, LANES), lambda hp, i: (i, hp)), pl.BlockSpec((s, LANES), lambda hp, i: (0, 4 + hp)),
                  pl.BlockSpec((s, LANES), lambda hp, i: (0, 8 + hp)), pl.BlockSpec((t, LANES), lambda hp, i: (i, hp)),
                  pl.BlockSpec((None, t, LANES), lambda hp, i: (hp, i, 0))],
        out_specs=[pl.BlockSpec((t, LANES), lambda hp, i: (i, hp)), pl.BlockSpec((s, LANES), lambda hp, i: (0, hp)),
                   pl.BlockSpec((s, LANES), lambda hp, i: (0, hp))],
        scratch_shapes=[pltpu.VMEM((2, t, LANES), BF16), pltpu.VMEM((2, t, LANES), BF16)]
        + [pltpu.VMEM((2, t, LANES), F32)] * 4,
        compiler_params=_cparams("arbitrary", "arbitrary"),
    )(qkv, qkv, qkv, do, tot)


def _mla_bwd(qn, qr, kv, kr, do, o, lse):
    s = qn.shape[0]
    t = min(ATT_TILE, s)
    pairs = MLA_HEADS // 2

    def body(qn_ref, qr_ref, kn_ref, v_ref, kr_ref, do_ref, o_ref, lse_ref, dqn_ref, dqr_ref, dkn_ref, dv_ref, dkr_ref,
             qcat_ref, dob_ref, lse_s, delta_s, dq_s):
        hp, i = pl.program_id(0), pl.program_id(1)

        @pl.when(i == 0)
        def _():
            dkn_ref[...] = jnp.zeros_like(dkn_ref)
            dv_ref[...] = jnp.zeros_like(dv_ref)
            dkr_ref[...] = jnp.zeros_like(dkr_ref)

        lane = lax.broadcasted_iota(jnp.int32, (1, LANES), 1)
        row, col = _tile_iotas(t)
        causal = col <= row
        q_pair, q_quad, do_pair, lse_pair = qn_ref[...], qr_ref[...], do_ref[...], lse_ref[...]
        do_o = do_pair * o_ref[...]
        zero = jnp.zeros_like(q_pair)
        ropes = []
        for hh in range(2):
            in_head = (lane // HEAD_DIM) == hh
            in_rope = (lane // MLA_ROPE) == (hp % 2) * 2 + hh
            ropes.append(in_rope)
            qcat_ref[hh, :, 0:LANES] = jnp.where(in_head, q_pair, zero)
            qcat_ref[hh, :, LANES:2 * LANES] = jnp.where(in_rope, q_quad, zero)
            dob_ref[hh] = jnp.where(in_head, do_pair, 0.0).astype(BF16)
            delta_s[hh] = jnp.broadcast_to(jnp.sum(jnp.where(in_head, do_o, 0.0), axis=-1, keepdims=True), (t, LANES))
            lse_s[hh] = jnp.broadcast_to(
                jnp.sum(jnp.where(lane == hh * HEAD_DIM, lse_pair, 0.0), axis=-1, keepdims=True), (t, LANES))
        dq_s[...] = jnp.zeros_like(dq_s)
        reps = t // LANES

        def tile(j, masked):
            rows = pl.ds(pl.multiple_of(j * t, t), t)
            kcat = jnp.concatenate([kn_ref[rows, :], kr_ref[rows, :]], axis=1)
            v = v_ref[rows, :]
            sc = [_dot_nt(qcat_ref[hh], kcat) for hh in HEADS]
            dp = [_dot_nt(dob_ref[hh], v) for hh in HEADS]
            p = [jnp.exp(sc[hh] - jnp.concatenate([lse_s[hh]] * reps, axis=1)) for hh in HEADS]
            if masked:
                p = [jnp.where(causal, p[hh], 0.0) for hh in HEADS]
            ds = [(p[hh] * (dp[hh] - jnp.concatenate([delta_s[hh]] * reps, axis=1))).astype(BF16) for hh in HEADS]
            for hh in HEADS:
                dq_s[hh] += _dot(ds[hh], kcat)
            dkcat = _dot_tn(ds[0], qcat_ref[0]) + _dot_tn(ds[1], qcat_ref[1])
            dkn_ref[rows, :] += dkcat[:, 0:LANES]
            dkr_ref[rows, :] += dkcat[:, LANES:2 * LANES]
            dv_ref[rows, :] += _dot_tn(p[0].astype(BF16), dob_ref[0]) + _dot_tn(p[1].astype(BF16), dob_ref[1])

        def step(j, carry):
            tile(j, False)
            return carry

        lax.fori_loop(0, i, step, 0)
        tile(i, True)
        dqn_ref[...] = jnp.where((lane // HEAD_DIM) == 0, dq_s[0, :, 0:LANES], dq_s[1, :, 0:LANES])
        dqr_ref[...] = (jnp.where(ropes[0], dq_s[0, :, LANES:2 * LANES], 0.0)
                        + jnp.where(ropes[1], dq_s[1, :, LANES:2 * LANES], 0.0))

    pair_block = pl.BlockSpec((t, LANES), lambda hp, i: (i, hp))
    outs = [jax.ShapeDtypeStruct((s, 512), F32), jax.ShapeDtypeStruct((pairs, s, LANES), F32),
            jax.ShapeDtypeStruct((s, 512), F32), jax.ShapeDtypeStruct((s, 512), F32), jax.ShapeDtypeStruct((pairs, s, LANES), F32)]
    return pl.pallas_call(
        body, name="mla_bwd", grid=(pairs, s // t), out_shape=outs,
        in_specs=[pair_block, pl.BlockSpec((t, LANES), lambda hp, i: (i, hp // 2)),
                  pl.BlockSpec((s, LANES), lambda hp, i: (0, hp)), pl.BlockSpec((s, LANES), lambda hp, i: (0, 4 + hp)),
                  pl.BlockSpec((s, LANES), lambda hp, i: (0, 0)), pair_block, pair_block,
                  pl.BlockSpec((None, t, LANES), lambda hp, i: (hp, i, 0))],
        out_specs=[pair_block, pl.BlockSpec((None, t, LANES), lambda hp, i: (hp, i, 0)),
                   pl.BlockSpec((s, LANES), lambda hp, i: (0, hp)), pl.BlockSpec((s, LANES), lambda hp, i: (0, hp)),
                   pl.BlockSpec((None, s, LANES), lambda hp, i: (hp, 0, 0))],
        scratch_shapes=[pltpu.VMEM((2, t, 2 * LANES), BF16), pltpu.VMEM((2, t, LANES), BF16), pltpu.VMEM((2, t, LANES), F32),
                        pltpu.VMEM((2, t, LANES), F32), pltpu.VMEM((2, t, 2 * LANES), F32)],
        compiler_params=_cparams("arbitrary", "arbitrary"),
    )(qn, qr, kv, kv, kr, do, o, lse)


def _ffn_bwd(dh2, dh2b, gate, up, h1, g_ffn, w_down_t, w_gate_t, w_up_t):
    s, d = h1.shape
    d_ff = gate.shape[1]
    tm = min(FFN_BWD_ROW_TILE, s)
    tf = _ffn_tile(d_ff)

    def act_body(dh2b_ref, gate_ref, up_ref, wdt_ref, dgate_ref, dup_ref, act_ref):
        dact = _dot(dh2b_ref[...], wdt_ref[...])
        gate_v = gate_ref[...].astype(F32)
        up_v = up_ref[...].astype(F32)
        sig = jax.nn.sigmoid(gate_v)
        silu = gate_v * sig
        dup_ref[...] = (dact * silu).astype(BF16)
        dgate_ref[...] = (dact * up_v * (sig * (1.0 + gate_v * (1.0 - sig)))).astype(BF16)
        act_ref[...] = (silu * up_v).astype(BF16)

    ff = pl.BlockSpec((tm, tf), lambda j, r: (r, j))
    dgate, dup, act = pl.pallas_call(
        act_body, name="ffn_bwd_act", grid=(d_ff // tf, s // tm), out_shape=[jax.ShapeDtypeStruct((s, d_ff), BF16)] * 3,
        in_specs=[pl.BlockSpec((tm, d), lambda j, r: (r, 0)), ff, ff, pl.BlockSpec((d, tf), lambda j, r: (0, j))],
        out_specs=[ff, ff, ff],
        compiler_params=_cparams("arbitrary", "arbitrary"),
    )(dh2b, gate, up, w_down_t)

    def df_body(dgate_ref, dup_ref, dh2_ref, h1_ref, g_ref, wgt_ref, wut_ref, dh1_ref, dh1b_ref, dg_ref):
        df = _dot(dgate_ref[...], wgt_ref[...]) + _dot(dup_ref[...], wut_ref[...])
        dx, dg = _rms_bwd(h1_ref[...], g_ref[...], df)
        dh1 = dh2_ref[...] + dx
        dh1_ref[...] = dh1
        dh1b_ref[...] = dh1.astype(BF16)
        _accumulate(dg_ref, dg, pl.program_id(0) == 0)

    outs = [jax.ShapeDtypeStruct((s, d), F32), jax.ShapeDtypeStruct((s, d), BF16), jax.ShapeDtypeStruct((1, d), F32)]
    dh1, dh1b, dg = pl.pallas_call(
        df_body, name="ffn_bwd_df", grid=(s // tm,), out_shape=outs,
        in_specs=[_row_spec(tm, d_ff), _row_spec(tm, d_ff), _row_spec(tm, d), _row_spec(tm, d), _full_spec((1, d)),
                  _full_spec(w_gate_t.shape), _full_spec(w_up_t.shape)],
        out_specs=[_row_spec(tm, d), _row_spec(tm, d), _full_spec((1, d))],
        compiler_params=_cparams("arbitrary"),
    )(dgate, dup, dh2, h1, g_ffn, w_gate_t, w_up_t)
    return dgate, dup, act, dh1, dh1b, dg


def _largest_tile(n, cap):
    for cand in range(cap, 0, -LANES):
        if n % cand == 0:
            return cand
    return n


def _tn_matmul(a, b, name):
    assert a.dtype == BF16 and b.dtype == BF16
    s, m = a.shape
    n = b.shape[1]
    if s * m * 2 <= TN_RESIDENT_BYTES:
        tm, tn = m, min(n, TN_BLOCK)
    else:
        tm, tn = TN_BLOCK, n

    def body(a_ref, b_ref, o_ref):
        o_ref[...] = _dot_tn(a_ref[...], b_ref[...])

    return pl.pallas_call(
        body, name=name, grid=(m // tm, n // tn), out_shape=jax.ShapeDtypeStruct((m, n), F32),
        in_specs=[pl.BlockSpec((s, tm), lambda i, j: (0, i)), pl.BlockSpec((s, tn), lambda i, j: (0, j))],
        out_specs=pl.BlockSpec((tm, tn), lambda i, j: (i, j)),
        compiler_params=_cparams("arbitrary", "arbitrary"),
    )(a, b)


def _attn_out_bwd(dh1, w_o_t, o_mla, o_sb, g_mla, g_sb):
    s, d = dh1.shape
    tm = min(ROW_TILE, s)

    def body(dh1_ref, wot_ref, oa_ref, ob_ref, ga_ref, gb_ref, doa_ref, dob_ref, dga_ref, dgb_ref):
        first = pl.program_id(0) == 0
        dh1b = dh1_ref[...]
        dxa, dga = _rms_bwd(oa_ref[...], ga_ref[...], _dot(dh1b, wot_ref[:, 0:512]))
        dxb, dgb = _rms_bwd(ob_ref[...], gb_ref[...], _dot(dh1b, wot_ref[:, 512:1024]))
        doa_ref[...] = dxa
        dob_ref[...] = dxb
        _accumulate(dga_ref, dga, first)
        _accumulate(dgb_ref, dgb, first)

    outs = [jax.ShapeDtypeStruct((s, 512), F32)] * 2 + [jax.ShapeDtypeStruct((1, 512), F32)] * 2
    return pl.pallas_call(
        body, name="attn_out_bwd", grid=(s // tm,), out_shape=outs,
        in_specs=[_row_spec(tm, d), _full_spec(w_o_t.shape), _row_spec(tm, 512), _row_spec(tm, 512),
                  _full_spec((1, 512)), _full_spec((1, 512))],
        out_specs=[_row_spec(tm, 512), _row_spec(tm, 512), _full_spec((1, 512)), _full_spec((1, 512))],
        compiler_params=_cparams("arbitrary"),
    )(dh1, w_o_t, o_mla, o_sb, g_mla, g_sb)


def _proj_in_bwd(dqn, dqr, dkn, dv, dkr, dq_sb, dk_sb, dv_sb, cq, ckv, x, dh1, cos, sin_a, sin_b,
                 g_q, g_kv, g_mix, w_uq_t, w_ukv_t, w_a_t):
    s, d = x.shape
    tm = min(PROJ_BWD_ROW_TILE, s)

    def body(dqn_ref, dqr_ref, dkn_ref, dv_ref, dkr_ref, dqs_ref, dks_ref, dvs_ref, cq_ref, ckv_ref, x_ref, dh1_ref,
             cos_ref, sa_ref, sb_ref, gq_ref, gkv_ref, gm_ref, wuqt_ref, wukvt_ref, wat_ref,
             dx_ref, dproj_ref, dq_ref, dkv_ref, dgq_ref, dgkv_ref, dgm_ref):
        first = pl.program_id(0) == 0
        lane = lax.broadcasted_iota(jnp.int32, (1, LANES), 1)
        cos_t, sa_t, sb_t = cos_ref[...], sa_ref[...], sb_ref[...]
        dq_ref[:, 0:512] = (dqn_ref[...] * MLA_SCALE).astype(BF16)
        for half in range(2):
            quad = (dqr_ref[2 * half] + dqr_ref[2 * half + 1]) * MLA_SCALE
            dq_ref[:, 512 + half * LANES:512 + (half + 1) * LANES] = _rope_t(quad, cos_t, sa_t, sb_t).astype(BF16)
        dcq, dgq = _rms_bwd(cq_ref[...], gq_ref[...], _dot(dq_ref[...], wuqt_ref[...]))
        _accumulate(dgq_ref, dgq, first)
        dkv_ref[:, 0:512] = dkn_ref[...].astype(BF16)
        dkv_ref[:, 512:1024] = dv_ref[...].astype(BF16)
        dckv, dgkv = _rms_bwd(ckv_ref[...], gkv_ref[...], _dot(dkv_ref[...], wukvt_ref[...]))
        _accumulate(dgkv_ref, dgkv, first)
        g = _rope_t(dkr_ref[0] + dkr_ref[1] + dkr_ref[2] + dkr_ref[3], cos_t, sa_t, sb_t)
        g = g + pltpu.roll(g, 96, 1) + pltpu.roll(g, 64, 1) + pltpu.roll(g, 32, 1)
        dproj_ref[:, 0:256] = dcq.astype(BF16)
        dproj_ref[:, 256:384] = dckv.astype(BF16)
        dproj_ref[:, 384:512] = jnp.where(lane < MLA_ROPE, g, 0.0).astype(BF16)
        dproj_ref[:, 512:1024] = (dqs_ref[...] * SB_SCALE).astype(BF16)
        dproj_ref[:, 1024:1536] = dks_ref[...].astype(BF16)
        dproj_ref[:, 1536:2048] = dvs_ref[...].astype(BF16)
        dxn, dgm = _rms_bwd(x_ref[...], gm_ref[...], _dot(dproj_ref[...], wat_ref[...]))
        dx_ref[...] = dh1_ref[...] + dxn
        _accumulate(dgm_ref, dgm, first)

    quad_spec = pl.BlockSpec((4, tm, LANES), lambda r: (0, r, 0))
    outs = [jax.ShapeDtypeStruct((s, d), F32), jax.ShapeDtypeStruct((s, 2048), BF16), jax.ShapeDtypeStruct((s, 768), BF16),
            jax.ShapeDtypeStruct((s, 1024), BF16), jax.ShapeDtypeStruct((1, 256), F32), jax.ShapeDtypeStruct((1, 128), F32),
            jax.ShapeDtypeStruct((1, d), F32)]
    return pl.pallas_call(
        body, name="proj_in_bwd", grid=(s // tm,), out_shape=outs,
        in_specs=[_row_spec(tm, 512), quad_spec, _row_spec(tm, 512), _row_spec(tm, 512), quad_spec,
                  _row_spec(tm, 512), _row_spec(tm, 512), _row_spec(tm, 512), _row_spec(tm, 256), _row_spec(tm, 128),
                  _row_spec(tm, d), _row_spec(tm, d), _row_spec(tm, LANES), _row_spec(tm, LANES), _row_spec(tm, LANES),
                  _full_spec((1, 256)), _full_spec((1, 128)), _full_spec((1, d)),
                  _full_spec(w_uq_t.shape), _full_spec(w_ukv_t.shape), _full_spec(w_a_t.shape)],
        out_specs=[_row_spec(tm, d), _row_spec(tm, 2048), _row_spec(tm, 768), _row_spec(tm, 1024),
                   _full_spec((1, 256)), _full_spec((1, 128)), _full_spec((1, d))],
        compiler_params=_cparams("arbitrary"),
    )(dqn, dqr, dkn, dv, dkr, dq_sb, dk_sb, dv_sb, cq, ckv, x, dh1, cos, sin_a, sin_b, g_q, g_kv, g_mix,
      w_uq_t, w_ukv_t, w_a_t)


ANY = pl.BlockSpec(memory_space=pl.ANY)


def _place():
    return lax.axis_index("x"), lax.axis_index("y"), lax.axis_index("c")


def _all_gather(shards, name):
    n = len(shards)

    def body(*refs):
        ins, outs = refs[:n], refs[n:2 * n]
        send_sems, recv_sems, local_sems = refs[2 * n:]
        x, y, c = _place()
        me, sibling = (x, y, c), (x, y, 1 - c)
        chips = [(1 - x, y), (x, 1 - y), (1 - x, 1 - y)]

        def slot(a, px, py, pc):
            return outs[a].at[4 * px + 2 * py + pc]

        def copy(a, k, block, to, src=None):
            return pltpu.make_async_remote_copy(
                src_ref=slot(a, *block) if src is None else src, dst_ref=slot(a, *block),
                send_sem=send_sems.at[a, k], recv_sem=recv_sems.at[a, k], device_id=to, device_id_type=MESH)

        mine, first, passed = [], [], []
        for a in range(n):
            own = pltpu.make_async_copy(ins[a], slot(a, *me), local_sems.at[a])
            own.start()
            mine.append(own)
            cps = [copy(a, 0, me, sibling, src=ins[a])]
            cps += [copy(a, 1 + j, me, (*chip, c), src=ins[a]) for j, chip in enumerate(chips)]
            for cp in cps:
                cp.start()
            first += cps
        for a in range(n):
            for j, chip in enumerate(chips):
                copy(a, 1 + j, (*chip, c), me).wait_recv()
                fwd = copy(a, 4 + j, (*chip, c), sibling)
                fwd.start()
                passed.append(fwd)
        for a in range(n):
            copy(a, 0, sibling, me).wait_recv()
            for j, chip in enumerate(chips):
                copy(a, 4 + j, (*chip, 1 - c), me).wait_recv()
        for cp in first + passed:
            cp.wait_send()
        for own in mine:
            own.wait()

    return pl.pallas_call(
        body, name=name,
        out_shape=[jax.ShapeDtypeStruct((N_DEV,) + v.shape, v.dtype) for v in shards],
        in_specs=[ANY] * n, out_specs=[ANY] * n,
        scratch_shapes=[pltpu.SemaphoreType.DMA((n, 7)), pltpu.SemaphoreType.DMA((n, 7)), pltpu.SemaphoreType.DMA((n,))],
    )(*shards)


def _sibling_exchange(parts):
    n = len(parts)

    def body(*refs):
        ins, outs = refs[:n], refs[n:2 * n]
        send_sems, recv_sems = refs[2 * n:]
        x, y, c = _place()
        copies = []
        for a in range(n):
            for q in range(4):
                cp = pltpu.make_async_remote_copy(
                    src_ref=ins[a].at[2 * q + 1 - c], dst_ref=outs[a].at[q],
                    send_sem=send_sems.at[a, q], recv_sem=recv_sems.at[a, q], device_id=(x, y, 1 - c), device_id_type=MESH)
                cp.start()
                copies.append(cp)
        for cp in copies:
            cp.wait()

    return pl.pallas_call(
        body, name="grad_sibling_exchange",
        out_shape=[jax.ShapeDtypeStruct((4,) + v.shape[1:], v.dtype) for v in parts],
        in_specs=[ANY] * n, out_specs=[ANY] * n,
        scratch_shapes=[pltpu.SemaphoreType.DMA((n, 4)), pltpu.SemaphoreType.DMA((n, 4))],
    )(*parts)


def _grad_row_tile(rows):
    return _largest_tile_rows(rows, 256)


def _largest_tile_rows(rows, cap):
    for cand in range(cap, 0, -8):
        if rows % cand == 0:
            return cand
    return rows


def _chip_sum(part, recv, name):
    _, r, cdim = part.shape
    part4 = part.reshape(4, 2, r, cdim)
    tr = _grad_row_tile(r)

    def body(p_ref, s_ref, o_ref):
        c = lax.axis_index("c")
        own = jnp.where(c == 1, p_ref[1], p_ref[0])
        o_ref[...] = (own + s_ref[...]).astype(BF16)

    return pl.pallas_call(
        body, name=name, grid=(4, r // tr), out_shape=jax.ShapeDtypeStruct((4, r, cdim), BF16),
        in_specs=[pl.BlockSpec((None, 2, tr, cdim), lambda q, i: (q, 0, i, 0)),
                  pl.BlockSpec((None, tr, cdim), lambda q, i: (q, i, 0))],
        out_specs=pl.BlockSpec((None, tr, cdim), lambda q, i: (q, i, 0)),
        compiler_params=_cparams("arbitrary", "arbitrary"),
    )(part4, recv)


def _chip_exchange(sums):
    n = len(sums)

    def body(*refs):
        ins, outs = refs[:n], refs[n:2 * n]
        send_sems, recv_sems, local_sems = refs[2 * n:]
        x, y, c = _place()
        my_chip = 2 * x + y
        peers = [(1 - x, y), (x, 1 - y), (1 - x, 1 - y)]
        copies, local = [], []
        for a in range(n):
            own = pltpu.make_async_copy(ins[a].at[my_chip], outs[a].at[my_chip], local_sems.at[a])
            own.start()
            local.append(own)
            for k, (tx, ty) in enumerate(peers):
                cp = pltpu.make_async_remote_copy(
                    src_ref=ins[a].at[2 * tx + ty], dst_ref=outs[a].at[my_chip],
                    send_sem=send_sems.at[a, k], recv_sem=recv_sems.at[a, k], device_id=(tx, ty, c), device_id_type=MESH)
                cp.start()
                copies.append(cp)
        for cp in copies:
            cp.wait()
        for own in local:
            own.wait()

    return pl.pallas_call(
        body, name="grad_chip_exchange",
        out_shape=[jax.ShapeDtypeStruct(v.shape, v.dtype) for v in sums],
        in_specs=[ANY] * n, out_specs=[ANY] * n,
        scratch_shapes=[pltpu.SemaphoreType.DMA((n, 3)), pltpu.SemaphoreType.DMA((n, 3)), pltpu.SemaphoreType.DMA((n,))],
    )(*sums)


def _adamw_math(w, g, m, v):
    m_new = ADAM_B1 * m + (1.0 - ADAM_B1) * g
    v_new = ADAM_B2 * v + (1.0 - ADAM_B2) * (g * g)
    m_hat = m_new / (1.0 - ADAM_B1 ** ADAM_STEP)
    v_hat = v_new / (1.0 - ADAM_B2 ** ADAM_STEP)
    delta = -ADAM_LR * (m_hat / (jnp.sqrt(v_hat) + ADAM_EPS) + ADAM_WD * w)
    return delta, m_new, v_new


def _adamw(slots, w, m, v, name):
    k, r, cdim = slots.shape
    tr = _grad_row_tile(r)

    def body(s_ref, w_ref, m_ref, v_ref, g_ref, d_ref, mo_ref, vo_ref):
        g = s_ref[0].astype(F32)
        for q in range(1, k):
            g = g + s_ref[q].astype(F32)
        g_ref[...] = g
        d_ref[...], mo_ref[...], vo_ref[...] = _adamw_math(w_ref[...], g, m_ref[...], v_ref[...])

    blk = pl.BlockSpec((tr, cdim), lambda i: (i, 0))
    return pl.pallas_call(
        body, name=name, grid=(r // tr,), out_shape=[jax.ShapeDtypeStruct((r, cdim), F32)] * 4,
        in_specs=[pl.BlockSpec((k, tr, cdim), lambda i: (0, i, 0)), blk, blk, blk], out_specs=[blk] * 4,
        compiler_params=_cparams("arbitrary"),
    )(slots, w, m, v)


def _stack_cols(g):
    n, r, c = g.shape
    return g.transpose(1, 0, 2).reshape(r, n * c)


def _split_cols(w):
    r, nc = w.shape
    return w.reshape(r, N_DEV, nc // N_DEV).transpose(1, 0, 2)


def _rope_tables(positions):
    inv_freq = ROPE_THETA ** (-jnp.arange(0, MLA_ROPE, 2, dtype=F32) / MLA_ROPE)
    ang = positions.astype(F32).reshape(-1, 1) * inv_freq[None, :]
    cos, sin, zero = jnp.cos(ang), jnp.sin(ang), jnp.zeros_like(ang)
    reps = LANES // MLA_ROPE
    return (jnp.tile(jnp.concatenate([cos, cos], axis=1), (1, reps)),
            jnp.tile(jnp.concatenate([-sin, zero], axis=1), (1, reps)),
            jnp.tile(jnp.concatenate([zero, sin], axis=1), (1, reps)))


def _local_step(x, positions, loss_target, gains, g_in, g_uq, g_ukv, g_o, g_gate, g_up, g_down):
    norm_mix, q_norm, kv_norm, out_mla, out_sb, norm_ffn, norm_final = gains
    d = x.shape[1]
    w_in = _stack_cols(g_in)
    w_a = jnp.concatenate([w_in[:, :416], jnp.zeros((d, 96), BF16), w_in[:, 416:]], axis=1)
    w_uq = jnp.concatenate([g_uq[:, :, :MLA_NOPE].transpose(1, 0, 2).reshape(Q_LORA, -1),
                            g_uq[:, :, MLA_NOPE:].transpose(1, 0, 2).reshape(Q_LORA, -1)], axis=1)
    w_ukv = jnp.concatenate([g_ukv[:, :, :MLA_NOPE].transpose(1, 0, 2).reshape(KV_LORA, -1),
                             g_ukv[:, :, MLA_NOPE:].transpose(1, 0, 2).reshape(KV_LORA, -1)], axis=1)
    w_o = g_o.reshape(-1, d)
    w_gate, w_up = _stack_cols(g_gate), _stack_cols(g_up)
    w_down = g_down.reshape(-1, d)
    cos, sin_a, sin_b = _rope_tables(positions)

    u, cq, ckv, cqn, ckvn, qn, qr, kv, kr, qkv_sb = _proj_in_fwd(x, norm_mix, w_a, q_norm, w_uq, kv_norm, w_ukv, cos, sin_a, sin_b)
    o_mla, lse = _mla_fwd(qn, qr, kv, kr)
    o_sb, tot = _sb_fwd(qkv_sb)
    merged, h1, f = _attn_out_fwd(o_mla, o_sb, out_mla, out_sb, w_o, x, norm_ffn)
    gate, up, h2 = _ffn_fwd(f, h1, w_gate, w_up, w_down)
    loss, dh2, dh2b, dg_final = _final_loss(h2, loss_target, norm_final.reshape(1, d))

    dgate, dup, act, dh1, dh1b, dg_ffn = _ffn_bwd(dh2, dh2b, gate, up, h1, norm_ffn, w_down.T, w_gate.T, w_up.T)
    dw_down = _tn_matmul(act, dh2b, "dw_down")
    dw_gate = _tn_matmul(f, dgate, "dw_gate")
    dw_up = _tn_matmul(f, dup, "dw_up")
    do_mla, do_sb, dg_mla, dg_sb = _attn_out_bwd(dh1b, w_o.T, o_mla, o_sb, out_mla, out_sb)
    dw_o = _tn_matmul(merged, dh1b, "dw_o")
    dq_sb, dk_sb, dv_sb = _sb_bwd(qkv_sb, do_sb, tot)
    dqn, dqr, dkn, dv, dkr = _mla_bwd(qn, qr, kv, kr, do_mla, o_mla, lse)
    dx, dproj, dq, dkv, dg_q, dg_kv, dg_mix = _proj_in_bwd(
        dqn, dqr, dkn, dv, dkr, dq_sb, dk_sb, dv_sb, cq, ckv, x, dh1, cos, sin_a, sin_b,
        q_norm, kv_norm, norm_mix, w_uq.T, w_ukv.T, w_a.T)
    dw_a = _tn_matmul(u, dproj, "dw_in")
    dw_uq = _tn_matmul(cqn, dq, "dw_uq")
    dw_ukv = _tn_matmul(ckvn, dkv, "dw_ukv")

    p_in = _split_cols(jnp.concatenate([dw_a[:, :416], dw_a[:, 512:]], axis=1))
    p_uq = jnp.concatenate([dw_uq[:, :512].reshape(Q_LORA, MLA_HEADS, MLA_NOPE),
                            dw_uq[:, 512:].reshape(Q_LORA, MLA_HEADS, MLA_ROPE)], axis=2).transpose(1, 0, 2)
    p_ukv = jnp.concatenate([dw_ukv[:, :512].reshape(KV_LORA, MLA_HEADS, MLA_NOPE),
                             dw_ukv[:, 512:].reshape(KV_LORA, MLA_HEADS, HEAD_DIM)], axis=2).transpose(1, 0, 2)
    p_o = dw_o.reshape(N_DEV, -1, d)
    p_down = dw_down.reshape(N_DEV, -1, d)
    parts = [p_in, p_uq, p_ukv, p_o, _split_cols(dw_gate), _split_cols(dw_up), p_down]
    gain_grads = [dg_mix, dg_q, dg_kv, dg_mla, dg_sb, dg_ffn, dg_final]
    return loss, dx, parts, gain_grads


def kernel(x, positions, norm_mix, w_in, q_latent_norm, w_uq, kv_latent_norm, w_ukv, out_norm_mla, out_norm_sb, w_o, norm_ffn, w_gate, w_up, w_down, norm_final, loss_target, m_norm_mix, m_w_in, m_q_latent_norm, m_w_uq, m_kv_latent_norm, m_w_ukv, m_out_norm_mla, m_out_norm_sb, m_w_o, m_norm_ffn, m_w_gate, m_w_up, m_w_down, m_norm_final, v_norm_mix, v_w_in, v_q_latent_norm, v_w_uq, v_kv_latent_norm, v_w_ukv, v_out_norm_mla, v_out_norm_sb, v_w_o, v_norm_ffn, v_w_gate, v_w_up, v_w_down, v_norm_final):
    mats = [w_in, w_uq, w_ukv, w_o, w_gate, w_up, w_down]
    mat_m = [m_w_in, m_w_uq, m_w_ukv, m_w_o, m_w_gate, m_w_up, m_w_down]
    mat_v = [v_w_in, v_w_uq, v_w_ukv, v_w_o, v_w_gate, v_w_up, v_w_down]
    mat_names = ["w_in", "w_uq", "w_ukv", "w_o", "w_gate", "w_up", "w_down"]
    gains = [norm_mix, q_latent_norm, kv_latent_norm, out_norm_mla, out_norm_sb, norm_ffn, norm_final]
    gain_m = [m_norm_mix, m_q_latent_norm, m_kv_latent_norm, m_out_norm_mla, m_out_norm_sb, m_norm_ffn, m_norm_final]
    gain_v = [v_norm_mix, v_q_latent_norm, v_kv_latent_norm, v_out_norm_mla, v_out_norm_sb, v_norm_ffn, v_norm_final]

    gathered = _all_gather([w[0].astype(BF16) for w in mats], "weight_all_gather")

    gains2d = [g.reshape(1, -1) for g in gains]
    loss_part, dx, parts, gain_grads = _local_step(x[0], positions[0], loss_target[0], gains2d, *gathered)

    from_sibling = _sibling_exchange(parts)
    chip_sums = [_chip_sum(p, r, "chip_sum_" + nm) for p, r, nm in zip(parts, from_sibling, mat_names)]
    slots = _chip_exchange(chip_sums)
    mat_out = [_adamw(sl, w[0], m[0], v[0], "adamw_" + nm)
               for sl, w, m, v, nm in zip(slots, mats, mat_m, mat_v, mat_names)]

    sizes = [g.size for g in gains]
    used = sum(sizes) + LANES
    rows = -(-used // (8 * LANES)) * 8

    def pack(vals, tail):
        flat = jnp.concatenate([v.reshape(-1) for v in vals] + [tail])
        return jnp.pad(flat, (0, rows * LANES - flat.size)).reshape(rows, LANES)

    zeros_tail = jnp.zeros((LANES,), F32)
    small = _all_gather([pack(gain_grads, loss_part.reshape(-1))], "gain_all_gather")[0]
    g_s, d_s, m_s, v_s = _adamw(small, pack(gains, zeros_tail), pack(gain_m, zeros_tail), pack(gain_v, zeros_tail), "adamw_gains")

    def unpack(packed):
        flat = packed.reshape(-1)
        outs, off = [], 0
        for g, n in zip(gains, sizes):
            outs.append(flat[off:off + n].reshape(g.shape))
            off += n
        return outs

    loss = g_s.reshape(-1)[sum(sizes)]

    order = ["norm_mix", "w_in", "q_latent_norm", "w_uq", "kv_latent_norm", "w_ukv", "out_norm_mla", "out_norm_sb",
             "w_o", "norm_ffn", "w_gate", "w_up", "w_down", "norm_final"]
    gain_names = ["norm_mix", "q_latent_norm", "kv_latent_norm", "out_norm_mla", "out_norm_sb", "norm_ffn", "norm_final"]
    result = [loss, dx[None]]
    for kind in range(4):
        small_parts = dict(zip(gain_names, unpack([g_s, d_s, m_s, v_s][kind])))
        mat_parts = {nm: out[kind][None] for nm, out in zip(mat_names, mat_out)}
        result += [small_parts[nm] if nm in small_parts else mat_parts[nm] for nm in order]
    return tuple(result)
```

```python
import functools
import math

import jax
import jax.numpy as jnp
from jax import lax
from jax.experimental import pallas as pl
from jax.experimental.pallas import tpu as pltpu

F32 = jnp.float32
BF16 = jnp.bfloat16
MESH = pl.DeviceIdType.MESH

EPS = 1e-6
ROPE_THETA = 10000.0
MLA_HEADS = 8
MLA_NOPE = 64
MLA_ROPE = 32
SB_HEADS = 8
HEAD_DIM = 64
Q_LORA = 256
KV_LORA = 128
MLA_SCALE = 1.0 / math.sqrt(MLA_NOPE + MLA_ROPE)
SB_SCALE = 1.0 / math.sqrt(HEAD_DIM)
LOG2E = math.log2(math.e)
N_DEV = 8

ADAM_LR = 0.001
ADAM_B1 = 0.9
ADAM_B2 = 0.999
ADAM_EPS = 1e-08
ADAM_WD = 0.01
ADAM_STEP = 10

LANES = 128
ATT_TILE = 512
TRI = 256
ROW_TILE = 512
FFN_BWD_ROW_TILE = 256
PROJ_BWD_ROW_TILE = 256
TN_BLOCK = 256
TN_RESIDENT_BYTES = 16 * 1024 * 1024
VMEM_LIMIT = 56 * 1024 * 1024
NEG = -1e30


def _cparams(*sem):
    return pltpu.CompilerParams(dimension_semantics=sem, vmem_limit_bytes=VMEM_LIMIT)


def _dot(a, b):
    return jnp.dot(a, b, preferred_element_type=F32)


def _dot_nt(a, b):
    return lax.dot_general(a, b, (((1,), (1,)), ((), ())), preferred_element_type=F32)


def _dot_tn(a, b):
    return lax.dot_general(a, b, (((0,), (0,)), ((), ())), preferred_element_type=F32)


def _rms(x, g):
    r = lax.rsqrt(jnp.mean(x * x, axis=-1, keepdims=True) + EPS)
    return x * r * g


def _rms_bwd(x, g, dy):
    r = lax.rsqrt(jnp.mean(x * x, axis=-1, keepdims=True) + EPS)
    n = x * r
    dn = dy * g
    dx = r * (dn - n * jnp.mean(dn * n, axis=-1, keepdims=True))
    return dx, jnp.sum(dy * n, axis=0, keepdims=True)


def _rope(x, cos, sin_a, sin_b):
    return x * cos + pltpu.roll(x, 112, 1) * sin_a + pltpu.roll(x, 16, 1) * sin_b


def _rope_t(g, cos, sin_a, sin_b):
    return g * cos + pltpu.roll(g * sin_a, 16, 1) + pltpu.roll(g * sin_b, 112, 1)


def _row_spec(tm, width):
    return pl.BlockSpec((tm, width), lambda r: (r, 0))


def _full_spec(shape):
    return pl.BlockSpec(shape, lambda *_: (0,) * len(shape))


def _accumulate(ref, val, first):
    @pl.when(first)
    def _():
        ref[...] = val

    @pl.when(jnp.logical_not(first))
    def _():
        ref[...] += val


def _proj_in_fwd(x, g_mix, w_a, g_q, w_uq, g_kv, w_ukv, cos, sin_a, sin_b):
    s, d = x.shape
    tm = min(ROW_TILE, s)

    def body(x_ref, gm_ref, wa_ref, gq_ref, wuq_ref, gkv_ref, wukv_ref, cos_ref, sa_ref, sb_ref,
             u_ref, cq_ref, ckv_ref, cqn_ref, ckvn_ref, qn_ref, qr_ref, kv_ref, kr_ref, sbq_ref):
        u = _rms(x_ref[...], gm_ref[...]).astype(BF16)
        u_ref[...] = u
        cq = _dot(u, wa_ref[:, 0:256])
        ckv = _dot(u, wa_ref[:, 256:384])
        kr = _dot(u, wa_ref[:, 384:512])
        cq_ref[...] = cq
        ckv_ref[...] = ckv
        cqn = _rms(cq, gq_ref[...]).astype(BF16)
        ckvn = _rms(ckv, gkv_ref[...]).astype(BF16)
        cqn_ref[...] = cqn
        ckvn_ref[...] = ckvn
        cos_t, sa_t, sb_t = cos_ref[...], sa_ref[...], sb_ref[...]
        qn_ref[...] = (_dot(cqn, wuq_ref[:, 0:512]) * MLA_SCALE).astype(BF16)
        for half in range(2):
            lo = 512 + half * LANES
            qr = _dot(cqn, wuq_ref[:, lo:lo + LANES])
            qr_ref[:, half * LANES:(half + 1) * LANES] = (_rope(qr, cos_t, sa_t, sb_t) * MLA_SCALE).astype(BF16)
        kv_ref[...] = _dot(ckvn, wukv_ref[...]).astype(BF16)
        krt = kr + pltpu.roll(kr, 32, 1) + pltpu.roll(kr, 64, 1) + pltpu.roll(kr, 96, 1)
        kr_ref[...] = _rope(krt, cos_t, sa_t, sb_t).astype(BF16)
        sbq_ref[:, 0:512] = (_dot(u, wa_ref[:, 512:1024]) * (SB_SCALE * LOG2E)).astype(BF16)
        sbq_ref[:, 512:1536] = _dot(u, wa_ref[:, 1024:2048]).astype(BF16)

    outs = [
        jax.ShapeDtypeStruct((s, d), BF16),
        jax.ShapeDtypeStruct((s, 256), F32),
        jax.ShapeDtypeStruct((s, 128), F32),
        jax.ShapeDtypeStruct((s, 256), BF16),
        jax.ShapeDtypeStruct((s, 128), BF16),
        jax.ShapeDtypeStruct((s, 512), BF16),
        jax.ShapeDtypeStruct((s, 256), BF16),
        jax.ShapeDtypeStruct((s, 1024), BF16),
        jax.ShapeDtypeStruct((s, 128), BF16),
        jax.ShapeDtypeStruct((s, 1536), BF16),
    ]
    return pl.pallas_call(
        body, name="proj_in_fwd", grid=(s // tm,), out_shape=outs,
        in_specs=[_row_spec(tm, d), _full_spec(g_mix.shape), _full_spec(w_a.shape), _full_spec(g_q.shape),
                  _full_spec(w_uq.shape), _full_spec(g_kv.shape), _full_spec(w_ukv.shape),
                  _row_spec(tm, LANES), _row_spec(tm, LANES), _row_spec(tm, LANES)],
        out_specs=[_row_spec(tm, o.shape[1]) for o in outs],
        compiler_params=_cparams("arbitrary"),
    )(x, g_mix, w_a, g_q, w_uq, g_kv, w_ukv, cos, sin_a, sin_b)


def _attn_out_fwd(o_mla, o_sb, g_mla, g_sb, w_o, x, g_ffn):
    s, d = x.shape
    tm = min(ROW_TILE, s)

    def body(oa_ref, ob_ref, ga_ref, gb_ref, wo_ref, x_ref, gf_ref, merged_ref, h1_ref, f_ref):
        na = _rms(oa_ref[...], ga_ref[...]).astype(BF16)
        nb = _rms(ob_ref[...], gb_ref[...]).astype(BF16)
        merged_ref[:, 0:512] = na
        merged_ref[:, 512:1024] = nb
        h1 = x_ref[...] + _dot(na, wo_ref[0:512, :]) + _dot(nb, wo_ref[512:1024, :])
        h1_ref[...] = h1
        f_ref[...] = _rms(h1, gf_ref[...]).astype(BF16)

    outs = [jax.ShapeDtypeStruct((s, d), BF16), jax.ShapeDtypeStruct((s, d), F32), jax.ShapeDtypeStruct((s, d), BF16)]
    return pl.pallas_call(
        body, name="attn_out_fwd", grid=(s // tm,), out_shape=outs,
        in_specs=[_row_spec(tm, 512), _row_spec(tm, 512), _full_spec(g_mla.shape), _full_spec(g_sb.shape),
                  _full_spec(w_o.shape), _row_spec(tm, d), _full_spec(g_ffn.shape)],
        out_specs=[_row_spec(tm, d)] * 3,
        compiler_params=_cparams("arbitrary"),
    )(o_mla, o_sb, g_mla, g_sb, w_o, x, g_ffn)


def _ffn_tile(d_ff):
    return d_ff // 2 if (d_ff // 2) % LANES == 0 else d_ff


def _ffn_fwd(f, h1, w_gate, w_up, w_down):
    s, d = h1.shape
    d_ff = w_gate.shape[1]
    tm = min(ROW_TILE, s)
    tf = _ffn_tile(d_ff)

    def body(f_ref, h1_ref, wg_ref, wu_ref, wd_ref, gate_ref, up_ref, h2_ref):
        j = pl.program_id(1)
        fb = f_ref[...]
        gate = _dot(fb, wg_ref[...])
        up = _dot(fb, wu_ref[...])
        gate_ref[...] = gate.astype(BF16)
        up_ref[...] = up.astype(BF16)
        act = (gate * jax.nn.sigmoid(gate) * up).astype(BF16)
        part = _dot(act, wd_ref[...])

        @pl.when(j == 0)
        def _():
            h2_ref[...] = h1_ref[...] + part

        @pl.when(j != 0)
        def _():
            h2_ref[...] += part

    outs = [jax.ShapeDtypeStruct((s, d_ff), BF16), jax.ShapeDtypeStruct((s, d_ff), BF16), jax.ShapeDtypeStruct((s, d), F32)]
    return pl.pallas_call(
        body, name="ffn_fwd", grid=(s // tm, d_ff // tf), out_shape=outs,
        in_specs=[pl.BlockSpec((tm, d), lambda r, j: (r, 0)), pl.BlockSpec((tm, d), lambda r, j: (r, 0)),
                  pl.BlockSpec((d, tf), lambda r, j: (0, j)), pl.BlockSpec((d, tf), lambda r, j: (0, j)),
                  pl.BlockSpec((tf, d), lambda r, j: (j, 0))],
        out_specs=[pl.BlockSpec((tm, tf), lambda r, j: (r, j)), pl.BlockSpec((tm, tf), lambda r, j: (r, j)),
                   pl.BlockSpec((tm, d), lambda r, j: (r, 0))],
        compiler_params=_cparams("arbitrary", "arbitrary"),
    )(f, h1, w_gate, w_up, w_down)


def _final_loss(h2, target, g_final):
    s, d = h2.shape
    tm = min(ROW_TILE, s)

    def body(h2_ref, t_ref, g_ref, loss_ref, dh2_ref, dh2b_ref, dg_ref):
        first = pl.program_id(0) == 0
        h2v = h2_ref[...]
        g = g_ref[...]
        diff = _rms(h2v, g) - t_ref[...]
        part = 0.5 * jnp.sum(jnp.mean(diff * diff, axis=-1, keepdims=True), axis=0, keepdims=True)
        _accumulate(loss_ref, jnp.broadcast_to(part, loss_ref.shape), first)
        dx, dg = _rms_bwd(h2v, g, diff * (1.0 / d))
        dh2_ref[...] = dx
        dh2b_ref[...] = dx.astype(BF16)
        _accumulate(dg_ref, dg, first)

    outs = [jax.ShapeDtypeStruct((1, LANES), F32), jax.ShapeDtypeStruct((s, d), F32), jax.ShapeDtypeStruct((s, d), BF16),
            jax.ShapeDtypeStruct((1, d), F32)]
    return pl.pallas_call(
        body, name="final_loss", grid=(s // tm,), out_shape=outs,
        in_specs=[_row_spec(tm, d), _row_spec(tm, d), _full_spec((1, d))],
        out_specs=[_full_spec((1, LANES)), _row_spec(tm, d), _row_spec(tm, d), _full_spec((1, d))],
        compiler_params=_cparams("arbitrary"),
    )(h2, target, g_final)


def _tile_iotas(t):
    return lax.broadcasted_iota(jnp.int32, (t, t), 0), lax.broadcasted_iota(jnp.int32, (t, t), 1)


def _stacked_mask(t, strict):
    row = lax.broadcasted_iota(jnp.int32, (2 * t, t), 0)
    col = lax.broadcasted_iota(jnp.int32, (2 * t, t), 1)
    row = jnp.where(row >= t, row - t, row)
    return col < row if strict else col <= row


def _mla_fwd(qn, qr, kv, kr, shards):
    s = qn.shape[0]
    t = min(ATT_TILE, s)
    pairs = MLA_HEADS // 2
    nq = s // t
    n = len(shards)

    def body(*refs):
        qn_ref, qr_ref, kn_ref, v_ref, kr_ref = refs[:5]
        o_ref, lse_ref = refs[5 + n:7 + n]
        qcat_ref, m_ref, l_ref, acc_ref = refs[7 + 2 * n:11 + 2 * n]
        hp, i = pl.program_id(0), pl.program_id(1)
        ride = _Exchange(True, refs[5:5 + n], refs[7 + n:7 + 2 * n], *refs[11 + 2 * n:])

        @pl.when((hp == 0) & (i == 0))
        def _():
            ride.start()

        lane = lax.broadcasted_iota(jnp.int32, (1, LANES), 1)
        row, col = _tile_iotas(t)
        causal = col <= row
        q_pair, q_quad = qn_ref[...], qr_ref[...]
        zero = jnp.zeros_like(q_pair)
        for hh in range(2):
            in_head = (lane // HEAD_DIM) == hh
            in_rope = (lane // MLA_ROPE) == (hp % 2) * 2 + hh
            qcat_ref[hh * t:(hh + 1) * t, 0:LANES] = jnp.where(in_head, q_pair, zero)
            qcat_ref[hh * t:(hh + 1) * t, LANES:2 * LANES] = jnp.where(in_rope, q_quad, zero)
        m_ref[...] = jnp.full_like(m_ref, NEG)
        l_ref[...] = jnp.zeros_like(l_ref)
        acc_ref[...] = jnp.zeros_like(acc_ref)

        def tile(j, masked):
            rows = pl.ds(pl.multiple_of(j * t, t), t)
            kcat = jnp.concatenate([kn_ref[rows, :], kr_ref[rows, :]], axis=1)
            v_ones = jnp.concatenate([v_ref[rows, :], jnp.ones((t, LANES), BF16)], axis=1)
            scores = [_dot_nt(qcat_ref[hh * t:(hh + 1) * t, :], kcat) for hh in range(2)]
            for hh in range(2):
                half = slice(hh * t, (hh + 1) * t)
                sc = jnp.where(causal, scores[hh], NEG) if masked else scores[hh]
                m = m_ref[half, :]
                m_new = jnp.maximum(m, jnp.max(sc, axis=-1, keepdims=True))
                alpha = jnp.exp(m - m_new)
                p = jnp.exp(sc - jnp.concatenate([m_new] * (t // LANES), axis=1))
                pv = _dot(p.astype(BF16), v_ones)
                l_ref[half, :] = alpha * l_ref[half, :] + pv[:, LANES:]
                acc_ref[half, :] = alpha * acc_ref[half, :] + pv[:, :LANES]
                m_ref[half, :] = m_new

        tile(i, True)

        def step(j, carry):
            tile(j, False)
            return carry

        lax.fori_loop(0, i, step, 0)
        first = (lane // HEAD_DIM) == 0
        o = acc_ref[...] / l_ref[...]
        lse = m_ref[...] + jnp.log(l_ref[...])
        o_ref[...] = jnp.where(first, o[0:t], o[t:2 * t])
        lse_ref[...] = jnp.where(first, lse[0:t], lse[t:2 * t])

        @pl.when((hp == pairs - 1) & (i == nq - 1))
        def _():
            ride.finish()

    gathered_shapes, sems = _exchange_shapes(True, shards)
    outs = [jax.ShapeDtypeStruct((s, 512), F32), jax.ShapeDtypeStruct((pairs, s, LANES), F32)] + gathered_shapes
    res = pl.pallas_call(
        body, name="mla_fwd", grid=(pairs, nq), out_shape=outs,
        in_specs=[pl.BlockSpec((t, LANES), lambda hp, i: (i, hp)), pl.BlockSpec((t, LANES), lambda hp, i: (i, hp // 2)),
                  pl.BlockSpec((s, LANES), lambda hp, i: (0, hp)), pl.BlockSpec((s, LANES), lambda hp, i: (0, 4 + hp)),
                  pl.BlockSpec((s, LANES), lambda hp, i: (0, 0))] + [ANY] * n,
        out_specs=[pl.BlockSpec((t, LANES), lambda hp, i: (i, hp)), pl.BlockSpec((None, t, LANES), lambda hp, i: (hp, i, 0))]
        + [ANY] * n,
        scratch_shapes=[pltpu.VMEM((2 * t, 2 * LANES), BF16), pltpu.VMEM((2 * t, LANES), F32), pltpu.VMEM((2 * t, LANES), F32),
                        pltpu.VMEM((2 * t, LANES), F32)] + sems,
        compiler_params=_cparams("arbitrary", "arbitrary"),
    )(qn, qr, kv, kv, kr, *shards)
    return res[0], res[1], res[2:]


HEADS = (0, 1)


def _sb_logs(z2, strict, masked):
    log_b = jnp.minimum(z2, 0.0) - jnp.log2(1.0 + jnp.exp2(-jnp.abs(z2)))
    log_1m = log_b - z2
    if masked:
        log_1m = jnp.where(strict, log_1m, 0.0)
    return log_1m, log_b


def _block_totals(x):
    t, w = x.shape
    nb = max(w // TRI, 1)
    bw = w // nb
    blocks = [x[:, b * bw:(b + 1) * bw] for b in range(nb)]
    totals = [jnp.broadcast_to(jnp.sum(blk, axis=-1, keepdims=True), (t, LANES)) for blk in blocks]
    whole = totals[0]
    for tot in totals[1:]:
        whole = whole + tot
    return blocks, totals, whole


def _running_sums(blocks, totals, tri, carry, suffix):
    nb = len(blocks)
    reps = blocks[0].shape[1] // LANES
    outs = [None] * nb
    run = carry
    for b in (range(nb - 1, -1, -1) if suffix else range(nb)):
        outs[b] = _dot(blocks[b].astype(BF16), tri) + jnp.concatenate([run] * reps, axis=1)
        run = run + totals[b]
    return outs[0] if nb == 1 else jnp.concatenate(outs, axis=1)


def _tri(t, rel):
    n = min(TRI, t)
    row, col = _tile_iotas(n)
    return rel(row, col).astype(BF16)


def _sb_fwd(qkv):
    s = qkv.shape[0]
    t = min(ATT_TILE, s)
    pairs = SB_HEADS // 2

    def body(q_ref, k_ref, v_ref, o_ref, tot_ref, qm_ref, right_ref, acc_ref):
        i = pl.program_id(1)
        lane = lax.broadcasted_iota(jnp.int32, (1, LANES), 1)
        row, col = _tile_iotas(t)
        strict = col < row
        t_suffix = _tri(t, lambda r, c: r > c)
        q_pair = q_ref[...]
        for hh in range(2):
            qm_ref[hh] = jnp.where((lane // HEAD_DIM) == hh, q_pair, jnp.zeros_like(q_pair))
        right_ref[...] = jnp.zeros_like(right_ref)
        acc_ref[...] = jnp.zeros_like(acc_ref)

        def tile(j, masked):
            rows = pl.ds(pl.multiple_of(j * t, t), t)
            k, v = k_ref[rows, :], v_ref[rows, :]
            for hh in HEADS:
                log_1m, log_b = _sb_logs(_dot_nt(qm_ref[hh], k), strict, masked)
                blocks, totals, whole = _block_totals(log_1m)
                a = jnp.exp2(log_b + _running_sums(blocks, totals, t_suffix, right_ref[hh], True))
                if masked:
                    a = jnp.where(strict, a, 0.0)
                right_ref[hh] += whole
                acc_ref[hh] += _dot(a.astype(BF16), v)

        tile(i, True)

        def step(n, carry):
            tile(i - 1 - n, False)
            return carry

        lax.fori_loop(0, i, step, 0)
        first = (lane // HEAD_DIM) == 0
        o_ref[...] = jnp.where(first, acc_ref[0], acc_ref[1])
        tot_ref[...] = jnp.where(first, right_ref[0], right_ref[1])

    outs = [jax.ShapeDtypeStruct((s, 512), F32), jax.ShapeDtypeStruct((pairs, s, LANES), F32)]
    return pl.pallas_call(
        body, name="sb_fwd", grid=(pairs, s // t), out_shape=outs,
        in_specs=[pl.BlockSpec((t, LANES), lambda hp, i: (i, hp)), pl.BlockSpec((s, LANES), lambda hp, i: (0, 4 + hp)),
                  pl.BlockSpec((s, LANES), lambda hp, i: (0, 8 + hp))],
        out_specs=[pl.BlockSpec((t, LANES), lambda hp, i: (i, hp)), pl.BlockSpec((None, t, LANES), lambda hp, i: (hp, i, 0))],
        scratch_shapes=[pltpu.VMEM((2, t, LANES), BF16), pltpu.VMEM((2, t, LANES), F32), pltpu.VMEM((2, t, LANES), F32)],
        compiler_params=_cparams("arbitrary", "arbitrary"),
    )(qkv, qkv, qkv)


def _sb_bwd(qkv, do, tot):
    s = qkv.shape[0]
    t = min(ATT_TILE, s)
    pairs = SB_HEADS // 2

    def body(q_ref, k_ref, v_ref, do_ref, tot_ref, dq_ref, dk_ref, dv_ref,
             qm_ref, dob_ref, total_s, left_l, left_g, dq_s):
        i = pl.program_id(1)

        @pl.when(i == 0)
        def _():
            dk_ref[...] = jnp.zeros_like(dk_ref)
            dv_ref[...] = jnp.zeros_like(dv_ref)

        lane = lax.broadcasted_iota(jnp.int32, (1, LANES), 1)
        row, col = _tile_iotas(t)
        strict = col < row
        t_suffix = _tri(t, lambda r, c: r > c)
        t_excl = _tri(t, lambda r, c: r < c)
        q_pair, do_pair, tot_pair = q_ref[...], do_ref[...], tot_ref[...]
        for hh in range(2):
            in_head = (lane // HEAD_DIM) == hh
            qm_ref[hh] = jnp.where(in_head, q_pair, jnp.zeros_like(q_pair))
            dob_ref[hh] = jnp.where(in_head, do_pair, 0.0).astype(BF16)
            total_s[hh] = jnp.broadcast_to(
                jnp.sum(jnp.where(lane == hh * HEAD_DIM, tot_pair, 0.0), axis=-1, keepdims=True), (t, LANES))
        left_l[...] = jnp.zeros_like(left_l)
        left_g[...] = jnp.zeros_like(left_g)
        dq_s[...] = jnp.zeros_like(dq_s)
        reps = t // LANES

        def tile(j, masked):
            rows = pl.ds(pl.multiple_of(j * t, t), t)
            k, v = k_ref[rows, :], v_ref[rows, :]
            z2 = [_dot_nt(qm_ref[hh], k) for hh in HEADS]
            d_a = [_dot_nt(dob_ref[hh], v) for hh in HEADS]
            for hh in HEADS:
                qm, dob = qm_ref[hh], dob_ref[hh]
                log_1m, log_b = _sb_logs(z2[hh], strict, masked)
                blocks, totals, whole = _block_totals(log_1m)
                done = left_l[hh] + whole
                left_l[hh] = done
                a = jnp.exp2(log_b + _running_sums(blocks, totals, t_suffix, total_s[hh] - done, True))
                if masked:
                    a = jnp.where(strict, a, 0.0)
                g = a * d_a[hh]
                blocks, totals, whole = _block_totals(g)
                before = _running_sums(blocks, totals, t_excl, left_g[hh], False)
                left_g[hh] += whole
                dz = g - jnp.exp2(log_b) * (g + before)
                if masked:
                    dz = jnp.where(strict, dz, 0.0)
                dzb = dz.astype(BF16)
                dq_s[hh] += _dot(dzb, k)
                dk_ref[rows, :] += _dot_tn(dzb, qm)
                dv_ref[rows, :] += _dot_tn(a.astype(BF16), dob)

        def step(j, carry):
            tile(j, False)
            return carry

        lax.fori_loop(0, i, step, 0)
        tile(i, True)
        dq_ref[...] = jnp.where((lane // HEAD_DIM) == 0, dq_s[0], dq_s[1])

        @pl.when(i == s // t - 1)
        def _():
            dk_ref[...] *= 1.0 / LOG2E

    outs = [jax.ShapeDtypeStruct((s, 512), F32)] * 3
    return pl.pallas_call(
        body, name="sb_bwd", grid=(pairs, s // t), out_shape=outs,
        in_specs=[pl.BlockSpec((t, LANES), lambda hp, i: (i, hp)), pl.BlockSpec((s, LANES), lambda hp, i: (0, 4 + hp)),
                  pl.BlockSpec((s, LANES), lambda hp, i: (0, 8 + hp)), pl.BlockSpec((t, LANES), lambda hp, i: (i, hp)),
                  pl.BlockSpec((None, t, LANES), lambda hp, i: (hp, i, 0))],
        out_specs=[pl.BlockSpec((t, LANES), lambda hp, i: (i, hp)), pl.BlockSpec((s, LANES), lambda hp, i: (0, hp)),
                   pl.BlockSpec((s, LANES), lambda hp, i: (0, hp))],
        scratch_shapes=[pltpu.VMEM((2, t, LANES), BF16), pltpu.VMEM((2, t, LANES), BF16)]
        + [pltpu.VMEM((2, t, LANES), F32)] * 4,
        compiler_params=_cparams("arbitrary", "arbitrary"),
    )(qkv, qkv, qkv, do, tot)


def _mla_bwd(qn, qr, kv, kr, do, o, lse, parts):
    s = qn.shape[0]
    t = min(ATT_TILE, s)
    pairs = MLA_HEADS // 2
    nq = s // t
    n = len(parts)

    def body(*refs):
        qn_ref, qr_ref, kn_ref, v_ref, kr_ref, do_ref, o_ref, lse_ref = refs[:8]
        dqn_ref, dqr_ref, dkn_ref, dv_ref, dkr_ref = refs[8 + n:13 + n]
        qcat_ref, dob_ref, lse_s, delta_s, dq_s = refs[13 + 2 * n:18 + 2 * n]
        hp, i = pl.program_id(0), pl.program_id(1)
        ride = _Exchange(False, refs[8:8 + n], refs[13 + n:13 + 2 * n], *refs[18 + 2 * n:])

        @pl.when((hp == 0) & (i == 0))
        def _():
            ride.start()

        @pl.when(i == 0)
        def _():
            dkn_ref[...] = jnp.zeros_like(dkn_ref)
            dv_ref[...] = jnp.zeros_like(dv_ref)
            dkr_ref[...] = jnp.zeros_like(dkr_ref)

        lane = lax.broadcasted_iota(jnp.int32, (1, LANES), 1)
        row, col = _tile_iotas(t)
        causal = col <= row
        q_pair, q_quad, do_pair, lse_pair = qn_ref[...], qr_ref[...], do_ref[...], lse_ref[...]
        do_o = do_pair * o_ref[...]
        zero = jnp.zeros_like(q_pair)
        ropes = []
        for hh in range(2):
            in_head = (lane // HEAD_DIM) == hh
            in_rope = (lane // MLA_ROPE) == (hp % 2) * 2 + hh
            ropes.append(in_rope)
            qcat_ref[hh, :, 0:LANES] = jnp.where(in_head, q_pair, zero)
            qcat_ref[hh, :, LANES:2 * LANES] = jnp.where(in_rope, q_quad, zero)
            dob_ref[hh] = jnp.where(in_head, do_pair, 0.0).astype(BF16)
            delta_s[hh] = jnp.broadcast_to(jnp.sum(jnp.where(in_head, do_o, 0.0), axis=-1, keepdims=True), (t, LANES))
            lse_s[hh] = jnp.broadcast_to(
                jnp.sum(jnp.where(lane == hh * HEAD_DIM, lse_pair, 0.0), axis=-1, keepdims=True), (t, LANES))
        dq_s[...] = jnp.zeros_like(dq_s)
        reps = t // LANES

        def tile(j, masked):
            rows = pl.ds(pl.multiple_of(j * t, t), t)
            kcat = jnp.concatenate([kn_ref[rows, :], kr_ref[rows, :]], axis=1)
            v = v_ref[rows, :]
            sc = [_dot_nt(qcat_ref[hh], kcat) for hh in HEADS]
            dp = [_dot_nt(dob_ref[hh], v) for hh in HEADS]
            p = [jnp.exp(sc[hh] - jnp.concatenate([lse_s[hh]] * reps, axis=1)) for hh in HEADS]
            if masked:
                p = [jnp.where(causal, p[hh], 0.0) for hh in HEADS]
            ds = [(p[hh] * (dp[hh] - jnp.concatenate([delta_s[hh]] * reps, axis=1))).astype(BF16) for hh in HEADS]
            for hh in HEADS:
                dq_s[hh] += _dot(ds[hh], kcat)
            dkcat = _dot_tn(ds[0], qcat_ref[0]) + _dot_tn(ds[1], qcat_ref[1])
            dkn_ref[rows, :] += dkcat[:, 0:LANES]
            dkr_ref[rows, :] += dkcat[:, LANES:2 * LANES]
            dv_ref[rows, :] += _dot_tn(p[0].astype(BF16), dob_ref[0]) + _dot_tn(p[1].astype(BF16), dob_ref[1])

        def step(j, carry):
            tile(j, False)
            return carry

        lax.fori_loop(0, i, step, 0)
        tile(i, True)
        dqn_ref[...] = jnp.where((lane // HEAD_DIM) == 0, dq_s[0, :, 0:LANES], dq_s[1, :, 0:LANES])
        dqr_ref[...] = (jnp.where(ropes[0], dq_s[0, :, LANES:2 * LANES], 0.0)
                        + jnp.where(ropes[1], dq_s[1, :, LANES:2 * LANES], 0.0))

        @pl.when((hp == pairs - 1) & (i == nq - 1))
        def _():
            ride.finish()

    pair_block = pl.BlockSpec((t, LANES), lambda hp, i: (i, hp))
    landed_shapes, sems = _exchange_shapes(False, parts)
    outs = [jax.ShapeDtypeStruct((s, 512), F32), jax.ShapeDtypeStruct((pairs, s, LANES), F32),
            jax.ShapeDtypeStruct((s, 512), F32), jax.ShapeDtypeStruct((s, 512), F32),
            jax.ShapeDtypeStruct((pairs, s, LANES), F32)] + landed_shapes
    res = pl.pallas_call(
        body, name="mla_bwd", grid=(pairs, nq), out_shape=outs,
        in_specs=[pair_block, pl.BlockSpec((t, LANES), lambda hp, i: (i, hp // 2)),
                  pl.BlockSpec((s, LANES), lambda hp, i: (0, hp)), pl.BlockSpec((s, LANES), lambda hp, i: (0, 4 + hp)),
                  pl.BlockSpec((s, LANES), lambda hp, i: (0, 0)), pair_block, pair_block,
                  pl.BlockSpec((None, t, LANES), lambda hp, i: (hp, i, 0))] + [ANY] * n,
        out_specs=[pair_block, pl.BlockSpec((None, t, LANES), lambda hp, i: (hp, i, 0)),
                   pl.BlockSpec((s, LANES), lambda hp, i: (0, hp)), pl.BlockSpec((s, LANES), lambda hp, i: (0, hp)),
                   pl.BlockSpec((None, s, LANES), lambda hp, i: (hp, 0, 0))] + [ANY] * n,
        scratch_shapes=[pltpu.VMEM((2, t, 2 * LANES), BF16), pltpu.VMEM((2, t, LANES), BF16), pltpu.VMEM((2, t, LANES), F32),
                        pltpu.VMEM((2, t, LANES), F32), pltpu.VMEM((2, t, 2 * LANES), F32)] + sems,
        compiler_params=_cparams("arbitrary", "arbitrary"),
    )(qn, qr, kv, kv, kr, do, o, lse, *parts)
    return res[:5], res[5:]


def _ffn_bwd(dh2, dh2b, gate, up, h1, g_ffn, w_down_t, w_gate_t, w_up_t):
    s, d = h1.shape
    d_ff = gate.shape[1]
    tm = min(FFN_BWD_ROW_TILE, s)
    tf = _ffn_tile(d_ff)

    def act_body(dh2b_ref, gate_ref, up_ref, wdt_ref, dgate_ref, dup_ref, act_ref):
        dact = _dot(dh2b_ref[...], wdt_ref[...])
        gate_v = gate_ref[...].astype(F32)
        up_v = up_ref[...].astype(F32)
        sig = jax.nn.sigmoid(gate_v)
        silu = gate_v * sig
        dup_ref[...] = (dact * silu).astype(BF16)
        dgate_ref[...] = (dact * up_v * (sig * (1.0 + gate_v * (1.0 - sig)))).astype(BF16)
        act_ref[...] = (silu * up_v).astype(BF16)

    ff = pl.BlockSpec((tm, tf), lambda j, r: (r, j))
    dgate, dup, act = pl.pallas_call(
        act_body, name="ffn_bwd_act", grid=(d_ff // tf, s // tm), out_shape=[jax.ShapeDtypeStruct((s, d_ff), BF16)] * 3,
        in_specs=[pl.BlockSpec((tm, d), lambda j, r: (r, 0)), ff, ff, pl.BlockSpec((d, tf), lambda j, r: (0, j))],
        out_specs=[ff, ff, ff],
        compiler_params=_cparams("arbitrary", "arbitrary"),
    )(dh2b, gate, up, w_down_t)

    def df_body(dgate_ref, dup_ref, dh2_ref, h1_ref, g_ref, wgt_ref, wut_ref, dh1_ref, dh1b_ref, dg_ref):
        df = _dot(dgate_ref[...], wgt_ref[...]) + _dot(dup_ref[...], wut_ref[...])
        dx, dg = _rms_bwd(h1_ref[...], g_ref[...], df)
        dh1 = dh2_ref[...] + dx
        dh1_ref[...] = dh1
        dh1b_ref[...] = dh1.astype(BF16)
        _accumulate(dg_ref, dg, pl.program_id(0) == 0)

    outs = [jax.ShapeDtypeStruct((s, d), F32), jax.ShapeDtypeStruct((s, d), BF16), jax.ShapeDtypeStruct((1, d), F32)]
    dh1, dh1b, dg = pl.pallas_call(
        df_body, name="ffn_bwd_df", grid=(s // tm,), out_shape=outs,
        in_specs=[_row_spec(tm, d_ff), _row_spec(tm, d_ff), _row_spec(tm, d), _row_spec(tm, d), _full_spec((1, d)),
                  _full_spec(w_gate_t.shape), _full_spec(w_up_t.shape)],
        out_specs=[_row_spec(tm, d), _row_spec(tm, d), _full_spec((1, d))],
        compiler_params=_cparams("arbitrary"),
    )(dgate, dup, dh2, h1, g_ffn, w_gate_t, w_up_t)
    return dgate, dup, act, dh1, dh1b, dg


def _largest_tile(n, cap):
    for cand in range(cap, 0, -LANES):
        if n % cand == 0:
            return cand
    return n


def _tn_matmul(a, b, name):
    assert a.dtype == BF16 and b.dtype == BF16
    s, m = a.shape
    n = b.shape[1]
    if s * m * 2 <= TN_RESIDENT_BYTES:
        tm, tn = m, min(n, TN_BLOCK)
    else:
        tm, tn = TN_BLOCK, n

    def body(a_ref, b_ref, o_ref):
        o_ref[...] = _dot_tn(a_ref[...], b_ref[...])

    return pl.pallas_call(
        body, name=name, grid=(m // tm, n // tn), out_shape=jax.ShapeDtypeStruct((m, n), F32),
        in_specs=[pl.BlockSpec((s, tm), lambda i, j: (0, i)), pl.BlockSpec((s, tn), lambda i, j: (0, j))],
        out_specs=pl.BlockSpec((tm, tn), lambda i, j: (i, j)),
        compiler_params=_cparams("arbitrary", "arbitrary"),
    )(a, b)


def _attn_out_bwd(dh1, w_o_t, o_mla, o_sb, g_mla, g_sb):
    s, d = dh1.shape
    tm = min(ROW_TILE, s)

    def body(dh1_ref, wot_ref, oa_ref, ob_ref, ga_ref, gb_ref, doa_ref, dob_ref, dga_ref, dgb_ref):
        first = pl.program_id(0) == 0
        dh1b = dh1_ref[...]
        dxa, dga = _rms_bwd(oa_ref[...], ga_ref[...], _dot(dh1b, wot_ref[:, 0:512]))
        dxb, dgb = _rms_bwd(ob_ref[...], gb_ref[...], _dot(dh1b, wot_ref[:, 512:1024]))
        doa_ref[...] = dxa
        dob_ref[...] = dxb
        _accumulate(dga_ref, dga, first)
        _accumulate(dgb_ref, dgb, first)

    outs = [jax.ShapeDtypeStruct((s, 512), F32)] * 2 + [jax.ShapeDtypeStruct((1, 512), F32)] * 2
    return pl.pallas_call(
        body, name="attn_out_bwd", grid=(s // tm,), out_shape=outs,
        in_specs=[_row_spec(tm, d), _full_spec(w_o_t.shape), _row_spec(tm, 512), _row_spec(tm, 512),
                  _full_spec((1, 512)), _full_spec((1, 512))],
        out_specs=[_row_spec(tm, 512), _row_spec(tm, 512), _full_spec((1, 512)), _full_spec((1, 512))],
        compiler_params=_cparams("arbitrary"),
    )(dh1, w_o_t, o_mla, o_sb, g_mla, g_sb)


def _proj_in_bwd(dqn, dqr, dkn, dv, dkr, dq_sb, dk_sb, dv_sb, cq, ckv, x, dh1, cos, sin_a, sin_b,
                 g_q, g_kv, g_mix, w_uq_t, w_ukv_t, w_a_t):
    s, d = x.shape
    tm = min(PROJ_BWD_ROW_TILE, s)

    def body(dqn_ref, dqr_ref, dkn_ref, dv_ref, dkr_ref, dqs_ref, dks_ref, dvs_ref, cq_ref, ckv_ref, x_ref, dh1_ref,
             cos_ref, sa_ref, sb_ref, gq_ref, gkv_ref, gm_ref, wuqt_ref, wukvt_ref, wat_ref,
             dx_ref, dproj_ref, dq_ref, dkv_ref, dgq_ref, dgkv_ref, dgm_ref):
        first = pl.program_id(0) == 0
        lane = lax.broadcasted_iota(jnp.int32, (1, LANES), 1)
        cos_t, sa_t, sb_t = cos_ref[...], sa_ref[...], sb_ref[...]
        dq_ref[:, 0:512] = (dqn_ref[...] * MLA_SCALE).astype(BF16)
        for half in range(2):
            quad = (dqr_ref[2 * half] + dqr_ref[2 * half + 1]) * MLA_SCALE
            dq_ref[:, 512 + half * LANES:512 + (half + 1) * LANES] = _rope_t(quad, cos_t, sa_t, sb_t).astype(BF16)
        dcq, dgq = _rms_bwd(cq_ref[...], gq_ref[...], _dot(dq_ref[...], wuqt_ref[...]))
        _accumulate(dgq_ref, dgq, first)
        dkv_ref[:, 0:512] = dkn_ref[...].astype(BF16)
        dkv_ref[:, 512:1024] = dv_ref[...].astype(BF16)
        dckv, dgkv = _rms_bwd(ckv_ref[...], gkv_ref[...], _dot(dkv_ref[...], wukvt_ref[...]))
        _accumulate(dgkv_ref, dgkv, first)
        g = _rope_t(dkr_ref[0] + dkr_ref[1] + dkr_ref[2] + dkr_ref[3], cos_t, sa_t, sb_t)
        g = g + pltpu.roll(g, 96, 1) + pltpu.roll(g, 64, 1) + pltpu.roll(g, 32, 1)
        dproj_ref[:, 0:256] = dcq.astype(BF16)
        dproj_ref[:, 256:384] = dckv.astype(BF16)
        dproj_ref[:, 384:512] = jnp.where(lane < MLA_ROPE, g, 0.0).astype(BF16)
        dproj_ref[:, 512:1024] = (dqs_ref[...] * SB_SCALE).astype(BF16)
        dproj_ref[:, 1024:1536] = dks_ref[...].astype(BF16)
        dproj_ref[:, 1536:2048] = dvs_ref[...].astype(BF16)
        dxn, dgm = _rms_bwd(x_ref[...], gm_ref[...], _dot(dproj_ref[...], wat_ref[...]))
        dx_ref[...] = dh1_ref[...] + dxn
        _accumulate(dgm_ref, dgm, first)

    quad_spec = pl.BlockSpec((4, tm, LANES), lambda r: (0, r, 0))
    outs = [jax.ShapeDtypeStruct((s, d), F32), jax.ShapeDtypeStruct((s, 2048), BF16), jax.ShapeDtypeStruct((s, 768), BF16),
            jax.ShapeDtypeStruct((s, 1024), BF16), jax.ShapeDtypeStruct((1, 256), F32), jax.ShapeDtypeStruct((1, 128), F32),
            jax.ShapeDtypeStruct((1, d), F32)]
    return pl.pallas_call(
        body, name="proj_in_bwd", grid=(s // tm,), out_shape=outs,
        in_specs=[_row_spec(tm, 512), quad_spec, _row_spec(tm, 512), _row_spec(tm, 512), quad_spec,
                  _row_spec(tm, 512), _row_spec(tm, 512), _row_spec(tm, 512), _row_spec(tm, 256), _row_spec(tm, 128),
                  _row_spec(tm, d), _row_spec(tm, d), _row_spec(tm, LANES), _row_spec(tm, LANES), _row_spec(tm, LANES),
                  _full_spec((1, 256)), _full_spec((1, 128)), _full_spec((1, d)),
                  _full_spec(w_uq_t.shape), _full_spec(w_ukv_t.shape), _full_spec(w_a_t.shape)],
        out_specs=[_row_spec(tm, d), _row_spec(tm, 2048), _row_spec(tm, 768), _row_spec(tm, 1024),
                   _full_spec((1, 256)), _full_spec((1, 128)), _full_spec((1, d))],
        compiler_params=_cparams("arbitrary"),
    )(dqn, dqr, dkn, dv, dkr, dq_sb, dk_sb, dv_sb, cq, ckv, x, dh1, cos, sin_a, sin_b, g_q, g_kv, g_mix,
      w_uq_t, w_ukv_t, w_a_t)


ANY = pl.BlockSpec(memory_space=pl.ANY)


def _place():
    return lax.axis_index("x"), lax.axis_index("y"), lax.axis_index("c")


def _all_gather(shards, name):
    n = len(shards)

    def body(*refs):
        ins, outs = refs[:n], refs[n:2 * n]
        send_sems, recv_sems, local_sems = refs[2 * n:]
        x, y, c = _place()
        me, sibling = (x, y, c), (x, y, 1 - c)
        chips = [(1 - x, y), (x, 1 - y), (1 - x, 1 - y)]

        def slot(a, px, py, pc):
            return outs[a].at[4 * px + 2 * py + pc]

        def copy(a, k, block, to, src=None):
            return pltpu.make_async_remote_copy(
                src_ref=slot(a, *block) if src is None else src, dst_ref=slot(a, *block),
                send_sem=send_sems.at[a, k], recv_sem=recv_sems.at[a, k], device_id=to, device_id_type=MESH)

        mine, first, passed = [], [], []
        for a in range(n):
            own = pltpu.make_async_copy(ins[a], slot(a, *me), local_sems.at[a])
            own.start()
            mine.append(own)
            cps = [copy(a, 0, me, sibling, src=ins[a])]
            cps += [copy(a, 1 + j, me, (*chip, c), src=ins[a]) for j, chip in enumerate(chips)]
            for cp in cps:
                cp.start()
            first += cps
        for a in range(n):
            for j, chip in enumerate(chips):
                copy(a, 1 + j, (*chip, c), me).wait_recv()
                fwd = copy(a, 4 + j, (*chip, c), sibling)
                fwd.start()
                passed.append(fwd)
        for a in range(n):
            copy(a, 0, sibling, me).wait_recv()
            for j, chip in enumerate(chips):
                copy(a, 4 + j, (*chip, 1 - c), me).wait_recv()
        for cp in first + passed:
            cp.wait_send()
        for own in mine:
            own.wait()

    return pl.pallas_call(
        body, name=name,
        out_shape=[jax.ShapeDtypeStruct((N_DEV,) + v.shape, v.dtype) for v in shards],
        in_specs=[ANY] * n, out_specs=[ANY] * n,
        scratch_shapes=[pltpu.SemaphoreType.DMA((n, 7)), pltpu.SemaphoreType.DMA((n, 7)), pltpu.SemaphoreType.DMA((n,))],
    )(*shards)


class _Exchange:
    def __init__(self, gather, ins, outs, send_sems, recv_sems, local_sems):
        self.gather, self.ins, self.outs = gather, ins, outs
        self.sems = (send_sems, recv_sems, local_sems)
        x, y, c = _place()
        self.me = 4 * x + 2 * y + c
        self.peers = []
        for k in range(1, N_DEV):
            px = 1 - x if k & 4 else x
            py = 1 - y if k & 2 else y
            pc = 1 - c if k & 1 else c
            self.peers.append(((px, py, pc), 4 * px + 2 * py + pc))

    def _remote(self, a, k, landing):
        send_sems, recv_sems, _ = self.sems
        where, number = self.peers[k]
        src = self.ins[a] if self.gather else self.ins[a].at[number]
        return pltpu.make_async_remote_copy(
            src_ref=src, dst_ref=self.outs[a].at[landing], send_sem=send_sems.at[a, k], recv_sem=recv_sems.at[a, k],
            device_id=where, device_id_type=MESH)

    def _local(self, a):
        src = self.ins[a] if self.gather else self.ins[a].at[self.me]
        return pltpu.make_async_copy(src, self.outs[a].at[self.me], self.sems[2].at[a])

    def start(self):
        for a in range(len(self.ins)):
            self._local(a).start()
            for k in range(N_DEV - 1):
                self._remote(a, k, self.me).start()

    def finish(self):
        for a in range(len(self.ins)):
            for k in range(N_DEV - 1):
                self._remote(a, k, self.peers[k][1]).wait_recv()
            for k in range(N_DEV - 1):
                self._remote(a, k, self.me).wait_send()
            self._local(a).wait()


def _exchange_shapes(gather, arrays):
    out_shape = [jax.ShapeDtypeStruct(((N_DEV,) + v.shape) if gather else v.shape, v.dtype) for v in arrays]
    n = len(arrays)
    sems = [pltpu.SemaphoreType.DMA((n, N_DEV - 1)), pltpu.SemaphoreType.DMA((n, N_DEV - 1)), pltpu.SemaphoreType.DMA((n,))]
    return out_shape, sems


def _exchange(gather, arrays, name):
    n = len(arrays)

    def body(*refs):
        ex = _Exchange(gather, refs[:n], refs[n:2 * n], *refs[2 * n:])
        ex.start()
        ex.finish()

    out_shape, sems = _exchange_shapes(gather, arrays)
    return pl.pallas_call(body, name=name, out_shape=out_shape, in_specs=[ANY] * n, out_specs=[ANY] * n,
                          scratch_shapes=sems)(*arrays)


def _grad_row_tile(rows):
    return _largest_tile_rows(rows, 256)


def _largest_tile_rows(rows, cap):
    for cand in range(cap, 0, -8):
        if rows % cand == 0:
            return cand
    return rows


def _adamw_math(w, g, m, v):
    m_new = ADAM_B1 * m + (1.0 - ADAM_B1) * g
    v_new = ADAM_B2 * v + (1.0 - ADAM_B2) * (g * g)
    m_hat = m_new / (1.0 - ADAM_B1 ** ADAM_STEP)
    v_hat = v_new / (1.0 - ADAM_B2 ** ADAM_STEP)
    delta = -ADAM_LR * (m_hat / (jnp.sqrt(v_hat) + ADAM_EPS) + ADAM_WD * w)
    return delta, m_new, v_new


def _adamw(slots, w, m, v, name):
    k, r, cdim = slots.shape
    tr = _grad_row_tile(r)

    def body(s_ref, w_ref, m_ref, v_ref, g_ref, d_ref, mo_ref, vo_ref):
        g = s_ref[0].astype(F32)
        for q in range(1, k):
            g = g + s_ref[q].astype(F32)
        g_ref[...] = g
        d_ref[...], mo_ref[...], vo_ref[...] = _adamw_math(w_ref[...], g, m_ref[...], v_ref[...])

    blk = pl.BlockSpec((tr, cdim), lambda i: (i, 0))
    return pl.pallas_call(
        body, name=name, grid=(r // tr,), out_shape=[jax.ShapeDtypeStruct((r, cdim), F32)] * 4,
        in_specs=[pl.BlockSpec((k, tr, cdim), lambda i: (0, i, 0)), blk, blk, blk], out_specs=[blk] * 4,
        compiler_params=_cparams("arbitrary"),
    )(slots, w, m, v)


def _stack_cols(g):
    n, r, c = g.shape
    return g.transpose(1, 0, 2).reshape(r, n * c)


def _split_cols(w):
    r, nc = w.shape
    return w.reshape(r, N_DEV, nc // N_DEV).transpose(1, 0, 2)


def _rope_tables(positions):
    inv_freq = ROPE_THETA ** (-jnp.arange(0, MLA_ROPE, 2, dtype=F32) / MLA_ROPE)
    ang = positions.astype(F32).reshape(-1, 1) * inv_freq[None, :]
    cos, sin, zero = jnp.cos(ang), jnp.sin(ang), jnp.zeros_like(ang)
    reps = LANES // MLA_ROPE
    return (jnp.tile(jnp.concatenate([cos, cos], axis=1), (1, reps)),
            jnp.tile(jnp.concatenate([-sin, zero], axis=1), (1, reps)),
            jnp.tile(jnp.concatenate([zero, sin], axis=1), (1, reps)))


def _local_step(x, positions, loss_target, gains, g_in, g_uq, g_ukv, late_shards):
    norm_mix, q_norm, kv_norm, out_mla, out_sb, norm_ffn, norm_final = gains
    d = x.shape[1]
    w_in = _stack_cols(g_in)
    w_a = jnp.concatenate([w_in[:, :416], jnp.zeros((d, 96), BF16), w_in[:, 416:]], axis=1)
    w_uq = jnp.concatenate([g_uq[:, :, :MLA_NOPE].transpose(1, 0, 2).reshape(Q_LORA, -1),
                            g_uq[:, :, MLA_NOPE:].transpose(1, 0, 2).reshape(Q_LORA, -1)], axis=1)
    w_ukv = jnp.concatenate([g_ukv[:, :, :MLA_NOPE].transpose(1, 0, 2).reshape(KV_LORA, -1),
                             g_ukv[:, :, MLA_NOPE:].transpose(1, 0, 2).reshape(KV_LORA, -1)], axis=1)
    cos, sin_a, sin_b = _rope_tables(positions)

    u, cq, ckv, cqn, ckvn, qn, qr, kv, kr, qkv_sb = _proj_in_fwd(x, norm_mix, w_a, q_norm, w_uq, kv_norm, w_ukv, cos, sin_a, sin_b)
    o_mla, lse, (g_o, g_gate, g_up, g_down) = _mla_fwd(qn, qr, kv, kr, late_shards)
    w_o = g_o.reshape(-1, d)
    w_gate, w_up = _stack_cols(g_gate), _stack_cols(g_up)
    w_down = g_down.reshape(-1, d)
    o_sb, tot = _sb_fwd(qkv_sb)
    merged, h1, f = _attn_out_fwd(o_mla, o_sb, out_mla, out_sb, w_o, x, norm_ffn)
    gate, up, h2 = _ffn_fwd(f, h1, w_gate, w_up, w_down)
    loss, dh2, dh2b, dg_final = _final_loss(h2, loss_target, norm_final.reshape(1, d))

    dgate, dup, act, dh1, dh1b, dg_ffn = _ffn_bwd(dh2, dh2b, gate, up, h1, norm_ffn, w_down.T, w_gate.T, w_up.T)
    dw_down = _tn_matmul(act, dh2b, "dw_down")
    dw_gate = _tn_matmul(f, dgate, "dw_gate")
    dw_up = _tn_matmul(f, dup, "dw_up")
    do_mla, do_sb, dg_mla, dg_sb = _attn_out_bwd(dh1b, w_o.T, o_mla, o_sb, out_mla, out_sb)
    dw_o = _tn_matmul(merged, dh1b, "dw_o")
    dq_sb, dk_sb, dv_sb = _sb_bwd(qkv_sb, do_sb, tot)
    early = [dw_o.reshape(N_DEV, -1, d), _split_cols(dw_gate), _split_cols(dw_up), dw_down.reshape(N_DEV, -1, d)]
    (dqn, dqr, dkn, dv, dkr), landed = _mla_bwd(qn, qr, kv, kr, do_mla, o_mla, lse, [p.astype(BF16) for p in early])
    dx, dproj, dq, dkv, dg_q, dg_kv, dg_mix = _proj_in_bwd(
        dqn, dqr, dkn, dv, dkr, dq_sb, dk_sb, dv_sb, cq, ckv, x, dh1, cos, sin_a, sin_b,
        q_norm, kv_norm, norm_mix, w_uq.T, w_ukv.T, w_a.T)
    dw_a = _tn_matmul(u, dproj, "dw_in")
    dw_uq = _tn_matmul(cqn, dq, "dw_uq")
    dw_ukv = _tn_matmul(ckvn, dkv, "dw_ukv")

    p_in = _split_cols(jnp.concatenate([dw_a[:, :416], dw_a[:, 512:]], axis=1))
    p_uq = jnp.concatenate([dw_uq[:, :512].reshape(Q_LORA, MLA_HEADS, MLA_NOPE),
                            dw_uq[:, 512:].reshape(Q_LORA, MLA_HEADS, MLA_ROPE)], axis=2).transpose(1, 0, 2)
    p_ukv = jnp.concatenate([dw_ukv[:, :512].reshape(KV_LORA, MLA_HEADS, MLA_NOPE),
                             dw_ukv[:, 512:].reshape(KV_LORA, MLA_HEADS, HEAD_DIM)], axis=2).transpose(1, 0, 2)
    late = [p.astype(BF16) for p in (p_in, p_uq, p_ukv)]
    gain_grads = [dg_mix, dg_q, dg_kv, dg_mla, dg_sb, dg_ffn, dg_final]
    return loss, dx, list(landed), late, gain_grads


def kernel(x, positions, norm_mix, w_in, q_latent_norm, w_uq, kv_latent_norm, w_ukv, out_norm_mla, out_norm_sb, w_o, norm_ffn, w_gate, w_up, w_down, norm_final, loss_target, m_norm_mix, m_w_in, m_q_latent_norm, m_w_uq, m_kv_latent_norm, m_w_ukv, m_out_norm_mla, m_out_norm_sb, m_w_o, m_norm_ffn, m_w_gate, m_w_up, m_w_down, m_norm_final, v_norm_mix, v_w_in, v_q_latent_norm, v_w_uq, v_kv_latent_norm, v_w_ukv, v_out_norm_mla, v_out_norm_sb, v_w_o, v_norm_ffn, v_w_gate, v_w_up, v_w_down, v_norm_final):
    mats = [w_in, w_uq, w_ukv, w_o, w_gate, w_up, w_down]
    mat_m = [m_w_in, m_w_uq, m_w_ukv, m_w_o, m_w_gate, m_w_up, m_w_down]
    mat_v = [v_w_in, v_w_uq, v_w_ukv, v_w_o, v_w_gate, v_w_up, v_w_down]
    mat_names = ["w_in", "w_uq", "w_ukv", "w_o", "w_gate", "w_up", "w_down"]
    gains = [norm_mix, q_latent_norm, kv_latent_norm, out_norm_mla, out_norm_sb, norm_ffn, norm_final]
    gain_m = [m_norm_mix, m_q_latent_norm, m_kv_latent_norm, m_out_norm_mla, m_out_norm_sb, m_norm_ffn, m_norm_final]
    gain_v = [v_norm_mix, v_q_latent_norm, v_kv_latent_norm, v_out_norm_mla, v_out_norm_sb, v_norm_ffn, v_norm_final]

    shards = [w[0].astype(BF16) for w in mats]
    g_in, g_uq, g_ukv = _all_gather(shards[:3], "weight_all_gather")

    gains2d = [g.reshape(1, -1) for g in gains]
    loss_part, dx, landed, late, gain_grads = _local_step(
        x[0], positions[0], loss_target[0], gains2d, g_in, g_uq, g_ukv, shards[3:])

    slots = list(_exchange(False, late, "grad_scatter")) + landed
    mat_out = [_adamw(sl, w[0], m[0], v[0], "adamw_" + nm)
               for sl, w, m, v, nm in zip(slots, mats, mat_m, mat_v, mat_names)]

    sizes = [g.size for g in gains]
    used = sum(sizes) + LANES
    rows = -(-used // (8 * LANES)) * 8

    def pack(vals, tail):
        flat = jnp.concatenate([v.reshape(-1) for v in vals] + [tail])
        return jnp.pad(flat, (0, rows * LANES - flat.size)).reshape(rows, LANES)

    zeros_tail = jnp.zeros((LANES,), F32)
    small = _all_gather([pack(gain_grads, loss_part.reshape(-1))], "gain_all_gather")[0]
    g_s, d_s, m_s, v_s = _adamw(small, pack(gains, zeros_tail), pack(gain_m, zeros_tail), pack(gain_v, zeros_tail), "adamw_gains")

    def unpack(packed):
        flat = packed.reshape(-1)
        outs, off = [], 0
        for g, n in zip(gains, sizes):
            outs.append(flat[off:off + n].reshape(g.shape))
            off += n
        return outs

    loss = g_s.reshape(-1)[sum(sizes)]

    order = ["norm_mix", "w_in", "q_latent_norm", "w_uq", "kv_latent_norm", "w_ukv", "out_norm_mla", "out_norm_sb",
             "w_o", "norm_ffn", "w_gate", "w_up", "w_down", "norm_final"]
    gain_names = ["norm_mix", "q_latent_norm", "kv_latent_norm", "out_norm_mla", "out_norm_sb", "norm_ffn", "norm_final"]
    result = [loss, dx[None]]
    for kind in range(4):
        small_parts = dict(zip(gain_names, unpack([g_s, d_s, m_s, v_s][kind])))
        mat_parts = {nm: out[kind][None] for nm, out in zip(mat_names, mat_out)}
        result += [small_parts[nm] if nm in small_parts else mat_parts[nm] for nm in order]
    return tuple(result)
```

```python
import functools
import math

import jax
import jax.numpy as jnp
from jax import lax
from jax.experimental import pallas as pl
from jax.experimental.pallas import tpu as pltpu

F32 = jnp.float32
BF16 = jnp.bfloat16
MESH = pl.DeviceIdType.MESH

EPS = 1e-6
ROPE_THETA = 10000.0
MLA_HEADS = 8
MLA_NOPE = 64
MLA_ROPE = 32
SB_HEADS = 8
HEAD_DIM = 64
Q_LORA = 256
KV_LORA = 128
MLA_SCALE = 1.0 / math.sqrt(MLA_NOPE + MLA_ROPE)
SB_SCALE = 1.0 / math.sqrt(HEAD_DIM)
LOG2E = math.log2(math.e)
SB_DEAD = -160.0
N_DEV = 8

ADAM_LR = 0.001
ADAM_B1 = 0.9
ADAM_B2 = 0.999
ADAM_EPS = 1e-08
ADAM_WD = 0.01
ADAM_STEP = 10

LANES = 128
ATT_TILE = 512
TRI = 256
ROW_TILE = 512
FFN_BWD_ROW_TILE = 256
PROJ_BWD_ROW_TILE = 256
TN_BLOCK = 256
TN_RESIDENT_BYTES = 16 * 1024 * 1024
VMEM_LIMIT = 56 * 1024 * 1024
NEG = -1e30


def _cparams(*sem):
    return pltpu.CompilerParams(dimension_semantics=sem, vmem_limit_bytes=VMEM_LIMIT)


def _dot(a, b):
    return jnp.dot(a, b, preferred_element_type=F32)


def _dot_nt(a, b):
    return lax.dot_general(a, b, (((1,), (1,)), ((), ())), preferred_element_type=F32)


def _dot_tn(a, b):
    return lax.dot_general(a, b, (((0,), (0,)), ((), ())), preferred_element_type=F32)


def _rms(x, g):
    r = lax.rsqrt(jnp.mean(x * x, axis=-1, keepdims=True) + EPS)
    return x * r * g


def _rms_bwd(x, g, dy):
    r = lax.rsqrt(jnp.mean(x * x, axis=-1, keepdims=True) + EPS)
    n = x * r
    dn = dy * g
    dx = r * (dn - n * jnp.mean(dn * n, axis=-1, keepdims=True))
    return dx, jnp.sum(dy * n, axis=0, keepdims=True)


def _rope(x, cos, sin_a, sin_b):
    return x * cos + pltpu.roll(x, 112, 1) * sin_a + pltpu.roll(x, 16, 1) * sin_b


def _rope_t(g, cos, sin_a, sin_b):
    return g * cos + pltpu.roll(g * sin_a, 16, 1) + pltpu.roll(g * sin_b, 112, 1)


def _row_spec(tm, width):
    return pl.BlockSpec((tm, width), lambda r: (r, 0))


def _full_spec(shape):
    return pl.BlockSpec(shape, lambda *_: (0,) * len(shape))


def _accumulate(ref, val, first):
    @pl.when(first)
    def _():
        ref[...] = val

    @pl.when(jnp.logical_not(first))
    def _():
        ref[...] += val


def _proj_in_fwd(x, g_mix, w_a, g_q, w_uq, g_kv, w_ukv, cos, sin_a, sin_b):
    s, d = x.shape
    tm = min(ROW_TILE, s)

    def body(x_ref, gm_ref, wa_ref, gq_ref, wuq_ref, gkv_ref, wukv_ref, cos_ref, sa_ref, sb_ref,
             u_ref, cq_ref, ckv_ref, cqn_ref, ckvn_ref, qn_ref, qr_ref, kv_ref, kr_ref, sbq_ref):
        u = _rms(x_ref[...], gm_ref[...]).astype(BF16)
        u_ref[...] = u
        cq = _dot(u, wa_ref[:, 0:256])
        ckv = _dot(u, wa_ref[:, 256:384])
        kr = _dot(u, wa_ref[:, 384:512])
        cq_ref[...] = cq
        ckv_ref[...] = ckv
        cqn = _rms(cq, gq_ref[...]).astype(BF16)
        ckvn = _rms(ckv, gkv_ref[...]).astype(BF16)
        cqn_ref[...] = cqn
        ckvn_ref[...] = ckvn
        cos_t, sa_t, sb_t = cos_ref[...], sa_ref[...], sb_ref[...]
        qn_ref[...] = (_dot(cqn, wuq_ref[:, 0:512]) * MLA_SCALE).astype(BF16)
        for half in range(2):
            lo = 512 + half * LANES
            qr = _dot(cqn, wuq_ref[:, lo:lo + LANES])
            qr_ref[:, half * LANES:(half + 1) * LANES] = (_rope(qr, cos_t, sa_t, sb_t) * MLA_SCALE).astype(BF16)
        kv_ref[...] = _dot(ckvn, wukv_ref[...]).astype(BF16)
        krt = kr + pltpu.roll(kr, 32, 1) + pltpu.roll(kr, 64, 1) + pltpu.roll(kr, 96, 1)
        kr_ref[...] = _rope(krt, cos_t, sa_t, sb_t).astype(BF16)
        sbq_ref[:, 0:512] = (_dot(u, wa_ref[:, 512:1024]) * (SB_SCALE * LOG2E)).astype(BF16)
        sbq_ref[:, 512:1536] = _dot(u, wa_ref[:, 1024:2048]).astype(BF16)

    outs = [
        jax.ShapeDtypeStruct((s, d), BF16),
        jax.ShapeDtypeStruct((s, 256), F32),
        jax.ShapeDtypeStruct((s, 128), F32),
        jax.ShapeDtypeStruct((s, 256), BF16),
        jax.ShapeDtypeStruct((s, 128), BF16),
        jax.ShapeDtypeStruct((s, 512), BF16),
        jax.ShapeDtypeStruct((s, 256), BF16),
        jax.ShapeDtypeStruct((s, 1024), BF16),
        jax.ShapeDtypeStruct((s, 128), BF16),
        jax.ShapeDtypeStruct((s, 1536), BF16),
    ]
    return pl.pallas_call(
        body, name="proj_in_fwd", grid=(s // tm,), out_shape=outs,
        in_specs=[_row_spec(tm, d), _full_spec(g_mix.shape), _full_spec(w_a.shape), _full_spec(g_q.shape),
                  _full_spec(w_uq.shape), _full_spec(g_kv.shape), _full_spec(w_ukv.shape),
                  _row_spec(tm, LANES), _row_spec(tm, LANES), _row_spec(tm, LANES)],
        out_specs=[_row_spec(tm, o.shape[1]) for o in outs],
        compiler_params=_cparams("arbitrary"),
    )(x, g_mix, w_a, g_q, w_uq, g_kv, w_ukv, cos, sin_a, sin_b)


def _attn_out_fwd(o_mla, o_sb, g_mla, g_sb, w_o, x, g_ffn):
    s, d = x.shape
    tm = min(ROW_TILE, s)

    def body(oa_ref, ob_ref, ga_ref, gb_ref, wo_ref, x_ref, gf_ref, merged_ref, h1_ref, f_ref):
        na = _rms(oa_ref[...], ga_ref[...]).astype(BF16)
        nb = _rms(ob_ref[...], gb_ref[...]).astype(BF16)
        merged_ref[:, 0:512] = na
        merged_ref[:, 512:1024] = nb
        h1 = x_ref[...] + _dot(na, wo_ref[0:512, :]) + _dot(nb, wo_ref[512:1024, :])
        h1_ref[...] = h1
        f_ref[...] = _rms(h1, gf_ref[...]).astype(BF16)

    outs = [jax.ShapeDtypeStruct((s, d), BF16), jax.ShapeDtypeStruct((s, d), F32), jax.ShapeDtypeStruct((s, d), BF16)]
    return pl.pallas_call(
        body, name="attn_out_fwd", grid=(s // tm,), out_shape=outs,
        in_specs=[_row_spec(tm, 512), _row_spec(tm, 512), _full_spec(g_mla.shape), _full_spec(g_sb.shape),
                  _full_spec(w_o.shape), _row_spec(tm, d), _full_spec(g_ffn.shape)],
        out_specs=[_row_spec(tm, d)] * 3,
        compiler_params=_cparams("arbitrary"),
    )(o_mla, o_sb, g_mla, g_sb, w_o, x, g_ffn)


def _ffn_tile(d_ff):
    return d_ff // 2 if (d_ff // 2) % LANES == 0 else d_ff


def _ffn_fwd(f, h1, w_gate, w_up, w_down):
    s, d = h1.shape
    d_ff = w_gate.shape[1]
    tm = min(ROW_TILE, s)
    tf = _ffn_tile(d_ff)

    def body(f_ref, h1_ref, wg_ref, wu_ref, wd_ref, gate_ref, up_ref, h2_ref):
        j = pl.program_id(1)
        fb = f_ref[...]
        gate = _dot(fb, wg_ref[...])
        up = _dot(fb, wu_ref[...])
        gate_ref[...] = gate.astype(BF16)
        up_ref[...] = up.astype(BF16)
        act = (gate * jax.nn.sigmoid(gate) * up).astype(BF16)
        part = _dot(act, wd_ref[...])

        @pl.when(j == 0)
        def _():
            h2_ref[...] = h1_ref[...] + part

        @pl.when(j != 0)
        def _():
            h2_ref[...] += part

    outs = [jax.ShapeDtypeStruct((s, d_ff), BF16), jax.ShapeDtypeStruct((s, d_ff), BF16), jax.ShapeDtypeStruct((s, d), F32)]
    return pl.pallas_call(
        body, name="ffn_fwd", grid=(s // tm, d_ff // tf), out_shape=outs,
        in_specs=[pl.BlockSpec((tm, d), lambda r, j: (r, 0)), pl.BlockSpec((tm, d), lambda r, j: (r, 0)),
                  pl.BlockSpec((d, tf), lambda r, j: (0, j)), pl.BlockSpec((d, tf), lambda r, j: (0, j)),
                  pl.BlockSpec((tf, d), lambda r, j: (j, 0))],
        out_specs=[pl.BlockSpec((tm, tf), lambda r, j: (r, j)), pl.BlockSpec((tm, tf), lambda r, j: (r, j)),
                   pl.BlockSpec((tm, d), lambda r, j: (r, 0))],
        compiler_params=_cparams("arbitrary", "arbitrary"),
    )(f, h1, w_gate, w_up, w_down)


def _final_loss(h2, target, g_final):
    s, d = h2.shape
    tm = min(ROW_TILE, s)

    def body(h2_ref, t_ref, g_ref, loss_ref, dh2_ref, dh2b_ref, dg_ref):
        first = pl.program_id(0) == 0
        h2v = h2_ref[...]
        g = g_ref[...]
        diff = _rms(h2v, g) - t_ref[...]
        part = 0.5 * jnp.sum(jnp.mean(diff * diff, axis=-1, keepdims=True), axis=0, keepdims=True)
        _accumulate(loss_ref, jnp.broadcast_to(part, loss_ref.shape), first)
        dx, dg = _rms_bwd(h2v, g, diff * (1.0 / d))
        dh2_ref[...] = dx
        dh2b_ref[...] = dx.astype(BF16)
        _accumulate(dg_ref, dg, first)

    outs = [jax.ShapeDtypeStruct((1, LANES), F32), jax.ShapeDtypeStruct((s, d), F32), jax.ShapeDtypeStruct((s, d), BF16),
            jax.ShapeDtypeStruct((1, d), F32)]
    return pl.pallas_call(
        body, name="final_loss", grid=(s // tm,), out_shape=outs,
        in_specs=[_row_spec(tm, d), _row_spec(tm, d), _full_spec((1, d))],
        out_specs=[_full_spec((1, LANES)), _row_spec(tm, d), _row_spec(tm, d), _full_spec((1, d))],
        compiler_params=_cparams("arbitrary"),
    )(h2, target, g_final)


def _tile_iotas(t):
    return lax.broadcasted_iota(jnp.int32, (t, t), 0), lax.broadcasted_iota(jnp.int32, (t, t), 1)


def _stacked_mask(t, strict):
    row = lax.broadcasted_iota(jnp.int32, (2 * t, t), 0)
    col = lax.broadcasted_iota(jnp.int32, (2 * t, t), 1)
    row = jnp.where(row >= t, row - t, row)
    return col < row if strict else col <= row


def _mla_fwd(qn, qr, kv, kr, shards):
    s = qn.shape[0]
    t = min(ATT_TILE, s)
    pairs = MLA_HEADS // 2
    nq = s // t
    n = len(shards)

    def body(*refs):
        qn_ref, qr_ref, kn_ref, v_ref, kr_ref = refs[:5]
        o_ref, lse_ref = refs[5 + n:7 + n]
        qcat_ref, m_ref, l_ref, acc_ref = refs[7 + 2 * n:11 + 2 * n]
        hp, i = pl.program_id(0), pl.program_id(1)
        ride = _Exchange(True, refs[5:5 + n], refs[7 + n:7 + 2 * n], *refs[11 + 2 * n:])

        @pl.when((hp == 0) & (i == 0))
        def _():
            ride.start()

        lane = lax.broadcasted_iota(jnp.int32, (1, LANES), 1)
        row, col = _tile_iotas(t)
        causal = col <= row
        q_pair, q_quad = qn_ref[...], qr_ref[...]
        zero = jnp.zeros_like(q_pair)
        for hh in range(2):
            in_head = (lane // HEAD_DIM) == hh
            in_rope = (lane // MLA_ROPE) == (hp % 2) * 2 + hh
            qcat_ref[hh * t:(hh + 1) * t, 0:LANES] = jnp.where(in_head, q_pair, zero)
            qcat_ref[hh * t:(hh + 1) * t, LANES:2 * LANES] = jnp.where(in_rope, q_quad, zero)
        m_ref[...] = jnp.full_like(m_ref, NEG)
        l_ref[...] = jnp.zeros_like(l_ref)
        acc_ref[...] = jnp.zeros_like(acc_ref)

        def tile(j, masked):
            rows = pl.ds(pl.multiple_of(j * t, t), t)
            kcat = jnp.concatenate([kn_ref[rows, :], kr_ref[rows, :]], axis=1)
            v_ones = jnp.concatenate([v_ref[rows, :], jnp.ones((t, LANES), BF16)], axis=1)
            scores = [_dot_nt(qcat_ref[hh * t:(hh + 1) * t, :], kcat) for hh in range(2)]
            for hh in range(2):
                half = slice(hh * t, (hh + 1) * t)
                sc = jnp.where(causal, scores[hh], NEG) if masked else scores[hh]
                m = m_ref[half, :]
                m_new = jnp.maximum(m, jnp.max(sc, axis=-1, keepdims=True))
                alpha = jnp.exp(m - m_new)
                p = jnp.exp(sc - jnp.concatenate([m_new] * (t // LANES), axis=1))
                pv = _dot(p.astype(BF16), v_ones)
                l_ref[half, :] = alpha * l_ref[half, :] + pv[:, LANES:]
                acc_ref[half, :] = alpha * acc_ref[half, :] + pv[:, :LANES]
                m_ref[half, :] = m_new

        tile(i, True)

        def step(j, carry):
            tile(j, False)
            return carry

        lax.fori_loop(0, i, step, 0)
        first = (lane // HEAD_DIM) == 0
        o = acc_ref[...] / l_ref[...]
        lse = m_ref[...] + jnp.log(l_ref[...])
        o_ref[...] = jnp.where(first, o[0:t], o[t:2 * t])
        lse_ref[...] = jnp.where(first, lse[0:t], lse[t:2 * t])

        @pl.when((hp == pairs - 1) & (i == nq - 1))
        def _():
            ride.finish()

    gathered_shapes, sems = _exchange_shapes(True, shards)
    outs = [jax.ShapeDtypeStruct((s, 512), F32), jax.ShapeDtypeStruct((pairs, s, LANES), F32)] + gathered_shapes
    res = pl.pallas_call(
        body, name="mla_fwd", grid=(pairs, nq), out_shape=outs,
        in_specs=[pl.BlockSpec((t, LANES), lambda hp, i: (i, hp)), pl.BlockSpec((t, LANES), lambda hp, i: (i, hp // 2)),
                  pl.BlockSpec((s, LANES), lambda hp, i: (0, hp)), pl.BlockSpec((s, LANES), lambda hp, i: (0, 4 + hp)),
                  pl.BlockSpec((s, LANES), lambda hp, i: (0, 0))] + [ANY] * n,
        out_specs=[pl.BlockSpec((t, LANES), lambda hp, i: (i, hp)), pl.BlockSpec((None, t, LANES), lambda hp, i: (hp, i, 0))]
        + [ANY] * n,
        scratch_shapes=[pltpu.VMEM((2 * t, 2 * LANES), BF16), pltpu.VMEM((2 * t, LANES), F32), pltpu.VMEM((2 * t, LANES), F32),
                        pltpu.VMEM((2 * t, LANES), F32)] + sems,
        compiler_params=_cparams("arbitrary", "arbitrary"),
    )(qn, qr, kv, kv, kr, *shards)
    return res[0], res[1], res[2:]


HEADS = (0, 1)


def _sb_logs(z2, strict, masked):
    log_b = jnp.minimum(z2, 0.0) - jnp.log2(1.0 + jnp.exp2(-jnp.abs(z2)))
    log_1m = log_b - z2
    if masked:
        log_1m = jnp.where(strict, log_1m, 0.0)
    return log_1m, log_b


def _block_totals(x):
    t, w = x.shape
    nb = max(w // TRI, 1)
    bw = w // nb
    blocks = [x[:, b * bw:(b + 1) * bw] for b in range(nb)]
    totals = [jnp.broadcast_to(jnp.sum(blk, axis=-1, keepdims=True), (t, LANES)) for blk in blocks]
    whole = totals[0]
    for tot in totals[1:]:
        whole = whole + tot
    return blocks, totals, whole


def _running_sums(blocks, totals, tri, carry, suffix):
    nb = len(blocks)
    reps = blocks[0].shape[1] // LANES
    outs = [None] * nb
    run = carry
    for b in (range(nb - 1, -1, -1) if suffix else range(nb)):
        outs[b] = _dot(blocks[b].astype(BF16), tri) + jnp.concatenate([run] * reps, axis=1)
        run = run + totals[b]
    return outs[0] if nb == 1 else jnp.concatenate(outs, axis=1)


def _tri(t, rel):
    n = min(TRI, t)
    row, col = _tile_iotas(n)
    return rel(row, col).astype(BF16)


def _sb_fwd(qkv):
    s = qkv.shape[0]
    t = min(ATT_TILE, s)
    pairs = SB_HEADS // 2

    def body(q_ref, k_ref, v_ref, o_ref, tot_ref, cnt_ref, qm_ref, right_ref, acc_ref):
        i = pl.program_id(1)
        lane = lax.broadcasted_iota(jnp.int32, (1, LANES), 1)
        row, col = _tile_iotas(t)
        strict = col < row
        t_suffix = _tri(t, lambda r, c: r > c)
        q_pair = q_ref[...]
        for hh in range(2):
            qm_ref[hh] = jnp.where((lane // HEAD_DIM) == hh, q_pair, jnp.zeros_like(q_pair))
        right_ref[...] = jnp.zeros_like(right_ref)
        acc_ref[...] = jnp.zeros_like(acc_ref)

        def tile(j, masked):
            rows = pl.ds(pl.multiple_of(j * t, t), t)
            k, v = k_ref[rows, :], v_ref[rows, :]
            for hh in HEADS:
                log_1m, log_b = _sb_logs(_dot_nt(qm_ref[hh], k), strict, masked)
                blocks, totals, whole = _block_totals(log_1m)
                a = jnp.exp2(log_b + _running_sums(blocks, totals, t_suffix, right_ref[hh], True))
                if masked:
                    a = jnp.where(strict, a, 0.0)
                right_ref[hh] += whole
                acc_ref[hh] += _dot(a.astype(BF16), v)

        tile(i, True)

        def alive(n):
            return (n < i) & (jnp.max(right_ref[...]) > SB_DEAD)

        def step(n):
            tile(i - 1 - n, False)
            return n + 1

        swept = lax.while_loop(alive, step, jnp.int32(0))
        cnt_ref[...] = jnp.full(cnt_ref.shape, swept.astype(F32))
        first = (lane // HEAD_DIM) == 0
        o_ref[...] = jnp.where(first, acc_ref[0], acc_ref[1])
        tot_ref[...] = jnp.where(first, right_ref[0], right_ref[1])

    outs = [jax.ShapeDtypeStruct((s, 512), F32), jax.ShapeDtypeStruct((pairs, s, LANES), F32),
            jax.ShapeDtypeStruct((pairs, s // t, 8, LANES), F32)]
    return pl.pallas_call(
        body, name="sb_fwd", grid=(pairs, s // t), out_shape=outs,
        in_specs=[pl.BlockSpec((t, LANES), lambda hp, i: (i, hp)), pl.BlockSpec((s, LANES), lambda hp, i: (0, 4 + hp)),
                  pl.BlockSpec((s, LANES), lambda hp, i: (0, 8 + hp))],
        out_specs=[pl.BlockSpec((t, LANES), lambda hp, i: (i, hp)), pl.BlockSpec((None, t, LANES), lambda hp, i: (hp, i, 0)),
                   pl.BlockSpec((None, None, 8, LANES), lambda hp, i: (hp, i, 0, 0))],
        scratch_shapes=[pltpu.VMEM((2, t, LANES), BF16), pltpu.VMEM((2, t, LANES), F32), pltpu.VMEM((2, t, LANES), F32)],
        compiler_params=_cparams("arbitrary", "arbitrary"),
    )(qkv, qkv, qkv)


def _sb_bwd(qkv, do, tot, cnt):
    s = qkv.shape[0]
    t = min(ATT_TILE, s)
    pairs = SB_HEADS // 2

    def body(q_ref, k_ref, v_ref, do_ref, tot_ref, cnt_ref, dq_ref, dk_ref, dv_ref,
             qm_ref, dob_ref, total_s, left_l, left_g, dq_s):
        i = pl.program_id(1)

        @pl.when(i == 0)
        def _():
            dk_ref[...] = jnp.zeros_like(dk_ref)
            dv_ref[...] = jnp.zeros_like(dv_ref)

        lane = lax.broadcasted_iota(jnp.int32, (1, LANES), 1)
        row, col = _tile_iotas(t)
        strict = col < row
        t_suffix = _tri(t, lambda r, c: r > c)
        t_excl = _tri(t, lambda r, c: r < c)
        q_pair, do_pair, tot_pair = q_ref[...], do_ref[...], tot_ref[...]
        for hh in range(2):
            in_head = (lane // HEAD_DIM) == hh
            qm_ref[hh] = jnp.where(in_head, q_pair, jnp.zeros_like(q_pair))
            dob_ref[hh] = jnp.where(in_head, do_pair, 0.0).astype(BF16)
            total_s[hh] = jnp.broadcast_to(
                jnp.sum(jnp.where(lane == hh * HEAD_DIM, tot_pair, 0.0), axis=-1, keepdims=True), (t, LANES))
        left_l[...] = jnp.zeros_like(left_l)
        left_g[...] = jnp.zeros_like(left_g)
        dq_s[...] = jnp.zeros_like(dq_s)
        reps = t // LANES

        def tile(j, masked):
            rows = pl.ds(pl.multiple_of(j * t, t), t)
            k, v = k_ref[rows, :], v_ref[rows, :]
            z2 = [_dot_nt(qm_ref[hh], k) for hh in HEADS]
            d_a = [_dot_nt(dob_ref[hh], v) for hh in HEADS]
            for hh in HEADS:
                qm, dob = qm_ref[hh], dob_ref[hh]
                log_1m, log_b = _sb_logs(z2[hh], strict, masked)
                blocks, totals, whole = _block_totals(log_1m)
                done = left_l[hh] + whole
                left_l[hh] = done
                a = jnp.exp2(log_b + _running_sums(blocks, totals, t_suffix, total_s[hh] - done, True))
                if masked:
                    a = jnp.where(strict, a, 0.0)
                g = a * d_a[hh]
                blocks, totals, whole = _block_totals(g)
                before = _running_sums(blocks, totals, t_excl, left_g[hh], False)
                left_g[hh] += whole
                dz = g - jnp.exp2(log_b) * (g + before)
                if masked:
                    dz = jnp.where(strict, dz, 0.0)
                dzb = dz.astype(BF16)
                dq_s[hh] += _dot(dzb, k)
                dk_ref[rows, :] += _dot_tn(dzb, qm)
                dv_ref[rows, :] += _dot_tn(a.astype(BF16), dob)

        def step(j, carry):
            tile(j, False)
            return carry

        swept = jnp.max(cnt_ref[...]).astype(jnp.int32)
        lax.fori_loop(i - swept, i, step, 0)
        tile(i, True)
        dq_ref[...] = jnp.where((lane // HEAD_DIM) == 0, dq_s[0], dq_s[1])

        @pl.when(i == s // t - 1)
        def _():
            dk_ref[...] *= 1.0 / LOG2E

    outs = [jax.ShapeDtypeStruct((s, 512), F32)] * 3
    return pl.pallas_call(
        body, name="sb_bwd", grid=(pairs, s // t), out_shape=outs,
        in_specs=[pl.BlockSpec((t, LANES), lambda hp, i: (i, hp)), pl.BlockSpec((s, LANES), lambda hp, i: (0, 4 + hp)),
                  pl.BlockSpec((s, LANES), lambda hp, i: (0, 8 + hp)), pl.BlockSpec((t, LANES), lambda hp, i: (i, hp)),
                  pl.BlockSpec((None, t, LANES), lambda hp, i: (hp, i, 0)),
                  pl.BlockSpec((None, None, 8, LANES), lambda hp, i: (hp, i, 0, 0))],
        out_specs=[pl.BlockSpec((t, LANES), lambda hp, i: (i, hp)), pl.BlockSpec((s, LANES), lambda hp, i: (0, hp)),
                   pl.BlockSpec((s, LANES), lambda hp, i: (0, hp))],
        scratch_shapes=[pltpu.VMEM((2, t, LANES), BF16), pltpu.VMEM((2, t, LANES), BF16)]
        + [pltpu.VMEM((2, t, LANES), F32)] * 4,
        compiler_params=_cparams("arbitrary", "arbitrary"),
    )(qkv, qkv, qkv, do, tot, cnt)


def _mla_bwd(qn, qr, kv, kr, do, o, lse, parts):
    s = qn.shape[0]
    t = min(ATT_TILE, s)
    pairs = MLA_HEADS // 2
    nq = s // t
    n = len(parts)

    def body(*refs):
        qn_ref, qr_ref, kn_ref, v_ref, kr_ref, do_ref, o_ref, lse_ref = refs[:8]
        dqn_ref, dqr_ref, dkn_ref, dv_ref, dkr_ref = refs[8 + n:13 + n]
        qcat_ref, dob_ref, lse_s, delta_s, dq_s = refs[13 + 2 * n:18 + 2 * n]
        hp, i = pl.program_id(0), pl.program_id(1)
        ride = _Exchange(False, refs[8:8 + n], refs[13 + n:13 + 2 * n], *refs[18 + 2 * n:])

        @pl.when((hp == 0) & (i == 0))
        def _():
            ride.start()

        @pl.when(i == 0)
        def _():
            dkn_ref[...] = jnp.zeros_like(dkn_ref)
            dv_ref[...] = jnp.zeros_like(dv_ref)
            dkr_ref[...] = jnp.zeros_like(dkr_ref)

        lane = lax.broadcasted_iota(jnp.int32, (1, LANES), 1)
        row, col = _tile_iotas(t)
        causal = col <= row
        q_pair, q_quad, do_pair, lse_pair = qn_ref[...], qr_ref[...], do_ref[...], lse_ref[...]
        do_o = do_pair * o_ref[...]
        zero = jnp.zeros_like(q_pair)
        ropes = []
        for hh in range(2):
            in_head = (lane // HEAD_DIM) == hh
            in_rope = (lane // MLA_ROPE) == (hp % 2) * 2 + hh
            ropes.append(in_rope)
            qcat_ref[hh, :, 0:LANES] = jnp.where(in_head, q_pair, zero)
            qcat_ref[hh, :, LANES:2 * LANES] = jnp.where(in_rope, q_quad, zero)
            dob_ref[hh] = jnp.where(in_head, do_pair, 0.0).astype(BF16)
            delta_s[hh] = jnp.broadcast_to(jnp.sum(jnp.where(in_head, do_o, 0.0), axis=-1, keepdims=True), (t, LANES))
            lse_s[hh] = jnp.broadcast_to(
                jnp.sum(jnp.where(lane == hh * HEAD_DIM, lse_pair, 0.0), axis=-1, keepdims=True), (t, LANES))
        dq_s[...] = jnp.zeros_like(dq_s)
        reps = t // LANES

        def tile(j, masked):
            rows = pl.ds(pl.multiple_of(j * t, t), t)
            kcat = jnp.concatenate([kn_ref[rows, :], kr_ref[rows, :]], axis=1)
            v = v_ref[rows, :]
            sc = [_dot_nt(qcat_ref[hh], kcat) for hh in HEADS]
            dp = [_dot_nt(dob_ref[hh], v) for hh in HEADS]
            p = [jnp.exp(sc[hh] - jnp.concatenate([lse_s[hh]] * reps, axis=1)) for hh in HEADS]
            if masked:
                p = [jnp.where(causal, p[hh], 0.0) for hh in HEADS]
            ds = [(p[hh] * (dp[hh] - jnp.concatenate([delta_s[hh]] * reps, axis=1))).astype(BF16) for hh in HEADS]
            for hh in HEADS:
                dq_s[hh] += _dot(ds[hh], kcat)
            dkcat = _dot_tn(ds[0], qcat_ref[0]) + _dot_tn(ds[1], qcat_ref[1])
            dkn_ref[rows, :] += dkcat[:, 0:LANES]
            dkr_ref[rows, :] += dkcat[:, LANES:2 * LANES]
            dv_ref[rows, :] += _dot_tn(p[0].astype(BF16), dob_ref[0]) + _dot_tn(p[1].astype(BF16), dob_ref[1])

        def step(j, carry):
            tile(j, False)
            return carry

        lax.fori_loop(0, i, step, 0)
        tile(i, True)
        dqn_ref[...] = jnp.where((lane // HEAD_DIM) == 0, dq_s[0, :, 0:LANES], dq_s[1, :, 0:LANES])
        dqr_ref[...] = (jnp.where(ropes[0], dq_s[0, :, LANES:2 * LANES], 0.0)
                        + jnp.where(ropes[1], dq_s[1, :, LANES:2 * LANES], 0.0))

        @pl.when((hp == pairs - 1) & (i == nq - 1))
        def _():
            ride.finish()

    pair_block = pl.BlockSpec((t, LANES), lambda hp, i: (i, hp))
    landed_shapes, sems = _exchange_shapes(False, parts)
    outs = [jax.ShapeDtypeStruct((s, 512), F32), jax.ShapeDtypeStruct((pairs, s, LANES), F32),
            jax.ShapeDtypeStruct((s, 512), F32), jax.ShapeDtypeStruct((s, 512), F32),
            jax.ShapeDtypeStruct((pairs, s, LANES), F32)] + landed_shapes
    res = pl.pallas_call(
        body, name="mla_bwd", grid=(pairs, nq), out_shape=outs,
        in_specs=[pair_block, pl.BlockSpec((t, LANES), lambda hp, i: (i, hp // 2)),
                  pl.BlockSpec((s, LANES), lambda hp, i: (0, hp)), pl.BlockSpec((s, LANES), lambda hp, i: (0, 4 + hp)),
                  pl.BlockSpec((s, LANES), lambda hp, i: (0, 0)), pair_block, pair_block,
                  pl.BlockSpec((None, t, LANES), lambda hp, i: (hp, i, 0))] + [ANY] * n,
        out_specs=[pair_block, pl.BlockSpec((None, t, LANES), lambda hp, i: (hp, i, 0)),
                   pl.BlockSpec((s, LANES), lambda hp, i: (0, hp)), pl.BlockSpec((s, LANES), lambda hp, i: (0, hp)),
                   pl.BlockSpec((None, s, LANES), lambda hp, i: (hp, 0, 0))] + [ANY] * n,
        scratch_shapes=[pltpu.VMEM((2, t, 2 * LANES), BF16), pltpu.VMEM((2, t, LANES), BF16), pltpu.VMEM((2, t, LANES), F32),
                        pltpu.VMEM((2, t, LANES), F32), pltpu.VMEM((2, t, 2 * LANES), F32)] + sems,
        compiler_params=_cparams("arbitrary", "arbitrary"),
    )(qn, qr, kv, kv, kr, do, o, lse, *parts)
    return res[:5], res[5:]


def _ffn_bwd(dh2, dh2b, gate, up, h1, g_ffn, w_down_t, w_gate_t, w_up_t):
    s, d = h1.shape
    d_ff = gate.shape[1]
    tm = min(FFN_BWD_ROW_TILE, s)
    tf = _ffn_tile(d_ff)

    def act_body(dh2b_ref, gate_ref, up_ref, wdt_ref, dgate_ref, dup_ref, act_ref):
        dact = _dot(dh2b_ref[...], wdt_ref[...])
        gate_v = gate_ref[...].astype(F32)
        up_v = up_ref[...].astype(F32)
        sig = jax.nn.sigmoid(gate_v)
        silu = gate_v * sig
        dup_ref[...] = (dact * silu).astype(BF16)
        dgate_ref[...] = (dact * up_v * (sig * (1.0 + gate_v * (1.0 - sig)))).astype(BF16)
        act_ref[...] = (silu * up_v).astype(BF16)

    ff = pl.BlockSpec((tm, tf), lambda j, r: (r, j))
    dgate, dup, act = pl.pallas_call(
        act_body, name="ffn_bwd_act", grid=(d_ff // tf, s // tm), out_shape=[jax.ShapeDtypeStruct((s, d_ff), BF16)] * 3,
        in_specs=[pl.BlockSpec((tm, d), lambda j, r: (r, 0)), ff, ff, pl.BlockSpec((d, tf), lambda j, r: (0, j))],
        out_specs=[ff, ff, ff],
        compiler_params=_cparams("arbitrary", "arbitrary"),
    )(dh2b, gate, up, w_down_t)

    def df_body(dgate_ref, dup_ref, dh2_ref, h1_ref, g_ref, wgt_ref, wut_ref, dh1_ref, dh1b_ref, dg_ref):
        df = _dot(dgate_ref[...], wgt_ref[...]) + _dot(dup_ref[...], wut_ref[...])
        dx, dg = _rms_bwd(h1_ref[...], g_ref[...], df)
        dh1 = dh2_ref[...] + dx
        dh1_ref[...] = dh1
        dh1b_ref[...] = dh1.astype(BF16)
        _accumulate(dg_ref, dg, pl.program_id(0) == 0)

    outs = [jax.ShapeDtypeStruct((s, d), F32), jax.ShapeDtypeStruct((s, d), BF16), jax.ShapeDtypeStruct((1, d), F32)]
    dh1, dh1b, dg = pl.pallas_call(
        df_body, name="ffn_bwd_df", grid=(s // tm,), out_shape=outs,
        in_specs=[_row_spec(tm, d_ff), _row_spec(tm, d_ff), _row_spec(tm, d), _row_spec(tm, d), _full_spec((1, d)),
                  _full_spec(w_gate_t.shape), _full_spec(w_up_t.shape)],
        out_specs=[_row_spec(tm, d), _row_spec(tm, d), _full_spec((1, d))],
        compiler_params=_cparams("arbitrary"),
    )(dgate, dup, dh2, h1, g_ffn, w_gate_t, w_up_t)
    return dgate, dup, act, dh1, dh1b, dg


def _largest_tile(n, cap):
    for cand in range(cap, 0, -LANES):
        if n % cand == 0:
            return cand
    return n


def _tn_matmul(a, b, name):
    assert a.dtype == BF16 and b.dtype == BF16
    s, m = a.shape
    n = b.shape[1]
    if s * m * 2 <= TN_RESIDENT_BYTES:
        tm, tn = m, min(n, TN_BLOCK)
    else:
        tm, tn = TN_BLOCK, n

    def body(a_ref, b_ref, o_ref):
        o_ref[...] = _dot_tn(a_ref[...], b_ref[...])

    return pl.pallas_call(
        body, name=name, grid=(m // tm, n // tn), out_shape=jax.ShapeDtypeStruct((m, n), F32),
        in_specs=[pl.BlockSpec((s, tm), lambda i, j: (0, i)), pl.BlockSpec((s, tn), lambda i, j: (0, j))],
        out_specs=pl.BlockSpec((tm, tn), lambda i, j: (i, j)),
        compiler_params=_cparams("arbitrary", "arbitrary"),
    )(a, b)


def _attn_out_bwd(dh1, w_o_t, o_mla, o_sb, g_mla, g_sb):
    s, d = dh1.shape
    tm = min(ROW_TILE, s)

    def body(dh1_ref, wot_ref, oa_ref, ob_ref, ga_ref, gb_ref, doa_ref, dob_ref, dga_ref, dgb_ref):
        first = pl.program_id(0) == 0
        dh1b = dh1_ref[...]
        dxa, dga = _rms_bwd(oa_ref[...], ga_ref[...], _dot(dh1b, wot_ref[:, 0:512]))
        dxb, dgb = _rms_bwd(ob_ref[...], gb_ref[...], _dot(dh1b, wot_ref[:, 512:1024]))
        doa_ref[...] = dxa
        dob_ref[...] = dxb
        _accumulate(dga_ref, dga, first)
        _accumulate(dgb_ref, dgb, first)

    outs = [jax.ShapeDtypeStruct((s, 512), F32)] * 2 + [jax.ShapeDtypeStruct((1, 512), F32)] * 2
    return pl.pallas_call(
        body, name="attn_out_bwd", grid=(s // tm,), out_shape=outs,
        in_specs=[_row_spec(tm, d), _full_spec(w_o_t.shape), _row_spec(tm, 512), _row_spec(tm, 512),
                  _full_spec((1, 512)), _full_spec((1, 512))],
        out_specs=[_row_spec(tm, 512), _row_spec(tm, 512), _full_spec((1, 512)), _full_spec((1, 512))],
        compiler_params=_cparams("arbitrary"),
    )(dh1, w_o_t, o_mla, o_sb, g_mla, g_sb)


def _proj_in_bwd(dqn, dqr, dkn, dv, dkr, dq_sb, dk_sb, dv_sb, cq, ckv, x, dh1, cos, sin_a, sin_b,
                 g_q, g_kv, g_mix, w_uq_t, w_ukv_t, w_a_t):
    s, d = x.shape
    tm = min(PROJ_BWD_ROW_TILE, s)

    def body(dqn_ref, dqr_ref, dkn_ref, dv_ref, dkr_ref, dqs_ref, dks_ref, dvs_ref, cq_ref, ckv_ref, x_ref, dh1_ref,
             cos_ref, sa_ref, sb_ref, gq_ref, gkv_ref, gm_ref, wuqt_ref, wukvt_ref, wat_ref,
             dx_ref, dproj_ref, dq_ref, dkv_ref, dgq_ref, dgkv_ref, dgm_ref):
        first = pl.program_id(0) == 0
        lane = lax.broadcasted_iota(jnp.int32, (1, LANES), 1)
        cos_t, sa_t, sb_t = cos_ref[...], sa_ref[...], sb_ref[...]
        dq_ref[:, 0:512] = (dqn_ref[...] * MLA_SCALE).astype(BF16)
        for half in range(2):
            quad = (dqr_ref[2 * half] + dqr_ref[2 * half + 1]) * MLA_SCALE
            dq_ref[:, 512 + half * LANES:512 + (half + 1) * LANES] = _rope_t(quad, cos_t, sa_t, sb_t).astype(BF16)
        dcq, dgq = _rms_bwd(cq_ref[...], gq_ref[...], _dot(dq_ref[...], wuqt_ref[...]))
        _accumulate(dgq_ref, dgq, first)
        dkv_ref[:, 0:512] = dkn_ref[...].astype(BF16)
        dkv_ref[:, 512:1024] = dv_ref[...].astype(BF16)
        dckv, dgkv = _rms_bwd(ckv_ref[...], gkv_ref[...], _dot(dkv_ref[...], wukvt_ref[...]))
        _accumulate(dgkv_ref, dgkv, first)
        g = _rope_t(dkr_ref[0] + dkr_ref[1] + dkr_ref[2] + dkr_ref[3], cos_t, sa_t, sb_t)
        g = g + pltpu.roll(g, 96, 1) + pltpu.roll(g, 64, 1) + pltpu.roll(g, 32, 1)
        dproj_ref[:, 0:256] = dcq.astype(BF16)
        dproj_ref[:, 256:384] = dckv.astype(BF16)
        dproj_ref[:, 384:512] = jnp.where(lane < MLA_ROPE, g, 0.0).astype(BF16)
        dproj_ref[:, 512:1024] = (dqs_ref[...] * SB_SCALE).astype(BF16)
        dproj_ref[:, 1024:1536] = dks_ref[...].astype(BF16)
        dproj_ref[:, 1536:2048] = dvs_ref[...].astype(BF16)
        dxn, dgm = _rms_bwd(x_ref[...], gm_ref[...], _dot(dproj_ref[...], wat_ref[...]))
        dx_ref[...] = dh1_ref[...] + dxn
        _accumulate(dgm_ref, dgm, first)

    quad_spec = pl.BlockSpec((4, tm, LANES), lambda r: (0, r, 0))
    outs = [jax.ShapeDtypeStruct((s, d), F32), jax.ShapeDtypeStruct((s, 2048), BF16), jax.ShapeDtypeStruct((s, 768), BF16),
            jax.ShapeDtypeStruct((s, 1024), BF16), jax.ShapeDtypeStruct((1, 256), F32), jax.ShapeDtypeStruct((1, 128), F32),
            jax.ShapeDtypeStruct((1, d), F32)]
    return pl.pallas_call(
        body, name="proj_in_bwd", grid=(s // tm,), out_shape=outs,
        in_specs=[_row_spec(tm, 512), quad_spec, _row_spec(tm, 512), _row_spec(tm, 512), quad_spec,
                  _row_spec(tm, 512), _row_spec(tm, 512), _row_spec(tm, 512), _row_spec(tm, 256), _row_spec(tm, 128),
                  _row_spec(tm, d), _row_spec(tm, d), _row_spec(tm, LANES), _row_spec(tm, LANES), _row_spec(tm, LANES),
                  _full_spec((1, 256)), _full_spec((1, 128)), _full_spec((1, d)),
                  _full_spec(w_uq_t.shape), _full_spec(w_ukv_t.shape), _full_spec(w_a_t.shape)],
        out_specs=[_row_spec(tm, d), _row_spec(tm, 2048), _row_spec(tm, 768), _row_spec(tm, 1024),
                   _full_spec((1, 256)), _full_spec((1, 128)), _full_spec((1, d))],
        compiler_params=_cparams("arbitrary"),
    )(dqn, dqr, dkn, dv, dkr, dq_sb, dk_sb, dv_sb, cq, ckv, x, dh1, cos, sin_a, sin_b, g_q, g_kv, g_mix,
      w_uq_t, w_ukv_t, w_a_t)


ANY = pl.BlockSpec(memory_space=pl.ANY)


def _place():
    return lax.axis_index("x"), lax.axis_index("y"), lax.axis_index("c")


def _all_gather(shards, name):
    n = len(shards)

    def body(*refs):
        ins, outs = refs[:n], refs[n:2 * n]
        send_sems, recv_sems, local_sems = refs[2 * n:]
        x, y, c = _place()
        me, sibling = (x, y, c), (x, y, 1 - c)
        chips = [(1 - x, y), (x, 1 - y), (1 - x, 1 - y)]

        def slot(a, px, py, pc):
            return outs[a].at[4 * px + 2 * py + pc]

        def copy(a, k, block, to, src=None):
            return pltpu.make_async_remote_copy(
                src_ref=slot(a, *block) if src is None else src, dst_ref=slot(a, *block),
                send_sem=send_sems.at[a, k], recv_sem=recv_sems.at[a, k], device_id=to, device_id_type=MESH)

        mine, first, passed = [], [], []
        for a in range(n):
            own = pltpu.make_async_copy(ins[a], slot(a, *me), local_sems.at[a])
            own.start()
            mine.append(own)
            cps = [copy(a, 0, me, sibling, src=ins[a])]
            cps += [copy(a, 1 + j, me, (*chip, c), src=ins[a]) for j, chip in enumerate(chips)]
            for cp in cps:
                cp.start()
            first += cps
        for a in range(n):
            for j, chip in enumerate(chips):
                copy(a, 1 + j, (*chip, c), me).wait_recv()
                fwd = copy(a, 4 + j, (*chip, c), sibling)
                fwd.start()
                passed.append(fwd)
        for a in range(n):
            copy(a, 0, sibling, me).wait_recv()
            for j, chip in enumerate(chips):
                copy(a, 4 + j, (*chip, 1 - c), me).wait_recv()
        for cp in first + passed:
            cp.wait_send()
        for own in mine:
            own.wait()

    return pl.pallas_call(
        body, name=name,
        out_shape=[jax.ShapeDtypeStruct((N_DEV,) + v.shape, v.dtype) for v in shards],
        in_specs=[ANY] * n, out_specs=[ANY] * n,
        scratch_shapes=[pltpu.SemaphoreType.DMA((n, 7)), pltpu.SemaphoreType.DMA((n, 7)), pltpu.SemaphoreType.DMA((n,))],
    )(*shards)


class _Exchange:
    def __init__(self, gather, ins, outs, send_sems, recv_sems, local_sems):
        self.gather, self.ins, self.outs = gather, ins, outs
        self.sems = (send_sems, recv_sems, local_sems)
        x, y, c = _place()
        self.me = 4 * x + 2 * y + c
        self.peers = []
        for k in range(1, N_DEV):
            px = 1 - x if k & 4 else x
            py = 1 - y if k & 2 else y
            pc = 1 - c if k & 1 else c
            self.peers.append(((px, py, pc), 4 * px + 2 * py + pc))

    def _remote(self, a, k, landing):
        send_sems, recv_sems, _ = self.sems
        where, number = self.peers[k]
        src = self.ins[a] if self.gather else self.ins[a].at[number]
        return pltpu.make_async_remote_copy(
            src_ref=src, dst_ref=self.outs[a].at[landing], send_sem=send_sems.at[a, k], recv_sem=recv_sems.at[a, k],
            device_id=where, device_id_type=MESH)

    def _local(self, a):
        src = self.ins[a] if self.gather else self.ins[a].at[self.me]
        return pltpu.make_async_copy(src, self.outs[a].at[self.me], self.sems[2].at[a])

    def start(self):
        for a in range(len(self.ins)):
            self._local(a).start()
            for k in range(N_DEV - 1):
                self._remote(a, k, self.me).start()

    def finish(self):
        for a in range(len(self.ins)):
            for k in range(N_DEV - 1):
                self._remote(a, k, self.peers[k][1]).wait_recv()
            for k in range(N_DEV - 1):
                self._remote(a, k, self.me).wait_send()
            self._local(a).wait()


def _exchange_shapes(gather, arrays):
    out_shape = [jax.ShapeDtypeStruct(((N_DEV,) + v.shape) if gather else v.shape, v.dtype) for v in arrays]
    n = len(arrays)
    sems = [pltpu.SemaphoreType.DMA((n, N_DEV - 1)), pltpu.SemaphoreType.DMA((n, N_DEV - 1)), pltpu.SemaphoreType.DMA((n,))]
    return out_shape, sems


def _exchange(gather, arrays, name):
    n = len(arrays)

    def body(*refs):
        ex = _Exchange(gather, refs[:n], refs[n:2 * n], *refs[2 * n:])
        ex.start()
        ex.finish()

    out_shape, sems = _exchange_shapes(gather, arrays)
    return pl.pallas_call(body, name=name, out_shape=out_shape, in_specs=[ANY] * n, out_specs=[ANY] * n,
                          scratch_shapes=sems)(*arrays)


def _grad_row_tile(rows):
    return _largest_tile_rows(rows, 256)


def _largest_tile_rows(rows, cap):
    for cand in range(cap, 0, -8):
        if rows % cand == 0:
            return cand
    return rows


def _adamw_math(w, g, m, v):
    m_new = ADAM_B1 * m + (1.0 - ADAM_B1) * g
    v_new = ADAM_B2 * v + (1.0 - ADAM_B2) * (g * g)
    m_hat = m_new / (1.0 - ADAM_B1 ** ADAM_STEP)
    v_hat = v_new / (1.0 - ADAM_B2 ** ADAM_STEP)
    delta = -ADAM_LR * (m_hat / (jnp.sqrt(v_hat) + ADAM_EPS) + ADAM_WD * w)
    return delta, m_new, v_new


def _adamw(slots, w, m, v, name):
    k, r, cdim = slots.shape
    tr = _grad_row_tile(r)

    def body(s_ref, w_ref, m_ref, v_ref, g_ref, d_ref, mo_ref, vo_ref):
        g = s_ref[0].astype(F32)
        for q in range(1, k):
            g = g + s_ref[q].astype(F32)
        g_ref[...] = g
        d_ref[...], mo_ref[...], vo_ref[...] = _adamw_math(w_ref[...], g, m_ref[...], v_ref[...])

    blk = pl.BlockSpec((tr, cdim), lambda i: (i, 0))
    return pl.pallas_call(
        body, name=name, grid=(r // tr,), out_shape=[jax.ShapeDtypeStruct((r, cdim), F32)] * 4,
        in_specs=[pl.BlockSpec((k, tr, cdim), lambda i: (0, i, 0)), blk, blk, blk], out_specs=[blk] * 4,
        compiler_params=_cparams("arbitrary"),
    )(slots, w, m, v)


def _stack_cols(g):
    n, r, c = g.shape
    return g.transpose(1, 0, 2).reshape(r, n * c)


def _split_cols(w):
    r, nc = w.shape
    return w.reshape(r, N_DEV, nc // N_DEV).transpose(1, 0, 2)


def _rope_tables(positions):
    inv_freq = ROPE_THETA ** (-jnp.arange(0, MLA_ROPE, 2, dtype=F32) / MLA_ROPE)
    ang = positions.astype(F32).reshape(-1, 1) * inv_freq[None, :]
    cos, sin, zero = jnp.cos(ang), jnp.sin(ang), jnp.zeros_like(ang)
    reps = LANES // MLA_ROPE
    return (jnp.tile(jnp.concatenate([cos, cos], axis=1), (1, reps)),
            jnp.tile(jnp.concatenate([-sin, zero], axis=1), (1, reps)),
            jnp.tile(jnp.concatenate([zero, sin], axis=1), (1, reps)))


def _local_step(x, positions, loss_target, gains, g_in, g_uq, g_ukv, late_shards):
    norm_mix, q_norm, kv_norm, out_mla, out_sb, norm_ffn, norm_final = gains
    d = x.shape[1]
    w_in = _stack_cols(g_in)
    w_a = jnp.concatenate([w_in[:, :416], jnp.zeros((d, 96), BF16), w_in[:, 416:]], axis=1)
    w_uq = jnp.concatenate([g_uq[:, :, :MLA_NOPE].transpose(1, 0, 2).reshape(Q_LORA, -1),
                            g_uq[:, :, MLA_NOPE:].transpose(1, 0, 2).reshape(Q_LORA, -1)], axis=1)
    w_ukv = jnp.concatenate([g_ukv[:, :, :MLA_NOPE].transpose(1, 0, 2).reshape(KV_LORA, -1),
                             g_ukv[:, :, MLA_NOPE:].transpose(1, 0, 2).reshape(KV_LORA, -1)], axis=1)
    cos, sin_a, sin_b = _rope_tables(positions)

    u, cq, ckv, cqn, ckvn, qn, qr, kv, kr, qkv_sb = _proj_in_fwd(x, norm_mix, w_a, q_norm, w_uq, kv_norm, w_ukv, cos, sin_a, sin_b)
    o_mla, lse, (g_o, g_gate, g_up, g_down) = _mla_fwd(qn, qr, kv, kr, late_shards)
    w_o = g_o.reshape(-1, d)
    w_gate, w_up = _stack_cols(g_gate), _stack_cols(g_up)
    w_down = g_down.reshape(-1, d)
    o_sb, tot, swept = _sb_fwd(qkv_sb)
    merged, h1, f = _attn_out_fwd(o_mla, o_sb, out_mla, out_sb, w_o, x, norm_ffn)
    gate, up, h2 = _ffn_fwd(f, h1, w_gate, w_up, w_down)
    loss, dh2, dh2b, dg_final = _final_loss(h2, loss_target, norm_final.reshape(1, d))

    dgate, dup, act, dh1, dh1b, dg_ffn = _ffn_bwd(dh2, dh2b, gate, up, h1, norm_ffn, w_down.T, w_gate.T, w_up.T)
    dw_down = _tn_matmul(act, dh2b, "dw_down")
    dw_gate = _tn_matmul(f, dgate, "dw_gate")
    dw_up = _tn_matmul(f, dup, "dw_up")
    do_mla, do_sb, dg_mla, dg_sb = _attn_out_bwd(dh1b, w_o.T, o_mla, o_sb, out_mla, out_sb)
    dw_o = _tn_matmul(merged, dh1b, "dw_o")
    dq_sb, dk_sb, dv_sb = _sb_bwd(qkv_sb, do_sb, tot, swept)
    early = [dw_o.reshape(N_DEV, -1, d), _split_cols(dw_gate), _split_cols(dw_up), dw_down.reshape(N_DEV, -1, d)]
    (dqn, dqr, dkn, dv, dkr), landed = _mla_bwd(qn, qr, kv, kr, do_mla, o_mla, lse, [p.astype(BF16) for p in early])
    dx, dproj, dq, dkv, dg_q, dg_kv, dg_mix = _proj_in_bwd(
        dqn, dqr, dkn, dv, dkr, dq_sb, dk_sb, dv_sb, cq, ckv, x, dh1, cos, sin_a, sin_b,
        q_norm, kv_norm, norm_mix, w_uq.T, w_ukv.T, w_a.T)
    dw_a = _tn_matmul(u, dproj, "dw_in")
    dw_uq = _tn_matmul(cqn, dq, "dw_uq")
    dw_ukv = _tn_matmul(ckvn, dkv, "dw_ukv")

    p_in = _split_cols(jnp.concatenate([dw_a[:, :416], dw_a[:, 512:]], axis=1))
    p_uq = jnp.concatenate([dw_uq[:, :512].reshape(Q_LORA, MLA_HEADS, MLA_NOPE),
                            dw_uq[:, 512:].reshape(Q_LORA, MLA_HEADS, MLA_ROPE)], axis=2).transpose(1, 0, 2)
    p_ukv = jnp.concatenate([dw_ukv[:, :512].reshape(KV_LORA, MLA_HEADS, MLA_NOPE),
                             dw_ukv[:, 512:].reshape(KV_LORA, MLA_HEADS, HEAD_DIM)], axis=2).transpose(1, 0, 2)
    late = [p.astype(BF16) for p in (p_in, p_uq, p_ukv)]
    gain_grads = [dg_mix, dg_q, dg_kv, dg_mla, dg_sb, dg_ffn, dg_final]
    return loss, dx, list(landed), late, gain_grads


def kernel(x, positions, norm_mix, w_in, q_latent_norm, w_uq, kv_latent_norm, w_ukv, out_norm_mla, out_norm_sb, w_o, norm_ffn, w_gate, w_up, w_down, norm_final, loss_target, m_norm_mix, m_w_in, m_q_latent_norm, m_w_uq, m_kv_latent_norm, m_w_ukv, m_out_norm_mla, m_out_norm_sb, m_w_o, m_norm_ffn, m_w_gate, m_w_up, m_w_down, m_norm_final, v_norm_mix, v_w_in, v_q_latent_norm, v_w_uq, v_kv_latent_norm, v_w_ukv, v_out_norm_mla, v_out_norm_sb, v_w_o, v_norm_ffn, v_w_gate, v_w_up, v_w_down, v_norm_final):
    mats = [w_in, w_uq, w_ukv, w_o, w_gate, w_up, w_down]
    mat_m = [m_w_in, m_w_uq, m_w_ukv, m_w_o, m_w_gate, m_w_up, m_w_down]
    mat_v = [v_w_in, v_w_uq, v_w_ukv, v_w_o, v_w_gate, v_w_up, v_w_down]
    mat_names = ["w_in", "w_uq", "w_ukv", "w_o", "w_gate", "w_up", "w_down"]
    gains = [norm_mix, q_latent_norm, kv_latent_norm, out_norm_mla, out_norm_sb, norm_ffn, norm_final]
    gain_m = [m_norm_mix, m_q_latent_norm, m_kv_latent_norm, m_out_norm_mla, m_out_norm_sb, m_norm_ffn, m_norm_final]
    gain_v = [v_norm_mix, v_q_latent_norm, v_kv_latent_norm, v_out_norm_mla, v_out_norm_sb, v_norm_ffn, v_norm_final]

    shards = [w[0].astype(BF16) for w in mats]
    g_in, g_uq, g_ukv = _all_gather(shards[:3], "weight_all_gather")

    gains2d = [g.reshape(1, -1) for g in gains]
    loss_part, dx, landed, late, gain_grads = _local_step(
        x[0], positions[0], loss_target[0], gains2d, g_in, g_uq, g_ukv, shards[3:])

    slots = list(_exchange(False, late, "grad_scatter")) + landed
    mat_out = [_adamw(sl, w[0], m[0], v[0], "adamw_" + nm)
               for sl, w, m, v, nm in zip(slots, mats, mat_m, mat_v, mat_names)]

    sizes = [g.size for g in gains]
    used = sum(sizes) + LANES
    rows = -(-used // (8 * LANES)) * 8

    def pack(vals, tail):
        flat = jnp.concatenate([v.reshape(-1) for v in vals] + [tail])
        return jnp.pad(flat, (0, rows * LANES - flat.size)).reshape(rows, LANES)

    zeros_tail = jnp.zeros((LANES,), F32)
    small = _all_gather([pack(gain_grads, loss_part.reshape(-1))], "gain_all_gather")[0]
    g_s, d_s, m_s, v_s = _adamw(small, pack(gains, zeros_tail), pack(gain_m, zeros_tail), pack(gain_v, zeros_tail), "adamw_gains")

    def unpack(packed):
        flat = packed.reshape(-1)
        outs, off = [], 0
        for g, n in zip(gains, sizes):
            outs.append(flat[off:off + n].reshape(g.shape))
            off += n
        return outs

    loss = g_s.reshape(-1)[sum(sizes)]

    order = ["norm_mix", "w_in", "q_latent_norm", "w_uq", "kv_latent_norm", "w_ukv", "out_norm_mla", "out_norm_sb",
             "w_o", "norm_ffn", "w_gate", "w_up", "w_down", "norm_final"]
    gain_names = ["norm_mix", "q_latent_norm", "kv_latent_norm", "out_norm_mla", "out_norm_sb", "norm_ffn", "norm_final"]
    result = [loss, dx[None]]
    for kind in range(4):
        small_parts = dict(zip(gain_names, unpack([g_s, d_s, m_s, v_s][kind])))
        mat_parts = {nm: out[kind][None] for nm, out in zip(mat_names, mat_out)}
        result += [small_parts[nm] if nm in small_parts else mat_parts[nm] for nm in order]
    return tuple(result)
```

```python
import functools
import math

import jax
import jax.numpy as jnp
from jax import lax
from jax.experimental import pallas as pl
from jax.experimental.pallas import tpu as pltpu

F32 = jnp.float32
BF16 = jnp.bfloat16
MESH = pl.DeviceIdType.MESH

EPS = 1e-6
ROPE_THETA = 10000.0
MLA_HEADS = 8
MLA_NOPE = 64
MLA_ROPE = 32
SB_HEADS = 8
HEAD_DIM = 64
Q_LORA = 256
KV_LORA = 128
MLA_SCALE = 1.0 / math.sqrt(MLA_NOPE + MLA_ROPE)
SB_SCALE = 1.0 / math.sqrt(HEAD_DIM)
LOG2E = math.log2(math.e)
SB_DEAD = -160.0
N_DEV = 8

ADAM_LR = 0.001
ADAM_B1 = 0.9
ADAM_B2 = 0.999
ADAM_EPS = 1e-08
ADAM_WD = 0.01
ADAM_STEP = 10

LANES = 128
ATT_TILE = 512
SB_TILE = 512
TRI = 256
ROW_TILE = 512
FFN_BWD_ROW_TILE = 256
PROJ_BWD_ROW_TILE = 256
TN_BLOCK = 256
TN_RESIDENT_BYTES = 16 * 1024 * 1024
VMEM_LIMIT = 56 * 1024 * 1024
NEG = -1e30


def _cparams(*sem):
    return pltpu.CompilerParams(dimension_semantics=sem, vmem_limit_bytes=VMEM_LIMIT)


def _dot(a, b):
    return jnp.dot(a, b, preferred_element_type=F32)


def _dot_nt(a, b):
    return lax.dot_general(a, b, (((1,), (1,)), ((), ())), preferred_element_type=F32)


def _dot_tn(a, b):
    return lax.dot_general(a, b, (((0,), (0,)), ((), ())), preferred_element_type=F32)


def _rms(x, g):
    r = lax.rsqrt(jnp.mean(x * x, axis=-1, keepdims=True) + EPS)
    return x * r * g


def _rms_bwd(x, g, dy):
    r = lax.rsqrt(jnp.mean(x * x, axis=-1, keepdims=True) + EPS)
    n = x * r
    dn = dy * g
    dx = r * (dn - n * jnp.mean(dn * n, axis=-1, keepdims=True))
    return dx, jnp.sum(dy * n, axis=0, keepdims=True)


def _rope(x, cos, sin_a, sin_b):
    return x * cos + pltpu.roll(x, 112, 1) * sin_a + pltpu.roll(x, 16, 1) * sin_b


def _rope_t(g, cos, sin_a, sin_b):
    return g * cos + pltpu.roll(g * sin_a, 16, 1) + pltpu.roll(g * sin_b, 112, 1)


def _row_spec(tm, width):
    return pl.BlockSpec((tm, width), lambda r: (r, 0))


def _full_spec(shape):
    return pl.BlockSpec(shape, lambda *_: (0,) * len(shape))


def _accumulate(ref, val, first):
    @pl.when(first)
    def _():
        ref[...] = val

    @pl.when(jnp.logical_not(first))
    def _():
        ref[...] += val


def _proj_in_fwd(x, g_mix, w_a, g_q, w_uq, g_kv, w_ukv, cos, sin_a, sin_b):
    s, d = x.shape
    tm = min(ROW_TILE, s)

    def body(x_ref, gm_ref, wa_ref, gq_ref, wuq_ref, gkv_ref, wukv_ref, cos_ref, sa_ref, sb_ref,
             u_ref, cq_ref, ckv_ref, cqn_ref, ckvn_ref, qn_ref, qr_ref, kv_ref, kr_ref, sbq_ref):
        u = _rms(x_ref[...], gm_ref[...]).astype(BF16)
        u_ref[...] = u
        cq = _dot(u, wa_ref[:, 0:256])
        ckv = _dot(u, wa_ref[:, 256:384])
        kr = _dot(u, wa_ref[:, 384:512])
        cq_ref[...] = cq
        ckv_ref[...] = ckv
        cqn = _rms(cq, gq_ref[...]).astype(BF16)
        ckvn = _rms(ckv, gkv_ref[...]).astype(BF16)
        cqn_ref[...] = cqn
        ckvn_ref[...] = ckvn
        cos_t, sa_t, sb_t = cos_ref[...], sa_ref[...], sb_ref[...]
        qn_ref[...] = (_dot(cqn, wuq_ref[:, 0:512]) * MLA_SCALE).astype(BF16)
        for half in range(2):
            lo = 512 + half * LANES
            qr = _dot(cqn, wuq_ref[:, lo:lo + LANES])
            qr_ref[:, half * LANES:(half + 1) * LANES] = (_rope(qr, cos_t, sa_t, sb_t) * MLA_SCALE).astype(BF16)
        kv_ref[...] = _dot(ckvn, wukv_ref[...]).astype(BF16)
        krt = kr + pltpu.roll(kr, 32, 1) + pltpu.roll(kr, 64, 1) + pltpu.roll(kr, 96, 1)
        kr_ref[...] = _rope(krt, cos_t, sa_t, sb_t).astype(BF16)
        sbq_ref[:, 0:512] = (_dot(u, wa_ref[:, 512:1024]) * (SB_SCALE * LOG2E)).astype(BF16)
        sbq_ref[:, 512:1536] = _dot(u, wa_ref[:, 1024:2048]).astype(BF16)

    outs = [
        jax.ShapeDtypeStruct((s, d), BF16),
        jax.ShapeDtypeStruct((s, 256), F32),
        jax.ShapeDtypeStruct((s, 128), F32),
        jax.ShapeDtypeStruct((s, 256), BF16),
        jax.ShapeDtypeStruct((s, 128), BF16),
        jax.ShapeDtypeStruct((s, 512), BF16),
        jax.ShapeDtypeStruct((s, 256), BF16),
        jax.ShapeDtypeStruct((s, 1024), BF16),
        jax.ShapeDtypeStruct((s, 128), BF16),
        jax.ShapeDtypeStruct((s, 1536), BF16),
    ]
    return pl.pallas_call(
        body, name="proj_in_fwd", grid=(s // tm,), out_shape=outs,
        in_specs=[_row_spec(tm, d), _full_spec(g_mix.shape), _full_spec(w_a.shape), _full_spec(g_q.shape),
                  _full_spec(w_uq.shape), _full_spec(g_kv.shape), _full_spec(w_ukv.shape),
                  _row_spec(tm, LANES), _row_spec(tm, LANES), _row_spec(tm, LANES)],
        out_specs=[_row_spec(tm, o.shape[1]) for o in outs],
        compiler_params=_cparams("arbitrary"),
    )(x, g_mix, w_a, g_q, w_uq, g_kv, w_ukv, cos, sin_a, sin_b)


def _attn_out_fwd(o_mla, o_sb, g_mla, g_sb, w_o, x, g_ffn):
    s, d = x.shape
    tm = min(ROW_TILE, s)

    def body(oa_ref, ob_ref, ga_ref, gb_ref, wo_ref, x_ref, gf_ref, merged_ref, h1_ref, f_ref):
        na = _rms(oa_ref[...], ga_ref[...]).astype(BF16)
        nb = _rms(ob_ref[...], gb_ref[...]).astype(BF16)
        merged_ref[:, 0:512] = na
        merged_ref[:, 512:1024] = nb
        h1 = x_ref[...] + _dot(na, wo_ref[0:512, :]) + _dot(nb, wo_ref[512:1024, :])
        h1_ref[...] = h1
        f_ref[...] = _rms(h1, gf_ref[...]).astype(BF16)

    outs = [jax.ShapeDtypeStruct((s, d), BF16), jax.ShapeDtypeStruct((s, d), F32), jax.ShapeDtypeStruct((s, d), BF16)]
    return pl.pallas_call(
        body, name="attn_out_fwd", grid=(s // tm,), out_shape=outs,
        in_specs=[_row_spec(tm, 512), _row_spec(tm, 512), _full_spec(g_mla.shape), _full_spec(g_sb.shape),
                  _full_spec(w_o.shape), _row_spec(tm, d), _full_spec(g_ffn.shape)],
        out_specs=[_row_spec(tm, d)] * 3,
        compiler_params=_cparams("arbitrary"),
    )(o_mla, o_sb, g_mla, g_sb, w_o, x, g_ffn)


def _ffn_tile(d_ff):
    return d_ff // 2 if (d_ff // 2) % LANES == 0 else d_ff


def _ffn_fwd(f, h1, w_gate, w_up, w_down):
    s, d = h1.shape
    d_ff = w_gate.shape[1]
    tm = min(ROW_TILE, s)
    tf = _ffn_tile(d_ff)

    def body(f_ref, h1_ref, wg_ref, wu_ref, wd_ref, gate_ref, up_ref, h2_ref):
        j = pl.program_id(1)
        fb = f_ref[...]
        gate = _dot(fb, wg_ref[...])
        up = _dot(fb, wu_ref[...])
        gate_ref[...] = gate.astype(BF16)
        up_ref[...] = up.astype(BF16)
        act = (gate * jax.nn.sigmoid(gate) * up).astype(BF16)
        part = _dot(act, wd_ref[...])

        @pl.when(j == 0)
        def _():
            h2_ref[...] = h1_ref[...] + part

        @pl.when(j != 0)
        def _():
            h2_ref[...] += part

    outs = [jax.ShapeDtypeStruct((s, d_ff), BF16), jax.ShapeDtypeStruct((s, d_ff), BF16), jax.ShapeDtypeStruct((s, d), F32)]
    return pl.pallas_call(
        body, name="ffn_fwd", grid=(s // tm, d_ff // tf), out_shape=outs,
        in_specs=[pl.BlockSpec((tm, d), lambda r, j: (r, 0)), pl.BlockSpec((tm, d), lambda r, j: (r, 0)),
                  pl.BlockSpec((d, tf), lambda r, j: (0, j)), pl.BlockSpec((d, tf), lambda r, j: (0, j)),
                  pl.BlockSpec((tf, d), lambda r, j: (j, 0))],
        out_specs=[pl.BlockSpec((tm, tf), lambda r, j: (r, j)), pl.BlockSpec((tm, tf), lambda r, j: (r, j)),
                   pl.BlockSpec((tm, d), lambda r, j: (r, 0))],
        compiler_params=_cparams("arbitrary", "arbitrary"),
    )(f, h1, w_gate, w_up, w_down)


def _final_loss(h2, target, g_final):
    s, d = h2.shape
    tm = min(ROW_TILE, s)

    def body(h2_ref, t_ref, g_ref, loss_ref, dh2_ref, dh2b_ref, dg_ref):
        first = pl.program_id(0) == 0
        h2v = h2_ref[...]
        g = g_ref[...]
        diff = _rms(h2v, g) - t_ref[...]
        part = 0.5 * jnp.sum(jnp.mean(diff * diff, axis=-1, keepdims=True), axis=0, keepdims=True)
        _accumulate(loss_ref, jnp.broadcast_to(part, loss_ref.shape), first)
        dx, dg = _rms_bwd(h2v, g, diff * (1.0 / d))
        dh2_ref[...] = dx
        dh2b_ref[...] = dx.astype(BF16)
        _accumulate(dg_ref, dg, first)

    outs = [jax.ShapeDtypeStruct((1, LANES), F32), jax.ShapeDtypeStruct((s, d), F32), jax.ShapeDtypeStruct((s, d), BF16),
            jax.ShapeDtypeStruct((1, d), F32)]
    return pl.pallas_call(
        body, name="final_loss", grid=(s // tm,), out_shape=outs,
        in_specs=[_row_spec(tm, d), _row_spec(tm, d), _full_spec((1, d))],
        out_specs=[_full_spec((1, LANES)), _row_spec(tm, d), _row_spec(tm, d), _full_spec((1, d))],
        compiler_params=_cparams("arbitrary"),
    )(h2, target, g_final)


def _tile_iotas(t):
    return lax.broadcasted_iota(jnp.int32, (t, t), 0), lax.broadcasted_iota(jnp.int32, (t, t), 1)


def _stacked_mask(t, strict):
    row = lax.broadcasted_iota(jnp.int32, (2 * t, t), 0)
    col = lax.broadcasted_iota(jnp.int32, (2 * t, t), 1)
    row = jnp.where(row >= t, row - t, row)
    return col < row if strict else col <= row


def _mla_fwd(qn, qr, kv, kr, shards):
    s = qn.shape[0]
    t = min(ATT_TILE, s)
    pairs = MLA_HEADS // 2
    nq = s // t
    n = len(shards)

    def body(*refs):
        qn_ref, qr_ref, kn_ref, v_ref, kr_ref = refs[:5]
        o_ref, lse_ref = refs[5 + n:7 + n]
        qcat_ref, m_ref, l_ref, acc_ref = refs[7 + 2 * n:11 + 2 * n]
        hp, i = pl.program_id(0), pl.program_id(1)
        ride = _Exchange(True, refs[5:5 + n], refs[7 + n:7 + 2 * n], *refs[11 + 2 * n:])

        @pl.when((hp == 0) & (i == 0))
        def _():
            ride.start()

        lane = lax.broadcasted_iota(jnp.int32, (1, LANES), 1)
        row, col = _tile_iotas(t)
        causal = col <= row
        q_pair, q_quad = qn_ref[...], qr_ref[...]
        zero = jnp.zeros_like(q_pair)
        for hh in range(2):
            in_head = (lane // HEAD_DIM) == hh
            in_rope = (lane // MLA_ROPE) == (hp % 2) * 2 + hh
            qcat_ref[hh * t:(hh + 1) * t, 0:LANES] = jnp.where(in_head, q_pair, zero)
            qcat_ref[hh * t:(hh + 1) * t, LANES:2 * LANES] = jnp.where(in_rope, q_quad, zero)
        m_ref[...] = jnp.full_like(m_ref, NEG)
        l_ref[...] = jnp.zeros_like(l_ref)
        acc_ref[...] = jnp.zeros_like(acc_ref)

        def tile(j, width, masked):
            rows = pl.ds(pl.multiple_of(j * t, t), width * t)
            kcat = jnp.concatenate([kn_ref[rows, :], kr_ref[rows, :]], axis=1)
            v_ones = jnp.concatenate([v_ref[rows, :], jnp.ones((width * t, LANES), BF16)], axis=1)
            scores = [_dot_nt(qcat_ref[hh * t:(hh + 1) * t, :], kcat) for hh in range(2)]
            for hh in range(2):
                half = slice(hh * t, (hh + 1) * t)
                sc = jnp.where(causal, scores[hh], NEG) if masked else scores[hh]
                m = m_ref[half, :]
                m_new = jnp.maximum(m, jnp.max(sc, axis=-1, keepdims=True))
                alpha = jnp.exp(m - m_new)
                p = jnp.exp(sc - jnp.concatenate([m_new] * (width * t // LANES), axis=1))
                pv = _dot(p.astype(BF16), v_ones)
                l_ref[half, :] = alpha * l_ref[half, :] + pv[:, LANES:]
                acc_ref[half, :] = alpha * acc_ref[half, :] + pv[:, :LANES]
                m_ref[half, :] = m_new

        tile(i, 1, True)

        def step(n, carry):
            tile(2 * n, 2, False)
            return carry

        lax.fori_loop(0, i // 2, step, 0)

        @pl.when(i % 2 == 1)
        def _():
            tile(i - 1, 1, False)
        first = (lane // HEAD_DIM) == 0
        o = acc_ref[...] / l_ref[...]
        lse = m_ref[...] + jnp.log(l_ref[...])
        o_ref[...] = jnp.where(first, o[0:t], o[t:2 * t])
        lse_ref[...] = jnp.where(first, lse[0:t], lse[t:2 * t])

        @pl.when((hp == pairs - 1) & (i == nq - 1))
        def _():
            ride.finish()

    gathered_shapes, sems = _exchange_shapes(True, shards)
    outs = [jax.ShapeDtypeStruct((s, 512), F32), jax.ShapeDtypeStruct((pairs, s, LANES), F32)] + gathered_shapes
    res = pl.pallas_call(
        body, name="mla_fwd", grid=(pairs, nq), out_shape=outs,
        in_specs=[pl.BlockSpec((t, LANES), lambda hp, i: (i, hp)), pl.BlockSpec((t, LANES), lambda hp, i: (i, hp // 2)),
                  pl.BlockSpec((s, LANES), lambda hp, i: (0, hp)), pl.BlockSpec((s, LANES), lambda hp, i: (0, 4 + hp)),
                  pl.BlockSpec((s, LANES), lambda hp, i: (0, 0))] + [ANY] * n,
        out_specs=[pl.BlockSpec((t, LANES), lambda hp, i: (i, hp)), pl.BlockSpec((None, t, LANES), lambda hp, i: (hp, i, 0))]
        + [ANY] * n,
        scratch_shapes=[pltpu.VMEM((2 * t, 2 * LANES), BF16), pltpu.VMEM((2 * t, LANES), F32), pltpu.VMEM((2 * t, LANES), F32),
                        pltpu.VMEM((2 * t, LANES), F32)] + sems,
        compiler_params=_cparams("arbitrary", "arbitrary"),
    )(qn, qr, kv, kv, kr, *shards)
    return res[0], res[1], res[2:]


HEADS = (0, 1)


def _sb_logs(z2, strict, masked):
    log_b = jnp.minimum(z2, 0.0) - jnp.log2(1.0 + jnp.exp2(-jnp.abs(z2)))
    log_1m = log_b - z2
    if masked:
        log_1m = jnp.where(strict, log_1m, 0.0)
    return log_1m, log_b


def _block_totals(x):
    t, w = x.shape
    nb = max(w // TRI, 1)
    bw = w // nb
    blocks = [x[:, b * bw:(b + 1) * bw] for b in range(nb)]
    totals = [jnp.broadcast_to(jnp.sum(blk, axis=-1, keepdims=True), (t, LANES)) for blk in blocks]
    whole = totals[0]
    for tot in totals[1:]:
        whole = whole + tot
    return blocks, totals, whole


def _running_sums(blocks, totals, tri, carry, suffix):
    nb = len(blocks)
    reps = blocks[0].shape[1] // LANES
    outs = [None] * nb
    run = carry
    for b in (range(nb - 1, -1, -1) if suffix else range(nb)):
        outs[b] = _dot(blocks[b].astype(BF16), tri) + jnp.concatenate([run] * reps, axis=1)
        run = run + totals[b]
    return outs[0] if nb == 1 else jnp.concatenate(outs, axis=1)


def _tri(t, rel):
    n = min(TRI, t)
    row, col = _tile_iotas(n)
    return rel(row, col).astype(BF16)


def _sb_fwd(qkv):
    s = qkv.shape[0]
    t = min(SB_TILE, s)
    pairs = SB_HEADS // 2

    def body(q_ref, k_ref, v_ref, o_ref, tot_ref, cnt_ref, qm_ref, right_ref, acc_ref):
        i = pl.program_id(1)
        lane = lax.broadcasted_iota(jnp.int32, (1, LANES), 1)
        row, col = _tile_iotas(t)
        strict = col < row
        t_suffix = _tri(t, lambda r, c: r > c)
        q_pair = q_ref[...]
        for hh in range(2):
            qm_ref[hh] = jnp.where((lane // HEAD_DIM) == hh, q_pair, jnp.zeros_like(q_pair))
        right_ref[...] = jnp.zeros_like(right_ref)
        acc_ref[...] = jnp.zeros_like(acc_ref)

        def tile(j, masked):
            rows = pl.ds(pl.multiple_of(j * t, t), t)
            k, v = k_ref[rows, :], v_ref[rows, :]
            for hh in HEADS:
                log_1m, log_b = _sb_logs(_dot_nt(qm_ref[hh], k), strict, masked)
                blocks, totals, whole = _block_totals(log_1m)
                a = jnp.exp2(log_b + _running_sums(blocks, totals, t_suffix, right_ref[hh], True))
                if masked:
                    a = jnp.where(strict, a, 0.0)
                right_ref[hh] += whole
                acc_ref[hh] += _dot(a.astype(BF16), v)

        tile(i, True)

        def alive(n):
            return (n < i) & (jnp.max(right_ref[...]) > SB_DEAD)

        def step(n):
            tile(i - 1 - n, False)
            return n + 1

        swept = lax.while_loop(alive, step, jnp.int32(0))
        cnt_ref[...] = jnp.full(cnt_ref.shape, swept.astype(F32))
        first = (lane // HEAD_DIM) == 0
        o_ref[...] = jnp.where(first, acc_ref[0], acc_ref[1])
        tot_ref[...] = jnp.where(first, right_ref[0], right_ref[1])

    outs = [jax.ShapeDtypeStruct((s, 512), F32), jax.ShapeDtypeStruct((pairs, s, LANES), F32),
            jax.ShapeDtypeStruct((pairs, s // t, 8, LANES), F32)]
    return pl.pallas_call(
        body, name="sb_fwd", grid=(pairs, s // t), out_shape=outs,
        in_specs=[pl.BlockSpec((t, LANES), lambda hp, i: (i, hp)), pl.BlockSpec((s, LANES), lambda hp, i: (0, 4 + hp)),
                  pl.BlockSpec((s, LANES), lambda hp, i: (0, 8 + hp))],
        out_specs=[pl.BlockSpec((t, LANES), lambda hp, i: (i, hp)), pl.BlockSpec((None, t, LANES), lambda hp, i: (hp, i, 0)),
                   pl.BlockSpec((None, None, 8, LANES), lambda hp, i: (hp, i, 0, 0))],
        scratch_shapes=[pltpu.VMEM((2, t, LANES), BF16), pltpu.VMEM((2, t, LANES), F32), pltpu.VMEM((2, t, LANES), F32)],
        compiler_params=_cparams("arbitrary", "arbitrary"),
    )(qkv, qkv, qkv)


def _sb_bwd(qkv, do, tot, cnt):
    s = qkv.shape[0]
    t = min(SB_TILE, s)
    pairs = SB_HEADS // 2

    def body(q_ref, k_ref, v_ref, do_ref, tot_ref, cnt_ref, dq_ref, dk_ref, dv_ref,
             qm_ref, dob_ref, total_s, left_l, left_g, dq_s):
        i = pl.program_id(1)

        @pl.when(i == 0)
        def _():
            dk_ref[...] = jnp.zeros_like(dk_ref)
            dv_ref[...] = jnp.zeros_like(dv_ref)

        lane = lax.broadcasted_iota(jnp.int32, (1, LANES), 1)
        row, col = _tile_iotas(t)
        strict = col < row
        t_suffix = _tri(t, lambda r, c: r > c)
        t_excl = _tri(t, lambda r, c: r < c)
        q_pair, do_pair, tot_pair = q_ref[...], do_ref[...], tot_ref[...]
        for hh in range(2):
            in_head = (lane // HEAD_DIM) == hh
            qm_ref[hh] = jnp.where(in_head, q_pair, jnp.zeros_like(q_pair))
            dob_ref[hh] = jnp.where(in_head, do_pair, 0.0).astype(BF16)
            total_s[hh] = jnp.broadcast_to(
                jnp.sum(jnp.where(lane == hh * HEAD_DIM, tot_pair, 0.0), axis=-1, keepdims=True), (t, LANES))
        left_l[...] = jnp.zeros_like(left_l)
        left_g[...] = jnp.zeros_like(left_g)
        dq_s[...] = jnp.zeros_like(dq_s)
        reps = t // LANES

        def tile(j, masked):
            rows = pl.ds(pl.multiple_of(j * t, t), t)
            k, v = k_ref[rows, :], v_ref[rows, :]
            z2 = [_dot_nt(qm_ref[hh], k) for hh in HEADS]
            d_a = [_dot_nt(dob_ref[hh], v) for hh in HEADS]
            for hh in HEADS:
                qm, dob = qm_ref[hh], dob_ref[hh]
                log_1m, log_b = _sb_logs(z2[hh], strict, masked)
                blocks, totals, whole = _block_totals(log_1m)
                done = left_l[hh] + whole
                left_l[hh] = done
                a = jnp.exp2(log_b + _running_sums(blocks, totals, t_suffix, total_s[hh] - done, True))
                if masked:
                    a = jnp.where(strict, a, 0.0)
                g = a * d_a[hh]
                blocks, totals, whole = _block_totals(g)
                before = _running_sums(blocks, totals, t_excl, left_g[hh], False)
                left_g[hh] += whole
                dz = g - jnp.exp2(log_b) * (g + before)
                if masked:
                    dz = jnp.where(strict, dz, 0.0)
                dzb = dz.astype(BF16)
                dq_s[hh] += _dot(dzb, k)
                dk_ref[rows, :] += _dot_tn(dzb, qm)
                dv_ref[rows, :] += _dot_tn(a.astype(BF16), dob)

        def step(j, carry):
            tile(j, False)
            return carry

        swept = jnp.max(cnt_ref[...]).astype(jnp.int32)
        lax.fori_loop(i - swept, i, step, 0)
        tile(i, True)
        dq_ref[...] = jnp.where((lane // HEAD_DIM) == 0, dq_s[0], dq_s[1])

        @pl.when(i == s // t - 1)
        def _():
            dk_ref[...] *= 1.0 / LOG2E

    outs = [jax.ShapeDtypeStruct((s, 512), F32)] * 3
    return pl.pallas_call(
        body, name="sb_bwd", grid=(pairs, s // t), out_shape=outs,
        in_specs=[pl.BlockSpec((t, LANES), lambda hp, i: (i, hp)), pl.BlockSpec((s, LANES), lambda hp, i: (0, 4 + hp)),
                  pl.BlockSpec((s, LANES), lambda hp, i: (0, 8 + hp)), pl.BlockSpec((t, LANES), lambda hp, i: (i, hp)),
                  pl.BlockSpec((None, t, LANES), lambda hp, i: (hp, i, 0)),
                  pl.BlockSpec((None, None, 8, LANES), lambda hp, i: (hp, i, 0, 0))],
        out_specs=[pl.BlockSpec((t, LANES), lambda hp, i: (i, hp)), pl.BlockSpec((s, LANES), lambda hp, i: (0, hp)),
                   pl.BlockSpec((s, LANES), lambda hp, i: (0, hp))],
        scratch_shapes=[pltpu.VMEM((2, t, LANES), BF16), pltpu.VMEM((2, t, LANES), BF16)]
        + [pltpu.VMEM((2, t, LANES), F32)] * 4,
        compiler_params=_cparams("arbitrary", "arbitrary"),
    )(qkv, qkv, qkv, do, tot, cnt)


def _mla_bwd(qn, qr, kv, kr, do, o, lse, parts):
    s = qn.shape[0]
    t = min(ATT_TILE, s)
    pairs = MLA_HEADS // 2
    nq = s // t
    n = len(parts)

    def body(*refs):
        qn_ref, qr_ref, kn_ref, v_ref, kr_ref, do_ref, o_ref, lse_ref = refs[:8]
        dqn_ref, dqr_ref, dkn_ref, dv_ref, dkr_ref = refs[8 + n:13 + n]
        qcat_ref, dob_ref, lse_s, delta_s, dq_s = refs[13 + 2 * n:18 + 2 * n]
        hp, i = pl.program_id(0), pl.program_id(1)
        ride = _Exchange(False, refs[8:8 + n], refs[13 + n:13 + 2 * n], *refs[18 + 2 * n:])

        @pl.when((hp == 0) & (i == 0))
        def _():
            ride.start()

        @pl.when(i == 0)
        def _():
            dkn_ref[...] = jnp.zeros_like(dkn_ref)
            dv_ref[...] = jnp.zeros_like(dv_ref)
            dkr_ref[...] = jnp.zeros_like(dkr_ref)

        lane = lax.broadcasted_iota(jnp.int32, (1, LANES), 1)
        row, col = _tile_iotas(t)
        causal = col <= row
        q_pair, q_quad, do_pair, lse_pair = qn_ref[...], qr_ref[...], do_ref[...], lse_ref[...]
        do_o = do_pair * o_ref[...]
        zero = jnp.zeros_like(q_pair)
        ropes = []
        for hh in range(2):
            in_head = (lane // HEAD_DIM) == hh
            in_rope = (lane // MLA_ROPE) == (hp % 2) * 2 + hh
            ropes.append(in_rope)
            qcat_ref[hh, :, 0:LANES] = jnp.where(in_head, q_pair, zero)
            qcat_ref[hh, :, LANES:2 * LANES] = jnp.where(in_rope, q_quad, zero)
            dob_ref[hh] = jnp.where(in_head, do_pair, 0.0).astype(BF16)
            delta_s[hh] = jnp.broadcast_to(jnp.sum(jnp.where(in_head, do_o, 0.0), axis=-1, keepdims=True), (t, LANES))
            lse_s[hh] = jnp.broadcast_to(
                jnp.sum(jnp.where(lane == hh * HEAD_DIM, lse_pair, 0.0), axis=-1, keepdims=True), (t, LANES))
        dq_s[...] = jnp.zeros_like(dq_s)
        reps = t // LANES

        def tile(j, masked):
            rows = pl.ds(pl.multiple_of(j * t, t), t)
            kcat = jnp.concatenate([kn_ref[rows, :], kr_ref[rows, :]], axis=1)
            v = v_ref[rows, :]
            sc = [_dot_nt(qcat_ref[hh], kcat) for hh in HEADS]
            dp = [_dot_nt(dob_ref[hh], v) for hh in HEADS]
            p = [jnp.exp(sc[hh] - jnp.concatenate([lse_s[hh]] * reps, axis=1)) for hh in HEADS]
            if masked:
                p = [jnp.where(causal, p[hh], 0.0) for hh in HEADS]
            ds = [(p[hh] * (dp[hh] - jnp.concatenate([delta_s[hh]] * reps, axis=1))).astype(BF16) for hh in HEADS]
            for hh in HEADS:
                dq_s[hh] += _dot(ds[hh], kcat)
            dkcat = _dot_tn(ds[0], qcat_ref[0]) + _dot_tn(ds[1], qcat_ref[1])
            dkn_ref[rows, :] += dkcat[:, 0:LANES]
            dkr_ref[rows, :] += dkcat[:, LANES:2 * LANES]
            dv_ref[rows, :] += _dot_tn(p[0].astype(BF16), dob_ref[0]) + _dot_tn(p[1].astype(BF16), dob_ref[1])

        def step(j, carry):
            tile(j, False)
            return carry

        lax.fori_loop(0, i, step, 0)
        tile(i, True)
        dqn_ref[...] = jnp.where((lane // HEAD_DIM) == 0, dq_s[0, :, 0:LANES], dq_s[1, :, 0:LANES])
        dqr_ref[...] = (jnp.where(ropes[0], dq_s[0, :, LANES:2 * LANES], 0.0)
                        + jnp.where(ropes[1], dq_s[1, :, LANES:2 * LANES], 0.0))

        @pl.when((hp == pairs - 1) & (i == nq - 1))
        def _():
            ride.finish()

    pair_block = pl.BlockSpec((t, LANES), lambda hp, i: (i, hp))
    landed_shapes, sems = _exchange_shapes(False, parts)
    outs = [jax.ShapeDtypeStruct((s, 512), F32), jax.ShapeDtypeStruct((pairs, s, LANES), F32),
            jax.ShapeDtypeStruct((s, 512), F32), jax.ShapeDtypeStruct((s, 512), F32),
            jax.ShapeDtypeStruct((pairs, s, LANES), F32)] + landed_shapes
    res = pl.pallas_call(
        body, name="mla_bwd", grid=(pairs, nq), out_shape=outs,
        in_specs=[pair_block, pl.BlockSpec((t, LANES), lambda hp, i: (i, hp // 2)),
                  pl.BlockSpec((s, LANES), lambda hp, i: (0, hp)), pl.BlockSpec((s, LANES), lambda hp, i: (0, 4 + hp)),
                  pl.BlockSpec((s, LANES), lambda hp, i: (0, 0)), pair_block, pair_block,
                  pl.BlockSpec((None, t, LANES), lambda hp, i: (hp, i, 0))] + [ANY] * n,
        out_specs=[pair_block, pl.BlockSpec((None, t, LANES), lambda hp, i: (hp, i, 0)),
                   pl.BlockSpec((s, LANES), lambda hp, i: (0, hp)), pl.BlockSpec((s, LANES), lambda hp, i: (0, hp)),
                   pl.BlockSpec((None, s, LANES), lambda hp, i: (hp, 0, 0))] + [ANY] * n,
        scratch_shapes=[pltpu.VMEM((2, t, 2 * LANES), BF16), pltpu.VMEM((2, t, LANES), BF16), pltpu.VMEM((2, t, LANES), F32),
                        pltpu.VMEM((2, t, LANES), F32), pltpu.VMEM((2, t, 2 * LANES), F32)] + sems,
        compiler_params=_cparams("arbitrary", "arbitrary"),
    )(qn, qr, kv, kv, kr, do, o, lse, *parts)
    return res[:5], res[5:]


def _ffn_bwd(dh2, dh2b, gate, up, h1, g_ffn, w_down, w_gate, w_up):
    s, d = h1.shape
    d_ff = gate.shape[1]
    tm = min(FFN_BWD_ROW_TILE, s)
    tf = _ffn_tile(d_ff)

    def act_body(dh2b_ref, gate_ref, up_ref, wd_ref, dgate_ref, dup_ref, act_ref):
        dact = _dot_nt(dh2b_ref[...], wd_ref[...])
        gate_v = gate_ref[...].astype(F32)
        up_v = up_ref[...].astype(F32)
        sig = jax.nn.sigmoid(gate_v)
        silu = gate_v * sig
        dup_ref[...] = (dact * silu).astype(BF16)
        dgate_ref[...] = (dact * up_v * (sig * (1.0 + gate_v * (1.0 - sig)))).astype(BF16)
        act_ref[...] = (silu * up_v).astype(BF16)

    ff = pl.BlockSpec((tm, tf), lambda j, r: (r, j))
    dgate, dup, act = pl.pallas_call(
        act_body, name="ffn_bwd_act", grid=(d_ff // tf, s // tm), out_shape=[jax.ShapeDtypeStruct((s, d_ff), BF16)] * 3,
        in_specs=[pl.BlockSpec((tm, d), lambda j, r: (r, 0)), ff, ff, pl.BlockSpec((tf, d), lambda j, r: (j, 0))],
        out_specs=[ff, ff, ff],
        compiler_params=_cparams("arbitrary", "arbitrary"),
    )(dh2b, gate, up, w_down)

    def df_body(dgate_ref, dup_ref, dh2_ref, h1_ref, g_ref, wg_ref, wu_ref, dh1_ref, dh1b_ref, dg_ref):
        df = _dot_nt(dgate_ref[...], wg_ref[...]) + _dot_nt(dup_ref[...], wu_ref[...])
        dx, dg = _rms_bwd(h1_ref[...], g_ref[...], df)
        dh1 = dh2_ref[...] + dx
        dh1_ref[...] = dh1
        dh1b_ref[...] = dh1.astype(BF16)
        _accumulate(dg_ref, dg, pl.program_id(0) == 0)

    outs = [jax.ShapeDtypeStruct((s, d), F32), jax.ShapeDtypeStruct((s, d), BF16), jax.ShapeDtypeStruct((1, d), F32)]
    dh1, dh1b, dg = pl.pallas_call(
        df_body, name="ffn_bwd_df", grid=(s // tm,), out_shape=outs,
        in_specs=[_row_spec(tm, d_ff), _row_spec(tm, d_ff), _row_spec(tm, d), _row_spec(tm, d), _full_spec((1, d)),
                  _full_spec(w_gate.shape), _full_spec(w_up.shape)],
        out_specs=[_row_spec(tm, d), _row_spec(tm, d), _full_spec((1, d))],
        compiler_params=_cparams("arbitrary"),
    )(dgate, dup, dh2, h1, g_ffn, w_gate, w_up)
    return dgate, dup, act, dh1, dh1b, dg


def _largest_tile(n, cap):
    for cand in range(cap, 0, -LANES):
        if n % cand == 0:
            return cand
    return n


def _tn_matmul(a, b, name):
    assert a.dtype == BF16 and b.dtype == BF16
    s, m = a.shape
    n = b.shape[1]
    if s * m * 2 <= TN_RESIDENT_BYTES:
        tm, tn = m, min(n, TN_BLOCK)
    else:
        tm, tn = TN_BLOCK, n

    def body(a_ref, b_ref, o_ref):
        o_ref[...] = _dot_tn(a_ref[...], b_ref[...])

    return pl.pallas_call(
        body, name=name, grid=(m // tm, n // tn), out_shape=jax.ShapeDtypeStruct((m, n), F32),
        in_specs=[pl.BlockSpec((s, tm), lambda i, j: (0, i)), pl.BlockSpec((s, tn), lambda i, j: (0, j))],
        out_specs=pl.BlockSpec((tm, tn), lambda i, j: (i, j)),
        compiler_params=_cparams("arbitrary", "arbitrary"),
    )(a, b)


def _attn_out_bwd(dh1, w_o, o_mla, o_sb, g_mla, g_sb):
    s, d = dh1.shape
    tm = min(ROW_TILE, s)

    def body(dh1_ref, wo_ref, oa_ref, ob_ref, ga_ref, gb_ref, doa_ref, dob_ref, dga_ref, dgb_ref):
        first = pl.program_id(0) == 0
        dh1b = dh1_ref[...]
        dxa, dga = _rms_bwd(oa_ref[...], ga_ref[...], _dot_nt(dh1b, wo_ref[0:512, :]))
        dxb, dgb = _rms_bwd(ob_ref[...], gb_ref[...], _dot_nt(dh1b, wo_ref[512:1024, :]))
        doa_ref[...] = dxa
        dob_ref[...] = dxb
        _accumulate(dga_ref, dga, first)
        _accumulate(dgb_ref, dgb, first)

    outs = [jax.ShapeDtypeStruct((s, 512), F32)] * 2 + [jax.ShapeDtypeStruct((1, 512), F32)] * 2
    return pl.pallas_call(
        body, name="attn_out_bwd", grid=(s // tm,), out_shape=outs,
        in_specs=[_row_spec(tm, d), _full_spec(w_o.shape), _row_spec(tm, 512), _row_spec(tm, 512),
                  _full_spec((1, 512)), _full_spec((1, 512))],
        out_specs=[_row_spec(tm, 512), _row_spec(tm, 512), _full_spec((1, 512)), _full_spec((1, 512))],
        compiler_params=_cparams("arbitrary"),
    )(dh1, w_o, o_mla, o_sb, g_mla, g_sb)


def _proj_in_bwd(dqn, dqr, dkn, dv, dkr, dq_sb, dk_sb, dv_sb, cq, ckv, x, dh1, cos, sin_a, sin_b,
                 g_q, g_kv, g_mix, w_uq, w_ukv, w_a):
    s, d = x.shape
    tm = min(PROJ_BWD_ROW_TILE, s)

    def body(dqn_ref, dqr_ref, dkn_ref, dv_ref, dkr_ref, dqs_ref, dks_ref, dvs_ref, cq_ref, ckv_ref, x_ref, dh1_ref,
             cos_ref, sa_ref, sb_ref, gq_ref, gkv_ref, gm_ref, wuq_ref, wukv_ref, wa_ref,
             dx_ref, dproj_ref, dq_ref, dkv_ref, dgq_ref, dgkv_ref, dgm_ref):
        first = pl.program_id(0) == 0
        lane = lax.broadcasted_iota(jnp.int32, (1, LANES), 1)
        cos_t, sa_t, sb_t = cos_ref[...], sa_ref[...], sb_ref[...]
        dq_ref[:, 0:512] = (dqn_ref[...] * MLA_SCALE).astype(BF16)
        for half in range(2):
            quad = (dqr_ref[2 * half] + dqr_ref[2 * half + 1]) * MLA_SCALE
            dq_ref[:, 512 + half * LANES:512 + (half + 1) * LANES] = _rope_t(quad, cos_t, sa_t, sb_t).astype(BF16)
        dcq, dgq = _rms_bwd(cq_ref[...], gq_ref[...], _dot_nt(dq_ref[...], wuq_ref[...]))
        _accumulate(dgq_ref, dgq, first)
        dkv_ref[:, 0:512] = dkn_ref[...].astype(BF16)
        dkv_ref[:, 512:1024] = dv_ref[...].astype(BF16)
        dckv, dgkv = _rms_bwd(ckv_ref[...], gkv_ref[...], _dot_nt(dkv_ref[...], wukv_ref[...]))
        _accumulate(dgkv_ref, dgkv, first)
        g = _rope_t(dkr_ref[0] + dkr_ref[1] + dkr_ref[2] + dkr_ref[3], cos_t, sa_t, sb_t)
        g = g + pltpu.roll(g, 96, 1) + pltpu.roll(g, 64, 1) + pltpu.roll(g, 32, 1)
        dproj_ref[:, 0:256] = dcq.astype(BF16)
        dproj_ref[:, 256:384] = dckv.astype(BF16)
        dproj_ref[:, 384:512] = jnp.where(lane < MLA_ROPE, g, 0.0).astype(BF16)
        dproj_ref[:, 512:1024] = (dqs_ref[...] * SB_SCALE).astype(BF16)
        dproj_ref[:, 1024:1536] = dks_ref[...].astype(BF16)
        dproj_ref[:, 1536:2048] = dvs_ref[...].astype(BF16)
        dxn, dgm = _rms_bwd(x_ref[...], gm_ref[...], _dot_nt(dproj_ref[...], wa_ref[...]))
        dx_ref[...] = dh1_ref[...] + dxn
        _accumulate(dgm_ref, dgm, first)

    quad_spec = pl.BlockSpec((4, tm, LANES), lambda r: (0, r, 0))
    outs = [jax.ShapeDtypeStruct((s, d), F32), jax.ShapeDtypeStruct((s, 2048), BF16), jax.ShapeDtypeStruct((s, 768), BF16),
            jax.ShapeDtypeStruct((s, 1024), BF16), jax.ShapeDtypeStruct((1, 256), F32), jax.ShapeDtypeStruct((1, 128), F32),
            jax.ShapeDtypeStruct((1, d), F32)]
    return pl.pallas_call(
        body, name="proj_in_bwd", grid=(s // tm,), out_shape=outs,
        in_specs=[_row_spec(tm, 512), quad_spec, _row_spec(tm, 512), _row_spec(tm, 512), quad_spec,
                  _row_spec(tm, 512), _row_spec(tm, 512), _row_spec(tm, 512), _row_spec(tm, 256), _row_spec(tm, 128),
                  _row_spec(tm, d), _row_spec(tm, d), _row_spec(tm, LANES), _row_spec(tm, LANES), _row_spec(tm, LANES),
                  _full_spec((1, 256)), _full_spec((1, 128)), _full_spec((1, d)),
                  _full_spec(w_uq.shape), _full_spec(w_ukv.shape), _full_spec(w_a.shape)],
        out_specs=[_row_spec(tm, d), _row_spec(tm, 2048), _row_spec(tm, 768), _row_spec(tm, 1024),
                   _full_spec((1, 256)), _full_spec((1, 128)), _full_spec((1, d))],
        compiler_params=_cparams("arbitrary"),
    )(dqn, dqr, dkn, dv, dkr, dq_sb, dk_sb, dv_sb, cq, ckv, x, dh1, cos, sin_a, sin_b, g_q, g_kv, g_mix,
      w_uq, w_ukv, w_a)


ANY = pl.BlockSpec(memory_space=pl.ANY)


def _place():
    return lax.axis_index("x"), lax.axis_index("y"), lax.axis_index("c")


def _all_gather(shards, name):
    n = len(shards)

    def body(*refs):
        ins, outs = refs[:n], refs[n:2 * n]
        send_sems, recv_sems, local_sems = refs[2 * n:]
        x, y, c = _place()
        me, sibling = (x, y, c), (x, y, 1 - c)
        chips = [(1 - x, y), (x, 1 - y), (1 - x, 1 - y)]

        def slot(a, px, py, pc):
            return outs[a].at[4 * px + 2 * py + pc]

        def copy(a, k, block, to, src=None):
            return pltpu.make_async_remote_copy(
                src_ref=slot(a, *block) if src is None else src, dst_ref=slot(a, *block),
                send_sem=send_sems.at[a, k], recv_sem=recv_sems.at[a, k], device_id=to, device_id_type=MESH)

        mine, first, passed = [], [], []
        for a in range(n):
            own = pltpu.make_async_copy(ins[a], slot(a, *me), local_sems.at[a])
            own.start()
            mine.append(own)
            cps = [copy(a, 0, me, sibling, src=ins[a])]
            cps += [copy(a, 1 + j, me, (*chip, c), src=ins[a]) for j, chip in enumerate(chips)]
            for cp in cps:
                cp.start()
            first += cps
        for a in range(n):
            for j, chip in enumerate(chips):
                copy(a, 1 + j, (*chip, c), me).wait_recv()
                fwd = copy(a, 4 + j, (*chip, c), sibling)
                fwd.start()
                passed.append(fwd)
        for a in range(n):
            copy(a, 0, sibling, me).wait_recv()
            for j, chip in enumerate(chips):
                copy(a, 4 + j, (*chip, 1 - c), me).wait_recv()
        for cp in first + passed:
            cp.wait_send()
        for own in mine:
            own.wait()

    return pl.pallas_call(
        body, name=name,
        out_shape=[jax.ShapeDtypeStruct((N_DEV,) + v.shape, v.dtype) for v in shards],
        in_specs=[ANY] * n, out_specs=[ANY] * n,
        scratch_shapes=[pltpu.SemaphoreType.DMA((n, 7)), pltpu.SemaphoreType.DMA((n, 7)), pltpu.SemaphoreType.DMA((n,))],
    )(*shards)


class _Exchange:
    def __init__(self, gather, ins, outs, send_sems, recv_sems, local_sems):
        self.gather, self.ins, self.outs = gather, ins, outs
        self.sems = (send_sems, recv_sems, local_sems)
        x, y, c = _place()
        self.me = 4 * x + 2 * y + c
        self.peers = []
        for k in range(1, N_DEV):
            px = 1 - x if k & 4 else x
            py = 1 - y if k & 2 else y
            pc = 1 - c if k & 1 else c
            self.peers.append(((px, py, pc), 4 * px + 2 * py + pc))

    def _remote(self, a, k, landing):
        send_sems, recv_sems, _ = self.sems
        where, number = self.peers[k]
        src = self.ins[a] if self.gather else self.ins[a].at[number]
        return pltpu.make_async_remote_copy(
            src_ref=src, dst_ref=self.outs[a].at[landing], send_sem=send_sems.at[a, k], recv_sem=recv_sems.at[a, k],
            device_id=where, device_id_type=MESH)

    def _local(self, a):
        src = self.ins[a] if self.gather else self.ins[a].at[self.me]
        return pltpu.make_async_copy(src, self.outs[a].at[self.me], self.sems[2].at[a])

    def start(self):
        for a in range(len(self.ins)):
            self._local(a).start()
            for k in range(N_DEV - 1):
                self._remote(a, k, self.me).start()

    def finish(self):
        for a in range(len(self.ins)):
            for k in range(N_DEV - 1):
                self._remote(a, k, self.peers[k][1]).wait_recv()
            for k in range(N_DEV - 1):
                self._remote(a, k, self.me).wait_send()
            self._local(a).wait()


def _exchange_shapes(gather, arrays):
    out_shape = [jax.ShapeDtypeStruct(((N_DEV,) + v.shape) if gather else v.shape, v.dtype) for v in arrays]
    n = len(arrays)
    sems = [pltpu.SemaphoreType.DMA((n, N_DEV - 1)), pltpu.SemaphoreType.DMA((n, N_DEV - 1)), pltpu.SemaphoreType.DMA((n,))]
    return out_shape, sems


def _exchange(gather, arrays, name):
    n = len(arrays)

    def body(*refs):
        ex = _Exchange(gather, refs[:n], refs[n:2 * n], *refs[2 * n:])
        ex.start()
        ex.finish()

    out_shape, sems = _exchange_shapes(gather, arrays)
    return pl.pallas_call(body, name=name, out_shape=out_shape, in_specs=[ANY] * n, out_specs=[ANY] * n,
                          scratch_shapes=sems)(*arrays)


def _grad_row_tile(rows):
    return _largest_tile_rows(rows, 256)


def _largest_tile_rows(rows, cap):
    for cand in range(cap, 0, -8):
        if rows % cand == 0:
            return cand
    return rows


def _adamw_math(w, g, m, v):
    m_new = ADAM_B1 * m + (1.0 - ADAM_B1) * g
    v_new = ADAM_B2 * v + (1.0 - ADAM_B2) * (g * g)
    m_hat = m_new / (1.0 - ADAM_B1 ** ADAM_STEP)
    v_hat = v_new / (1.0 - ADAM_B2 ** ADAM_STEP)
    delta = -ADAM_LR * (m_hat / (jnp.sqrt(v_hat) + ADAM_EPS) + ADAM_WD * w)
    return delta, m_new, v_new


def _adamw(slots, w, m, v, name):
    k, r, cdim = slots.shape
    tr = _grad_row_tile(r)

    def body(s_ref, w_ref, m_ref, v_ref, g_ref, d_ref, mo_ref, vo_ref):
        g = s_ref[0].astype(F32)
        for q in range(1, k):
            g = g + s_ref[q].astype(F32)
        g_ref[...] = g
        d_ref[...], mo_ref[...], vo_ref[...] = _adamw_math(w_ref[...], g, m_ref[...], v_ref[...])

    blk = pl.BlockSpec((tr, cdim), lambda i: (i, 0))
    return pl.pallas_call(
        body, name=name, grid=(r // tr,), out_shape=[jax.ShapeDtypeStruct((r, cdim), F32)] * 4,
        in_specs=[pl.BlockSpec((k, tr, cdim), lambda i: (0, i, 0)), blk, blk, blk], out_specs=[blk] * 4,
        compiler_params=_cparams("arbitrary"),
    )(slots, w, m, v)


def _stack_cols(g):
    n, r, c = g.shape
    return g.transpose(1, 0, 2).reshape(r, n * c)


def _split_cols(w):
    r, nc = w.shape
    return w.reshape(r, N_DEV, nc // N_DEV).transpose(1, 0, 2)


def _rope_tables(positions):
    inv_freq = ROPE_THETA ** (-jnp.arange(0, MLA_ROPE, 2, dtype=F32) / MLA_ROPE)
    ang = positions.astype(F32).reshape(-1, 1) * inv_freq[None, :]
    cos, sin, zero = jnp.cos(ang), jnp.sin(ang), jnp.zeros_like(ang)
    reps = LANES // MLA_ROPE
    return (jnp.tile(jnp.concatenate([cos, cos], axis=1), (1, reps)),
            jnp.tile(jnp.concatenate([-sin, zero], axis=1), (1, reps)),
            jnp.tile(jnp.concatenate([zero, sin], axis=1), (1, reps)))


def _local_step(x, positions, loss_target, gains, g_in, g_uq, g_ukv, late_shards):
    norm_mix, q_norm, kv_norm, out_mla, out_sb, norm_ffn, norm_final = gains
    d = x.shape[1]
    w_in = _stack_cols(g_in)
    w_a = jnp.concatenate([w_in[:, :416], jnp.zeros((d, 96), BF16), w_in[:, 416:]], axis=1)
    w_uq = jnp.concatenate([g_uq[:, :, :MLA_NOPE].transpose(1, 0, 2).reshape(Q_LORA, -1),
                            g_uq[:, :, MLA_NOPE:].transpose(1, 0, 2).reshape(Q_LORA, -1)], axis=1)
    w_ukv = jnp.concatenate([g_ukv[:, :, :MLA_NOPE].transpose(1, 0, 2).reshape(KV_LORA, -1),
                             g_ukv[:, :, MLA_NOPE:].transpose(1, 0, 2).reshape(KV_LORA, -1)], axis=1)
    cos, sin_a, sin_b = _rope_tables(positions)

    u, cq, ckv, cqn, ckvn, qn, qr, kv, kr, qkv_sb = _proj_in_fwd(x, norm_mix, w_a, q_norm, w_uq, kv_norm, w_ukv, cos, sin_a, sin_b)
    o_mla, lse, (g_o, g_gate, g_up, g_down) = _mla_fwd(qn, qr, kv, kr, late_shards)
    w_o = g_o.reshape(-1, d)
    w_gate, w_up = _stack_cols(g_gate), _stack_cols(g_up)
    w_down = g_down.reshape(-1, d)
    o_sb, tot, swept = _sb_fwd(qkv_sb)
    merged, h1, f = _attn_out_fwd(o_mla, o_sb, out_mla, out_sb, w_o, x, norm_ffn)
    gate, up, h2 = _ffn_fwd(f, h1, w_gate, w_up, w_down)
    loss, dh2, dh2b, dg_final = _final_loss(h2, loss_target, norm_final.reshape(1, d))

    dgate, dup, act, dh1, dh1b, dg_ffn = _ffn_bwd(dh2, dh2b, gate, up, h1, norm_ffn, w_down, w_gate, w_up)
    dw_down = _tn_matmul(act, dh2b, "dw_down")
    dw_gate = _tn_matmul(f, dgate, "dw_gate")
    dw_up = _tn_matmul(f, dup, "dw_up")
    do_mla, do_sb, dg_mla, dg_sb = _attn_out_bwd(dh1b, w_o, o_mla, o_sb, out_mla, out_sb)
    dw_o = _tn_matmul(merged, dh1b, "dw_o")
    dq_sb, dk_sb, dv_sb = _sb_bwd(qkv_sb, do_sb, tot, swept)
    early = [dw_o.reshape(N_DEV, -1, d), _split_cols(dw_gate), _split_cols(dw_up), dw_down.reshape(N_DEV, -1, d)]
    (dqn, dqr, dkn, dv, dkr), landed = _mla_bwd(qn, qr, kv, kr, do_mla, o_mla, lse, [p.astype(BF16) for p in early])
    dx, dproj, dq, dkv, dg_q, dg_kv, dg_mix = _proj_in_bwd(
        dqn, dqr, dkn, dv, dkr, dq_sb, dk_sb, dv_sb, cq, ckv, x, dh1, cos, sin_a, sin_b,
        q_norm, kv_norm, norm_mix, w_uq, w_ukv, w_a)
    dw_a = _tn_matmul(u, dproj, "dw_in")
    dw_uq = _tn_matmul(cqn, dq, "dw_uq")
    dw_ukv = _tn_matmul(ckvn, dkv, "dw_ukv")

    p_in = _split_cols(jnp.concatenate([dw_a[:, :416], dw_a[:, 512:]], axis=1))
    p_uq = jnp.concatenate([dw_uq[:, :512].reshape(Q_LORA, MLA_HEADS, MLA_NOPE),
                            dw_uq[:, 512:].reshape(Q_LORA, MLA_HEADS, MLA_ROPE)], axis=2).transpose(1, 0, 2)
    p_ukv = jnp.concatenate([dw_ukv[:, :512].reshape(KV_LORA, MLA_HEADS, MLA_NOPE),
                             dw_ukv[:, 512:].reshape(KV_LORA, MLA_HEADS, HEAD_DIM)], axis=2).transpose(1, 0, 2)
    late = [p.astype(BF16) for p in (p_in, p_uq, p_ukv)]
    gain_grads = [dg_mix, dg_q, dg_kv, dg_mla, dg_sb, dg_ffn, dg_final]
    return loss, dx, list(landed), late, gain_grads


def kernel(x, positions, norm_mix, w_in, q_latent_norm, w_uq, kv_latent_norm, w_ukv, out_norm_mla, out_norm_sb, w_o, norm_ffn, w_gate, w_up, w_down, norm_final, loss_target, m_norm_mix, m_w_in, m_q_latent_norm, m_w_uq, m_kv_latent_norm, m_w_ukv, m_out_norm_mla, m_out_norm_sb, m_w_o, m_norm_ffn, m_w_gate, m_w_up, m_w_down, m_norm_final, v_norm_mix, v_w_in, v_q_latent_norm, v_w_uq, v_kv_latent_norm, v_w_ukv, v_out_norm_mla, v_out_norm_sb, v_w_o, v_norm_ffn, v_w_gate, v_w_up, v_w_down, v_norm_final):
    mats = [w_in, w_uq, w_ukv, w_o, w_gate, w_up, w_down]
    mat_m = [m_w_in, m_w_uq, m_w_ukv, m_w_o, m_w_gate, m_w_up, m_w_down]
    mat_v = [v_w_in, v_w_uq, v_w_ukv, v_w_o, v_w_gate, v_w_up, v_w_down]
    mat_names = ["w_in", "w_uq", "w_ukv", "w_o", "w_gate", "w_up", "w_down"]
    gains = [norm_mix, q_latent_norm, kv_latent_norm, out_norm_mla, out_norm_sb, norm_ffn, norm_final]
    gain_m = [m_norm_mix, m_q_latent_norm, m_kv_latent_norm, m_out_norm_mla, m_out_norm_sb, m_norm_ffn, m_norm_final]
    gain_v = [v_norm_mix, v_q_latent_norm, v_kv_latent_norm, v_out_norm_mla, v_out_norm_sb, v_norm_ffn, v_norm_final]

    shards = [w[0].astype(BF16) for w in mats]
    g_in, g_uq, g_ukv = _all_gather(shards[:3], "weight_all_gather")

    gains2d = [g.reshape(1, -1) for g in gains]
    loss_part, dx, landed, late, gain_grads = _local_step(
        x[0], positions[0], loss_target[0], gains2d, g_in, g_uq, g_ukv, shards[3:])

    slots = list(_exchange(False, late, "grad_scatter")) + landed
    mat_out = [_adamw(sl, w[0], m[0], v[0], "adamw_" + nm)
               for sl, w, m, v, nm in zip(slots, mats, mat_m, mat_v, mat_names)]

    sizes = [g.size for g in gains]
    used = sum(sizes) + LANES
    rows = -(-used // (8 * LANES)) * 8

    def pack(vals, tail):
        flat = jnp.concatenate([v.reshape(-1) for v in vals] + [tail])
        return jnp.pad(flat, (0, rows * LANES - flat.size)).reshape(rows, LANES)

    zeros_tail = jnp.zeros((LANES,), F32)
    small = _all_gather([pack(gain_grads, loss_part.reshape(-1))], "gain_all_gather")[0]
    g_s, d_s, m_s, v_s = _adamw(small, pack(gains, zeros_tail), pack(gain_m, zeros_tail), pack(gain_v, zeros_tail), "adamw_gains")

    def unpack(packed):
        flat = packed.reshape(-1)
        outs, off = [], 0
        for g, n in zip(gains, sizes):
            outs.append(flat[off:off + n].reshape(g.shape))
            off += n
        return outs

    loss = g_s.reshape(-1)[sum(sizes)]

    order = ["norm_mix", "w_in", "q_latent_norm", "w_uq", "kv_latent_norm", "w_ukv", "out_norm_mla", "out_norm_sb",
             "w_o", "norm_ffn", "w_gate", "w_up", "w_down", "norm_final"]
    gain_names = ["norm_mix", "q_latent_norm", "kv_latent_norm", "out_norm_mla", "out_norm_sb", "norm_ffn", "norm_final"]
    result = [loss, dx[None]]
    for kind in range(4):
        small_parts = dict(zip(gain_names, unpack([g_s, d_s, m_s, v_s][kind])))
        mat_parts = {nm: out[kind][None] for nm, out in zip(mat_names, mat_out)}
        result += [small_parts[nm] if nm in small_parts else mat_parts[nm] for nm in order]
    return tuple(result)
```

```python
import functools
import math

import jax
import jax.numpy as jnp
from jax import lax
from jax.experimental import pallas as pl
from jax.experimental.pallas import tpu as pltpu

F32 = jnp.float32
BF16 = jnp.bfloat16
MESH = pl.DeviceIdType.MESH

EPS = 1e-6
ROPE_THETA = 10000.0
MLA_HEADS = 8
MLA_NOPE = 64
MLA_ROPE = 32
SB_HEADS = 8
HEAD_DIM = 64
Q_LORA = 256
KV_LORA = 128
MLA_SCALE = 1.0 / math.sqrt(MLA_NOPE + MLA_ROPE)
SB_SCALE = 1.0 / math.sqrt(HEAD_DIM)
LOG2E = math.log2(math.e)
SB_DEAD = -160.0
N_DEV = 8

ADAM_LR = 0.001
ADAM_B1 = 0.9
ADAM_B2 = 0.999
ADAM_EPS = 1e-08
ADAM_WD = 0.01
ADAM_STEP = 10

LANES = 128
ATT_TILE = 512
SB_TILE = 512
TRI = 256
ROW_TILE = 512
FFN_BWD_ROW_TILE = 256
PROJ_BWD_ROW_TILE = 256
TN_BLOCK = 256
TN_RESIDENT_BYTES = 16 * 1024 * 1024
VMEM_LIMIT = 56 * 1024 * 1024
NEG = -1e30


def _cparams(*sem):
    return pltpu.CompilerParams(dimension_semantics=sem, vmem_limit_bytes=VMEM_LIMIT)


def _dot(a, b):
    return jnp.dot(a, b, preferred_element_type=F32)


def _dot_nt(a, b):
    return lax.dot_general(a, b, (((1,), (1,)), ((), ())), preferred_element_type=F32)


def _dot_tn(a, b):
    return lax.dot_general(a, b, (((0,), (0,)), ((), ())), preferred_element_type=F32)


def _rms(x, g):
    r = lax.rsqrt(jnp.mean(x * x, axis=-1, keepdims=True) + EPS)
    return x * r * g


def _rms_bwd(x, g, dy):
    r = lax.rsqrt(jnp.mean(x * x, axis=-1, keepdims=True) + EPS)
    n = x * r
    dn = dy * g
    dx = r * (dn - n * jnp.mean(dn * n, axis=-1, keepdims=True))
    return dx, jnp.sum(dy * n, axis=0, keepdims=True)


def _rope(x, cos, sin_a, sin_b):
    return x * cos + pltpu.roll(x, 112, 1) * sin_a + pltpu.roll(x, 16, 1) * sin_b


def _rope_t(g, cos, sin_a, sin_b):
    return g * cos + pltpu.roll(g * sin_a, 16, 1) + pltpu.roll(g * sin_b, 112, 1)


def _row_spec(tm, width):
    return pl.BlockSpec((tm, width), lambda r: (r, 0))


def _full_spec(shape):
    return pl.BlockSpec(shape, lambda *_: (0,) * len(shape))


def _accumulate(ref, val, first):
    @pl.when(first)
    def _():
        ref[...] = val

    @pl.when(jnp.logical_not(first))
    def _():
        ref[...] += val


def _proj_in_fwd(x, g_mix, w_a, g_q, w_uq, g_kv, w_ukv, cos, sin_a, sin_b):
    s, d = x.shape
    tm = min(ROW_TILE, s)

    def body(x_ref, gm_ref, wa_ref, gq_ref, wuq_ref, gkv_ref, wukv_ref, cos_ref, sa_ref, sb_ref,
             u_ref, cq_ref, ckv_ref, cqn_ref, ckvn_ref, qn_ref, qr_ref, kv_ref, kr_ref, sbq_ref):
        u = _rms(x_ref[...], gm_ref[...]).astype(BF16)
        u_ref[...] = u
        cq = _dot(u, wa_ref[:, 0:256])
        ckv = _dot(u, wa_ref[:, 256:384])
        kr = _dot(u, wa_ref[:, 384:512])
        cq_ref[...] = cq
        ckv_ref[...] = ckv
        cqn = _rms(cq, gq_ref[...]).astype(BF16)
        ckvn = _rms(ckv, gkv_ref[...]).astype(BF16)
        cqn_ref[...] = cqn
        ckvn_ref[...] = ckvn
        cos_t, sa_t, sb_t = cos_ref[...], sa_ref[...], sb_ref[...]
        qn_ref[...] = (_dot(cqn, wuq_ref[:, 0:512]) * MLA_SCALE).astype(BF16)
        for half in range(2):
            lo = 512 + half * LANES
            qr = _dot(cqn, wuq_ref[:, lo:lo + LANES])
            qr_ref[:, half * LANES:(half + 1) * LANES] = (_rope(qr, cos_t, sa_t, sb_t) * MLA_SCALE).astype(BF16)
        kv_ref[...] = _dot(ckvn, wukv_ref[...]).astype(BF16)
        krt = kr + pltpu.roll(kr, 32, 1) + pltpu.roll(kr, 64, 1) + pltpu.roll(kr, 96, 1)
        kr_ref[...] = _rope(krt, cos_t, sa_t, sb_t).astype(BF16)
        sbq_ref[:, 0:512] = (_dot(u, wa_ref[:, 512:1024]) * (SB_SCALE * LOG2E)).astype(BF16)
        sbq_ref[:, 512:1536] = _dot(u, wa_ref[:, 1024:2048]).astype(BF16)

    outs = [
        jax.ShapeDtypeStruct((s, d), BF16),
        jax.ShapeDtypeStruct((s, 256), F32),
        jax.ShapeDtypeStruct((s, 128), F32),
        jax.ShapeDtypeStruct((s, 256), BF16),
        jax.ShapeDtypeStruct((s, 128), BF16),
        jax.ShapeDtypeStruct((s, 512), BF16),
        jax.ShapeDtypeStruct((s, 256), BF16),
        jax.ShapeDtypeStruct((s, 1024), BF16),
        jax.ShapeDtypeStruct((s, 128), BF16),
        jax.ShapeDtypeStruct((s, 1536), BF16),
    ]
    return pl.pallas_call(
        body, name="proj_in_fwd", grid=(s // tm,), out_shape=outs,
        in_specs=[_row_spec(tm, d), _full_spec(g_mix.shape), _full_spec(w_a.shape), _full_spec(g_q.shape),
                  _full_spec(w_uq.shape), _full_spec(g_kv.shape), _full_spec(w_ukv.shape),
                  _row_spec(tm, LANES), _row_spec(tm, LANES), _row_spec(tm, LANES)],
        out_specs=[_row_spec(tm, o.shape[1]) for o in outs],
        compiler_params=_cparams("arbitrary"),
    )(x, g_mix, w_a, g_q, w_uq, g_kv, w_ukv, cos, sin_a, sin_b)


def _attn_out_fwd(o_mla, o_sb, g_mla, g_sb, w_o, x, g_ffn):
    s, d = x.shape
    tm = min(ROW_TILE, s)

    def body(oa_ref, ob_ref, ga_ref, gb_ref, wo_ref, x_ref, gf_ref, merged_ref, h1_ref, f_ref):
        na = _rms(oa_ref[...], ga_ref[...]).astype(BF16)
        nb = _rms(ob_ref[...], gb_ref[...]).astype(BF16)
        merged_ref[:, 0:512] = na
        merged_ref[:, 512:1024] = nb
        h1 = x_ref[...] + _dot(na, wo_ref[0:512, :]) + _dot(nb, wo_ref[512:1024, :])
        h1_ref[...] = h1
        f_ref[...] = _rms(h1, gf_ref[...]).astype(BF16)

    outs = [jax.ShapeDtypeStruct((s, d), BF16), jax.ShapeDtypeStruct((s, d), F32), jax.ShapeDtypeStruct((s, d), BF16)]
    return pl.pallas_call(
        body, name="attn_out_fwd", grid=(s // tm,), out_shape=outs,
        in_specs=[_row_spec(tm, 512), _row_spec(tm, 512), _full_spec(g_mla.shape), _full_spec(g_sb.shape),
                  _full_spec(w_o.shape), _row_spec(tm, d), _full_spec(g_ffn.shape)],
        out_specs=[_row_spec(tm, d)] * 3,
        compiler_params=_cparams("arbitrary"),
    )(o_mla, o_sb, g_mla, g_sb, w_o, x, g_ffn)


def _ffn_tile(d_ff):
    return d_ff // 2 if (d_ff // 2) % LANES == 0 else d_ff


def _ffn_fwd(f, h1, w_gate, w_up, w_down):
    s, d = h1.shape
    d_ff = w_gate.shape[1]
    tm = min(ROW_TILE, s)
    tf = _ffn_tile(d_ff)

    def body(f_ref, h1_ref, wg_ref, wu_ref, wd_ref, gate_ref, up_ref, h2_ref):
        j = pl.program_id(1)
        fb = f_ref[...]
        gate = _dot(fb, wg_ref[...])
        up = _dot(fb, wu_ref[...])
        gate_ref[...] = gate.astype(BF16)
        up_ref[...] = up.astype(BF16)
        act = (gate * jax.nn.sigmoid(gate) * up).astype(BF16)
        part = _dot(act, wd_ref[...])

        @pl.when(j == 0)
        def _():
            h2_ref[...] = h1_ref[...] + part

        @pl.when(j != 0)
        def _():
            h2_ref[...] += part

    outs = [jax.ShapeDtypeStruct((s, d_ff), BF16), jax.ShapeDtypeStruct((s, d_ff), BF16), jax.ShapeDtypeStruct((s, d), F32)]
    return pl.pallas_call(
        body, name="ffn_fwd", grid=(s // tm, d_ff // tf), out_shape=outs,
        in_specs=[pl.BlockSpec((tm, d), lambda r, j: (r, 0)), pl.BlockSpec((tm, d), lambda r, j: (r, 0)),
                  pl.BlockSpec((d, tf), lambda r, j: (0, j)), pl.BlockSpec((d, tf), lambda r, j: (0, j)),
                  pl.BlockSpec((tf, d), lambda r, j: (j, 0))],
        out_specs=[pl.BlockSpec((tm, tf), lambda r, j: (r, j)), pl.BlockSpec((tm, tf), lambda r, j: (r, j)),
                   pl.BlockSpec((tm, d), lambda r, j: (r, 0))],
        compiler_params=_cparams("arbitrary", "arbitrary"),
    )(f, h1, w_gate, w_up, w_down)


def _final_loss(h2, target, g_final):
    s, d = h2.shape
    tm = min(ROW_TILE, s)

    def body(h2_ref, t_ref, g_ref, loss_ref, dh2_ref, dh2b_ref, dg_ref):
        first = pl.program_id(0) == 0
        h2v = h2_ref[...]
        g = g_ref[...]
        diff = _rms(h2v, g) - t_ref[...]
        part = 0.5 * jnp.sum(jnp.mean(diff * diff, axis=-1, keepdims=True), axis=0, keepdims=True)
        _accumulate(loss_ref, jnp.broadcast_to(part, loss_ref.shape), first)
        dx, dg = _rms_bwd(h2v, g, diff * (1.0 / d))
        dh2_ref[...] = dx
        dh2b_ref[...] = dx.astype(BF16)
        _accumulate(dg_ref, dg, first)

    outs = [jax.ShapeDtypeStruct((1, LANES), F32), jax.ShapeDtypeStruct((s, d), F32), jax.ShapeDtypeStruct((s, d), BF16),
            jax.ShapeDtypeStruct((1, d), F32)]
    return pl.pallas_call(
        body, name="final_loss", grid=(s // tm,), out_shape=outs,
        in_specs=[_row_spec(tm, d), _row_spec(tm, d), _full_spec((1, d))],
        out_specs=[_full_spec((1, LANES)), _row_spec(tm, d), _row_spec(tm, d), _full_spec((1, d))],
        compiler_params=_cparams("arbitrary"),
    )(h2, target, g_final)


def _tile_iotas(t):
    return lax.broadcasted_iota(jnp.int32, (t, t), 0), lax.broadcasted_iota(jnp.int32, (t, t), 1)


def _stacked_mask(t, strict):
    row = lax.broadcasted_iota(jnp.int32, (2 * t, t), 0)
    col = lax.broadcasted_iota(jnp.int32, (2 * t, t), 1)
    row = jnp.where(row >= t, row - t, row)
    return col < row if strict else col <= row


def _mla_fwd(qn, qr, kv, kr, shards):
    s = qn.shape[0]
    t = min(ATT_TILE, s)
    pairs = MLA_HEADS // 2
    nq = s // t
    n = len(shards)

    def body(*refs):
        qn_ref, qr_ref, kn_ref, v_ref, kr_ref = refs[:5]
        o_ref, lse_ref = refs[5 + n:7 + n]
        qcat_ref, m_ref, l_ref, acc_ref = refs[7 + 2 * n:11 + 2 * n]
        hp, i = pl.program_id(0), pl.program_id(1)
        ride = _Exchange(True, refs[5:5 + n], refs[7 + n:7 + 2 * n], *refs[11 + 2 * n:])

        @pl.when((hp == 0) & (i == 0))
        def _():
            ride.start()

        lane = lax.broadcasted_iota(jnp.int32, (1, LANES), 1)
        row, col = _tile_iotas(t)
        causal = col <= row
        q_pair, q_quad = qn_ref[...], qr_ref[...]
        zero = jnp.zeros_like(q_pair)
        for hh in range(2):
            in_head = (lane // HEAD_DIM) == hh
            in_rope = (lane // MLA_ROPE) == (hp % 2) * 2 + hh
            qcat_ref[hh * t:(hh + 1) * t, 0:LANES] = jnp.where(in_head, q_pair, zero)
            qcat_ref[hh * t:(hh + 1) * t, LANES:2 * LANES] = jnp.where(in_rope, q_quad, zero)
        m_ref[...] = jnp.full_like(m_ref, NEG)
        l_ref[...] = jnp.zeros_like(l_ref)
        acc_ref[...] = jnp.zeros_like(acc_ref)

        def tile(j, width, masked):
            rows = pl.ds(pl.multiple_of(j * t, t), width * t)
            kcat = jnp.concatenate([kn_ref[rows, :], kr_ref[rows, :]], axis=1)
            v_ones = jnp.concatenate([v_ref[rows, :], jnp.ones((width * t, LANES), BF16)], axis=1)
            scores = [_dot_nt(qcat_ref[hh * t:(hh + 1) * t, :], kcat) for hh in range(2)]
            for hh in range(2):
                half = slice(hh * t, (hh + 1) * t)
                sc = jnp.where(causal, scores[hh], NEG) if masked else scores[hh]
                m = m_ref[half, :]
                m_new = jnp.maximum(m, jnp.max(sc, axis=-1, keepdims=True))
                alpha = jnp.exp(m - m_new)
                p = jnp.exp(sc - jnp.concatenate([m_new] * (width * t // LANES), axis=1))
                pv = _dot(p.astype(BF16), v_ones)
                l_ref[half, :] = alpha * l_ref[half, :] + pv[:, LANES:]
                acc_ref[half, :] = alpha * acc_ref[half, :] + pv[:, :LANES]
                m_ref[half, :] = m_new

        tile(i, 1, True)

        def step(n, carry):
            tile(2 * n, 2, False)
            return carry

        lax.fori_loop(0, i // 2, step, 0)

        @pl.when(i % 2 == 1)
        def _():
            tile(i - 1, 1, False)
        first = (lane // HEAD_DIM) == 0
        o = acc_ref[...] / l_ref[...]
        lse = m_ref[...] + jnp.log(l_ref[...])
        o_ref[...] = jnp.where(first, o[0:t], o[t:2 * t])
        lse_ref[...] = jnp.where(first, lse[0:t], lse[t:2 * t])

        @pl.when((hp == pairs - 1) & (i == nq - 1))
        def _():
            ride.finish()

    gathered_shapes, sems = _exchange_shapes(True, shards)
    outs = [jax.ShapeDtypeStruct((s, 512), F32), jax.ShapeDtypeStruct((pairs, s, LANES), F32)] + gathered_shapes
    res = pl.pallas_call(
        body, name="mla_fwd", grid=(pairs, nq), out_shape=outs,
        in_specs=[pl.BlockSpec((t, LANES), lambda hp, i: (i, hp)), pl.BlockSpec((t, LANES), lambda hp, i: (i, hp // 2)),
                  pl.BlockSpec((s, LANES), lambda hp, i: (0, hp)), pl.BlockSpec((s, LANES), lambda hp, i: (0, 4 + hp)),
                  pl.BlockSpec((s, LANES), lambda hp, i: (0, 0))] + [ANY] * n,
        out_specs=[pl.BlockSpec((t, LANES), lambda hp, i: (i, hp)), pl.BlockSpec((None, t, LANES), lambda hp, i: (hp, i, 0))]
        + [ANY] * n,
        scratch_shapes=[pltpu.VMEM((2 * t, 2 * LANES), BF16), pltpu.VMEM((2 * t, LANES), F32), pltpu.VMEM((2 * t, LANES), F32),
                        pltpu.VMEM((2 * t, LANES), F32)] + sems,
        compiler_params=_cparams("arbitrary", "arbitrary"),
    )(qn, qr, kv, kv, kr, *shards)
    return res[0], res[1], res[2:]


HEADS = (0, 1)


def _sb_logs(z2, strict, masked):
    log_b = jnp.minimum(z2, 0.0) - jnp.log2(1.0 + jnp.exp2(-jnp.abs(z2)))
    log_1m = log_b - z2
    if masked:
        log_1m = jnp.where(strict, log_1m, 0.0)
    return log_1m, log_b


def _block_totals(x):
    t, w = x.shape
    nb = max(w // TRI, 1)
    bw = w // nb
    blocks = [x[:, b * bw:(b + 1) * bw] for b in range(nb)]
    totals = [jnp.broadcast_to(jnp.sum(blk, axis=-1, keepdims=True), (t, LANES)) for blk in blocks]
    whole = totals[0]
    for tot in totals[1:]:
        whole = whole + tot
    return blocks, totals, whole


def _running_sums(blocks, totals, tri, carry, suffix):
    nb = len(blocks)
    reps = blocks[0].shape[1] // LANES
    outs = [None] * nb
    run = carry
    for b in (range(nb - 1, -1, -1) if suffix else range(nb)):
        outs[b] = _dot(blocks[b].astype(BF16), tri) + jnp.concatenate([run] * reps, axis=1)
        run = run + totals[b]
    return outs[0] if nb == 1 else jnp.concatenate(outs, axis=1)


def _tri(t, rel):
    n = min(TRI, t)
    row, col = _tile_iotas(n)
    return rel(row, col).astype(BF16)


def _sweep_width(t):
    return t // 2 if t // 2 >= TRI else t


def _sb_fwd(qkv):
    s = qkv.shape[0]
    t = min(SB_TILE, s)
    sw = _sweep_width(t)
    pairs = SB_HEADS // 2

    def body(q_ref, k_ref, v_ref, o_ref, tot_ref, cnt_ref, qm_ref, right_ref, acc_ref):
        i = pl.program_id(1)
        lane = lax.broadcasted_iota(jnp.int32, (1, LANES), 1)
        row, col = _tile_iotas(t)
        strict = col < row
        t_suffix = _tri(t, lambda r, c: r > c)
        q_pair = q_ref[...]
        for hh in range(2):
            qm_ref[hh] = jnp.where((lane // HEAD_DIM) == hh, q_pair, jnp.zeros_like(q_pair))
        right_ref[...] = jnp.zeros_like(right_ref)
        acc_ref[...] = jnp.zeros_like(acc_ref)

        def tile(start, width, masked):
            rows = pl.ds(pl.multiple_of(start, width), width)
            k, v = k_ref[rows, :], v_ref[rows, :]
            for hh in HEADS:
                log_1m, log_b = _sb_logs(_dot_nt(qm_ref[hh], k), strict, masked)
                blocks, totals, whole = _block_totals(log_1m)
                a = jnp.exp2(log_b + _running_sums(blocks, totals, t_suffix, right_ref[hh], True))
                if masked:
                    a = jnp.where(strict, a, 0.0)
                right_ref[hh] += whole
                acc_ref[hh] += _dot(a.astype(BF16), v)

        tile(i * t, t, True)

        def alive(n):
            return (n < i * (t // sw)) & (jnp.max(right_ref[...]) > SB_DEAD)

        def step(n):
            tile((i * (t // sw) - 1 - n) * sw, sw, False)
            return n + 1

        swept = lax.while_loop(alive, step, jnp.int32(0))
        cnt_ref[...] = jnp.full(cnt_ref.shape, swept.astype(F32))
        first = (lane // HEAD_DIM) == 0
        o_ref[...] = jnp.where(first, acc_ref[0], acc_ref[1])
        tot_ref[...] = jnp.where(first, right_ref[0], right_ref[1])

    outs = [jax.ShapeDtypeStruct((s, 512), F32), jax.ShapeDtypeStruct((pairs, s, LANES), F32),
            jax.ShapeDtypeStruct((pairs, s // t, 8, LANES), F32)]
    return pl.pallas_call(
        body, name="sb_fwd", grid=(pairs, s // t), out_shape=outs,
        in_specs=[pl.BlockSpec((t, LANES), lambda hp, i: (i, hp)), pl.BlockSpec((s, LANES), lambda hp, i: (0, 4 + hp)),
                  pl.BlockSpec((s, LANES), lambda hp, i: (0, 8 + hp))],
        out_specs=[pl.BlockSpec((t, LANES), lambda hp, i: (i, hp)), pl.BlockSpec((None, t, LANES), lambda hp, i: (hp, i, 0)),
                   pl.BlockSpec((None, None, 8, LANES), lambda hp, i: (hp, i, 0, 0))],
        scratch_shapes=[pltpu.VMEM((2, t, LANES), BF16), pltpu.VMEM((2, t, LANES), F32), pltpu.VMEM((2, t, LANES), F32)],
        compiler_params=_cparams("arbitrary", "arbitrary"),
    )(qkv, qkv, qkv)


def _sb_bwd(qkv, do, tot, cnt):
    s = qkv.shape[0]
    t = min(SB_TILE, s)
    sw = _sweep_width(t)
    pairs = SB_HEADS // 2

    def body(q_ref, k_ref, v_ref, do_ref, tot_ref, cnt_ref, dq_ref, dk_ref, dv_ref,
             qm_ref, dob_ref, total_s, left_l, left_g, dq_s):
        i = pl.program_id(1)

        @pl.when(i == 0)
        def _():
            dk_ref[...] = jnp.zeros_like(dk_ref)
            dv_ref[...] = jnp.zeros_like(dv_ref)

        lane = lax.broadcasted_iota(jnp.int32, (1, LANES), 1)
        row, col = _tile_iotas(t)
        strict = col < row
        t_suffix = _tri(t, lambda r, c: r > c)
        t_excl = _tri(t, lambda r, c: r < c)
        q_pair, do_pair, tot_pair = q_ref[...], do_ref[...], tot_ref[...]
        for hh in range(2):
            in_head = (lane // HEAD_DIM) == hh
            qm_ref[hh] = jnp.where(in_head, q_pair, jnp.zeros_like(q_pair))
            dob_ref[hh] = jnp.where(in_head, do_pair, 0.0).astype(BF16)
            total_s[hh] = jnp.broadcast_to(
                jnp.sum(jnp.where(lane == hh * HEAD_DIM, tot_pair, 0.0), axis=-1, keepdims=True), (t, LANES))
        left_l[...] = jnp.zeros_like(left_l)
        left_g[...] = jnp.zeros_like(left_g)
        dq_s[...] = jnp.zeros_like(dq_s)
        reps = t // LANES

        def tile(start, width, masked):
            rows = pl.ds(pl.multiple_of(start, width), width)
            k, v = k_ref[rows, :], v_ref[rows, :]
            z2 = [_dot_nt(qm_ref[hh], k) for hh in HEADS]
            d_a = [_dot_nt(dob_ref[hh], v) for hh in HEADS]
            for hh in HEADS:
                qm, dob = qm_ref[hh], dob_ref[hh]
                log_1m, log_b = _sb_logs(z2[hh], strict, masked)
                blocks, totals, whole = _block_totals(log_1m)
                done = left_l[hh] + whole
                left_l[hh] = done
                a = jnp.exp2(log_b + _running_sums(blocks, totals, t_suffix, total_s[hh] - done, True))
                if masked:
                    a = jnp.where(strict, a, 0.0)
                g = a * d_a[hh]
                blocks, totals, whole = _block_totals(g)
                before = _running_sums(blocks, totals, t_excl, left_g[hh], False)
                left_g[hh] += whole
                dz = g - jnp.exp2(log_b) * (g + before)
                if masked:
                    dz = jnp.where(strict, dz, 0.0)
                dzb = dz.astype(BF16)
                dq_s[hh] += _dot(dzb, k)
                dk_ref[rows, :] += _dot_tn(dzb, qm)
                dv_ref[rows, :] += _dot_tn(a.astype(BF16), dob)

        def step(h, carry):
            tile(h * sw, sw, False)
            return carry

        swept = jnp.max(cnt_ref[...]).astype(jnp.int32)
        lax.fori_loop(i * (t // sw) - swept, i * (t // sw), step, 0)
        tile(i * t, t, True)
        dq_ref[...] = jnp.where((lane // HEAD_DIM) == 0, dq_s[0], dq_s[1])

        @pl.when(i == s // t - 1)
        def _():
            dk_ref[...] *= 1.0 / LOG2E

    outs = [jax.ShapeDtypeStruct((s, 512), F32)] * 3
    return pl.pallas_call(
        body, name="sb_bwd", grid=(pairs, s // t), out_shape=outs,
        in_specs=[pl.BlockSpec((t, LANES), lambda hp, i: (i, hp)), pl.BlockSpec((s, LANES), lambda hp, i: (0, 4 + hp)),
                  pl.BlockSpec((s, LANES), lambda hp, i: (0, 8 + hp)), pl.BlockSpec((t, LANES), lambda hp, i: (i, hp)),
                  pl.BlockSpec((None, t, LANES), lambda hp, i: (hp, i, 0)),
                  pl.BlockSpec((None, None, 8, LANES), lambda hp, i: (hp, i, 0, 0))],
        out_specs=[pl.BlockSpec((t, LANES), lambda hp, i: (i, hp)), pl.BlockSpec((s, LANES), lambda hp, i: (0, hp)),
                   pl.BlockSpec((s, LANES), lambda hp, i: (0, hp))],
        scratch_shapes=[pltpu.VMEM((2, t, LANES), BF16), pltpu.VMEM((2, t, LANES), BF16)]
        + [pltpu.VMEM((2, t, LANES), F32)] * 4,
        compiler_params=_cparams("arbitrary", "arbitrary"),
    )(qkv, qkv, qkv, do, tot, cnt)


def _mla_bwd(qn, qr, kv, kr, do, o, lse, parts):
    s = qn.shape[0]
    t = min(ATT_TILE, s)
    pairs = MLA_HEADS // 2
    nq = s // t
    n = len(parts)

    def body(*refs):
        qn_ref, qr_ref, kn_ref, v_ref, kr_ref, do_ref, o_ref, lse_ref = refs[:8]
        dqn_ref, dqr_ref, dkn_ref, dv_ref, dkr_ref = refs[8 + n:13 + n]
        qcat_ref, dob_ref, lse_s, delta_s, dq_s = refs[13 + 2 * n:18 + 2 * n]
        hp, i = pl.program_id(0), pl.program_id(1)
        ride = _Exchange(False, refs[8:8 + n], refs[13 + n:13 + 2 * n], *refs[18 + 2 * n:])

        @pl.when((hp == 0) & (i == 0))
        def _():
            ride.start()

        @pl.when(i == 0)
        def _():
            dkn_ref[...] = jnp.zeros_like(dkn_ref)
            dv_ref[...] = jnp.zeros_like(dv_ref)
            dkr_ref[...] = jnp.zeros_like(dkr_ref)

        lane = lax.broadcasted_iota(jnp.int32, (1, LANES), 1)
        row, col = _tile_iotas(t)
        causal = col <= row
        q_pair, q_quad, do_pair, lse_pair = qn_ref[...], qr_ref[...], do_ref[...], lse_ref[...]
        do_o = do_pair * o_ref[...]
        zero = jnp.zeros_like(q_pair)
        ropes = []
        for hh in range(2):
            in_head = (lane // HEAD_DIM) == hh
            in_rope = (lane // MLA_ROPE) == (hp % 2) * 2 + hh
            ropes.append(in_rope)
            qcat_ref[hh, :, 0:LANES] = jnp.where(in_head, q_pair, zero)
            qcat_ref[hh, :, LANES:2 * LANES] = jnp.where(in_rope, q_quad, zero)
            dob_ref[hh] = jnp.where(in_head, do_pair, 0.0).astype(BF16)
            delta_s[hh] = jnp.broadcast_to(jnp.sum(jnp.where(in_head, do_o, 0.0), axis=-1, keepdims=True), (t, LANES))
            lse_s[hh] = jnp.broadcast_to(
                jnp.sum(jnp.where(lane == hh * HEAD_DIM, lse_pair, 0.0), axis=-1, keepdims=True), (t, LANES))
        dq_s[...] = jnp.zeros_like(dq_s)
        reps = t // LANES

        def tile(j, width, masked):
            rows = pl.ds(pl.multiple_of(j * t, t), width * t)
            kcat = jnp.concatenate([kn_ref[rows, :], kr_ref[rows, :]], axis=1)
            v = v_ref[rows, :]
            sc = [_dot_nt(qcat_ref[hh], kcat) for hh in HEADS]
            dp = [_dot_nt(dob_ref[hh], v) for hh in HEADS]
            p = [jnp.exp(sc[hh] - jnp.concatenate([lse_s[hh]] * (width * reps), axis=1)) for hh in HEADS]
            if masked:
                p = [jnp.where(causal, p[hh], 0.0) for hh in HEADS]
            ds = [(p[hh] * (dp[hh] - jnp.concatenate([delta_s[hh]] * (width * reps), axis=1))).astype(BF16) for hh in HEADS]
            for hh in HEADS:
                dq_s[hh] += _dot(ds[hh], kcat)
            dkcat = _dot_tn(ds[0], qcat_ref[0]) + _dot_tn(ds[1], qcat_ref[1])
            dkn_ref[rows, :] += dkcat[:, 0:LANES]
            dkr_ref[rows, :] += dkcat[:, LANES:2 * LANES]
            dv_ref[rows, :] += _dot_tn(p[0].astype(BF16), dob_ref[0]) + _dot_tn(p[1].astype(BF16), dob_ref[1])

        def step(n, carry):
            tile(2 * n, 2, False)
            return carry

        lax.fori_loop(0, i // 2, step, 0)

        @pl.when(i % 2 == 1)
        def _():
            tile(i - 1, 1, False)

        tile(i, 1, True)
        dqn_ref[...] = jnp.where((lane // HEAD_DIM) == 0, dq_s[0, :, 0:LANES], dq_s[1, :, 0:LANES])
        dqr_ref[...] = (jnp.where(ropes[0], dq_s[0, :, LANES:2 * LANES], 0.0)
                        + jnp.where(ropes[1], dq_s[1, :, LANES:2 * LANES], 0.0))

        @pl.when((hp == pairs - 1) & (i == nq - 1))
        def _():
            ride.finish()

    pair_block = pl.BlockSpec((t, LANES), lambda hp, i: (i, hp))
    once = pl.Buffered(1)
    landed_shapes, sems = _exchange_shapes(False, parts)
    outs = [jax.ShapeDtypeStruct((s, 512), F32), jax.ShapeDtypeStruct((pairs, s, LANES), F32),
            jax.ShapeDtypeStruct((s, 512), F32), jax.ShapeDtypeStruct((s, 512), F32),
            jax.ShapeDtypeStruct((pairs, s, LANES), F32)] + landed_shapes
    res = pl.pallas_call(
        body, name="mla_bwd", grid=(pairs, nq), out_shape=outs,
        in_specs=[pair_block, pl.BlockSpec((t, LANES), lambda hp, i: (i, hp // 2)),
                  pl.BlockSpec((s, LANES), lambda hp, i: (0, hp), pipeline_mode=once),
                  pl.BlockSpec((s, LANES), lambda hp, i: (0, 4 + hp), pipeline_mode=once),
                  pl.BlockSpec((s, LANES), lambda hp, i: (0, 0), pipeline_mode=once), pair_block, pair_block,
                  pl.BlockSpec((None, t, LANES), lambda hp, i: (hp, i, 0))] + [ANY] * n,
        out_specs=[pair_block, pl.BlockSpec((None, t, LANES), lambda hp, i: (hp, i, 0)),
                   pl.BlockSpec((s, LANES), lambda hp, i: (0, hp), pipeline_mode=once),
                   pl.BlockSpec((s, LANES), lambda hp, i: (0, hp), pipeline_mode=once),
                   pl.BlockSpec((None, s, LANES), lambda hp, i: (hp, 0, 0), pipeline_mode=once)] + [ANY] * n,
        scratch_shapes=[pltpu.VMEM((2, t, 2 * LANES), BF16), pltpu.VMEM((2, t, LANES), BF16), pltpu.VMEM((2, t, LANES), F32),
                        pltpu.VMEM((2, t, LANES), F32), pltpu.VMEM((2, t, 2 * LANES), F32)] + sems,
        compiler_params=_cparams("arbitrary", "arbitrary"),
    )(qn, qr, kv, kv, kr, do, o, lse, *parts)
    return res[:5], res[5:]


def _ffn_bwd(dh2, dh2b, gate, up, h1, g_ffn, w_down, w_gate, w_up):
    s, d = h1.shape
    d_ff = gate.shape[1]
    tm = min(FFN_BWD_ROW_TILE, s)
    tf = _ffn_tile(d_ff)

    def act_body(dh2b_ref, gate_ref, up_ref, wd_ref, dgate_ref, dup_ref, act_ref):
        dact = _dot_nt(dh2b_ref[...], wd_ref[...])
        gate_v = gate_ref[...].astype(F32)
        up_v = up_ref[...].astype(F32)
        sig = jax.nn.sigmoid(gate_v)
        silu = gate_v * sig
        dup_ref[...] = (dact * silu).astype(BF16)
        dgate_ref[...] = (dact * up_v * (sig * (1.0 + gate_v * (1.0 - sig)))).astype(BF16)
        act_ref[...] = (silu * up_v).astype(BF16)

    ff = pl.BlockSpec((tm, tf), lambda j, r: (r, j))
    dgate, dup, act = pl.pallas_call(
        act_body, name="ffn_bwd_act", grid=(d_ff // tf, s // tm), out_shape=[jax.ShapeDtypeStruct((s, d_ff), BF16)] * 3,
        in_specs=[pl.BlockSpec((tm, d), lambda j, r: (r, 0)), ff, ff, pl.BlockSpec((tf, d), lambda j, r: (j, 0))],
        out_specs=[ff, ff, ff],
        compiler_params=_cparams("arbitrary", "arbitrary"),
    )(dh2b, gate, up, w_down)

    def df_body(dgate_ref, dup_ref, dh2_ref, h1_ref, g_ref, wg_ref, wu_ref, dh1_ref, dh1b_ref, dg_ref):
        df = _dot_nt(dgate_ref[...], wg_ref[...]) + _dot_nt(dup_ref[...], wu_ref[...])
        dx, dg = _rms_bwd(h1_ref[...], g_ref[...], df)
        dh1 = dh2_ref[...] + dx
        dh1_ref[...] = dh1
        dh1b_ref[...] = dh1.astype(BF16)
        _accumulate(dg_ref, dg, pl.program_id(0) == 0)

    outs = [jax.ShapeDtypeStruct((s, d), F32), jax.ShapeDtypeStruct((s, d), BF16), jax.ShapeDtypeStruct((1, d), F32)]
    dh1, dh1b, dg = pl.pallas_call(
        df_body, name="ffn_bwd_df", grid=(s // tm,), out_shape=outs,
        in_specs=[_row_spec(tm, d_ff), _row_spec(tm, d_ff), _row_spec(tm, d), _row_spec(tm, d), _full_spec((1, d)),
                  _full_spec(w_gate.shape), _full_spec(w_up.shape)],
        out_specs=[_row_spec(tm, d), _row_spec(tm, d), _full_spec((1, d))],
        compiler_params=_cparams("arbitrary"),
    )(dgate, dup, dh2, h1, g_ffn, w_gate, w_up)
    return dgate, dup, act, dh1, dh1b, dg


def _largest_tile(n, cap):
    for cand in range(cap, 0, -LANES):
        if n % cand == 0:
            return cand
    return n


def _tn_matmul(a, b, name):
    assert a.dtype == BF16 and b.dtype == BF16
    s, m = a.shape
    n = b.shape[1]
    if s * m * 2 <= TN_RESIDENT_BYTES:
        tm, tn = m, min(n, TN_BLOCK)
    else:
        tm, tn = TN_BLOCK, n

    def body(a_ref, b_ref, o_ref):
        o_ref[...] = _dot_tn(a_ref[...], b_ref[...])

    return pl.pallas_call(
        body, name=name, grid=(m // tm, n // tn), out_shape=jax.ShapeDtypeStruct((m, n), F32),
        in_specs=[pl.BlockSpec((s, tm), lambda i, j: (0, i)), pl.BlockSpec((s, tn), lambda i, j: (0, j))],
        out_specs=pl.BlockSpec((tm, tn), lambda i, j: (i, j)),
        compiler_params=_cparams("arbitrary", "arbitrary"),
    )(a, b)


def _attn_out_bwd(dh1, w_o, o_mla, o_sb, g_mla, g_sb):
    s, d = dh1.shape
    tm = min(ROW_TILE, s)

    def body(dh1_ref, wo_ref, oa_ref, ob_ref, ga_ref, gb_ref, doa_ref, dob_ref, dga_ref, dgb_ref):
        first = pl.program_id(0) == 0
        dh1b = dh1_ref[...]
        dxa, dga = _rms_bwd(oa_ref[...], ga_ref[...], _dot_nt(dh1b, wo_ref[0:512, :]))
        dxb, dgb = _rms_bwd(ob_ref[...], gb_ref[...], _dot_nt(dh1b, wo_ref[512:1024, :]))
        doa_ref[...] = dxa
        dob_ref[...] = dxb
        _accumulate(dga_ref, dga, first)
        _accumulate(dgb_ref, dgb, first)

    outs = [jax.ShapeDtypeStruct((s, 512), F32)] * 2 + [jax.ShapeDtypeStruct((1, 512), F32)] * 2
    return pl.pallas_call(
        body, name="attn_out_bwd", grid=(s // tm,), out_shape=outs,
        in_specs=[_row_spec(tm, d), _full_spec(w_o.shape), _row_spec(tm, 512), _row_spec(tm, 512),
                  _full_spec((1, 512)), _full_spec((1, 512))],
        out_specs=[_row_spec(tm, 512), _row_spec(tm, 512), _full_spec((1, 512)), _full_spec((1, 512))],
        compiler_params=_cparams("arbitrary"),
    )(dh1, w_o, o_mla, o_sb, g_mla, g_sb)


def _proj_in_bwd(dqn, dqr, dkn, dv, dkr, dq_sb, dk_sb, dv_sb, cq, ckv, x, dh1, cos, sin_a, sin_b,
                 g_q, g_kv, g_mix, w_uq, w_ukv, w_a):
    s, d = x.shape
    tm = min(PROJ_BWD_ROW_TILE, s)

    def body(dqn_ref, dqr_ref, dkn_ref, dv_ref, dkr_ref, dqs_ref, dks_ref, dvs_ref, cq_ref, ckv_ref, x_ref, dh1_ref,
             cos_ref, sa_ref, sb_ref, gq_ref, gkv_ref, gm_ref, wuq_ref, wukv_ref, wa_ref,
             dx_ref, dproj_ref, dq_ref, dkv_ref, dgq_ref, dgkv_ref, dgm_ref):
        first = pl.program_id(0) == 0
        lane = lax.broadcasted_iota(jnp.int32, (1, LANES), 1)
        cos_t, sa_t, sb_t = cos_ref[...], sa_ref[...], sb_ref[...]
        dq_ref[:, 0:512] = (dqn_ref[...] * MLA_SCALE).astype(BF16)
        for half in range(2):
            quad = (dqr_ref[2 * half] + dqr_ref[2 * half + 1]) * MLA_SCALE
            dq_ref[:, 512 + half * LANES:512 + (half + 1) * LANES] = _rope_t(quad, cos_t, sa_t, sb_t).astype(BF16)
        dcq, dgq = _rms_bwd(cq_ref[...], gq_ref[...], _dot_nt(dq_ref[...], wuq_ref[...]))
        _accumulate(dgq_ref, dgq, first)
        dkv_ref[:, 0:512] = dkn_ref[...].astype(BF16)
        dkv_ref[:, 512:1024] = dv_ref[...].astype(BF16)
        dckv, dgkv = _rms_bwd(ckv_ref[...], gkv_ref[...], _dot_nt(dkv_ref[...], wukv_ref[...]))
        _accumulate(dgkv_ref, dgkv, first)
        g = _rope_t(dkr_ref[0] + dkr_ref[1] + dkr_ref[2] + dkr_ref[3], cos_t, sa_t, sb_t)
        g = g + pltpu.roll(g, 96, 1) + pltpu.roll(g, 64, 1) + pltpu.roll(g, 32, 1)
        dproj_ref[:, 0:256] = dcq.astype(BF16)
        dproj_ref[:, 256:384] = dckv.astype(BF16)
        dproj_ref[:, 384:512] = jnp.where(lane < MLA_ROPE, g, 0.0).astype(BF16)
        dproj_ref[:, 512:1024] = (dqs_ref[...] * SB_SCALE).astype(BF16)
        dproj_ref[:, 1024:1536] = dks_ref[...].astype(BF16)
        dproj_ref[:, 1536:2048] = dvs_ref[...].astype(BF16)
        dxn, dgm = _rms_bwd(x_ref[...], gm_ref[...], _dot_nt(dproj_ref[...], wa_ref[...]))
        dx_ref[...] = dh1_ref[...] + dxn
        _accumulate(dgm_ref, dgm, first)

    quad_spec = pl.BlockSpec((4, tm, LANES), lambda r: (0, r, 0))
    outs = [jax.ShapeDtypeStruct((s, d), F32), jax.ShapeDtypeStruct((s, 2048), BF16), jax.ShapeDtypeStruct((s, 768), BF16),
            jax.ShapeDtypeStruct((s, 1024), BF16), jax.ShapeDtypeStruct((1, 256), F32), jax.ShapeDtypeStruct((1, 128), F32),
            jax.ShapeDtypeStruct((1, d), F32)]
    return pl.pallas_call(
        body, name="proj_in_bwd", grid=(s // tm,), out_shape=outs,
        in_specs=[_row_spec(tm, 512), quad_spec, _row_spec(tm, 512), _row_spec(tm, 512), quad_spec,
                  _row_spec(tm, 512), _row_spec(tm, 512), _row_spec(tm, 512), _row_spec(tm, 256), _row_spec(tm, 128),
                  _row_spec(tm, d), _row_spec(tm, d), _row_spec(tm, LANES), _row_spec(tm, LANES), _row_spec(tm, LANES),
                  _full_spec((1, 256)), _full_spec((1, 128)), _full_spec((1, d)),
                  _full_spec(w_uq.shape), _full_spec(w_ukv.shape), _full_spec(w_a.shape)],
        out_specs=[_row_spec(tm, d), _row_spec(tm, 2048), _row_spec(tm, 768), _row_spec(tm, 1024),
                   _full_spec((1, 256)), _full_spec((1, 128)), _full_spec((1, d))],
        compiler_params=_cparams("arbitrary"),
    )(dqn, dqr, dkn, dv, dkr, dq_sb, dk_sb, dv_sb, cq, ckv, x, dh1, cos, sin_a, sin_b, g_q, g_kv, g_mix,
      w_uq, w_ukv, w_a)


ANY = pl.BlockSpec(memory_space=pl.ANY)


def _place():
    return lax.axis_index("x"), lax.axis_index("y"), lax.axis_index("c")


def _all_gather(shards, name):
    n = len(shards)

    def body(*refs):
        ins, outs = refs[:n], refs[n:2 * n]
        send_sems, recv_sems, local_sems = refs[2 * n:]
        x, y, c = _place()
        me, sibling = (x, y, c), (x, y, 1 - c)
        chips = [(1 - x, y), (x, 1 - y), (1 - x, 1 - y)]

        def slot(a, px, py, pc):
            return outs[a].at[4 * px + 2 * py + pc]

        def copy(a, k, block, to, src=None):
            return pltpu.make_async_remote_copy(
                src_ref=slot(a, *block) if src is None else src, dst_ref=slot(a, *block),
                send_sem=send_sems.at[a, k], recv_sem=recv_sems.at[a, k], device_id=to, device_id_type=MESH)

        mine, first, passed = [], [], []
        for a in range(n):
            own = pltpu.make_async_copy(ins[a], slot(a, *me), local_sems.at[a])
            own.start()
            mine.append(own)
            cps = [copy(a, 0, me, sibling, src=ins[a])]
            cps += [copy(a, 1 + j, me, (*chip, c), src=ins[a]) for j, chip in enumerate(chips)]
            for cp in cps:
                cp.start()
            first += cps
        for a in range(n):
            for j, chip in enumerate(chips):
                copy(a, 1 + j, (*chip, c), me).wait_recv()
                fwd = copy(a, 4 + j, (*chip, c), sibling)
                fwd.start()
                passed.append(fwd)
        for a in range(n):
            copy(a, 0, sibling, me).wait_recv()
            for j, chip in enumerate(chips):
                copy(a, 4 + j, (*chip, 1 - c), me).wait_recv()
        for cp in first + passed:
            cp.wait_send()
        for own in mine:
            own.wait()

    return pl.pallas_call(
        body, name=name,
        out_shape=[jax.ShapeDtypeStruct((N_DEV,) + v.shape, v.dtype) for v in shards],
        in_specs=[ANY] * n, out_specs=[ANY] * n,
        scratch_shapes=[pltpu.SemaphoreType.DMA((n, 7)), pltpu.SemaphoreType.DMA((n, 7)), pltpu.SemaphoreType.DMA((n,))],
    )(*shards)


class _Exchange:
    def __init__(self, gather, ins, outs, send_sems, recv_sems, local_sems):
        self.gather, self.ins, self.outs = gather, ins, outs
        self.sems = (send_sems, recv_sems, local_sems)
        x, y, c = _place()
        self.me = 4 * x + 2 * y + c
        self.peers = []
        for k in range(1, N_DEV):
            px = 1 - x if k & 4 else x
            py = 1 - y if k & 2 else y
            pc = 1 - c if k & 1 else c
            self.peers.append(((px, py, pc), 4 * px + 2 * py + pc))

    def _remote(self, a, k, landing):
        send_sems, recv_sems, _ = self.sems
        where, number = self.peers[k]
        src = self.ins[a] if self.gather else self.ins[a].at[number]
        return pltpu.make_async_remote_copy(
            src_ref=src, dst_ref=self.outs[a].at[landing], send_sem=send_sems.at[a, k], recv_sem=recv_sems.at[a, k],
            device_id=where, device_id_type=MESH)

    def _local(self, a):
        src = self.ins[a] if self.gather else self.ins[a].at[self.me]
        return pltpu.make_async_copy(src, self.outs[a].at[self.me], self.sems[2].at[a])

    def start(self):
        for a in range(len(self.ins)):
            self._local(a).start()
            for k in range(N_DEV - 1):
                self._remote(a, k, self.me).start()

    def finish(self):
        for a in range(len(self.ins)):
            for k in range(N_DEV - 1):
                self._remote(a, k, self.peers[k][1]).wait_recv()
            for k in range(N_DEV - 1):
                self._remote(a, k, self.me).wait_send()
            self._local(a).wait()


def _exchange_shapes(gather, arrays):
    out_shape = [jax.ShapeDtypeStruct(((N_DEV,) + v.shape) if gather else v.shape, v.dtype) for v in arrays]
    n = len(arrays)
    sems = [pltpu.SemaphoreType.DMA((n, N_DEV - 1)), pltpu.SemaphoreType.DMA((n, N_DEV - 1)), pltpu.SemaphoreType.DMA((n,))]
    return out_shape, sems


def _exchange(gather, arrays, name):
    n = len(arrays)

    def body(*refs):
        ex = _Exchange(gather, refs[:n], refs[n:2 * n], *refs[2 * n:])
        ex.start()
        ex.finish()

    out_shape, sems = _exchange_shapes(gather, arrays)
    return pl.pallas_call(body, name=name, out_shape=out_shape, in_specs=[ANY] * n, out_specs=[ANY] * n,
                          scratch_shapes=sems)(*arrays)


def _grad_row_tile(rows):
    return _largest_tile_rows(rows, 256)


def _largest_tile_rows(rows, cap):
    for cand in range(cap, 0, -8):
        if rows % cand == 0:
            return cand
    return rows


def _adamw_math(w, g, m, v):
    m_new = ADAM_B1 * m + (1.0 - ADAM_B1) * g
    v_new = ADAM_B2 * v + (1.0 - ADAM_B2) * (g * g)
    m_hat = m_new / (1.0 - ADAM_B1 ** ADAM_STEP)
    v_hat = v_new / (1.0 - ADAM_B2 ** ADAM_STEP)
    delta = -ADAM_LR * (m_hat / (jnp.sqrt(v_hat) + ADAM_EPS) + ADAM_WD * w)
    return delta, m_new, v_new


def _adamw(slots, w, m, v, name):
    k, r, cdim = slots.shape
    tr = _grad_row_tile(r)

    def body(s_ref, w_ref, m_ref, v_ref, g_ref, d_ref, mo_ref, vo_ref):
        g = s_ref[0].astype(F32)
        for q in range(1, k):
            g = g + s_ref[q].astype(F32)
        g_ref[...] = g
        d_ref[...], mo_ref[...], vo_ref[...] = _adamw_math(w_ref[...], g, m_ref[...], v_ref[...])

    blk = pl.BlockSpec((tr, cdim), lambda i: (i, 0))
    return pl.pallas_call(
        body, name=name, grid=(r // tr,), out_shape=[jax.ShapeDtypeStruct((r, cdim), F32)] * 4,
        in_specs=[pl.BlockSpec((k, tr, cdim), lambda i: (0, i, 0)), blk, blk, blk], out_specs=[blk] * 4,
        compiler_params=_cparams("arbitrary"),
    )(slots, w, m, v)


def _stack_cols(g):
    n, r, c = g.shape
    return g.transpose(1, 0, 2).reshape(r, n * c)


def _split_cols(w):
    r, nc = w.shape
    return w.reshape(r, N_DEV, nc // N_DEV).transpose(1, 0, 2)


def _rope_tables(positions):
    inv_freq = ROPE_THETA ** (-jnp.arange(0, MLA_ROPE, 2, dtype=F32) / MLA_ROPE)
    ang = positions.astype(F32).reshape(-1, 1) * inv_freq[None, :]
    cos, sin, zero = jnp.cos(ang), jnp.sin(ang), jnp.zeros_like(ang)
    reps = LANES // MLA_ROPE
    return (jnp.tile(jnp.concatenate([cos, cos], axis=1), (1, reps)),
            jnp.tile(jnp.concatenate([-sin, zero], axis=1), (1, reps)),
            jnp.tile(jnp.concatenate([zero, sin], axis=1), (1, reps)))


def _local_step(x, positions, loss_target, gains, g_in, g_uq, g_ukv, late_shards):
    norm_mix, q_norm, kv_norm, out_mla, out_sb, norm_ffn, norm_final = gains
    d = x.shape[1]
    w_in = _stack_cols(g_in)
    w_a = jnp.concatenate([w_in[:, :416], jnp.zeros((d, 96), BF16), w_in[:, 416:]], axis=1)
    w_uq = jnp.concatenate([g_uq[:, :, :MLA_NOPE].transpose(1, 0, 2).reshape(Q_LORA, -1),
                            g_uq[:, :, MLA_NOPE:].transpose(1, 0, 2).reshape(Q_LORA, -1)], axis=1)
    w_ukv = jnp.concatenate([g_ukv[:, :, :MLA_NOPE].transpose(1, 0, 2).reshape(KV_LORA, -1),
                             g_ukv[:, :, MLA_NOPE:].transpose(1, 0, 2).reshape(KV_LORA, -1)], axis=1)
    cos, sin_a, sin_b = _rope_tables(positions)

    u, cq, ckv, cqn, ckvn, qn, qr, kv, kr, qkv_sb = _proj_in_fwd(x, norm_mix, w_a, q_norm, w_uq, kv_norm, w_ukv, cos, sin_a, sin_b)
    o_mla, lse, (g_o, g_gate, g_up, g_down) = _mla_fwd(qn, qr, kv, kr, late_shards)
    w_o = g_o.reshape(-1, d)
    w_gate, w_up = _stack_cols(g_gate), _stack_cols(g_up)
    w_down = g_down.reshape(-1, d)
    o_sb, tot, swept = _sb_fwd(qkv_sb)
    merged, h1, f = _attn_out_fwd(o_mla, o_sb, out_mla, out_sb, w_o, x, norm_ffn)
    gate, up, h2 = _ffn_fwd(f, h1, w_gate, w_up, w_down)
    loss, dh2, dh2b, dg_final = _final_loss(h2, loss_target, norm_final.reshape(1, d))

    dgate, dup, act, dh1, dh1b, dg_ffn = _ffn_bwd(dh2, dh2b, gate, up, h1, norm_ffn, w_down, w_gate, w_up)
    dw_down = _tn_matmul(act, dh2b, "dw_down")
    dw_gate = _tn_matmul(f, dgate, "dw_gate")
    dw_up = _tn_matmul(f, dup, "dw_up")
    do_mla, do_sb, dg_mla, dg_sb = _attn_out_bwd(dh1b, w_o, o_mla, o_sb, out_mla, out_sb)
    dw_o = _tn_matmul(merged, dh1b, "dw_o")
    dq_sb, dk_sb, dv_sb = _sb_bwd(qkv_sb, do_sb, tot, swept)
    early = [dw_o.reshape(N_DEV, -1, d), _split_cols(dw_gate), _split_cols(dw_up), dw_down.reshape(N_DEV, -1, d)]
    (dqn, dqr, dkn, dv, dkr), landed = _mla_bwd(qn, qr, kv, kr, do_mla, o_mla, lse, [p.astype(BF16) for p in early])
    dx, dproj, dq, dkv, dg_q, dg_kv, dg_mix = _proj_in_bwd(
        dqn, dqr, dkn, dv, dkr, dq_sb, dk_sb, dv_sb, cq, ckv, x, dh1, cos, sin_a, sin_b,
        q_norm, kv_norm, norm_mix, w_uq, w_ukv, w_a)
    dw_a = _tn_matmul(u, dproj, "dw_in")
    dw_uq = _tn_matmul(cqn, dq, "dw_uq")
    dw_ukv = _tn_matmul(ckvn, dkv, "dw_ukv")

    p_in = _split_cols(jnp.concatenate([dw_a[:, :416], dw_a[:, 512:]], axis=1))
    p_uq = jnp.concatenate([dw_uq[:, :512].reshape(Q_LORA, MLA_HEADS, MLA_NOPE),
                            dw_uq[:, 512:].reshape(Q_LORA, MLA_HEADS, MLA_ROPE)], axis=2).transpose(1, 0, 2)
    p_ukv = jnp.concatenate([dw_ukv[:, :512].reshape(KV_LORA, MLA_HEADS, MLA_NOPE),
                             dw_ukv[:, 512:].reshape(KV_LORA, MLA_HEADS, HEAD_DIM)], axis=2).transpose(1, 0, 2)
    late = [p.astype(BF16) for p in (p_in, p_uq, p_ukv)]
    gain_grads = [dg_mix, dg_q, dg_kv, dg_mla, dg_sb, dg_ffn, dg_final]
    return loss, dx, list(landed), late, gain_grads


def kernel(x, positions, norm_mix, w_in, q_latent_norm, w_uq, kv_latent_norm, w_ukv, out_norm_mla, out_norm_sb, w_o, norm_ffn, w_gate, w_up, w_down, norm_final, loss_target, m_norm_mix, m_w_in, m_q_latent_norm, m_w_uq, m_kv_latent_norm, m_w_ukv, m_out_norm_mla, m_out_norm_sb, m_w_o, m_norm_ffn, m_w_gate, m_w_up, m_w_down, m_norm_final, v_norm_mix, v_w_in, v_q_latent_norm, v_w_uq, v_kv_latent_norm, v_w_ukv, v_out_norm_mla, v_out_norm_sb, v_w_o, v_norm_ffn, v_w_gate, v_w_up, v_w_down, v_norm_final):
    mats = [w_in, w_uq, w_ukv, w_o, w_gate, w_up, w_down]
    mat_m = [m_w_in, m_w_uq, m_w_ukv, m_w_o, m_w_gate, m_w_up, m_w_down]
    mat_v = [v_w_in, v_w_uq, v_w_ukv, v_w_o, v_w_gate, v_w_up, v_w_down]
    mat_names = ["w_in", "w_uq", "w_ukv", "w_o", "w_gate", "w_up", "w_down"]
    gains = [norm_mix, q_latent_norm, kv_latent_norm, out_norm_mla, out_norm_sb, norm_ffn, norm_final]
    gain_m = [m_norm_mix, m_q_latent_norm, m_kv_latent_norm, m_out_norm_mla, m_out_norm_sb, m_norm_ffn, m_norm_final]
    gain_v = [v_norm_mix, v_q_latent_norm, v_kv_latent_norm, v_out_norm_mla, v_out_norm_sb, v_norm_ffn, v_norm_final]

    shards = [w[0].astype(BF16) for w in mats]
    g_in, g_uq, g_ukv = _all_gather(shards[:3], "weight_all_gather")

    gains2d = [g.reshape(1, -1) for g in gains]
    loss_part, dx, landed, late, gain_grads = _local_step(
        x[0], positions[0], loss_target[0], gains2d, g_in, g_uq, g_ukv, shards[3:])

    slots = list(_exchange(False, late, "grad_scatter")) + landed
    mat_out = [_adamw(sl, w[0], m[0], v[0], "adamw_" + nm)
               for sl, w, m, v, nm in zip(slots, mats, mat_m, mat_v, mat_names)]

    sizes = [g.size for g in gains]
    used = sum(sizes) + LANES
    rows = -(-used // (8 * LANES)) * 8

    def pack(vals, tail):
        flat = jnp.concatenate([v.reshape(-1) for v in vals] + [tail])
        return jnp.pad(flat, (0, rows * LANES - flat.size)).reshape(rows, LANES)

    zeros_tail = jnp.zeros((LANES,), F32)
    small = _all_gather([pack(gain_grads, loss_part.reshape(-1))], "gain_all_gather")[0]
    g_s, d_s, m_s, v_s = _adamw(small, pack(gains, zeros_tail), pack(gain_m, zeros_tail), pack(gain_v, zeros_tail), "adamw_gains")

    def unpack(packed):
        flat = packed.reshape(-1)
        outs, off = [], 0
        for g, n in zip(gains, sizes):
            outs.append(flat[off:off + n].reshape(g.shape))
            off += n
        return outs

    loss = g_s.reshape(-1)[sum(sizes)]

    order = ["norm_mix", "w_in", "q_latent_norm", "w_uq", "kv_latent_norm", "w_ukv", "out_norm_mla", "out_norm_sb",
             "w_o", "norm_ffn", "w_gate", "w_up", "w_down", "norm_final"]
    gain_names = ["norm_mix", "q_latent_norm", "kv_latent_norm", "out_norm_mla", "out_norm_sb", "norm_ffn", "norm_final"]
    result = [loss, dx[None]]
    for kind in range(4):
        small_parts = dict(zip(gain_names, unpack([g_s, d_s, m_s, v_s][kind])))
        mat_parts = {nm: out[kind][None] for nm, out in zip(mat_names, mat_out)}
        result += [small_parts[nm] if nm in small_parts else mat_parts[nm] for nm in order]
    return tuple(result)
```

```python
import functools
import math

import jax
import jax.numpy as jnp
from jax import lax
from jax.experimental import pallas as pl
from jax.experimental.pallas import tpu as pltpu

F32 = jnp.float32
BF16 = jnp.bfloat16
MESH = pl.DeviceIdType.MESH

EPS = 1e-6
ROPE_THETA = 10000.0
MLA_HEADS = 8
MLA_NOPE = 64
MLA_ROPE = 32
SB_HEADS = 8
HEAD_DIM = 64
Q_LORA = 256
KV_LORA = 128
MLA_SCALE = 1.0 / math.sqrt(MLA_NOPE + MLA_ROPE)
SB_SCALE = 1.0 / math.sqrt(HEAD_DIM)
LOG2E = math.log2(math.e)
SB_DEAD = -160.0
N_DEV = 8

ADAM_LR = 0.001
ADAM_B1 = 0.9
ADAM_B2 = 0.999
ADAM_EPS = 1e-08
ADAM_WD = 0.01
ADAM_STEP = 10

LANES = 128
ATT_TILE = 512
SB_TILE = 512
TRI = 256
ROW_TILE = 512
FFN_BWD_ROW_TILE = 256
PROJ_BWD_ROW_TILE = 256
TN_BLOCK = 256
TN_RESIDENT_BYTES = 16 * 1024 * 1024
VMEM_LIMIT = 56 * 1024 * 1024
NEG = -1e30


def _cparams(*sem):
    return pltpu.CompilerParams(dimension_semantics=sem, vmem_limit_bytes=VMEM_LIMIT)


def _dot(a, b):
    return jnp.dot(a, b, preferred_element_type=F32)


def _dot_nt(a, b):
    return lax.dot_general(a, b, (((1,), (1,)), ((), ())), preferred_element_type=F32)


def _dot_tn(a, b):
    return lax.dot_general(a, b, (((0,), (0,)), ((), ())), preferred_element_type=F32)


def _rms(x, g):
    r = lax.rsqrt(jnp.mean(x * x, axis=-1, keepdims=True) + EPS)
    return x * r * g


def _rms_bwd(x, g, dy):
    r = lax.rsqrt(jnp.mean(x * x, axis=-1, keepdims=True) + EPS)
    n = x * r
    dn = dy * g
    dx = r * (dn - n * jnp.mean(dn * n, axis=-1, keepdims=True))
    return dx, jnp.sum(dy * n, axis=0, keepdims=True)


def _rope(x, cos, sin_a, sin_b):
    return x * cos + pltpu.roll(x, 112, 1) * sin_a + pltpu.roll(x, 16, 1) * sin_b


def _rope_t(g, cos, sin_a, sin_b):
    return g * cos + pltpu.roll(g * sin_a, 16, 1) + pltpu.roll(g * sin_b, 112, 1)


def _row_spec(tm, width):
    return pl.BlockSpec((tm, width), lambda r: (r, 0))


def _full_spec(shape):
    return pl.BlockSpec(shape, lambda *_: (0,) * len(shape))


def _accumulate(ref, val, first):
    @pl.when(first)
    def _():
        ref[...] = val

    @pl.when(jnp.logical_not(first))
    def _():
        ref[...] += val


def _proj_in_fwd(x, g_mix, w_a, g_q, w_uq, g_kv, w_ukv, cos, sin_a, sin_b):
    s, d = x.shape
    tm = min(ROW_TILE, s)

    def body(x_ref, gm_ref, wa_ref, gq_ref, wuq_ref, gkv_ref, wukv_ref, cos_ref, sa_ref, sb_ref,
             u_ref, cq_ref, ckv_ref, cqn_ref, ckvn_ref, qn_ref, qr_ref, kv_ref, kr_ref, sbq_ref):
        u = _rms(x_ref[...], gm_ref[...]).astype(BF16)
        u_ref[...] = u
        cq = _dot(u, wa_ref[:, 0:256])
        ckv = _dot(u, wa_ref[:, 256:384])
        kr = _dot(u, wa_ref[:, 384:512])
        cq_ref[...] = cq
        ckv_ref[...] = ckv
        cqn = _rms(cq, gq_ref[...]).astype(BF16)
        ckvn = _rms(ckv, gkv_ref[...]).astype(BF16)
        cqn_ref[...] = cqn
        ckvn_ref[...] = ckvn
        cos_t, sa_t, sb_t = cos_ref[...], sa_ref[...], sb_ref[...]
        qn_ref[...] = (_dot(cqn, wuq_ref[:, 0:512]) * MLA_SCALE).astype(BF16)
        for half in range(2):
            lo = 512 + half * LANES
            qr = _dot(cqn, wuq_ref[:, lo:lo + LANES])
            qr_ref[:, half * LANES:(half + 1) * LANES] = (_rope(qr, cos_t, sa_t, sb_t) * MLA_SCALE).astype(BF16)
        kv_ref[...] = _dot(ckvn, wukv_ref[...]).astype(BF16)
        krt = kr + pltpu.roll(kr, 32, 1) + pltpu.roll(kr, 64, 1) + pltpu.roll(kr, 96, 1)
        kr_ref[...] = _rope(krt, cos_t, sa_t, sb_t).astype(BF16)
        sbq_ref[:, 0:512] = (_dot(u, wa_ref[:, 512:1024]) * (SB_SCALE * LOG2E)).astype(BF16)
        sbq_ref[:, 512:1536] = _dot(u, wa_ref[:, 1024:2048]).astype(BF16)

    outs = [
        jax.ShapeDtypeStruct((s, d), BF16),
        jax.ShapeDtypeStruct((s, 256), F32),
        jax.ShapeDtypeStruct((s, 128), F32),
        jax.ShapeDtypeStruct((s, 256), BF16),
        jax.ShapeDtypeStruct((s, 128), BF16),
        jax.ShapeDtypeStruct((s, 512), BF16),
        jax.ShapeDtypeStruct((s, 256), BF16),
        jax.ShapeDtypeStruct((s, 1024), BF16),
        jax.ShapeDtypeStruct((s, 128), BF16),
        jax.ShapeDtypeStruct((s, 1536), BF16),
    ]
    return pl.pallas_call(
        body, name="proj_in_fwd", grid=(s // tm,), out_shape=outs,
        in_specs=[_row_spec(tm, d), _full_spec(g_mix.shape), _full_spec(w_a.shape), _full_spec(g_q.shape),
                  _full_spec(w_uq.shape), _full_spec(g_kv.shape), _full_spec(w_ukv.shape),
                  _row_spec(tm, LANES), _row_spec(tm, LANES), _row_spec(tm, LANES)],
        out_specs=[_row_spec(tm, o.shape[1]) for o in outs],
        compiler_params=_cparams("arbitrary"),
    )(x, g_mix, w_a, g_q, w_uq, g_kv, w_ukv, cos, sin_a, sin_b)


def _attn_out_fwd(o_mla, o_sb, g_mla, g_sb, w_o, x, g_ffn):
    s, d = x.shape
    tm = min(ROW_TILE, s)

    def body(oa_ref, ob_ref, ga_ref, gb_ref, wo_ref, x_ref, gf_ref, merged_ref, h1_ref, f_ref):
        na = _rms(oa_ref[...], ga_ref[...]).astype(BF16)
        nb = _rms(ob_ref[...], gb_ref[...]).astype(BF16)
        merged_ref[:, 0:512] = na
        merged_ref[:, 512:1024] = nb
        h1 = x_ref[...] + _dot(na, wo_ref[0:512, :]) + _dot(nb, wo_ref[512:1024, :])
        h1_ref[...] = h1
        f_ref[...] = _rms(h1, gf_ref[...]).astype(BF16)

    outs = [jax.ShapeDtypeStruct((s, d), BF16), jax.ShapeDtypeStruct((s, d), F32), jax.ShapeDtypeStruct((s, d), BF16)]
    return pl.pallas_call(
        body, name="attn_out_fwd", grid=(s // tm,), out_shape=outs,
        in_specs=[_row_spec(tm, 512), _row_spec(tm, 512), _full_spec(g_mla.shape), _full_spec(g_sb.shape),
                  _full_spec(w_o.shape), _row_spec(tm, d), _full_spec(g_ffn.shape)],
        out_specs=[_row_spec(tm, d)] * 3,
        compiler_params=_cparams("arbitrary"),
    )(o_mla, o_sb, g_mla, g_sb, w_o, x, g_ffn)


def _ffn_tile(d_ff):
    return d_ff // 2 if (d_ff // 2) % LANES == 0 else d_ff


def _ffn_fwd(f, h1, w_gate, w_up, w_down):
    s, d = h1.shape
    d_ff = w_gate.shape[1]
    tm = min(ROW_TILE, s)
    tf = _ffn_tile(d_ff)

    def body(f_ref, h1_ref, wg_ref, wu_ref, wd_ref, gate_ref, up_ref, h2_ref):
        j = pl.program_id(1)
        fb = f_ref[...]
        gate = _dot(fb, wg_ref[...])
        up = _dot(fb, wu_ref[...])
        gate_ref[...] = gate.astype(BF16)
        up_ref[...] = up.astype(BF16)
        act = (gate * jax.nn.sigmoid(gate) * up).astype(BF16)
        part = _dot(act, wd_ref[...])

        @pl.when(j == 0)
        def _():
            h2_ref[...] = h1_ref[...] + part

        @pl.when(j != 0)
        def _():
            h2_ref[...] += part

    outs = [jax.ShapeDtypeStruct((s, d_ff), BF16), jax.ShapeDtypeStruct((s, d_ff), BF16), jax.ShapeDtypeStruct((s, d), F32)]
    return pl.pallas_call(
        body, name="ffn_fwd", grid=(s // tm, d_ff // tf), out_shape=outs,
        in_specs=[pl.BlockSpec((tm, d), lambda r, j: (r, 0)), pl.BlockSpec((tm, d), lambda r, j: (r, 0)),
                  pl.BlockSpec((d, tf), lambda r, j: (0, j)), pl.BlockSpec((d, tf), lambda r, j: (0, j)),
                  pl.BlockSpec((tf, d), lambda r, j: (j, 0))],
        out_specs=[pl.BlockSpec((tm, tf), lambda r, j: (r, j)), pl.BlockSpec((tm, tf), lambda r, j: (r, j)),
                   pl.BlockSpec((tm, d), lambda r, j: (r, 0))],
        compiler_params=_cparams("arbitrary", "arbitrary"),
    )(f, h1, w_gate, w_up, w_down)


def _final_loss(h2, target, g_final):
    s, d = h2.shape
    tm = min(ROW_TILE, s)

    def body(h2_ref, t_ref, g_ref, loss_ref, dh2_ref, dh2b_ref, dg_ref):
        first = pl.program_id(0) == 0
        h2v = h2_ref[...]
        g = g_ref[...]
        diff = _rms(h2v, g) - t_ref[...]
        part = 0.5 * jnp.sum(jnp.mean(diff * diff, axis=-1, keepdims=True), axis=0, keepdims=True)
        _accumulate(loss_ref, jnp.broadcast_to(part, loss_ref.shape), first)
        dx, dg = _rms_bwd(h2v, g, diff * (1.0 / d))
        dh2_ref[...] = dx
        dh2b_ref[...] = dx.astype(BF16)
        _accumulate(dg_ref, dg, first)

    outs = [jax.ShapeDtypeStruct((1, LANES), F32), jax.ShapeDtypeStruct((s, d), F32), jax.ShapeDtypeStruct((s, d), BF16),
            jax.ShapeDtypeStruct((1, d), F32)]
    return pl.pallas_call(
        body, name="final_loss", grid=(s // tm,), out_shape=outs,
        in_specs=[_row_spec(tm, d), _row_spec(tm, d), _full_spec((1, d))],
        out_specs=[_full_spec((1, LANES)), _row_spec(tm, d), _row_spec(tm, d), _full_spec((1, d))],
        compiler_params=_cparams("arbitrary"),
    )(h2, target, g_final)


def _tile_iotas(t):
    return lax.broadcasted_iota(jnp.int32, (t, t), 0), lax.broadcasted_iota(jnp.int32, (t, t), 1)


def _stacked_mask(t, strict):
    row = lax.broadcasted_iota(jnp.int32, (2 * t, t), 0)
    col = lax.broadcasted_iota(jnp.int32, (2 * t, t), 1)
    row = jnp.where(row >= t, row - t, row)
    return col < row if strict else col <= row


def _mla_fwd(qn, qr, kv, kr, shards):
    s = qn.shape[0]
    t = min(ATT_TILE, s)
    pairs = MLA_HEADS // 2
    nq = s // t
    n = len(shards)

    def body(*refs):
        qn_ref, qr_ref, kn_ref, v_ref, kr_ref = refs[:5]
        o_ref, lse_ref = refs[5 + n:7 + n]
        qcat_ref, m_ref, l_ref, acc_ref = refs[7 + 2 * n:11 + 2 * n]
        hp, i = pl.program_id(0), pl.program_id(1)
        ride = _Exchange(True, refs[5:5 + n], refs[7 + n:7 + 2 * n], *refs[11 + 2 * n:])

        @pl.when((hp == 0) & (i == 0))
        def _():
            ride.start()

        lane = lax.broadcasted_iota(jnp.int32, (1, LANES), 1)
        row, col = _tile_iotas(t)
        causal = col <= row
        q_pair, q_quad = qn_ref[...], qr_ref[...]
        zero = jnp.zeros_like(q_pair)
        for hh in range(2):
            in_head = (lane // HEAD_DIM) == hh
            in_rope = (lane // MLA_ROPE) == (hp % 2) * 2 + hh
            qcat_ref[hh * t:(hh + 1) * t, 0:LANES] = jnp.where(in_head, q_pair, zero)
            qcat_ref[hh * t:(hh + 1) * t, LANES:2 * LANES] = jnp.where(in_rope, q_quad, zero)
        m_ref[...] = jnp.full_like(m_ref, NEG)
        l_ref[...] = jnp.zeros_like(l_ref)
        acc_ref[...] = jnp.zeros_like(acc_ref)

        def tile(j, width, masked):
            rows = pl.ds(pl.multiple_of(j * t, t), width * t)
            kcat = jnp.concatenate([kn_ref[rows, :], kr_ref[rows, :]], axis=1)
            v_ones = jnp.concatenate([v_ref[rows, :], jnp.ones((width * t, LANES), BF16)], axis=1)
            scores = [_dot_nt(qcat_ref[hh * t:(hh + 1) * t, :], kcat) for hh in range(2)]
            for hh in range(2):
                half = slice(hh * t, (hh + 1) * t)
                sc = jnp.where(causal, scores[hh], NEG) if masked else scores[hh]
                m = m_ref[half, :]
                m_new = jnp.maximum(m, jnp.max(sc, axis=-1, keepdims=True))
                alpha = jnp.exp(m - m_new)
                p = jnp.exp(sc - jnp.concatenate([m_new] * (width * t // LANES), axis=1))
                pv = _dot(p.astype(BF16), v_ones)
                l_ref[half, :] = alpha * l_ref[half, :] + pv[:, LANES:]
                acc_ref[half, :] = alpha * acc_ref[half, :] + pv[:, :LANES]
                m_ref[half, :] = m_new

        tile(i, 1, True)

        def step(n, carry):
            tile(2 * n, 2, False)
            return carry

        lax.fori_loop(0, i // 2, step, 0)

        @pl.when(i % 2 == 1)
        def _():
            tile(i - 1, 1, False)
        first = (lane // HEAD_DIM) == 0
        o = acc_ref[...] / l_ref[...]
        lse = m_ref[...] + jnp.log(l_ref[...])
        o_ref[...] = jnp.where(first, o[0:t], o[t:2 * t])
        lse_ref[...] = jnp.where(first, lse[0:t], lse[t:2 * t])

        @pl.when((hp == pairs - 1) & (i == nq - 1))
        def _():
            ride.finish()

    gathered_shapes, sems = _exchange_shapes(True, shards)
    outs = [jax.ShapeDtypeStruct((s, 512), F32), jax.ShapeDtypeStruct((pairs, s, LANES), F32)] + gathered_shapes
    res = pl.pallas_call(
        body, name="mla_fwd", grid=(pairs, nq), out_shape=outs,
        in_specs=[pl.BlockSpec((t, LANES), lambda hp, i: (i, hp)), pl.BlockSpec((t, LANES), lambda hp, i: (i, hp // 2)),
                  pl.BlockSpec((s, LANES), lambda hp, i: (0, hp)), pl.BlockSpec((s, LANES), lambda hp, i: (0, 4 + hp)),
                  pl.BlockSpec((s, LANES), lambda hp, i: (0, 0))] + [ANY] * n,
        out_specs=[pl.BlockSpec((t, LANES), lambda hp, i: (i, hp)), pl.BlockSpec((None, t, LANES), lambda hp, i: (hp, i, 0))]
        + [ANY] * n,
        scratch_shapes=[pltpu.VMEM((2 * t, 2 * LANES), BF16), pltpu.VMEM((2 * t, LANES), F32), pltpu.VMEM((2 * t, LANES), F32),
                        pltpu.VMEM((2 * t, LANES), F32)] + sems,
        compiler_params=_cparams("arbitrary", "arbitrary"),
    )(qn, qr, kv, kv, kr, *shards)
    return res[0], res[1], res[2:]


HEADS = (0, 1)


def _sb_logs(z2, strict, masked):
    log_b = jnp.minimum(z2, 0.0) - jnp.log2(1.0 + jnp.exp2(-jnp.abs(z2)))
    log_1m = log_b - z2
    if masked:
        log_1m = jnp.where(strict, log_1m, 0.0)
    return log_1m, log_b


def _block_totals(x):
    t, w = x.shape
    nb = max(w // TRI, 1)
    bw = w // nb
    blocks = [x[:, b * bw:(b + 1) * bw] for b in range(nb)]
    totals = [jnp.broadcast_to(jnp.sum(blk, axis=-1, keepdims=True), (t, LANES)) for blk in blocks]
    whole = totals[0]
    for tot in totals[1:]:
        whole = whole + tot
    return blocks, totals, whole


def _running_sums(blocks, totals, tri, carry, suffix):
    nb = len(blocks)
    reps = blocks[0].shape[1] // LANES
    outs = [None] * nb
    run = carry
    for b in (range(nb - 1, -1, -1) if suffix else range(nb)):
        outs[b] = _dot(blocks[b].astype(BF16), tri) + jnp.concatenate([run] * reps, axis=1)
        run = run + totals[b]
    return outs[0] if nb == 1 else jnp.concatenate(outs, axis=1)


def _tri(t, rel):
    n = min(TRI, t)
    row, col = _tile_iotas(n)
    return rel(row, col).astype(BF16)


def _sweep_width(t):
    return t // 2 if t // 2 >= TRI else t


def _sb_fwd(qkv):
    s = qkv.shape[0]
    t = min(SB_TILE, s)
    sw = _sweep_width(t)
    pairs = SB_HEADS // 2

    def body(q_ref, k_ref, v_ref, o_ref, tot_ref, cnt_ref, qm_ref, right_ref, acc_ref):
        i = pl.program_id(1)
        lane = lax.broadcasted_iota(jnp.int32, (1, LANES), 1)
        row, col = _tile_iotas(t)
        strict = col < row
        t_suffix = _tri(t, lambda r, c: r > c)
        q_pair = q_ref[...]
        for hh in range(2):
            qm_ref[hh] = jnp.where((lane // HEAD_DIM) == hh, q_pair, jnp.zeros_like(q_pair))
        right_ref[...] = jnp.zeros_like(right_ref)
        acc_ref[...] = jnp.zeros_like(acc_ref)

        def tile(start, width, masked):
            rows = pl.ds(pl.multiple_of(start, width), width)
            k, v = k_ref[rows, :], v_ref[rows, :]
            for hh in HEADS:
                log_1m, log_b = _sb_logs(_dot_nt(qm_ref[hh], k), strict, masked)
                blocks, totals, whole = _block_totals(log_1m)
                a = jnp.exp2(log_b + _running_sums(blocks, totals, t_suffix, right_ref[hh], True))
                if masked:
                    a = jnp.where(strict, a, 0.0)
                right_ref[hh] += whole
                acc_ref[hh] += _dot(a.astype(BF16), v)

        tile(i * t, t, True)

        def alive(n):
            return (n < i * (t // sw)) & (jnp.max(right_ref[...]) > SB_DEAD)

        def step(n):
            tile((i * (t // sw) - 1 - n) * sw, sw, False)
            return n + 1

        swept = lax.while_loop(alive, step, jnp.int32(0))
        cnt_ref[...] = jnp.full(cnt_ref.shape, swept.astype(F32))
        first = (lane // HEAD_DIM) == 0
        o_ref[...] = jnp.where(first, acc_ref[0], acc_ref[1])
        tot_ref[...] = jnp.where(first, right_ref[0], right_ref[1])

    outs = [jax.ShapeDtypeStruct((s, 512), F32), jax.ShapeDtypeStruct((pairs, s, LANES), F32),
            jax.ShapeDtypeStruct((pairs, s // t, 8, LANES), F32)]
    return pl.pallas_call(
        body, name="sb_fwd", grid=(pairs, s // t), out_shape=outs,
        in_specs=[pl.BlockSpec((t, LANES), lambda hp, i: (i, hp)), pl.BlockSpec((s, LANES), lambda hp, i: (0, 4 + hp)),
                  pl.BlockSpec((s, LANES), lambda hp, i: (0, 8 + hp))],
        out_specs=[pl.BlockSpec((t, LANES), lambda hp, i: (i, hp)), pl.BlockSpec((None, t, LANES), lambda hp, i: (hp, i, 0)),
                   pl.BlockSpec((None, None, 8, LANES), lambda hp, i: (hp, i, 0, 0))],
        scratch_shapes=[pltpu.VMEM((2, t, LANES), BF16), pltpu.VMEM((2, t, LANES), F32), pltpu.VMEM((2, t, LANES), F32)],
        compiler_params=_cparams("arbitrary", "arbitrary"),
    )(qkv, qkv, qkv)


def _sb_bwd(qkv, do, tot, cnt):
    s = qkv.shape[0]
    t = min(SB_TILE, s)
    sw = _sweep_width(t)
    pairs = SB_HEADS // 2

    def body(q_ref, k_ref, v_ref, do_ref, tot_ref, cnt_ref, dq_ref, dk_ref, dv_ref,
             qm_ref, dob_ref, total_s, left_l, left_g, dq_s):
        i = pl.program_id(1)

        @pl.when(i == 0)
        def _():
            dk_ref[...] = jnp.zeros_like(dk_ref)
            dv_ref[...] = jnp.zeros_like(dv_ref)

        lane = lax.broadcasted_iota(jnp.int32, (1, LANES), 1)
        row, col = _tile_iotas(t)
        strict = col < row
        t_suffix = _tri(t, lambda r, c: r > c)
        t_excl = _tri(t, lambda r, c: r < c)
        q_pair, do_pair, tot_pair = q_ref[...], do_ref[...], tot_ref[...]
        for hh in range(2):
            in_head = (lane // HEAD_DIM) == hh
            qm_ref[hh] = jnp.where(in_head, q_pair, jnp.zeros_like(q_pair))
            dob_ref[hh] = jnp.where(in_head, do_pair, 0.0).astype(BF16)
            total_s[hh] = jnp.broadcast_to(
                jnp.sum(jnp.where(lane == hh * HEAD_DIM, tot_pair, 0.0), axis=-1, keepdims=True), (t, LANES))
        left_l[...] = jnp.zeros_like(left_l)
        left_g[...] = jnp.zeros_like(left_g)
        dq_s[...] = jnp.zeros_like(dq_s)
        reps = t // LANES

        def tile(start, width, masked):
            rows = pl.ds(pl.multiple_of(start, width), width)
            k, v = k_ref[rows, :], v_ref[rows, :]
            z2 = [_dot_nt(qm_ref[hh], k) for hh in HEADS]
            d_a = [_dot_nt(dob_ref[hh], v) for hh in HEADS]
            for hh in HEADS:
                qm, dob = qm_ref[hh], dob_ref[hh]
                log_1m, log_b = _sb_logs(z2[hh], strict, masked)
                blocks, totals, whole = _block_totals(log_1m)
                done = left_l[hh] + whole
                left_l[hh] = done
                a = jnp.exp2(log_b + _running_sums(blocks, totals, t_suffix, total_s[hh] - done, True))
                if masked:
                    a = jnp.where(strict, a, 0.0)
                g = a * d_a[hh]
                blocks, totals, whole = _block_totals(g)
                before = _running_sums(blocks, totals, t_excl, left_g[hh], False)
                left_g[hh] += whole
                dz = g - jnp.exp2(log_b) * (g + before)
                if masked:
                    dz = jnp.where(strict, dz, 0.0)
                dzb = dz.astype(BF16)
                dq_s[hh] += _dot(dzb, k)
                dk_ref[rows, :] += _dot_tn(dzb, qm)
                dv_ref[rows, :] += _dot_tn(a.astype(BF16), dob)

        def step(h, carry):
            tile(h * sw, sw, False)
            return carry

        swept = jnp.max(cnt_ref[...]).astype(jnp.int32)
        lax.fori_loop(i * (t // sw) - swept, i * (t // sw), step, 0)
        tile(i * t, t, True)
        dq_ref[...] = jnp.where((lane // HEAD_DIM) == 0, dq_s[0], dq_s[1])

        @pl.when(i == s // t - 1)
        def _():
            dk_ref[...] *= 1.0 / LOG2E

    outs = [jax.ShapeDtypeStruct((s, 512), F32)] * 3
    return pl.pallas_call(
        body, name="sb_bwd", grid=(pairs, s // t), out_shape=outs,
        in_specs=[pl.BlockSpec((t, LANES), lambda hp, i: (i, hp)), pl.BlockSpec((s, LANES), lambda hp, i: (0, 4 + hp)),
                  pl.BlockSpec((s, LANES), lambda hp, i: (0, 8 + hp)), pl.BlockSpec((t, LANES), lambda hp, i: (i, hp)),
                  pl.BlockSpec((None, t, LANES), lambda hp, i: (hp, i, 0)),
                  pl.BlockSpec((None, None, 8, LANES), lambda hp, i: (hp, i, 0, 0))],
        out_specs=[pl.BlockSpec((t, LANES), lambda hp, i: (i, hp)), pl.BlockSpec((s, LANES), lambda hp, i: (0, hp)),
                   pl.BlockSpec((s, LANES), lambda hp, i: (0, hp))],
        scratch_shapes=[pltpu.VMEM((2, t, LANES), BF16), pltpu.VMEM((2, t, LANES), BF16)]
        + [pltpu.VMEM((2, t, LANES), F32)] * 4,
        compiler_params=_cparams("arbitrary", "arbitrary"),
    )(qkv, qkv, qkv, do, tot, cnt)


def _mla_bwd(qn, qr, kv, kr, do, o, lse, parts):
    s = qn.shape[0]
    t = min(ATT_TILE, s)
    pairs = MLA_HEADS // 2
    nq = s // t
    n = len(parts)

    def body(*refs):
        qn_ref, qr_ref, kn_ref, v_ref, kr_ref, do_ref, o_ref, lse_ref = refs[:8]
        dqn_ref, dqr_ref, dkn_ref, dv_ref, dkr_ref = refs[8 + n:13 + n]
        qcat_ref, dob_ref, lse_s, delta_s, dq_s = refs[13 + 2 * n:18 + 2 * n]
        hp, i = pl.program_id(0), pl.program_id(1)
        ride = _Exchange(False, refs[8:8 + n], refs[13 + n:13 + 2 * n], *refs[18 + 2 * n:])

        @pl.when((hp == 0) & (i == 0))
        def _():
            ride.start()

        @pl.when(i == 0)
        def _():
            dkn_ref[...] = jnp.zeros_like(dkn_ref)
            dv_ref[...] = jnp.zeros_like(dv_ref)
            dkr_ref[...] = jnp.zeros_like(dkr_ref)

        lane = lax.broadcasted_iota(jnp.int32, (1, LANES), 1)
        row, col = _tile_iotas(t)
        causal = col <= row
        q_pair, q_quad, do_pair, lse_pair = qn_ref[...], qr_ref[...], do_ref[...], lse_ref[...]
        do_o = do_pair * o_ref[...]
        zero = jnp.zeros_like(q_pair)
        ropes = []
        for hh in range(2):
            in_head = (lane // HEAD_DIM) == hh
            in_rope = (lane // MLA_ROPE) == (hp % 2) * 2 + hh
            ropes.append(in_rope)
            qcat_ref[hh, :, 0:LANES] = jnp.where(in_head, q_pair, zero)
            qcat_ref[hh, :, LANES:2 * LANES] = jnp.where(in_rope, q_quad, zero)
            dob_ref[hh] = jnp.where(in_head, do_pair, 0.0).astype(BF16)
            delta_s[hh] = jnp.broadcast_to(jnp.sum(jnp.where(in_head, do_o, 0.0), axis=-1, keepdims=True), (t, LANES))
            lse_s[hh] = jnp.broadcast_to(
                jnp.sum(jnp.where(lane == hh * HEAD_DIM, lse_pair, 0.0), axis=-1, keepdims=True), (t, LANES))
        dq_s[...] = jnp.zeros_like(dq_s)
        reps = t // LANES

        def tile(j, width, masked):
            rows = pl.ds(pl.multiple_of(j * t, t), width * t)
            kcat = jnp.concatenate([kn_ref[rows, :], kr_ref[rows, :]], axis=1)
            v = v_ref[rows, :]
            sc = [_dot_nt(qcat_ref[hh], kcat) for hh in HEADS]
            dp = [_dot_nt(dob_ref[hh], v) for hh in HEADS]
            p = [jnp.exp(sc[hh] - jnp.concatenate([lse_s[hh]] * (width * reps), axis=1)) for hh in HEADS]
            if masked:
                p = [jnp.where(causal, p[hh], 0.0) for hh in HEADS]
            ds = [(p[hh] * (dp[hh] - jnp.concatenate([delta_s[hh]] * (width * reps), axis=1))).astype(BF16) for hh in HEADS]
            for hh in HEADS:
                dq_s[hh] += _dot(ds[hh], kcat)
            dkcat = _dot_tn(ds[0], qcat_ref[0]) + _dot_tn(ds[1], qcat_ref[1])
            dkn_ref[rows, :] += dkcat[:, 0:LANES]
            dkr_ref[rows, :] += dkcat[:, LANES:2 * LANES]
            dv_ref[rows, :] += _dot_tn(p[0].astype(BF16), dob_ref[0]) + _dot_tn(p[1].astype(BF16), dob_ref[1])

        def step(n, carry):
            tile(2 * n, 2, False)
            return carry

        lax.fori_loop(0, i // 2, step, 0)

        @pl.when(i % 2 == 1)
        def _():
            tile(i - 1, 1, False)

        tile(i, 1, True)
        dqn_ref[...] = jnp.where((lane // HEAD_DIM) == 0, dq_s[0, :, 0:LANES], dq_s[1, :, 0:LANES])
        dqr_ref[...] = (jnp.where(ropes[0], dq_s[0, :, LANES:2 * LANES], 0.0)
                        + jnp.where(ropes[1], dq_s[1, :, LANES:2 * LANES], 0.0))

        @pl.when((hp == pairs - 1) & (i == nq - 1))
        def _():
            ride.finish()

    pair_block = pl.BlockSpec((t, LANES), lambda hp, i: (i, hp))
    once = pl.Buffered(1)
    landed_shapes, sems = _exchange_shapes(False, parts)
    outs = [jax.ShapeDtypeStruct((s, 512), F32), jax.ShapeDtypeStruct((pairs, s, LANES), F32),
            jax.ShapeDtypeStruct((s, 512), F32), jax.ShapeDtypeStruct((s, 512), F32),
            jax.ShapeDtypeStruct((pairs, s, LANES), F32)] + landed_shapes
    res = pl.pallas_call(
        body, name="mla_bwd", grid=(pairs, nq), out_shape=outs,
        in_specs=[pair_block, pl.BlockSpec((t, LANES), lambda hp, i: (i, hp // 2)),
                  pl.BlockSpec((s, LANES), lambda hp, i: (0, hp), pipeline_mode=once),
                  pl.BlockSpec((s, LANES), lambda hp, i: (0, 4 + hp), pipeline_mode=once),
                  pl.BlockSpec((s, LANES), lambda hp, i: (0, 0), pipeline_mode=once), pair_block, pair_block,
                  pl.BlockSpec((None, t, LANES), lambda hp, i: (hp, i, 0))] + [ANY] * n,
        out_specs=[pair_block, pl.BlockSpec((None, t, LANES), lambda hp, i: (hp, i, 0)),
                   pl.BlockSpec((s, LANES), lambda hp, i: (0, hp), pipeline_mode=once),
                   pl.BlockSpec((s, LANES), lambda hp, i: (0, hp), pipeline_mode=once),
                   pl.BlockSpec((None, s, LANES), lambda hp, i: (hp, 0, 0), pipeline_mode=once)] + [ANY] * n,
        scratch_shapes=[pltpu.VMEM((2, t, 2 * LANES), BF16), pltpu.VMEM((2, t, LANES), BF16), pltpu.VMEM((2, t, LANES), F32),
                        pltpu.VMEM((2, t, LANES), F32), pltpu.VMEM((2, t, 2 * LANES), F32)] + sems,
        compiler_params=_cparams("arbitrary", "arbitrary"),
    )(qn, qr, kv, kv, kr, do, o, lse, *parts)
    return res[:5], res[5:]


def _ffn_bwd(dh2, dh2b, gate, up, h1, g_ffn, w_down, w_gate, w_up):
    s, d = h1.shape
    d_ff = gate.shape[1]
    tm = min(FFN_BWD_ROW_TILE, s)
    tf = _ffn_tile(d_ff)

    def act_body(dh2b_ref, gate_ref, up_ref, wd_ref, dgate_ref, dup_ref, act_ref):
        dact = _dot_nt(dh2b_ref[...], wd_ref[...])
        gate_v = gate_ref[...].astype(F32)
        up_v = up_ref[...].astype(F32)
        sig = jax.nn.sigmoid(gate_v)
        silu = gate_v * sig
        dup_ref[...] = (dact * silu).astype(BF16)
        dgate_ref[...] = (dact * up_v * (sig * (1.0 + gate_v * (1.0 - sig)))).astype(BF16)
        act_ref[...] = (silu * up_v).astype(BF16)

    ff = pl.BlockSpec((tm, tf), lambda j, r: (r, j))
    dgate, dup, act = pl.pallas_call(
        act_body, name="ffn_bwd_act", grid=(d_ff // tf, s // tm), out_shape=[jax.ShapeDtypeStruct((s, d_ff), BF16)] * 3,
        in_specs=[pl.BlockSpec((tm, d), lambda j, r: (r, 0)), ff, ff, pl.BlockSpec((tf, d), lambda j, r: (j, 0))],
        out_specs=[ff, ff, ff],
        compiler_params=_cparams("arbitrary", "arbitrary"),
    )(dh2b, gate, up, w_down)

    def df_body(dgate_ref, dup_ref, dh2_ref, h1_ref, g_ref, wg_ref, wu_ref, dh1_ref, dh1b_ref, dg_ref):
        df = _dot_nt(dgate_ref[...], wg_ref[...]) + _dot_nt(dup_ref[...], wu_ref[...])
        dx, dg = _rms_bwd(h1_ref[...], g_ref[...], df)
        dh1 = dh2_ref[...] + dx
        dh1_ref[...] = dh1
        dh1b_ref[...] = dh1.astype(BF16)
        _accumulate(dg_ref, dg, pl.program_id(0) == 0)

    outs = [jax.ShapeDtypeStruct((s, d), F32), jax.ShapeDtypeStruct((s, d), BF16), jax.ShapeDtypeStruct((1, d), F32)]
    dh1, dh1b, dg = pl.pallas_call(
        df_body, name="ffn_bwd_df", grid=(s // tm,), out_shape=outs,
        in_specs=[_row_spec(tm, d_ff), _row_spec(tm, d_ff), _row_spec(tm, d), _row_spec(tm, d), _full_spec((1, d)),
                  _full_spec(w_gate.shape), _full_spec(w_up.shape)],
        out_specs=[_row_spec(tm, d), _row_spec(tm, d), _full_spec((1, d))],
        compiler_params=_cparams("arbitrary"),
    )(dgate, dup, dh2, h1, g_ffn, w_gate, w_up)
    return dgate, dup, act, dh1, dh1b, dg


def _largest_tile(n, cap):
    for cand in range(cap, 0, -LANES):
        if n % cand == 0:
            return cand
    return n


def _tn_matmul(a, b, name):
    assert a.dtype == BF16 and b.dtype == BF16
    s, m = a.shape
    n = b.shape[1]
    if s * m * 2 <= TN_RESIDENT_BYTES:
        tm, tn = m, min(n, TN_BLOCK)
    else:
        tm, tn = TN_BLOCK, n

    def body(a_ref, b_ref, o_ref):
        o_ref[...] = _dot_tn(a_ref[...], b_ref[...]).astype(BF16)

    return pl.pallas_call(
        body, name=name, grid=(m // tm, n // tn), out_shape=jax.ShapeDtypeStruct((m, n), BF16),
        in_specs=[pl.BlockSpec((s, tm), lambda i, j: (0, i)), pl.BlockSpec((s, tn), lambda i, j: (0, j))],
        out_specs=pl.BlockSpec((tm, tn), lambda i, j: (i, j)),
        compiler_params=_cparams("arbitrary", "arbitrary"),
    )(a, b)


def _attn_out_bwd(dh1, w_o, o_mla, o_sb, g_mla, g_sb):
    s, d = dh1.shape
    tm = min(ROW_TILE, s)

    def body(dh1_ref, wo_ref, oa_ref, ob_ref, ga_ref, gb_ref, doa_ref, dob_ref, dga_ref, dgb_ref):
        first = pl.program_id(0) == 0
        dh1b = dh1_ref[...]
        dxa, dga = _rms_bwd(oa_ref[...], ga_ref[...], _dot_nt(dh1b, wo_ref[0:512, :]))
        dxb, dgb = _rms_bwd(ob_ref[...], gb_ref[...], _dot_nt(dh1b, wo_ref[512:1024, :]))
        doa_ref[...] = dxa
        dob_ref[...] = dxb
        _accumulate(dga_ref, dga, first)
        _accumulate(dgb_ref, dgb, first)

    outs = [jax.ShapeDtypeStruct((s, 512), F32)] * 2 + [jax.ShapeDtypeStruct((1, 512), F32)] * 2
    return pl.pallas_call(
        body, name="attn_out_bwd", grid=(s // tm,), out_shape=outs,
        in_specs=[_row_spec(tm, d), _full_spec(w_o.shape), _row_spec(tm, 512), _row_spec(tm, 512),
                  _full_spec((1, 512)), _full_spec((1, 512))],
        out_specs=[_row_spec(tm, 512), _row_spec(tm, 512), _full_spec((1, 512)), _full_spec((1, 512))],
        compiler_params=_cparams("arbitrary"),
    )(dh1, w_o, o_mla, o_sb, g_mla, g_sb)


def _proj_in_bwd(dqn, dqr, dkn, dv, dkr, dq_sb, dk_sb, dv_sb, cq, ckv, x, dh1, cos, sin_a, sin_b,
                 g_q, g_kv, g_mix, w_uq, w_ukv, w_a):
    s, d = x.shape
    tm = min(PROJ_BWD_ROW_TILE, s)

    def body(dqn_ref, dqr_ref, dkn_ref, dv_ref, dkr_ref, dqs_ref, dks_ref, dvs_ref, cq_ref, ckv_ref, x_ref, dh1_ref,
             cos_ref, sa_ref, sb_ref, gq_ref, gkv_ref, gm_ref, wuq_ref, wukv_ref, wa_ref,
             dx_ref, dproj_ref, dq_ref, dkv_ref, dgq_ref, dgkv_ref, dgm_ref):
        first = pl.program_id(0) == 0
        lane = lax.broadcasted_iota(jnp.int32, (1, LANES), 1)
        cos_t, sa_t, sb_t = cos_ref[...], sa_ref[...], sb_ref[...]
        dq_ref[:, 0:512] = (dqn_ref[...] * MLA_SCALE).astype(BF16)
        for half in range(2):
            quad = (dqr_ref[2 * half] + dqr_ref[2 * half + 1]) * MLA_SCALE
            dq_ref[:, 512 + half * LANES:512 + (half + 1) * LANES] = _rope_t(quad, cos_t, sa_t, sb_t).astype(BF16)
        dcq, dgq = _rms_bwd(cq_ref[...], gq_ref[...], _dot_nt(dq_ref[...], wuq_ref[...]))
        _accumulate(dgq_ref, dgq, first)
        dkv_ref[:, 0:512] = dkn_ref[...].astype(BF16)
        dkv_ref[:, 512:1024] = dv_ref[...].astype(BF16)
        dckv, dgkv = _rms_bwd(ckv_ref[...], gkv_ref[...], _dot_nt(dkv_ref[...], wukv_ref[...]))
        _accumulate(dgkv_ref, dgkv, first)
        g = _rope_t(dkr_ref[0] + dkr_ref[1] + dkr_ref[2] + dkr_ref[3], cos_t, sa_t, sb_t)
        g = g + pltpu.roll(g, 96, 1) + pltpu.roll(g, 64, 1) + pltpu.roll(g, 32, 1)
        dproj_ref[:, 0:256] = dcq.astype(BF16)
        dproj_ref[:, 256:384] = dckv.astype(BF16)
        dproj_ref[:, 384:512] = jnp.where(lane < MLA_ROPE, g, 0.0).astype(BF16)
        dproj_ref[:, 512:1024] = (dqs_ref[...] * SB_SCALE).astype(BF16)
        dproj_ref[:, 1024:1536] = dks_ref[...].astype(BF16)
        dproj_ref[:, 1536:2048] = dvs_ref[...].astype(BF16)
        dxn, dgm = _rms_bwd(x_ref[...], gm_ref[...], _dot_nt(dproj_ref[...], wa_ref[...]))
        dx_ref[...] = dh1_ref[...] + dxn
        _accumulate(dgm_ref, dgm, first)

    quad_spec = pl.BlockSpec((4, tm, LANES), lambda r: (0, r, 0))
    outs = [jax.ShapeDtypeStruct((s, d), F32), jax.ShapeDtypeStruct((s, 2048), BF16), jax.ShapeDtypeStruct((s, 768), BF16),
            jax.ShapeDtypeStruct((s, 1024), BF16), jax.ShapeDtypeStruct((1, 256), F32), jax.ShapeDtypeStruct((1, 128), F32),
            jax.ShapeDtypeStruct((1, d), F32)]
    return pl.pallas_call(
        body, name="proj_in_bwd", grid=(s // tm,), out_shape=outs,
        in_specs=[_row_spec(tm, 512), quad_spec, _row_spec(tm, 512), _row_spec(tm, 512), quad_spec,
                  _row_spec(tm, 512), _row_spec(tm, 512), _row_spec(tm, 512), _row_spec(tm, 256), _row_spec(tm, 128),
                  _row_spec(tm, d), _row_spec(tm, d), _row_spec(tm, LANES), _row_spec(tm, LANES), _row_spec(tm, LANES),
                  _full_spec((1, 256)), _full_spec((1, 128)), _full_spec((1, d)),
                  _full_spec(w_uq.shape), _full_spec(w_ukv.shape), _full_spec(w_a.shape)],
        out_specs=[_row_spec(tm, d), _row_spec(tm, 2048), _row_spec(tm, 768), _row_spec(tm, 1024),
                   _full_spec((1, 256)), _full_spec((1, 128)), _full_spec((1, d))],
        compiler_params=_cparams("arbitrary"),
    )(dqn, dqr, dkn, dv, dkr, dq_sb, dk_sb, dv_sb, cq, ckv, x, dh1, cos, sin_a, sin_b, g_q, g_kv, g_mix,
      w_uq, w_ukv, w_a)


ANY = pl.BlockSpec(memory_space=pl.ANY)


def _place():
    return lax.axis_index("x"), lax.axis_index("y"), lax.axis_index("c")


def _all_gather(shards, name):
    n = len(shards)

    def body(*refs):
        ins, outs = refs[:n], refs[n:2 * n]
        send_sems, recv_sems, local_sems = refs[2 * n:]
        x, y, c = _place()
        me, sibling = (x, y, c), (x, y, 1 - c)
        chips = [(1 - x, y), (x, 1 - y), (1 - x, 1 - y)]

        def slot(a, px, py, pc):
            return outs[a].at[4 * px + 2 * py + pc]

        def copy(a, k, block, to, src=None):
            return pltpu.make_async_remote_copy(
                src_ref=slot(a, *block) if src is None else src, dst_ref=slot(a, *block),
                send_sem=send_sems.at[a, k], recv_sem=recv_sems.at[a, k], device_id=to, device_id_type=MESH)

        mine, first, passed = [], [], []
        for a in range(n):
            own = pltpu.make_async_copy(ins[a], slot(a, *me), local_sems.at[a])
            own.start()
            mine.append(own)
            cps = [copy(a, 0, me, sibling, src=ins[a])]
            cps += [copy(a, 1 + j, me, (*chip, c), src=ins[a]) for j, chip in enumerate(chips)]
            for cp in cps:
                cp.start()
            first += cps
        for a in range(n):
            for j, chip in enumerate(chips):
                copy(a, 1 + j, (*chip, c), me).wait_recv()
                fwd = copy(a, 4 + j, (*chip, c), sibling)
                fwd.start()
                passed.append(fwd)
        for a in range(n):
            copy(a, 0, sibling, me).wait_recv()
            for j, chip in enumerate(chips):
                copy(a, 4 + j, (*chip, 1 - c), me).wait_recv()
        for cp in first + passed:
            cp.wait_send()
        for own in mine:
            own.wait()

    return pl.pallas_call(
        body, name=name,
        out_shape=[jax.ShapeDtypeStruct((N_DEV,) + v.shape, v.dtype) for v in shards],
        in_specs=[ANY] * n, out_specs=[ANY] * n,
        scratch_shapes=[pltpu.SemaphoreType.DMA((n, 7)), pltpu.SemaphoreType.DMA((n, 7)), pltpu.SemaphoreType.DMA((n,))],
    )(*shards)


class _Exchange:
    def __init__(self, gather, ins, outs, send_sems, recv_sems, local_sems):
        self.gather, self.ins, self.outs = gather, ins, outs
        self.sems = (send_sems, recv_sems, local_sems)
        x, y, c = _place()
        self.me = 4 * x + 2 * y + c
        self.peers = []
        for k in range(1, N_DEV):
            px = 1 - x if k & 4 else x
            py = 1 - y if k & 2 else y
            pc = 1 - c if k & 1 else c
            self.peers.append(((px, py, pc), 4 * px + 2 * py + pc))

    def _remote(self, a, k, landing):
        send_sems, recv_sems, _ = self.sems
        where, number = self.peers[k]
        src = self.ins[a] if self.gather else self.ins[a].at[number]
        return pltpu.make_async_remote_copy(
            src_ref=src, dst_ref=self.outs[a].at[landing], send_sem=send_sems.at[a, k], recv_sem=recv_sems.at[a, k],
            device_id=where, device_id_type=MESH)

    def _local(self, a):
        src = self.ins[a] if self.gather else self.ins[a].at[self.me]
        return pltpu.make_async_copy(src, self.outs[a].at[self.me], self.sems[2].at[a])

    def start(self):
        for a in range(len(self.ins)):
            self._local(a).start()
            for k in range(N_DEV - 1):
                self._remote(a, k, self.me).start()

    def finish(self):
        for a in range(len(self.ins)):
            for k in range(N_DEV - 1):
                self._remote(a, k, self.peers[k][1]).wait_recv()
            for k in range(N_DEV - 1):
                self._remote(a, k, self.me).wait_send()
            self._local(a).wait()


def _exchange_shapes(gather, arrays):
    out_shape = [jax.ShapeDtypeStruct(((N_DEV,) + v.shape) if gather else v.shape, v.dtype) for v in arrays]
    n = len(arrays)
    sems = [pltpu.SemaphoreType.DMA((n, N_DEV - 1)), pltpu.SemaphoreType.DMA((n, N_DEV - 1)), pltpu.SemaphoreType.DMA((n,))]
    return out_shape, sems


def _exchange(gathers, scatters, name):
    ng, ns = len(gathers), len(scatters)
    n = ng + ns

    def body(*refs):
        ins, outs, sems = refs[:n], refs[n:2 * n], refs[2 * n:]
        both = [_Exchange(True, ins[:ng], outs[:ng], *sems[:3]), _Exchange(False, ins[ng:], outs[ng:], *sems[3:])]
        for ex in both:
            ex.start()
        for ex in both:
            ex.finish()

    g_shapes, g_sems = _exchange_shapes(True, gathers)
    s_shapes, s_sems = _exchange_shapes(False, scatters)
    res = pl.pallas_call(body, name=name, out_shape=g_shapes + s_shapes, in_specs=[ANY] * n, out_specs=[ANY] * n,
                         scratch_shapes=g_sems + s_sems)(*gathers, *scatters)
    return res[:ng], res[ng:]


def _grad_row_tile(rows):
    return _largest_tile_rows(rows, 256)


def _largest_tile_rows(rows, cap):
    for cand in range(cap, 0, -8):
        if rows % cand == 0:
            return cand
    return rows


def _adamw_math(w, g, m, v):
    m_new = ADAM_B1 * m + (1.0 - ADAM_B1) * g
    v_new = ADAM_B2 * v + (1.0 - ADAM_B2) * (g * g)
    m_hat = m_new / (1.0 - ADAM_B1 ** ADAM_STEP)
    v_hat = v_new / (1.0 - ADAM_B2 ** ADAM_STEP)
    delta = -ADAM_LR * (m_hat / (jnp.sqrt(v_hat) + ADAM_EPS) + ADAM_WD * w)
    return delta, m_new, v_new


def _adamw(slots, w, m, v, name):
    k, r, cdim = slots.shape
    tr = _grad_row_tile(r)

    def body(s_ref, w_ref, m_ref, v_ref, g_ref, d_ref, mo_ref, vo_ref):
        g = s_ref[0].astype(F32)
        for q in range(1, k):
            g = g + s_ref[q].astype(F32)
        g_ref[...] = g
        d_ref[...], mo_ref[...], vo_ref[...] = _adamw_math(w_ref[...], g, m_ref[...], v_ref[...])

    blk = pl.BlockSpec((tr, cdim), lambda i: (i, 0))
    return pl.pallas_call(
        body, name=name, grid=(r // tr,), out_shape=[jax.ShapeDtypeStruct((r, cdim), F32)] * 4,
        in_specs=[pl.BlockSpec((k, tr, cdim), lambda i: (0, i, 0)), blk, blk, blk], out_specs=[blk] * 4,
        compiler_params=_cparams("arbitrary"),
    )(slots, w, m, v)


def _stack_cols(g):
    n, r, c = g.shape
    return g.transpose(1, 0, 2).reshape(r, n * c)


def _split_cols(w):
    r, nc = w.shape
    return w.reshape(r, N_DEV, nc // N_DEV).transpose(1, 0, 2)


def _rope_tables(positions):
    inv_freq = ROPE_THETA ** (-jnp.arange(0, MLA_ROPE, 2, dtype=F32) / MLA_ROPE)
    ang = positions.astype(F32).reshape(-1, 1) * inv_freq[None, :]
    cos, sin, zero = jnp.cos(ang), jnp.sin(ang), jnp.zeros_like(ang)
    reps = LANES // MLA_ROPE
    return (jnp.tile(jnp.concatenate([cos, cos], axis=1), (1, reps)),
            jnp.tile(jnp.concatenate([-sin, zero], axis=1), (1, reps)),
            jnp.tile(jnp.concatenate([zero, sin], axis=1), (1, reps)))


def _local_step(x, positions, loss_target, gains, g_in, g_uq, g_ukv, late_shards):
    norm_mix, q_norm, kv_norm, out_mla, out_sb, norm_ffn, norm_final = gains
    d = x.shape[1]
    w_in = _stack_cols(g_in)
    w_a = jnp.concatenate([w_in[:, :416], jnp.zeros((d, 96), BF16), w_in[:, 416:]], axis=1)
    w_uq = jnp.concatenate([g_uq[:, :, :MLA_NOPE].transpose(1, 0, 2).reshape(Q_LORA, -1),
                            g_uq[:, :, MLA_NOPE:].transpose(1, 0, 2).reshape(Q_LORA, -1)], axis=1)
    w_ukv = jnp.concatenate([g_ukv[:, :, :MLA_NOPE].transpose(1, 0, 2).reshape(KV_LORA, -1),
                             g_ukv[:, :, MLA_NOPE:].transpose(1, 0, 2).reshape(KV_LORA, -1)], axis=1)
    cos, sin_a, sin_b = _rope_tables(positions)

    u, cq, ckv, cqn, ckvn, qn, qr, kv, kr, qkv_sb = _proj_in_fwd(x, norm_mix, w_a, q_norm, w_uq, kv_norm, w_ukv, cos, sin_a, sin_b)
    o_mla, lse, (g_o, g_gate, g_up, g_down) = _mla_fwd(qn, qr, kv, kr, late_shards)
    w_o = g_o.reshape(-1, d)
    w_gate, w_up = _stack_cols(g_gate), _stack_cols(g_up)
    w_down = g_down.reshape(-1, d)
    o_sb, tot, swept = _sb_fwd(qkv_sb)
    merged, h1, f = _attn_out_fwd(o_mla, o_sb, out_mla, out_sb, w_o, x, norm_ffn)
    gate, up, h2 = _ffn_fwd(f, h1, w_gate, w_up, w_down)
    loss, dh2, dh2b, dg_final = _final_loss(h2, loss_target, norm_final.reshape(1, d))

    dgate, dup, act, dh1, dh1b, dg_ffn = _ffn_bwd(dh2, dh2b, gate, up, h1, norm_ffn, w_down, w_gate, w_up)
    dw_down = _tn_matmul(act, dh2b, "dw_down")
    dw_gate = _tn_matmul(f, dgate, "dw_gate")
    dw_up = _tn_matmul(f, dup, "dw_up")
    do_mla, do_sb, dg_mla, dg_sb = _attn_out_bwd(dh1b, w_o, o_mla, o_sb, out_mla, out_sb)
    dw_o = _tn_matmul(merged, dh1b, "dw_o")
    dq_sb, dk_sb, dv_sb = _sb_bwd(qkv_sb, do_sb, tot, swept)
    early = [dw_o.reshape(N_DEV, -1, d), _split_cols(dw_gate), _split_cols(dw_up), dw_down.reshape(N_DEV, -1, d)]
    (dqn, dqr, dkn, dv, dkr), landed = _mla_bwd(qn, qr, kv, kr, do_mla, o_mla, lse, early)
    dx, dproj, dq, dkv, dg_q, dg_kv, dg_mix = _proj_in_bwd(
        dqn, dqr, dkn, dv, dkr, dq_sb, dk_sb, dv_sb, cq, ckv, x, dh1, cos, sin_a, sin_b,
        q_norm, kv_norm, norm_mix, w_uq, w_ukv, w_a)
    dw_a = _tn_matmul(u, dproj, "dw_in")
    dw_uq = _tn_matmul(cqn, dq, "dw_uq")
    dw_ukv = _tn_matmul(ckvn, dkv, "dw_ukv")

    p_in = _split_cols(jnp.concatenate([dw_a[:, :416], dw_a[:, 512:]], axis=1))
    p_uq = jnp.concatenate([dw_uq[:, :512].reshape(Q_LORA, MLA_HEADS, MLA_NOPE),
                            dw_uq[:, 512:].reshape(Q_LORA, MLA_HEADS, MLA_ROPE)], axis=2).transpose(1, 0, 2)
    p_ukv = jnp.concatenate([dw_ukv[:, :512].reshape(KV_LORA, MLA_HEADS, MLA_NOPE),
                             dw_ukv[:, 512:].reshape(KV_LORA, MLA_HEADS, HEAD_DIM)], axis=2).transpose(1, 0, 2)
    late = [p_in, p_uq, p_ukv]
    gain_grads = [dg_mix, dg_q, dg_kv, dg_mla, dg_sb, dg_ffn, dg_final]
    return loss, dx, list(landed), late, gain_grads


def kernel(x, positions, norm_mix, w_in, q_latent_norm, w_uq, kv_latent_norm, w_ukv, out_norm_mla, out_norm_sb, w_o, norm_ffn, w_gate, w_up, w_down, norm_final, loss_target, m_norm_mix, m_w_in, m_q_latent_norm, m_w_uq, m_kv_latent_norm, m_w_ukv, m_out_norm_mla, m_out_norm_sb, m_w_o, m_norm_ffn, m_w_gate, m_w_up, m_w_down, m_norm_final, v_norm_mix, v_w_in, v_q_latent_norm, v_w_uq, v_kv_latent_norm, v_w_ukv, v_out_norm_mla, v_out_norm_sb, v_w_o, v_norm_ffn, v_w_gate, v_w_up, v_w_down, v_norm_final):
    mats = [w_in, w_uq, w_ukv, w_o, w_gate, w_up, w_down]
    mat_m = [m_w_in, m_w_uq, m_w_ukv, m_w_o, m_w_gate, m_w_up, m_w_down]
    mat_v = [v_w_in, v_w_uq, v_w_ukv, v_w_o, v_w_gate, v_w_up, v_w_down]
    mat_names = ["w_in", "w_uq", "w_ukv", "w_o", "w_gate", "w_up", "w_down"]
    gains = [norm_mix, q_latent_norm, kv_latent_norm, out_norm_mla, out_norm_sb, norm_ffn, norm_final]
    gain_m = [m_norm_mix, m_q_latent_norm, m_kv_latent_norm, m_out_norm_mla, m_out_norm_sb, m_norm_ffn, m_norm_final]
    gain_v = [v_norm_mix, v_q_latent_norm, v_kv_latent_norm, v_out_norm_mla, v_out_norm_sb, v_norm_ffn, v_norm_final]

    shards = [w[0].astype(BF16) for w in mats]
    g_in, g_uq, g_ukv = _all_gather(shards[:3], "weight_all_gather")

    gains2d = [g.reshape(1, -1) for g in gains]
    loss_part, dx, landed, late, gain_grads = _local_step(
        x[0], positions[0], loss_target[0], gains2d, g_in, g_uq, g_ukv, shards[3:])

    sizes = [g.size for g in gains]
    used = sum(sizes) + LANES
    rows = -(-used // (8 * LANES)) * 8

    def pack(vals, tail):
        flat = jnp.concatenate([v.reshape(-1) for v in vals] + [tail])
        return jnp.pad(flat, (0, rows * LANES - flat.size)).reshape(rows, LANES)

    (small,), scattered = _exchange([pack(gain_grads, loss_part.reshape(-1))], late, "grad_exchange")

    mat_out = [_adamw(sl, w[0], m[0], v[0], "adamw_" + nm)
               for sl, w, m, v, nm in zip(list(scattered) + landed, mats, mat_m, mat_v, mat_names)]
    zeros_tail = jnp.zeros((LANES,), F32)
    g_s, d_s, m_s, v_s = _adamw(small, pack(gains, zeros_tail), pack(gain_m, zeros_tail), pack(gain_v, zeros_tail), "adamw_gains")

    def unpack(packed):
        flat = packed.reshape(-1)
        outs, off = [], 0
        for g, n in zip(gains, sizes):
            outs.append(flat[off:off + n].reshape(g.shape))
            off += n
        return outs

    loss = g_s.reshape(-1)[sum(sizes)]

    order = ["norm_mix", "w_in", "q_latent_norm", "w_uq", "kv_latent_norm", "w_ukv", "out_norm_mla", "out_norm_sb",
             "w_o", "norm_ffn", "w_gate", "w_up", "w_down", "norm_final"]
    gain_names = ["norm_mix", "q_latent_norm", "kv_latent_norm", "out_norm_mla", "out_norm_sb", "norm_ffn", "norm_final"]
    result = [loss, dx[None]]
    for kind in range(4):
        small_parts = dict(zip(gain_names, unpack([g_s, d_s, m_s, v_s][kind])))
        mat_parts = {nm: out[kind][None] for nm, out in zip(mat_names, mat_out)}
        result += [small_parts[nm] if nm in small_parts else mat_parts[nm] for nm in order]
    return tuple(result)
```

```python
import functools
import math

import jax
import jax.numpy as jnp
from jax import lax
from jax.experimental import pallas as pl
from jax.experimental.pallas import tpu as pltpu

F32 = jnp.float32
BF16 = jnp.bfloat16
MESH = pl.DeviceIdType.MESH

EPS = 1e-6
ROPE_THETA = 10000.0
MLA_HEADS = 8
MLA_NOPE = 64
MLA_ROPE = 32
SB_HEADS = 8
HEAD_DIM = 64
Q_LORA = 256
KV_LORA = 128
MLA_SCALE = 1.0 / math.sqrt(MLA_NOPE + MLA_ROPE)
SB_SCALE = 1.0 / math.sqrt(HEAD_DIM)
LOG2E = math.log2(math.e)
SB_DEAD = -160.0
N_DEV = 8

ADAM_LR = 0.001
ADAM_B1 = 0.9
ADAM_B2 = 0.999
ADAM_EPS = 1e-08
ADAM_WD = 0.01
ADAM_STEP = 10

LANES = 128
ATT_TILE = 512
SB_TILE = 512
TRI = 256
ROW_TILE = 512
FFN_BWD_ROW_TILE = 256
PROJ_BWD_ROW_TILE = 256
TN_BLOCK = 256
TN_RESIDENT_BYTES = 16 * 1024 * 1024
VMEM_LIMIT = 56 * 1024 * 1024
NEG = -1e30


def _cparams(*sem):
    return pltpu.CompilerParams(dimension_semantics=sem, vmem_limit_bytes=VMEM_LIMIT)


def _dot(a, b):
    return jnp.dot(a, b, preferred_element_type=F32)


def _dot_nt(a, b):
    return lax.dot_general(a, b, (((1,), (1,)), ((), ())), preferred_element_type=F32)


def _dot_tn(a, b):
    return lax.dot_general(a, b, (((0,), (0,)), ((), ())), preferred_element_type=F32)


def _rms(x, g):
    r = lax.rsqrt(jnp.mean(x * x, axis=-1, keepdims=True) + EPS)
    return x * r * g


def _rms_bwd(x, g, dy):
    r = lax.rsqrt(jnp.mean(x * x, axis=-1, keepdims=True) + EPS)
    n = x * r
    dn = dy * g
    dx = r * (dn - n * jnp.mean(dn * n, axis=-1, keepdims=True))
    return dx, jnp.sum(dy * n, axis=0, keepdims=True)


def _rope(x, cos, sin_a, sin_b):
    return x * cos + pltpu.roll(x, 112, 1) * sin_a + pltpu.roll(x, 16, 1) * sin_b


def _rope_t(g, cos, sin_a, sin_b):
    return g * cos + pltpu.roll(g * sin_a, 16, 1) + pltpu.roll(g * sin_b, 112, 1)


def _row_spec(tm, width):
    return pl.BlockSpec((tm, width), lambda r: (r, 0))


def _full_spec(shape):
    return pl.BlockSpec(shape, lambda *_: (0,) * len(shape))


def _accumulate(ref, val, first):
    @pl.when(first)
    def _():
        ref[...] = val

    @pl.when(jnp.logical_not(first))
    def _():
        ref[...] += val


def _proj_in_fwd(x, g_mix, w_a, g_q, w_uq, g_kv, w_ukv, cos, sin_a, sin_b):
    s, d = x.shape
    tm = min(ROW_TILE, s)

    def body(x_ref, gm_ref, wa_ref, gq_ref, wuq_ref, gkv_ref, wukv_ref, cos_ref, sa_ref, sb_ref,
             u_ref, cq_ref, ckv_ref, cqn_ref, ckvn_ref, qn_ref, qr_ref, kv_ref, kr_ref, sbq_ref):
        u = _rms(x_ref[...], gm_ref[...]).astype(BF16)
        u_ref[...] = u
        cq = _dot(u, wa_ref[:, 0:256])
        ckv = _dot(u, wa_ref[:, 256:384])
        kr = _dot(u, wa_ref[:, 384:512])
        cq_ref[...] = cq
        ckv_ref[...] = ckv
        cqn = _rms(cq, gq_ref[...]).astype(BF16)
        ckvn = _rms(ckv, gkv_ref[...]).astype(BF16)
        cqn_ref[...] = cqn
        ckvn_ref[...] = ckvn
        cos_t, sa_t, sb_t = cos_ref[...], sa_ref[...], sb_ref[...]
        qn_ref[...] = (_dot(cqn, wuq_ref[:, 0:512]) * MLA_SCALE).astype(BF16)
        for half in range(2):
            lo = 512 + half * LANES
            qr = _dot(cqn, wuq_ref[:, lo:lo + LANES])
            qr_ref[:, half * LANES:(half + 1) * LANES] = (_rope(qr, cos_t, sa_t, sb_t) * MLA_SCALE).astype(BF16)
        kv_ref[...] = _dot(ckvn, wukv_ref[...]).astype(BF16)
        krt = kr + pltpu.roll(kr, 32, 1) + pltpu.roll(kr, 64, 1) + pltpu.roll(kr, 96, 1)
        kr_ref[...] = _rope(krt, cos_t, sa_t, sb_t).astype(BF16)
        sbq_ref[:, 0:512] = (_dot(u, wa_ref[:, 512:1024]) * (SB_SCALE * LOG2E)).astype(BF16)
        sbq_ref[:, 512:1536] = _dot(u, wa_ref[:, 1024:2048]).astype(BF16)

    outs = [
        jax.ShapeDtypeStruct((s, d), BF16),
        jax.ShapeDtypeStruct((s, 256), F32),
        jax.ShapeDtypeStruct((s, 128), F32),
        jax.ShapeDtypeStruct((s, 256), BF16),
        jax.ShapeDtypeStruct((s, 128), BF16),
        jax.ShapeDtypeStruct((s, 512), BF16),
        jax.ShapeDtypeStruct((s, 256), BF16),
        jax.ShapeDtypeStruct((s, 1024), BF16),
        jax.ShapeDtypeStruct((s, 128), BF16),
        jax.ShapeDtypeStruct((s, 1536), BF16),
    ]
    return pl.pallas_call(
        body, name="proj_in_fwd", grid=(s // tm,), out_shape=outs,
        in_specs=[_row_spec(tm, d), _full_spec(g_mix.shape), _full_spec(w_a.shape), _full_spec(g_q.shape),
                  _full_spec(w_uq.shape), _full_spec(g_kv.shape), _full_spec(w_ukv.shape),
                  _row_spec(tm, LANES), _row_spec(tm, LANES), _row_spec(tm, LANES)],
        out_specs=[_row_spec(tm, o.shape[1]) for o in outs],
        compiler_params=_cparams("arbitrary"),
    )(x, g_mix, w_a, g_q, w_uq, g_kv, w_ukv, cos, sin_a, sin_b)


def _attn_out_fwd(o_mla, o_sb, g_mla, g_sb, w_o, x, g_ffn):
    s, d = x.shape
    tm = min(ROW_TILE, s)

    def body(oa_ref, ob_ref, ga_ref, gb_ref, wo_ref, x_ref, gf_ref, merged_ref, h1_ref, f_ref):
        na = _rms(oa_ref[...], ga_ref[...]).astype(BF16)
        nb = _rms(ob_ref[...], gb_ref[...]).astype(BF16)
        merged_ref[:, 0:512] = na
        merged_ref[:, 512:1024] = nb
        h1 = x_ref[...] + _dot(na, wo_ref[0:512, :]) + _dot(nb, wo_ref[512:1024, :])
        h1_ref[...] = h1
        f_ref[...] = _rms(h1, gf_ref[...]).astype(BF16)

    outs = [jax.ShapeDtypeStruct((s, d), BF16), jax.ShapeDtypeStruct((s, d), F32), jax.ShapeDtypeStruct((s, d), BF16)]
    return pl.pallas_call(
        body, name="attn_out_fwd", grid=(s // tm,), out_shape=outs,
        in_specs=[_row_spec(tm, 512), _row_spec(tm, 512), _full_spec(g_mla.shape), _full_spec(g_sb.shape),
                  _full_spec(w_o.shape), _row_spec(tm, d), _full_spec(g_ffn.shape)],
        out_specs=[_row_spec(tm, d)] * 3,
        compiler_params=_cparams("arbitrary"),
    )(o_mla, o_sb, g_mla, g_sb, w_o, x, g_ffn)


def _ffn_tile(d_ff):
    return d_ff // 2 if (d_ff // 2) % LANES == 0 else d_ff


def _ffn_fwd(f, h1, w_gate, w_up, w_down):
    s, d = h1.shape
    d_ff = w_gate.shape[1]
    tm = min(ROW_TILE, s)
    tf = _ffn_tile(d_ff)

    def body(f_ref, h1_ref, wg_ref, wu_ref, wd_ref, gate_ref, up_ref, h2_ref):
        j = pl.program_id(1)
        fb = f_ref[...]
        gate = _dot(fb, wg_ref[...])
        up = _dot(fb, wu_ref[...])
        gate_ref[...] = gate.astype(BF16)
        up_ref[...] = up.astype(BF16)
        act = (gate * jax.nn.sigmoid(gate) * up).astype(BF16)
        part = _dot(act, wd_ref[...])

        @pl.when(j == 0)
        def _():
            h2_ref[...] = h1_ref[...] + part

        @pl.when(j != 0)
        def _():
            h2_ref[...] += part

    outs = [jax.ShapeDtypeStruct((s, d_ff), BF16), jax.ShapeDtypeStruct((s, d_ff), BF16), jax.ShapeDtypeStruct((s, d), F32)]
    return pl.pallas_call(
        body, name="ffn_fwd", grid=(s // tm, d_ff // tf), out_shape=outs,
        in_specs=[pl.BlockSpec((tm, d), lambda r, j: (r, 0)), pl.BlockSpec((tm, d), lambda r, j: (r, 0)),
                  pl.BlockSpec((d, tf), lambda r, j: (0, j)), pl.BlockSpec((d, tf), lambda r, j: (0, j)),
                  pl.BlockSpec((tf, d), lambda r, j: (j, 0))],
        out_specs=[pl.BlockSpec((tm, tf), lambda r, j: (r, j)), pl.BlockSpec((tm, tf), lambda r, j: (r, j)),
                   pl.BlockSpec((tm, d), lambda r, j: (r, 0))],
        compiler_params=_cparams("arbitrary", "arbitrary"),
    )(f, h1, w_gate, w_up, w_down)


def _final_loss(h2, target, g_final):
    s, d = h2.shape
    tm = min(ROW_TILE, s)

    def body(h2_ref, t_ref, g_ref, loss_ref, dh2_ref, dh2b_ref, dg_ref):
        first = pl.program_id(0) == 0
        h2v = h2_ref[...]
        g = g_ref[...]
        diff = _rms(h2v, g) - t_ref[...]
        part = 0.5 * jnp.sum(jnp.mean(diff * diff, axis=-1, keepdims=True), axis=0, keepdims=True)
        _accumulate(loss_ref, jnp.broadcast_to(part, loss_ref.shape), first)
        dx, dg = _rms_bwd(h2v, g, diff * (1.0 / d))
        dh2_ref[...] = dx
        dh2b_ref[...] = dx.astype(BF16)
        _accumulate(dg_ref, dg, first)

    outs = [jax.ShapeDtypeStruct((1, LANES), F32), jax.ShapeDtypeStruct((s, d), F32), jax.ShapeDtypeStruct((s, d), BF16),
            jax.ShapeDtypeStruct((1, d), F32)]
    return pl.pallas_call(
        body, name="final_loss", grid=(s // tm,), out_shape=outs,
        in_specs=[_row_spec(tm, d), _row_spec(tm, d), _full_spec((1, d))],
        out_specs=[_full_spec((1, LANES)), _row_spec(tm, d), _row_spec(tm, d), _full_spec((1, d))],
        compiler_params=_cparams("arbitrary"),
    )(h2, target, g_final)


def _tile_iotas(t):
    return lax.broadcasted_iota(jnp.int32, (t, t), 0), lax.broadcasted_iota(jnp.int32, (t, t), 1)


def _stacked_mask(t, strict):
    row = lax.broadcasted_iota(jnp.int32, (2 * t, t), 0)
    col = lax.broadcasted_iota(jnp.int32, (2 * t, t), 1)
    row = jnp.where(row >= t, row - t, row)
    return col < row if strict else col <= row


def _mla_fwd(qn, qr, kv, kr, shards):
    s = qn.shape[0]
    t = min(ATT_TILE, s)
    pairs = MLA_HEADS // 2
    nq = s // t
    n = len(shards)

    def body(*refs):
        qn_ref, qr_ref, kn_ref, v_ref, kr_ref = refs[:5]
        o_ref, lse_ref = refs[5 + n:7 + n]
        qcat_ref, m_ref, l_ref, acc_ref = refs[7 + 2 * n:11 + 2 * n]
        hp, i = pl.program_id(0), pl.program_id(1)
        ride = _Exchange(True, refs[5:5 + n], refs[7 + n:7 + 2 * n], *refs[11 + 2 * n:])

        @pl.when((hp == 0) & (i == 0))
        def _():
            ride.start()

        lane = lax.broadcasted_iota(jnp.int32, (1, LANES), 1)
        row, col = _tile_iotas(t)
        causal = col <= row
        q_pair, q_quad = qn_ref[...], qr_ref[...]
        zero = jnp.zeros_like(q_pair)
        for hh in range(2):
            in_head = (lane // HEAD_DIM) == hh
            in_rope = (lane // MLA_ROPE) == (hp % 2) * 2 + hh
            qcat_ref[hh * t:(hh + 1) * t, 0:LANES] = jnp.where(in_head, q_pair, zero)
            qcat_ref[hh * t:(hh + 1) * t, LANES:2 * LANES] = jnp.where(in_rope, q_quad, zero)
        m_ref[...] = jnp.full_like(m_ref, NEG)
        l_ref[...] = jnp.zeros_like(l_ref)
        acc_ref[...] = jnp.zeros_like(acc_ref)

        def tile(j, width, masked):
            rows = pl.ds(pl.multiple_of(j * t, t), width * t)
            kcat = jnp.concatenate([kn_ref[rows, :], kr_ref[rows, :]], axis=1)
            v_ones = jnp.concatenate([v_ref[rows, :], jnp.ones((width * t, LANES), BF16)], axis=1)
            scores = [_dot_nt(qcat_ref[hh * t:(hh + 1) * t, :], kcat) for hh in range(2)]
            for hh in range(2):
                half = slice(hh * t, (hh + 1) * t)
                sc = jnp.where(causal, scores[hh], NEG) if masked else scores[hh]
                m = m_ref[half, :]
                m_new = jnp.maximum(m, jnp.max(sc, axis=-1, keepdims=True))
                alpha = jnp.exp(m - m_new)
                p = jnp.exp(sc - jnp.concatenate([m_new] * (width * t // LANES), axis=1))
                pv = _dot(p.astype(BF16), v_ones)
                l_ref[half, :] = alpha * l_ref[half, :] + pv[:, LANES:]
                acc_ref[half, :] = alpha * acc_ref[half, :] + pv[:, :LANES]
                m_ref[half, :] = m_new

        tile(i, 1, True)

        def step(n, carry):
            tile(4 * n, 4, False)
            return carry

        lax.fori_loop(0, i // 4, step, 0)

        @pl.when(i % 4 >= 2)
        def _():
            tile((i // 4) * 4, 2, False)

        @pl.when(i % 2 == 1)
        def _():
            tile(i - 1, 1, False)

        first = (lane // HEAD_DIM) == 0
        o = acc_ref[...] / l_ref[...]
        lse = m_ref[...] + jnp.log(l_ref[...])
        o_ref[...] = jnp.where(first, o[0:t], o[t:2 * t])
        lse_ref[...] = jnp.where(first, lse[0:t], lse[t:2 * t])

        @pl.when((hp == pairs - 1) & (i == nq - 1))
        def _():
            ride.finish()

    gathered_shapes, sems = _exchange_shapes(True, shards)
    outs = [jax.ShapeDtypeStruct((s, 512), F32), jax.ShapeDtypeStruct((pairs, s, LANES), F32)] + gathered_shapes
    res = pl.pallas_call(
        body, name="mla_fwd", grid=(pairs, nq), out_shape=outs,
        in_specs=[pl.BlockSpec((t, LANES), lambda hp, i: (i, hp)), pl.BlockSpec((t, LANES), lambda hp, i: (i, hp // 2)),
                  pl.BlockSpec((s, LANES), lambda hp, i: (0, hp)), pl.BlockSpec((s, LANES), lambda hp, i: (0, 4 + hp)),
                  pl.BlockSpec((s, LANES), lambda hp, i: (0, 0))] + [ANY] * n,
        out_specs=[pl.BlockSpec((t, LANES), lambda hp, i: (i, hp)), pl.BlockSpec((None, t, LANES), lambda hp, i: (hp, i, 0))]
        + [ANY] * n,
        scratch_shapes=[pltpu.VMEM((2 * t, 2 * LANES), BF16), pltpu.VMEM((2 * t, LANES), F32), pltpu.VMEM((2 * t, LANES), F32),
                        pltpu.VMEM((2 * t, LANES), F32)] + sems,
        compiler_params=_cparams("arbitrary", "arbitrary"),
    )(qn, qr, kv, kv, kr, *shards)
    return res[0], res[1], res[2:]


HEADS = (0, 1)


def _sb_logs(z2, strict, masked):
    log_b = jnp.minimum(z2, 0.0) - jnp.log2(1.0 + jnp.exp2(-jnp.abs(z2)))
    log_1m = log_b - z2
    if masked:
        log_1m = jnp.where(strict, log_1m, 0.0)
    return log_1m, log_b


def _block_totals(x):
    t, w = x.shape
    nb = max(w // TRI, 1)
    bw = w // nb
    blocks = [x[:, b * bw:(b + 1) * bw] for b in range(nb)]
    totals = [jnp.broadcast_to(jnp.sum(blk, axis=-1, keepdims=True), (t, LANES)) for blk in blocks]
    whole = totals[0]
    for tot in totals[1:]:
        whole = whole + tot
    return blocks, totals, whole


def _running_sums(blocks, totals, tri, carry, suffix):
    nb = len(blocks)
    reps = blocks[0].shape[1] // LANES
    outs = [None] * nb
    run = carry
    for b in (range(nb - 1, -1, -1) if suffix else range(nb)):
        outs[b] = _dot(blocks[b].astype(BF16), tri) + jnp.concatenate([run] * reps, axis=1)
        run = run + totals[b]
    return outs[0] if nb == 1 else jnp.concatenate(outs, axis=1)


def _tri(t, rel):
    n = min(TRI, t)
    row, col = _tile_iotas(n)
    return rel(row, col).astype(BF16)


def _sweep_width(t):
    return t // 2 if t // 2 >= TRI else t


def _sb_fwd(qkv):
    s = qkv.shape[0]
    t = min(SB_TILE, s)
    sw = _sweep_width(t)
    pairs = SB_HEADS // 2

    def body(q_ref, k_ref, v_ref, o_ref, tot_ref, cnt_ref, qm_ref, right_ref, acc_ref):
        i = pl.program_id(1)
        lane = lax.broadcasted_iota(jnp.int32, (1, LANES), 1)
        row, col = _tile_iotas(t)
        strict = col < row
        t_suffix = _tri(t, lambda r, c: r > c)
        q_pair = q_ref[...]
        for hh in range(2):
            qm_ref[hh] = jnp.where((lane // HEAD_DIM) == hh, q_pair, jnp.zeros_like(q_pair))
        right_ref[...] = jnp.zeros_like(right_ref)
        acc_ref[...] = jnp.zeros_like(acc_ref)

        def tile(start, width, masked):
            rows = pl.ds(pl.multiple_of(start, width), width)
            k, v = k_ref[rows, :], v_ref[rows, :]
            for hh in HEADS:
                log_1m, log_b = _sb_logs(_dot_nt(qm_ref[hh], k), strict, masked)
                blocks, totals, whole = _block_totals(log_1m)
                a = jnp.exp2(log_b + _running_sums(blocks, totals, t_suffix, right_ref[hh], True))
                if masked:
                    a = jnp.where(strict, a, 0.0)
                right_ref[hh] += whole
                acc_ref[hh] += _dot(a.astype(BF16), v)

        tile(i * t, t, True)

        def alive(n):
            return (n < i * (t // sw)) & (jnp.max(right_ref[...]) > SB_DEAD)

        def step(n):
            tile((i * (t // sw) - 1 - n) * sw, sw, False)
            return n + 1

        swept = lax.while_loop(alive, step, jnp.int32(0))
        cnt_ref[...] = jnp.full(cnt_ref.shape, swept.astype(F32))
        first = (lane // HEAD_DIM) == 0
        o_ref[...] = jnp.where(first, acc_ref[0], acc_ref[1])
        tot_ref[...] = jnp.where(first, right_ref[0], right_ref[1])

    outs = [jax.ShapeDtypeStruct((s, 512), F32), jax.ShapeDtypeStruct((pairs, s, LANES), F32),
            jax.ShapeDtypeStruct((pairs, s // t, 8, LANES), F32)]
    return pl.pallas_call(
        body, name="sb_fwd", grid=(pairs, s // t), out_shape=outs,
        in_specs=[pl.BlockSpec((t, LANES), lambda hp, i: (i, hp)), pl.BlockSpec((s, LANES), lambda hp, i: (0, 4 + hp)),
                  pl.BlockSpec((s, LANES), lambda hp, i: (0, 8 + hp))],
        out_specs=[pl.BlockSpec((t, LANES), lambda hp, i: (i, hp)), pl.BlockSpec((None, t, LANES), lambda hp, i: (hp, i, 0)),
                   pl.BlockSpec((None, None, 8, LANES), lambda hp, i: (hp, i, 0, 0))],
        scratch_shapes=[pltpu.VMEM((2, t, LANES), BF16), pltpu.VMEM((2, t, LANES), F32), pltpu.VMEM((2, t, LANES), F32)],
        compiler_params=_cparams("arbitrary", "arbitrary"),
    )(qkv, qkv, qkv)


def _sb_bwd(qkv, do, tot, cnt):
    s = qkv.shape[0]
    t = min(SB_TILE, s)
    sw = _sweep_width(t)
    pairs = SB_HEADS // 2

    def body(q_ref, k_ref, v_ref, do_ref, tot_ref, cnt_ref, dq_ref, dk_ref, dv_ref,
             qm_ref, dob_ref, total_s, left_l, left_g, dq_s):
        i = pl.program_id(1)

        @pl.when(i == 0)
        def _():
            dk_ref[...] = jnp.zeros_like(dk_ref)
            dv_ref[...] = jnp.zeros_like(dv_ref)

        lane = lax.broadcasted_iota(jnp.int32, (1, LANES), 1)
        row, col = _tile_iotas(t)
        strict = col < row
        t_suffix = _tri(t, lambda r, c: r > c)
        t_excl = _tri(t, lambda r, c: r < c)
        q_pair, do_pair, tot_pair = q_ref[...], do_ref[...], tot_ref[...]
        for hh in range(2):
            in_head = (lane // HEAD_DIM) == hh
            qm_ref[hh] = jnp.where(in_head, q_pair, jnp.zeros_like(q_pair))
            dob_ref[hh] = jnp.where(in_head, do_pair, 0.0).astype(BF16)
            total_s[hh] = jnp.broadcast_to(
                jnp.sum(jnp.where(lane == hh * HEAD_DIM, tot_pair, 0.0), axis=-1, keepdims=True), (t, LANES))
        left_l[...] = jnp.zeros_like(left_l)
        left_g[...] = jnp.zeros_like(left_g)
        dq_s[...] = jnp.zeros_like(dq_s)
        reps = t // LANES

        def tile(start, width, masked):
            rows = pl.ds(pl.multiple_of(start, width), width)
            k, v = k_ref[rows, :], v_ref[rows, :]
            z2 = [_dot_nt(qm_ref[hh], k) for hh in HEADS]
            d_a = [_dot_nt(dob_ref[hh], v) for hh in HEADS]
            for hh in HEADS:
                qm, dob = qm_ref[hh], dob_ref[hh]
                log_1m, log_b = _sb_logs(z2[hh], strict, masked)
                blocks, totals, whole = _block_totals(log_1m)
                done = left_l[hh] + whole
                left_l[hh] = done
                a = jnp.exp2(log_b + _running_sums(blocks, totals, t_suffix, total_s[hh] - done, True))
                if masked:
                    a = jnp.where(strict, a, 0.0)
                g = a * d_a[hh]
                blocks, totals, whole = _block_totals(g)
                before = _running_sums(blocks, totals, t_excl, left_g[hh], False)
                left_g[hh] += whole
                dz = g - jnp.exp2(log_b) * (g + before)
                if masked:
                    dz = jnp.where(strict, dz, 0.0)
                dzb = dz.astype(BF16)
                dq_s[hh] += _dot(dzb, k)
                dk_ref[rows, :] += _dot_tn(dzb, qm)
                dv_ref[rows, :] += _dot_tn(a.astype(BF16), dob)

        def step(h, carry):
            tile(h * sw, sw, False)
            return carry

        swept = jnp.max(cnt_ref[...]).astype(jnp.int32)
        lax.fori_loop(i * (t // sw) - swept, i * (t // sw), step, 0)
        tile(i * t, t, True)
        dq_ref[...] = jnp.where((lane // HEAD_DIM) == 0, dq_s[0], dq_s[1])

        @pl.when(i == s // t - 1)
        def _():
            dk_ref[...] *= 1.0 / LOG2E

    outs = [jax.ShapeDtypeStruct((s, 512), F32)] * 3
    return pl.pallas_call(
        body, name="sb_bwd", grid=(pairs, s // t), out_shape=outs,
        in_specs=[pl.BlockSpec((t, LANES), lambda hp, i: (i, hp)), pl.BlockSpec((s, LANES), lambda hp, i: (0, 4 + hp)),
                  pl.BlockSpec((s, LANES), lambda hp, i: (0, 8 + hp)), pl.BlockSpec((t, LANES), lambda hp, i: (i, hp)),
                  pl.BlockSpec((None, t, LANES), lambda hp, i: (hp, i, 0)),
                  pl.BlockSpec((None, None, 8, LANES), lambda hp, i: (hp, i, 0, 0))],
        out_specs=[pl.BlockSpec((t, LANES), lambda hp, i: (i, hp)), pl.BlockSpec((s, LANES), lambda hp, i: (0, hp)),
                   pl.BlockSpec((s, LANES), lambda hp, i: (0, hp))],
        scratch_shapes=[pltpu.VMEM((2, t, LANES), BF16), pltpu.VMEM((2, t, LANES), BF16)]
        + [pltpu.VMEM((2, t, LANES), F32)] * 4,
        compiler_params=_cparams("arbitrary", "arbitrary"),
    )(qkv, qkv, qkv, do, tot, cnt)


def _mla_bwd(qn, qr, kv, kr, do, o, lse, parts):
    s = qn.shape[0]
    t = min(ATT_TILE, s)
    pairs = MLA_HEADS // 2
    nq = s // t
    n = len(parts)

    def body(*refs):
        qn_ref, qr_ref, kn_ref, v_ref, kr_ref, do_ref, o_ref, lse_ref = refs[:8]
        dqn_ref, dqr_ref, dkn_ref, dv_ref, dkr_ref = refs[8 + n:13 + n]
        qcat_ref, dob_ref, lse_s, delta_s, dq_s = refs[13 + 2 * n:18 + 2 * n]
        hp, i = pl.program_id(0), pl.program_id(1)
        ride = _Exchange(False, refs[8:8 + n], refs[13 + n:13 + 2 * n], *refs[18 + 2 * n:])

        @pl.when((hp == 0) & (i == 0))
        def _():
            ride.start()

        @pl.when(i == 0)
        def _():
            dkn_ref[...] = jnp.zeros_like(dkn_ref)
            dv_ref[...] = jnp.zeros_like(dv_ref)
            dkr_ref[...] = jnp.zeros_like(dkr_ref)

        lane = lax.broadcasted_iota(jnp.int32, (1, LANES), 1)
        row, col = _tile_iotas(t)
        causal = col <= row
        q_pair, q_quad, do_pair, lse_pair = qn_ref[...], qr_ref[...], do_ref[...], lse_ref[...]
        do_o = do_pair * o_ref[...]
        zero = jnp.zeros_like(q_pair)
        ropes = []
        for hh in range(2):
            in_head = (lane // HEAD_DIM) == hh
            in_rope = (lane // MLA_ROPE) == (hp % 2) * 2 + hh
            ropes.append(in_rope)
            qcat_ref[hh, :, 0:LANES] = jnp.where(in_head, q_pair, zero)
            qcat_ref[hh, :, LANES:2 * LANES] = jnp.where(in_rope, q_quad, zero)
            dob_ref[hh] = jnp.where(in_head, do_pair, 0.0).astype(BF16)
            delta_s[hh] = jnp.broadcast_to(jnp.sum(jnp.where(in_head, do_o, 0.0), axis=-1, keepdims=True), (t, LANES))
            lse_s[hh] = jnp.broadcast_to(
                jnp.sum(jnp.where(lane == hh * HEAD_DIM, lse_pair, 0.0), axis=-1, keepdims=True), (t, LANES))
        dq_s[...] = jnp.zeros_like(dq_s)
        reps = t // LANES

        def tile(j, width, masked):
            rows = pl.ds(pl.multiple_of(j * t, t), width * t)
            kcat = jnp.concatenate([kn_ref[rows, :], kr_ref[rows, :]], axis=1)
            v = v_ref[rows, :]
            sc = [_dot_nt(qcat_ref[hh], kcat) for hh in HEADS]
            dp = [_dot_nt(dob_ref[hh], v) for hh in HEADS]
            p = [jnp.exp(sc[hh] - jnp.concatenate([lse_s[hh]] * (width * reps), axis=1)) for hh in HEADS]
            if masked:
                p = [jnp.where(causal, p[hh], 0.0) for hh in HEADS]
            ds = [(p[hh] * (dp[hh] - jnp.concatenate([delta_s[hh]] * (width * reps), axis=1))).astype(BF16) for hh in HEADS]
            for hh in HEADS:
                dq_s[hh] += _dot(ds[hh], kcat)
            dkcat = _dot_tn(ds[0], qcat_ref[0]) + _dot_tn(ds[1], qcat_ref[1])
            dkn_ref[rows, :] += dkcat[:, 0:LANES]
            dkr_ref[rows, :] += dkcat[:, LANES:2 * LANES]
            dv_ref[rows, :] += _dot_tn(p[0].astype(BF16), dob_ref[0]) + _dot_tn(p[1].astype(BF16), dob_ref[1])

        def step(n, carry):
            tile(4 * n, 4, False)
            return carry

        lax.fori_loop(0, i // 4, step, 0)

        @pl.when(i % 4 >= 2)
        def _():
            tile((i // 4) * 4, 2, False)

        @pl.when(i % 2 == 1)
        def _():
            tile(i - 1, 1, False)

        tile(i, 1, True)
        dqn_ref[...] =jnp.where((lane // HEAD_DIM) == 0, dq_s[0, :, 0:LANES], dq_s[1, :, 0:LANES])
        dqr_ref[...] = (jnp.where(ropes[0], dq_s[0, :, LANES:2 * LANES], 0.0)
                        + jnp.where(ropes[1], dq_s[1, :, LANES:2 * LANES], 0.0))

        @pl.when((hp == pairs - 1) & (i == nq - 1))
        def _():
            ride.finish()

    pair_block = pl.BlockSpec((t, LANES), lambda hp, i: (i, hp))
    once = pl.Buffered(1)
    landed_shapes, sems = _exchange_shapes(False, parts)
    outs = [jax.ShapeDtypeStruct((s, 512), F32), jax.ShapeDtypeStruct((pairs, s, LANES), F32),
            jax.ShapeDtypeStruct((s, 512), F32), jax.ShapeDtypeStruct((s, 512), F32),
            jax.ShapeDtypeStruct((pairs, s, LANES), F32)] + landed_shapes
    res = pl.pallas_call(
        body, name="mla_bwd", grid=(pairs, nq), out_shape=outs,
        in_specs=[pair_block, pl.BlockSpec((t, LANES), lambda hp, i: (i, hp // 2)),
                  pl.BlockSpec((s, LANES), lambda hp, i: (0, hp), pipeline_mode=once),
                  pl.BlockSpec((s, LANES), lambda hp, i: (0, 4 + hp), pipeline_mode=once),
                  pl.BlockSpec((s, LANES), lambda hp, i: (0, 0), pipeline_mode=once), pair_block, pair_block,
                  pl.BlockSpec((None, t, LANES), lambda hp, i: (hp, i, 0))] + [ANY] * n,
        out_specs=[pair_block, pl.BlockSpec((None, t, LANES), lambda hp, i: (hp, i, 0)),
                   pl.BlockSpec((s, LANES), lambda hp, i: (0, hp), pipeline_mode=once),
                   pl.BlockSpec((s, LANES), lambda hp, i: (0, hp), pipeline_mode=once),
                   pl.BlockSpec((None, s, LANES), lambda hp, i: (hp, 0, 0), pipeline_mode=once)] + [ANY] * n,
        scratch_shapes=[pltpu.VMEM((2, t, 2 * LANES), BF16), pltpu.VMEM((2, t, LANES), BF16), pltpu.VMEM((2, t, LANES), F32),
                        pltpu.VMEM((2, t, LANES), F32), pltpu.VMEM((2, t, 2 * LANES), F32)] + sems,
        compiler_params=_cparams("arbitrary", "arbitrary"),
    )(qn, qr, kv, kv, kr, do, o, lse, *parts)
    return res[:5], res[5:]


def _ffn_bwd(dh2, dh2b, gate, up, h1, g_ffn, w_down, w_gate, w_up):
    s, d = h1.shape
    d_ff = gate.shape[1]
    tm = min(FFN_BWD_ROW_TILE, s)
    tf = _ffn_tile(d_ff)

    def act_body(dh2b_ref, gate_ref, up_ref, wd_ref, dgate_ref, dup_ref, act_ref):
        dact = _dot_nt(dh2b_ref[...], wd_ref[...])
        gate_v = gate_ref[...].astype(F32)
        up_v = up_ref[...].astype(F32)
        sig = jax.nn.sigmoid(gate_v)
        silu = gate_v * sig
        dup_ref[...] = (dact * silu).astype(BF16)
        dgate_ref[...] = (dact * up_v * (sig * (1.0 + gate_v * (1.0 - sig)))).astype(BF16)
        act_ref[...] = (silu * up_v).astype(BF16)

    ff = pl.BlockSpec((tm, tf), lambda j, r: (r, j))
    dgate, dup, act = pl.pallas_call(
        act_body, name="ffn_bwd_act", grid=(d_ff // tf, s // tm), out_shape=[jax.ShapeDtypeStruct((s, d_ff), BF16)] * 3,
        in_specs=[pl.BlockSpec((tm, d), lambda j, r: (r, 0)), ff, ff, pl.BlockSpec((tf, d), lambda j, r: (j, 0))],
        out_specs=[ff, ff, ff],
        compiler_params=_cparams("arbitrary", "arbitrary"),
    )(dh2b, gate, up, w_down)

    def df_body(dgate_ref, dup_ref, dh2_ref, h1_ref, g_ref, wg_ref, wu_ref, dh1_ref, dh1b_ref, dg_ref):
        df = _dot_nt(dgate_ref[...], wg_ref[...]) + _dot_nt(dup_ref[...], wu_ref[...])
        dx, dg = _rms_bwd(h1_ref[...], g_ref[...], df)
        dh1 = dh2_ref[...] + dx
        dh1_ref[...] = dh1
        dh1b_ref[...] = dh1.astype(BF16)
        _accumulate(dg_ref, dg, pl.program_id(0) == 0)

    outs = [jax.ShapeDtypeStruct((s, d), F32), jax.ShapeDtypeStruct((s, d), BF16), jax.ShapeDtypeStruct((1, d), F32)]
    dh1, dh1b, dg = pl.pallas_call(
        df_body, name="ffn_bwd_df", grid=(s // tm,), out_shape=outs,
        in_specs=[_row_spec(tm, d_ff), _row_spec(tm, d_ff), _row_spec(tm, d), _row_spec(tm, d), _full_spec((1, d)),
                  _full_spec(w_gate.shape), _full_spec(w_up.shape)],
        out_specs=[_row_spec(tm, d), _row_spec(tm, d), _full_spec((1, d))],
        compiler_params=_cparams("arbitrary"),
    )(dgate, dup, dh2, h1, g_ffn, w_gate, w_up)
    return dgate, dup, act, dh1, dh1b, dg


def _largest_tile(n, cap):
    for cand in range(cap, 0, -LANES):
        if n % cand == 0:
            return cand
    return n


def _tn_matmul(a, b, name):
    assert a.dtype == BF16 and b.dtype == BF16
    s, m = a.shape
    n = b.shape[1]
    if s * m * 2 <= TN_RESIDENT_BYTES:
        tm, tn = m, min(n, TN_BLOCK)
    else:
        tm, tn = TN_BLOCK, n

    def body(a_ref, b_ref, o_ref):
        o_ref[...] = _dot_tn(a_ref[...], b_ref[...]).astype(BF16)

    return pl.pallas_call(
        body, name=name, grid=(m // tm, n // tn), out_shape=jax.ShapeDtypeStruct((m, n), BF16),
        in_specs=[pl.BlockSpec((s, tm), lambda i, j: (0, i)), pl.BlockSpec((s, tn), lambda i, j: (0, j))],
        out_specs=pl.BlockSpec((tm, tn), lambda i, j: (i, j)),
        compiler_params=_cparams("arbitrary", "arbitrary"),
    )(a, b)


def _attn_out_bwd(dh1, w_o, o_mla, o_sb, g_mla, g_sb):
    s, d = dh1.shape
    tm = min(ROW_TILE, s)

    def body(dh1_ref, wo_ref, oa_ref, ob_ref, ga_ref, gb_ref, doa_ref, dob_ref, dga_ref, dgb_ref):
        first = pl.program_id(0) == 0
        dh1b = dh1_ref[...]
        dxa, dga = _rms_bwd(oa_ref[...], ga_ref[...], _dot_nt(dh1b, wo_ref[0:512, :]))
        dxb, dgb = _rms_bwd(ob_ref[...], gb_ref[...], _dot_nt(dh1b, wo_ref[512:1024, :]))
        doa_ref[...] = dxa
        dob_ref[...] = dxb
        _accumulate(dga_ref, dga, first)
        _accumulate(dgb_ref, dgb, first)

    outs = [jax.ShapeDtypeStruct((s, 512), F32)] * 2 + [jax.ShapeDtypeStruct((1, 512), F32)] * 2
    return pl.pallas_call(
        body, name="attn_out_bwd", grid=(s // tm,), out_shape=outs,
        in_specs=[_row_spec(tm, d), _full_spec(w_o.shape), _row_spec(tm, 512), _row_spec(tm, 512),
                  _full_spec((1, 512)), _full_spec((1, 512))],
        out_specs=[_row_spec(tm, 512), _row_spec(tm, 512), _full_spec((1, 512)), _full_spec((1, 512))],
        compiler_params=_cparams("arbitrary"),
    )(dh1, w_o, o_mla, o_sb, g_mla, g_sb)


def _proj_in_bwd(dqn, dqr, dkn, dv, dkr, dq_sb, dk_sb, dv_sb, cq, ckv, x, dh1, cos, sin_a, sin_b,
                 g_q, g_kv, g_mix, w_uq, w_ukv, w_a):
    s, d = x.shape
    tm = min(PROJ_BWD_ROW_TILE, s)

    def body(dqn_ref, dqr_ref, dkn_ref, dv_ref, dkr_ref, dqs_ref, dks_ref, dvs_ref, cq_ref, ckv_ref, x_ref, dh1_ref,
             cos_ref, sa_ref, sb_ref, gq_ref, gkv_ref, gm_ref, wuq_ref, wukv_ref, wa_ref,
             dx_ref, dproj_ref, dq_ref, dkv_ref, dgq_ref, dgkv_ref, dgm_ref):
        first = pl.program_id(0) == 0
        lane = lax.broadcasted_iota(jnp.int32, (1, LANES), 1)
        cos_t, sa_t, sb_t = cos_ref[...], sa_ref[...], sb_ref[...]
        dq_ref[:, 0:512] = (dqn_ref[...] * MLA_SCALE).astype(BF16)
        for half in range(2):
            quad = (dqr_ref[2 * half] + dqr_ref[2 * half + 1]) * MLA_SCALE
            dq_ref[:, 512 + half * LANES:512 + (half + 1) * LANES] = _rope_t(quad, cos_t, sa_t, sb_t).astype(BF16)
        dcq, dgq = _rms_bwd(cq_ref[...], gq_ref[...], _dot_nt(dq_ref[...], wuq_ref[...]))
        _accumulate(dgq_ref, dgq, first)
        dkv_ref[:, 0:512] = dkn_ref[...].astype(BF16)
        dkv_ref[:, 512:1024] = dv_ref[...].astype(BF16)
        dckv, dgkv = _rms_bwd(ckv_ref[...], gkv_ref[...], _dot_nt(dkv_ref[...], wukv_ref[...]))
        _accumulate(dgkv_ref, dgkv, first)
        g = _rope_t(dkr_ref[0] + dkr_ref[1] + dkr_ref[2] + dkr_ref[3], cos_t, sa_t, sb_t)
        g = g + pltpu.roll(g, 96, 1) + pltpu.roll(g, 64, 1) + pltpu.roll(g, 32, 1)
        dproj_ref[:, 0:256] = dcq.astype(BF16)
        dproj_ref[:, 256:384] = dckv.astype(BF16)
        dproj_ref[:, 384:512] = jnp.where(lane < MLA_ROPE, g, 0.0).astype(BF16)
        dproj_ref[:, 512:1024] = (dqs_ref[...] * SB_SCALE).astype(BF16)
        dproj_ref[:, 1024:1536] = dks_ref[...].astype(BF16)
        dproj_ref[:, 1536:2048] = dvs_ref[...].astype(BF16)
        dxn, dgm = _rms_bwd(x_ref[...], gm_ref[...], _dot_nt(dproj_ref[...], wa_ref[...]))
        dx_ref[...] = dh1_ref[...] + dxn
        _accumulate(dgm_ref, dgm, first)

    quad_spec = pl.BlockSpec((4, tm, LANES), lambda r: (0, r, 0))
    outs = [jax.ShapeDtypeStruct((s, d), F32), jax.ShapeDtypeStruct((s, 2048), BF16), jax.ShapeDtypeStruct((s, 768), BF16),
            jax.ShapeDtypeStruct((s, 1024), BF16), jax.ShapeDtypeStruct((1, 256), F32), jax.ShapeDtypeStruct((1, 128), F32),
            jax.ShapeDtypeStruct((1, d), F32)]
    return pl.pallas_call(
        body, name="proj_in_bwd", grid=(s // tm,), out_shape=outs,
        in_specs=[_row_spec(tm, 512), quad_spec, _row_spec(tm, 512), _row_spec(tm, 512), quad_spec,
                  _row_spec(tm, 512), _row_spec(tm, 512), _row_spec(tm, 512), _row_spec(tm, 256), _row_spec(tm, 128),
                  _row_spec(tm, d), _row_spec(tm, d), _row_spec(tm, LANES), _row_spec(tm, LANES), _row_spec(tm, LANES),
                  _full_spec((1, 256)), _full_spec((1, 128)), _full_spec((1, d)),
                  _full_spec(w_uq.shape), _full_spec(w_ukv.shape), _full_spec(w_a.shape)],
        out_specs=[_row_spec(tm, d), _row_spec(tm, 2048), _row_spec(tm, 768), _row_spec(tm, 1024),
                   _full_spec((1, 256)), _full_spec((1, 128)), _full_spec((1, d))],
        compiler_params=_cparams("arbitrary"),
    )(dqn, dqr, dkn, dv, dkr, dq_sb, dk_sb, dv_sb, cq, ckv, x, dh1, cos, sin_a, sin_b, g_q, g_kv, g_mix,
      w_uq, w_ukv, w_a)


ANY = pl.BlockSpec(memory_space=pl.ANY)


def _place():
    return lax.axis_index("x"), lax.axis_index("y"), lax.axis_index("c")


def _all_gather(shards, name):
    n = len(shards)

    def body(*refs):
        ins, outs = refs[:n], refs[n:2 * n]
        send_sems, recv_sems, local_sems = refs[2 * n:]
        x, y, c = _place()
        me, sibling = (x, y, c), (x, y, 1 - c)
        chips = [(1 - x, y), (x, 1 - y), (1 - x, 1 - y)]

        def slot(a, px, py, pc):
            return outs[a].at[4 * px + 2 * py + pc]

        def copy(a, k, block, to, src=None):
            return pltpu.make_async_remote_copy(
                src_ref=slot(a, *block) if src is None else src, dst_ref=slot(a, *block),
                send_sem=send_sems.at[a, k], recv_sem=recv_sems.at[a, k], device_id=to, device_id_type=MESH)

        mine, first, passed = [], [], []
        for a in range(n):
            own = pltpu.make_async_copy(ins[a], slot(a, *me), local_sems.at[a])
            own.start()
            mine.append(own)
            cps = [copy(a, 0, me, sibling, src=ins[a])]
            cps += [copy(a, 1 + j, me, (*chip, c), src=ins[a]) for j, chip in enumerate(chips)]
            for cp in cps:
                cp.start()
            first += cps
        for a in range(n):
            for j, chip in enumerate(chips):
                copy(a, 1 + j, (*chip, c), me).wait_recv()
                fwd = copy(a, 4 + j, (*chip, c), sibling)
                fwd.start()
                passed.append(fwd)
        for a in range(n):
            copy(a, 0, sibling, me).wait_recv()
            for j, chip in enumerate(chips):
                copy(a, 4 + j, (*chip, 1 - c), me).wait_recv()
        for cp in first + passed:
            cp.wait_send()
        for own in mine:
            own.wait()

    return pl.pallas_call(
        body, name=name,
        out_shape=[jax.ShapeDtypeStruct((N_DEV,) + v.shape, v.dtype) for v in shards],
        in_specs=[ANY] * n, out_specs=[ANY] * n,
        scratch_shapes=[pltpu.SemaphoreType.DMA((n, 7)), pltpu.SemaphoreType.DMA((n, 7)), pltpu.SemaphoreType.DMA((n,))],
    )(*shards)


class _Exchange:
    def __init__(self, gather, ins, outs, send_sems, recv_sems, local_sems):
        self.gather, self.ins, self.outs = gather, ins, outs
        self.sems = (send_sems, recv_sems, local_sems)
        x, y, c = _place()
        self.me = 4 * x + 2 * y + c
        self.peers = []
        for k in range(1, N_DEV):
            px = 1 - x if k & 4 else x
            py = 1 - y if k & 2 else y
            pc = 1 - c if k & 1 else c
            self.peers.append(((px, py, pc), 4 * px + 2 * py + pc))

    def _remote(self, a, k, landing):
        send_sems, recv_sems, _ = self.sems
        where, number = self.peers[k]
        src = self.ins[a] if self.gather else self.ins[a].at[number]
        return pltpu.make_async_remote_copy(
            src_ref=src, dst_ref=self.outs[a].at[landing], send_sem=send_sems.at[a, k], recv_sem=recv_sems.at[a, k],
            device_id=where, device_id_type=MESH)

    def _local(self, a):
        src = self.ins[a] if self.gather else self.ins[a].at[self.me]
        return pltpu.make_async_copy(src, self.outs[a].at[self.me], self.sems[2].at[a])

    def start(self):
        for a in range(len(self.ins)):
            self._local(a).start()
            for k in range(N_DEV - 1):
                self._remote(a, k, self.me).start()

    def finish(self):
        for a in range(len(self.ins)):
            for k in range(N_DEV - 1):
                self._remote(a, k, self.peers[k][1]).wait_recv()
            for k in range(N_DEV - 1):
                self._remote(a, k, self.me).wait_send()
            self._local(a).wait()


def _exchange_shapes(gather, arrays):
    out_shape = [jax.ShapeDtypeStruct(((N_DEV,) + v.shape) if gather else v.shape, v.dtype) for v in arrays]
    n = len(arrays)
    sems = [pltpu.SemaphoreType.DMA((n, N_DEV - 1)), pltpu.SemaphoreType.DMA((n, N_DEV - 1)), pltpu.SemaphoreType.DMA((n,))]
    return out_shape, sems


def _exchange(gathers, scatters, name):
    ng, ns = len(gathers), len(scatters)
    n = ng + ns

    def body(*refs):
        ins, outs, sems = refs[:n], refs[n:2 * n], refs[2 * n:]
        both = [_Exchange(True, ins[:ng], outs[:ng], *sems[:3]), _Exchange(False, ins[ng:], outs[ng:], *sems[3:])]
        for ex in both:
            ex.start()
        for ex in both:
            ex.finish()

    g_shapes, g_sems = _exchange_shapes(True, gathers)
    s_shapes, s_sems = _exchange_shapes(False, scatters)
    res = pl.pallas_call(body, name=name, out_shape=g_shapes + s_shapes, in_specs=[ANY] * n, out_specs=[ANY] * n,
                         scratch_shapes=g_sems + s_sems)(*gathers, *scatters)
    return res[:ng], res[ng:]


def _grad_row_tile(rows):
    return _largest_tile_rows(rows, 256)


def _largest_tile_rows(rows, cap):
    for cand in range(cap, 0, -8):
        if rows % cand == 0:
            return cand
    return rows


def _adamw_math(w, g, m, v):
    m_new = ADAM_B1 * m + (1.0 - ADAM_B1) * g
    v_new = ADAM_B2 * v + (1.0 - ADAM_B2) * (g * g)
    m_hat = m_new / (1.0 - ADAM_B1 ** ADAM_STEP)
    v_hat = v_new / (1.0 - ADAM_B2 ** ADAM_STEP)
    delta = -ADAM_LR * (m_hat / (jnp.sqrt(v_hat) + ADAM_EPS) + ADAM_WD * w)
    return delta, m_new, v_new


def _adamw(slots, w, m, v, name):
    k, r, cdim = slots.shape
    tr = _grad_row_tile(r)

    def body(s_ref, w_ref, m_ref, v_ref, g_ref, d_ref, mo_ref, vo_ref):
        g = s_ref[0].astype(F32)
        for q in range(1, k):
            g = g + s_ref[q].astype(F32)
        g_ref[...] = g
        d_ref[...], mo_ref[...], vo_ref[...] = _adamw_math(w_ref[...], g, m_ref[...], v_ref[...])

    blk = pl.BlockSpec((tr, cdim), lambda i: (i, 0))
    return pl.pallas_call(
        body, name=name, grid=(r // tr,), out_shape=[jax.ShapeDtypeStruct((r, cdim), F32)] * 4,
        in_specs=[pl.BlockSpec((k, tr, cdim), lambda i: (0, i, 0)), blk, blk, blk], out_specs=[blk] * 4,
        compiler_params=_cparams("arbitrary"),
    )(slots, w, m, v)


def _stack_cols(g):
    n, r, c = g.shape
    return g.transpose(1, 0, 2).reshape(r, n * c)


def _split_cols(w):
    r, nc = w.shape
    return w.reshape(r, N_DEV, nc // N_DEV).transpose(1, 0, 2)


def _rope_tables(positions):
    inv_freq = ROPE_THETA ** (-jnp.arange(0, MLA_ROPE, 2, dtype=F32) / MLA_ROPE)
    ang = positions.astype(F32).reshape(-1, 1) * inv_freq[None, :]
    cos, sin, zero = jnp.cos(ang), jnp.sin(ang), jnp.zeros_like(ang)
    reps = LANES // MLA_ROPE
    return (jnp.tile(jnp.concatenate([cos, cos], axis=1), (1, reps)),
            jnp.tile(jnp.concatenate([-sin, zero], axis=1), (1, reps)),
            jnp.tile(jnp.concatenate([zero, sin], axis=1), (1, reps)))


def _local_step(x, positions, loss_target, gains, g_in, g_uq, g_ukv, late_shards):
    norm_mix, q_norm, kv_norm, out_mla, out_sb, norm_ffn, norm_final = gains
    d = x.shape[1]
    w_in = _stack_cols(g_in)
    w_a = jnp.concatenate([w_in[:, :416], jnp.zeros((d, 96), BF16), w_in[:, 416:]], axis=1)
    w_uq = jnp.concatenate([g_uq[:, :, :MLA_NOPE].transpose(1, 0, 2).reshape(Q_LORA, -1),
                            g_uq[:, :, MLA_NOPE:].transpose(1, 0, 2).reshape(Q_LORA, -1)], axis=1)
    w_ukv = jnp.concatenate([g_ukv[:, :, :MLA_NOPE].transpose(1, 0, 2).reshape(KV_LORA, -1),
                             g_ukv[:, :, MLA_NOPE:].transpose(1, 0, 2).reshape(KV_LORA, -1)], axis=1)
    cos, sin_a, sin_b = _rope_tables(positions)

    u, cq, ckv, cqn, ckvn, qn, qr, kv, kr, qkv_sb = _proj_in_fwd(x, norm_mix, w_a, q_norm, w_uq, kv_norm, w_ukv, cos, sin_a, sin_b)
    o_mla, lse, (g_o, g_gate, g_up, g_down) = _mla_fwd(qn, qr, kv, kr, late_shards)
    w_o = g_o.reshape(-1, d)
    w_gate, w_up = _stack_cols(g_gate), _stack_cols(g_up)
    w_down = g_down.reshape(-1, d)
    o_sb, tot, swept = _sb_fwd(qkv_sb)
    merged, h1, f = _attn_out_fwd(o_mla, o_sb, out_mla, out_sb, w_o, x, norm_ffn)
    gate, up, h2 = _ffn_fwd(f, h1, w_gate, w_up, w_down)
    loss, dh2, dh2b, dg_final = _final_loss(h2, loss_target, norm_final.reshape(1, d))

    dgate, dup, act, dh1, dh1b, dg_ffn = _ffn_bwd(dh2, dh2b, gate, up, h1, norm_ffn, w_down, w_gate, w_up)
    dw_down = _tn_matmul(act, dh2b, "dw_down")
    dw_gate = _tn_matmul(f, dgate, "dw_gate")
    dw_up = _tn_matmul(f, dup, "dw_up")
    do_mla, do_sb, dg_mla, dg_sb = _attn_out_bwd(dh1b, w_o, o_mla, o_sb, out_mla, out_sb)
    dw_o = _tn_matmul(merged, dh1b, "dw_o")
    dq_sb, dk_sb, dv_sb = _sb_bwd(qkv_sb, do_sb, tot, swept)
    early = [dw_o.reshape(N_DEV, -1, d), _split_cols(dw_gate), _split_cols(dw_up), dw_down.reshape(N_DEV, -1, d)]
    (dqn, dqr, dkn, dv, dkr), landed = _mla_bwd(qn, qr, kv, kr, do_mla, o_mla, lse, early)
    dx, dproj, dq, dkv, dg_q, dg_kv, dg_mix = _proj_in_bwd(
        dqn, dqr, dkn, dv, dkr, dq_sb, dk_sb, dv_sb, cq, ckv, x, dh1, cos, sin_a, sin_b,
        q_norm, kv_norm, norm_mix, w_uq, w_ukv, w_a)
    dw_a = _tn_matmul(u, dproj, "dw_in")
    dw_uq = _tn_matmul(cqn, dq, "dw_uq")
    dw_ukv = _tn_matmul(ckvn, dkv, "dw_ukv")

    p_in = _split_cols(jnp.concatenate([dw_a[:, :416], dw_a[:, 512:]], axis=1))
    p_uq = jnp.concatenate([dw_uq[:, :512].reshape(Q_LORA, MLA_HEADS, MLA_NOPE),
                            dw_uq[:, 512:].reshape(Q_LORA, MLA_HEADS, MLA_ROPE)], axis=2).transpose(1, 0, 2)
    p_ukv = jnp.concatenate([dw_ukv[:, :512].reshape(KV_LORA, MLA_HEADS, MLA_NOPE),
                             dw_ukv[:, 512:].reshape(KV_LORA, MLA_HEADS, HEAD_DIM)], axis=2).transpose(1, 0, 2)
    late = [p_in, p_uq, p_ukv]
    gain_grads = [dg_mix, dg_q, dg_kv, dg_mla, dg_sb, dg_ffn, dg_final]
    return loss, dx, list(landed), late, gain_grads


def kernel(x, positions, norm_mix, w_in, q_latent_norm, w_uq, kv_latent_norm, w_ukv, out_norm_mla, out_norm_sb, w_o, norm_ffn, w_gate, w_up, w_down, norm_final, loss_target, m_norm_mix, m_w_in, m_q_latent_norm, m_w_uq, m_kv_latent_norm, m_w_ukv, m_out_norm_mla, m_out_norm_sb, m_w_o, m_norm_ffn, m_w_gate, m_w_up, m_w_down, m_norm_final, v_norm_mix, v_w_in, v_q_latent_norm, v_w_uq, v_kv_latent_norm, v_w_ukv, v_out_norm_mla, v_out_norm_sb, v_w_o, v_norm_ffn, v_w_gate, v_w_up, v_w_down, v_norm_final):
    mats = [w_in, w_uq, w_ukv, w_o, w_gate, w_up, w_down]
    mat_m = [m_w_in, m_w_uq, m_w_ukv, m_w_o, m_w_gate, m_w_up, m_w_down]
    mat_v = [v_w_in, v_w_uq, v_w_ukv, v_w_o, v_w_gate, v_w_up, v_w_down]
    mat_names = ["w_in", "w_uq", "w_ukv", "w_o", "w_gate", "w_up", "w_down"]
    gains = [norm_mix, q_latent_norm, kv_latent_norm, out_norm_mla, out_norm_sb, norm_ffn, norm_final]
    gain_m = [m_norm_mix, m_q_latent_norm, m_kv_latent_norm, m_out_norm_mla, m_out_norm_sb, m_norm_ffn, m_norm_final]
    gain_v = [v_norm_mix, v_q_latent_norm, v_kv_latent_norm, v_out_norm_mla, v_out_norm_sb, v_norm_ffn, v_norm_final]

    shards = [w[0].astype(BF16) for w in mats]
    g_in, g_uq, g_ukv = _all_gather(shards[:3], "weight_all_gather")

    gains2d = [g.reshape(1, -1) for g in gains]
    loss_part, dx, landed, late, gain_grads = _local_step(
        x[0], positions[0], loss_target[0], gains2d, g_in, g_uq, g_ukv, shards[3:])

    sizes = [g.size for g in gains]
    used = sum(sizes) + LANES
    rows = -(-used // (8 * LANES)) * 8

    def pack(vals, tail):
        flat = jnp.concatenate([v.reshape(-1) for v in vals] + [tail])
        return jnp.pad(flat, (0, rows * LANES - flat.size)).reshape(rows, LANES)

    (small,), scattered = _exchange([pack(gain_grads, loss_part.reshape(-1))], late, "grad_exchange")

    mat_out = [_adamw(sl, w[0], m[0], v[0], "adamw_" + nm)
               for sl, w, m, v, nm in zip(list(scattered) + landed, mats, mat_m, mat_v, mat_names)]
    zeros_tail = jnp.zeros((LANES,), F32)
    g_s, d_s, m_s, v_s = _adamw(small, pack(gains, zeros_tail), pack(gain_m, zeros_tail), pack(gain_v, zeros_tail), "adamw_gains")

    def unpack(packed):
        flat = packed.reshape(-1)
        outs, off = [], 0
        for g, n in zip(gains, sizes):
            outs.append(flat[off:off + n].reshape(g.shape))
            off += n
        return outs

    loss = g_s.reshape(-1)[sum(sizes)]

    order = ["norm_mix", "w_in", "q_latent_norm", "w_uq", "kv_latent_norm", "w_ukv", "out_norm_mla", "out_norm_sb",
             "w_o", "norm_ffn", "w_gate", "w_up", "w_down", "norm_final"]
    gain_names = ["norm_mix", "q_latent_norm", "kv_latent_norm", "out_norm_mla", "out_norm_sb", "norm_ffn", "norm_final"]
    result = [loss, dx[None]]
    for kind in range(4):
        small_parts = dict(zip(gain_names, unpack([g_s, d_s, m_s, v_s][kind])))
        mat_parts = {nm: out[kind][None] for nm, out in zip(mat_names, mat_out)}
        result += [small_parts[nm] if nm in small_parts else mat_parts[nm] for nm in order]
    return tuple(result)
```

```python
import math

import jax
import jax.numpy as jnp
from jax import lax
from jax.experimental import pallas as pl
from jax.experimental.pallas import tpu as pltpu

F32 = jnp.float32
BF16 = jnp.bfloat16
MESH = pl.DeviceIdType.MESH

EPS = 1e-6
ROPE_THETA = 10000.0
MLA_HEADS = 8
MLA_NOPE = 64
MLA_ROPE = 32
SB_HEADS = 8
HEAD_DIM = 64
Q_LORA = 256
KV_LORA = 128
MLA_SCALE = 1.0 / math.sqrt(MLA_NOPE + MLA_ROPE)
SB_SCALE = 1.0 / math.sqrt(HEAD_DIM)
LOG2E = math.log2(math.e)
SB_DEAD = -160.0
N_DEV = 8

ADAM_LR = 0.001
ADAM_B1 = 0.9
ADAM_B2 = 0.999
ADAM_EPS = 1e-08
ADAM_WD = 0.01
ADAM_STEP = 10

LANES = 128
ATT_TILE = 512
SB_TILE = 512
TRI = 256
ROW_TILE = 512
FFN_BWD_ROW_TILE = 256
PROJ_BWD_ROW_TILE = 256
TN_BLOCK = 256
TN_RESIDENT_BYTES = 16 * 1024 * 1024
VMEM_LIMIT = 56 * 1024 * 1024
NEG = -1e30


def _cparams(*sem):
    return pltpu.CompilerParams(dimension_semantics=sem, vmem_limit_bytes=VMEM_LIMIT)


def _dot(a, b):
    return jnp.dot(a, b, preferred_element_type=F32)


def _dot_nt(a, b):
    return lax.dot_general(a, b, (((1,), (1,)), ((), ())), preferred_element_type=F32)


def _dot_tn(a, b):
    return lax.dot_general(a, b, (((0,), (0,)), ((), ())), preferred_element_type=F32)


def _rms(x, g):
    r = lax.rsqrt(jnp.mean(x * x, axis=-1, keepdims=True) + EPS)
    return x * r * g


def _rms_bwd(x, g, dy):
    r = lax.rsqrt(jnp.mean(x * x, axis=-1, keepdims=True) + EPS)
    n = x * r
    dn = dy * g
    dx = r * (dn - n * jnp.mean(dn * n, axis=-1, keepdims=True))
    return dx, jnp.sum(dy * n, axis=0, keepdims=True)


def _rope(x, cos, sin_a, sin_b):
    return x * cos + pltpu.roll(x, 112, 1) * sin_a + pltpu.roll(x, 16, 1) * sin_b


def _rope_t(g, cos, sin_a, sin_b):
    return g * cos + pltpu.roll(g * sin_a, 16, 1) + pltpu.roll(g * sin_b, 112, 1)


def _row_spec(tm, width):
    return pl.BlockSpec((tm, width), lambda r: (r, 0))


def _full_spec(shape):
    return pl.BlockSpec(shape, lambda *_: (0,) * len(shape))


def _accumulate(ref, val, first):
    @pl.when(first)
    def _():
        ref[...] = val

    @pl.when(jnp.logical_not(first))
    def _():
        ref[...] += val


def _proj_in_fwd(x, g_mix, w_a, g_q, w_uq, g_kv, w_ukv, cos, sin_a, sin_b):
    s, d = x.shape
    tm = min(ROW_TILE, s)

    def body(x_ref, gm_ref, wa_ref, gq_ref, wuq_ref, gkv_ref, wukv_ref, cos_ref, sa_ref, sb_ref,
             u_ref, cq_ref, ckv_ref, cqn_ref, ckvn_ref, qn_ref, qr_ref, kv_ref, kr_ref, sbq_ref):
        u = _rms(x_ref[...], gm_ref[...]).astype(BF16)
        u_ref[...] = u
        cq = _dot(u, wa_ref[:, 0:256])
        ckv = _dot(u, wa_ref[:, 256:384])
        kr = _dot(u, wa_ref[:, 384:512])
        cq_ref[...] = cq
        ckv_ref[...] = ckv
        cqn = _rms(cq, gq_ref[...]).astype(BF16)
        ckvn = _rms(ckv, gkv_ref[...]).astype(BF16)
        cqn_ref[...] = cqn
        ckvn_ref[...] = ckvn
        cos_t, sa_t, sb_t = cos_ref[...], sa_ref[...], sb_ref[...]
        qn_ref[...] = (_dot(cqn, wuq_ref[:, 0:512]) * MLA_SCALE).astype(BF16)
        for half in range(2):
            lo = 512 + half * LANES
            qr = _dot(cqn, wuq_ref[:, lo:lo + LANES])
            qr_ref[:, half * LANES:(half + 1) * LANES] = (_rope(qr, cos_t, sa_t, sb_t) * MLA_SCALE).astype(BF16)
        kv_ref[...] = _dot(ckvn, wukv_ref[...]).astype(BF16)
        krt = kr + pltpu.roll(kr, 32, 1) + pltpu.roll(kr, 64, 1) + pltpu.roll(kr, 96, 1)
        kr_ref[...] = _rope(krt, cos_t, sa_t, sb_t).astype(BF16)
        sbq_ref[:, 0:512] = (_dot(u, wa_ref[:, 512:1024]) * (SB_SCALE * LOG2E)).astype(BF16)
        sbq_ref[:, 512:1536] = _dot(u, wa_ref[:, 1024:2048]).astype(BF16)

    outs = [
        jax.ShapeDtypeStruct((s, d), BF16),
        jax.ShapeDtypeStruct((s, 256), F32),
        jax.ShapeDtypeStruct((s, 128), F32),
        jax.ShapeDtypeStruct((s, 256), BF16),
        jax.ShapeDtypeStruct((s, 128), BF16),
        jax.ShapeDtypeStruct((s, 512), BF16),
        jax.ShapeDtypeStruct((s, 256), BF16),
        jax.ShapeDtypeStruct((s, 1024), BF16),
        jax.ShapeDtypeStruct((s, 128), BF16),
        jax.ShapeDtypeStruct((s, 1536), BF16),
    ]
    return pl.pallas_call(
        body, name="proj_in_fwd", grid=(s // tm,), out_shape=outs,
        in_specs=[_row_spec(tm, d), _full_spec(g_mix.shape), _full_spec(w_a.shape), _full_spec(g_q.shape),
                  _full_spec(w_uq.shape), _full_spec(g_kv.shape), _full_spec(w_ukv.shape),
                  _row_spec(tm, LANES), _row_spec(tm, LANES), _row_spec(tm, LANES)],
        out_specs=[_row_spec(tm, o.shape[1]) for o in outs],
        compiler_params=_cparams("arbitrary"),
    )(x, g_mix, w_a, g_q, w_uq, g_kv, w_ukv, cos, sin_a, sin_b)


def _attn_out_fwd(o_mla, o_sb, g_mla, g_sb, w_o, x, g_ffn):
    s, d = x.shape
    tm = min(ROW_TILE, s)

    def body(oa_ref, ob_ref, ga_ref, gb_ref, wo_ref, x_ref, gf_ref, merged_ref, h1_ref, f_ref):
        na = _rms(oa_ref[...], ga_ref[...]).astype(BF16)
        nb = _rms(ob_ref[...], gb_ref[...]).astype(BF16)
        merged_ref[:, 0:512] = na
        merged_ref[:, 512:1024] = nb
        h1 = x_ref[...] + _dot(na, wo_ref[0:512, :]) + _dot(nb, wo_ref[512:1024, :])
        h1_ref[...] = h1
        f_ref[...] = _rms(h1, gf_ref[...]).astype(BF16)

    outs = [jax.ShapeDtypeStruct((s, d), BF16), jax.ShapeDtypeStruct((s, d), F32), jax.ShapeDtypeStruct((s, d), BF16)]
    return pl.pallas_call(
        body, name="attn_out_fwd", grid=(s // tm,), out_shape=outs,
        in_specs=[_row_spec(tm, 512), _row_spec(tm, 512), _full_spec(g_mla.shape), _full_spec(g_sb.shape),
                  _full_spec(w_o.shape), _row_spec(tm, d), _full_spec(g_ffn.shape)],
        out_specs=[_row_spec(tm, d)] * 3,
        compiler_params=_cparams("arbitrary"),
    )(o_mla, o_sb, g_mla, g_sb, w_o, x, g_ffn)


def _ffn_tile(d_ff):
    return d_ff // 2 if (d_ff // 2) % LANES == 0 else d_ff


def _ffn_fwd(f, h1, w_gate, w_up, w_down):
    s, d = h1.shape
    d_ff = w_gate.shape[1]
    tm = min(ROW_TILE, s)
    tf = _ffn_tile(d_ff)

    def body(f_ref, h1_ref, wg_ref, wu_ref, wd_ref, gate_ref, up_ref, h2_ref):
        j = pl.program_id(1)
        fb = f_ref[...]
        gate = _dot(fb, wg_ref[...])
        up = _dot(fb, wu_ref[...])
        gate_ref[...] = gate.astype(BF16)
        up_ref[...] = up.astype(BF16)
        act = (gate * jax.nn.sigmoid(gate) * up).astype(BF16)
        part = _dot(act, wd_ref[...])

        @pl.when(j == 0)
        def _():
            h2_ref[...] = h1_ref[...] + part

        @pl.when(j != 0)
        def _():
            h2_ref[...] += part

    outs = [jax.ShapeDtypeStruct((s, d_ff), BF16), jax.ShapeDtypeStruct((s, d_ff), BF16), jax.ShapeDtypeStruct((s, d), F32)]
    return pl.pallas_call(
        body, name="ffn_fwd", grid=(s // tm, d_ff // tf), out_shape=outs,
        in_specs=[pl.BlockSpec((tm, d), lambda r, j: (r, 0)), pl.BlockSpec((tm, d), lambda r, j: (r, 0)),
                  pl.BlockSpec((d, tf), lambda r, j: (0, j)), pl.BlockSpec((d, tf), lambda r, j: (0, j)),
                  pl.BlockSpec((tf, d), lambda r, j: (j, 0))],
        out_specs=[pl.BlockSpec((tm, tf), lambda r, j: (r, j)), pl.BlockSpec((tm, tf), lambda r, j: (r, j)),
                   pl.BlockSpec((tm, d), lambda r, j: (r, 0))],
        compiler_params=_cparams("arbitrary", "arbitrary"),
    )(f, h1, w_gate, w_up, w_down)


def _final_loss(h2, target, g_final):
    s, d = h2.shape
    tm = min(ROW_TILE, s)

    def body(h2_ref, t_ref, g_ref, loss_ref, dh2_ref, dh2b_ref, dg_ref):
        first = pl.program_id(0) == 0
        h2v = h2_ref[...]
        g = g_ref[...]
        diff = _rms(h2v, g) - t_ref[...]
        part = 0.5 * jnp.sum(jnp.mean(diff * diff, axis=-1, keepdims=True), axis=0, keepdims=True)
        _accumulate(loss_ref, jnp.broadcast_to(part, loss_ref.shape), first)
        dx, dg = _rms_bwd(h2v, g, diff * (1.0 / d))
        dh2_ref[...] = dx
        dh2b_ref[...] = dx.astype(BF16)
        _accumulate(dg_ref, dg, first)

    outs = [jax.ShapeDtypeStruct((1, LANES), F32), jax.ShapeDtypeStruct((s, d), F32), jax.ShapeDtypeStruct((s, d), BF16),
            jax.ShapeDtypeStruct((1, d), F32)]
    return pl.pallas_call(
        body, name="final_loss", grid=(s // tm,), out_shape=outs,
        in_specs=[_row_spec(tm, d), _row_spec(tm, d), _full_spec((1, d))],
        out_specs=[_full_spec((1, LANES)), _row_spec(tm, d), _row_spec(tm, d), _full_spec((1, d))],
        compiler_params=_cparams("arbitrary"),
    )(h2, target, g_final)


def _tile_iotas(t):
    return lax.broadcasted_iota(jnp.int32, (t, t), 0), lax.broadcasted_iota(jnp.int32, (t, t), 1)


def _mla_fwd(qn, qr, kv, kr, shards):
    s = qn.shape[0]
    t = min(ATT_TILE, s)
    pairs = MLA_HEADS // 2
    nq = s // t
    n = len(shards)

    def body(*refs):
        qn_ref, qr_ref, kn_ref, v_ref, kr_ref = refs[:5]
        o_ref, lse_ref = refs[5 + n:7 + n]
        qcat_ref, m_ref, l_ref, acc_ref = refs[7 + 2 * n:11 + 2 * n]
        hp, i = pl.program_id(0), pl.program_id(1)
        ride = _Exchange(True, refs[5:5 + n], refs[7 + n:7 + 2 * n], *refs[11 + 2 * n:])

        @pl.when((hp == 0) & (i == 0))
        def _():
            ride.start()

        lane = lax.broadcasted_iota(jnp.int32, (1, LANES), 1)
        row, col = _tile_iotas(t)
        causal = col <= row
        q_pair, q_quad = qn_ref[...], qr_ref[...]
        zero = jnp.zeros_like(q_pair)
        for hh in range(2):
            in_head = (lane // HEAD_DIM) == hh
            in_rope = (lane // MLA_ROPE) == (hp % 2) * 2 + hh
            qcat_ref[hh * t:(hh + 1) * t, 0:LANES] = jnp.where(in_head, q_pair, zero)
            qcat_ref[hh * t:(hh + 1) * t, LANES:2 * LANES] = jnp.where(in_rope, q_quad, zero)
        m_ref[...] = jnp.full_like(m_ref, NEG)
        l_ref[...] = jnp.zeros_like(l_ref)
        acc_ref[...] = jnp.zeros_like(acc_ref)

        def tile(j, width, masked):
            rows = pl.ds(pl.multiple_of(j * t, t), width * t)
            kcat = jnp.concatenate([kn_ref[rows, :], kr_ref[rows, :]], axis=1)
            v_ones = jnp.concatenate([v_ref[rows, :], jnp.ones((width * t, LANES), BF16)], axis=1)
            scores = [_dot_nt(qcat_ref[hh * t:(hh + 1) * t, :], kcat) for hh in range(2)]
            for hh in range(2):
                half = slice(hh * t, (hh + 1) * t)
                sc = jnp.where(causal, scores[hh], NEG) if masked else scores[hh]
                m = m_ref[half, :]
                m_new = jnp.maximum(m, jnp.max(sc, axis=-1, keepdims=True))
                alpha = jnp.exp(m - m_new)
                p = jnp.exp(sc - jnp.concatenate([m_new] * (width * t // LANES), axis=1))
                pv = _dot(p.astype(BF16), v_ones)
                l_ref[half, :] = alpha * l_ref[half, :] + pv[:, LANES:]
                acc_ref[half, :] = alpha * acc_ref[half, :] + pv[:, :LANES]
                m_ref[half, :] = m_new

        tile(i, 1, True)

        def step(n, carry):
            tile(4 * n, 4, False)
            return carry

        lax.fori_loop(0, i // 4, step, 0)

        @pl.when(i % 4 >= 2)
        def _():
            tile((i // 4) * 4, 2, False)

        @pl.when(i % 2 == 1)
        def _():
            tile(i - 1, 1, False)

        first = (lane // HEAD_DIM) == 0
        o = acc_ref[...] / l_ref[...]
        lse = m_ref[...] + jnp.log(l_ref[...])
        o_ref[...] = jnp.where(first, o[0:t], o[t:2 * t])
        lse_ref[...] = jnp.where(first, lse[0:t], lse[t:2 * t])

        @pl.when((hp == pairs - 1) & (i == nq - 1))
        def _():
            ride.finish()

    gathered_shapes, sems = _exchange_shapes(True, shards)
    outs = [jax.ShapeDtypeStruct((s, 512), F32), jax.ShapeDtypeStruct((pairs, s, LANES), F32)] + gathered_shapes
    res = pl.pallas_call(
        body, name="mla_fwd", grid=(pairs, nq), out_shape=outs,
        in_specs=[pl.BlockSpec((t, LANES), lambda hp, i: (i, hp)), pl.BlockSpec((t, LANES), lambda hp, i: (i, hp // 2)),
                  pl.BlockSpec((s, LANES), lambda hp, i: (0, hp)), pl.BlockSpec((s, LANES), lambda hp, i: (0, 4 + hp)),
                  pl.BlockSpec((s, LANES), lambda hp, i: (0, 0))] + [ANY] * n,
        out_specs=[pl.BlockSpec((t, LANES), lambda hp, i: (i, hp)), pl.BlockSpec((None, t, LANES), lambda hp, i: (hp, i, 0))]
        + [ANY] * n,
        scratch_shapes=[pltpu.VMEM((2 * t, 2 * LANES), BF16), pltpu.VMEM((2 * t, LANES), F32), pltpu.VMEM((2 * t, LANES), F32),
                        pltpu.VMEM((2 * t, LANES), F32)] + sems,
        compiler_params=_cparams("arbitrary", "arbitrary"),
    )(qn, qr, kv, kv, kr, *shards)
    return res[0], res[1], res[2:]


HEADS = (0, 1)


def _sb_logs(z2, strict, masked):
    log_b = jnp.minimum(z2, 0.0) - jnp.log2(1.0 + jnp.exp2(-jnp.abs(z2)))
    log_1m = log_b - z2
    if masked:
        log_1m = jnp.where(strict, log_1m, 0.0)
    return log_1m, log_b


def _block_totals(x):
    t, w = x.shape
    nb = max(w // TRI, 1)
    bw = w // nb
    blocks = [x[:, b * bw:(b + 1) * bw] for b in range(nb)]
    totals = [jnp.broadcast_to(jnp.sum(blk, axis=-1, keepdims=True), (t, LANES)) for blk in blocks]
    whole = totals[0]
    for tot in totals[1:]:
        whole = whole + tot
    return blocks, totals, whole


def _running_sums(blocks, totals, tri, carry, suffix):
    nb = len(blocks)
    reps = blocks[0].shape[1] // LANES
    outs = [None] * nb
    run = carry
    for b in (range(nb - 1, -1, -1) if suffix else range(nb)):
        outs[b] = _dot(blocks[b].astype(BF16), tri) + jnp.concatenate([run] * reps, axis=1)
        run = run + totals[b]
    return outs[0] if nb == 1 else jnp.concatenate(outs, axis=1)


def _tri(t, rel):
    n = min(TRI, t)
    row, col = _tile_iotas(n)
    return rel(row, col).astype(BF16)


def _sweep_width(t):
    return t // 2 if t // 2 >= TRI else t


def _sb_fwd(qkv):
    s = qkv.shape[0]
    t = min(SB_TILE, s)
    sw = _sweep_width(t)
    pairs = SB_HEADS // 2

    def body(q_ref, k_ref, v_ref, o_ref, tot_ref, cnt_ref, qm_ref, right_ref, acc_ref):
        i = pl.program_id(1)
        lane = lax.broadcasted_iota(jnp.int32, (1, LANES), 1)
        row, col = _tile_iotas(t)
        strict = col < row
        t_suffix = _tri(t, lambda r, c: r > c)
        q_pair = q_ref[...]
        for hh in range(2):
            qm_ref[hh] = jnp.where((lane // HEAD_DIM) == hh, q_pair, jnp.zeros_like(q_pair))
        right_ref[...] = jnp.zeros_like(right_ref)
        acc_ref[...] = jnp.zeros_like(acc_ref)

        def tile(start, width, masked):
            rows = pl.ds(pl.multiple_of(start, width), width)
            k, v = k_ref[rows, :], v_ref[rows, :]
            for hh in HEADS:
                log_1m, log_b = _sb_logs(_dot_nt(qm_ref[hh], k), strict, masked)
                blocks, totals, whole = _block_totals(log_1m)
                a = jnp.exp2(log_b + _running_sums(blocks, totals, t_suffix, right_ref[hh], True))
                if masked:
                    a = jnp.where(strict, a, 0.0)
                right_ref[hh] += whole
                acc_ref[hh] += _dot(a.astype(BF16), v)

        tile(i * t, t, True)

        def alive(n):
            return (n < i * (t // sw)) & (jnp.max(right_ref[...]) > SB_DEAD)

        def step(n):
            tile((i * (t // sw) - 1 - n) * sw, sw, False)
            return n + 1

        swept = lax.while_loop(alive, step, jnp.int32(0))
        cnt_ref[...] = jnp.full(cnt_ref.shape, swept.astype(F32))
        first = (lane // HEAD_DIM) == 0
        o_ref[...] = jnp.where(first, acc_ref[0], acc_ref[1])
        tot_ref[...] = jnp.where(first, right_ref[0], right_ref[1])

    outs = [jax.ShapeDtypeStruct((s, 512), F32), jax.ShapeDtypeStruct((pairs, s, LANES), F32),
            jax.ShapeDtypeStruct((pairs, s // t, 8, LANES), F32)]
    return pl.pallas_call(
        body, name="sb_fwd", grid=(pairs, s // t), out_shape=outs,
        in_specs=[pl.BlockSpec((t, LANES), lambda hp, i: (i, hp)), pl.BlockSpec((s, LANES), lambda hp, i: (0, 4 + hp)),
                  pl.BlockSpec((s, LANES), lambda hp, i: (0, 8 + hp))],
        out_specs=[pl.BlockSpec((t, LANES), lambda hp, i: (i, hp)), pl.BlockSpec((None, t, LANES), lambda hp, i: (hp, i, 0)),
                   pl.BlockSpec((None, None, 8, LANES), lambda hp, i: (hp, i, 0, 0))],
        scratch_shapes=[pltpu.VMEM((2, t, LANES), BF16), pltpu.VMEM((2, t, LANES), F32), pltpu.VMEM((2, t, LANES), F32)],
        compiler_params=_cparams("arbitrary", "arbitrary"),
    )(qkv, qkv, qkv)


def _sb_bwd(qkv, do, tot, cnt):
    s = qkv.shape[0]
    t = min(SB_TILE, s)
    sw = _sweep_width(t)
    pairs = SB_HEADS // 2

    def body(q_ref, k_ref, v_ref, do_ref, tot_ref, cnt_ref, dq_ref, dk_ref, dv_ref,
             qm_ref, dob_ref, total_s, left_l, left_g, dq_s):
        i = pl.program_id(1)

        @pl.when(i == 0)
        def _():
            dk_ref[...] = jnp.zeros_like(dk_ref)
            dv_ref[...] = jnp.zeros_like(dv_ref)

        lane = lax.broadcasted_iota(jnp.int32, (1, LANES), 1)
        row, col = _tile_iotas(t)
        strict = col < row
        t_suffix = _tri(t, lambda r, c: r > c)
        t_excl = _tri(t, lambda r, c: r < c)
        q_pair, do_pair, tot_pair = q_ref[...], do_ref[...], tot_ref[...]
        for hh in range(2):
            in_head = (lane // HEAD_DIM) == hh
            qm_ref[hh] = jnp.where(in_head, q_pair, jnp.zeros_like(q_pair))
            dob_ref[hh] = jnp.where(in_head, do_pair, 0.0).astype(BF16)
            total_s[hh] = jnp.broadcast_to(
                jnp.sum(jnp.where(lane == hh * HEAD_DIM, tot_pair, 0.0), axis=-1, keepdims=True), (t, LANES))
        left_l[...] = jnp.zeros_like(left_l)
        left_g[...] = jnp.zeros_like(left_g)
        dq_s[...] = jnp.zeros_like(dq_s)

        def tile(start, width, masked):
            rows = pl.ds(pl.multiple_of(start, width), width)
            k, v = k_ref[rows, :], v_ref[rows, :]
            z2 = [_dot_nt(qm_ref[hh], k) for hh in HEADS]
            d_a = [_dot_nt(dob_ref[hh], v) for hh in HEADS]
            for hh in HEADS:
                qm, dob = qm_ref[hh], dob_ref[hh]
                log_1m, log_b = _sb_logs(z2[hh], strict, masked)
                blocks, totals, whole = _block_totals(log_1m)
                done = left_l[hh] + whole
                left_l[hh] = done
                a = jnp.exp2(log_b + _running_sums(blocks, totals, t_suffix, total_s[hh] - done, True))
                if masked:
                    a = jnp.where(strict, a, 0.0)
                g = a * d_a[hh]
                blocks, totals, whole = _block_totals(g)
                before = _running_sums(blocks, totals, t_excl, left_g[hh], False)
                left_g[hh] += whole
                dz = g - jnp.exp2(log_b) * (g + before)
                if masked:
                    dz = jnp.where(strict, dz, 0.0)
                dzb = dz.astype(BF16)
                dq_s[hh] += _dot(dzb, k)
                dk_ref[rows, :] += _dot_tn(dzb, qm)
                dv_ref[rows, :] += _dot_tn(a.astype(BF16), dob)

        def step(h, carry):
            tile(h * sw, sw, False)
            return carry

        swept = jnp.max(cnt_ref[...]).astype(jnp.int32)
        lax.fori_loop(i * (t // sw) - swept, i * (t // sw), step, 0)
        tile(i * t, t, True)
        dq_ref[...] = jnp.where((lane // HEAD_DIM) == 0, dq_s[0], dq_s[1])

        @pl.when(i == s // t - 1)
        def _():
            dk_ref[...] *= 1.0 / LOG2E

    outs = [jax.ShapeDtypeStruct((s, 512), F32)] * 3
    return pl.pallas_call(
        body, name="sb_bwd", grid=(pairs, s // t), out_shape=outs,
        in_specs=[pl.BlockSpec((t, LANES), lambda hp, i: (i, hp)), pl.BlockSpec((s, LANES), lambda hp, i: (0, 4 + hp)),
                  pl.BlockSpec((s, LANES), lambda hp, i: (0, 8 + hp)), pl.BlockSpec((t, LANES), lambda hp, i: (i, hp)),
                  pl.BlockSpec((None, t, LANES), lambda hp, i: (hp, i, 0)),
                  pl.BlockSpec((None, None, 8, LANES), lambda hp, i: (hp, i, 0, 0))],
        out_specs=[pl.BlockSpec((t, LANES), lambda hp, i: (i, hp)), pl.BlockSpec((s, LANES), lambda hp, i: (0, hp)),
                   pl.BlockSpec((s, LANES), lambda hp, i: (0, hp))],
        scratch_shapes=[pltpu.VMEM((2, t, LANES), BF16), pltpu.VMEM((2, t, LANES), BF16)]
        + [pltpu.VMEM((2, t, LANES), F32)] * 4,
        compiler_params=_cparams("arbitrary", "arbitrary"),
    )(qkv, qkv, qkv, do, tot, cnt)


def _mla_bwd(qn, qr, kv, kr, do, o, lse, parts):
    s = qn.shape[0]
    t = min(ATT_TILE, s)
    pairs = MLA_HEADS // 2
    nq = s // t
    n = len(parts)

    def body(*refs):
        qn_ref, qr_ref, kn_ref, v_ref, kr_ref, do_ref, o_ref, lse_ref = refs[:8]
        dqn_ref, dqr_ref, dkn_ref, dv_ref, dkr_ref = refs[8 + n:13 + n]
        qcat_ref, dob_ref, lse_s, delta_s, dq_s = refs[13 + 2 * n:18 + 2 * n]
        hp, i = pl.program_id(0), pl.program_id(1)
        ride = _Exchange(False, refs[8:8 + n], refs[13 + n:13 + 2 * n], *refs[18 + 2 * n:])

        @pl.when((hp == 0) & (i == 0))
        def _():
            ride.start()

        @pl.when(i == 0)
        def _():
            dkn_ref[...] = jnp.zeros_like(dkn_ref)
            dv_ref[...] = jnp.zeros_like(dv_ref)
            dkr_ref[...] = jnp.zeros_like(dkr_ref)

        lane = lax.broadcasted_iota(jnp.int32, (1, LANES), 1)
        row, col = _tile_iotas(t)
        causal = col <= row
        q_pair, q_quad, do_pair, lse_pair = qn_ref[...], qr_ref[...], do_ref[...], lse_ref[...]
        do_o = do_pair * o_ref[...]
        zero = jnp.zeros_like(q_pair)
        ropes = []
        for hh in range(2):
            in_head = (lane // HEAD_DIM) == hh
            in_rope = (lane // MLA_ROPE) == (hp % 2) * 2 + hh
            ropes.append(in_rope)
            qcat_ref[hh, :, 0:LANES] = jnp.where(in_head, q_pair, zero)
            qcat_ref[hh, :, LANES:2 * LANES] = jnp.where(in_rope, q_quad, zero)
            dob_ref[hh] = jnp.where(in_head, do_pair, 0.0).astype(BF16)
            delta_s[hh] = jnp.broadcast_to(jnp.sum(jnp.where(in_head, do_o, 0.0), axis=-1, keepdims=True), (t, LANES))
            lse_s[hh] = jnp.broadcast_to(
                jnp.sum(jnp.where(lane == hh * HEAD_DIM, lse_pair, 0.0), axis=-1, keepdims=True), (t, LANES))
        dq_s[...] = jnp.zeros_like(dq_s)
        reps = t // LANES

        def tile(j, width, masked):
            rows = pl.ds(pl.multiple_of(j * t, t), width * t)
            kcat = jnp.concatenate([kn_ref[rows, :], kr_ref[rows, :]], axis=1)
            v = v_ref[rows, :]
            sc = [_dot_nt(qcat_ref[hh], kcat) for hh in HEADS]
            dp = [_dot_nt(dob_ref[hh], v) for hh in HEADS]
            p = [jnp.exp(sc[hh] - jnp.concatenate([lse_s[hh]] * (width * reps), axis=1)) for hh in HEADS]
            if masked:
                p = [jnp.where(causal, p[hh], 0.0) for hh in HEADS]
            ds = [(p[hh] * (dp[hh] - jnp.concatenate([delta_s[hh]] * (width * reps), axis=1))).astype(BF16) for hh in HEADS]
            for hh in HEADS:
                dq_s[hh] += _dot(ds[hh], kcat)
            dkcat = _dot_tn(ds[0], qcat_ref[0]) + _dot_tn(ds[1], qcat_ref[1])
            dkn_ref[rows, :] += dkcat[:, 0:LANES]
            dkr_ref[rows, :] += dkcat[:, LANES:2 * LANES]
            dv_ref[rows, :] += _dot_tn(p[0].astype(BF16), dob_ref[0]) + _dot_tn(p[1].astype(BF16), dob_ref[1])

        def step(n, carry):
            tile(4 * n, 4, False)
            return carry

        lax.fori_loop(0, i // 4, step, 0)

        @pl.when(i % 4 >= 2)
        def _():
            tile((i // 4) * 4, 2, False)

        @pl.when(i % 2 == 1)
        def _():
            tile(i - 1, 1, False)

        tile(i, 1, True)
        dqn_ref[...] =jnp.where((lane // HEAD_DIM) == 0, dq_s[0, :, 0:LANES], dq_s[1, :, 0:LANES])
        dqr_ref[...] = (jnp.where(ropes[0], dq_s[0, :, LANES:2 * LANES], 0.0)
                        + jnp.where(ropes[1], dq_s[1, :, LANES:2 * LANES], 0.0))

        @pl.when((hp == pairs - 1) & (i == nq - 1))
        def _():
            ride.finish()

    pair_block = pl.BlockSpec((t, LANES), lambda hp, i: (i, hp))
    once = pl.Buffered(1)
    landed_shapes, sems = _exchange_shapes(False, parts)
    outs = [jax.ShapeDtypeStruct((s, 512), F32), jax.ShapeDtypeStruct((pairs, s, LANES), F32),
            jax.ShapeDtypeStruct((s, 512), F32), jax.ShapeDtypeStruct((s, 512), F32),
            jax.ShapeDtypeStruct((pairs, s, LANES), F32)] + landed_shapes
    res = pl.pallas_call(
        body, name="mla_bwd", grid=(pairs, nq), out_shape=outs,
        in_specs=[pair_block, pl.BlockSpec((t, LANES), lambda hp, i: (i, hp // 2)),
                  pl.BlockSpec((s, LANES), lambda hp, i: (0, hp), pipeline_mode=once),
                  pl.BlockSpec((s, LANES), lambda hp, i: (0, 4 + hp), pipeline_mode=once),
                  pl.BlockSpec((s, LANES), lambda hp, i: (0, 0), pipeline_mode=once), pair_block, pair_block,
                  pl.BlockSpec((None, t, LANES), lambda hp, i: (hp, i, 0))] + [ANY] * n,
        out_specs=[pair_block, pl.BlockSpec((None, t, LANES), lambda hp, i: (hp, i, 0)),
                   pl.BlockSpec((s, LANES), lambda hp, i: (0, hp), pipeline_mode=once),
                   pl.BlockSpec((s, LANES), lambda hp, i: (0, hp), pipeline_mode=once),
                   pl.BlockSpec((None, s, LANES), lambda hp, i: (hp, 0, 0), pipeline_mode=once)] + [ANY] * n,
        scratch_shapes=[pltpu.VMEM((2, t, 2 * LANES), BF16), pltpu.VMEM((2, t, LANES), BF16), pltpu.VMEM((2, t, LANES), F32),
                        pltpu.VMEM((2, t, LANES), F32), pltpu.VMEM((2, t, 2 * LANES), F32)] + sems,
        compiler_params=_cparams("arbitrary", "arbitrary"),
    )(qn, qr, kv, kv, kr, do, o, lse, *parts)
    return res[:5], res[5:]


def _ffn_bwd(dh2, dh2b, gate, up, h1, g_ffn, w_down, w_gate, w_up):
    s, d = h1.shape
    d_ff = gate.shape[1]
    tm = min(FFN_BWD_ROW_TILE, s)
    tf = _ffn_tile(d_ff)

    def act_body(dh2b_ref, gate_ref, up_ref, wd_ref, dgate_ref, dup_ref, act_ref):
        dact = _dot_nt(dh2b_ref[...], wd_ref[...])
        gate_v = gate_ref[...].astype(F32)
        up_v = up_ref[...].astype(F32)
        sig = jax.nn.sigmoid(gate_v)
        silu = gate_v * sig
        dup_ref[...] = (dact * silu).astype(BF16)
        dgate_ref[...] = ((dact * up_v) * (sig * (1.0 + gate_v - silu))).astype(BF16)
        act_ref[...] = (silu * up_v).astype(BF16)

    ff = pl.BlockSpec((tm, tf), lambda j, r: (r, j))
    dgate, dup, act = pl.pallas_call(
        act_body, name="ffn_bwd_act", grid=(d_ff // tf, s // tm), out_shape=[jax.ShapeDtypeStruct((s, d_ff), BF16)] * 3,
        in_specs=[pl.BlockSpec((tm, d), lambda j, r: (r, 0)), ff, ff, pl.BlockSpec((tf, d), lambda j, r: (j, 0))],
        out_specs=[ff, ff, ff],
        compiler_params=_cparams("arbitrary", "arbitrary"),
    )(dh2b, gate, up, w_down)

    def df_body(dgate_ref, dup_ref, dh2_ref, h1_ref, g_ref, wg_ref, wu_ref, dh1_ref, dh1b_ref, dg_ref):
        df = _dot_nt(dgate_ref[...], wg_ref[...]) + _dot_nt(dup_ref[...], wu_ref[...])
        dx, dg = _rms_bwd(h1_ref[...], g_ref[...], df)
        dh1 = dh2_ref[...] + dx
        dh1_ref[...] = dh1
        dh1b_ref[...] = dh1.astype(BF16)
        _accumulate(dg_ref, dg, pl.program_id(0) == 0)

    outs = [jax.ShapeDtypeStruct((s, d), F32), jax.ShapeDtypeStruct((s, d), BF16), jax.ShapeDtypeStruct((1, d), F32)]
    dh1, dh1b, dg = pl.pallas_call(
        df_body, name="ffn_bwd_df", grid=(s // tm,), out_shape=outs,
        in_specs=[_row_spec(tm, d_ff), _row_spec(tm, d_ff), _row_spec(tm, d), _row_spec(tm, d), _full_spec((1, d)),
                  _full_spec(w_gate.shape), _full_spec(w_up.shape)],
        out_specs=[_row_spec(tm, d), _row_spec(tm, d), _full_spec((1, d))],
        compiler_params=_cparams("arbitrary"),
    )(dgate, dup, dh2, h1, g_ffn, w_gate, w_up)
    return dgate, dup, act, dh1, dh1b, dg


def _tn_matmul(a, b, name):
    assert a.dtype == BF16 and b.dtype == BF16
    s, m = a.shape
    n = b.shape[1]
    if s * m * 2 <= TN_RESIDENT_BYTES:
        tm, tn = m, min(n, TN_BLOCK)
    else:
        tm, tn = TN_BLOCK, n

    def body(a_ref, b_ref, o_ref):
        o_ref[...] = _dot_tn(a_ref[...], b_ref[...]).astype(BF16)

    return pl.pallas_call(
        body, name=name, grid=(m // tm, n // tn), out_shape=jax.ShapeDtypeStruct((m, n), BF16),
        in_specs=[pl.BlockSpec((s, tm), lambda i, j: (0, i)), pl.BlockSpec((s, tn), lambda i, j: (0, j))],
        out_specs=pl.BlockSpec((tm, tn), lambda i, j: (i, j)),
        compiler_params=_cparams("arbitrary", "arbitrary"),
    )(a, b)


def _attn_out_bwd(dh1, w_o, o_mla, o_sb, g_mla, g_sb):
    s, d = dh1.shape
    tm = min(ROW_TILE, s)

    def body(dh1_ref, wo_ref, oa_ref, ob_ref, ga_ref, gb_ref, doa_ref, dob_ref, dga_ref, dgb_ref):
        first = pl.program_id(0) == 0
        dh1b = dh1_ref[...]
        dxa, dga = _rms_bwd(oa_ref[...], ga_ref[...], _dot_nt(dh1b, wo_ref[0:512, :]))
        dxb, dgb = _rms_bwd(ob_ref[...], gb_ref[...], _dot_nt(dh1b, wo_ref[512:1024, :]))
        doa_ref[...] = dxa
        dob_ref[...] = dxb
        _accumulate(dga_ref, dga, first)
        _accumulate(dgb_ref, dgb, first)

    outs = [jax.ShapeDtypeStruct((s, 512), F32)] * 2 + [jax.ShapeDtypeStruct((1, 512), F32)] * 2
    return pl.pallas_call(
        body, name="attn_out_bwd", grid=(s // tm,), out_shape=outs,
        in_specs=[_row_spec(tm, d), _full_spec(w_o.shape), _row_spec(tm, 512), _row_spec(tm, 512),
                  _full_spec((1, 512)), _full_spec((1, 512))],
        out_specs=[_row_spec(tm, 512), _row_spec(tm, 512), _full_spec((1, 512)), _full_spec((1, 512))],
        compiler_params=_cparams("arbitrary"),
    )(dh1, w_o, o_mla, o_sb, g_mla, g_sb)


def _proj_in_bwd(dqn, dqr, dkn, dv, dkr, dq_sb, dk_sb, dv_sb, cq, ckv, x, dh1, cos, sin_a, sin_b,
                 g_q, g_kv, g_mix, w_uq, w_ukv, w_a):
    s, d = x.shape
    tm = min(PROJ_BWD_ROW_TILE, s)

    def body(dqn_ref, dqr_ref, dkn_ref, dv_ref, dkr_ref, dqs_ref, dks_ref, dvs_ref, cq_ref, ckv_ref, x_ref, dh1_ref,
             cos_ref, sa_ref, sb_ref, gq_ref, gkv_ref, gm_ref, wuq_ref, wukv_ref, wa_ref,
             dx_ref, dproj_ref, dq_ref, dkv_ref, dgq_ref, dgkv_ref, dgm_ref):
        first = pl.program_id(0) == 0
        lane = lax.broadcasted_iota(jnp.int32, (1, LANES), 1)
        cos_t, sa_t, sb_t = cos_ref[...], sa_ref[...], sb_ref[...]
        dq_ref[:, 0:512] = (dqn_ref[...] * MLA_SCALE).astype(BF16)
        for half in range(2):
            quad = (dqr_ref[2 * half] + dqr_ref[2 * half + 1]) * MLA_SCALE
            dq_ref[:, 512 + half * LANES:512 + (half + 1) * LANES] = _rope_t(quad, cos_t, sa_t, sb_t).astype(BF16)
        dcq, dgq = _rms_bwd(cq_ref[...], gq_ref[...], _dot_nt(dq_ref[...], wuq_ref[...]))
        _accumulate(dgq_ref, dgq, first)
        dkv_ref[:, 0:512] = dkn_ref[...].astype(BF16)
        dkv_ref[:, 512:1024] = dv_ref[...].astype(BF16)
        dckv, dgkv = _rms_bwd(ckv_ref[...], gkv_ref[...], _dot_nt(dkv_ref[...], wukv_ref[...]))
        _accumulate(dgkv_ref, dgkv, first)
        g = _rope_t(dkr_ref[0] + dkr_ref[1] + dkr_ref[2] + dkr_ref[3], cos_t, sa_t, sb_t)
        g = g + pltpu.roll(g, 96, 1) + pltpu.roll(g, 64, 1) + pltpu.roll(g, 32, 1)
        dproj_ref[:, 0:256] = dcq.astype(BF16)
        dproj_ref[:, 256:384] = dckv.astype(BF16)
        dproj_ref[:, 384:512] = jnp.where(lane < MLA_ROPE, g, 0.0).astype(BF16)
        dproj_ref[:, 512:1024] = (dqs_ref[...] * SB_SCALE).astype(BF16)
        dproj_ref[:, 1024:1536] = dks_ref[...].astype(BF16)
        dproj_ref[:, 1536:2048] = dvs_ref[...].astype(BF16)
        dxn, dgm = _rms_bwd(x_ref[...], gm_ref[...], _dot_nt(dproj_ref[...], wa_ref[...]))
        dx_ref[...] = dh1_ref[...] + dxn
        _accumulate(dgm_ref, dgm, first)

    quad_spec = pl.BlockSpec((4, tm, LANES), lambda r: (0, r, 0))
    outs = [jax.ShapeDtypeStruct((s, d), F32), jax.ShapeDtypeStruct((s, 2048), BF16), jax.ShapeDtypeStruct((s, 768), BF16),
            jax.ShapeDtypeStruct((s, 1024), BF16), jax.ShapeDtypeStruct((1, 256), F32), jax.ShapeDtypeStruct((1, 128), F32),
            jax.ShapeDtypeStruct((1, d), F32)]
    return pl.pallas_call(
        body, name="proj_in_bwd", grid=(s // tm,), out_shape=outs,
        in_specs=[_row_spec(tm, 512), quad_spec, _row_spec(tm, 512), _row_spec(tm, 512), quad_spec,
                  _row_spec(tm, 512), _row_spec(tm, 512), _row_spec(tm, 512), _row_spec(tm, 256), _row_spec(tm, 128),
                  _row_spec(tm, d), _row_spec(tm, d), _row_spec(tm, LANES), _row_spec(tm, LANES), _row_spec(tm, LANES),
                  _full_spec((1, 256)), _full_spec((1, 128)), _full_spec((1, d)),
                  _full_spec(w_uq.shape), _full_spec(w_ukv.shape), _full_spec(w_a.shape)],
        out_specs=[_row_spec(tm, d), _row_spec(tm, 2048), _row_spec(tm, 768), _row_spec(tm, 1024),
                   _full_spec((1, 256)), _full_spec((1, 128)), _full_spec((1, d))],
        compiler_params=_cparams("arbitrary"),
    )(dqn, dqr, dkn, dv, dkr, dq_sb, dk_sb, dv_sb, cq, ckv, x, dh1, cos, sin_a, sin_b, g_q, g_kv, g_mix,
      w_uq, w_ukv, w_a)


ANY = pl.BlockSpec(memory_space=pl.ANY)


def _place():
    return lax.axis_index("x"), lax.axis_index("y"), lax.axis_index("c")


def _all_gather(shards, name):
    n = len(shards)

    def body(*refs):
        ins, outs = refs[:n], refs[n:2 * n]
        send_sems, recv_sems, local_sems = refs[2 * n:]
        x, y, c = _place()
        me, sibling = (x, y, c), (x, y, 1 - c)
        chips = [(1 - x, y), (x, 1 - y), (1 - x, 1 - y)]

        def slot(a, px, py, pc):
            return outs[a].at[4 * px + 2 * py + pc]

        def copy(a, k, block, to, src=None):
            return pltpu.make_async_remote_copy(
                src_ref=slot(a, *block) if src is None else src, dst_ref=slot(a, *block),
                send_sem=send_sems.at[a, k], recv_sem=recv_sems.at[a, k], device_id=to, device_id_type=MESH)

        mine, first, passed = [], [], []
        for a in range(n):
            own = pltpu.make_async_copy(ins[a], slot(a, *me), local_sems.at[a])
            own.start()
            mine.append(own)
            cps = [copy(a, 0, me, sibling, src=ins[a])]
            cps += [copy(a, 1 + j, me, (*chip, c), src=ins[a]) for j, chip in enumerate(chips)]
            for cp in cps:
                cp.start()
            first += cps
        for a in range(n):
            for j, chip in enumerate(chips):
                copy(a, 1 + j, (*chip, c), me).wait_recv()
                fwd = copy(a, 4 + j, (*chip, c), sibling)
                fwd.start()
                passed.append(fwd)
        for a in range(n):
            copy(a, 0, sibling, me).wait_recv()
            for j, chip in enumerate(chips):
                copy(a, 4 + j, (*chip, 1 - c), me).wait_recv()
        for cp in first + passed:
            cp.wait_send()
        for own in mine:
            own.wait()

    return pl.pallas_call(
        body, name=name,
        out_shape=[jax.ShapeDtypeStruct((N_DEV,) + v.shape, v.dtype) for v in shards],
        in_specs=[ANY] * n, out_specs=[ANY] * n,
        scratch_shapes=[pltpu.SemaphoreType.DMA((n, 7)), pltpu.SemaphoreType.DMA((n, 7)), pltpu.SemaphoreType.DMA((n,))],
    )(*shards)


class _Exchange:
    def __init__(self, gather, ins, outs, send_sems, recv_sems, local_sems):
        self.gather, self.ins, self.outs = gather, ins, outs
        self.sems = (send_sems, recv_sems, local_sems)
        x, y, c = _place()
        self.me = 4 * x + 2 * y + c
        self.peers = []
        for k in range(1, N_DEV):
            px = 1 - x if k & 4 else x
            py = 1 - y if k & 2 else y
            pc = 1 - c if k & 1 else c
            self.peers.append(((px, py, pc), 4 * px + 2 * py + pc))

    def _remote(self, a, k, landing):
        send_sems, recv_sems, _ = self.sems
        where, number = self.peers[k]
        src = self.ins[a] if self.gather else self.ins[a].at[number]
        return pltpu.make_async_remote_copy(
            src_ref=src, dst_ref=self.outs[a].at[landing], send_sem=send_sems.at[a, k], recv_sem=recv_sems.at[a, k],
            device_id=where, device_id_type=MESH)

    def _local(self, a):
        src = self.ins[a] if self.gather else self.ins[a].at[self.me]
        return pltpu.make_async_copy(src, self.outs[a].at[self.me], self.sems[2].at[a])

    def start(self):
        for a in range(len(self.ins)):
            self._local(a).start()
            for k in range(N_DEV - 1):
                self._remote(a, k, self.me).start()

    def finish(self):
        for a in range(len(self.ins)):
            for k in range(N_DEV - 1):
                self._remote(a, k, self.peers[k][1]).wait_recv()
            for k in range(N_DEV - 1):
                self._remote(a, k, self.me).wait_send()
            self._local(a).wait()


def _exchange_shapes(gather, arrays):
    out_shape = [jax.ShapeDtypeStruct(((N_DEV,) + v.shape) if gather else v.shape, v.dtype) for v in arrays]
    n = len(arrays)
    sems = [pltpu.SemaphoreType.DMA((n, N_DEV - 1)), pltpu.SemaphoreType.DMA((n, N_DEV - 1)), pltpu.SemaphoreType.DMA((n,))]
    return out_shape, sems


def _exchange(gathers, scatters, name):
    ng, ns = len(gathers), len(scatters)
    n = ng + ns

    def body(*refs):
        ins, outs, sems = refs[:n], refs[n:2 * n], refs[2 * n:]
        both = [_Exchange(True, ins[:ng], outs[:ng], *sems[:3]), _Exchange(False, ins[ng:], outs[ng:], *sems[3:])]
        for ex in both:
            ex.start()
        for ex in both:
            ex.finish()

    g_shapes, g_sems = _exchange_shapes(True, gathers)
    s_shapes, s_sems = _exchange_shapes(False, scatters)
    res = pl.pallas_call(body, name=name, out_shape=g_shapes + s_shapes, in_specs=[ANY] * n, out_specs=[ANY] * n,
                         scratch_shapes=g_sems + s_sems)(*gathers, *scatters)
    return res[:ng], res[ng:]


def _grad_row_tile(rows):
    return _largest_tile_rows(rows, 256)


def _largest_tile_rows(rows, cap):
    for cand in range(cap, 0, -8):
        if rows % cand == 0:
            return cand
    return rows


def _adamw_math(w, g, m, v):
    m_new = ADAM_B1 * m + (1.0 - ADAM_B1) * g
    v_new = ADAM_B2 * v + (1.0 - ADAM_B2) * (g * g)
    m_hat = m_new / (1.0 - ADAM_B1 ** ADAM_STEP)
    v_hat = v_new / (1.0 - ADAM_B2 ** ADAM_STEP)
    delta = -ADAM_LR * (m_hat / (jnp.sqrt(v_hat) + ADAM_EPS) + ADAM_WD * w)
    return delta, m_new, v_new


def _adamw(slots, w, m, v, name):
    k, r, cdim = slots.shape
    tr = _grad_row_tile(r)

    def body(s_ref, w_ref, m_ref, v_ref, g_ref, d_ref, mo_ref, vo_ref):
        g = s_ref[0].astype(F32)
        for q in range(1, k):
            g = g + s_ref[q].astype(F32)
        g_ref[...] = g
        d_ref[...], mo_ref[...], vo_ref[...] = _adamw_math(w_ref[...], g, m_ref[...], v_ref[...])

    blk = pl.BlockSpec((tr, cdim), lambda i: (i, 0))
    return pl.pallas_call(
        body, name=name, grid=(r // tr,), out_shape=[jax.ShapeDtypeStruct((r, cdim), F32)] * 4,
        in_specs=[pl.BlockSpec((k, tr, cdim), lambda i: (0, i, 0)), blk, blk, blk], out_specs=[blk] * 4,
        compiler_params=_cparams("arbitrary"),
    )(slots, w, m, v)


def _stack_cols(g):
    n, r, c = g.shape
    return g.transpose(1, 0, 2).reshape(r, n * c)


def _split_cols(w):
    r, nc = w.shape
    return w.reshape(r, N_DEV, nc // N_DEV).transpose(1, 0, 2)


def _rope_tables(positions):
    inv_freq = ROPE_THETA ** (-jnp.arange(0, MLA_ROPE, 2, dtype=F32) / MLA_ROPE)
    ang = positions.astype(F32).reshape(-1, 1) * inv_freq[None, :]
    cos, sin, zero = jnp.cos(ang), jnp.sin(ang), jnp.zeros_like(ang)
    reps = LANES // MLA_ROPE
    return (jnp.tile(jnp.concatenate([cos, cos], axis=1), (1, reps)),
            jnp.tile(jnp.concatenate([-sin, zero], axis=1), (1, reps)),
            jnp.tile(jnp.concatenate([zero, sin], axis=1), (1, reps)))


def _local_step(x, positions, loss_target, gains, g_in, g_uq, g_ukv, late_shards):
    norm_mix, q_norm, kv_norm, out_mla, out_sb, norm_ffn, norm_final = gains
    d = x.shape[1]
    w_in = _stack_cols(g_in)
    w_a = jnp.concatenate([w_in[:, :416], jnp.zeros((d, 96), BF16), w_in[:, 416:]], axis=1)
    w_uq = jnp.concatenate([g_uq[:, :, :MLA_NOPE].transpose(1, 0, 2).reshape(Q_LORA, -1),
                            g_uq[:, :, MLA_NOPE:].transpose(1, 0, 2).reshape(Q_LORA, -1)], axis=1)
    w_ukv = jnp.concatenate([g_ukv[:, :, :MLA_NOPE].transpose(1, 0, 2).reshape(KV_LORA, -1),
                             g_ukv[:, :, MLA_NOPE:].transpose(1, 0, 2).reshape(KV_LORA, -1)], axis=1)
    cos, sin_a, sin_b = _rope_tables(positions)

    u, cq, ckv, cqn, ckvn, qn, qr, kv, kr, qkv_sb = _proj_in_fwd(x, norm_mix, w_a, q_norm, w_uq, kv_norm, w_ukv, cos, sin_a, sin_b)
    o_mla, lse, (g_o, g_gate, g_up, g_down) = _mla_fwd(qn, qr, kv, kr, late_shards)
    w_o = g_o.reshape(-1, d)
    w_gate, w_up = _stack_cols(g_gate), _stack_cols(g_up)
    w_down = g_down.reshape(-1, d)
    o_sb, tot, swept = _sb_fwd(qkv_sb)
    merged, h1, f = _attn_out_fwd(o_mla, o_sb, out_mla, out_sb, w_o, x, norm_ffn)
    gate, up, h2 = _ffn_fwd(f, h1, w_gate, w_up, w_down)
    loss, dh2, dh2b, dg_final = _final_loss(h2, loss_target, norm_final.reshape(1, d))

    dgate, dup, act, dh1, dh1b, dg_ffn = _ffn_bwd(dh2, dh2b, gate, up, h1, norm_ffn, w_down, w_gate, w_up)
    dw_down = _tn_matmul(act, dh2b, "dw_down")
    dw_gate = _tn_matmul(f, dgate, "dw_gate")
    dw_up = _tn_matmul(f, dup, "dw_up")
    do_mla, do_sb, dg_mla, dg_sb = _attn_out_bwd(dh1b, w_o, o_mla, o_sb, out_mla, out_sb)
    dw_o = _tn_matmul(merged, dh1b, "dw_o")
    dq_sb, dk_sb, dv_sb = _sb_bwd(qkv_sb, do_sb, tot, swept)
    early = [dw_o.reshape(N_DEV, -1, d), _split_cols(dw_gate), _split_cols(dw_up), dw_down.reshape(N_DEV, -1, d)]
    (dqn, dqr, dkn, dv, dkr), landed = _mla_bwd(qn, qr, kv, kr, do_mla, o_mla, lse, early)
    dx, dproj, dq, dkv, dg_q, dg_kv, dg_mix = _proj_in_bwd(
        dqn, dqr, dkn, dv, dkr, dq_sb, dk_sb, dv_sb, cq, ckv, x, dh1, cos, sin_a, sin_b,
        q_norm, kv_norm, norm_mix, w_uq, w_ukv, w_a)
    dw_a = _tn_matmul(u, dproj, "dw_in")
    dw_uq = _tn_matmul(cqn, dq, "dw_uq")
    dw_ukv = _tn_matmul(ckvn, dkv, "dw_ukv")

    p_in = _split_cols(jnp.concatenate([dw_a[:, :416], dw_a[:, 512:]], axis=1))
    p_uq = jnp.concatenate([dw_uq[:, :512].reshape(Q_LORA, MLA_HEADS, MLA_NOPE),
                            dw_uq[:, 512:].reshape(Q_LORA, MLA_HEADS, MLA_ROPE)], axis=2).transpose(1, 0, 2)
    p_ukv = jnp.concatenate([dw_ukv[:, :512].reshape(KV_LORA, MLA_HEADS, MLA_NOPE),
                             dw_ukv[:, 512:].reshape(KV_LORA, MLA_HEADS, HEAD_DIM)], axis=2).transpose(1, 0, 2)
    late = [p_in, p_uq, p_ukv]
    gain_grads = [dg_mix, dg_q, dg_kv, dg_mla, dg_sb, dg_ffn, dg_final]
    return loss, dx, list(landed), late, gain_grads


def kernel(x, positions, norm_mix, w_in, q_latent_norm, w_uq, kv_latent_norm, w_ukv, out_norm_mla, out_norm_sb, w_o, norm_ffn, w_gate, w_up, w_down, norm_final, loss_target, m_norm_mix, m_w_in, m_q_latent_norm, m_w_uq, m_kv_latent_norm, m_w_ukv, m_out_norm_mla, m_out_norm_sb, m_w_o, m_norm_ffn, m_w_gate, m_w_up, m_w_down, m_norm_final, v_norm_mix, v_w_in, v_q_latent_norm, v_w_uq, v_kv_latent_norm, v_w_ukv, v_out_norm_mla, v_out_norm_sb, v_w_o, v_norm_ffn, v_w_gate, v_w_up, v_w_down, v_norm_final):
    mats = [w_in, w_uq, w_ukv, w_o, w_gate, w_up, w_down]
    mat_m = [m_w_in, m_w_uq, m_w_ukv, m_w_o, m_w_gate, m_w_up, m_w_down]
    mat_v = [v_w_in, v_w_uq, v_w_ukv, v_w_o, v_w_gate, v_w_up, v_w_down]
    mat_names = ["w_in", "w_uq", "w_ukv", "w_o", "w_gate", "w_up", "w_down"]
    gains = [norm_mix, q_latent_norm, kv_latent_norm, out_norm_mla, out_norm_sb, norm_ffn, norm_final]
    gain_m = [m_norm_mix, m_q_latent_norm, m_kv_latent_norm, m_out_norm_mla, m_out_norm_sb, m_norm_ffn, m_norm_final]
    gain_v = [v_norm_mix, v_q_latent_norm, v_kv_latent_norm, v_out_norm_mla, v_out_norm_sb, v_norm_ffn, v_norm_final]

    shards = [w[0].astype(BF16) for w in mats]
    g_in, g_uq, g_ukv = _all_gather(shards[:3], "weight_all_gather")

    gains2d = [g.reshape(1, -1) for g in gains]
    loss_part, dx, landed, late, gain_grads = _local_step(
        x[0], positions[0], loss_target[0], gains2d, g_in, g_uq, g_ukv, shards[3:])

    sizes = [g.size for g in gains]
    used = sum(sizes) + LANES
    rows = -(-used // (8 * LANES)) * 8

    def pack(vals, tail):
        flat = jnp.concatenate([v.reshape(-1) for v in vals] + [tail])
        return jnp.pad(flat, (0, rows * LANES - flat.size)).reshape(rows, LANES)

    (small,), scattered = _exchange([pack(gain_grads, loss_part.reshape(-1))], late, "grad_exchange")

    mat_out = [_adamw(sl, w[0], m[0], v[0], "adamw_" + nm)
               for sl, w, m, v, nm in zip(list(scattered) + landed, mats, mat_m, mat_v, mat_names)]
    zeros_tail = jnp.zeros((LANES,), F32)
    g_s, d_s, m_s, v_s = _adamw(small, pack(gains, zeros_tail), pack(gain_m, zeros_tail), pack(gain_v, zeros_tail), "adamw_gains")

    def unpack(packed):
        flat = packed.reshape(-1)
        outs, off = [], 0
        for g, n in zip(gains, sizes):
            outs.append(flat[off:off + n].reshape(g.shape))
            off += n
        return outs

    loss = g_s.reshape(-1)[sum(sizes)]

    order = ["norm_mix", "w_in", "q_latent_norm", "w_uq", "kv_latent_norm", "w_ukv", "out_norm_mla", "out_norm_sb",
             "w_o", "norm_ffn", "w_gate", "w_up", "w_down", "norm_final"]
    gain_names = ["norm_mix", "q_latent_norm", "kv_latent_norm", "out_norm_mla", "out_norm_sb", "norm_ffn", "norm_final"]
    result = [loss, dx[None]]
    for kind in range(4):
        small_parts = dict(zip(gain_names, unpack([g_s, d_s, m_s, v_s][kind])))
        mat_parts = {nm: out[kind][None] for nm, out in zip(mat_names, mat_out)}
        result += [small_parts[nm] if nm in small_parts else mat_parts[nm] for nm in order]
    return tuple(result)
```

```python
import math

import jax
import jax.numpy as jnp
from jax import lax
from jax.experimental import pallas as pl
from jax.experimental.pallas import tpu as pltpu

F32 = jnp.float32
BF16 = jnp.bfloat16
MESH = pl.DeviceIdType.MESH

EPS = 1e-6
ROPE_THETA = 10000.0
MLA_HEADS = 8
MLA_NOPE = 64
MLA_ROPE = 32
SB_HEADS = 8
HEAD_DIM = 64
Q_LORA = 256
KV_LORA = 128
MLA_SCALE = 1.0 / math.sqrt(MLA_NOPE + MLA_ROPE)
SB_SCALE = 1.0 / math.sqrt(HEAD_DIM)
LOG2E = math.log2(math.e)
SB_DEAD = -160.0
N_DEV = 8

ADAM_LR = 0.001
ADAM_B1 = 0.9
ADAM_B2 = 0.999
ADAM_EPS = 1e-08
ADAM_WD = 0.01
ADAM_STEP = 10

LANES = 128
ATT_TILE = 512
SB_TILE = 512
TRI = 256
ROW_TILE = 512
FFN_BWD_ROW_TILE = 256
PROJ_BWD_ROW_TILE = 256
TN_BLOCK = 256
TN_RESIDENT_BYTES = 16 * 1024 * 1024
VMEM_LIMIT = 56 * 1024 * 1024
NEG = -1e30


def _cparams(*sem):
    return pltpu.CompilerParams(dimension_semantics=sem, vmem_limit_bytes=VMEM_LIMIT)


def _dot(a, b):
    return jnp.dot(a, b, preferred_element_type=F32)


def _dot_nt(a, b):
    return lax.dot_general(a, b, (((1,), (1,)), ((), ())), preferred_element_type=F32)


def _dot_tn(a, b):
    return lax.dot_general(a, b, (((0,), (0,)), ((), ())), preferred_element_type=F32)


def _rms(x, g):
    r = lax.rsqrt(jnp.mean(x * x, axis=-1, keepdims=True) + EPS)
    return x * r * g


def _rms_bwd(x, g, dy):
    r = lax.rsqrt(jnp.mean(x * x, axis=-1, keepdims=True) + EPS)
    n = x * r
    dn = dy * g
    dx = r * (dn - n * jnp.mean(dn * n, axis=-1, keepdims=True))
    return dx, jnp.sum(dy * n, axis=0, keepdims=True)


def _rope(x, cos, sin_a, sin_b):
    return x * cos + pltpu.roll(x, 112, 1) * sin_a + pltpu.roll(x, 16, 1) * sin_b


def _rope_t(g, cos, sin_a, sin_b):
    return g * cos + pltpu.roll(g * sin_a, 16, 1) + pltpu.roll(g * sin_b, 112, 1)


def _row_spec(tm, width):
    return pl.BlockSpec((tm, width), lambda r: (r, 0))


def _full_spec(shape):
    return pl.BlockSpec(shape, lambda *_: (0,) * len(shape))


def _accumulate(ref, val, first):
    @pl.when(first)
    def _():
        ref[...] = val

    @pl.when(jnp.logical_not(first))
    def _():
        ref[...] += val


def _proj_in_fwd(x, g_mix, w_a, g_q, w_uq, g_kv, w_ukv, cos, sin_a, sin_b):
    s, d = x.shape
    tm = min(ROW_TILE, s)

    def body(x_ref, gm_ref, wa_ref, gq_ref, wuq_ref, gkv_ref, wukv_ref, cos_ref, sa_ref, sb_ref,
             u_ref, cq_ref, ckv_ref, cqn_ref, ckvn_ref, qn_ref, qr_ref, kv_ref, kr_ref, sbq_ref):
        u = _rms(x_ref[...], gm_ref[...]).astype(BF16)
        u_ref[...] = u
        cq = _dot(u, wa_ref[:, 0:256])
        ckv = _dot(u, wa_ref[:, 256:384])
        kr = _dot(u, wa_ref[:, 384:512])
        cq_ref[...] = cq
        ckv_ref[...] = ckv
        cqn = _rms(cq, gq_ref[...]).astype(BF16)
        ckvn = _rms(ckv, gkv_ref[...]).astype(BF16)
        cqn_ref[...] = cqn
        ckvn_ref[...] = ckvn
        cos_t, sa_t, sb_t = cos_ref[...], sa_ref[...], sb_ref[...]
        qn_ref[...] = (_dot(cqn, wuq_ref[:, 0:512]) * MLA_SCALE).astype(BF16)
        for half in range(2):
            lo = 512 + half * LANES
            qr = _dot(cqn, wuq_ref[:, lo:lo + LANES])
            qr_ref[:, half * LANES:(half + 1) * LANES] = (_rope(qr, cos_t, sa_t, sb_t) * MLA_SCALE).astype(BF16)
        kv_ref[...] = _dot(ckvn, wukv_ref[...]).astype(BF16)
        krt = kr + pltpu.roll(kr, 32, 1) + pltpu.roll(kr, 64, 1) + pltpu.roll(kr, 96, 1)
        kr_ref[...] = _rope(krt, cos_t, sa_t, sb_t).astype(BF16)
        sbq_ref[:, 0:512] = (_dot(u, wa_ref[:, 512:1024]) * (SB_SCALE * LOG2E)).astype(BF16)
        sbq_ref[:, 512:1536] = _dot(u, wa_ref[:, 1024:2048]).astype(BF16)

    outs = [
        jax.ShapeDtypeStruct((s, d), BF16),
        jax.ShapeDtypeStruct((s, 256), F32),
        jax.ShapeDtypeStruct((s, 128), F32),
        jax.ShapeDtypeStruct((s, 256), BF16),
        jax.ShapeDtypeStruct((s, 128), BF16),
        jax.ShapeDtypeStruct((s, 512), BF16),
        jax.ShapeDtypeStruct((s, 256), BF16),
        jax.ShapeDtypeStruct((s, 1024), BF16),
        jax.ShapeDtypeStruct((s, 128), BF16),
        jax.ShapeDtypeStruct((s, 1536), BF16),
    ]
    return pl.pallas_call(
        body, name="proj_in_fwd", grid=(s // tm,), out_shape=outs,
        in_specs=[_row_spec(tm, d), _full_spec(g_mix.shape), _full_spec(w_a.shape), _full_spec(g_q.shape),
                  _full_spec(w_uq.shape), _full_spec(g_kv.shape), _full_spec(w_ukv.shape),
                  _row_spec(tm, LANES), _row_spec(tm, LANES), _row_spec(tm, LANES)],
        out_specs=[_row_spec(tm, o.shape[1]) for o in outs],
        compiler_params=_cparams("arbitrary"),
    )(x, g_mix, w_a, g_q, w_uq, g_kv, w_ukv, cos, sin_a, sin_b)


def _attn_out_fwd(o_mla, o_sb, g_mla, g_sb, w_o, x, g_ffn):
    s, d = x.shape
    tm = min(ROW_TILE, s)

    def body(oa_ref, ob_ref, ga_ref, gb_ref, wo_ref, x_ref, gf_ref, merged_ref, h1_ref, f_ref):
        na = _rms(oa_ref[...], ga_ref[...]).astype(BF16)
        nb = _rms(ob_ref[...], gb_ref[...]).astype(BF16)
        merged_ref[:, 0:512] = na
        merged_ref[:, 512:1024] = nb
        h1 = x_ref[...] + _dot(na, wo_ref[0:512, :]) + _dot(nb, wo_ref[512:1024, :])
        h1_ref[...] = h1
        f_ref[...] = _rms(h1, gf_ref[...]).astype(BF16)

    outs = [jax.ShapeDtypeStruct((s, d), BF16), jax.ShapeDtypeStruct((s, d), F32), jax.ShapeDtypeStruct((s, d), BF16)]
    return pl.pallas_call(
        body, name="attn_out_fwd", grid=(s // tm,), out_shape=outs,
        in_specs=[_row_spec(tm, 512), _row_spec(tm, 512), _full_spec(g_mla.shape), _full_spec(g_sb.shape),
                  _full_spec(w_o.shape), _row_spec(tm, d), _full_spec(g_ffn.shape)],
        out_specs=[_row_spec(tm, d)] * 3,
        compiler_params=_cparams("arbitrary"),
    )(o_mla, o_sb, g_mla, g_sb, w_o, x, g_ffn)


def _ffn_tile(d_ff):
    return d_ff // 2 if (d_ff // 2) % LANES == 0 else d_ff


def _ffn_fwd(f, h1, w_gate, w_up, w_down):
    s, d = h1.shape
    d_ff = w_gate.shape[1]
    tm = min(ROW_TILE, s)
    tf = _ffn_tile(d_ff)

    def body(f_ref, h1_ref, wg_ref, wu_ref, wd_ref, gate_ref, up_ref, act_ref, h2_ref):
        j = pl.program_id(1)
        fb = f_ref[...]
        gate = _dot(fb, wg_ref[...])
        up = _dot(fb, wu_ref[...])
        gate_ref[...] = gate.astype(BF16)
        up_ref[...] = up.astype(BF16)
        act = (gate * jax.nn.sigmoid(gate) * up).astype(BF16)
        act_ref[...] = act
        part = _dot(act, wd_ref[...])

        @pl.when(j == 0)
        def _():
            h2_ref[...] = h1_ref[...] + part

        @pl.when(j != 0)
        def _():
            h2_ref[...] += part

    outs = [jax.ShapeDtypeStruct((s, d_ff), BF16)] * 3 + [jax.ShapeDtypeStruct((s, d), F32)]
    return pl.pallas_call(
        body, name="ffn_fwd", grid=(s // tm, d_ff // tf), out_shape=outs,
        in_specs=[pl.BlockSpec((tm, d), lambda r, j: (r, 0)), pl.BlockSpec((tm, d), lambda r, j: (r, 0)),
                  pl.BlockSpec((d, tf), lambda r, j: (0, j)), pl.BlockSpec((d, tf), lambda r, j: (0, j)),
                  pl.BlockSpec((tf, d), lambda r, j: (j, 0))],
        out_specs=[pl.BlockSpec((tm, tf), lambda r, j: (r, j))] * 3 + [pl.BlockSpec((tm, d), lambda r, j: (r, 0))],
        compiler_params=_cparams("arbitrary", "arbitrary"),
    )(f, h1, w_gate, w_up, w_down)


def _final_loss(h2, target, g_final):
    s, d = h2.shape
    tm = min(ROW_TILE, s)

    def body(h2_ref, t_ref, g_ref, loss_ref, dh2_ref, dh2b_ref, dg_ref):
        first = pl.program_id(0) == 0
        h2v = h2_ref[...]
        g = g_ref[...]
        diff = _rms(h2v, g) - t_ref[...]
        part = 0.5 * jnp.sum(jnp.mean(diff * diff, axis=-1, keepdims=True), axis=0, keepdims=True)
        _accumulate(loss_ref, jnp.broadcast_to(part, loss_ref.shape), first)
        dx, dg = _rms_bwd(h2v, g, diff * (1.0 / d))
        dh2_ref[...] = dx
        dh2b_ref[...] = dx.astype(BF16)
        _accumulate(dg_ref, dg, first)

    outs = [jax.ShapeDtypeStruct((1, LANES), F32), jax.ShapeDtypeStruct((s, d), F32), jax.ShapeDtypeStruct((s, d), BF16),
            jax.ShapeDtypeStruct((1, d), F32)]
    return pl.pallas_call(
        body, name="final_loss", grid=(s // tm,), out_shape=outs,
        in_specs=[_row_spec(tm, d), _row_spec(tm, d), _full_spec((1, d))],
        out_specs=[_full_spec((1, LANES)), _row_spec(tm, d), _row_spec(tm, d), _full_spec((1, d))],
        compiler_params=_cparams("arbitrary"),
    )(h2, target, g_final)


def _tile_iotas(t):
    return lax.broadcasted_iota(jnp.int32, (t, t), 0), lax.broadcasted_iota(jnp.int32, (t, t), 1)


def _mla_fwd(qn, qr, kv, kr, shards):
    s = qn.shape[0]
    t = min(ATT_TILE, s)
    pairs = MLA_HEADS // 2
    nq = s // t
    n = len(shards)

    def body(*refs):
        qn_ref, qr_ref, kn_ref, v_ref, kr_ref = refs[:5]
        o_ref, lse_ref = refs[5 + n:7 + n]
        qcat_ref, m_ref, l_ref, acc_ref = refs[7 + 2 * n:11 + 2 * n]
        hp, i = pl.program_id(0), pl.program_id(1)
        ride = _Exchange(True, refs[5:5 + n], refs[7 + n:7 + 2 * n], *refs[11 + 2 * n:])

        @pl.when((hp == 0) & (i == 0))
        def _():
            ride.start()

        lane = lax.broadcasted_iota(jnp.int32, (1, LANES), 1)
        row, col = _tile_iotas(t)
        causal = col <= row
        q_pair, q_quad = qn_ref[...], qr_ref[...]
        zero = jnp.zeros_like(q_pair)
        for hh in range(2):
            in_head = (lane // HEAD_DIM) == hh
            in_rope = (lane // MLA_ROPE) == (hp % 2) * 2 + hh
            qcat_ref[hh * t:(hh + 1) * t, 0:LANES] = jnp.where(in_head, q_pair, zero)
            qcat_ref[hh * t:(hh + 1) * t, LANES:2 * LANES] = jnp.where(in_rope, q_quad, zero)
        m_ref[...] = jnp.full_like(m_ref, NEG)
        l_ref[...] = jnp.zeros_like(l_ref)
        acc_ref[...] = jnp.zeros_like(acc_ref)

        def tile(j, width, masked):
            rows = pl.ds(pl.multiple_of(j * t, t), width * t)
            kcat = jnp.concatenate([kn_ref[rows, :], kr_ref[rows, :]], axis=1)
            v_ones = jnp.concatenate([v_ref[rows, :], jnp.ones((width * t, LANES), BF16)], axis=1)
            scores = [_dot_nt(qcat_ref[hh * t:(hh + 1) * t, :], kcat) for hh in range(2)]
            for hh in range(2):
                half = slice(hh * t, (hh + 1) * t)
                sc = jnp.where(causal, scores[hh], NEG) if masked else scores[hh]
                m = m_ref[half, :]
                m_new = jnp.maximum(m, jnp.max(sc, axis=-1, keepdims=True))
                alpha = jnp.exp(m - m_new)
                p = jnp.exp(sc - jnp.concatenate([m_new] * (width * t // LANES), axis=1))
                pv = _dot(p.astype(BF16), v_ones)
                l_ref[half, :] = alpha * l_ref[half, :] + pv[:, LANES:]
                acc_ref[half, :] = alpha * acc_ref[half, :] + pv[:, :LANES]
                m_ref[half, :] = m_new

        tile(i, 1, True)

        def step(n, carry):
            tile(4 * n, 4, False)
            return carry

        lax.fori_loop(0, i // 4, step, 0)

        @pl.when(i % 4 >= 2)
        def _():
            tile((i // 4) * 4, 2, False)

        @pl.when(i % 2 == 1)
        def _():
            tile(i - 1, 1, False)

        first = (lane // HEAD_DIM) == 0
        o = acc_ref[...] / l_ref[...]
        lse = m_ref[...] + jnp.log(l_ref[...])
        o_ref[...] = jnp.where(first, o[0:t], o[t:2 * t])
        lse_ref[...] = jnp.where(first, lse[0:t], lse[t:2 * t])

        @pl.when((hp == pairs - 1) & (i == nq - 1))
        def _():
            ride.finish()

    gathered_shapes, sems = _exchange_shapes(True, shards)
    outs = [jax.ShapeDtypeStruct((s, 512), F32), jax.ShapeDtypeStruct((pairs, s, LANES), F32)] + gathered_shapes
    res = pl.pallas_call(
        body, name="mla_fwd", grid=(pairs, nq), out_shape=outs,
        in_specs=[pl.BlockSpec((t, LANES), lambda hp, i: (i, hp)), pl.BlockSpec((t, LANES), lambda hp, i: (i, hp // 2)),
                  pl.BlockSpec((s, LANES), lambda hp, i: (0, hp)), pl.BlockSpec((s, LANES), lambda hp, i: (0, 4 + hp)),
                  pl.BlockSpec((s, LANES), lambda hp, i: (0, 0))] + [ANY] * n,
        out_specs=[pl.BlockSpec((t, LANES), lambda hp, i: (i, hp)), pl.BlockSpec((None, t, LANES), lambda hp, i: (hp, i, 0))]
        + [ANY] * n,
        scratch_shapes=[pltpu.VMEM((2 * t, 2 * LANES), BF16), pltpu.VMEM((2 * t, LANES), F32), pltpu.VMEM((2 * t, LANES), F32),
                        pltpu.VMEM((2 * t, LANES), F32)] + sems,
        compiler_params=_cparams("arbitrary", "arbitrary"),
    )(qn, qr, kv, kv, kr, *shards)
    return res[0], res[1], res[2:]


HEADS = (0, 1)


def _sb_logs(z2, strict, masked):
    log_b = jnp.minimum(z2, 0.0) - jnp.log2(1.0 + jnp.exp2(-jnp.abs(z2)))
    log_1m = log_b - z2
    if masked:
        log_1m = jnp.where(strict, log_1m, 0.0)
    return log_1m, log_b


def _block_totals(x):
    t, w = x.shape
    nb = max(w // TRI, 1)
    bw = w // nb
    blocks = [x[:, b * bw:(b + 1) * bw] for b in range(nb)]
    totals = [jnp.broadcast_to(jnp.sum(blk, axis=-1, keepdims=True), (t, LANES)) for blk in blocks]
    whole = totals[0]
    for tot in totals[1:]:
        whole = whole + tot
    return blocks, totals, whole


def _running_sums(blocks, totals, tri, carry, suffix):
    nb = len(blocks)
    reps = blocks[0].shape[1] // LANES
    outs = [None] * nb
    run = carry
    for b in (range(nb - 1, -1, -1) if suffix else range(nb)):
        outs[b] = _dot(blocks[b].astype(BF16), tri) + jnp.concatenate([run] * reps, axis=1)
        run = run + totals[b]
    return outs[0] if nb == 1 else jnp.concatenate(outs, axis=1)


def _tri(t, rel):
    n = min(TRI, t)
    row, col = _tile_iotas(n)
    return rel(row, col).astype(BF16)


def _sweep_width(t):
    return t // 2 if t // 2 >= TRI else t


def _sb_fwd(qkv):
    s = qkv.shape[0]
    t = min(SB_TILE, s)
    sw = _sweep_width(t)
    pairs = SB_HEADS // 2

    def body(q_ref, k_ref, v_ref, o_ref, tot_ref, cnt_ref, qm_ref, right_ref, acc_ref):
        i = pl.program_id(1)
        lane = lax.broadcasted_iota(jnp.int32, (1, LANES), 1)
        row, col = _tile_iotas(t)
        strict = col < row
        t_suffix = _tri(t, lambda r, c: r > c)
        q_pair = q_ref[...]
        for hh in range(2):
            qm_ref[hh] = jnp.where((lane // HEAD_DIM) == hh, q_pair, jnp.zeros_like(q_pair))
        right_ref[...] = jnp.zeros_like(right_ref)
        acc_ref[...] = jnp.zeros_like(acc_ref)

        def tile(start, width, masked):
            rows = pl.ds(pl.multiple_of(start, width), width)
            k, v = k_ref[rows, :], v_ref[rows, :]
            for hh in HEADS:
                log_1m, log_b = _sb_logs(_dot_nt(qm_ref[hh], k), strict, masked)
                blocks, totals, whole = _block_totals(log_1m)
                a = jnp.exp2(log_b + _running_sums(blocks, totals, t_suffix, right_ref[hh], True))
                if masked:
                    a = jnp.where(strict, a, 0.0)
                right_ref[hh] += whole
                acc_ref[hh] += _dot(a.astype(BF16), v)

        tile(i * t, t, True)

        def alive(n):
            return (n < i * (t // sw)) & (jnp.max(right_ref[...]) > SB_DEAD)

        def step(n):
            tile((i * (t // sw) - 1 - n) * sw, sw, False)
            return n + 1

        swept = lax.while_loop(alive, step, jnp.int32(0))
        cnt_ref[...] = jnp.full(cnt_ref.shape, swept.astype(F32))
        first = (lane // HEAD_DIM) == 0
        o_ref[...] = jnp.where(first, acc_ref[0], acc_ref[1])
        tot_ref[...] = jnp.where(first, right_ref[0], right_ref[1])

    outs = [jax.ShapeDtypeStruct((s, 512), F32), jax.ShapeDtypeStruct((pairs, s, LANES), F32),
            jax.ShapeDtypeStruct((pairs, s // t, 8, LANES), F32)]
    return pl.pallas_call(
        body, name="sb_fwd", grid=(pairs, s // t), out_shape=outs,
        in_specs=[pl.BlockSpec((t, LANES), lambda hp, i: (i, hp)), pl.BlockSpec((s, LANES), lambda hp, i: (0, 4 + hp)),
                  pl.BlockSpec((s, LANES), lambda hp, i: (0, 8 + hp))],
        out_specs=[pl.BlockSpec((t, LANES), lambda hp, i: (i, hp)), pl.BlockSpec((None, t, LANES), lambda hp, i: (hp, i, 0)),
                   pl.BlockSpec((None, None, 8, LANES), lambda hp, i: (hp, i, 0, 0))],
        scratch_shapes=[pltpu.VMEM((2, t, LANES), BF16), pltpu.VMEM((2, t, LANES), F32), pltpu.VMEM((2, t, LANES), F32)],
        compiler_params=_cparams("arbitrary", "arbitrary"),
    )(qkv, qkv, qkv)


def _sb_bwd(qkv, do, tot, cnt):
    s = qkv.shape[0]
    t = min(SB_TILE, s)
    sw = _sweep_width(t)
    pairs = SB_HEADS // 2

    def body(q_ref, k_ref, v_ref, do_ref, tot_ref, cnt_ref, dq_ref, dk_ref, dv_ref,
             qm_ref, dob_ref, total_s, left_l, left_g, dq_s):
        i = pl.program_id(1)

        @pl.when(i == 0)
        def _():
            dk_ref[...] = jnp.zeros_like(dk_ref)
            dv_ref[...] = jnp.zeros_like(dv_ref)

        lane = lax.broadcasted_iota(jnp.int32, (1, LANES), 1)
        row, col = _tile_iotas(t)
        strict = col < row
        t_suffix = _tri(t, lambda r, c: r > c)
        t_excl = _tri(t, lambda r, c: r < c)
        q_pair, do_pair, tot_pair = q_ref[...], do_ref[...], tot_ref[...]
        for hh in range(2):
            in_head = (lane // HEAD_DIM) == hh
            qm_ref[hh] = jnp.where(in_head, q_pair, jnp.zeros_like(q_pair))
            dob_ref[hh] = jnp.where(in_head, do_pair, 0.0).astype(BF16)
            total_s[hh] = jnp.broadcast_to(
                jnp.sum(jnp.where(lane == hh * HEAD_DIM, tot_pair, 0.0), axis=-1, keepdims=True), (t, LANES))
        left_l[...] = jnp.zeros_like(left_l)
        left_g[...] = jnp.zeros_like(left_g)
        dq_s[...] = jnp.zeros_like(dq_s)

        def tile(start, width, masked):
            rows = pl.ds(pl.multiple_of(start, width), width)
            k, v = k_ref[rows, :], v_ref[rows, :]
            z2 = [_dot_nt(qm_ref[hh], k) for hh in HEADS]
            d_a = [_dot_nt(dob_ref[hh], v) for hh in HEADS]
            for hh in HEADS:
                qm, dob = qm_ref[hh], dob_ref[hh]
                log_1m, log_b = _sb_logs(z2[hh], strict, masked)
                blocks, totals, whole = _block_totals(log_1m)
                done = left_l[hh] + whole
                left_l[hh] = done
                a = jnp.exp2(log_b + _running_sums(blocks, totals, t_suffix, total_s[hh] - done, True))
                if masked:
                    a = jnp.where(strict, a, 0.0)
                g = a * d_a[hh]
                blocks, totals, whole = _block_totals(g)
                before = _running_sums(blocks, totals, t_excl, left_g[hh], False)
                left_g[hh] += whole
                dz = g - jnp.exp2(log_b) * (g + before)
                if masked:
                    dz = jnp.where(strict, dz, 0.0)
                dzb = dz.astype(BF16)
                dq_s[hh] += _dot(dzb, k)
                dk_ref[rows, :] += _dot_tn(dzb, qm)
                dv_ref[rows, :] += _dot_tn(a.astype(BF16), dob)

        def step(h, carry):
            tile(h * sw, sw, False)
            return carry

        swept = jnp.max(cnt_ref[...]).astype(jnp.int32)
        lax.fori_loop(i * (t // sw) - swept, i * (t // sw), step, 0)
        tile(i * t, t, True)
        dq_ref[...] = jnp.where((lane // HEAD_DIM) == 0, dq_s[0], dq_s[1])

        @pl.when(i == s // t - 1)
        def _():
            dk_ref[...] *= 1.0 / LOG2E

    outs = [jax.ShapeDtypeStruct((s, 512), F32)] * 3
    return pl.pallas_call(
        body, name="sb_bwd", grid=(pairs, s // t), out_shape=outs,
        in_specs=[pl.BlockSpec((t, LANES), lambda hp, i: (i, hp)), pl.BlockSpec((s, LANES), lambda hp, i: (0, 4 + hp)),
                  pl.BlockSpec((s, LANES), lambda hp, i: (0, 8 + hp)), pl.BlockSpec((t, LANES), lambda hp, i: (i, hp)),
                  pl.BlockSpec((None, t, LANES), lambda hp, i: (hp, i, 0)),
                  pl.BlockSpec((None, None, 8, LANES), lambda hp, i: (hp, i, 0, 0))],
        out_specs=[pl.BlockSpec((t, LANES), lambda hp, i: (i, hp)), pl.BlockSpec((s, LANES), lambda hp, i: (0, hp)),
                   pl.BlockSpec((s, LANES), lambda hp, i: (0, hp))],
        scratch_shapes=[pltpu.VMEM((2, t, LANES), BF16), pltpu.VMEM((2, t, LANES), BF16)]
        + [pltpu.VMEM((2, t, LANES), F32)] * 4,
        compiler_params=_cparams("arbitrary", "arbitrary"),
    )(qkv, qkv, qkv, do, tot, cnt)


def _mla_bwd(qn, qr, kv, kr, do, o, lse, parts):
    s = qn.shape[0]
    t = min(ATT_TILE, s)
    pairs = MLA_HEADS // 2
    nq = s // t
    n = len(parts)

    def body(*refs):
        qn_ref, qr_ref, kn_ref, v_ref, kr_ref, do_ref, o_ref, lse_ref = refs[:8]
        dqn_ref, dqr_ref, dkn_ref, dv_ref, dkr_ref = refs[8 + n:13 + n]
        qcat_ref, dob_ref, lse_s, delta_s, dq_s = refs[13 + 2 * n:18 + 2 * n]
        hp, i = pl.program_id(0), pl.program_id(1)
        ride = _Exchange(False, refs[8:8 + n], refs[13 + n:13 + 2 * n], *refs[18 + 2 * n:])

        @pl.when((hp == 0) & (i == 0))
        def _():
            ride.start()

        @pl.when(i == 0)
        def _():
            dkn_ref[...] = jnp.zeros_like(dkn_ref)
            dv_ref[...] = jnp.zeros_like(dv_ref)
            dkr_ref[...] = jnp.zeros_like(dkr_ref)

        lane = lax.broadcasted_iota(jnp.int32, (1, LANES), 1)
        row, col = _tile_iotas(t)
        causal = col <= row
        q_pair, q_quad, do_pair, lse_pair = qn_ref[...], qr_ref[...], do_ref[...], lse_ref[...]
        do_o = do_pair * o_ref[...]
        zero = jnp.zeros_like(q_pair)
        ropes = []
        for hh in range(2):
            in_head = (lane // HEAD_DIM) == hh
            in_rope = (lane // MLA_ROPE) == (hp % 2) * 2 + hh
            ropes.append(in_rope)
            qcat_ref[hh, :, 0:LANES] = jnp.where(in_head, q_pair, zero)
            qcat_ref[hh, :, LANES:2 * LANES] = jnp.where(in_rope, q_quad, zero)
            dob_ref[hh] = jnp.where(in_head, do_pair, 0.0).astype(BF16)
            delta_s[hh] = jnp.broadcast_to(jnp.sum(jnp.where(in_head, do_o, 0.0), axis=-1, keepdims=True), (t, LANES))
            lse_s[hh] = jnp.broadcast_to(
                jnp.sum(jnp.where(lane == hh * HEAD_DIM, lse_pair, 0.0), axis=-1, keepdims=True), (t, LANES))
        dq_s[...] = jnp.zeros_like(dq_s)
        reps = t // LANES

        def tile(j, width, masked):
            rows = pl.ds(pl.multiple_of(j * t, t), width * t)
            kcat = jnp.concatenate([kn_ref[rows, :], kr_ref[rows, :]], axis=1)
            v = v_ref[rows, :]
            sc = [_dot_nt(qcat_ref[hh], kcat) for hh in HEADS]
            dp = [_dot_nt(dob_ref[hh], v) for hh in HEADS]
            p = [jnp.exp(sc[hh] - jnp.concatenate([lse_s[hh]] * (width * reps), axis=1)) for hh in HEADS]
            if masked:
                p = [jnp.where(causal, p[hh], 0.0) for hh in HEADS]
            ds = [(p[hh] * (dp[hh] - jnp.concatenate([delta_s[hh]] * (width * reps), axis=1))).astype(BF16) for hh in HEADS]
            for hh in HEADS:
                dq_s[hh] += _dot(ds[hh], kcat)
            dkcat = _dot_tn(ds[0], qcat_ref[0]) + _dot_tn(ds[1], qcat_ref[1])
            dkn_ref[rows, :] += dkcat[:, 0:LANES]
            dkr_ref[rows, :] += dkcat[:, LANES:2 * LANES]
            dv_ref[rows, :] += _dot_tn(p[0].astype(BF16), dob_ref[0]) + _dot_tn(p[1].astype(BF16), dob_ref[1])

        def step(n, carry):
            tile(4 * n, 4, False)
            return carry

        lax.fori_loop(0, i // 4, step, 0)

        @pl.when(i % 4 >= 2)
        def _():
            tile((i // 4) * 4, 2, False)

        @pl.when(i % 2 == 1)
        def _():
            tile(i - 1, 1, False)

        tile(i, 1, True)
        dqn_ref[...] =jnp.where((lane // HEAD_DIM) == 0, dq_s[0, :, 0:LANES], dq_s[1, :, 0:LANES])
        dqr_ref[...] = (jnp.where(ropes[0], dq_s[0, :, LANES:2 * LANES], 0.0)
                        + jnp.where(ropes[1], dq_s[1, :, LANES:2 * LANES], 0.0))

        @pl.when((hp == pairs - 1) & (i == nq - 1))
        def _():
            ride.finish()

    pair_block = pl.BlockSpec((t, LANES), lambda hp, i: (i, hp))
    once = pl.Buffered(1)
    landed_shapes, sems = _exchange_shapes(False, parts)
    outs = [jax.ShapeDtypeStruct((s, 512), F32), jax.ShapeDtypeStruct((pairs, s, LANES), F32),
            jax.ShapeDtypeStruct((s, 512), F32), jax.ShapeDtypeStruct((s, 512), F32),
            jax.ShapeDtypeStruct((pairs, s, LANES), F32)] + landed_shapes
    res = pl.pallas_call(
        body, name="mla_bwd", grid=(pairs, nq), out_shape=outs,
        in_specs=[pair_block, pl.BlockSpec((t, LANES), lambda hp, i: (i, hp // 2)),
                  pl.BlockSpec((s, LANES), lambda hp, i: (0, hp), pipeline_mode=once),
                  pl.BlockSpec((s, LANES), lambda hp, i: (0, 4 + hp), pipeline_mode=once),
                  pl.BlockSpec((s, LANES), lambda hp, i: (0, 0), pipeline_mode=once), pair_block, pair_block,
                  pl.BlockSpec((None, t, LANES), lambda hp, i: (hp, i, 0))] + [ANY] * n,
        out_specs=[pair_block, pl.BlockSpec((None, t, LANES), lambda hp, i: (hp, i, 0)),
                   pl.BlockSpec((s, LANES), lambda hp, i: (0, hp), pipeline_mode=once),
                   pl.BlockSpec((s, LANES), lambda hp, i: (0, hp), pipeline_mode=once),
                   pl.BlockSpec((None, s, LANES), lambda hp, i: (hp, 0, 0), pipeline_mode=once)] + [ANY] * n,
        scratch_shapes=[pltpu.VMEM((2, t, 2 * LANES), BF16), pltpu.VMEM((2, t, LANES), BF16), pltpu.VMEM((2, t, LANES), F32),
                        pltpu.VMEM((2, t, LANES), F32), pltpu.VMEM((2, t, 2 * LANES), F32)] + sems,
        compiler_params=_cparams("arbitrary", "arbitrary"),
    )(qn, qr, kv, kv, kr, do, o, lse, *parts)
    return res[:5], res[5:]


def _ffn_bwd(dh2, dh2b, gate, up, h1, g_ffn, w_down, w_gate, w_up):
    s, d = h1.shape
    d_ff = gate.shape[1]
    tm = min(FFN_BWD_ROW_TILE, s)
    tf = _ffn_tile(d_ff)

    def act_body(dh2b_ref, gate_ref, up_ref, wd_ref, dgate_ref, dup_ref):
        dact = _dot_nt(dh2b_ref[...], wd_ref[...])
        gate_v = gate_ref[...].astype(F32)
        up_v = up_ref[...].astype(F32)
        sig = jax.nn.sigmoid(gate_v)
        silu = gate_v * sig
        dup_ref[...] = (dact * silu).astype(BF16)
        dgate_ref[...] = ((dact * up_v) * (sig * (1.0 + gate_v - silu))).astype(BF16)

    ff = pl.BlockSpec((tm, tf), lambda j, r: (r, j))
    dgate, dup = pl.pallas_call(
        act_body, name="ffn_bwd_act", grid=(d_ff // tf, s // tm), out_shape=[jax.ShapeDtypeStruct((s, d_ff), BF16)] * 2,
        in_specs=[pl.BlockSpec((tm, d), lambda j, r: (r, 0)), ff, ff, pl.BlockSpec((tf, d), lambda j, r: (j, 0))],
        out_specs=[ff, ff],
        compiler_params=_cparams("arbitrary", "arbitrary"),
    )(dh2b, gate, up, w_down)

    def df_body(dgate_ref, dup_ref, dh2_ref, h1_ref, g_ref, wg_ref, wu_ref, dh1_ref, dh1b_ref, dg_ref):
        df = _dot_nt(dgate_ref[...], wg_ref[...]) + _dot_nt(dup_ref[...], wu_ref[...])
        dx, dg = _rms_bwd(h1_ref[...], g_ref[...], df)
        dh1 = dh2_ref[...] + dx
        dh1_ref[...] = dh1
        dh1b_ref[...] = dh1.astype(BF16)
        _accumulate(dg_ref, dg, pl.program_id(0) == 0)

    outs = [jax.ShapeDtypeStruct((s, d), F32), jax.ShapeDtypeStruct((s, d), BF16), jax.ShapeDtypeStruct((1, d), F32)]
    dh1, dh1b, dg = pl.pallas_call(
        df_body, name="ffn_bwd_df", grid=(s // tm,), out_shape=outs,
        in_specs=[_row_spec(tm, d_ff), _row_spec(tm, d_ff), _row_spec(tm, d), _row_spec(tm, d), _full_spec((1, d)),
                  _full_spec(w_gate.shape), _full_spec(w_up.shape)],
        out_specs=[_row_spec(tm, d), _row_spec(tm, d), _full_spec((1, d))],
        compiler_params=_cparams("arbitrary"),
    )(dgate, dup, dh2, h1, g_ffn, w_gate, w_up)
    return dgate, dup, dh1, dh1b, dg


def _tn_matmul(a, b, name):
    assert a.dtype == BF16 and b.dtype == BF16
    s, m = a.shape
    n = b.shape[1]
    if s * m * 2 <= TN_RESIDENT_BYTES:
        tm, tn = m, min(n, TN_BLOCK)
    else:
        tm, tn = TN_BLOCK, n

    def body(a_ref, b_ref, o_ref):
        o_ref[...] = _dot_tn(a_ref[...], b_ref[...]).astype(BF16)

    return pl.pallas_call(
        body, name=name, grid=(m // tm, n // tn), out_shape=jax.ShapeDtypeStruct((m, n), BF16),
        in_specs=[pl.BlockSpec((s, tm), lambda i, j: (0, i)), pl.BlockSpec((s, tn), lambda i, j: (0, j))],
        out_specs=pl.BlockSpec((tm, tn), lambda i, j: (i, j)),
        compiler_params=_cparams("arbitrary", "arbitrary"),
    )(a, b)


def _attn_out_bwd(dh1, w_o, o_mla, o_sb, g_mla, g_sb):
    s, d = dh1.shape
    tm = min(ROW_TILE, s)

    def body(dh1_ref, wo_ref, oa_ref, ob_ref, ga_ref, gb_ref, doa_ref, dob_ref, dga_ref, dgb_ref):
        first = pl.program_id(0) == 0
        dh1b = dh1_ref[...]
        dxa, dga = _rms_bwd(oa_ref[...], ga_ref[...], _dot_nt(dh1b, wo_ref[0:512, :]))
        dxb, dgb = _rms_bwd(ob_ref[...], gb_ref[...], _dot_nt(dh1b, wo_ref[512:1024, :]))
        doa_ref[...] = dxa
        dob_ref[...] = dxb
        _accumulate(dga_ref, dga, first)
        _accumulate(dgb_ref, dgb, first)

    outs = [jax.ShapeDtypeStruct((s, 512), F32)] * 2 + [jax.ShapeDtypeStruct((1, 512), F32)] * 2
    return pl.pallas_call(
        body, name="attn_out_bwd", grid=(s // tm,), out_shape=outs,
        in_specs=[_row_spec(tm, d), _full_spec(w_o.shape), _row_spec(tm, 512), _row_spec(tm, 512),
                  _full_spec((1, 512)), _full_spec((1, 512))],
        out_specs=[_row_spec(tm, 512), _row_spec(tm, 512), _full_spec((1, 512)), _full_spec((1, 512))],
        compiler_params=_cparams("arbitrary"),
    )(dh1, w_o, o_mla, o_sb, g_mla, g_sb)


def _proj_in_bwd(dqn, dqr, dkn, dv, dkr, dq_sb, dk_sb, dv_sb, cq, ckv, x, dh1, cos, sin_a, sin_b,
                 g_q, g_kv, g_mix, w_uq, w_ukv, w_a):
    s, d = x.shape
    tm = min(PROJ_BWD_ROW_TILE, s)

    def body(dqn_ref, dqr_ref, dkn_ref, dv_ref, dkr_ref, dqs_ref, dks_ref, dvs_ref, cq_ref, ckv_ref, x_ref, dh1_ref,
             cos_ref, sa_ref, sb_ref, gq_ref, gkv_ref, gm_ref, wuq_ref, wukv_ref, wa_ref,
             dx_ref, dproj_ref, dq_ref, dkv_ref, dgq_ref, dgkv_ref, dgm_ref):
        first = pl.program_id(0) == 0
        lane = lax.broadcasted_iota(jnp.int32, (1, LANES), 1)
        cos_t, sa_t, sb_t = cos_ref[...], sa_ref[...], sb_ref[...]
        dq_ref[:, 0:512] = (dqn_ref[...] * MLA_SCALE).astype(BF16)
        for half in range(2):
            quad = (dqr_ref[2 * half] + dqr_ref[2 * half + 1]) * MLA_SCALE
            dq_ref[:, 512 + half * LANES:512 + (half + 1) * LANES] = _rope_t(quad, cos_t, sa_t, sb_t).astype(BF16)
        dcq, dgq = _rms_bwd(cq_ref[...], gq_ref[...], _dot_nt(dq_ref[...], wuq_ref[...]))
        _accumulate(dgq_ref, dgq, first)
        dkv_ref[:, 0:512] = dkn_ref[...].astype(BF16)
        dkv_ref[:, 512:1024] = dv_ref[...].astype(BF16)
        dckv, dgkv = _rms_bwd(ckv_ref[...], gkv_ref[...], _dot_nt(dkv_ref[...], wukv_ref[...]))
        _accumulate(dgkv_ref, dgkv, first)
        g = _rope_t(dkr_ref[0] + dkr_ref[1] + dkr_ref[2] + dkr_ref[3], cos_t, sa_t, sb_t)
        g = g + pltpu.roll(g, 96, 1) + pltpu.roll(g, 64, 1) + pltpu.roll(g, 32, 1)
        dproj_ref[:, 0:256] = dcq.astype(BF16)
        dproj_ref[:, 256:384] = dckv.astype(BF16)
        dproj_ref[:, 384:512] = jnp.where(lane < MLA_ROPE, g, 0.0).astype(BF16)
        dproj_ref[:, 512:1024] = (dqs_ref[...] * SB_SCALE).astype(BF16)
        dproj_ref[:, 1024:1536] = dks_ref[...].astype(BF16)
        dproj_ref[:, 1536:2048] = dvs_ref[...].astype(BF16)
        dxn, dgm = _rms_bwd(x_ref[...], gm_ref[...], _dot_nt(dproj_ref[...], wa_ref[...]))
        dx_ref[...] = dh1_ref[...] + dxn
        _accumulate(dgm_ref, dgm, first)

    quad_spec = pl.BlockSpec((4, tm, LANES), lambda r: (0, r, 0))
    outs = [jax.ShapeDtypeStruct((s, d), F32), jax.ShapeDtypeStruct((s, 2048), BF16), jax.ShapeDtypeStruct((s, 768), BF16),
            jax.ShapeDtypeStruct((s, 1024), BF16), jax.ShapeDtypeStruct((1, 256), F32), jax.ShapeDtypeStruct((1, 128), F32),
            jax.ShapeDtypeStruct((1, d), F32)]
    return pl.pallas_call(
        body, name="proj_in_bwd", grid=(s // tm,), out_shape=outs,
        in_specs=[_row_spec(tm, 512), quad_spec, _row_spec(tm, 512), _row_spec(tm, 512), quad_spec,
                  _row_spec(tm, 512), _row_spec(tm, 512), _row_spec(tm, 512), _row_spec(tm, 256), _row_spec(tm, 128),
                  _row_spec(tm, d), _row_spec(tm, d), _row_spec(tm, LANES), _row_spec(tm, LANES), _row_spec(tm, LANES),
                  _full_spec((1, 256)), _full_spec((1, 128)), _full_spec((1, d)),
                  _full_spec(w_uq.shape), _full_spec(w_ukv.shape), _full_spec(w_a.shape)],
        out_specs=[_row_spec(tm, d), _row_spec(tm, 2048), _row_spec(tm, 768), _row_spec(tm, 1024),
                   _full_spec((1, 256)), _full_spec((1, 128)), _full_spec((1, d))],
        compiler_params=_cparams("arbitrary"),
    )(dqn, dqr, dkn, dv, dkr, dq_sb, dk_sb, dv_sb, cq, ckv, x, dh1, cos, sin_a, sin_b, g_q, g_kv, g_mix,
      w_uq, w_ukv, w_a)


ANY = pl.BlockSpec(memory_space=pl.ANY)


def _place():
    return lax.axis_index("x"), lax.axis_index("y"), lax.axis_index("c")


def _all_gather(shards, name):
    n = len(shards)

    def body(*refs):
        ins, outs = refs[:n], refs[n:2 * n]
        send_sems, recv_sems, local_sems = refs[2 * n:]
        x, y, c = _place()
        me, sibling = (x, y, c), (x, y, 1 - c)
        chips = [(1 - x, y), (x, 1 - y), (1 - x, 1 - y)]

        def slot(a, px, py, pc):
            return outs[a].at[4 * px + 2 * py + pc]

        def copy(a, k, block, to, src=None):
            return pltpu.make_async_remote_copy(
                src_ref=slot(a, *block) if src is None else src, dst_ref=slot(a, *block),
                send_sem=send_sems.at[a, k], recv_sem=recv_sems.at[a, k], device_id=to, device_id_type=MESH)

        mine, first, passed = [], [], []
        for a in range(n):
            own = pltpu.make_async_copy(ins[a], slot(a, *me), local_sems.at[a])
            own.start()
            mine.append(own)
            cps = [copy(a, 0, me, sibling, src=ins[a])]
            cps += [copy(a, 1 + j, me, (*chip, c), src=ins[a]) for j, chip in enumerate(chips)]
            for cp in cps:
                cp.start()
            first += cps
        for a in range(n):
            for j, chip in enumerate(chips):
                copy(a, 1 + j, (*chip, c), me).wait_recv()
                fwd = copy(a, 4 + j, (*chip, c), sibling)
                fwd.start()
                passed.append(fwd)
        for a in range(n):
            copy(a, 0, sibling, me).wait_recv()
            for j, chip in enumerate(chips):
                copy(a, 4 + j, (*chip, 1 - c), me).wait_recv()
        for cp in first + passed:
            cp.wait_send()
        for own in mine:
            own.wait()

    return pl.pallas_call(
        body, name=name,
        out_shape=[jax.ShapeDtypeStruct((N_DEV,) + v.shape, v.dtype) for v in shards],
        in_specs=[ANY] * n, out_specs=[ANY] * n,
        scratch_shapes=[pltpu.SemaphoreType.DMA((n, 7)), pltpu.SemaphoreType.DMA((n, 7)), pltpu.SemaphoreType.DMA((n,))],
    )(*shards)


class _Exchange:
    def __init__(self, gather, ins, outs, send_sems, recv_sems, local_sems):
        self.gather, self.ins, self.outs = gather, ins, outs
        self.sems = (send_sems, recv_sems, local_sems)
        x, y, c = _place()
        self.me = 4 * x + 2 * y + c
        self.peers = []
        for k in range(1, N_DEV):
            px = 1 - x if k & 4 else x
            py = 1 - y if k & 2 else y
            pc = 1 - c if k & 1 else c
            self.peers.append(((px, py, pc), 4 * px + 2 * py + pc))

    def _remote(self, a, k, landing):
        send_sems, recv_sems, _ = self.sems
        where, number = self.peers[k]
        src = self.ins[a] if self.gather else self.ins[a].at[number]
        return pltpu.make_async_remote_copy(
            src_ref=src, dst_ref=self.outs[a].at[landing], send_sem=send_sems.at[a, k], recv_sem=recv_sems.at[a, k],
            device_id=where, device_id_type=MESH)

    def _local(self, a):
        src = self.ins[a] if self.gather else self.ins[a].at[self.me]
        return pltpu.make_async_copy(src, self.outs[a].at[self.me], self.sems[2].at[a])

    def start(self):
        for a in range(len(self.ins)):
            self._local(a).start()
            for k in range(N_DEV - 1):
                self._remote(a, k, self.me).start()

    def finish(self):
        for a in range(len(self.ins)):
            for k in range(N_DEV - 1):
                self._remote(a, k, self.peers[k][1]).wait_recv()
            for k in range(N_DEV - 1):
                self._remote(a, k, self.me).wait_send()
            self._local(a).wait()


def _exchange_shapes(gather, arrays):
    out_shape = [jax.ShapeDtypeStruct(((N_DEV,) + v.shape) if gather else v.shape, v.dtype) for v in arrays]
    n = len(arrays)
    sems = [pltpu.SemaphoreType.DMA((n, N_DEV - 1)), pltpu.SemaphoreType.DMA((n, N_DEV - 1)), pltpu.SemaphoreType.DMA((n,))]
    return out_shape, sems


def _exchange(gathers, scatters, name):
    ng, ns = len(gathers), len(scatters)
    n = ng + ns

    def body(*refs):
        ins, outs, sems = refs[:n], refs[n:2 * n], refs[2 * n:]
        both = [_Exchange(True, ins[:ng], outs[:ng], *sems[:3]), _Exchange(False, ins[ng:], outs[ng:], *sems[3:])]
        for ex in both:
            ex.start()
        for ex in both:
            ex.finish()

    g_shapes, g_sems = _exchange_shapes(True, gathers)
    s_shapes, s_sems = _exchange_shapes(False, scatters)
    res = pl.pallas_call(body, name=name, out_shape=g_shapes + s_shapes, in_specs=[ANY] * n, out_specs=[ANY] * n,
                         scratch_shapes=g_sems + s_sems)(*gathers, *scatters)
    return res[:ng], res[ng:]


def _grad_row_tile(rows):
    return _largest_tile_rows(rows, 256)


def _largest_tile_rows(rows, cap):
    for cand in range(cap, 0, -8):
        if rows % cand == 0:
            return cand
    return rows


def _adamw_math(w, g, m, v):
    m_new = ADAM_B1 * m + (1.0 - ADAM_B1) * g
    v_new = ADAM_B2 * v + (1.0 - ADAM_B2) * (g * g)
    m_hat = m_new / (1.0 - ADAM_B1 ** ADAM_STEP)
    v_hat = v_new / (1.0 - ADAM_B2 ** ADAM_STEP)
    delta = -ADAM_LR * (m_hat / (jnp.sqrt(v_hat) + ADAM_EPS) + ADAM_WD * w)
    return delta, m_new, v_new


def _adamw(slots, w, m, v, name):
    k, r, cdim = slots.shape
    tr = _grad_row_tile(r)

    def body(s_ref, w_ref, m_ref, v_ref, g_ref, d_ref, mo_ref, vo_ref):
        g = s_ref[0].astype(F32)
        for q in range(1, k):
            g = g + s_ref[q].astype(F32)
        g_ref[...] = g
        d_ref[...], mo_ref[...], vo_ref[...] = _adamw_math(w_ref[...], g, m_ref[...], v_ref[...])

    blk = pl.BlockSpec((tr, cdim), lambda i: (i, 0))
    return pl.pallas_call(
        body, name=name, grid=(r // tr,), out_shape=[jax.ShapeDtypeStruct((r, cdim), F32)] * 4,
        in_specs=[pl.BlockSpec((k, tr, cdim), lambda i: (0, i, 0)), blk, blk, blk], out_specs=[blk] * 4,
        compiler_params=_cparams("arbitrary"),
    )(slots, w, m, v)


def _stack_cols(g):
    n, r, c = g.shape
    return g.transpose(1, 0, 2).reshape(r, n * c)


def _split_cols(w):
    r, nc = w.shape
    return w.reshape(r, N_DEV, nc // N_DEV).transpose(1, 0, 2)


def _rope_tables(positions):
    inv_freq = ROPE_THETA ** (-jnp.arange(0, MLA_ROPE, 2, dtype=F32) / MLA_ROPE)
    ang = positions.astype(F32).reshape(-1, 1) * inv_freq[None, :]
    cos, sin, zero = jnp.cos(ang), jnp.sin(ang), jnp.zeros_like(ang)
    reps = LANES // MLA_ROPE
    return (jnp.tile(jnp.concatenate([cos, cos], axis=1), (1, reps)),
            jnp.tile(jnp.concatenate([-sin, zero], axis=1), (1, reps)),
            jnp.tile(jnp.concatenate([zero, sin], axis=1), (1, reps)))


def _local_step(x, positions, loss_target, gains, g_in, g_uq, g_ukv, late_shards):
    norm_mix, q_norm, kv_norm, out_mla, out_sb, norm_ffn, norm_final = gains
    d = x.shape[1]
    w_in = _stack_cols(g_in)
    w_a = jnp.concatenate([w_in[:, :416], jnp.zeros((d, 96), BF16), w_in[:, 416:]], axis=1)
    w_uq = jnp.concatenate([g_uq[:, :, :MLA_NOPE].transpose(1, 0, 2).reshape(Q_LORA, -1),
                            g_uq[:, :, MLA_NOPE:].transpose(1, 0, 2).reshape(Q_LORA, -1)], axis=1)
    w_ukv = jnp.concatenate([g_ukv[:, :, :MLA_NOPE].transpose(1, 0, 2).reshape(KV_LORA, -1),
                             g_ukv[:, :, MLA_NOPE:].transpose(1, 0, 2).reshape(KV_LORA, -1)], axis=1)
    cos, sin_a, sin_b = _rope_tables(positions)

    u, cq, ckv, cqn, ckvn, qn, qr, kv, kr, qkv_sb = _proj_in_fwd(x, norm_mix, w_a, q_norm, w_uq, kv_norm, w_ukv, cos, sin_a, sin_b)
    o_mla, lse, (g_o, g_gate, g_up, g_down) = _mla_fwd(qn, qr, kv, kr, late_shards)
    w_o = g_o.reshape(-1, d)
    w_gate, w_up = _stack_cols(g_gate), _stack_cols(g_up)
    w_down = g_down.reshape(-1, d)
    o_sb, tot, swept = _sb_fwd(qkv_sb)
    merged, h1, f = _attn_out_fwd(o_mla, o_sb, out_mla, out_sb, w_o, x, norm_ffn)
    gate, up, act, h2 = _ffn_fwd(f, h1, w_gate, w_up, w_down)
    loss, dh2, dh2b, dg_final = _final_loss(h2, loss_target, norm_final.reshape(1, d))

    dgate, dup, dh1, dh1b, dg_ffn = _ffn_bwd(dh2, dh2b, gate, up, h1, norm_ffn, w_down, w_gate, w_up)
    dw_down = _tn_matmul(act, dh2b, "dw_down")
    dw_gate = _tn_matmul(f, dgate, "dw_gate")
    dw_up = _tn_matmul(f, dup, "dw_up")
    do_mla, do_sb, dg_mla, dg_sb = _attn_out_bwd(dh1b, w_o, o_mla, o_sb, out_mla, out_sb)
    dw_o = _tn_matmul(merged, dh1b, "dw_o")
    dq_sb, dk_sb, dv_sb = _sb_bwd(qkv_sb, do_sb, tot, swept)
    early = [dw_o.reshape(N_DEV, -1, d), _split_cols(dw_gate), _split_cols(dw_up), dw_down.reshape(N_DEV, -1, d)]
    (dqn, dqr, dkn, dv, dkr), landed = _mla_bwd(qn, qr, kv, kr, do_mla, o_mla, lse, early)
    dx, dproj, dq, dkv, dg_q, dg_kv, dg_mix = _proj_in_bwd(
        dqn, dqr, dkn, dv, dkr, dq_sb, dk_sb, dv_sb, cq, ckv, x, dh1, cos, sin_a, sin_b,
        q_norm, kv_norm, norm_mix, w_uq, w_ukv, w_a)
    dw_a = _tn_matmul(u, dproj, "dw_in")
    dw_uq = _tn_matmul(cqn, dq, "dw_uq")
    dw_ukv = _tn_matmul(ckvn, dkv, "dw_ukv")

    p_in = _split_cols(jnp.concatenate([dw_a[:, :416], dw_a[:, 512:]], axis=1))
    p_uq = jnp.concatenate([dw_uq[:, :512].reshape(Q_LORA, MLA_HEADS, MLA_NOPE),
                            dw_uq[:, 512:].reshape(Q_LORA, MLA_HEADS, MLA_ROPE)], axis=2).transpose(1, 0, 2)
    p_ukv = jnp.concatenate([dw_ukv[:, :512].reshape(KV_LORA, MLA_HEADS, MLA_NOPE),
                             dw_ukv[:, 512:].reshape(KV_LORA, MLA_HEADS, HEAD_DIM)], axis=2).transpose(1, 0, 2)
    late = [p_in, p_uq, p_ukv]
    gain_grads = [dg_mix, dg_q, dg_kv, dg_mla, dg_sb, dg_ffn, dg_final]
    return loss, dx, list(landed), late, gain_grads


def kernel(x, positions, norm_mix, w_in, q_latent_norm, w_uq, kv_latent_norm, w_ukv, out_norm_mla, out_norm_sb, w_o, norm_ffn, w_gate, w_up, w_down, norm_final, loss_target, m_norm_mix, m_w_in, m_q_latent_norm, m_w_uq, m_kv_latent_norm, m_w_ukv, m_out_norm_mla, m_out_norm_sb, m_w_o, m_norm_ffn, m_w_gate, m_w_up, m_w_down, m_norm_final, v_norm_mix, v_w_in, v_q_latent_norm, v_w_uq, v_kv_latent_norm, v_w_ukv, v_out_norm_mla, v_out_norm_sb, v_w_o, v_norm_ffn, v_w_gate, v_w_up, v_w_down, v_norm_final):
    mats = [w_in, w_uq, w_ukv, w_o, w_gate, w_up, w_down]
    mat_m = [m_w_in, m_w_uq, m_w_ukv, m_w_o, m_w_gate, m_w_up, m_w_down]
    mat_v = [v_w_in, v_w_uq, v_w_ukv, v_w_o, v_w_gate, v_w_up, v_w_down]
    mat_names = ["w_in", "w_uq", "w_ukv", "w_o", "w_gate", "w_up", "w_down"]
    gains = [norm_mix, q_latent_norm, kv_latent_norm, out_norm_mla, out_norm_sb, norm_ffn, norm_final]
    gain_m = [m_norm_mix, m_q_latent_norm, m_kv_latent_norm, m_out_norm_mla, m_out_norm_sb, m_norm_ffn, m_norm_final]
    gain_v = [v_norm_mix, v_q_latent_norm, v_kv_latent_norm, v_out_norm_mla, v_out_norm_sb, v_norm_ffn, v_norm_final]

    shards = [w[0].astype(BF16) for w in mats]
    g_in, g_uq, g_ukv = _all_gather(shards[:3], "weight_all_gather")

    gains2d = [g.reshape(1, -1) for g in gains]
    loss_part, dx, landed, late, gain_grads = _local_step(
        x[0], positions[0], loss_target[0], gains2d, g_in, g_uq, g_ukv, shards[3:])

    sizes = [g.size for g in gains]
    used = sum(sizes) + LANES
    rows = -(-used // (8 * LANES)) * 8

    def pack(vals, tail):
        flat = jnp.concatenate([v.reshape(-1) for v in vals] + [tail])
        return jnp.pad(flat, (0, rows * LANES - flat.size)).reshape(rows, LANES)

    (small,), scattered = _exchange([pack(gain_grads, loss_part.reshape(-1))], late, "grad_exchange")

    mat_out = [_adamw(sl, w[0], m[0], v[0], "adamw_" + nm)
               for sl, w, m, v, nm in zip(list(scattered) + landed, mats, mat_m, mat_v, mat_names)]
    zeros_tail = jnp.zeros((LANES,), F32)
    g_s, d_s, m_s, v_s = _adamw(small, pack(gains, zeros_tail), pack(gain_m, zeros_tail), pack(gain_v, zeros_tail), "adamw_gains")

    def unpack(packed):
        flat = packed.reshape(-1)
        outs, off = [], 0
        for g, n in zip(gains, sizes):
            outs.append(flat[off:off + n].reshape(g.shape))
            off += n
        return outs

    loss = g_s.reshape(-1)[sum(sizes)]

    order = ["norm_mix", "w_in", "q_latent_norm", "w_uq", "kv_latent_norm", "w_ukv", "out_norm_mla", "out_norm_sb",
             "w_o", "norm_ffn", "w_gate", "w_up", "w_down", "norm_final"]
    gain_names = ["norm_mix", "q_latent_norm", "kv_latent_norm", "out_norm_mla", "out_norm_sb", "norm_ffn", "norm_final"]
    result = [loss, dx[None]]
    for kind in range(4):
        small_parts = dict(zip(gain_names, unpack([g_s, d_s, m_s, v_s][kind])))
        mat_parts = {nm: out[kind][None] for nm, out in zip(mat_names, mat_out)}
        result += [small_parts[nm] if nm in small_parts else mat_parts[nm] for nm in order]
    return tuple(result)
```

```python
import math

import jax
import jax.numpy as jnp
from jax import lax
from jax.experimental import pallas as pl
from jax.experimental.pallas import tpu as pltpu

F32 = jnp.float32
BF16 = jnp.bfloat16
MESH = pl.DeviceIdType.MESH

EPS = 1e-6
ROPE_THETA = 10000.0
MLA_HEADS = 8
MLA_NOPE = 64
MLA_ROPE = 32
SB_HEADS = 8
HEAD_DIM = 64
Q_LORA = 256
KV_LORA = 128
MLA_SCALE = 1.0 / math.sqrt(MLA_NOPE + MLA_ROPE)
SB_SCALE = 1.0 / math.sqrt(HEAD_DIM)
LOG2E = math.log2(math.e)
SB_DEAD = -160.0
N_DEV = 8

ADAM_LR = 0.001
ADAM_B1 = 0.9
ADAM_B2 = 0.999
ADAM_EPS = 1e-08
ADAM_WD = 0.01
ADAM_STEP = 10

LANES = 128
ATT_TILE = 512
SB_TILE = 512
TRI = 256
ROW_TILE = 512
FFN_BWD_ROW_TILE = 256
PROJ_BWD_ROW_TILE = 256
TN_BLOCK = 256
TN_RESIDENT_BYTES = 16 * 1024 * 1024
VMEM_LIMIT = 56 * 1024 * 1024
NEG = -1e30


def _cparams(*sem):
    return pltpu.CompilerParams(dimension_semantics=sem, vmem_limit_bytes=VMEM_LIMIT)


def _dot(a, b):
    return jnp.dot(a, b, preferred_element_type=F32)


def _dot_nt(a, b):
    return lax.dot_general(a, b, (((1,), (1,)), ((), ())), preferred_element_type=F32)


def _dot_tn(a, b):
    return lax.dot_general(a, b, (((0,), (0,)), ((), ())), preferred_element_type=F32)


def _rms(x, g):
    r = lax.rsqrt(jnp.mean(x * x, axis=-1, keepdims=True) + EPS)
    return x * r * g


def _rms_bwd(x, g, dy):
    r = lax.rsqrt(jnp.mean(x * x, axis=-1, keepdims=True) + EPS)
    n = x * r
    dn = dy * g
    dx = r * (dn - n * jnp.mean(dn * n, axis=-1, keepdims=True))
    return dx, jnp.sum(dy * n, axis=0, keepdims=True)


def _rope(x, cos, sin_a, sin_b):
    return x * cos + pltpu.roll(x, 112, 1) * sin_a + pltpu.roll(x, 16, 1) * sin_b


def _rope_t(g, cos, sin_a, sin_b):
    return g * cos + pltpu.roll(g * sin_a, 16, 1) + pltpu.roll(g * sin_b, 112, 1)


def _row_spec(tm, width):
    return pl.BlockSpec((tm, width), lambda r: (r, 0))


def _full_spec(shape):
    return pl.BlockSpec(shape, lambda *_: (0,) * len(shape))


def _accumulate(ref, val, first):
    @pl.when(first)
    def _():
        ref[...] = val

    @pl.when(jnp.logical_not(first))
    def _():
        ref[...] += val


def _proj_in_fwd(x, g_mix, w_a, g_q, w_uq, g_kv, w_ukv, cos, sin_a, sin_b):
    s, d = x.shape
    tm = min(ROW_TILE, s)

    def body(x_ref, gm_ref, wa_ref, gq_ref, wuq_ref, gkv_ref, wukv_ref, cos_ref, sa_ref, sb_ref,
             u_ref, cq_ref, ckv_ref, cqn_ref, ckvn_ref, qn_ref, qr_ref, kv_ref, kr_ref, sbq_ref):
        u = _rms(x_ref[...], gm_ref[...]).astype(BF16)
        u_ref[...] = u
        cq = _dot(u, wa_ref[:, 0:256])
        ckv = _dot(u, wa_ref[:, 256:384])
        kr = _dot(u, wa_ref[:, 384:512])
        cq_ref[...] = cq
        ckv_ref[...] = ckv
        cqn = _rms(cq, gq_ref[...]).astype(BF16)
        ckvn = _rms(ckv, gkv_ref[...]).astype(BF16)
        cqn_ref[...] = cqn
        ckvn_ref[...] = ckvn
        cos_t, sa_t, sb_t = cos_ref[...], sa_ref[...], sb_ref[...]
        qn_ref[...] = (_dot(cqn, wuq_ref[:, 0:512]) * MLA_SCALE).astype(BF16)
        for half in range(2):
            lo = 512 + half * LANES
            qr = _dot(cqn, wuq_ref[:, lo:lo + LANES])
            qr_ref[:, half * LANES:(half + 1) * LANES] = (_rope(qr, cos_t, sa_t, sb_t) * MLA_SCALE).astype(BF16)
        kv_ref[...] = _dot(ckvn, wukv_ref[...]).astype(BF16)
        krt = kr + pltpu.roll(kr, 32, 1) + pltpu.roll(kr, 64, 1) + pltpu.roll(kr, 96, 1)
        kr_ref[...] = _rope(krt, cos_t, sa_t, sb_t).astype(BF16)
        sbq_ref[:, 0:512] = (_dot(u, wa_ref[:, 512:1024]) * (SB_SCALE * LOG2E)).astype(BF16)
        sbq_ref[:, 512:1536] = _dot(u, wa_ref[:, 1024:2048]).astype(BF16)

    outs = [
        jax.ShapeDtypeStruct((s, d), BF16),
        jax.ShapeDtypeStruct((s, 256), F32),
        jax.ShapeDtypeStruct((s, 128), F32),
        jax.ShapeDtypeStruct((s, 256), BF16),
        jax.ShapeDtypeStruct((s, 128), BF16),
        jax.ShapeDtypeStruct((s, 512), BF16),
        jax.ShapeDtypeStruct((s, 256), BF16),
        jax.ShapeDtypeStruct((s, 1024), BF16),
        jax.ShapeDtypeStruct((s, 128), BF16),
        jax.ShapeDtypeStruct((s, 1536), BF16),
    ]
    return pl.pallas_call(
        body, name="proj_in_fwd", grid=(s // tm,), out_shape=outs,
        in_specs=[_row_spec(tm, d), _full_spec(g_mix.shape), _full_spec(w_a.shape), _full_spec(g_q.shape),
                  _full_spec(w_uq.shape), _full_spec(g_kv.shape), _full_spec(w_ukv.shape),
                  _row_spec(tm, LANES), _row_spec(tm, LANES), _row_spec(tm, LANES)],
        out_specs=[_row_spec(tm, o.shape[1]) for o in outs],
        compiler_params=_cparams("arbitrary"),
    )(x, g_mix, w_a, g_q, w_uq, g_kv, w_ukv, cos, sin_a, sin_b)


def _attn_out_fwd(o_mla, o_sb, g_mla, g_sb, w_o, x, g_ffn):
    s, d = x.shape
    tm = min(ROW_TILE, s)

    def body(oa_ref, ob_ref, ga_ref, gb_ref, wo_ref, x_ref, gf_ref, merged_ref, h1_ref, f_ref):
        na = _rms(oa_ref[...], ga_ref[...]).astype(BF16)
        nb = _rms(ob_ref[...], gb_ref[...]).astype(BF16)
        merged_ref[:, 0:512] = na
        merged_ref[:, 512:1024] = nb
        h1 = x_ref[...] + _dot(na, wo_ref[0:512, :]) + _dot(nb, wo_ref[512:1024, :])
        h1_ref[...] = h1
        f_ref[...] = _rms(h1, gf_ref[...]).astype(BF16)

    outs = [jax.ShapeDtypeStruct((s, d), BF16), jax.ShapeDtypeStruct((s, d), F32), jax.ShapeDtypeStruct((s, d), BF16)]
    return pl.pallas_call(
        body, name="attn_out_fwd", grid=(s // tm,), out_shape=outs,
        in_specs=[_row_spec(tm, 512), _row_spec(tm, 512), _full_spec(g_mla.shape), _full_spec(g_sb.shape),
                  _full_spec(w_o.shape), _row_spec(tm, d), _full_spec(g_ffn.shape)],
        out_specs=[_row_spec(tm, d)] * 3,
        compiler_params=_cparams("arbitrary"),
    )(o_mla, o_sb, g_mla, g_sb, w_o, x, g_ffn)


def _ffn_tile(d_ff):
    return d_ff // 2 if (d_ff // 2) % LANES == 0 else d_ff


def _ffn_fwd(f, h1, w_gate, w_up, w_down):
    s, d = h1.shape
    d_ff = w_gate.shape[1]
    tm = min(ROW_TILE, s)
    tf = _ffn_tile(d_ff)

    def body(f_ref, h1_ref, wg_ref, wu_ref, wd_ref, gate_ref, up_ref, h2_ref):
        j = pl.program_id(1)
        fb = f_ref[...]
        gate = _dot(fb, wg_ref[...])
        up = _dot(fb, wu_ref[...])
        gate_ref[...] = gate.astype(BF16)
        up_ref[...] = up.astype(BF16)
        act = (gate * jax.nn.sigmoid(gate) * up).astype(BF16)
        part = _dot(act, wd_ref[...])

        @pl.when(j == 0)
        def _():
            h2_ref[...] = h1_ref[...] + part

        @pl.when(j != 0)
        def _():
            h2_ref[...] += part

    outs = [jax.ShapeDtypeStruct((s, d_ff), BF16), jax.ShapeDtypeStruct((s, d_ff), BF16), jax.ShapeDtypeStruct((s, d), F32)]
    return pl.pallas_call(
        body, name="ffn_fwd", grid=(s // tm, d_ff // tf), out_shape=outs,
        in_specs=[pl.BlockSpec((tm, d), lambda r, j: (r, 0)), pl.BlockSpec((tm, d), lambda r, j: (r, 0)),
                  pl.BlockSpec((d, tf), lambda r, j: (0, j)), pl.BlockSpec((d, tf), lambda r, j: (0, j)),
                  pl.BlockSpec((tf, d), lambda r, j: (j, 0))],
        out_specs=[pl.BlockSpec((tm, tf), lambda r, j: (r, j)), pl.BlockSpec((tm, tf), lambda r, j: (r, j)),
                   pl.BlockSpec((tm, d), lambda r, j: (r, 0))],
        compiler_params=_cparams("arbitrary", "arbitrary"),
    )(f, h1, w_gate, w_up, w_down)


def _final_loss(h2, target, g_final):
    s, d = h2.shape
    tm = min(ROW_TILE, s)

    def body(h2_ref, t_ref, g_ref, loss_ref, dh2_ref, dh2b_ref, dg_ref):
        first = pl.program_id(0) == 0
        h2v = h2_ref[...]
        g = g_ref[...]
        diff = _rms(h2v, g) - t_ref[...]
        part = 0.5 * jnp.sum(jnp.mean(diff * diff, axis=-1, keepdims=True), axis=0, keepdims=True)
        _accumulate(loss_ref, jnp.broadcast_to(part, loss_ref.shape), first)
        dx, dg = _rms_bwd(h2v, g, diff * (1.0 / d))
        dh2_ref[...] = dx
        dh2b_ref[...] = dx.astype(BF16)
        _accumulate(dg_ref, dg, first)

    outs = [jax.ShapeDtypeStruct((1, LANES), F32), jax.ShapeDtypeStruct((s, d), F32), jax.ShapeDtypeStruct((s, d), BF16),
            jax.ShapeDtypeStruct((1, d), F32)]
    return pl.pallas_call(
        body, name="final_loss", grid=(s // tm,), out_shape=outs,
        in_specs=[_row_spec(tm, d), _row_spec(tm, d), _full_spec((1, d))],
        out_specs=[_full_spec((1, LANES)), _row_spec(tm, d), _row_spec(tm, d), _full_spec((1, d))],
        compiler_params=_cparams("arbitrary"),
    )(h2, target, g_final)


def _tile_iotas(t):
    return lax.broadcasted_iota(jnp.int32, (t, t), 0), lax.broadcasted_iota(jnp.int32, (t, t), 1)


def _mla_fwd(qn, qr, kv, kr, shards):
    s = qn.shape[0]
    t = min(ATT_TILE, s)
    pairs = MLA_HEADS // 2
    nq = s // t
    n = len(shards)

    def body(*refs):
        qn_ref, qr_ref, kn_ref, v_ref, kr_ref = refs[:5]
        o_ref, lse_ref = refs[5 + n:7 + n]
        qcat_ref, m_ref, l_ref, acc_ref = refs[7 + 2 * n:11 + 2 * n]
        hp, i = pl.program_id(0), pl.program_id(1)
        ride = _Exchange(True, refs[5:5 + n], refs[7 + n:7 + 2 * n], *refs[11 + 2 * n:])

        @pl.when((hp == 0) & (i == 0))
        def _():
            ride.start()

        lane = lax.broadcasted_iota(jnp.int32, (1, LANES), 1)
        row, col = _tile_iotas(t)
        causal = col <= row
        q_pair, q_quad = qn_ref[...], qr_ref[...]
        zero = jnp.zeros_like(q_pair)
        for hh in range(2):
            in_head = (lane // HEAD_DIM) == hh
            in_rope = (lane // MLA_ROPE) == (hp % 2) * 2 + hh
            qcat_ref[hh * t:(hh + 1) * t, 0:LANES] = jnp.where(in_head, q_pair, zero)
            qcat_ref[hh * t:(hh + 1) * t, LANES:2 * LANES] = jnp.where(in_rope, q_quad, zero)
        m_ref[...] = jnp.full_like(m_ref, NEG)
        l_ref[...] = jnp.zeros_like(l_ref)
        acc_ref[...] = jnp.zeros_like(acc_ref)

        def tile(j, width, masked):
            rows = pl.ds(pl.multiple_of(j * t, t), width * t)
            kcat = jnp.concatenate([kn_ref[rows, :], kr_ref[rows, :]], axis=1)
            v_ones = jnp.concatenate([v_ref[rows, :], jnp.ones((width * t, LANES), BF16)], axis=1)
            scores = [_dot_nt(qcat_ref[hh * t:(hh + 1) * t, :], kcat) for hh in range(2)]
            for hh in range(2):
                half = slice(hh * t, (hh + 1) * t)
                sc = jnp.where(causal, scores[hh], NEG) if masked else scores[hh]
                m = m_ref[half, :]
                m_new = jnp.maximum(m, jnp.max(sc, axis=-1, keepdims=True))
                alpha = jnp.exp(m - m_new)
                p = jnp.exp(sc - jnp.concatenate([m_new] * (width * t // LANES), axis=1))
                pv = _dot(p.astype(BF16), v_ones)
                l_ref[half, :] = alpha * l_ref[half, :] + pv[:, LANES:]
                acc_ref[half, :] = alpha * acc_ref[half, :] + pv[:, :LANES]
                m_ref[half, :] = m_new

        tile(i, 1, True)

        def step(n, carry):
            tile(4 * n, 4, False)
            return carry

        lax.fori_loop(0, i // 4, step, 0)

        @pl.when(i % 4 >= 2)
        def _():
            tile((i // 4) * 4, 2, False)

        @pl.when(i % 2 == 1)
        def _():
            tile(i - 1, 1, False)

        first = (lane // HEAD_DIM) == 0
        o = acc_ref[...] / l_ref[...]
        lse = m_ref[...] + jnp.log(l_ref[...])
        o_ref[...] = jnp.where(first, o[0:t], o[t:2 * t])
        lse_ref[...] = jnp.where(first, lse[0:t], lse[t:2 * t])

        @pl.when((hp == pairs - 1) & (i == nq - 1))
        def _():
            ride.finish()

    gathered_shapes, sems = _exchange_shapes(True, shards)
    outs = [jax.ShapeDtypeStruct((s, 512), F32), jax.ShapeDtypeStruct((pairs, s, LANES), F32)] + gathered_shapes
    res = pl.pallas_call(
        body, name="mla_fwd", grid=(pairs, nq), out_shape=outs,
        in_specs=[pl.BlockSpec((t, LANES), lambda hp, i: (i, hp)), pl.BlockSpec((t, LANES), lambda hp, i: (i, hp // 2)),
                  pl.BlockSpec((s, LANES), lambda hp, i: (0, hp)), pl.BlockSpec((s, LANES), lambda hp, i: (0, 4 + hp)),
                  pl.BlockSpec((s, LANES), lambda hp, i: (0, 0))] + [ANY] * n,
        out_specs=[pl.BlockSpec((t, LANES), lambda hp, i: (i, hp)), pl.BlockSpec((None, t, LANES), lambda hp, i: (hp, i, 0))]
        + [ANY] * n,
        scratch_shapes=[pltpu.VMEM((2 * t, 2 * LANES), BF16), pltpu.VMEM((2 * t, LANES), F32), pltpu.VMEM((2 * t, LANES), F32),
                        pltpu.VMEM((2 * t, LANES), F32)] + sems,
        compiler_params=_cparams("arbitrary", "arbitrary"),
    )(qn, qr, kv, kv, kr, *shards)
    return res[0], res[1], res[2:]


HEADS = (0, 1)


def _sb_logs(z2, strict, masked):
    log_b = jnp.minimum(z2, 0.0) - jnp.log2(1.0 + jnp.exp2(-jnp.abs(z2)))
    log_1m = log_b - z2
    if masked:
        log_1m = jnp.where(strict, log_1m, 0.0)
    return log_1m, log_b


def _block_totals(x):
    t, w = x.shape
    nb = max(w // TRI, 1)
    bw = w // nb
    blocks = [x[:, b * bw:(b + 1) * bw] for b in range(nb)]
    totals = [jnp.broadcast_to(jnp.sum(blk, axis=-1, keepdims=True), (t, LANES)) for blk in blocks]
    whole = totals[0]
    for tot in totals[1:]:
        whole = whole + tot
    return blocks, totals, whole


def _running_sums(blocks, totals, tri, carry, suffix):
    nb = len(blocks)
    reps = blocks[0].shape[1] // LANES
    outs = [None] * nb
    run = carry
    for b in (range(nb - 1, -1, -1) if suffix else range(nb)):
        outs[b] = _dot(blocks[b].astype(BF16), tri) + jnp.concatenate([run] * reps, axis=1)
        run = run + totals[b]
    return outs[0] if nb == 1 else jnp.concatenate(outs, axis=1)


def _tri(t, rel):
    n = min(TRI, t)
    row, col = _tile_iotas(n)
    return rel(row, col).astype(BF16)


def _sweep_width(t):
    return t // 2 if t // 2 >= TRI else t


def _sb_fwd(qkv):
    s = qkv.shape[0]
    t = min(SB_TILE, s)
    sw = _sweep_width(t)
    pairs = SB_HEADS // 2

    def body(q_ref, k_ref, v_ref, o_ref, tot_ref, cnt_ref, qm_ref, right_ref, acc_ref):
        i = pl.program_id(1)
        lane = lax.broadcasted_iota(jnp.int32, (1, LANES), 1)
        row, col = _tile_iotas(t)
        strict = col < row
        t_suffix = _tri(t, lambda r, c: r > c)
        q_pair = q_ref[...]
        for hh in range(2):
            qm_ref[hh] = jnp.where((lane // HEAD_DIM) == hh, q_pair, jnp.zeros_like(q_pair))
        right_ref[...] = jnp.zeros_like(right_ref)
        acc_ref[...] = jnp.zeros_like(acc_ref)

        def tile(start, width, masked):
            rows = pl.ds(pl.multiple_of(start, width), width)
            k, v = k_ref[rows, :], v_ref[rows, :]
            for hh in HEADS:
                log_1m, log_b = _sb_logs(_dot_nt(qm_ref[hh], k), strict, masked)
                blocks, totals, whole = _block_totals(log_1m)
                a = jnp.exp2(log_b + _running_sums(blocks, totals, t_suffix, right_ref[hh], True))
                if masked:
                    a = jnp.where(strict, a, 0.0)
                right_ref[hh] += whole
                acc_ref[hh] += _dot(a.astype(BF16), v)

        tile(i * t, t, True)

        def alive(n):
            return (n < i * (t // sw)) & (jnp.max(right_ref[...]) > SB_DEAD)

        def step(n):
            tile((i * (t // sw) - 1 - n) * sw, sw, False)
            return n + 1

        swept = lax.while_loop(alive, step, jnp.int32(0))
        cnt_ref[...] = jnp.full(cnt_ref.shape, swept.astype(F32))
        first = (lane // HEAD_DIM) == 0
        o_ref[...] = jnp.where(first, acc_ref[0], acc_ref[1])
        tot_ref[...] = jnp.where(first, right_ref[0], right_ref[1])

    outs = [jax.ShapeDtypeStruct((s, 512), F32), jax.ShapeDtypeStruct((pairs, s, LANES), F32),
            jax.ShapeDtypeStruct((pairs, s // t, 8, LANES), F32)]
    return pl.pallas_call(
        body, name="sb_fwd", grid=(pairs, s // t), out_shape=outs,
        in_specs=[pl.BlockSpec((t, LANES), lambda hp, i: (i, hp)), pl.BlockSpec((s, LANES), lambda hp, i: (0, 4 + hp)),
                  pl.BlockSpec((s, LANES), lambda hp, i: (0, 8 + hp))],
        out_specs=[pl.BlockSpec((t, LANES), lambda hp, i: (i, hp)), pl.BlockSpec((None, t, LANES), lambda hp, i: (hp, i, 0)),
                   pl.BlockSpec((None, None, 8, LANES), lambda hp, i: (hp, i, 0, 0))],
        scratch_shapes=[pltpu.VMEM((2, t, LANES), BF16), pltpu.VMEM((2, t, LANES), F32), pltpu.VMEM((2, t, LANES), F32)],
        compiler_params=_cparams("arbitrary", "arbitrary"),
    )(qkv, qkv, qkv)


def _sb_bwd(qkv, do, tot, cnt):
    s = qkv.shape[0]
    t = min(SB_TILE, s)
    sw = _sweep_width(t)
    pairs = SB_HEADS // 2

    def body(q_ref, k_ref, v_ref, do_ref, tot_ref, cnt_ref, dq_ref, dk_ref, dv_ref,
             qm_ref, dob_ref, total_s, left_l, left_g, dq_s, dk_s, dv_s):
        i = pl.program_id(1)

        @pl.when(i == 0)
        def _():
            dk_s[...] = jnp.zeros_like(dk_s)
            dv_s[...] = jnp.zeros_like(dv_s)

        lane = lax.broadcasted_iota(jnp.int32, (1, LANES), 1)
        row, col = _tile_iotas(t)
        strict = col < row
        t_suffix = _tri(t, lambda r, c: r > c)
        t_excl = _tri(t, lambda r, c: r < c)
        q_pair, do_pair, tot_pair = q_ref[...], do_ref[...], tot_ref[...]
        for hh in range(2):
            in_head = (lane // HEAD_DIM) == hh
            qm_ref[hh] = jnp.where(in_head, q_pair, jnp.zeros_like(q_pair))
            dob_ref[hh] = jnp.where(in_head, do_pair, 0.0).astype(BF16)
            total_s[hh] = jnp.broadcast_to(
                jnp.sum(jnp.where(lane == hh * HEAD_DIM, tot_pair, 0.0), axis=-1, keepdims=True), (t, LANES))
        left_l[...] = jnp.zeros_like(left_l)
        left_g[...] = jnp.zeros_like(left_g)
        dq_s[...] = jnp.zeros_like(dq_s)

        def tile(start, width, masked):
            rows = pl.ds(pl.multiple_of(start, width), width)
            k, v = k_ref[rows, :], v_ref[rows, :]
            z2 = [_dot_nt(qm_ref[hh], k) for hh in HEADS]
            d_a = [_dot_nt(dob_ref[hh], v) for hh in HEADS]
            for hh in HEADS:
                qm, dob = qm_ref[hh], dob_ref[hh]
                log_1m, log_b = _sb_logs(z2[hh], strict, masked)
                blocks, totals, whole = _block_totals(log_1m)
                done = left_l[hh] + whole
                left_l[hh] = done
                a = jnp.exp2(log_b + _running_sums(blocks, totals, t_suffix, total_s[hh] - done, True))
                if masked:
                    a = jnp.where(strict, a, 0.0)
                g = a * d_a[hh]
                blocks, totals, whole = _block_totals(g)
                before = _running_sums(blocks, totals, t_excl, left_g[hh], False)
                left_g[hh] += whole
                dz = g - jnp.exp2(log_b) * (g + before)
                if masked:
                    dz = jnp.where(strict, dz, 0.0)
                dzb = dz.astype(BF16)
                dq_s[hh] += _dot(dzb, k)
                dk_s[rows, :] += _dot_tn(dzb, qm)
                dv_s[rows, :] += _dot_tn(a.astype(BF16), dob)

        def step(h, carry):
            tile(h * sw, sw, False)
            return carry

        swept = jnp.max(cnt_ref[...]).astype(jnp.int32)
        lax.fori_loop(i * (t // sw) - swept, i * (t // sw), step, 0)
        tile(i * t, t, True)
        dq_ref[...] = (jnp.where((lane // HEAD_DIM) == 0, dq_s[0], dq_s[1]) * SB_SCALE).astype(BF16)

        @pl.when(i == s // t - 1)
        def _():
            dk_ref[...] = (dk_s[...] * (1.0 / LOG2E)).astype(BF16)
            dv_ref[...] = dv_s[...].astype(BF16)

    outs = [jax.ShapeDtypeStruct((s, 512), BF16)] * 3
    return pl.pallas_call(
        body, name="sb_bwd", grid=(pairs, s // t), out_shape=outs,
        in_specs=[pl.BlockSpec((t, LANES), lambda hp, i: (i, hp)), pl.BlockSpec((s, LANES), lambda hp, i: (0, 4 + hp)),
                  pl.BlockSpec((s, LANES), lambda hp, i: (0, 8 + hp)), pl.BlockSpec((t, LANES), lambda hp, i: (i, hp)),
                  pl.BlockSpec((None, t, LANES), lambda hp, i: (hp, i, 0)),
                  pl.BlockSpec((None, None, 8, LANES), lambda hp, i: (hp, i, 0, 0))],
        out_specs=[pl.BlockSpec((t, LANES), lambda hp, i: (i, hp)), pl.BlockSpec((s, LANES), lambda hp, i: (0, hp)),
                   pl.BlockSpec((s, LANES), lambda hp, i: (0, hp))],
        scratch_shapes=[pltpu.VMEM((2, t, LANES), BF16), pltpu.VMEM((2, t, LANES), BF16)]
        + [pltpu.VMEM((2, t, LANES), F32)] * 4 + [pltpu.VMEM((s, LANES), F32)] * 2,
        compiler_params=_cparams("arbitrary", "arbitrary"),
    )(qkv, qkv, qkv, do, tot, cnt)


def _mla_bwd(qn, qr, kv, kr, do, o, lse, parts):
    s = qn.shape[0]
    t = min(ATT_TILE, s)
    pairs = MLA_HEADS // 2
    nq = s // t
    n = len(parts)

    def body(*refs):
        qn_ref, qr_ref, kn_ref, v_ref, kr_ref, do_ref, o_ref, lse_ref = refs[:8]
        dqn_ref, dqr_ref, dkn_ref, dv_ref, dkr_ref = refs[8 + n:13 + n]
        qcat_ref, dob_ref, lse_s, delta_s, dq_s, dkn_s, dv_s, dkr_s = refs[13 + 2 * n:21 + 2 * n]
        hp, i = pl.program_id(0), pl.program_id(1)
        ride = _Exchange(False, refs[8:8 + n], refs[13 + n:13 + 2 * n], *refs[21 + 2 * n:])

        @pl.when((hp == 0) & (i == 0))
        def _():
            ride.start()

        @pl.when(i == 0)
        def _():
            dkn_s[...] = jnp.zeros_like(dkn_s)
            dv_s[...] = jnp.zeros_like(dv_s)
            dkr_s[...] = jnp.zeros_like(dkr_s)

        lane = lax.broadcasted_iota(jnp.int32, (1, LANES), 1)
        row, col = _tile_iotas(t)
        causal = col <= row
        q_pair, q_quad, do_pair, lse_pair = qn_ref[...], qr_ref[...], do_ref[...], lse_ref[...]
        do_o = do_pair * o_ref[...]
        zero = jnp.zeros_like(q_pair)
        ropes = []
        for hh in range(2):
            in_head = (lane // HEAD_DIM) == hh
            in_rope = (lane // MLA_ROPE) == (hp % 2) * 2 + hh
            ropes.append(in_rope)
            qcat_ref[hh, :, 0:LANES] = jnp.where(in_head, q_pair, zero)
            qcat_ref[hh, :, LANES:2 * LANES] = jnp.where(in_rope, q_quad, zero)
            dob_ref[hh] = jnp.where(in_head, do_pair, 0.0).astype(BF16)
            delta_s[hh] = jnp.broadcast_to(jnp.sum(jnp.where(in_head, do_o, 0.0), axis=-1, keepdims=True), (t, LANES))
            lse_s[hh] = jnp.broadcast_to(
                jnp.sum(jnp.where(lane == hh * HEAD_DIM, lse_pair, 0.0), axis=-1, keepdims=True), (t, LANES))
        dq_s[...] = jnp.zeros_like(dq_s)
        reps = t // LANES

        def tile(j, width, masked):
            rows = pl.ds(pl.multiple_of(j * t, t), width * t)
            kcat = jnp.concatenate([kn_ref[rows, :], kr_ref[rows, :]], axis=1)
            v = v_ref[rows, :]
            sc = [_dot_nt(qcat_ref[hh], kcat) for hh in HEADS]
            dp = [_dot_nt(dob_ref[hh], v) for hh in HEADS]
            p = [jnp.exp(sc[hh] - jnp.concatenate([lse_s[hh]] * (width * reps), axis=1)) for hh in HEADS]
            if masked:
                p = [jnp.where(causal, p[hh], 0.0) for hh in HEADS]
            ds = [(p[hh] * (dp[hh] - jnp.concatenate([delta_s[hh]] * (width * reps), axis=1))).astype(BF16) for hh in HEADS]
            for hh in HEADS:
                dq_s[hh] += _dot(ds[hh], kcat)
            dkcat = _dot_tn(ds[0], qcat_ref[0]) + _dot_tn(ds[1], qcat_ref[1])
            dkn_s[rows, :] += dkcat[:, 0:LANES]
            dkr_s[rows, :] += dkcat[:, LANES:2 * LANES]
            dv_s[rows, :] += _dot_tn(p[0].astype(BF16), dob_ref[0]) + _dot_tn(p[1].astype(BF16), dob_ref[1])

        def step(n, carry):
            tile(4 * n, 4, False)
            return carry

        lax.fori_loop(0, i // 4, step, 0)

        @pl.when(i % 4 >= 2)
        def _():
            tile((i // 4) * 4, 2, False)

        @pl.when(i % 2 == 1)
        def _():
            tile(i - 1, 1, False)

        tile(i, 1, True)
        dqn_ref[...] = (jnp.where((lane // HEAD_DIM) == 0, dq_s[0, :, 0:LANES], dq_s[1, :, 0:LANES]) * MLA_SCALE).astype(BF16)
        dqr_ref[...] = ((jnp.where(ropes[0], dq_s[0, :, LANES:2 * LANES], 0.0)
                         + jnp.where(ropes[1], dq_s[1, :, LANES:2 * LANES], 0.0)) * MLA_SCALE).astype(BF16)

        @pl.when(i == nq - 1)
        def _():
            dkn_ref[...] = dkn_s[...].astype(BF16)
            dv_ref[...] = dv_s[...].astype(BF16)
            dkr_ref[...] = dkr_s[...].astype(BF16)

        @pl.when((hp == pairs - 1) & (i == nq - 1))
        def _():
            ride.finish()

    pair_block = pl.BlockSpec((t, LANES), lambda hp, i: (i, hp))
    once = pl.Buffered(1)
    landed_shapes, sems = _exchange_shapes(False, parts)
    outs = [jax.ShapeDtypeStruct((s, 512), BF16), jax.ShapeDtypeStruct((pairs, s, LANES), BF16),
            jax.ShapeDtypeStruct((s, 512), BF16), jax.ShapeDtypeStruct((s, 512), BF16),
            jax.ShapeDtypeStruct((pairs, s, LANES), BF16)] + landed_shapes
    res = pl.pallas_call(
        body, name="mla_bwd", grid=(pairs, nq), out_shape=outs,
        in_specs=[pair_block, pl.BlockSpec((t, LANES), lambda hp, i: (i, hp // 2)),
                  pl.BlockSpec((s, LANES), lambda hp, i: (0, hp), pipeline_mode=once),
                  pl.BlockSpec((s, LANES), lambda hp, i: (0, 4 + hp), pipeline_mode=once),
                  pl.BlockSpec((s, LANES), lambda hp, i: (0, 0), pipeline_mode=once), pair_block, pair_block,
                  pl.BlockSpec((None, t, LANES), lambda hp, i: (hp, i, 0))] + [ANY] * n,
        out_specs=[pair_block, pl.BlockSpec((None, t, LANES), lambda hp, i: (hp, i, 0)),
                   pl.BlockSpec((s, LANES), lambda hp, i: (0, hp), pipeline_mode=once),
                   pl.BlockSpec((s, LANES), lambda hp, i: (0, hp), pipeline_mode=once),
                   pl.BlockSpec((None, s, LANES), lambda hp, i: (hp, 0, 0), pipeline_mode=once)] + [ANY] * n,
        scratch_shapes=[pltpu.VMEM((2, t, 2 * LANES), BF16), pltpu.VMEM((2, t, LANES), BF16), pltpu.VMEM((2, t, LANES), F32),
                        pltpu.VMEM((2, t, LANES), F32), pltpu.VMEM((2, t, 2 * LANES), F32)]
        + [pltpu.VMEM((s, LANES), F32)] * 3 + sems,
        compiler_params=_cparams("arbitrary", "arbitrary"),
    )(qn, qr, kv, kv, kr, do, o, lse, *parts)
    return res[:5], res[5:]


def _ffn_bwd(dh2, dh2b, gate, up, h1, g_ffn, w_down, w_gate, w_up):
    s, d = h1.shape
    d_ff = gate.shape[1]
    tm = min(FFN_BWD_ROW_TILE, s)
    tf = _ffn_tile(d_ff)

    def act_body(dh2b_ref, gate_ref, up_ref, wd_ref, dgate_ref, dup_ref, act_ref):
        dact = _dot_nt(dh2b_ref[...], wd_ref[...])
        gate_v = gate_ref[...].astype(F32)
        up_v = up_ref[...].astype(F32)
        sig = jax.nn.sigmoid(gate_v)
        silu = gate_v * sig
        dup_ref[...] = (dact * silu).astype(BF16)
        dgate_ref[...] = ((dact * up_v) * (sig * (1.0 + gate_v - silu))).astype(BF16)
        act_ref[...] = (silu * up_v).astype(BF16)

    ff = pl.BlockSpec((tm, tf), lambda j, r: (r, j))
    dgate, dup, act = pl.pallas_call(
        act_body, name="ffn_bwd_act", grid=(d_ff // tf, s // tm), out_shape=[jax.ShapeDtypeStruct((s, d_ff), BF16)] * 3,
        in_specs=[pl.BlockSpec((tm, d), lambda j, r: (r, 0)), ff, ff, pl.BlockSpec((tf, d), lambda j, r: (j, 0))],
        out_specs=[ff, ff, ff],
        compiler_params=_cparams("arbitrary", "arbitrary"),
    )(dh2b, gate, up, w_down)

    def df_body(dgate_ref, dup_ref, dh2_ref, h1_ref, g_ref, wg_ref, wu_ref, dh1_ref, dh1b_ref, dg_ref):
        df = _dot_nt(dgate_ref[...], wg_ref[...]) + _dot_nt(dup_ref[...], wu_ref[...])
        dx, dg = _rms_bwd(h1_ref[...], g_ref[...], df)
        dh1 = dh2_ref[...] + dx
        dh1_ref[...] = dh1
        dh1b_ref[...] = dh1.astype(BF16)
        _accumulate(dg_ref, dg, pl.program_id(0) == 0)

    outs = [jax.ShapeDtypeStruct((s, d), F32), jax.ShapeDtypeStruct((s, d), BF16), jax.ShapeDtypeStruct((1, d), F32)]
    dh1, dh1b, dg = pl.pallas_call(
        df_body, name="ffn_bwd_df", grid=(s // tm,), out_shape=outs,
        in_specs=[_row_spec(tm, d_ff), _row_spec(tm, d_ff), _row_spec(tm, d), _row_spec(tm, d), _full_spec((1, d)),
                  _full_spec(w_gate.shape), _full_spec(w_up.shape)],
        out_specs=[_row_spec(tm, d), _row_spec(tm, d), _full_spec((1, d))],
        compiler_params=_cparams("arbitrary"),
    )(dgate, dup, dh2, h1, g_ffn, w_gate, w_up)
    return dgate, dup, act, dh1, dh1b, dg


def _tn_matmul(a, b, name):
    assert a.dtype == BF16 and b.dtype == BF16
    s, m = a.shape
    n = b.shape[1]
    if s * m * 2 <= TN_RESIDENT_BYTES:
        tm, tn = m, min(n, TN_BLOCK)
    else:
        tm, tn = TN_BLOCK, n

    def body(a_ref, b_ref, o_ref):
        o_ref[...] = _dot_tn(a_ref[...], b_ref[...]).astype(BF16)

    return pl.pallas_call(
        body, name=name, grid=(m // tm, n // tn), out_shape=jax.ShapeDtypeStruct((m, n), BF16),
        in_specs=[pl.BlockSpec((s, tm), lambda i, j: (0, i)), pl.BlockSpec((s, tn), lambda i, j: (0, j))],
        out_specs=pl.BlockSpec((tm, tn), lambda i, j: (i, j)),
        compiler_params=_cparams("arbitrary", "arbitrary"),
    )(a, b)


def _attn_out_bwd(dh1, w_o, o_mla, o_sb, g_mla, g_sb):
    s, d = dh1.shape
    tm = min(ROW_TILE, s)

    def body(dh1_ref, wo_ref, oa_ref, ob_ref, ga_ref, gb_ref, doa_ref, dob_ref, dga_ref, dgb_ref):
        first = pl.program_id(0) == 0
        dh1b = dh1_ref[...]
        dxa, dga = _rms_bwd(oa_ref[...], ga_ref[...], _dot_nt(dh1b, wo_ref[0:512, :]))
        dxb, dgb = _rms_bwd(ob_ref[...], gb_ref[...], _dot_nt(dh1b, wo_ref[512:1024, :]))
        doa_ref[...] = dxa
        dob_ref[...] = dxb
        _accumulate(dga_ref, dga, first)
        _accumulate(dgb_ref, dgb, first)

    outs = [jax.ShapeDtypeStruct((s, 512), F32)] * 2 + [jax.ShapeDtypeStruct((1, 512), F32)] * 2
    return pl.pallas_call(
        body, name="attn_out_bwd", grid=(s // tm,), out_shape=outs,
        in_specs=[_row_spec(tm, d), _full_spec(w_o.shape), _row_spec(tm, 512), _row_spec(tm, 512),
                  _full_spec((1, 512)), _full_spec((1, 512))],
        out_specs=[_row_spec(tm, 512), _row_spec(tm, 512), _full_spec((1, 512)), _full_spec((1, 512))],
        compiler_params=_cparams("arbitrary"),
    )(dh1, w_o, o_mla, o_sb, g_mla, g_sb)


def _proj_in_bwd(dqn, dqr, dkn, dv, dkr, dq_sb, dk_sb, dv_sb, cq, ckv, x, dh1, cos, sin_a, sin_b,
                 g_q, g_kv, g_mix, w_uq, w_ukv, w_a):
    s, d = x.shape
    tm = min(PROJ_BWD_ROW_TILE, s)

    def body(dqn_ref, dqr_ref, dkn_ref, dv_ref, dkr_ref, dqs_ref, dks_ref, dvs_ref, cq_ref, ckv_ref, x_ref, dh1_ref,
             cos_ref, sa_ref, sb_ref, gq_ref, gkv_ref, gm_ref, wuq_ref, wukv_ref, wa_ref,
             dx_ref, dproj_ref, dq_ref, dkv_ref, dgq_ref, dgkv_ref, dgm_ref):
        first = pl.program_id(0) == 0
        lane = lax.broadcasted_iota(jnp.int32, (1, LANES), 1)
        cos_t, sa_t, sb_t = cos_ref[...], sa_ref[...], sb_ref[...]
        dq_ref[:, 0:512] = dqn_ref[...]
        for half in range(2):
            quad = dqr_ref[2 * half].astype(F32) + dqr_ref[2 * half + 1].astype(F32)
            dq_ref[:, 512 + half * LANES:512 + (half + 1) * LANES] = _rope_t(quad, cos_t, sa_t, sb_t).astype(BF16)
        dcq, dgq = _rms_bwd(cq_ref[...], gq_ref[...], _dot_nt(dq_ref[...], wuq_ref[...]))
        _accumulate(dgq_ref, dgq, first)
        dkv_ref[:, 0:512] = dkn_ref[...]
        dkv_ref[:, 512:1024] = dv_ref[...]
        dckv, dgkv = _rms_bwd(ckv_ref[...], gkv_ref[...], _dot_nt(dkv_ref[...], wukv_ref[...]))
        _accumulate(dgkv_ref, dgkv, first)
        pairs_sum = (dkr_ref[0].astype(F32) + dkr_ref[1].astype(F32)) + (dkr_ref[2].astype(F32) + dkr_ref[3].astype(F32))
        g = _rope_t(pairs_sum, cos_t, sa_t, sb_t)
        g = g + pltpu.roll(g, 96, 1) + pltpu.roll(g, 64, 1) + pltpu.roll(g, 32, 1)
        dproj_ref[:, 0:256] = dcq.astype(BF16)
        dproj_ref[:, 256:384] = dckv.astype(BF16)
        dproj_ref[:, 384:512] = jnp.where(lane < MLA_ROPE, g, 0.0).astype(BF16)
        dproj_ref[:, 512:1024] = dqs_ref[...]
        dproj_ref[:, 1024:1536] = dks_ref[...]
        dproj_ref[:, 1536:2048] = dvs_ref[...]
        dxn, dgm = _rms_bwd(x_ref[...], gm_ref[...], _dot_nt(dproj_ref[...], wa_ref[...]))
        dx_ref[...] = dh1_ref[...] + dxn
        _accumulate(dgm_ref, dgm, first)

    quad_spec = pl.BlockSpec((4, tm, LANES), lambda r: (0, r, 0))
    outs = [jax.ShapeDtypeStruct((s, d), F32), jax.ShapeDtypeStruct((s, 2048), BF16), jax.ShapeDtypeStruct((s, 768), BF16),
            jax.ShapeDtypeStruct((s, 1024), BF16), jax.ShapeDtypeStruct((1, 256), F32), jax.ShapeDtypeStruct((1, 128), F32),
            jax.ShapeDtypeStruct((1, d), F32)]
    return pl.pallas_call(
        body, name="proj_in_bwd", grid=(s // tm,), out_shape=outs,
        in_specs=[_row_spec(tm, 512), quad_spec, _row_spec(tm, 512), _row_spec(tm, 512), quad_spec,
                  _row_spec(tm, 512), _row_spec(tm, 512), _row_spec(tm, 512), _row_spec(tm, 256), _row_spec(tm, 128),
                  _row_spec(tm, d), _row_spec(tm, d), _row_spec(tm, LANES), _row_spec(tm, LANES), _row_spec(tm, LANES),
                  _full_spec((1, 256)), _full_spec((1, 128)), _full_spec((1, d)),
                  _full_spec(w_uq.shape), _full_spec(w_ukv.shape), _full_spec(w_a.shape)],
        out_specs=[_row_spec(tm, d), _row_spec(tm, 2048), _row_spec(tm, 768), _row_spec(tm, 1024),
                   _full_spec((1, 256)), _full_spec((1, 128)), _full_spec((1, d))],
        compiler_params=_cparams("arbitrary"),
    )(dqn, dqr, dkn, dv, dkr, dq_sb, dk_sb, dv_sb, cq, ckv, x, dh1, cos, sin_a, sin_b, g_q, g_kv, g_mix,
      w_uq, w_ukv, w_a)


ANY = pl.BlockSpec(memory_space=pl.ANY)


def _place():
    return lax.axis_index("x"), lax.axis_index("y"), lax.axis_index("c")


def _all_gather(shards, name):
    n = len(shards)

    def body(*refs):
        ins, outs = refs[:n], refs[n:2 * n]
        send_sems, recv_sems, local_sems = refs[2 * n:]
        x, y, c = _place()
        me, sibling = (x, y, c), (x, y, 1 - c)
        chips = [(1 - x, y), (x, 1 - y), (1 - x, 1 - y)]

        def slot(a, px, py, pc):
            return outs[a].at[4 * px + 2 * py + pc]

        def copy(a, k, block, to, src=None):
            return pltpu.make_async_remote_copy(
                src_ref=slot(a, *block) if src is None else src, dst_ref=slot(a, *block),
                send_sem=send_sems.at[a, k], recv_sem=recv_sems.at[a, k], device_id=to, device_id_type=MESH)

        mine, first, passed = [], [], []
        for a in range(n):
            own = pltpu.make_async_copy(ins[a], slot(a, *me), local_sems.at[a])
            own.start()
            mine.append(own)
            cps = [copy(a, 0, me, sibling, src=ins[a])]
            cps += [copy(a, 1 + j, me, (*chip, c), src=ins[a]) for j, chip in enumerate(chips)]
            for cp in cps:
                cp.start()
            first += cps
        for a in range(n):
            for j, chip in enumerate(chips):
                copy(a, 1 + j, (*chip, c), me).wait_recv()
                fwd = copy(a, 4 + j, (*chip, c), sibling)
                fwd.start()
                passed.append(fwd)
        for a in range(n):
            copy(a, 0, sibling, me).wait_recv()
            for j, chip in enumerate(chips):
                copy(a, 4 + j, (*chip, 1 - c), me).wait_recv()
        for cp in first + passed:
            cp.wait_send()
        for own in mine:
            own.wait()

    return pl.pallas_call(
        body, name=name,
        out_shape=[jax.ShapeDtypeStruct((N_DEV,) + v.shape, v.dtype) for v in shards],
        in_specs=[ANY] * n, out_specs=[ANY] * n,
        scratch_shapes=[pltpu.SemaphoreType.DMA((n, 7)), pltpu.SemaphoreType.DMA((n, 7)), pltpu.SemaphoreType.DMA((n,))],
    )(*shards)


class _Exchange:
    def __init__(self, gather, ins, outs, send_sems, recv_sems, local_sems):
        self.gather, self.ins, self.outs = gather, ins, outs
        self.sems = (send_sems, recv_sems, local_sems)
        x, y, c = _place()
        self.me = 4 * x + 2 * y + c
        self.peers = []
        for k in range(1, N_DEV):
            px = 1 - x if k & 4 else x
            py = 1 - y if k & 2 else y
            pc = 1 - c if k & 1 else c
            self.peers.append(((px, py, pc), 4 * px + 2 * py + pc))

    def _remote(self, a, k, landing):
        send_sems, recv_sems, _ = self.sems
        where, number = self.peers[k]
        src = self.ins[a] if self.gather else self.ins[a].at[number]
        return pltpu.make_async_remote_copy(
            src_ref=src, dst_ref=self.outs[a].at[landing], send_sem=send_sems.at[a, k], recv_sem=recv_sems.at[a, k],
            device_id=where, device_id_type=MESH)

    def _local(self, a):
        src = self.ins[a] if self.gather else self.ins[a].at[self.me]
        return pltpu.make_async_copy(src, self.outs[a].at[self.me], self.sems[2].at[a])

    def start(self):
        for a in range(len(self.ins)):
            self._local(a).start()
            for k in range(N_DEV - 1):
                self._remote(a, k, self.me).start()

    def finish(self):
        for a in range(len(self.ins)):
            for k in range(N_DEV - 1):
                self._remote(a, k, self.peers[k][1]).wait_recv()
            for k in range(N_DEV - 1):
                self._remote(a, k, self.me).wait_send()
            self._local(a).wait()


def _exchange_shapes(gather, arrays):
    out_shape = [jax.ShapeDtypeStruct(((N_DEV,) + v.shape) if gather else v.shape, v.dtype) for v in arrays]
    n = len(arrays)
    sems = [pltpu.SemaphoreType.DMA((n, N_DEV - 1)), pltpu.SemaphoreType.DMA((n, N_DEV - 1)), pltpu.SemaphoreType.DMA((n,))]
    return out_shape, sems


def _exchange(gathers, scatters, name):
    ng, ns = len(gathers), len(scatters)
    n = ng + ns

    def body(*refs):
        ins, outs, sems = refs[:n], refs[n:2 * n], refs[2 * n:]
        both = [_Exchange(True, ins[:ng], outs[:ng], *sems[:3]), _Exchange(False, ins[ng:], outs[ng:], *sems[3:])]
        for ex in both:
            ex.start()
        for ex in both:
            ex.finish()

    g_shapes, g_sems = _exchange_shapes(True, gathers)
    s_shapes, s_sems = _exchange_shapes(False, scatters)
    res = pl.pallas_call(body, name=name, out_shape=g_shapes + s_shapes, in_specs=[ANY] * n, out_specs=[ANY] * n,
                         scratch_shapes=g_sems + s_sems)(*gathers, *scatters)
    return res[:ng], res[ng:]


def _grad_row_tile(rows):
    return _largest_tile_rows(rows, 256)


def _largest_tile_rows(rows, cap):
    for cand in range(cap, 0, -8):
        if rows % cand == 0:
            return cand
    return rows


def _adamw_math(w, g, m, v):
    m_new = ADAM_B1 * m + (1.0 - ADAM_B1) * g
    v_new = ADAM_B2 * v + (1.0 - ADAM_B2) * (g * g)
    m_hat = m_new / (1.0 - ADAM_B1 ** ADAM_STEP)
    v_hat = v_new / (1.0 - ADAM_B2 ** ADAM_STEP)
    delta = -ADAM_LR * (m_hat / (jnp.sqrt(v_hat) + ADAM_EPS) + ADAM_WD * w)
    return delta, m_new, v_new


def _adamw(slots, w, m, v, name):
    k, r, cdim = slots.shape
    tr = _grad_row_tile(r)

    def body(s_ref, w_ref, m_ref, v_ref, g_ref, d_ref, mo_ref, vo_ref):
        g = s_ref[0].astype(F32)
        for q in range(1, k):
            g = g + s_ref[q].astype(F32)
        g_ref[...] = g
        d_ref[...], mo_ref[...], vo_ref[...] = _adamw_math(w_ref[...], g, m_ref[...], v_ref[...])

    blk = pl.BlockSpec((tr, cdim), lambda i: (i, 0))
    return pl.pallas_call(
        body, name=name, grid=(r // tr,), out_shape=[jax.ShapeDtypeStruct((r, cdim), F32)] * 4,
        in_specs=[pl.BlockSpec((k, tr, cdim), lambda i: (0, i, 0)), blk, blk, blk], out_specs=[blk] * 4,
        compiler_params=_cparams("arbitrary"),
    )(slots, w, m, v)


def _stack_cols(g):
    n, r, c = g.shape
    return g.transpose(1, 0, 2).reshape(r, n * c)


def _split_cols(w):
    r, nc = w.shape
    return w.reshape(r, N_DEV, nc // N_DEV).transpose(1, 0, 2)


def _rope_tables(positions):
    inv_freq = ROPE_THETA ** (-jnp.arange(0, MLA_ROPE, 2, dtype=F32) / MLA_ROPE)
    ang = positions.astype(F32).reshape(-1, 1) * inv_freq[None, :]
    cos, sin, zero = jnp.cos(ang), jnp.sin(ang), jnp.zeros_like(ang)
    reps = LANES // MLA_ROPE
    return (jnp.tile(jnp.concatenate([cos, cos], axis=1), (1, reps)),
            jnp.tile(jnp.concatenate([-sin, zero], axis=1), (1, reps)),
            jnp.tile(jnp.concatenate([zero, sin], axis=1), (1, reps)))


def _local_step(x, positions, loss_target, gains, g_in, g_uq, g_ukv, late_shards):
    norm_mix, q_norm, kv_norm, out_mla, out_sb, norm_ffn, norm_final = gains
    d = x.shape[1]
    w_in = _stack_cols(g_in)
    w_a = jnp.concatenate([w_in[:, :416], jnp.zeros((d, 96), BF16), w_in[:, 416:]], axis=1)
    w_uq = jnp.concatenate([g_uq[:, :, :MLA_NOPE].transpose(1, 0, 2).reshape(Q_LORA, -1),
                            g_uq[:, :, MLA_NOPE:].transpose(1, 0, 2).reshape(Q_LORA, -1)], axis=1)
    w_ukv = jnp.concatenate([g_ukv[:, :, :MLA_NOPE].transpose(1, 0, 2).reshape(KV_LORA, -1),
                             g_ukv[:, :, MLA_NOPE:].transpose(1, 0, 2).reshape(KV_LORA, -1)], axis=1)
    cos, sin_a, sin_b = _rope_tables(positions)

    u, cq, ckv, cqn, ckvn, qn, qr, kv, kr, qkv_sb = _proj_in_fwd(x, norm_mix, w_a, q_norm, w_uq, kv_norm, w_ukv, cos, sin_a, sin_b)
    o_mla, lse, (g_o, g_gate, g_up, g_down) = _mla_fwd(qn, qr, kv, kr, late_shards)
    w_o = g_o.reshape(-1, d)
    w_gate, w_up = _stack_cols(g_gate), _stack_cols(g_up)
    w_down = g_down.reshape(-1, d)
    o_sb, tot, swept = _sb_fwd(qkv_sb)
    merged, h1, f = _attn_out_fwd(o_mla, o_sb, out_mla, out_sb, w_o, x, norm_ffn)
    gate, up, h2 = _ffn_fwd(f, h1, w_gate, w_up, w_down)
    loss, dh2, dh2b, dg_final = _final_loss(h2, loss_target, norm_final.reshape(1, d))

    dgate, dup, act, dh1, dh1b, dg_ffn = _ffn_bwd(dh2, dh2b, gate, up, h1, norm_ffn, w_down, w_gate, w_up)
    dw_down = _tn_matmul(act, dh2b, "dw_down")
    dw_gate = _tn_matmul(f, dgate, "dw_gate")
    dw_up = _tn_matmul(f, dup, "dw_up")
    do_mla, do_sb, dg_mla, dg_sb = _attn_out_bwd(dh1b, w_o, o_mla, o_sb, out_mla, out_sb)
    dw_o = _tn_matmul(merged, dh1b, "dw_o")
    dq_sb, dk_sb, dv_sb = _sb_bwd(qkv_sb, do_sb, tot, swept)
    early = [dw_o.reshape(N_DEV, -1, d), _split_cols(dw_gate), _split_cols(dw_up), dw_down.reshape(N_DEV, -1, d)]
    (dqn, dqr, dkn, dv, dkr), landed = _mla_bwd(qn, qr, kv, kr, do_mla, o_mla, lse, early)
    dx, dproj, dq, dkv, dg_q, dg_kv, dg_mix = _proj_in_bwd(
        dqn, dqr, dkn, dv, dkr, dq_sb, dk_sb, dv_sb, cq, ckv, x, dh1, cos, sin_a, sin_b,
        q_norm, kv_norm, norm_mix, w_uq, w_ukv, w_a)
    dw_a = _tn_matmul(u, dproj, "dw_in")
    dw_uq = _tn_matmul(cqn, dq, "dw_uq")
    dw_ukv = _tn_matmul(ckvn, dkv, "dw_ukv")

    p_in = _split_cols(jnp.concatenate([dw_a[:, :416], dw_a[:, 512:]], axis=1))
    p_uq = jnp.concatenate([dw_uq[:, :512].reshape(Q_LORA, MLA_HEADS, MLA_NOPE),
                            dw_uq[:, 512:].reshape(Q_LORA, MLA_HEADS, MLA_ROPE)], axis=2).transpose(1, 0, 2)
    p_ukv = jnp.concatenate([dw_ukv[:, :512].reshape(KV_LORA, MLA_HEADS, MLA_NOPE),
                             dw_ukv[:, 512:].reshape(KV_LORA, MLA_HEADS, HEAD_DIM)], axis=2).transpose(1, 0, 2)
    late = [p_in, p_uq, p_ukv]
    gain_grads = [dg_mix, dg_q, dg_kv, dg_mla, dg_sb, dg_ffn, dg_final]
    return loss, dx, list(landed), late, gain_grads


def kernel(x, positions, norm_mix, w_in, q_latent_norm, w_uq, kv_latent_norm, w_ukv, out_norm_mla, out_norm_sb, w_o, norm_ffn, w_gate, w_up, w_down, norm_final, loss_target, m_norm_mix, m_w_in, m_q_latent_norm, m_w_uq, m_kv_latent_norm, m_w_ukv, m_out_norm_mla, m_out_norm_sb, m_w_o, m_norm_ffn, m_w_gate, m_w_up, m_w_down, m_norm_final, v_norm_mix, v_w_in, v_q_latent_norm, v_w_uq, v_kv_latent_norm, v_w_ukv, v_out_norm_mla, v_out_norm_sb, v_w_o, v_norm_ffn, v_w_gate, v_w_up, v_w_down, v_norm_final):
    mats = [w_in, w_uq, w_ukv, w_o, w_gate, w_up, w_down]
    mat_m = [m_w_in, m_w_uq, m_w_ukv, m_w_o, m_w_gate, m_w_up, m_w_down]
    mat_v = [v_w_in, v_w_uq, v_w_ukv, v_w_o, v_w_gate, v_w_up, v_w_down]
    mat_names = ["w_in", "w_uq", "w_ukv", "w_o", "w_gate", "w_up", "w_down"]
    gains = [norm_mix, q_latent_norm, kv_latent_norm, out_norm_mla, out_norm_sb, norm_ffn, norm_final]
    gain_m = [m_norm_mix, m_q_latent_norm, m_kv_latent_norm, m_out_norm_mla, m_out_norm_sb, m_norm_ffn, m_norm_final]
    gain_v = [v_norm_mix, v_q_latent_norm, v_kv_latent_norm, v_out_norm_mla, v_out_norm_sb, v_norm_ffn, v_norm_final]

    shards = [w[0].astype(BF16) for w in mats]
    g_in, g_uq, g_ukv = _all_gather(shards[:3], "weight_all_gather")

    gains2d = [g.reshape(1, -1) for g in gains]
    loss_part, dx, landed, late, gain_grads = _local_step(
        x[0], positions[0], loss_target[0], gains2d, g_in, g_uq, g_ukv, shards[3:])

    sizes = [g.size for g in gains]
    used = sum(sizes) + LANES
    rows = -(-used // (8 * LANES)) * 8

    def pack(vals, tail):
        flat = jnp.concatenate([v.reshape(-1) for v in vals] + [tail])
        return jnp.pad(flat, (0, rows * LANES - flat.size)).reshape(rows, LANES)

    (small,), scattered = _exchange([pack(gain_grads, loss_part.reshape(-1))], late, "grad_exchange")

    mat_out = [_adamw(sl, w[0], m[0], v[0], "adamw_" + nm)
               for sl, w, m, v, nm in zip(list(scattered) + landed, mats, mat_m, mat_v, mat_names)]
    zeros_tail = jnp.zeros((LANES,), F32)
    g_s, d_s, m_s, v_s = _adamw(small, pack(gains, zeros_tail), pack(gain_m, zeros_tail), pack(gain_v, zeros_tail), "adamw_gains")

    def unpack(packed):
        flat = packed.reshape(-1)
        outs, off = [], 0
        for g, n in zip(gains, sizes):
            outs.append(flat[off:off + n].reshape(g.shape))
            off += n
        return outs

    loss = g_s.reshape(-1)[sum(sizes)]

    order = ["norm_mix", "w_in", "q_latent_norm", "w_uq", "kv_latent_norm", "w_ukv", "out_norm_mla", "out_norm_sb",
             "w_o", "norm_ffn", "w_gate", "w_up", "w_down", "norm_final"]
    gain_names = ["norm_mix", "q_latent_norm", "kv_latent_norm", "out_norm_mla", "out_norm_sb", "norm_ffn", "norm_final"]
    result = [loss, dx[None]]
    for kind in range(4):
        small_parts = dict(zip(gain_names, unpack([g_s, d_s, m_s, v_s][kind])))
        mat_parts = {nm: out[kind][None] for nm, out in zip(mat_names, mat_out)}
        result += [small_parts[nm] if nm in small_parts else mat_parts[nm] for nm in order]
    return tuple(result)
```

```python
import math

import jax
import jax.numpy as jnp
from jax import lax
from jax.experimental import pallas as pl
from jax.experimental.pallas import tpu as pltpu

F32 = jnp.float32
BF16 = jnp.bfloat16
MESH = pl.DeviceIdType.MESH

EPS = 1e-6
ROPE_THETA = 10000.0
MLA_HEADS = 8
MLA_NOPE = 64
MLA_ROPE = 32
SB_HEADS = 8
HEAD_DIM = 64
Q_LORA = 256
KV_LORA = 128
MLA_SCALE = 1.0 / math.sqrt(MLA_NOPE + MLA_ROPE)
SB_SCALE = 1.0 / math.sqrt(HEAD_DIM)
LOG2E = math.log2(math.e)
SB_DEAD = -160.0
N_DEV = 8

ADAM_LR = 0.001
ADAM_B1 = 0.9
ADAM_B2 = 0.999
ADAM_EPS = 1e-08
ADAM_WD = 0.01
ADAM_STEP = 10

LANES = 128
ATT_TILE = 512
SB_TILE = 512
TRI = 256
ROW_TILE = 512
FFN_BWD_ROW_TILE = 256
PROJ_BWD_ROW_TILE = 256
TN_BLOCK = 256
TN_RESIDENT_BYTES = 16 * 1024 * 1024
VMEM_LIMIT = 56 * 1024 * 1024
NEG = -1e30


def _cparams(*sem):
    return pltpu.CompilerParams(dimension_semantics=sem, vmem_limit_bytes=VMEM_LIMIT)


def _dot(a, b):
    return jnp.dot(a, b, preferred_element_type=F32)


def _dot_nt(a, b):
    return lax.dot_general(a, b, (((1,), (1,)), ((), ())), preferred_element_type=F32)


def _dot_tn(a, b):
    return lax.dot_general(a, b, (((0,), (0,)), ((), ())), preferred_element_type=F32)


def _rms(x, g):
    r = lax.rsqrt(jnp.mean(x * x, axis=-1, keepdims=True) + EPS)
    return x * r * g


def _rms_bwd(x, g, dy):
    r = lax.rsqrt(jnp.mean(x * x, axis=-1, keepdims=True) + EPS)
    n = x * r
    dn = dy * g
    dx = r * (dn - n * jnp.mean(dn * n, axis=-1, keepdims=True))
    return dx, jnp.sum(dy * n, axis=0, keepdims=True)


def _rope(x, cos, sin_a, sin_b):
    return x * cos + pltpu.roll(x, 112, 1) * sin_a + pltpu.roll(x, 16, 1) * sin_b


def _rope_t(g, cos, sin_a, sin_b):
    return g * cos + pltpu.roll(g * sin_a, 16, 1) + pltpu.roll(g * sin_b, 112, 1)


def _row_spec(tm, width):
    return pl.BlockSpec((tm, width), lambda r: (r, 0))


def _full_spec(shape):
    return pl.BlockSpec(shape, lambda *_: (0,) * len(shape))


def _accumulate(ref, val, first):
    @pl.when(first)
    def _():
        ref[...] = val

    @pl.when(jnp.logical_not(first))
    def _():
        ref[...] += val


def _proj_in_fwd(x, g_mix, w_a, g_q, w_uq, g_kv, w_ukv, cos, sin_a, sin_b):
    s, d = x.shape
    tm = min(ROW_TILE, s)

    def body(x_ref, gm_ref, wa_ref, gq_ref, wuq_ref, gkv_ref, wukv_ref, cos_ref, sa_ref, sb_ref,
             u_ref, cq_ref, ckv_ref, cqn_ref, ckvn_ref, qn_ref, qr_ref, kv_ref, kr_ref, sbq_ref):
        u = _rms(x_ref[...], gm_ref[...]).astype(BF16)
        u_ref[...] = u
        cq = _dot(u, wa_ref[:, 0:256])
        ckv = _dot(u, wa_ref[:, 256:384])
        kr = _dot(u, wa_ref[:, 384:512])
        cq_ref[...] = cq
        ckv_ref[...] = ckv
        cqn = _rms(cq, gq_ref[...]).astype(BF16)
        ckvn = _rms(ckv, gkv_ref[...]).astype(BF16)
        cqn_ref[...] = cqn
        ckvn_ref[...] = ckvn
        cos_t, sa_t, sb_t = cos_ref[...], sa_ref[...], sb_ref[...]
        qn_ref[...] = (_dot(cqn, wuq_ref[:, 0:512]) * MLA_SCALE).astype(BF16)
        for half in range(2):
            lo = 512 + half * LANES
            qr = _dot(cqn, wuq_ref[:, lo:lo + LANES])
            qr_ref[:, half * LANES:(half + 1) * LANES] = (_rope(qr, cos_t, sa_t, sb_t) * MLA_SCALE).astype(BF16)
        kv_ref[...] = _dot(ckvn, wukv_ref[...]).astype(BF16)
        krt = kr + pltpu.roll(kr, 32, 1) + pltpu.roll(kr, 64, 1) + pltpu.roll(kr, 96, 1)
        kr_ref[...] = _rope(krt, cos_t, sa_t, sb_t).astype(BF16)
        sbq_ref[:, 0:512] = (_dot(u, wa_ref[:, 512:1024]) * (SB_SCALE * LOG2E)).astype(BF16)
        sbq_ref[:, 512:1536] = _dot(u, wa_ref[:, 1024:2048]).astype(BF16)

    outs = [
        jax.ShapeDtypeStruct((s, d), BF16),
        jax.ShapeDtypeStruct((s, 256), F32),
        jax.ShapeDtypeStruct((s, 128), F32),
        jax.ShapeDtypeStruct((s, 256), BF16),
        jax.ShapeDtypeStruct((s, 128), BF16),
        jax.ShapeDtypeStruct((s, 512), BF16),
        jax.ShapeDtypeStruct((s, 256), BF16),
        jax.ShapeDtypeStruct((s, 1024), BF16),
        jax.ShapeDtypeStruct((s, 128), BF16),
        jax.ShapeDtypeStruct((s, 1536), BF16),
    ]
    return pl.pallas_call(
        body, name="proj_in_fwd", grid=(s // tm,), out_shape=outs,
        in_specs=[_row_spec(tm, d), _full_spec(g_mix.shape), _full_spec(w_a.shape), _full_spec(g_q.shape),
                  _full_spec(w_uq.shape), _full_spec(g_kv.shape), _full_spec(w_ukv.shape),
                  _row_spec(tm, LANES), _row_spec(tm, LANES), _row_spec(tm, LANES)],
        out_specs=[_row_spec(tm, o.shape[1]) for o in outs],
        compiler_params=_cparams("arbitrary"),
    )(x, g_mix, w_a, g_q, w_uq, g_kv, w_ukv, cos, sin_a, sin_b)


def _attn_out_fwd(o_mla, o_sb, g_mla, g_sb, w_o, x, g_ffn):
    s, d = x.shape
    tm = min(ROW_TILE, s)

    def body(oa_ref, ob_ref, ga_ref, gb_ref, wo_ref, x_ref, gf_ref, merged_ref, h1_ref, f_ref):
        na = _rms(oa_ref[...], ga_ref[...]).astype(BF16)
        nb = _rms(ob_ref[...], gb_ref[...]).astype(BF16)
        merged_ref[:, 0:512] = na
        merged_ref[:, 512:1024] = nb
        h1 = x_ref[...] + _dot(na, wo_ref[0:512, :]) + _dot(nb, wo_ref[512:1024, :])
        h1_ref[...] = h1
        f_ref[...] = _rms(h1, gf_ref[...]).astype(BF16)

    outs = [jax.ShapeDtypeStruct((s, d), BF16), jax.ShapeDtypeStruct((s, d), F32), jax.ShapeDtypeStruct((s, d), BF16)]
    return pl.pallas_call(
        body, name="attn_out_fwd", grid=(s // tm,), out_shape=outs,
        in_specs=[_row_spec(tm, 512), _row_spec(tm, 512), _full_spec(g_mla.shape), _full_spec(g_sb.shape),
                  _full_spec(w_o.shape), _row_spec(tm, d), _full_spec(g_ffn.shape)],
        out_specs=[_row_spec(tm, d)] * 3,
        compiler_params=_cparams("arbitrary"),
    )(o_mla, o_sb, g_mla, g_sb, w_o, x, g_ffn)


def _ffn_tile(d_ff):
    return d_ff // 2 if (d_ff // 2) % LANES == 0 else d_ff


def _ffn_fwd(f, h1, w_gate_t, w_up_t, w_down):
    s, d = h1.shape
    d_ff = w_gate_t.shape[0]
    tm = min(ROW_TILE, s)
    tf = _ffn_tile(d_ff)

    def body(f_ref, h1_ref, wgt_ref, wut_ref, wd_ref, gate_ref, up_ref, h2_ref):
        j = pl.program_id(1)
        fb = f_ref[...]
        gate = _dot_nt(fb, wgt_ref[...])
        up = _dot_nt(fb, wut_ref[...])
        gate_ref[...] = gate.astype(BF16)
        up_ref[...] = up.astype(BF16)
        act = (gate * jax.nn.sigmoid(gate) * up).astype(BF16)
        part = _dot(act, wd_ref[...])

        @pl.when(j == 0)
        def _():
            h2_ref[...] = h1_ref[...] + part

        @pl.when(j != 0)
        def _():
            h2_ref[...] += part

    outs = [jax.ShapeDtypeStruct((s, d_ff), BF16), jax.ShapeDtypeStruct((s, d_ff), BF16), jax.ShapeDtypeStruct((s, d), F32)]
    return pl.pallas_call(
        body, name="ffn_fwd", grid=(s // tm, d_ff // tf), out_shape=outs,
        in_specs=[pl.BlockSpec((tm, d), lambda r, j: (r, 0)), pl.BlockSpec((tm, d), lambda r, j: (r, 0)),
                  pl.BlockSpec((tf, d), lambda r, j: (j, 0)), pl.BlockSpec((tf, d), lambda r, j: (j, 0)),
                  pl.BlockSpec((tf, d), lambda r, j: (j, 0))],
        out_specs=[pl.BlockSpec((tm, tf), lambda r, j: (r, j)), pl.BlockSpec((tm, tf), lambda r, j: (r, j)),
                   pl.BlockSpec((tm, d), lambda r, j: (r, 0))],
        compiler_params=_cparams("arbitrary", "arbitrary"),
    )(f, h1, w_gate_t, w_up_t, w_down)


def _final_loss(h2, target, g_final):
    s, d = h2.shape
    tm = min(ROW_TILE, s)

    def body(h2_ref, t_ref, g_ref, loss_ref, dh2_ref, dh2b_ref, dg_ref):
        first = pl.program_id(0) == 0
        h2v = h2_ref[...]
        g = g_ref[...]
        diff = _rms(h2v, g) - t_ref[...]
        part = 0.5 * jnp.sum(jnp.mean(diff * diff, axis=-1, keepdims=True), axis=0, keepdims=True)
        _accumulate(loss_ref, jnp.broadcast_to(part, loss_ref.shape), first)
        dx, dg = _rms_bwd(h2v, g, diff * (1.0 / d))
        dh2_ref[...] = dx
        dh2b_ref[...] = dx.astype(BF16)
        _accumulate(dg_ref, dg, first)

    outs = [jax.ShapeDtypeStruct((1, LANES), F32), jax.ShapeDtypeStruct((s, d), F32), jax.ShapeDtypeStruct((s, d), BF16),
            jax.ShapeDtypeStruct((1, d), F32)]
    return pl.pallas_call(
        body, name="final_loss", grid=(s // tm,), out_shape=outs,
        in_specs=[_row_spec(tm, d), _row_spec(tm, d), _full_spec((1, d))],
        out_specs=[_full_spec((1, LANES)), _row_spec(tm, d), _row_spec(tm, d), _full_spec((1, d))],
        compiler_params=_cparams("arbitrary"),
    )(h2, target, g_final)


def _tile_iotas(t):
    return lax.broadcasted_iota(jnp.int32, (t, t), 0), lax.broadcasted_iota(jnp.int32, (t, t), 1)


def _mla_fwd(qn, qr, kv, kr, shards):
    s = qn.shape[0]
    t = min(ATT_TILE, s)
    pairs = MLA_HEADS // 2
    nq = s // t
    n = len(shards)

    def body(*refs):
        qn_ref, qr_ref, kn_ref, v_ref, kr_ref = refs[:5]
        o_ref, lse_ref = refs[5 + n:7 + n]
        qcat_ref, m_ref, l_ref, acc_ref = refs[7 + 2 * n:11 + 2 * n]
        hp, i = pl.program_id(0), pl.program_id(1)
        ride = _Exchange(True, refs[5:5 + n], refs[7 + n:7 + 2 * n], *refs[11 + 2 * n:])

        @pl.when((hp == 0) & (i == 0))
        def _():
            ride.start()

        lane = lax.broadcasted_iota(jnp.int32, (1, LANES), 1)
        row, col = _tile_iotas(t)
        causal = col <= row
        q_pair, q_quad = qn_ref[...], qr_ref[...]
        zero = jnp.zeros_like(q_pair)
        for hh in range(2):
            in_head = (lane // HEAD_DIM) == hh
            in_rope = (lane // MLA_ROPE) == (hp % 2) * 2 + hh
            qcat_ref[hh * t:(hh + 1) * t, 0:LANES] = jnp.where(in_head, q_pair, zero)
            qcat_ref[hh * t:(hh + 1) * t, LANES:2 * LANES] = jnp.where(in_rope, q_quad, zero)
        m_ref[...] = jnp.full_like(m_ref, NEG)
        l_ref[...] = jnp.zeros_like(l_ref)
        acc_ref[...] = jnp.zeros_like(acc_ref)

        def tile(j, width, masked):
            rows = pl.ds(pl.multiple_of(j * t, t), width * t)
            kcat = jnp.concatenate([kn_ref[rows, :], kr_ref[rows, :]], axis=1)
            v_ones = jnp.concatenate([v_ref[rows, :], jnp.ones((width * t, LANES), BF16)], axis=1)
            scores = [_dot_nt(qcat_ref[hh * t:(hh + 1) * t, :], kcat) for hh in range(2)]
            for hh in range(2):
                half = slice(hh * t, (hh + 1) * t)
                sc = jnp.where(causal, scores[hh], NEG) if masked else scores[hh]
                m = m_ref[half, :]
                m_new = jnp.maximum(m, jnp.max(sc, axis=-1, keepdims=True))
                alpha = jnp.exp(m - m_new)
                p = jnp.exp(sc - jnp.concatenate([m_new] * (width * t // LANES), axis=1))
                pv = _dot(p.astype(BF16), v_ones)
                l_ref[half, :] = alpha * l_ref[half, :] + pv[:, LANES:]
                acc_ref[half, :] = alpha * acc_ref[half, :] + pv[:, :LANES]
                m_ref[half, :] = m_new

        tile(i, 1, True)

        def step(n, carry):
            tile(4 * n, 4, False)
            return carry

        lax.fori_loop(0, i // 4, step, 0)

        @pl.when(i % 4 >= 2)
        def _():
            tile((i // 4) * 4, 2, False)

        @pl.when(i % 2 == 1)
        def _():
            tile(i - 1, 1, False)

        first = (lane // HEAD_DIM) == 0
        o = acc_ref[...] / l_ref[...]
        lse = m_ref[...] + jnp.log(l_ref[...])
        o_ref[...] = jnp.where(first, o[0:t], o[t:2 * t])
        lse_ref[...] = jnp.where(first, lse[0:t], lse[t:2 * t])

        @pl.when((hp == pairs - 1) & (i == nq - 1))
        def _():
            ride.finish()

    gathered_shapes, sems = _exchange_shapes(True, shards)
    outs = [jax.ShapeDtypeStruct((s, 512), F32), jax.ShapeDtypeStruct((pairs, s, LANES), F32)] + gathered_shapes
    res = pl.pallas_call(
        body, name="mla_fwd", grid=(pairs, nq), out_shape=outs,
        in_specs=[pl.BlockSpec((t, LANES), lambda hp, i: (i, hp)), pl.BlockSpec((t, LANES), lambda hp, i: (i, hp // 2)),
                  pl.BlockSpec((s, LANES), lambda hp, i: (0, hp)), pl.BlockSpec((s, LANES), lambda hp, i: (0, 4 + hp)),
                  pl.BlockSpec((s, LANES), lambda hp, i: (0, 0))] + [ANY] * n,
        out_specs=[pl.BlockSpec((t, LANES), lambda hp, i: (i, hp)), pl.BlockSpec((None, t, LANES), lambda hp, i: (hp, i, 0))]
        + [ANY] * n,
        scratch_shapes=[pltpu.VMEM((2 * t, 2 * LANES), BF16), pltpu.VMEM((2 * t, LANES), F32), pltpu.VMEM((2 * t, LANES), F32),
                        pltpu.VMEM((2 * t, LANES), F32)] + sems,
        compiler_params=_cparams("arbitrary", "arbitrary"),
    )(qn, qr, kv, kv, kr, *shards)
    return res[0], res[1], res[2:]


HEADS = (0, 1)


def _sb_logs(z2, strict, masked):
    log_b = jnp.minimum(z2, 0.0) - jnp.log2(1.0 + jnp.exp2(-jnp.abs(z2)))
    log_1m = log_b - z2
    if masked:
        log_1m = jnp.where(strict, log_1m, 0.0)
    return log_1m, log_b


def _block_totals(x):
    t, w = x.shape
    nb = max(w // TRI, 1)
    bw = w // nb
    blocks = [x[:, b * bw:(b + 1) * bw] for b in range(nb)]
    totals = [jnp.broadcast_to(jnp.sum(blk, axis=-1, keepdims=True), (t, LANES)) for blk in blocks]
    whole = totals[0]
    for tot in totals[1:]:
        whole = whole + tot
    return blocks, totals, whole


def _running_sums(blocks, totals, tri, carry, suffix):
    nb = len(blocks)
    reps = blocks[0].shape[1] // LANES
    outs = [None] * nb
    run = carry
    for b in (range(nb - 1, -1, -1) if suffix else range(nb)):
        outs[b] = _dot(blocks[b].astype(BF16), tri) + jnp.concatenate([run] * reps, axis=1)
        run = run + totals[b]
    return outs[0] if nb == 1 else jnp.concatenate(outs, axis=1)


def _tri(t, rel):
    n = min(TRI, t)
    row, col = _tile_iotas(n)
    return rel(row, col).astype(BF16)


def _sweep_width(t):
    return t // 2 if t // 2 >= TRI else t


def _sb_fwd(qkv):
    s = qkv.shape[0]
    t = min(SB_TILE, s)
    sw = _sweep_width(t)
    pairs = SB_HEADS // 2

    def body(q_ref, k_ref, v_ref, o_ref, tot_ref, cnt_ref, qm_ref, right_ref, acc_ref):
        i = pl.program_id(1)
        lane = lax.broadcasted_iota(jnp.int32, (1, LANES), 1)
        row, col = _tile_iotas(t)
        strict = col < row
        t_suffix = _tri(t, lambda r, c: r > c)
        q_pair = q_ref[...]
        for hh in range(2):
            qm_ref[hh] = jnp.where((lane // HEAD_DIM) == hh, q_pair, jnp.zeros_like(q_pair))
        right_ref[...] = jnp.zeros_like(right_ref)
        acc_ref[...] = jnp.zeros_like(acc_ref)

        def tile(start, width, masked):
            rows = pl.ds(pl.multiple_of(start, width), width)
            k, v = k_ref[rows, :], v_ref[rows, :]
            for hh in HEADS:
                log_1m, log_b = _sb_logs(_dot_nt(qm_ref[hh], k), strict, masked)
                blocks, totals, whole = _block_totals(log_1m)
                a = jnp.exp2(log_b + _running_sums(blocks, totals, t_suffix, right_ref[hh], True))
                if masked:
                    a = jnp.where(strict, a, 0.0)
                right_ref[hh] += whole
                acc_ref[hh] += _dot(a.astype(BF16), v)

        tile(i * t, t, True)

        def alive(n):
            return (n < i * (t // sw)) & (jnp.max(right_ref[...]) > SB_DEAD)

        def step(n):
            tile((i * (t // sw) - 1 - n) * sw, sw, False)
            return n + 1

        swept = lax.while_loop(alive, step, jnp.int32(0))
        cnt_ref[...] = jnp.full(cnt_ref.shape, swept.astype(F32))
        first = (lane // HEAD_DIM) == 0
        o_ref[...] = jnp.where(first, acc_ref[0], acc_ref[1])
        tot_ref[...] = jnp.where(first, right_ref[0], right_ref[1])

    outs = [jax.ShapeDtypeStruct((s, 512), F32), jax.ShapeDtypeStruct((pairs, s, LANES), F32),
            jax.ShapeDtypeStruct((pairs, s // t, 8, LANES), F32)]
    return pl.pallas_call(
        body, name="sb_fwd", grid=(pairs, s // t), out_shape=outs,
        in_specs=[pl.BlockSpec((t, LANES), lambda hp, i: (i, hp)), pl.BlockSpec((s, LANES), lambda hp, i: (0, 4 + hp)),
                  pl.BlockSpec((s, LANES), lambda hp, i: (0, 8 + hp))],
        out_specs=[pl.BlockSpec((t, LANES), lambda hp, i: (i, hp)), pl.BlockSpec((None, t, LANES), lambda hp, i: (hp, i, 0)),
                   pl.BlockSpec((None, None, 8, LANES), lambda hp, i: (hp, i, 0, 0))],
        scratch_shapes=[pltpu.VMEM((2, t, LANES), BF16), pltpu.VMEM((2, t, LANES), F32), pltpu.VMEM((2, t, LANES), F32)],
        compiler_params=_cparams("arbitrary", "arbitrary"),
    )(qkv, qkv, qkv)


def _sb_bwd(qkv, do, tot, cnt):
    s = qkv.shape[0]
    t = min(SB_TILE, s)
    sw = _sweep_width(t)
    pairs = SB_HEADS // 2

    def body(q_ref, k_ref, v_ref, do_ref, tot_ref, cnt_ref, dq_ref, dk_ref, dv_ref,
             qm_ref, dob_ref, total_s, left_l, left_g, dq_s, dk_s, dv_s):
        i = pl.program_id(1)

        @pl.when(i == 0)
        def _():
            dk_s[...] = jnp.zeros_like(dk_s)
            dv_s[...] = jnp.zeros_like(dv_s)

        lane = lax.broadcasted_iota(jnp.int32, (1, LANES), 1)
        row, col = _tile_iotas(t)
        strict = col < row
        t_suffix = _tri(t, lambda r, c: r > c)
        t_excl = _tri(t, lambda r, c: r < c)
        q_pair, do_pair, tot_pair = q_ref[...], do_ref[...], tot_ref[...]
        for hh in range(2):
            in_head = (lane // HEAD_DIM) == hh
            qm_ref[hh] = jnp.where(in_head, q_pair, jnp.zeros_like(q_pair))
            dob_ref[hh] = jnp.where(in_head, do_pair, 0.0).astype(BF16)
            total_s[hh] = jnp.broadcast_to(
                jnp.sum(jnp.where(lane == hh * HEAD_DIM, tot_pair, 0.0), axis=-1, keepdims=True), (t, LANES))
        left_l[...] = jnp.zeros_like(left_l)
        left_g[...] = jnp.zeros_like(left_g)
        dq_s[...] = jnp.zeros_like(dq_s)

        def tile(start, width, masked):
            rows = pl.ds(pl.multiple_of(start, width), width)
            k, v = k_ref[rows, :], v_ref[rows, :]
            z2 = [_dot_nt(qm_ref[hh], k) for hh in HEADS]
            d_a = [_dot_nt(dob_ref[hh], v) for hh in HEADS]
            for hh in HEADS:
                qm, dob = qm_ref[hh], dob_ref[hh]
                log_1m, log_b = _sb_logs(z2[hh], strict, masked)
                blocks, totals, whole = _block_totals(log_1m)
                done = left_l[hh] + whole
                left_l[hh] = done
                a = jnp.exp2(log_b + _running_sums(blocks, totals, t_suffix, total_s[hh] - done, True))
                if masked:
                    a = jnp.where(strict, a, 0.0)
                g = a * d_a[hh]
                blocks, totals, whole = _block_totals(g)
                before = _running_sums(blocks, totals, t_excl, left_g[hh], False)
                left_g[hh] += whole
                dz = g - jnp.exp2(log_b) * (g + before)
                if masked:
                    dz = jnp.where(strict, dz, 0.0)
                dzb = dz.astype(BF16)
                dq_s[hh] += _dot(dzb, k)
                dk_s[rows, :] += _dot_tn(dzb, qm)
                dv_s[rows, :] += _dot_tn(a.astype(BF16), dob)

        def step(h, carry):
            tile(h * sw, sw, False)
            return carry

        swept = jnp.max(cnt_ref[...]).astype(jnp.int32)
        lax.fori_loop(i * (t // sw) - swept, i * (t // sw), step, 0)
        tile(i * t, t, True)
        dq_ref[...] = (jnp.where((lane // HEAD_DIM) == 0, dq_s[0], dq_s[1]) * SB_SCALE).astype(BF16)

        @pl.when(i == s // t - 1)
        def _():
            dk_ref[...] = (dk_s[...] * (1.0 / LOG2E)).astype(BF16)
            dv_ref[...] = dv_s[...].astype(BF16)

    outs = [jax.ShapeDtypeStruct((s, 512), BF16)] * 3
    return pl.pallas_call(
        body, name="sb_bwd", grid=(pairs, s // t), out_shape=outs,
        in_specs=[pl.BlockSpec((t, LANES), lambda hp, i: (i, hp)), pl.BlockSpec((s, LANES), lambda hp, i: (0, 4 + hp)),
                  pl.BlockSpec((s, LANES), lambda hp, i: (0, 8 + hp)), pl.BlockSpec((t, LANES), lambda hp, i: (i, hp)),
                  pl.BlockSpec((None, t, LANES), lambda hp, i: (hp, i, 0)),
                  pl.BlockSpec((None, None, 8, LANES), lambda hp, i: (hp, i, 0, 0))],
        out_specs=[pl.BlockSpec((t, LANES), lambda hp, i: (i, hp)), pl.BlockSpec((s, LANES), lambda hp, i: (0, hp)),
                   pl.BlockSpec((s, LANES), lambda hp, i: (0, hp))],
        scratch_shapes=[pltpu.VMEM((2, t, LANES), BF16), pltpu.VMEM((2, t, LANES), BF16)]
        + [pltpu.VMEM((2, t, LANES), F32)] * 4 + [pltpu.VMEM((s, LANES), F32)] * 2,
        compiler_params=_cparams("arbitrary", "arbitrary"),
    )(qkv, qkv, qkv, do, tot, cnt)


def _mla_bwd(qn, qr, kv, kr, do, o, lse, parts):
    s = qn.shape[0]
    t = min(ATT_TILE, s)
    pairs = MLA_HEADS // 2
    nq = s // t
    n = len(parts)

    def body(*refs):
        qn_ref, qr_ref, kn_ref, v_ref, kr_ref, do_ref, o_ref, lse_ref = refs[:8]
        dqn_ref, dqr_ref, dkn_ref, dv_ref, dkr_ref = refs[8 + n:13 + n]
        qcat_ref, dob_ref, lse_s, delta_s, dq_s, dkn_s, dv_s, dkr_s = refs[13 + 2 * n:21 + 2 * n]
        hp, i = pl.program_id(0), pl.program_id(1)
        ride = _Exchange(False, refs[8:8 + n], refs[13 + n:13 + 2 * n], *refs[21 + 2 * n:])

        @pl.when((hp == 0) & (i == 0))
        def _():
            ride.start()

        @pl.when(i == 0)
        def _():
            dkn_s[...] = jnp.zeros_like(dkn_s)
            dv_s[...] = jnp.zeros_like(dv_s)
            dkr_s[...] = jnp.zeros_like(dkr_s)

        lane = lax.broadcasted_iota(jnp.int32, (1, LANES), 1)
        row, col = _tile_iotas(t)
        causal = col <= row
        q_pair, q_quad, do_pair, lse_pair = qn_ref[...], qr_ref[...], do_ref[...], lse_ref[...]
        do_o = do_pair * o_ref[...]
        zero = jnp.zeros_like(q_pair)
        ropes = []
        for hh in range(2):
            in_head = (lane // HEAD_DIM) == hh
            in_rope = (lane // MLA_ROPE) == (hp % 2) * 2 + hh
            ropes.append(in_rope)
            qcat_ref[hh, :, 0:LANES] = jnp.where(in_head, q_pair, zero)
            qcat_ref[hh, :, LANES:2 * LANES] = jnp.where(in_rope, q_quad, zero)
            dob_ref[hh] = jnp.where(in_head, do_pair, 0.0).astype(BF16)
            delta_s[hh] = jnp.broadcast_to(jnp.sum(jnp.where(in_head, do_o, 0.0), axis=-1, keepdims=True), (t, LANES))
            lse_s[hh] = jnp.broadcast_to(
                jnp.sum(jnp.where(lane == hh * HEAD_DIM, lse_pair, 0.0), axis=-1, keepdims=True), (t, LANES))
        dq_s[...] = jnp.zeros_like(dq_s)
        reps = t // LANES

        def tile(j, width, masked):
            rows = pl.ds(pl.multiple_of(j * t, t), width * t)
            kcat = jnp.concatenate([kn_ref[rows, :], kr_ref[rows, :]], axis=1)
            v = v_ref[rows, :]
            sc = [_dot_nt(qcat_ref[hh], kcat) for hh in HEADS]
            dp = [_dot_nt(dob_ref[hh], v) for hh in HEADS]
            p = [jnp.exp(sc[hh] - jnp.concatenate([lse_s[hh]] * (width * reps), axis=1)) for hh in HEADS]
            if masked:
                p = [jnp.where(causal, p[hh], 0.0) for hh in HEADS]
            ds = [(p[hh] * (dp[hh] - jnp.concatenate([delta_s[hh]] * (width * reps), axis=1))).astype(BF16) for hh in HEADS]
            for hh in HEADS:
                dq_s[hh] += _dot(ds[hh], kcat)
            dkcat = _dot_tn(ds[0], qcat_ref[0]) + _dot_tn(ds[1], qcat_ref[1])
            dkn_s[rows, :] += dkcat[:, 0:LANES]
            dkr_s[rows, :] += dkcat[:, LANES:2 * LANES]
            dv_s[rows, :] += _dot_tn(p[0].astype(BF16), dob_ref[0]) + _dot_tn(p[1].astype(BF16), dob_ref[1])

        def step(n, carry):
            tile(4 * n, 4, False)
            return carry

        lax.fori_loop(0, i // 4, step, 0)

        @pl.when(i % 4 >= 2)
        def _():
            tile((i // 4) * 4, 2, False)

        @pl.when(i % 2 == 1)
        def _():
            tile(i - 1, 1, False)

        tile(i, 1, True)
        dqn_ref[...] = (jnp.where((lane // HEAD_DIM) == 0, dq_s[0, :, 0:LANES], dq_s[1, :, 0:LANES]) * MLA_SCALE).astype(BF16)
        dqr_ref[...] = ((jnp.where(ropes[0], dq_s[0, :, LANES:2 * LANES], 0.0)
                         + jnp.where(ropes[1], dq_s[1, :, LANES:2 * LANES], 0.0)) * MLA_SCALE).astype(BF16)

        @pl.when(i == nq - 1)
        def _():
            dkn_ref[...] = dkn_s[...].astype(BF16)
            dv_ref[...] = dv_s[...].astype(BF16)
            dkr_ref[...] = dkr_s[...].astype(BF16)

        @pl.when((hp == pairs - 1) & (i == nq - 1))
        def _():
            ride.finish()

    pair_block = pl.BlockSpec((t, LANES), lambda hp, i: (i, hp))
    once = pl.Buffered(1)
    landed_shapes, sems = _exchange_shapes(False, parts)
    outs = [jax.ShapeDtypeStruct((s, 512), BF16), jax.ShapeDtypeStruct((pairs, s, LANES), BF16),
            jax.ShapeDtypeStruct((s, 512), BF16), jax.ShapeDtypeStruct((s, 512), BF16),
            jax.ShapeDtypeStruct((pairs, s, LANES), BF16)] + landed_shapes
    res = pl.pallas_call(
        body, name="mla_bwd", grid=(pairs, nq), out_shape=outs,
        in_specs=[pair_block, pl.BlockSpec((t, LANES), lambda hp, i: (i, hp // 2)),
                  pl.BlockSpec((s, LANES), lambda hp, i: (0, hp), pipeline_mode=once),
                  pl.BlockSpec((s, LANES), lambda hp, i: (0, 4 + hp), pipeline_mode=once),
                  pl.BlockSpec((s, LANES), lambda hp, i: (0, 0), pipeline_mode=once), pair_block, pair_block,
                  pl.BlockSpec((None, t, LANES), lambda hp, i: (hp, i, 0))] + [ANY] * n,
        out_specs=[pair_block, pl.BlockSpec((None, t, LANES), lambda hp, i: (hp, i, 0)),
                   pl.BlockSpec((s, LANES), lambda hp, i: (0, hp), pipeline_mode=once),
                   pl.BlockSpec((s, LANES), lambda hp, i: (0, hp), pipeline_mode=once),
                   pl.BlockSpec((None, s, LANES), lambda hp, i: (hp, 0, 0), pipeline_mode=once)] + [ANY] * n,
        scratch_shapes=[pltpu.VMEM((2, t, 2 * LANES), BF16), pltpu.VMEM((2, t, LANES), BF16), pltpu.VMEM((2, t, LANES), F32),
                        pltpu.VMEM((2, t, LANES), F32), pltpu.VMEM((2, t, 2 * LANES), F32)]
        + [pltpu.VMEM((s, LANES), F32)] * 3 + sems,
        compiler_params=_cparams("arbitrary", "arbitrary"),
    )(qn, qr, kv, kv, kr, do, o, lse, *parts)
    return res[:5], res[5:]


def _ffn_bwd(dh2, dh2b, gate, up, h1, g_ffn, w_down, w_gate_t, w_up_t):
    s, d = h1.shape
    d_ff = gate.shape[1]
    tm = min(FFN_BWD_ROW_TILE, s)
    tf = _ffn_tile(d_ff)

    def act_body(dh2b_ref, gate_ref, up_ref, wd_ref, dgate_ref, dup_ref, act_ref):
        dact = _dot_nt(dh2b_ref[...], wd_ref[...])
        gate_v = gate_ref[...].astype(F32)
        up_v = up_ref[...].astype(F32)
        sig = jax.nn.sigmoid(gate_v)
        silu = gate_v * sig
        dup_ref[...] = (dact * silu).astype(BF16)
        dgate_ref[...] = ((dact * up_v) * (sig * (1.0 + gate_v - silu))).astype(BF16)
        act_ref[...] = (silu * up_v).astype(BF16)

    ff = pl.BlockSpec((tm, tf), lambda j, r: (r, j))
    dgate, dup, act = pl.pallas_call(
        act_body, name="ffn_bwd_act", grid=(d_ff // tf, s // tm), out_shape=[jax.ShapeDtypeStruct((s, d_ff), BF16)] * 3,
        in_specs=[pl.BlockSpec((tm, d), lambda j, r: (r, 0)), ff, ff, pl.BlockSpec((tf, d), lambda j, r: (j, 0))],
        out_specs=[ff, ff, ff],
        compiler_params=_cparams("arbitrary", "arbitrary"),
    )(dh2b, gate, up, w_down)

    def df_body(dgate_ref, dup_ref, dh2_ref, h1_ref, g_ref, wgt_ref, wut_ref, dh1_ref, dh1b_ref, dg_ref):
        df = _dot(dgate_ref[...], wgt_ref[...]) + _dot(dup_ref[...], wut_ref[...])
        dx, dg = _rms_bwd(h1_ref[...], g_ref[...], df)
        dh1 = dh2_ref[...] + dx
        dh1_ref[...] = dh1
        dh1b_ref[...] = dh1.astype(BF16)
        _accumulate(dg_ref, dg, pl.program_id(0) == 0)

    outs = [jax.ShapeDtypeStruct((s, d), F32), jax.ShapeDtypeStruct((s, d), BF16), jax.ShapeDtypeStruct((1, d), F32)]
    dh1, dh1b, dg = pl.pallas_call(
        df_body, name="ffn_bwd_df", grid=(s // tm,), out_shape=outs,
        in_specs=[_row_spec(tm, d_ff), _row_spec(tm, d_ff), _row_spec(tm, d), _row_spec(tm, d), _full_spec((1, d)),
                  _full_spec(w_gate_t.shape), _full_spec(w_up_t.shape)],
        out_specs=[_row_spec(tm, d), _row_spec(tm, d), _full_spec((1, d))],
        compiler_params=_cparams("arbitrary"),
    )(dgate, dup, dh2, h1, g_ffn, w_gate_t, w_up_t)
    return dgate, dup, act, dh1, dh1b, dg


def _tn_matmul(a, b, name):
    assert a.dtype == BF16 and b.dtype == BF16
    s, m = a.shape
    n = b.shape[1]
    if s * m * 2 <= TN_RESIDENT_BYTES:
        tm, tn = m, min(n, TN_BLOCK)
    else:
        tm, tn = TN_BLOCK, n

    def body(a_ref, b_ref, o_ref):
        o_ref[...] = _dot_tn(a_ref[...], b_ref[...]).astype(BF16)

    return pl.pallas_call(
        body, name=name, grid=(m // tm, n // tn), out_shape=jax.ShapeDtypeStruct((m, n), BF16),
        in_specs=[pl.BlockSpec((s, tm), lambda i, j: (0, i)), pl.BlockSpec((s, tn), lambda i, j: (0, j))],
        out_specs=pl.BlockSpec((tm, tn), lambda i, j: (i, j)),
        compiler_params=_cparams("arbitrary", "arbitrary"),
    )(a, b)


def _attn_out_bwd(dh1, w_o, o_mla, o_sb, g_mla, g_sb):
    s, d = dh1.shape
    tm = min(ROW_TILE, s)

    def body(dh1_ref, wo_ref, oa_ref, ob_ref, ga_ref, gb_ref, doa_ref, dob_ref, dga_ref, dgb_ref):
        first = pl.program_id(0) == 0
        dh1b = dh1_ref[...]
        dxa, dga = _rms_bwd(oa_ref[...], ga_ref[...], _dot_nt(dh1b, wo_ref[0:512, :]))
        dxb, dgb = _rms_bwd(ob_ref[...], gb_ref[...], _dot_nt(dh1b, wo_ref[512:1024, :]))
        doa_ref[...] = dxa
        dob_ref[...] = dxb
        _accumulate(dga_ref, dga, first)
        _accumulate(dgb_ref, dgb, first)

    outs = [jax.ShapeDtypeStruct((s, 512), F32)] * 2 + [jax.ShapeDtypeStruct((1, 512), F32)] * 2
    return pl.pallas_call(
        body, name="attn_out_bwd", grid=(s // tm,), out_shape=outs,
        in_specs=[_row_spec(tm, d), _full_spec(w_o.shape), _row_spec(tm, 512), _row_spec(tm, 512),
                  _full_spec((1, 512)), _full_spec((1, 512))],
        out_specs=[_row_spec(tm, 512), _row_spec(tm, 512), _full_spec((1, 512)), _full_spec((1, 512))],
        compiler_params=_cparams("arbitrary"),
    )(dh1, w_o, o_mla, o_sb, g_mla, g_sb)


def _proj_in_bwd(dqn, dqr, dkn, dv, dkr, dq_sb, dk_sb, dv_sb, cq, ckv, x, dh1, cos, sin_a, sin_b,
                 g_q, g_kv, g_mix, w_uq, w_ukv, w_a):
    s, d = x.shape
    tm = min(PROJ_BWD_ROW_TILE, s)

    def body(dqn_ref, dqr_ref, dkn_ref, dv_ref, dkr_ref, dqs_ref, dks_ref, dvs_ref, cq_ref, ckv_ref, x_ref, dh1_ref,
             cos_ref, sa_ref, sb_ref, gq_ref, gkv_ref, gm_ref, wuq_ref, wukv_ref, wa_ref,
             dx_ref, dproj_ref, dq_ref, dkv_ref, dgq_ref, dgkv_ref, dgm_ref):
        first = pl.program_id(0) == 0
        lane = lax.broadcasted_iota(jnp.int32, (1, LANES), 1)
        cos_t, sa_t, sb_t = cos_ref[...], sa_ref[...], sb_ref[...]
        dq_ref[:, 0:512] = dqn_ref[...]
        for half in range(2):
            quad = dqr_ref[2 * half].astype(F32) + dqr_ref[2 * half + 1].astype(F32)
            dq_ref[:, 512 + half * LANES:512 + (half + 1) * LANES] = _rope_t(quad, cos_t, sa_t, sb_t).astype(BF16)
        dcq, dgq = _rms_bwd(cq_ref[...], gq_ref[...], _dot_nt(dq_ref[...], wuq_ref[...]))
        _accumulate(dgq_ref, dgq, first)
        dkv_ref[:, 0:512] = dkn_ref[...]
        dkv_ref[:, 512:1024] = dv_ref[...]
        dckv, dgkv = _rms_bwd(ckv_ref[...], gkv_ref[...], _dot_nt(dkv_ref[...], wukv_ref[...]))
        _accumulate(dgkv_ref, dgkv, first)
        pairs_sum = (dkr_ref[0].astype(F32) + dkr_ref[1].astype(F32)) + (dkr_ref[2].astype(F32) + dkr_ref[3].astype(F32))
        g = _rope_t(pairs_sum, cos_t, sa_t, sb_t)
        g = g + pltpu.roll(g, 96, 1) + pltpu.roll(g, 64, 1) + pltpu.roll(g, 32, 1)
        dproj_ref[:, 0:256] = dcq.astype(BF16)
        dproj_ref[:, 256:384] = dckv.astype(BF16)
        dproj_ref[:, 384:512] = jnp.where(lane < MLA_ROPE, g, 0.0).astype(BF16)
        dproj_ref[:, 512:1024] = dqs_ref[...]
        dproj_ref[:, 1024:1536] = dks_ref[...]
        dproj_ref[:, 1536:2048] = dvs_ref[...]
        dxn, dgm = _rms_bwd(x_ref[...], gm_ref[...], _dot_nt(dproj_ref[...], wa_ref[...]))
        dx_ref[...] = dh1_ref[...] + dxn
        _accumulate(dgm_ref, dgm, first)

    quad_spec = pl.BlockSpec((4, tm, LANES), lambda r: (0, r, 0))
    outs = [jax.ShapeDtypeStruct((s, d), F32), jax.ShapeDtypeStruct((s, 2048), BF16), jax.ShapeDtypeStruct((s, 768), BF16),
            jax.ShapeDtypeStruct((s, 1024), BF16), jax.ShapeDtypeStruct((1, 256), F32), jax.ShapeDtypeStruct((1, 128), F32),
            jax.ShapeDtypeStruct((1, d), F32)]
    return pl.pallas_call(
        body, name="proj_in_bwd", grid=(s // tm,), out_shape=outs,
        in_specs=[_row_spec(tm, 512), quad_spec, _row_spec(tm, 512), _row_spec(tm, 512), quad_spec,
                  _row_spec(tm, 512), _row_spec(tm, 512), _row_spec(tm, 512), _row_spec(tm, 256), _row_spec(tm, 128),
                  _row_spec(tm, d), _row_spec(tm, d), _row_spec(tm, LANES), _row_spec(tm, LANES), _row_spec(tm, LANES),
                  _full_spec((1, 256)), _full_spec((1, 128)), _full_spec((1, d)),
                  _full_spec(w_uq.shape), _full_spec(w_ukv.shape), _full_spec(w_a.shape)],
        out_specs=[_row_spec(tm, d), _row_spec(tm, 2048), _row_spec(tm, 768), _row_spec(tm, 1024),
                   _full_spec((1, 256)), _full_spec((1, 128)), _full_spec((1, d))],
        compiler_params=_cparams("arbitrary"),
    )(dqn, dqr, dkn, dv, dkr, dq_sb, dk_sb, dv_sb, cq, ckv, x, dh1, cos, sin_a, sin_b, g_q, g_kv, g_mix,
      w_uq, w_ukv, w_a)


ANY = pl.BlockSpec(memory_space=pl.ANY)


def _place():
    return lax.axis_index("x"), lax.axis_index("y"), lax.axis_index("c")


def _all_gather(shards, name):
    n = len(shards)

    def body(*refs):
        ins, outs = refs[:n], refs[n:2 * n]
        send_sems, recv_sems, local_sems = refs[2 * n:]
        x, y, c = _place()
        me, sibling = (x, y, c), (x, y, 1 - c)
        chips = [(1 - x, y), (x, 1 - y), (1 - x, 1 - y)]

        def slot(a, px, py, pc):
            return outs[a].at[4 * px + 2 * py + pc]

        def copy(a, k, block, to, src=None):
            return pltpu.make_async_remote_copy(
                src_ref=slot(a, *block) if src is None else src, dst_ref=slot(a, *block),
                send_sem=send_sems.at[a, k], recv_sem=recv_sems.at[a, k], device_id=to, device_id_type=MESH)

        mine, first, passed = [], [], []
        for a in range(n):
            own = pltpu.make_async_copy(ins[a], slot(a, *me), local_sems.at[a])
            own.start()
            mine.append(own)
            cps = [copy(a, 0, me, sibling, src=ins[a])]
            cps += [copy(a, 1 + j, me, (*chip, c), src=ins[a]) for j, chip in enumerate(chips)]
            for cp in cps:
                cp.start()
            first += cps
        for a in range(n):
            for j, chip in enumerate(chips):
                copy(a, 1 + j, (*chip, c), me).wait_recv()
                fwd = copy(a, 4 + j, (*chip, c), sibling)
                fwd.start()
                passed.append(fwd)
        for a in range(n):
            copy(a, 0, sibling, me).wait_recv()
            for j, chip in enumerate(chips):
                copy(a, 4 + j, (*chip, 1 - c), me).wait_recv()
        for cp in first + passed:
            cp.wait_send()
        for own in mine:
            own.wait()

    return pl.pallas_call(
        body, name=name,
        out_shape=[jax.ShapeDtypeStruct((N_DEV,) + v.shape, v.dtype) for v in shards],
        in_specs=[ANY] * n, out_specs=[ANY] * n,
        scratch_shapes=[pltpu.SemaphoreType.DMA((n, 7)), pltpu.SemaphoreType.DMA((n, 7)), pltpu.SemaphoreType.DMA((n,))],
    )(*shards)


class _Exchange:
    def __init__(self, gather, ins, outs, send_sems, recv_sems, local_sems):
        self.gather, self.ins, self.outs = gather, ins, outs
        self.sems = (send_sems, recv_sems, local_sems)
        x, y, c = _place()
        self.me = 4 * x + 2 * y + c
        self.peers = []
        for k in range(1, N_DEV):
            px = 1 - x if k & 4 else x
            py = 1 - y if k & 2 else y
            pc = 1 - c if k & 1 else c
            self.peers.append(((px, py, pc), 4 * px + 2 * py + pc))

    def _remote(self, a, k, landing):
        send_sems, recv_sems, _ = self.sems
        where, number = self.peers[k]
        src = self.ins[a] if self.gather else self.ins[a].at[number]
        return pltpu.make_async_remote_copy(
            src_ref=src, dst_ref=self.outs[a].at[landing], send_sem=send_sems.at[a, k], recv_sem=recv_sems.at[a, k],
            device_id=where, device_id_type=MESH)

    def _local(self, a):
        src = self.ins[a] if self.gather else self.ins[a].at[self.me]
        return pltpu.make_async_copy(src, self.outs[a].at[self.me], self.sems[2].at[a])

    def start(self):
        for a in range(len(self.ins)):
            self._local(a).start()
            for k in range(N_DEV - 1):
                self._remote(a, k, self.me).start()

    def finish(self):
        for a in range(len(self.ins)):
            for k in range(N_DEV - 1):
                self._remote(a, k, self.peers[k][1]).wait_recv()
            for k in range(N_DEV - 1):
                self._remote(a, k, self.me).wait_send()
            self._local(a).wait()


def _exchange_shapes(gather, arrays):
    out_shape = [jax.ShapeDtypeStruct(((N_DEV,) + v.shape) if gather else v.shape, v.dtype) for v in arrays]
    n = len(arrays)
    sems = [pltpu.SemaphoreType.DMA((n, N_DEV - 1)), pltpu.SemaphoreType.DMA((n, N_DEV - 1)), pltpu.SemaphoreType.DMA((n,))]
    return out_shape, sems


def _exchange(gathers, scatters, name):
    ng, ns = len(gathers), len(scatters)
    n = ng + ns

    def body(*refs):
        ins, outs, sems = refs[:n], refs[n:2 * n], refs[2 * n:]
        both = [_Exchange(True, ins[:ng], outs[:ng], *sems[:3]), _Exchange(False, ins[ng:], outs[ng:], *sems[3:])]
        for ex in both:
            ex.start()
        for ex in both:
            ex.finish()

    g_shapes, g_sems = _exchange_shapes(True, gathers)
    s_shapes, s_sems = _exchange_shapes(False, scatters)
    res = pl.pallas_call(body, name=name, out_shape=g_shapes + s_shapes, in_specs=[ANY] * n, out_specs=[ANY] * n,
                         scratch_shapes=g_sems + s_sems)(*gathers, *scatters)
    return res[:ng], res[ng:]


def _grad_row_tile(rows):
    return _largest_tile_rows(rows, 256)


def _largest_tile_rows(rows, cap):
    for cand in range(cap, 0, -8):
        if rows % cand == 0:
            return cand
    return rows


def _adamw_math(w, g, m, v):
    m_new = ADAM_B1 * m + (1.0 - ADAM_B1) * g
    v_new = ADAM_B2 * v + (1.0 - ADAM_B2) * (g * g)
    m_hat = m_new / (1.0 - ADAM_B1 ** ADAM_STEP)
    v_hat = v_new / (1.0 - ADAM_B2 ** ADAM_STEP)
    delta = -ADAM_LR * (m_hat / (jnp.sqrt(v_hat) + ADAM_EPS) + ADAM_WD * w)
    return delta, m_new, v_new


def _adamw(slots, w, m, v, name):
    k, r, cdim = slots.shape
    tr = _grad_row_tile(r)

    def body(s_ref, w_ref, m_ref, v_ref, g_ref, d_ref, mo_ref, vo_ref):
        g = s_ref[0].astype(F32)
        for q in range(1, k):
            g = g + s_ref[q].astype(F32)
        g_ref[...] = g
        d_ref[...], mo_ref[...], vo_ref[...] = _adamw_math(w_ref[...], g, m_ref[...], v_ref[...])

    blk = pl.BlockSpec((tr, cdim), lambda i: (i, 0))
    return pl.pallas_call(
        body, name=name, grid=(r // tr,), out_shape=[jax.ShapeDtypeStruct((r, cdim), F32)] * 4,
        in_specs=[pl.BlockSpec((k, tr, cdim), lambda i: (0, i, 0)), blk, blk, blk], out_specs=[blk] * 4,
        compiler_params=_cparams("arbitrary"),
    )(slots, w, m, v)


def _stack_cols(g):
    n, r, c = g.shape
    return g.transpose(1, 0, 2).reshape(r, n * c)


def _split_cols(w):
    r, nc = w.shape
    return w.reshape(r, N_DEV, nc // N_DEV).transpose(1, 0, 2)


def _rope_tables(positions):
    inv_freq = ROPE_THETA ** (-jnp.arange(0, MLA_ROPE, 2, dtype=F32) / MLA_ROPE)
    ang = positions.astype(F32).reshape(-1, 1) * inv_freq[None, :]
    cos, sin, zero = jnp.cos(ang), jnp.sin(ang), jnp.zeros_like(ang)
    reps = LANES // MLA_ROPE
    return (jnp.tile(jnp.concatenate([cos, cos], axis=1), (1, reps)),
            jnp.tile(jnp.concatenate([-sin, zero], axis=1), (1, reps)),
            jnp.tile(jnp.concatenate([zero, sin], axis=1), (1, reps)))


def _local_step(x, positions, loss_target, gains, g_in, g_uq, g_ukv, late_shards):
    norm_mix, q_norm, kv_norm, out_mla, out_sb, norm_ffn, norm_final = gains
    d = x.shape[1]
    w_in = _stack_cols(g_in)
    w_a = jnp.concatenate([w_in[:, :416], jnp.zeros((d, 96), BF16), w_in[:, 416:]], axis=1)
    w_uq = jnp.concatenate([g_uq[:, :, :MLA_NOPE].transpose(1, 0, 2).reshape(Q_LORA, -1),
                            g_uq[:, :, MLA_NOPE:].transpose(1, 0, 2).reshape(Q_LORA, -1)], axis=1)
    w_ukv = jnp.concatenate([g_ukv[:, :, :MLA_NOPE].transpose(1, 0, 2).reshape(KV_LORA, -1),
                             g_ukv[:, :, MLA_NOPE:].transpose(1, 0, 2).reshape(KV_LORA, -1)], axis=1)
    cos, sin_a, sin_b = _rope_tables(positions)

    u, cq, ckv, cqn, ckvn, qn, qr, kv, kr, qkv_sb = _proj_in_fwd(x, norm_mix, w_a, q_norm, w_uq, kv_norm, w_ukv, cos, sin_a, sin_b)
    o_mla, lse, (g_o, g_gate, g_up, g_down) = _mla_fwd(qn, qr, kv, kr, late_shards)
    w_o = g_o.reshape(-1, d)
    w_gate_t, w_up_t = g_gate.reshape(-1, d), g_up.reshape(-1, d)
    w_down = g_down.reshape(-1, d)
    o_sb, tot, swept = _sb_fwd(qkv_sb)
    merged, h1, f = _attn_out_fwd(o_mla, o_sb, out_mla, out_sb, w_o, x, norm_ffn)
    gate, up, h2 = _ffn_fwd(f, h1, w_gate_t, w_up_t, w_down)
    loss, dh2, dh2b, dg_final = _final_loss(h2, loss_target, norm_final.reshape(1, d))

    dgate, dup, act, dh1, dh1b, dg_ffn = _ffn_bwd(dh2, dh2b, gate, up, h1, norm_ffn, w_down, w_gate_t, w_up_t)
    dw_down = _tn_matmul(act, dh2b, "dw_down")
    dw_gate_t = _tn_matmul(dgate, f, "dw_gate")
    dw_up_t = _tn_matmul(dup, f, "dw_up")
    do_mla, do_sb, dg_mla, dg_sb = _attn_out_bwd(dh1b, w_o, o_mla, o_sb, out_mla, out_sb)
    dw_o = _tn_matmul(merged, dh1b, "dw_o")
    dq_sb, dk_sb, dv_sb = _sb_bwd(qkv_sb, do_sb, tot, swept)
    early = [g.reshape(N_DEV, -1, d) for g in (dw_o, dw_gate_t, dw_up_t, dw_down)]
    (dqn, dqr, dkn, dv, dkr), landed = _mla_bwd(qn, qr, kv, kr, do_mla, o_mla, lse, early)
    landed = [landed[0], landed[1].transpose(0, 2, 1), landed[2].transpose(0, 2, 1), landed[3]]
    dx, dproj, dq, dkv, dg_q, dg_kv, dg_mix = _proj_in_bwd(
        dqn, dqr, dkn, dv, dkr, dq_sb, dk_sb, dv_sb, cq, ckv, x, dh1, cos, sin_a, sin_b,
        q_norm, kv_norm, norm_mix, w_uq, w_ukv, w_a)
    dw_a = _tn_matmul(u, dproj, "dw_in")
    dw_uq = _tn_matmul(cqn, dq, "dw_uq")
    dw_ukv = _tn_matmul(ckvn, dkv, "dw_ukv")

    p_in = _split_cols(jnp.concatenate([dw_a[:, :416], dw_a[:, 512:]], axis=1))
    p_uq = jnp.concatenate([dw_uq[:, :512].reshape(Q_LORA, MLA_HEADS, MLA_NOPE),
                            dw_uq[:, 512:].reshape(Q_LORA, MLA_HEADS, MLA_ROPE)], axis=2).transpose(1, 0, 2)
    p_ukv = jnp.concatenate([dw_ukv[:, :512].reshape(KV_LORA, MLA_HEADS, MLA_NOPE),
                             dw_ukv[:, 512:].reshape(KV_LORA, MLA_HEADS, HEAD_DIM)], axis=2).transpose(1, 0, 2)
    late = [p_in, p_uq, p_ukv]
    gain_grads = [dg_mix, dg_q, dg_kv, dg_mla, dg_sb, dg_ffn, dg_final]
    return loss, dx, list(landed), late, gain_grads


def kernel(x, positions, norm_mix, w_in, q_latent_norm, w_uq, kv_latent_norm, w_ukv, out_norm_mla, out_norm_sb, w_o, norm_ffn, w_gate, w_up, w_down, norm_final, loss_target, m_norm_mix, m_w_in, m_q_latent_norm, m_w_uq, m_kv_latent_norm, m_w_ukv, m_out_norm_mla, m_out_norm_sb, m_w_o, m_norm_ffn, m_w_gate, m_w_up, m_w_down, m_norm_final, v_norm_mix, v_w_in, v_q_latent_norm, v_w_uq, v_kv_latent_norm, v_w_ukv, v_out_norm_mla, v_out_norm_sb, v_w_o, v_norm_ffn, v_w_gate, v_w_up, v_w_down, v_norm_final):
    mats = [w_in, w_uq, w_ukv, w_o, w_gate, w_up, w_down]
    mat_m = [m_w_in, m_w_uq, m_w_ukv, m_w_o, m_w_gate, m_w_up, m_w_down]
    mat_v = [v_w_in, v_w_uq, v_w_ukv, v_w_o, v_w_gate, v_w_up, v_w_down]
    mat_names = ["w_in", "w_uq", "w_ukv", "w_o", "w_gate", "w_up", "w_down"]
    gains = [norm_mix, q_latent_norm, kv_latent_norm, out_norm_mla, out_norm_sb, norm_ffn, norm_final]
    gain_m = [m_norm_mix, m_q_latent_norm, m_kv_latent_norm, m_out_norm_mla, m_out_norm_sb, m_norm_ffn, m_norm_final]
    gain_v = [v_norm_mix, v_q_latent_norm, v_kv_latent_norm, v_out_norm_mla, v_out_norm_sb, v_norm_ffn, v_norm_final]

    shards = [w[0].astype(BF16) for w in mats]
    shards[4], shards[5] = shards[4].T, shards[5].T
    g_in, g_uq, g_ukv = _all_gather(shards[:3], "weight_all_gather")

    gains2d = [g.reshape(1, -1) for g in gains]
    loss_part, dx, landed, late, gain_grads = _local_step(
        x[0], positions[0], loss_target[0], gains2d, g_in, g_uq, g_ukv, shards[3:])

    sizes = [g.size for g in gains]
    used = sum(sizes) + LANES
    rows = -(-used // (8 * LANES)) * 8

    def pack(vals, tail):
        flat = jnp.concatenate([v.reshape(-1) for v in vals] + [tail])
        return jnp.pad(flat, (0, rows * LANES - flat.size)).reshape(rows, LANES)

    (small,), scattered = _exchange([pack(gain_grads, loss_part.reshape(-1))], late, "grad_exchange")

    mat_out = [_adamw(sl, w[0], m[0], v[0], "adamw_" + nm)
               for sl, w, m, v, nm in zip(list(scattered) + landed, mats, mat_m, mat_v, mat_names)]
    zeros_tail = jnp.zeros((LANES,), F32)
    g_s, d_s, m_s, v_s = _adamw(small, pack(gains, zeros_tail), pack(gain_m, zeros_tail), pack(gain_v, zeros_tail), "adamw_gains")

    def unpack(packed):
        flat = packed.reshape(-1)
        outs, off = [], 0
        for g, n in zip(gains, sizes):
            outs.append(flat[off:off + n].reshape(g.shape))
            off += n
        return outs

    loss = g_s.reshape(-1)[sum(sizes)]

    order = ["norm_mix", "w_in", "q_latent_norm", "w_uq", "kv_latent_norm", "w_ukv", "out_norm_mla", "out_norm_sb",
             "w_o", "norm_ffn", "w_gate", "w_up", "w_down", "norm_final"]
    gain_names = ["norm_mix", "q_latent_norm", "kv_latent_norm", "out_norm_mla", "out_norm_sb", "norm_ffn", "norm_final"]
    result = [loss, dx[None]]
    for kind in range(4):
        small_parts = dict(zip(gain_names, unpack([g_s, d_s, m_s, v_s][kind])))
        mat_parts = {nm: out[kind][None] for nm, out in zip(mat_names, mat_out)}
        result += [small_parts[nm] if nm in small_parts else mat_parts[nm] for nm in order]
    return tuple(result)
```

```python
import math

import jax
import jax.numpy as jnp
from jax import lax
from jax.experimental import pallas as pl
from jax.experimental.pallas import tpu as pltpu

F32 = jnp.float32
BF16 = jnp.bfloat16
MESH = pl.DeviceIdType.MESH

EPS = 1e-6
ROPE_THETA = 10000.0
MLA_HEADS = 8
MLA_NOPE = 64
MLA_ROPE = 32
SB_HEADS = 8
HEAD_DIM = 64
Q_LORA = 256
KV_LORA = 128
MLA_SCALE = 1.0 / math.sqrt(MLA_NOPE + MLA_ROPE)
SB_SCALE = 1.0 / math.sqrt(HEAD_DIM)
LOG2E = math.log2(math.e)
SB_DEAD = -160.0
N_DEV = 8

ADAM_LR = 0.001
ADAM_B1 = 0.9
ADAM_B2 = 0.999
ADAM_EPS = 1e-08
ADAM_WD = 0.01
ADAM_STEP = 10

LANES = 128
ATT_TILE = 512
SB_TILE = 512
TRI = 256
ROW_TILE = 512
STREAM_ROW_TILE = 1024
FFN_BWD_ROW_TILE = 256
PROJ_BWD_ROW_TILE = 256
TN_BLOCK = 256
TN_RESIDENT_BYTES = 16 * 1024 * 1024
VMEM_LIMIT = 56 * 1024 * 1024
NEG = -1e30


def _cparams(*sem):
    return pltpu.CompilerParams(dimension_semantics=sem, vmem_limit_bytes=VMEM_LIMIT)


def _dot(a, b):
    return jnp.dot(a, b, preferred_element_type=F32)


def _dot_nt(a, b):
    return lax.dot_general(a, b, (((1,), (1,)), ((), ())), preferred_element_type=F32)


def _dot_tn(a, b):
    return lax.dot_general(a, b, (((0,), (0,)), ((), ())), preferred_element_type=F32)


def _rms(x, g):
    r = lax.rsqrt(jnp.mean(x * x, axis=-1, keepdims=True) + EPS)
    return x * r * g


def _rms_bwd(x, g, dy):
    r = lax.rsqrt(jnp.mean(x * x, axis=-1, keepdims=True) + EPS)
    n = x * r
    dn = dy * g
    dx = r * (dn - n * jnp.mean(dn * n, axis=-1, keepdims=True))
    return dx, jnp.sum(dy * n, axis=0, keepdims=True)


def _rope(x, cos, sin_a, sin_b):
    return x * cos + pltpu.roll(x, 112, 1) * sin_a + pltpu.roll(x, 16, 1) * sin_b


def _rope_t(g, cos, sin_a, sin_b):
    return g * cos + pltpu.roll(g * sin_a, 16, 1) + pltpu.roll(g * sin_b, 112, 1)


def _row_spec(tm, width):
    return pl.BlockSpec((tm, width), lambda r: (r, 0))


def _full_spec(shape):
    return pl.BlockSpec(shape, lambda *_: (0,) * len(shape))


def _accumulate(ref, val, first):
    @pl.when(first)
    def _():
        ref[...] = val

    @pl.when(jnp.logical_not(first))
    def _():
        ref[...] += val


def _proj_in_fwd(x, g_mix, w_a, g_q, w_uq, g_kv, w_ukv, cos, sin_a, sin_b):
    s, d = x.shape
    tm = min(ROW_TILE, s)

    def body(x_ref, gm_ref, wa_ref, gq_ref, wuq_ref, gkv_ref, wukv_ref, cos_ref, sa_ref, sb_ref,
             u_ref, cq_ref, ckv_ref, cqn_ref, ckvn_ref, qn_ref, qr_ref, kv_ref, kr_ref, sbq_ref):
        u = _rms(x_ref[...], gm_ref[...]).astype(BF16)
        u_ref[...] = u
        cq = _dot(u, wa_ref[:, 0:256])
        ckv = _dot(u, wa_ref[:, 256:384])
        kr = _dot(u, wa_ref[:, 384:512])
        cq_ref[...] = cq
        ckv_ref[...] = ckv
        cqn = _rms(cq, gq_ref[...]).astype(BF16)
        ckvn = _rms(ckv, gkv_ref[...]).astype(BF16)
        cqn_ref[...] = cqn
        ckvn_ref[...] = ckvn
        cos_t, sa_t, sb_t = cos_ref[...], sa_ref[...], sb_ref[...]
        qn_ref[...] = (_dot(cqn, wuq_ref[:, 0:512]) * MLA_SCALE).astype(BF16)
        for half in range(2):
            lo = 512 + half * LANES
            qr = _dot(cqn, wuq_ref[:, lo:lo + LANES])
            qr_ref[:, half * LANES:(half + 1) * LANES] = (_rope(qr, cos_t, sa_t, sb_t) * MLA_SCALE).astype(BF16)
        kv_ref[...] = _dot(ckvn, wukv_ref[...]).astype(BF16)
        krt = kr + pltpu.roll(kr, 32, 1) + pltpu.roll(kr, 64, 1) + pltpu.roll(kr, 96, 1)
        kr_ref[...] = _rope(krt, cos_t, sa_t, sb_t).astype(BF16)
        sbq_ref[:, 0:512] = (_dot(u, wa_ref[:, 512:1024]) * (SB_SCALE * LOG2E)).astype(BF16)
        sbq_ref[:, 512:1536] = _dot(u, wa_ref[:, 1024:2048]).astype(BF16)

    outs = [
        jax.ShapeDtypeStruct((s, d), BF16),
        jax.ShapeDtypeStruct((s, 256), F32),
        jax.ShapeDtypeStruct((s, 128), F32),
        jax.ShapeDtypeStruct((s, 256), BF16),
        jax.ShapeDtypeStruct((s, 128), BF16),
        jax.ShapeDtypeStruct((s, 512), BF16),
        jax.ShapeDtypeStruct((s, 256), BF16),
        jax.ShapeDtypeStruct((s, 1024), BF16),
        jax.ShapeDtypeStruct((s, 128), BF16),
        jax.ShapeDtypeStruct((s, 1536), BF16),
    ]
    return pl.pallas_call(
        body, name="proj_in_fwd", grid=(s // tm,), out_shape=outs,
        in_specs=[_row_spec(tm, d), _full_spec(g_mix.shape), _full_spec(w_a.shape), _full_spec(g_q.shape),
                  _full_spec(w_uq.shape), _full_spec(g_kv.shape), _full_spec(w_ukv.shape),
                  _row_spec(tm, LANES), _row_spec(tm, LANES), _row_spec(tm, LANES)],
        out_specs=[_row_spec(tm, o.shape[1]) for o in outs],
        compiler_params=_cparams("arbitrary"),
    )(x, g_mix, w_a, g_q, w_uq, g_kv, w_ukv, cos, sin_a, sin_b)


def _attn_out_fwd(o_mla, o_sb, g_mla, g_sb, w_o, x, g_ffn):
    s, d = x.shape
    tm = min(STREAM_ROW_TILE, s)

    def body(oa_ref, ob_ref, ga_ref, gb_ref, wo_ref, x_ref, gf_ref, merged_ref, h1_ref, f_ref):
        na = _rms(oa_ref[...], ga_ref[...]).astype(BF16)
        nb = _rms(ob_ref[...], gb_ref[...]).astype(BF16)
        merged_ref[:, 0:512] = na
        merged_ref[:, 512:1024] = nb
        h1 = x_ref[...] + _dot(na, wo_ref[0:512, :]) + _dot(nb, wo_ref[512:1024, :])
        h1_ref[...] = h1
        f_ref[...] = _rms(h1, gf_ref[...]).astype(BF16)

    outs = [jax.ShapeDtypeStruct((s, d), BF16), jax.ShapeDtypeStruct((s, d), F32), jax.ShapeDtypeStruct((s, d), BF16)]
    return pl.pallas_call(
        body, name="attn_out_fwd", grid=(s // tm,), out_shape=outs,
        in_specs=[_row_spec(tm, 512), _row_spec(tm, 512), _full_spec(g_mla.shape), _full_spec(g_sb.shape),
                  _full_spec(w_o.shape), _row_spec(tm, d), _full_spec(g_ffn.shape)],
        out_specs=[_row_spec(tm, d)] * 3,
        compiler_params=_cparams("arbitrary"),
    )(o_mla, o_sb, g_mla, g_sb, w_o, x, g_ffn)


def _ffn_tile(d_ff):
    return d_ff // 2 if (d_ff // 2) % LANES == 0 else d_ff


def _ffn_fwd(f, h1, w_gate_t, w_up_t, w_down):
    s, d = h1.shape
    d_ff = w_gate_t.shape[0]
    tm = min(ROW_TILE, s)
    tf = _ffn_tile(d_ff)

    def body(f_ref, h1_ref, wgt_ref, wut_ref, wd_ref, gate_ref, up_ref, h2_ref):
        j = pl.program_id(1)
        fb = f_ref[...]
        gate = _dot_nt(fb, wgt_ref[...])
        up = _dot_nt(fb, wut_ref[...])
        gate_ref[...] = gate.astype(BF16)
        up_ref[...] = up.astype(BF16)
        act = (gate * jax.nn.sigmoid(gate) * up).astype(BF16)
        part = _dot(act, wd_ref[...])

        @pl.when(j == 0)
        def _():
            h2_ref[...] = h1_ref[...] + part

        @pl.when(j != 0)
        def _():
            h2_ref[...] += part

    outs = [jax.ShapeDtypeStruct((s, d_ff), BF16), jax.ShapeDtypeStruct((s, d_ff), BF16), jax.ShapeDtypeStruct((s, d), F32)]
    return pl.pallas_call(
        body, name="ffn_fwd", grid=(s // tm, d_ff // tf), out_shape=outs,
        in_specs=[pl.BlockSpec((tm, d), lambda r, j: (r, 0)), pl.BlockSpec((tm, d), lambda r, j: (r, 0)),
                  pl.BlockSpec((tf, d), lambda r, j: (j, 0)), pl.BlockSpec((tf, d), lambda r, j: (j, 0)),
                  pl.BlockSpec((tf, d), lambda r, j: (j, 0))],
        out_specs=[pl.BlockSpec((tm, tf), lambda r, j: (r, j)), pl.BlockSpec((tm, tf), lambda r, j: (r, j)),
                   pl.BlockSpec((tm, d), lambda r, j: (r, 0))],
        compiler_params=_cparams("arbitrary", "arbitrary"),
    )(f, h1, w_gate_t, w_up_t, w_down)


def _final_loss(h2, target, g_final):
    s, d = h2.shape
    tm = min(STREAM_ROW_TILE, s)

    def body(h2_ref, t_ref, g_ref, loss_ref, dh2_ref, dh2b_ref, dg_ref):
        first = pl.program_id(0) == 0
        h2v = h2_ref[...]
        g = g_ref[...]
        diff = _rms(h2v, g) - t_ref[...]
        part = 0.5 * jnp.sum(jnp.mean(diff * diff, axis=-1, keepdims=True), axis=0, keepdims=True)
        _accumulate(loss_ref, jnp.broadcast_to(part, loss_ref.shape), first)
        dx, dg = _rms_bwd(h2v, g, diff * (1.0 / d))
        dh2_ref[...] = dx
        dh2b_ref[...] = dx.astype(BF16)
        _accumulate(dg_ref, dg, first)

    outs = [jax.ShapeDtypeStruct((1, LANES), F32), jax.ShapeDtypeStruct((s, d), F32), jax.ShapeDtypeStruct((s, d), BF16),
            jax.ShapeDtypeStruct((1, d), F32)]
    return pl.pallas_call(
        body, name="final_loss", grid=(s // tm,), out_shape=outs,
        in_specs=[_row_spec(tm, d), _row_spec(tm, d), _full_spec((1, d))],
        out_specs=[_full_spec((1, LANES)), _row_spec(tm, d), _row_spec(tm, d), _full_spec((1, d))],
        compiler_params=_cparams("arbitrary"),
    )(h2, target, g_final)


def _tile_iotas(t):
    return lax.broadcasted_iota(jnp.int32, (t, t), 0), lax.broadcasted_iota(jnp.int32, (t, t), 1)


def _mla_fwd(qn, qr, kv, kr, shards):
    s = qn.shape[0]
    t = min(ATT_TILE, s)
    pairs = MLA_HEADS // 2
    nq = s // t
    n = len(shards)

    def body(*refs):
        qn_ref, qr_ref, kn_ref, v_ref, kr_ref = refs[:5]
        o_ref, lse_ref = refs[5 + n:7 + n]
        qcat_ref, m_ref, l_ref, acc_ref = refs[7 + 2 * n:11 + 2 * n]
        hp, i = pl.program_id(0), pl.program_id(1)
        ride = _Exchange(True, refs[5:5 + n], refs[7 + n:7 + 2 * n], *refs[11 + 2 * n:])

        @pl.when((hp == 0) & (i == 0))
        def _():
            ride.start()

        lane = lax.broadcasted_iota(jnp.int32, (1, LANES), 1)
        row, col = _tile_iotas(t)
        causal = col <= row
        q_pair, q_quad = qn_ref[...], qr_ref[...]
        zero = jnp.zeros_like(q_pair)
        for hh in range(2):
            in_head = (lane // HEAD_DIM) == hh
            in_rope = (lane // MLA_ROPE) == (hp % 2) * 2 + hh
            qcat_ref[hh * t:(hh + 1) * t, 0:LANES] = jnp.where(in_head, q_pair, zero)
            qcat_ref[hh * t:(hh + 1) * t, LANES:2 * LANES] = jnp.where(in_rope, q_quad, zero)
        m_ref[...] = jnp.full_like(m_ref, NEG)
        l_ref[...] = jnp.zeros_like(l_ref)
        acc_ref[...] = jnp.zeros_like(acc_ref)

        def tile(j, width, masked):
            rows = pl.ds(pl.multiple_of(j * t, t), width * t)
            kcat = jnp.concatenate([kn_ref[rows, :], kr_ref[rows, :]], axis=1)
            v_ones = jnp.concatenate([v_ref[rows, :], jnp.ones((width * t, LANES), BF16)], axis=1)
            scores = [_dot_nt(qcat_ref[hh * t:(hh + 1) * t, :], kcat) for hh in range(2)]
            for hh in range(2):
                half = slice(hh * t, (hh + 1) * t)
                sc = jnp.where(causal, scores[hh], NEG) if masked else scores[hh]
                m = m_ref[half, :]
                m_new = jnp.maximum(m, jnp.max(sc, axis=-1, keepdims=True))
                alpha = jnp.exp(m - m_new)
                p = jnp.exp(sc - jnp.concatenate([m_new] * (width * t // LANES), axis=1))
                pv = _dot(p.astype(BF16), v_ones)
                l_ref[half, :] = alpha * l_ref[half, :] + pv[:, LANES:]
                acc_ref[half, :] = alpha * acc_ref[half, :] + pv[:, :LANES]
                m_ref[half, :] = m_new

        tile(i, 1, True)

        def step(n, carry):
            tile(4 * n, 4, False)
            return carry

        lax.fori_loop(0, i // 4, step, 0)

        @pl.when(i % 4 >= 2)
        def _():
            tile((i // 4) * 4, 2, False)

        @pl.when(i % 2 == 1)
        def _():
            tile(i - 1, 1, False)

        first = (lane // HEAD_DIM) == 0
        o = acc_ref[...] / l_ref[...]
        lse = m_ref[...] + jnp.log(l_ref[...])
        o_ref[...] = jnp.where(first, o[0:t], o[t:2 * t])
        lse_ref[...] = jnp.where(first, lse[0:t], lse[t:2 * t])

        @pl.when((hp == pairs - 1) & (i == nq - 1))
        def _():
            ride.finish()

    gathered_shapes, sems = _exchange_shapes(True, shards)
    outs = [jax.ShapeDtypeStruct((s, 512), F32), jax.ShapeDtypeStruct((pairs, s, LANES), F32)] + gathered_shapes
    res = pl.pallas_call(
        body, name="mla_fwd", grid=(pairs, nq), out_shape=outs,
        in_specs=[pl.BlockSpec((t, LANES), lambda hp, i: (i, hp)), pl.BlockSpec((t, LANES), lambda hp, i: (i, hp // 2)),
                  pl.BlockSpec((s, LANES), lambda hp, i: (0, hp)), pl.BlockSpec((s, LANES), lambda hp, i: (0, 4 + hp)),
                  pl.BlockSpec((s, LANES), lambda hp, i: (0, 0))] + [ANY] * n,
        out_specs=[pl.BlockSpec((t, LANES), lambda hp, i: (i, hp)), pl.BlockSpec((None, t, LANES), lambda hp, i: (hp, i, 0))]
        + [ANY] * n,
        scratch_shapes=[pltpu.VMEM((2 * t, 2 * LANES), BF16), pltpu.VMEM((2 * t, LANES), F32), pltpu.VMEM((2 * t, LANES), F32),
                        pltpu.VMEM((2 * t, LANES), F32)] + sems,
        compiler_params=_cparams("arbitrary", "arbitrary"),
    )(qn, qr, kv, kv, kr, *shards)
    return res[0], res[1], res[2:]


HEADS = (0, 1)


def _sb_logs(z2, strict, masked):
    log_b = jnp.minimum(z2, 0.0) - jnp.log2(1.0 + jnp.exp2(-jnp.abs(z2)))
    log_1m = log_b - z2
    if masked:
        log_1m = jnp.where(strict, log_1m, 0.0)
    return log_1m, log_b


def _block_totals(x):
    t, w = x.shape
    nb = max(w // TRI, 1)
    bw = w // nb
    blocks = [x[:, b * bw:(b + 1) * bw] for b in range(nb)]
    totals = [jnp.broadcast_to(jnp.sum(blk, axis=-1, keepdims=True), (t, LANES)) for blk in blocks]
    whole = totals[0]
    for tot in totals[1:]:
        whole = whole + tot
    return blocks, totals, whole


def _running_sums(blocks, totals, tri, carry, suffix):
    nb = len(blocks)
    reps = blocks[0].shape[1] // LANES
    outs = [None] * nb
    run = carry
    for b in (range(nb - 1, -1, -1) if suffix else range(nb)):
        outs[b] = _dot(blocks[b].astype(BF16), tri) + jnp.concatenate([run] * reps, axis=1)
        run = run + totals[b]
    return outs[0] if nb == 1 else jnp.concatenate(outs, axis=1)


def _tri(t, rel):
    n = min(TRI, t)
    row, col = _tile_iotas(n)
    return rel(row, col).astype(BF16)


def _sweep_width(t):
    return t // 2 if t // 2 >= TRI else t


def _sb_fwd(qkv):
    s = qkv.shape[0]
    t = min(SB_TILE, s)
    sw = _sweep_width(t)
    pairs = SB_HEADS // 2

    def body(q_ref, k_ref, v_ref, o_ref, tot_ref, cnt_ref, qm_ref, right_ref, acc_ref):
        i = pl.program_id(1)
        lane = lax.broadcasted_iota(jnp.int32, (1, LANES), 1)
        row, col = _tile_iotas(t)
        strict = col < row
        t_suffix = _tri(t, lambda r, c: r > c)
        q_pair = q_ref[...]
        for hh in range(2):
            qm_ref[hh] = jnp.where((lane // HEAD_DIM) == hh, q_pair, jnp.zeros_like(q_pair))
        right_ref[...] = jnp.zeros_like(right_ref)
        acc_ref[...] = jnp.zeros_like(acc_ref)

        def tile(start, width, masked):
            rows = pl.ds(pl.multiple_of(start, width), width)
            k, v = k_ref[rows, :], v_ref[rows, :]
            for hh in HEADS:
                log_1m, log_b = _sb_logs(_dot_nt(qm_ref[hh], k), strict, masked)
                blocks, totals, whole = _block_totals(log_1m)
                a = jnp.exp2(log_b + _running_sums(blocks, totals, t_suffix, right_ref[hh], True))
                if masked:
                    a = jnp.where(strict, a, 0.0)
                right_ref[hh] += whole
                acc_ref[hh] += _dot(a.astype(BF16), v)

        tile(i * t, t, True)

        def alive(n):
            return (n < i * (t // sw)) & (jnp.max(right_ref[...]) > SB_DEAD)

        def step(n):
            tile((i * (t // sw) - 1 - n) * sw, sw, False)
            return n + 1

        swept = lax.while_loop(alive, step, jnp.int32(0))
        cnt_ref[...] = jnp.full(cnt_ref.shape, swept.astype(F32))
        first = (lane // HEAD_DIM) == 0
        o_ref[...] = jnp.where(first, acc_ref[0], acc_ref[1])
        tot_ref[...] = jnp.where(first, right_ref[0], right_ref[1])

    outs = [jax.ShapeDtypeStruct((s, 512), F32), jax.ShapeDtypeStruct((pairs, s, LANES), F32),
            jax.ShapeDtypeStruct((pairs, s // t, 8, LANES), F32)]
    return pl.pallas_call(
        body, name="sb_fwd", grid=(pairs, s // t), out_shape=outs,
        in_specs=[pl.BlockSpec((t, LANES), lambda hp, i: (i, hp)), pl.BlockSpec((s, LANES), lambda hp, i: (0, 4 + hp)),
                  pl.BlockSpec((s, LANES), lambda hp, i: (0, 8 + hp))],
        out_specs=[pl.BlockSpec((t, LANES), lambda hp, i: (i, hp)), pl.BlockSpec((None, t, LANES), lambda hp, i: (hp, i, 0)),
                   pl.BlockSpec((None, None, 8, LANES), lambda hp, i: (hp, i, 0, 0))],
        scratch_shapes=[pltpu.VMEM((2, t, LANES), BF16), pltpu.VMEM((2, t, LANES), F32), pltpu.VMEM((2, t, LANES), F32)],
        compiler_params=_cparams("arbitrary", "arbitrary"),
    )(qkv, qkv, qkv)


def _sb_bwd(qkv, do, tot, cnt):
    s = qkv.shape[0]
    t = min(SB_TILE, s)
    sw = _sweep_width(t)
    pairs = SB_HEADS // 2

    def body(q_ref, k_ref, v_ref, do_ref, tot_ref, cnt_ref, dq_ref, dk_ref, dv_ref,
             qm_ref, dob_ref, total_s, left_l, left_g, dq_s, dk_s, dv_s):
        i = pl.program_id(1)

        @pl.when(i == 0)
        def _():
            dk_s[...] = jnp.zeros_like(dk_s)
            dv_s[...] = jnp.zeros_like(dv_s)

        lane = lax.broadcasted_iota(jnp.int32, (1, LANES), 1)
        row, col = _tile_iotas(t)
        strict = col < row
        t_suffix = _tri(t, lambda r, c: r > c)
        t_excl = _tri(t, lambda r, c: r < c)
        q_pair, do_pair, tot_pair = q_ref[...], do_ref[...], tot_ref[...]
        for hh in range(2):
            in_head = (lane // HEAD_DIM) == hh
            qm_ref[hh] = jnp.where(in_head, q_pair, jnp.zeros_like(q_pair))
            dob_ref[hh] = jnp.where(in_head, do_pair, 0.0).astype(BF16)
            total_s[hh] = jnp.broadcast_to(
                jnp.sum(jnp.where(lane == hh * HEAD_DIM, tot_pair, 0.0), axis=-1, keepdims=True), (t, LANES))
        left_l[...] = jnp.zeros_like(left_l)
        left_g[...] = jnp.zeros_like(left_g)
        dq_s[...] = jnp.zeros_like(dq_s)

        def tile(start, width, masked):
            rows = pl.ds(pl.multiple_of(start, width), width)
            k, v = k_ref[rows, :], v_ref[rows, :]
            z2 = [_dot_nt(qm_ref[hh], k) for hh in HEADS]
            d_a = [_dot_nt(dob_ref[hh], v) for hh in HEADS]
            for hh in HEADS:
                qm, dob = qm_ref[hh], dob_ref[hh]
                log_1m, log_b = _sb_logs(z2[hh], strict, masked)
                blocks, totals, whole = _block_totals(log_1m)
                done = left_l[hh] + whole
                left_l[hh] = done
                a = jnp.exp2(log_b + _running_sums(blocks, totals, t_suffix, total_s[hh] - done, True))
                if masked:
                    a = jnp.where(strict, a, 0.0)
                g = a * d_a[hh]
                blocks, totals, whole = _block_totals(g)
                before = _running_sums(blocks, totals, t_excl, left_g[hh], False)
                left_g[hh] += whole
                dz = g - jnp.exp2(log_b) * (g + before)
                if masked:
                    dz = jnp.where(strict, dz, 0.0)
                dzb = dz.astype(BF16)
                dq_s[hh] += _dot(dzb, k)
                dk_s[rows, :] += _dot_tn(dzb, qm)
                dv_s[rows, :] += _dot_tn(a.astype(BF16), dob)

        def step(h, carry):
            tile(h * sw, sw, False)
            return carry

        swept = jnp.max(cnt_ref[...]).astype(jnp.int32)
        lax.fori_loop(i * (t // sw) - swept, i * (t // sw), step, 0)
        tile(i * t, t, True)
        dq_ref[...] = (jnp.where((lane // HEAD_DIM) == 0, dq_s[0], dq_s[1]) * SB_SCALE).astype(BF16)

        @pl.when(i == s // t - 1)
        def _():
            dk_ref[...] = (dk_s[...] * (1.0 / LOG2E)).astype(BF16)
            dv_ref[...] = dv_s[...].astype(BF16)

    outs = [jax.ShapeDtypeStruct((s, 512), BF16)] * 3
    return pl.pallas_call(
        body, name="sb_bwd", grid=(pairs, s // t), out_shape=outs,
        in_specs=[pl.BlockSpec((t, LANES), lambda hp, i: (i, hp)), pl.BlockSpec((s, LANES), lambda hp, i: (0, 4 + hp)),
                  pl.BlockSpec((s, LANES), lambda hp, i: (0, 8 + hp)), pl.BlockSpec((t, LANES), lambda hp, i: (i, hp)),
                  pl.BlockSpec((None, t, LANES), lambda hp, i: (hp, i, 0)),
                  pl.BlockSpec((None, None, 8, LANES), lambda hp, i: (hp, i, 0, 0))],
        out_specs=[pl.BlockSpec((t, LANES), lambda hp, i: (i, hp)), pl.BlockSpec((s, LANES), lambda hp, i: (0, hp)),
                   pl.BlockSpec((s, LANES), lambda hp, i: (0, hp))],
        scratch_shapes=[pltpu.VMEM((2, t, LANES), BF16), pltpu.VMEM((2, t, LANES), BF16)]
        + [pltpu.VMEM((2, t, LANES), F32)] * 4 + [pltpu.VMEM((s, LANES), F32)] * 2,
        compiler_params=_cparams("arbitrary", "arbitrary"),
    )(qkv, qkv, qkv, do, tot, cnt)


def _mla_bwd(qn, qr, kv, kr, do, o, lse, parts):
    s = qn.shape[0]
    t = min(ATT_TILE, s)
    pairs = MLA_HEADS // 2
    nq = s // t
    n = len(parts)

    def body(*refs):
        qn_ref, qr_ref, kn_ref, v_ref, kr_ref, do_ref, o_ref, lse_ref = refs[:8]
        dqn_ref, dqr_ref, dkn_ref, dv_ref, dkr_ref = refs[8 + n:13 + n]
        qcat_ref, dob_ref, lse_s, delta_s, dq_s, dkn_s, dv_s, dkr_s = refs[13 + 2 * n:21 + 2 * n]
        hp, i = pl.program_id(0), pl.program_id(1)
        ride = _Exchange(False, refs[8:8 + n], refs[13 + n:13 + 2 * n], *refs[21 + 2 * n:])

        @pl.when((hp == 0) & (i == 0))
        def _():
            ride.start()

        @pl.when(i == 0)
        def _():
            dkn_s[...] = jnp.zeros_like(dkn_s)
            dv_s[...] = jnp.zeros_like(dv_s)
            dkr_s[...] = jnp.zeros_like(dkr_s)

        lane = lax.broadcasted_iota(jnp.int32, (1, LANES), 1)
        row, col = _tile_iotas(t)
        causal = col <= row
        q_pair, q_quad, do_pair, lse_pair = qn_ref[...], qr_ref[...], do_ref[...], lse_ref[...]
        do_o = do_pair * o_ref[...]
        zero = jnp.zeros_like(q_pair)
        ropes = []
        for hh in range(2):
            in_head = (lane // HEAD_DIM) == hh
            in_rope = (lane // MLA_ROPE) == (hp % 2) * 2 + hh
            ropes.append(in_rope)
            qcat_ref[hh, :, 0:LANES] = jnp.where(in_head, q_pair, zero)
            qcat_ref[hh, :, LANES:2 * LANES] = jnp.where(in_rope, q_quad, zero)
            dob_ref[hh] = jnp.where(in_head, do_pair, 0.0).astype(BF16)
            delta_s[hh] = jnp.broadcast_to(jnp.sum(jnp.where(in_head, do_o, 0.0), axis=-1, keepdims=True), (t, LANES))
            lse_s[hh] = jnp.broadcast_to(
                jnp.sum(jnp.where(lane == hh * HEAD_DIM, lse_pair, 0.0), axis=-1, keepdims=True), (t, LANES))
        dq_s[...] = jnp.zeros_like(dq_s)
        reps = t // LANES

        def tile(j, width, masked):
            rows = pl.ds(pl.multiple_of(j * t, t), width * t)
            kcat = jnp.concatenate([kn_ref[rows, :], kr_ref[rows, :]], axis=1)
            v = v_ref[rows, :]
            sc = [_dot_nt(qcat_ref[hh], kcat) for hh in HEADS]
            dp = [_dot_nt(dob_ref[hh], v) for hh in HEADS]
            p = [jnp.exp(sc[hh] - jnp.concatenate([lse_s[hh]] * (width * reps), axis=1)) for hh in HEADS]
            if masked:
                p = [jnp.where(causal, p[hh], 0.0) for hh in HEADS]
            ds = [(p[hh] * (dp[hh] - jnp.concatenate([delta_s[hh]] * (width * reps), axis=1))).astype(BF16) for hh in HEADS]
            for hh in HEADS:
                dq_s[hh] += _dot(ds[hh], kcat)
            dkcat = _dot_tn(ds[0], qcat_ref[0]) + _dot_tn(ds[1], qcat_ref[1])
            dkn_s[rows, :] += dkcat[:, 0:LANES]
            dkr_s[rows, :] += dkcat[:, LANES:2 * LANES]
            dv_s[rows, :] += _dot_tn(p[0].astype(BF16), dob_ref[0]) + _dot_tn(p[1].astype(BF16), dob_ref[1])

        def step(n, carry):
            tile(4 * n, 4, False)
            return carry

        lax.fori_loop(0, i // 4, step, 0)

        @pl.when(i % 4 >= 2)
        def _():
            tile((i // 4) * 4, 2, False)

        @pl.when(i % 2 == 1)
        def _():
            tile(i - 1, 1, False)

        tile(i, 1, True)
        dqn_ref[...] = (jnp.where((lane // HEAD_DIM) == 0, dq_s[0, :, 0:LANES], dq_s[1, :, 0:LANES]) * MLA_SCALE).astype(BF16)
        dqr_ref[...] = ((jnp.where(ropes[0], dq_s[0, :, LANES:2 * LANES], 0.0)
                         + jnp.where(ropes[1], dq_s[1, :, LANES:2 * LANES], 0.0)) * MLA_SCALE).astype(BF16)

        @pl.when(i == nq - 1)
        def _():
            dkn_ref[...] = dkn_s[...].astype(BF16)
            dv_ref[...] = dv_s[...].astype(BF16)
            dkr_ref[...] = dkr_s[...].astype(BF16)

        @pl.when((hp == pairs - 1) & (i == nq - 1))
        def _():
            ride.finish()

    pair_block = pl.BlockSpec((t, LANES), lambda hp, i: (i, hp))
    once = pl.Buffered(1)
    landed_shapes, sems = _exchange_shapes(False, parts)
    outs = [jax.ShapeDtypeStruct((s, 512), BF16), jax.ShapeDtypeStruct((pairs, s, LANES), BF16),
            jax.ShapeDtypeStruct((s, 512), BF16), jax.ShapeDtypeStruct((s, 512), BF16),
            jax.ShapeDtypeStruct((pairs, s, LANES), BF16)] + landed_shapes
    res = pl.pallas_call(
        body, name="mla_bwd", grid=(pairs, nq), out_shape=outs,
        in_specs=[pair_block, pl.BlockSpec((t, LANES), lambda hp, i: (i, hp // 2)),
                  pl.BlockSpec((s, LANES), lambda hp, i: (0, hp), pipeline_mode=once),
                  pl.BlockSpec((s, LANES), lambda hp, i: (0, 4 + hp), pipeline_mode=once),
                  pl.BlockSpec((s, LANES), lambda hp, i: (0, 0), pipeline_mode=once), pair_block, pair_block,
                  pl.BlockSpec((None, t, LANES), lambda hp, i: (hp, i, 0))] + [ANY] * n,
        out_specs=[pair_block, pl.BlockSpec((None, t, LANES), lambda hp, i: (hp, i, 0)),
                   pl.BlockSpec((s, LANES), lambda hp, i: (0, hp), pipeline_mode=once),
                   pl.BlockSpec((s, LANES), lambda hp, i: (0, hp), pipeline_mode=once),
                   pl.BlockSpec((None, s, LANES), lambda hp, i: (hp, 0, 0), pipeline_mode=once)] + [ANY] * n,
        scratch_shapes=[pltpu.VMEM((2, t, 2 * LANES), BF16), pltpu.VMEM((2, t, LANES), BF16), pltpu.VMEM((2, t, LANES), F32),
                        pltpu.VMEM((2, t, LANES), F32), pltpu.VMEM((2, t, 2 * LANES), F32)]
        + [pltpu.VMEM((s, LANES), F32)] * 3 + sems,
        compiler_params=_cparams("arbitrary", "arbitrary"),
    )(qn, qr, kv, kv, kr, do, o, lse, *parts)
    return res[:5], res[5:]


def _ffn_bwd(dh2, dh2b, gate, up, h1, g_ffn, w_down, w_gate_t, w_up_t):
    s, d = h1.shape
    d_ff = gate.shape[1]
    tm = min(FFN_BWD_ROW_TILE, s)
    tf = _ffn_tile(d_ff)

    def act_body(dh2b_ref, gate_ref, up_ref, wd_ref, dgate_ref, dup_ref, act_ref):
        dact = _dot_nt(dh2b_ref[...], wd_ref[...])
        gate_v = gate_ref[...].astype(F32)
        up_v = up_ref[...].astype(F32)
        sig = jax.nn.sigmoid(gate_v)
        silu = gate_v * sig
        dup_ref[...] = (dact * silu).astype(BF16)
        dgate_ref[...] = ((dact * up_v) * (sig * (1.0 + gate_v - silu))).astype(BF16)
        act_ref[...] = (silu * up_v).astype(BF16)

    ff = pl.BlockSpec((tm, tf), lambda j, r: (r, j))
    dgate, dup, act = pl.pallas_call(
        act_body, name="ffn_bwd_act", grid=(d_ff // tf, s // tm), out_shape=[jax.ShapeDtypeStruct((s, d_ff), BF16)] * 3,
        in_specs=[pl.BlockSpec((tm, d), lambda j, r: (r, 0)), ff, ff, pl.BlockSpec((tf, d), lambda j, r: (j, 0))],
        out_specs=[ff, ff, ff],
        compiler_params=_cparams("arbitrary", "arbitrary"),
    )(dh2b, gate, up, w_down)

    def df_body(dgate_ref, dup_ref, dh2_ref, h1_ref, g_ref, wgt_ref, wut_ref, dh1_ref, dh1b_ref, dg_ref):
        df = _dot(dgate_ref[...], wgt_ref[...]) + _dot(dup_ref[...], wut_ref[...])
        dx, dg = _rms_bwd(h1_ref[...], g_ref[...], df)
        dh1 = dh2_ref[...] + dx
        dh1_ref[...] = dh1
        dh1b_ref[...] = dh1.astype(BF16)
        _accumulate(dg_ref, dg, pl.program_id(0) == 0)

    outs = [jax.ShapeDtypeStruct((s, d), F32), jax.ShapeDtypeStruct((s, d), BF16), jax.ShapeDtypeStruct((1, d), F32)]
    dh1, dh1b, dg = pl.pallas_call(
        df_body, name="ffn_bwd_df", grid=(s // tm,), out_shape=outs,
        in_specs=[_row_spec(tm, d_ff), _row_spec(tm, d_ff), _row_spec(tm, d), _row_spec(tm, d), _full_spec((1, d)),
                  _full_spec(w_gate_t.shape), _full_spec(w_up_t.shape)],
        out_specs=[_row_spec(tm, d), _row_spec(tm, d), _full_spec((1, d))],
        compiler_params=_cparams("arbitrary"),
    )(dgate, dup, dh2, h1, g_ffn, w_gate_t, w_up_t)
    return dgate, dup, act, dh1, dh1b, dg


def _tn_matmul(a, b, name):
    assert a.dtype == BF16 and b.dtype == BF16
    s, m = a.shape
    n = b.shape[1]
    if s * m * 2 <= TN_RESIDENT_BYTES:
        tm, tn = m, min(n, TN_BLOCK)
    else:
        tm, tn = TN_BLOCK, n

    def body(a_ref, b_ref, o_ref):
        o_ref[...] = _dot_tn(a_ref[...], b_ref[...]).astype(BF16)

    return pl.pallas_call(
        body, name=name, grid=(m // tm, n // tn), out_shape=jax.ShapeDtypeStruct((m, n), BF16),
        in_specs=[pl.BlockSpec((s, tm), lambda i, j: (0, i)), pl.BlockSpec((s, tn), lambda i, j: (0, j))],
        out_specs=pl.BlockSpec((tm, tn), lambda i, j: (i, j)),
        compiler_params=_cparams("arbitrary", "arbitrary"),
    )(a, b)


def _attn_out_bwd(dh1, w_o, o_mla, o_sb, g_mla, g_sb):
    s, d = dh1.shape
    tm = min(STREAM_ROW_TILE, s)

    def body(dh1_ref, wo_ref, oa_ref, ob_ref, ga_ref, gb_ref, doa_ref, dob_ref, dga_ref, dgb_ref):
        first = pl.program_id(0) == 0
        dh1b = dh1_ref[...]
        dxa, dga = _rms_bwd(oa_ref[...], ga_ref[...], _dot_nt(dh1b, wo_ref[0:512, :]))
        dxb, dgb = _rms_bwd(ob_ref[...], gb_ref[...], _dot_nt(dh1b, wo_ref[512:1024, :]))
        doa_ref[...] = dxa
        dob_ref[...] = dxb
        _accumulate(dga_ref, dga, first)
        _accumulate(dgb_ref, dgb, first)

    outs = [jax.ShapeDtypeStruct((s, 512), F32)] * 2 + [jax.ShapeDtypeStruct((1, 512), F32)] * 2
    return pl.pallas_call(
        body, name="attn_out_bwd", grid=(s // tm,), out_shape=outs,
        in_specs=[_row_spec(tm, d), _full_spec(w_o.shape), _row_spec(tm, 512), _row_spec(tm, 512),
                  _full_spec((1, 512)), _full_spec((1, 512))],
        out_specs=[_row_spec(tm, 512), _row_spec(tm, 512), _full_spec((1, 512)), _full_spec((1, 512))],
        compiler_params=_cparams("arbitrary"),
    )(dh1, w_o, o_mla, o_sb, g_mla, g_sb)


def _proj_in_bwd(dqn, dqr, dkn, dv, dkr, dq_sb, dk_sb, dv_sb, cq, ckv, x, dh1, cos, sin_a, sin_b,
                 g_q, g_kv, g_mix, w_uq, w_ukv, w_a):
    s, d = x.shape
    tm = min(PROJ_BWD_ROW_TILE, s)

    def body(dqn_ref, dqr_ref, dkn_ref, dv_ref, dkr_ref, dqs_ref, dks_ref, dvs_ref, cq_ref, ckv_ref, x_ref, dh1_ref,
             cos_ref, sa_ref, sb_ref, gq_ref, gkv_ref, gm_ref, wuq_ref, wukv_ref, wa_ref,
             dx_ref, dproj_ref, dq_ref, dkv_ref, dgq_ref, dgkv_ref, dgm_ref):
        first = pl.program_id(0) == 0
        lane = lax.broadcasted_iota(jnp.int32, (1, LANES), 1)
        cos_t, sa_t, sb_t = cos_ref[...], sa_ref[...], sb_ref[...]
        dq_ref[:, 0:512] = dqn_ref[...]
        for half in range(2):
            quad = dqr_ref[2 * half].astype(F32) + dqr_ref[2 * half + 1].astype(F32)
            dq_ref[:, 512 + half * LANES:512 + (half + 1) * LANES] = _rope_t(quad, cos_t, sa_t, sb_t).astype(BF16)
        dcq, dgq = _rms_bwd(cq_ref[...], gq_ref[...], _dot_nt(dq_ref[...], wuq_ref[...]))
        _accumulate(dgq_ref, dgq, first)
        dkv_ref[:, 0:512] = dkn_ref[...]
        dkv_ref[:, 512:1024] = dv_ref[...]
        dckv, dgkv = _rms_bwd(ckv_ref[...], gkv_ref[...], _dot_nt(dkv_ref[...], wukv_ref[...]))
        _accumulate(dgkv_ref, dgkv, first)
        pairs_sum = (dkr_ref[0].astype(F32) + dkr_ref[1].astype(F32)) + (dkr_ref[2].astype(F32) + dkr_ref[3].astype(F32))
        g = _rope_t(pairs_sum, cos_t, sa_t, sb_t)
        g = g + pltpu.roll(g, 96, 1) + pltpu.roll(g, 64, 1) + pltpu.roll(g, 32, 1)
        dproj_ref[:, 0:256] = dcq.astype(BF16)
        dproj_ref[:, 256:384] = dckv.astype(BF16)
        dproj_ref[:, 384:512] = jnp.where(lane < MLA_ROPE, g, 0.0).astype(BF16)
        dproj_ref[:, 512:1024] = dqs_ref[...]
        dproj_ref[:, 1024:1536] = dks_ref[...]
        dproj_ref[:, 1536:2048] = dvs_ref[...]
        dxn, dgm = _rms_bwd(x_ref[...], gm_ref[...], _dot_nt(dproj_ref[...], wa_ref[...]))
        dx_ref[...] = dh1_ref[...] + dxn
        _accumulate(dgm_ref, dgm, first)

    quad_spec = pl.BlockSpec((4, tm, LANES), lambda r: (0, r, 0))
    outs = [jax.ShapeDtypeStruct((s, d), F32), jax.ShapeDtypeStruct((s, 2048), BF16), jax.ShapeDtypeStruct((s, 768), BF16),
            jax.ShapeDtypeStruct((s, 1024), BF16), jax.ShapeDtypeStruct((1, 256), F32), jax.ShapeDtypeStruct((1, 128), F32),
            jax.ShapeDtypeStruct((1, d), F32)]
    return pl.pallas_call(
        body, name="proj_in_bwd", grid=(s // tm,), out_shape=outs,
        in_specs=[_row_spec(tm, 512), quad_spec, _row_spec(tm, 512), _row_spec(tm, 512), quad_spec,
                  _row_spec(tm, 512), _row_spec(tm, 512), _row_spec(tm, 512), _row_spec(tm, 256), _row_spec(tm, 128),
                  _row_spec(tm, d), _row_spec(tm, d), _row_spec(tm, LANES), _row_spec(tm, LANES), _row_spec(tm, LANES),
                  _full_spec((1, 256)), _full_spec((1, 128)), _full_spec((1, d)),
                  _full_spec(w_uq.shape), _full_spec(w_ukv.shape), _full_spec(w_a.shape)],
        out_specs=[_row_spec(tm, d), _row_spec(tm, 2048), _row_spec(tm, 768), _row_spec(tm, 1024),
                   _full_spec((1, 256)), _full_spec((1, 128)), _full_spec((1, d))],
        compiler_params=_cparams("arbitrary"),
    )(dqn, dqr, dkn, dv, dkr, dq_sb, dk_sb, dv_sb, cq, ckv, x, dh1, cos, sin_a, sin_b, g_q, g_kv, g_mix,
      w_uq, w_ukv, w_a)


ANY = pl.BlockSpec(memory_space=pl.ANY)


def _place():
    return lax.axis_index("x"), lax.axis_index("y"), lax.axis_index("c")


def _all_gather(shards, name):
    n = len(shards)

    def body(*refs):
        ins, outs = refs[:n], refs[n:2 * n]
        send_sems, recv_sems, local_sems = refs[2 * n:]
        x, y, c = _place()
        me, sibling = (x, y, c), (x, y, 1 - c)
        chips = [(1 - x, y), (x, 1 - y), (1 - x, 1 - y)]

        def slot(a, px, py, pc):
            return outs[a].at[4 * px + 2 * py + pc]

        def copy(a, k, block, to, src=None):
            return pltpu.make_async_remote_copy(
                src_ref=slot(a, *block) if src is None else src, dst_ref=slot(a, *block),
                send_sem=send_sems.at[a, k], recv_sem=recv_sems.at[a, k], device_id=to, device_id_type=MESH)

        mine, first, passed = [], [], []
        for a in range(n):
            own = pltpu.make_async_copy(ins[a], slot(a, *me), local_sems.at[a])
            own.start()
            mine.append(own)
            cps = [copy(a, 0, me, sibling, src=ins[a])]
            cps += [copy(a, 1 + j, me, (*chip, c), src=ins[a]) for j, chip in enumerate(chips)]
            for cp in cps:
                cp.start()
            first += cps
        for a in range(n):
            for j, chip in enumerate(chips):
                copy(a, 1 + j, (*chip, c), me).wait_recv()
                fwd = copy(a, 4 + j, (*chip, c), sibling)
                fwd.start()
                passed.append(fwd)
        for a in range(n):
            copy(a, 0, sibling, me).wait_recv()
            for j, chip in enumerate(chips):
                copy(a, 4 + j, (*chip, 1 - c), me).wait_recv()
        for cp in first + passed:
            cp.wait_send()
        for own in mine:
            own.wait()

    return pl.pallas_call(
        body, name=name,
        out_shape=[jax.ShapeDtypeStruct((N_DEV,) + v.shape, v.dtype) for v in shards],
        in_specs=[ANY] * n, out_specs=[ANY] * n,
        scratch_shapes=[pltpu.SemaphoreType.DMA((n, 7)), pltpu.SemaphoreType.DMA((n, 7)), pltpu.SemaphoreType.DMA((n,))],
    )(*shards)


class _Exchange:
    def __init__(self, gather, ins, outs, send_sems, recv_sems, local_sems):
        self.gather, self.ins, self.outs = gather, ins, outs
        self.sems = (send_sems, recv_sems, local_sems)
        x, y, c = _place()
        self.me = 4 * x + 2 * y + c
        self.peers = []
        for k in range(1, N_DEV):
            px = 1 - x if k & 4 else x
            py = 1 - y if k & 2 else y
            pc = 1 - c if k & 1 else c
            self.peers.append(((px, py, pc), 4 * px + 2 * py + pc))

    def _remote(self, a, k, landing):
        send_sems, recv_sems, _ = self.sems
        where, number = self.peers[k]
        src = self.ins[a] if self.gather else self.ins[a].at[number]
        return pltpu.make_async_remote_copy(
            src_ref=src, dst_ref=self.outs[a].at[landing], send_sem=send_sems.at[a, k], recv_sem=recv_sems.at[a, k],
            device_id=where, device_id_type=MESH)

    def _local(self, a):
        src = self.ins[a] if self.gather else self.ins[a].at[self.me]
        return pltpu.make_async_copy(src, self.outs[a].at[self.me], self.sems[2].at[a])

    def start(self):
        for a in range(len(self.ins)):
            self._local(a).start()
            for k in range(N_DEV - 1):
                self._remote(a, k, self.me).start()

    def finish(self):
        for a in range(len(self.ins)):
            for k in range(N_DEV - 1):
                self._remote(a, k, self.peers[k][1]).wait_recv()
            for k in range(N_DEV - 1):
                self._remote(a, k, self.me).wait_send()
            self._local(a).wait()


def _exchange_shapes(gather, arrays):
    out_shape = [jax.ShapeDtypeStruct(((N_DEV,) + v.shape) if gather else v.shape, v.dtype) for v in arrays]
    n = len(arrays)
    sems = [pltpu.SemaphoreType.DMA((n, N_DEV - 1)), pltpu.SemaphoreType.DMA((n, N_DEV - 1)), pltpu.SemaphoreType.DMA((n,))]
    return out_shape, sems


def _exchange(gathers, scatters, name):
    ng, ns = len(gathers), len(scatters)
    n = ng + ns

    def body(*refs):
        ins, outs, sems = refs[:n], refs[n:2 * n], refs[2 * n:]
        both = [_Exchange(True, ins[:ng], outs[:ng], *sems[:3]), _Exchange(False, ins[ng:], outs[ng:], *sems[3:])]
        for ex in both:
            ex.start()
        for ex in both:
            ex.finish()

    g_shapes, g_sems = _exchange_shapes(True, gathers)
    s_shapes, s_sems = _exchange_shapes(False, scatters)
    res = pl.pallas_call(body, name=name, out_shape=g_shapes + s_shapes, in_specs=[ANY] * n, out_specs=[ANY] * n,
                         scratch_shapes=g_sems + s_sems)(*gathers, *scatters)
    return res[:ng], res[ng:]


def _grad_row_tile(rows):
    return _largest_tile_rows(rows, 256)


def _largest_tile_rows(rows, cap):
    for cand in range(cap, 0, -8):
        if rows % cand == 0:
            return cand
    return rows


def _adamw_math(w, g, m, v):
    m_new = ADAM_B1 * m + (1.0 - ADAM_B1) * g
    v_new = ADAM_B2 * v + (1.0 - ADAM_B2) * (g * g)
    m_hat = m_new / (1.0 - ADAM_B1 ** ADAM_STEP)
    v_hat = v_new / (1.0 - ADAM_B2 ** ADAM_STEP)
    delta = -ADAM_LR * (m_hat / (jnp.sqrt(v_hat) + ADAM_EPS) + ADAM_WD * w)
    return delta, m_new, v_new


def _adamw(slots, w, m, v, name):
    k, r, cdim = slots.shape
    tr = _grad_row_tile(r)

    def body(s_ref, w_ref, m_ref, v_ref, g_ref, d_ref, mo_ref, vo_ref):
        g = s_ref[0].astype(F32)
        for q in range(1, k):
            g = g + s_ref[q].astype(F32)
        g_ref[...] = g
        d_ref[...], mo_ref[...], vo_ref[...] = _adamw_math(w_ref[...], g, m_ref[...], v_ref[...])

    blk = pl.BlockSpec((tr, cdim), lambda i: (i, 0))
    return pl.pallas_call(
        body, name=name, grid=(r // tr,), out_shape=[jax.ShapeDtypeStruct((r, cdim), F32)] * 4,
        in_specs=[pl.BlockSpec((k, tr, cdim), lambda i: (0, i, 0)), blk, blk, blk], out_specs=[blk] * 4,
        compiler_params=_cparams("arbitrary"),
    )(slots, w, m, v)


def _stack_cols(g):
    n, r, c = g.shape
    return g.transpose(1, 0, 2).reshape(r, n * c)


def _split_cols(w):
    r, nc = w.shape
    return w.reshape(r, N_DEV, nc // N_DEV).transpose(1, 0, 2)


def _rope_tables(positions):
    inv_freq = ROPE_THETA ** (-jnp.arange(0, MLA_ROPE, 2, dtype=F32) / MLA_ROPE)
    ang = positions.astype(F32).reshape(-1, 1) * inv_freq[None, :]
    cos, sin, zero = jnp.cos(ang), jnp.sin(ang), jnp.zeros_like(ang)
    reps = LANES // MLA_ROPE
    return (jnp.tile(jnp.concatenate([cos, cos], axis=1), (1, reps)),
            jnp.tile(jnp.concatenate([-sin, zero], axis=1), (1, reps)),
            jnp.tile(jnp.concatenate([zero, sin], axis=1), (1, reps)))


def _local_step(x, positions, loss_target, gains, g_in, g_uq, g_ukv, late_shards):
    norm_mix, q_norm, kv_norm, out_mla, out_sb, norm_ffn, norm_final = gains
    d = x.shape[1]
    w_in = _stack_cols(g_in)
    w_a = jnp.concatenate([w_in[:, :416], jnp.zeros((d, 96), BF16), w_in[:, 416:]], axis=1)
    w_uq = jnp.concatenate([g_uq[:, :, :MLA_NOPE].transpose(1, 0, 2).reshape(Q_LORA, -1),
                            g_uq[:, :, MLA_NOPE:].transpose(1, 0, 2).reshape(Q_LORA, -1)], axis=1)
    w_ukv = jnp.concatenate([g_ukv[:, :, :MLA_NOPE].transpose(1, 0, 2).reshape(KV_LORA, -1),
                             g_ukv[:, :, MLA_NOPE:].transpose(1, 0, 2).reshape(KV_LORA, -1)], axis=1)
    cos, sin_a, sin_b = _rope_tables(positions)

    u, cq, ckv, cqn, ckvn, qn, qr, kv, kr, qkv_sb = _proj_in_fwd(x, norm_mix, w_a, q_norm, w_uq, kv_norm, w_ukv, cos, sin_a, sin_b)
    o_mla, lse, (g_o, g_gate, g_up, g_down) = _mla_fwd(qn, qr, kv, kr, late_shards)
    w_o = g_o.reshape(-1, d)
    w_gate_t, w_up_t = g_gate.reshape(-1, d), g_up.reshape(-1, d)
    w_down = g_down.reshape(-1, d)
    o_sb, tot, swept = _sb_fwd(qkv_sb)
    merged, h1, f = _attn_out_fwd(o_mla, o_sb, out_mla, out_sb, w_o, x, norm_ffn)
    gate, up, h2 = _ffn_fwd(f, h1, w_gate_t, w_up_t, w_down)
    loss, dh2, dh2b, dg_final = _final_loss(h2, loss_target, norm_final.reshape(1, d))

    dgate, dup, act, dh1, dh1b, dg_ffn = _ffn_bwd(dh2, dh2b, gate, up, h1, norm_ffn, w_down, w_gate_t, w_up_t)
    dw_down = _tn_matmul(act, dh2b, "dw_down")
    dw_gate_t = _tn_matmul(dgate, f, "dw_gate")
    dw_up_t = _tn_matmul(dup, f, "dw_up")
    do_mla, do_sb, dg_mla, dg_sb = _attn_out_bwd(dh1b, w_o, o_mla, o_sb, out_mla, out_sb)
    dw_o = _tn_matmul(merged, dh1b, "dw_o")
    dq_sb, dk_sb, dv_sb = _sb_bwd(qkv_sb, do_sb, tot, swept)
    early = [g.reshape(N_DEV, -1, d) for g in (dw_o, dw_gate_t, dw_up_t, dw_down)]
    (dqn, dqr, dkn, dv, dkr), landed = _mla_bwd(qn, qr, kv, kr, do_mla, o_mla, lse, early)
    landed = [landed[0], landed[1].transpose(0, 2, 1), landed[2].transpose(0, 2, 1), landed[3]]
    dx, dproj, dq, dkv, dg_q, dg_kv, dg_mix = _proj_in_bwd(
        dqn, dqr, dkn, dv, dkr, dq_sb, dk_sb, dv_sb, cq, ckv, x, dh1, cos, sin_a, sin_b,
        q_norm, kv_norm, norm_mix, w_uq, w_ukv, w_a)
    dw_a = _tn_matmul(u, dproj, "dw_in")
    dw_uq = _tn_matmul(cqn, dq, "dw_uq")
    dw_ukv = _tn_matmul(ckvn, dkv, "dw_ukv")

    p_in = _split_cols(jnp.concatenate([dw_a[:, :416], dw_a[:, 512:]], axis=1))
    p_uq = jnp.concatenate([dw_uq[:, :512].reshape(Q_LORA, MLA_HEADS, MLA_NOPE),
                            dw_uq[:, 512:].reshape(Q_LORA, MLA_HEADS, MLA_ROPE)], axis=2).transpose(1, 0, 2)
    p_ukv = jnp.concatenate([dw_ukv[:, :512].reshape(KV_LORA, MLA_HEADS, MLA_NOPE),
                             dw_ukv[:, 512:].reshape(KV_LORA, MLA_HEADS, HEAD_DIM)], axis=2).transpose(1, 0, 2)
    late = [p_in, p_uq, p_ukv]
    gain_grads = [dg_mix, dg_q, dg_kv, dg_mla, dg_sb, dg_ffn, dg_final]
    return loss, dx, list(landed), late, gain_grads


def kernel(x, positions, norm_mix, w_in, q_latent_norm, w_uq, kv_latent_norm, w_ukv, out_norm_mla, out_norm_sb, w_o, norm_ffn, w_gate, w_up, w_down, norm_final, loss_target, m_norm_mix, m_w_in, m_q_latent_norm, m_w_uq, m_kv_latent_norm, m_w_ukv, m_out_norm_mla, m_out_norm_sb, m_w_o, m_norm_ffn, m_w_gate, m_w_up, m_w_down, m_norm_final, v_norm_mix, v_w_in, v_q_latent_norm, v_w_uq, v_kv_latent_norm, v_w_ukv, v_out_norm_mla, v_out_norm_sb, v_w_o, v_norm_ffn, v_w_gate, v_w_up, v_w_down, v_norm_final):
    mats = [w_in, w_uq, w_ukv, w_o, w_gate, w_up, w_down]
    mat_m = [m_w_in, m_w_uq, m_w_ukv, m_w_o, m_w_gate, m_w_up, m_w_down]
    mat_v = [v_w_in, v_w_uq, v_w_ukv, v_w_o, v_w_gate, v_w_up, v_w_down]
    mat_names = ["w_in", "w_uq", "w_ukv", "w_o", "w_gate", "w_up", "w_down"]
    gains = [norm_mix, q_latent_norm, kv_latent_norm, out_norm_mla, out_norm_sb, norm_ffn, norm_final]
    gain_m = [m_norm_mix, m_q_latent_norm, m_kv_latent_norm, m_out_norm_mla, m_out_norm_sb, m_norm_ffn, m_norm_final]
    gain_v = [v_norm_mix, v_q_latent_norm, v_kv_latent_norm, v_out_norm_mla, v_out_norm_sb, v_norm_ffn, v_norm_final]

    shards = [w[0].astype(BF16) for w in mats]
    shards[4], shards[5] = shards[4].T, shards[5].T
    g_in, g_uq, g_ukv = _all_gather(shards[:3], "weight_all_gather")

    gains2d = [g.reshape(1, -1) for g in gains]
    loss_part, dx, landed, late, gain_grads = _local_step(
        x[0], positions[0], loss_target[0], gains2d, g_in, g_uq, g_ukv, shards[3:])

    sizes = [g.size for g in gains]
    used = sum(sizes) + LANES
    rows = -(-used // (8 * LANES)) * 8

    def pack(vals, tail):
        flat = jnp.concatenate([v.reshape(-1) for v in vals] + [tail])
        return jnp.pad(flat, (0, rows * LANES - flat.size)).reshape(rows, LANES)

    (small,), scattered = _exchange([pack(gain_grads, loss_part.reshape(-1))], late, "grad_exchange")

    mat_out = [_adamw(sl, w[0], m[0], v[0], "adamw_" + nm)
               for sl, w, m, v, nm in zip(list(scattered) + landed, mats, mat_m, mat_v, mat_names)]
    zeros_tail = jnp.zeros((LANES,), F32)
    g_s, d_s, m_s, v_s = _adamw(small, pack(gains, zeros_tail), pack(gain_m, zeros_tail), pack(gain_v, zeros_tail), "adamw_gains")

    def unpack(packed):
        flat = packed.reshape(-1)
        outs, off = [], 0
        for g, n in zip(gains, sizes):
            outs.append(flat[off:off + n].reshape(g.shape))
            off += n
        return outs

    loss = g_s.reshape(-1)[sum(sizes)]

    order = ["norm_mix", "w_in", "q_latent_norm", "w_uq", "kv_latent_norm", "w_ukv", "out_norm_mla", "out_norm_sb",
             "w_o", "norm_ffn", "w_gate", "w_up", "w_down", "norm_final"]
    gain_names = ["norm_mix", "q_latent_norm", "kv_latent_norm", "out_norm_mla", "out_norm_sb", "norm_ffn", "norm_final"]
    result = [loss, dx[None]]
    for kind in range(4):
        small_parts = dict(zip(gain_names, unpack([g_s, d_s, m_s, v_s][kind])))
        mat_parts = {nm: out[kind][None] for nm, out in zip(mat_names, mat_out)}
        result += [small_parts[nm] if nm in small_parts else mat_parts[nm] for nm in order]
    return tuple(result)
```

```python
import math

import jax
import jax.numpy as jnp
from jax import lax
from jax.experimental import pallas as pl
from jax.experimental.pallas import tpu as pltpu

F32 = jnp.float32
BF16 = jnp.bfloat16
MESH = pl.DeviceIdType.MESH

EPS = 1e-6
ROPE_THETA = 10000.0
MLA_HEADS = 8
MLA_NOPE = 64
MLA_ROPE = 32
SB_HEADS = 8
HEAD_DIM = 64
Q_LORA = 256
KV_LORA = 128
MLA_SCALE = 1.0 / math.sqrt(MLA_NOPE + MLA_ROPE)
SB_SCALE = 1.0 / math.sqrt(HEAD_DIM)
LOG2E = math.log2(math.e)
SB_DEAD = -160.0
N_DEV = 8

ADAM_LR = 0.001
ADAM_B1 = 0.9
ADAM_B2 = 0.999
ADAM_EPS = 1e-08
ADAM_WD = 0.01
ADAM_STEP = 10

LANES = 128
ATT_TILE = 512
SB_TILE = 512
TRI = 256
ROW_TILE = 512
STREAM_ROW_TILE = 1024
FFN_BWD_ROW_TILE = 512
PROJ_BWD_ROW_TILE = 512
TN_BLOCK = 256
TN_RESIDENT_BYTES = 16 * 1024 * 1024
VMEM_LIMIT = 56 * 1024 * 1024
NEG = -1e30


def _cparams(*sem):
    return pltpu.CompilerParams(dimension_semantics=sem, vmem_limit_bytes=VMEM_LIMIT)


def _dot(a, b):
    return jnp.dot(a, b, preferred_element_type=F32)


def _dot_nt(a, b):
    return lax.dot_general(a, b, (((1,), (1,)), ((), ())), preferred_element_type=F32)


def _dot_tn(a, b):
    return lax.dot_general(a, b, (((0,), (0,)), ((), ())), preferred_element_type=F32)


def _rms(x, g):
    r = lax.rsqrt(jnp.mean(x * x, axis=-1, keepdims=True) + EPS)
    return x * r * g


def _rms_bwd(x, g, dy):
    r = lax.rsqrt(jnp.mean(x * x, axis=-1, keepdims=True) + EPS)
    n = x * r
    dn = dy * g
    dx = r * (dn - n * jnp.mean(dn * n, axis=-1, keepdims=True))
    return dx, jnp.sum(dy * n, axis=0, keepdims=True)


def _rope(x, cos, sin_a, sin_b):
    return x * cos + pltpu.roll(x, 112, 1) * sin_a + pltpu.roll(x, 16, 1) * sin_b


def _rope_t(g, cos, sin_a, sin_b):
    return g * cos + pltpu.roll(g * sin_a, 16, 1) + pltpu.roll(g * sin_b, 112, 1)


def _row_spec(tm, width):
    return pl.BlockSpec((tm, width), lambda r: (r, 0))


def _full_spec(shape):
    return pl.BlockSpec(shape, lambda *_: (0,) * len(shape))


def _accumulate(ref, val, first):
    @pl.when(first)
    def _():
        ref[...] = val

    @pl.when(jnp.logical_not(first))
    def _():
        ref[...] += val


def _proj_in_fwd(x, g_mix, w_a, g_q, w_uq, g_kv, w_ukv, cos, sin_a, sin_b):
    s, d = x.shape
    tm = min(ROW_TILE, s)

    def body(x_ref, gm_ref, wa_ref, gq_ref, wuq_ref, gkv_ref, wukv_ref, cos_ref, sa_ref, sb_ref,
             u_ref, cq_ref, ckv_ref, cqn_ref, ckvn_ref, qn_ref, qr_ref, kv_ref, kr_ref, sbq_ref):
        u = _rms(x_ref[...], gm_ref[...]).astype(BF16)
        u_ref[...] = u
        cq = _dot(u, wa_ref[:, 0:256])
        ckv = _dot(u, wa_ref[:, 256:384])
        kr = _dot(u, wa_ref[:, 384:512])
        cq_ref[...] = cq
        ckv_ref[...] = ckv
        cqn = _rms(cq, gq_ref[...]).astype(BF16)
        ckvn = _rms(ckv, gkv_ref[...]).astype(BF16)
        cqn_ref[...] = cqn
        ckvn_ref[...] = ckvn
        cos_t, sa_t, sb_t = cos_ref[...], sa_ref[...], sb_ref[...]
        qn_ref[...] = (_dot(cqn, wuq_ref[:, 0:512]) * MLA_SCALE).astype(BF16)
        for half in range(2):
            lo = 512 + half * LANES
            qr = _dot(cqn, wuq_ref[:, lo:lo + LANES])
            qr_ref[:, half * LANES:(half + 1) * LANES] = (_rope(qr, cos_t, sa_t, sb_t) * MLA_SCALE).astype(BF16)
        kv_ref[...] = _dot(ckvn, wukv_ref[...]).astype(BF16)
        krt = kr + pltpu.roll(kr, 32, 1) + pltpu.roll(kr, 64, 1) + pltpu.roll(kr, 96, 1)
        kr_ref[...] = _rope(krt, cos_t, sa_t, sb_t).astype(BF16)
        sbq_ref[:, 0:512] = (_dot(u, wa_ref[:, 512:1024]) * (SB_SCALE * LOG2E)).astype(BF16)
        sbq_ref[:, 512:1536] = _dot(u, wa_ref[:, 1024:2048]).astype(BF16)

    outs = [
        jax.ShapeDtypeStruct((s, d), BF16),
        jax.ShapeDtypeStruct((s, 256), F32),
        jax.ShapeDtypeStruct((s, 128), F32),
        jax.ShapeDtypeStruct((s, 256), BF16),
        jax.ShapeDtypeStruct((s, 128), BF16),
        jax.ShapeDtypeStruct((s, 512), BF16),
        jax.ShapeDtypeStruct((s, 256), BF16),
        jax.ShapeDtypeStruct((s, 1024), BF16),
        jax.ShapeDtypeStruct((s, 128), BF16),
        jax.ShapeDtypeStruct((s, 1536), BF16),
    ]
    return pl.pallas_call(
        body, name="proj_in_fwd", grid=(s // tm,), out_shape=outs,
        in_specs=[_row_spec(tm, d), _full_spec(g_mix.shape), _full_spec(w_a.shape), _full_spec(g_q.shape),
                  _full_spec(w_uq.shape), _full_spec(g_kv.shape), _full_spec(w_ukv.shape),
                  _row_spec(tm, LANES), _row_spec(tm, LANES), _row_spec(tm, LANES)],
        out_specs=[_row_spec(tm, o.shape[1]) for o in outs],
        compiler_params=_cparams("arbitrary"),
    )(x, g_mix, w_a, g_q, w_uq, g_kv, w_ukv, cos, sin_a, sin_b)


def _attn_out_fwd(o_mla, o_sb, g_mla, g_sb, w_o, x, g_ffn):
    s, d = x.shape
    tm = min(STREAM_ROW_TILE, s)

    def body(oa_ref, ob_ref, ga_ref, gb_ref, wo_ref, x_ref, gf_ref, merged_ref, h1_ref, f_ref):
        na = _rms(oa_ref[...], ga_ref[...]).astype(BF16)
        nb = _rms(ob_ref[...], gb_ref[...]).astype(BF16)
        merged_ref[:, 0:512] = na
        merged_ref[:, 512:1024] = nb
        h1 = x_ref[...] + _dot(na, wo_ref[0:512, :]) + _dot(nb, wo_ref[512:1024, :])
        h1_ref[...] = h1
        f_ref[...] = _rms(h1, gf_ref[...]).astype(BF16)

    outs = [jax.ShapeDtypeStruct((s, d), BF16), jax.ShapeDtypeStruct((s, d), F32), jax.ShapeDtypeStruct((s, d), BF16)]
    return pl.pallas_call(
        body, name="attn_out_fwd", grid=(s // tm,), out_shape=outs,
        in_specs=[_row_spec(tm, 512), _row_spec(tm, 512), _full_spec(g_mla.shape), _full_spec(g_sb.shape),
                  _full_spec(w_o.shape), _row_spec(tm, d), _full_spec(g_ffn.shape)],
        out_specs=[_row_spec(tm, d)] * 3,
        compiler_params=_cparams("arbitrary"),
    )(o_mla, o_sb, g_mla, g_sb, w_o, x, g_ffn)


def _ffn_tile(d_ff):
    return d_ff // 2 if (d_ff // 2) % LANES == 0 else d_ff


def _ffn_fwd(f, h1, w_gate_t, w_up_t, w_down):
    s, d = h1.shape
    d_ff = w_gate_t.shape[0]
    tm = min(ROW_TILE, s)
    tf = _ffn_tile(d_ff)

    def body(f_ref, h1_ref, wgt_ref, wut_ref, wd_ref, gate_ref, up_ref, h2_ref):
        j = pl.program_id(1)
        fb = f_ref[...]
        gate = _dot_nt(fb, wgt_ref[...])
        up = _dot_nt(fb, wut_ref[...])
        gate_ref[...] = gate.astype(BF16)
        up_ref[...] = up.astype(BF16)
        act = (gate * jax.nn.sigmoid(gate) * up).astype(BF16)
        part = _dot(act, wd_ref[...])

        @pl.when(j == 0)
        def _():
            h2_ref[...] = h1_ref[...] + part

        @pl.when(j != 0)
        def _():
            h2_ref[...] += part

    outs = [jax.ShapeDtypeStruct((s, d_ff), BF16), jax.ShapeDtypeStruct((s, d_ff), BF16), jax.ShapeDtypeStruct((s, d), F32)]
    return pl.pallas_call(
        body, name="ffn_fwd", grid=(s // tm, d_ff // tf), out_shape=outs,
        in_specs=[pl.BlockSpec((tm, d), lambda r, j: (r, 0)), pl.BlockSpec((tm, d), lambda r, j: (r, 0)),
                  pl.BlockSpec((tf, d), lambda r, j: (j, 0)), pl.BlockSpec((tf, d), lambda r, j: (j, 0)),
                  pl.BlockSpec((tf, d), lambda r, j: (j, 0))],
        out_specs=[pl.BlockSpec((tm, tf), lambda r, j: (r, j)), pl.BlockSpec((tm, tf), lambda r, j: (r, j)),
                   pl.BlockSpec((tm, d), lambda r, j: (r, 0))],
        compiler_params=_cparams("arbitrary", "arbitrary"),
    )(f, h1, w_gate_t, w_up_t, w_down)


def _final_loss(h2, target, g_final):
    s, d = h2.shape
    tm = min(STREAM_ROW_TILE, s)

    def body(h2_ref, t_ref, g_ref, loss_ref, dh2_ref, dh2b_ref, dg_ref):
        first = pl.program_id(0) == 0
        h2v = h2_ref[...]
        g = g_ref[...]
        diff = _rms(h2v, g) - t_ref[...]
        part = 0.5 * jnp.sum(jnp.mean(diff * diff, axis=-1, keepdims=True), axis=0, keepdims=True)
        _accumulate(loss_ref, jnp.broadcast_to(part, loss_ref.shape), first)
        dx, dg = _rms_bwd(h2v, g, diff * (1.0 / d))
        dh2_ref[...] = dx
        dh2b_ref[...] = dx.astype(BF16)
        _accumulate(dg_ref, dg, first)

    outs = [jax.ShapeDtypeStruct((1, LANES), F32), jax.ShapeDtypeStruct((s, d), F32), jax.ShapeDtypeStruct((s, d), BF16),
            jax.ShapeDtypeStruct((1, d), F32)]
    return pl.pallas_call(
        body, name="final_loss", grid=(s // tm,), out_shape=outs,
        in_specs=[_row_spec(tm, d), _row_spec(tm, d), _full_spec((1, d))],
        out_specs=[_full_spec((1, LANES)), _row_spec(tm, d), _row_spec(tm, d), _full_spec((1, d))],
        compiler_params=_cparams("arbitrary"),
    )(h2, target, g_final)


def _tile_iotas(t):
    return lax.broadcasted_iota(jnp.int32, (t, t), 0), lax.broadcasted_iota(jnp.int32, (t, t), 1)


def _mla_fwd(qn, qr, kv, kr, shards):
    s = qn.shape[0]
    t = min(ATT_TILE, s)
    pairs = MLA_HEADS // 2
    nq = s // t
    n = len(shards)

    def body(*refs):
        qn_ref, qr_ref, kn_ref, v_ref, kr_ref = refs[:5]
        o_ref, lse_ref = refs[5 + n:7 + n]
        qcat_ref, m_ref, l_ref, acc_ref = refs[7 + 2 * n:11 + 2 * n]
        hp, i = pl.program_id(0), pl.program_id(1)
        ride = _Exchange(True, refs[5:5 + n], refs[7 + n:7 + 2 * n], *refs[11 + 2 * n:])

        @pl.when((hp == 0) & (i == 0))
        def _():
            ride.start()

        lane = lax.broadcasted_iota(jnp.int32, (1, LANES), 1)
        row, col = _tile_iotas(t)
        causal = col <= row
        q_pair, q_quad = qn_ref[...], qr_ref[...]
        zero = jnp.zeros_like(q_pair)
        for hh in range(2):
            in_head = (lane // HEAD_DIM) == hh
            in_rope = (lane // MLA_ROPE) == (hp % 2) * 2 + hh
            qcat_ref[hh * t:(hh + 1) * t, 0:LANES] = jnp.where(in_head, q_pair, zero)
            qcat_ref[hh * t:(hh + 1) * t, LANES:2 * LANES] = jnp.where(in_rope, q_quad, zero)
        m_ref[...] = jnp.full_like(m_ref, NEG)
        l_ref[...] = jnp.zeros_like(l_ref)
        acc_ref[...] = jnp.zeros_like(acc_ref)

        def tile(j, width, masked):
            rows = pl.ds(pl.multiple_of(j * t, t), width * t)
            kcat = jnp.concatenate([kn_ref[rows, :], kr_ref[rows, :]], axis=1)
            v_ones = jnp.concatenate([v_ref[rows, :], jnp.ones((width * t, LANES), BF16)], axis=1)
            scores = [_dot_nt(qcat_ref[hh * t:(hh + 1) * t, :], kcat) for hh in range(2)]
            for hh in range(2):
                half = slice(hh * t, (hh + 1) * t)
                sc = jnp.where(causal, scores[hh], NEG) if masked else scores[hh]
                m = m_ref[half, :]
                m_new = jnp.maximum(m, jnp.max(sc, axis=-1, keepdims=True))
                alpha = jnp.exp(m - m_new)
                p = jnp.exp(sc - jnp.concatenate([m_new] * (width * t // LANES), axis=1))
                pv = _dot(p.astype(BF16), v_ones)
                l_ref[half, :] = alpha * l_ref[half, :] + pv[:, LANES:]
                acc_ref[half, :] = alpha * acc_ref[half, :] + pv[:, :LANES]
                m_ref[half, :] = m_new

        tile(i, 1, True)

        def step(n, carry):
            tile(4 * n, 4, False)
            return carry

        lax.fori_loop(0, i // 4, step, 0)

        @pl.when(i % 4 >= 2)
        def _():
            tile((i // 4) * 4, 2, False)

        @pl.when(i % 2 == 1)
        def _():
            tile(i - 1, 1, False)

        first = (lane // HEAD_DIM) == 0
        o = acc_ref[...] / l_ref[...]
        lse = m_ref[...] + jnp.log(l_ref[...])
        o_ref[...] = jnp.where(first, o[0:t], o[t:2 * t])
        lse_ref[...] = jnp.where(first, lse[0:t], lse[t:2 * t])

        @pl.when((hp == pairs - 1) & (i == nq - 1))
        def _():
            ride.finish()

    gathered_shapes, sems = _exchange_shapes(True, shards)
    outs = [jax.ShapeDtypeStruct((s, 512), F32), jax.ShapeDtypeStruct((pairs, s, LANES), F32)] + gathered_shapes
    res = pl.pallas_call(
        body, name="mla_fwd", grid=(pairs, nq), out_shape=outs,
        in_specs=[pl.BlockSpec((t, LANES), lambda hp, i: (i, hp)), pl.BlockSpec((t, LANES), lambda hp, i: (i, hp // 2)),
                  pl.BlockSpec((s, LANES), lambda hp, i: (0, hp)), pl.BlockSpec((s, LANES), lambda hp, i: (0, 4 + hp)),
                  pl.BlockSpec((s, LANES), lambda hp, i: (0, 0))] + [ANY] * n,
        out_specs=[pl.BlockSpec((t, LANES), lambda hp, i: (i, hp)), pl.BlockSpec((None, t, LANES), lambda hp, i: (hp, i, 0))]
        + [ANY] * n,
        scratch_shapes=[pltpu.VMEM((2 * t, 2 * LANES), BF16), pltpu.VMEM((2 * t, LANES), F32), pltpu.VMEM((2 * t, LANES), F32),
                        pltpu.VMEM((2 * t, LANES), F32)] + sems,
        compiler_params=_cparams("arbitrary", "arbitrary"),
    )(qn, qr, kv, kv, kr, *shards)
    return res[0], res[1], res[2:]


HEADS = (0, 1)


def _sb_logs(z2, strict, masked):
    log_b = jnp.minimum(z2, 0.0) - jnp.log2(1.0 + jnp.exp2(-jnp.abs(z2)))
    log_1m = log_b - z2
    if masked:
        log_1m = jnp.where(strict, log_1m, 0.0)
    return log_1m, log_b


def _block_totals(x):
    t, w = x.shape
    nb = max(w // TRI, 1)
    bw = w // nb
    blocks = [x[:, b * bw:(b + 1) * bw] for b in range(nb)]
    totals = [jnp.broadcast_to(jnp.sum(blk, axis=-1, keepdims=True), (t, LANES)) for blk in blocks]
    whole = totals[0]
    for tot in totals[1:]:
        whole = whole + tot
    return blocks, totals, whole


def _running_sums(blocks, totals, tri, carry, suffix):
    nb = len(blocks)
    reps = blocks[0].shape[1] // LANES
    outs = [None] * nb
    run = carry
    for b in (range(nb - 1, -1, -1) if suffix else range(nb)):
        outs[b] = _dot(blocks[b].astype(BF16), tri) + jnp.concatenate([run] * reps, axis=1)
        run = run + totals[b]
    return outs[0] if nb == 1 else jnp.concatenate(outs, axis=1)


def _tri(t, rel):
    n = min(TRI, t)
    row, col = _tile_iotas(n)
    return rel(row, col).astype(BF16)


def _sweep_width(t):
    return t // 2 if t // 2 >= TRI else t


def _sb_fwd(qkv):
    s = qkv.shape[0]
    t = min(SB_TILE, s)
    sw = _sweep_width(t)
    pairs = SB_HEADS // 2

    def body(q_ref, k_ref, v_ref, o_ref, tot_ref, cnt_ref, qm_ref, right_ref, acc_ref):
        i = pl.program_id(1)
        lane = lax.broadcasted_iota(jnp.int32, (1, LANES), 1)
        row, col = _tile_iotas(t)
        strict = col < row
        t_suffix = _tri(t, lambda r, c: r > c)
        q_pair = q_ref[...]
        for hh in range(2):
            qm_ref[hh] = jnp.where((lane // HEAD_DIM) == hh, q_pair, jnp.zeros_like(q_pair))
        right_ref[...] = jnp.zeros_like(right_ref)
        acc_ref[...] = jnp.zeros_like(acc_ref)

        def tile(start, width, masked):
            rows = pl.ds(pl.multiple_of(start, width), width)
            k, v = k_ref[rows, :], v_ref[rows, :]
            for hh in HEADS:
                log_1m, log_b = _sb_logs(_dot_nt(qm_ref[hh], k), strict, masked)
                blocks, totals, whole = _block_totals(log_1m)
                a = jnp.exp2(log_b + _running_sums(blocks, totals, t_suffix, right_ref[hh], True))
                if masked:
                    a = jnp.where(strict, a, 0.0)
                right_ref[hh] += whole
                acc_ref[hh] += _dot(a.astype(BF16), v)

        tile(i * t, t, True)

        def alive(n):
            return (n < i * (t // sw)) & (jnp.max(right_ref[...]) > SB_DEAD)

        def step(n):
            tile((i * (t // sw) - 1 - n) * sw, sw, False)
            return n + 1

        swept = lax.while_loop(alive, step, jnp.int32(0))
        cnt_ref[...] = jnp.full(cnt_ref.shape, swept.astype(F32))
        first = (lane // HEAD_DIM) == 0
        o_ref[...] = jnp.where(first, acc_ref[0], acc_ref[1])
        tot_ref[...] = jnp.where(first, right_ref[0], right_ref[1])

    outs = [jax.ShapeDtypeStruct((s, 512), F32), jax.ShapeDtypeStruct((pairs, s, LANES), F32),
            jax.ShapeDtypeStruct((pairs, s // t, 8, LANES), F32)]
    return pl.pallas_call(
        body, name="sb_fwd", grid=(pairs, s // t), out_shape=outs,
        in_specs=[pl.BlockSpec((t, LANES), lambda hp, i: (i, hp)), pl.BlockSpec((s, LANES), lambda hp, i: (0, 4 + hp)),
                  pl.BlockSpec((s, LANES), lambda hp, i: (0, 8 + hp))],
        out_specs=[pl.BlockSpec((t, LANES), lambda hp, i: (i, hp)), pl.BlockSpec((None, t, LANES), lambda hp, i: (hp, i, 0)),
                   pl.BlockSpec((None, None, 8, LANES), lambda hp, i: (hp, i, 0, 0))],
        scratch_shapes=[pltpu.VMEM((2, t, LANES), BF16), pltpu.VMEM((2, t, LANES), F32), pltpu.VMEM((2, t, LANES), F32)],
        compiler_params=_cparams("arbitrary", "arbitrary"),
    )(qkv, qkv, qkv)


def _sb_bwd(qkv, do, tot, cnt):
    s = qkv.shape[0]
    t = min(SB_TILE, s)
    sw = _sweep_width(t)
    pairs = SB_HEADS // 2

    def body(q_ref, k_ref, v_ref, do_ref, tot_ref, cnt_ref, dq_ref, dk_ref, dv_ref,
             qm_ref, dob_ref, total_s, left_l, left_g, dq_s, dk_s, dv_s):
        i = pl.program_id(1)

        @pl.when(i == 0)
        def _():
            dk_s[...] = jnp.zeros_like(dk_s)
            dv_s[...] = jnp.zeros_like(dv_s)

        lane = lax.broadcasted_iota(jnp.int32, (1, LANES), 1)
        row, col = _tile_iotas(t)
        strict = col < row
        t_suffix = _tri(t, lambda r, c: r > c)
        t_excl = _tri(t, lambda r, c: r < c)
        q_pair, do_pair, tot_pair = q_ref[...], do_ref[...], tot_ref[...]
        for hh in range(2):
            in_head = (lane // HEAD_DIM) == hh
            qm_ref[hh] = jnp.where(in_head, q_pair, jnp.zeros_like(q_pair))
            dob_ref[hh] = jnp.where(in_head, do_pair, 0.0).astype(BF16)
            total_s[hh] = jnp.broadcast_to(
                jnp.sum(jnp.where(lane == hh * HEAD_DIM, tot_pair, 0.0), axis=-1, keepdims=True), (t, LANES))
        left_l[...] = jnp.zeros_like(left_l)
        left_g[...] = jnp.zeros_like(left_g)
        dq_s[...] = jnp.zeros_like(dq_s)

        def tile(start, width, masked):
            rows = pl.ds(pl.multiple_of(start, width), width)
            k, v = k_ref[rows, :], v_ref[rows, :]
            z2 = [_dot_nt(qm_ref[hh], k) for hh in HEADS]
            d_a = [_dot_nt(dob_ref[hh], v) for hh in HEADS]
            for hh in HEADS:
                qm, dob = qm_ref[hh], dob_ref[hh]
                log_1m, log_b = _sb_logs(z2[hh], strict, masked)
                blocks, totals, whole = _block_totals(log_1m)
                done = left_l[hh] + whole
                left_l[hh] = done
                a = jnp.exp2(log_b + _running_sums(blocks, totals, t_suffix, total_s[hh] - done, True))
                if masked:
                    a = jnp.where(strict, a, 0.0)
                g = a * d_a[hh]
                blocks, totals, whole = _block_totals(g)
                before = _running_sums(blocks, totals, t_excl, left_g[hh], False)
                left_g[hh] += whole
                dz = g - jnp.exp2(log_b) * (g + before)
                if masked:
                    dz = jnp.where(strict, dz, 0.0)
                dzb = dz.astype(BF16)
                dq_s[hh] += _dot(dzb, k)
                dk_s[rows, :] += _dot_tn(dzb, qm)
                dv_s[rows, :] += _dot_tn(a.astype(BF16), dob)

        def step(h, carry):
            tile(h * sw, sw, False)
            return carry

        swept = jnp.max(cnt_ref[...]).astype(jnp.int32)
        lax.fori_loop(i * (t // sw) - swept, i * (t // sw), step, 0)
        tile(i * t, t, True)
        dq_ref[...] = (jnp.where((lane // HEAD_DIM) == 0, dq_s[0], dq_s[1]) * SB_SCALE).astype(BF16)

        @pl.when(i == s // t - 1)
        def _():
            dk_ref[...] = (dk_s[...] * (1.0 / LOG2E)).astype(BF16)
            dv_ref[...] = dv_s[...].astype(BF16)

    outs = [jax.ShapeDtypeStruct((s, 512), BF16)] * 3
    return pl.pallas_call(
        body, name="sb_bwd", grid=(pairs, s // t), out_shape=outs,
        in_specs=[pl.BlockSpec((t, LANES), lambda hp, i: (i, hp)), pl.BlockSpec((s, LANES), lambda hp, i: (0, 4 + hp)),
                  pl.BlockSpec((s, LANES), lambda hp, i: (0, 8 + hp)), pl.BlockSpec((t, LANES), lambda hp, i: (i, hp)),
                  pl.BlockSpec((None, t, LANES), lambda hp, i: (hp, i, 0)),
                  pl.BlockSpec((None, None, 8, LANES), lambda hp, i: (hp, i, 0, 0))],
        out_specs=[pl.BlockSpec((t, LANES), lambda hp, i: (i, hp)), pl.BlockSpec((s, LANES), lambda hp, i: (0, hp)),
                   pl.BlockSpec((s, LANES), lambda hp, i: (0, hp))],
        scratch_shapes=[pltpu.VMEM((2, t, LANES), BF16), pltpu.VMEM((2, t, LANES), BF16)]
        + [pltpu.VMEM((2, t, LANES), F32)] * 4 + [pltpu.VMEM((s, LANES), F32)] * 2,
        compiler_params=_cparams("arbitrary", "arbitrary"),
    )(qkv, qkv, qkv, do, tot, cnt)


def _mla_bwd(qn, qr, kv, kr, do, o, lse, parts):
    s = qn.shape[0]
    t = min(ATT_TILE, s)
    pairs = MLA_HEADS // 2
    nq = s // t
    n = len(parts)

    def body(*refs):
        qn_ref, qr_ref, kn_ref, v_ref, kr_ref, do_ref, o_ref, lse_ref = refs[:8]
        dqn_ref, dqr_ref, dkn_ref, dv_ref, dkr_ref = refs[8 + n:13 + n]
        qcat_ref, dob_ref, lse_s, delta_s, dq_s, dkn_s, dv_s, dkr_s = refs[13 + 2 * n:21 + 2 * n]
        hp, i = pl.program_id(0), pl.program_id(1)
        ride = _Exchange(False, refs[8:8 + n], refs[13 + n:13 + 2 * n], *refs[21 + 2 * n:])

        @pl.when((hp == 0) & (i == 0))
        def _():
            ride.start()

        @pl.when(i == 0)
        def _():
            dkn_s[...] = jnp.zeros_like(dkn_s)
            dv_s[...] = jnp.zeros_like(dv_s)
            dkr_s[...] = jnp.zeros_like(dkr_s)

        lane = lax.broadcasted_iota(jnp.int32, (1, LANES), 1)
        row, col = _tile_iotas(t)
        causal = col <= row
        q_pair, q_quad, do_pair, lse_pair = qn_ref[...], qr_ref[...], do_ref[...], lse_ref[...]
        do_o = do_pair * o_ref[...]
        zero = jnp.zeros_like(q_pair)
        ropes = []
        for hh in range(2):
            in_head = (lane // HEAD_DIM) == hh
            in_rope = (lane // MLA_ROPE) == (hp % 2) * 2 + hh
            ropes.append(in_rope)
            qcat_ref[hh, :, 0:LANES] = jnp.where(in_head, q_pair, zero)
            qcat_ref[hh, :, LANES:2 * LANES] = jnp.where(in_rope, q_quad, zero)
            dob_ref[hh] = jnp.where(in_head, do_pair, 0.0).astype(BF16)
            delta_s[hh] = jnp.broadcast_to(jnp.sum(jnp.where(in_head, do_o, 0.0), axis=-1, keepdims=True), (t, LANES))
            lse_s[hh] = jnp.broadcast_to(
                jnp.sum(jnp.where(lane == hh * HEAD_DIM, lse_pair, 0.0), axis=-1, keepdims=True), (t, LANES))
        dq_s[...] = jnp.zeros_like(dq_s)
        reps = t // LANES

        def tile(j, width, masked):
            rows = pl.ds(pl.multiple_of(j * t, t), width * t)
            kcat = jnp.concatenate([kn_ref[rows, :], kr_ref[rows, :]], axis=1)
            v = v_ref[rows, :]
            sc = [_dot_nt(qcat_ref[hh], kcat) for hh in HEADS]
            dp = [_dot_nt(dob_ref[hh], v) for hh in HEADS]
            p = [jnp.exp(sc[hh] - jnp.concatenate([lse_s[hh]] * (width * reps), axis=1)) for hh in HEADS]
            if masked:
                p = [jnp.where(causal, p[hh], 0.0) for hh in HEADS]
            ds = [(p[hh] * (dp[hh] - jnp.concatenate([delta_s[hh]] * (width * reps), axis=1))).astype(BF16) for hh in HEADS]
            for hh in HEADS:
                dq_s[hh] += _dot(ds[hh], kcat)
            dkcat = _dot_tn(ds[0], qcat_ref[0]) + _dot_tn(ds[1], qcat_ref[1])
            dkn_s[rows, :] += dkcat[:, 0:LANES]
            dkr_s[rows, :] += dkcat[:, LANES:2 * LANES]
            dv_s[rows, :] += _dot_tn(p[0].astype(BF16), dob_ref[0]) + _dot_tn(p[1].astype(BF16), dob_ref[1])

        def step(n, carry):
            tile(4 * n, 4, False)
            return carry

        lax.fori_loop(0, i // 4, step, 0)

        @pl.when(i % 4 >= 2)
        def _():
            tile((i // 4) * 4, 2, False)

        @pl.when(i % 2 == 1)
        def _():
            tile(i - 1, 1, False)

        tile(i, 1, True)
        dqn_ref[...] = (jnp.where((lane // HEAD_DIM) == 0, dq_s[0, :, 0:LANES], dq_s[1, :, 0:LANES]) * MLA_SCALE).astype(BF16)
        dqr_ref[...] = ((jnp.where(ropes[0], dq_s[0, :, LANES:2 * LANES], 0.0)
                         + jnp.where(ropes[1], dq_s[1, :, LANES:2 * LANES], 0.0)) * MLA_SCALE).astype(BF16)

        @pl.when(i == nq - 1)
        def _():
            dkn_ref[...] = dkn_s[...].astype(BF16)
            dv_ref[...] = dv_s[...].astype(BF16)
            dkr_ref[...] = dkr_s[...].astype(BF16)

        @pl.when((hp == pairs - 1) & (i == nq - 1))
        def _():
            ride.finish()

    pair_block = pl.BlockSpec((t, LANES), lambda hp, i: (i, hp))
    once = pl.Buffered(1)
    landed_shapes, sems = _exchange_shapes(False, parts)
    outs = [jax.ShapeDtypeStruct((s, 512), BF16), jax.ShapeDtypeStruct((pairs, s, LANES), BF16),
            jax.ShapeDtypeStruct((s, 512), BF16), jax.ShapeDtypeStruct((s, 512), BF16),
            jax.ShapeDtypeStruct((pairs, s, LANES), BF16)] + landed_shapes
    res = pl.pallas_call(
        body, name="mla_bwd", grid=(pairs, nq), out_shape=outs,
        in_specs=[pair_block, pl.BlockSpec((t, LANES), lambda hp, i: (i, hp // 2)),
                  pl.BlockSpec((s, LANES), lambda hp, i: (0, hp), pipeline_mode=once),
                  pl.BlockSpec((s, LANES), lambda hp, i: (0, 4 + hp), pipeline_mode=once),
                  pl.BlockSpec((s, LANES), lambda hp, i: (0, 0), pipeline_mode=once), pair_block, pair_block,
                  pl.BlockSpec((None, t, LANES), lambda hp, i: (hp, i, 0))] + [ANY] * n,
        out_specs=[pair_block, pl.BlockSpec((None, t, LANES), lambda hp, i: (hp, i, 0)),
                   pl.BlockSpec((s, LANES), lambda hp, i: (0, hp), pipeline_mode=once),
                   pl.BlockSpec((s, LANES), lambda hp, i: (0, hp), pipeline_mode=once),
                   pl.BlockSpec((None, s, LANES), lambda hp, i: (hp, 0, 0), pipeline_mode=once)] + [ANY] * n,
        scratch_shapes=[pltpu.VMEM((2, t, 2 * LANES), BF16), pltpu.VMEM((2, t, LANES), BF16), pltpu.VMEM((2, t, LANES), F32),
                        pltpu.VMEM((2, t, LANES), F32), pltpu.VMEM((2, t, 2 * LANES), F32)]
        + [pltpu.VMEM((s, LANES), F32)] * 3 + sems,
        compiler_params=_cparams("arbitrary", "arbitrary"),
    )(qn, qr, kv, kv, kr, do, o, lse, *parts)
    return res[:5], res[5:]


def _ffn_bwd(dh2, dh2b, gate, up, h1, g_ffn, w_down, w_gate_t, w_up_t):
    s, d = h1.shape
    d_ff = gate.shape[1]
    tm = min(FFN_BWD_ROW_TILE, s)
    tf = _ffn_tile(d_ff)

    def act_body(dh2b_ref, gate_ref, up_ref, wd_ref, dgate_ref, dup_ref, act_ref):
        dact = _dot_nt(dh2b_ref[...], wd_ref[...])
        gate_v = gate_ref[...].astype(F32)
        up_v = up_ref[...].astype(F32)
        sig = jax.nn.sigmoid(gate_v)
        silu = gate_v * sig
        dup_ref[...] = (dact * silu).astype(BF16)
        dgate_ref[...] = ((dact * up_v) * (sig * (1.0 + gate_v - silu))).astype(BF16)
        act_ref[...] = (silu * up_v).astype(BF16)

    ff = pl.BlockSpec((tm, tf), lambda j, r: (r, j))
    dgate, dup, act = pl.pallas_call(
        act_body, name="ffn_bwd_act", grid=(d_ff // tf, s // tm), out_shape=[jax.ShapeDtypeStruct((s, d_ff), BF16)] * 3,
        in_specs=[pl.BlockSpec((tm, d), lambda j, r: (r, 0)), ff, ff, pl.BlockSpec((tf, d), lambda j, r: (j, 0))],
        out_specs=[ff, ff, ff],
        compiler_params=_cparams("arbitrary", "arbitrary"),
    )(dh2b, gate, up, w_down)

    def df_body(dgate_ref, dup_ref, dh2_ref, h1_ref, g_ref, wgt_ref, wut_ref, dh1_ref, dh1b_ref, dg_ref):
        df = _dot(dgate_ref[...], wgt_ref[...]) + _dot(dup_ref[...], wut_ref[...])
        dx, dg = _rms_bwd(h1_ref[...], g_ref[...], df)
        dh1 = dh2_ref[...] + dx
        dh1_ref[...] = dh1
        dh1b_ref[...] = dh1.astype(BF16)
        _accumulate(dg_ref, dg, pl.program_id(0) == 0)

    outs = [jax.ShapeDtypeStruct((s, d), F32), jax.ShapeDtypeStruct((s, d), BF16), jax.ShapeDtypeStruct((1, d), F32)]
    dh1, dh1b, dg = pl.pallas_call(
        df_body, name="ffn_bwd_df", grid=(s // tm,), out_shape=outs,
        in_specs=[_row_spec(tm, d_ff), _row_spec(tm, d_ff), _row_spec(tm, d), _row_spec(tm, d), _full_spec((1, d)),
                  pl.BlockSpec(w_gate_t.shape, lambda r: (0, 0), pipeline_mode=pl.Buffered(1)),
                  pl.BlockSpec(w_up_t.shape, lambda r: (0, 0), pipeline_mode=pl.Buffered(1))],
        out_specs=[_row_spec(tm, d), _row_spec(tm, d), _full_spec((1, d))],
        compiler_params=_cparams("arbitrary"),
    )(dgate, dup, dh2, h1, g_ffn, w_gate_t, w_up_t)
    return dgate, dup, act, dh1, dh1b, dg


def _tn_matmul(a, b, name):
    assert a.dtype == BF16 and b.dtype == BF16
    s, m = a.shape
    n = b.shape[1]
    if s * m * 2 <= TN_RESIDENT_BYTES:
        tm, tn = m, min(n, TN_BLOCK)
    else:
        tm, tn = TN_BLOCK, n

    def body(a_ref, b_ref, o_ref):
        o_ref[...] = _dot_tn(a_ref[...], b_ref[...]).astype(BF16)

    return pl.pallas_call(
        body, name=name, grid=(m // tm, n // tn), out_shape=jax.ShapeDtypeStruct((m, n), BF16),
        in_specs=[pl.BlockSpec((s, tm), lambda i, j: (0, i)), pl.BlockSpec((s, tn), lambda i, j: (0, j))],
        out_specs=pl.BlockSpec((tm, tn), lambda i, j: (i, j)),
        compiler_params=_cparams("arbitrary", "arbitrary"),
    )(a, b)


def _attn_out_bwd(dh1, w_o, o_mla, o_sb, g_mla, g_sb):
    s, d = dh1.shape
    tm = min(STREAM_ROW_TILE, s)

    def body(dh1_ref, wo_ref, oa_ref, ob_ref, ga_ref, gb_ref, doa_ref, dob_ref, dga_ref, dgb_ref):
        first = pl.program_id(0) == 0
        dh1b = dh1_ref[...]
        dxa, dga = _rms_bwd(oa_ref[...], ga_ref[...], _dot_nt(dh1b, wo_ref[0:512, :]))
        dxb, dgb = _rms_bwd(ob_ref[...], gb_ref[...], _dot_nt(dh1b, wo_ref[512:1024, :]))
        doa_ref[...] = dxa
        dob_ref[...] = dxb
        _accumulate(dga_ref, dga, first)
        _accumulate(dgb_ref, dgb, first)

    outs = [jax.ShapeDtypeStruct((s, 512), F32)] * 2 + [jax.ShapeDtypeStruct((1, 512), F32)] * 2
    return pl.pallas_call(
        body, name="attn_out_bwd", grid=(s // tm,), out_shape=outs,
        in_specs=[_row_spec(tm, d), _full_spec(w_o.shape), _row_spec(tm, 512), _row_spec(tm, 512),
                  _full_spec((1, 512)), _full_spec((1, 512))],
        out_specs=[_row_spec(tm, 512), _row_spec(tm, 512), _full_spec((1, 512)), _full_spec((1, 512))],
        compiler_params=_cparams("arbitrary"),
    )(dh1, w_o, o_mla, o_sb, g_mla, g_sb)


def _proj_in_bwd(dqn, dqr, dkn, dv, dkr, dq_sb, dk_sb, dv_sb, cq, ckv, x, dh1, cos, sin_a, sin_b,
                 g_q, g_kv, g_mix, w_uq, w_ukv, w_a):
    s, d = x.shape
    tm = min(PROJ_BWD_ROW_TILE, s)

    def body(dqn_ref, dqr_ref, dkn_ref, dv_ref, dkr_ref, dqs_ref, dks_ref, dvs_ref, cq_ref, ckv_ref, x_ref, dh1_ref,
             cos_ref, sa_ref, sb_ref, gq_ref, gkv_ref, gm_ref, wuq_ref, wukv_ref, wa_ref,
             dx_ref, dproj_ref, dq_ref, dkv_ref, dgq_ref, dgkv_ref, dgm_ref):
        first = pl.program_id(0) == 0
        lane = lax.broadcasted_iota(jnp.int32, (1, LANES), 1)
        cos_t, sa_t, sb_t = cos_ref[...], sa_ref[...], sb_ref[...]
        dq_ref[:, 0:512] = dqn_ref[...]
        for half in range(2):
            quad = dqr_ref[2 * half].astype(F32) + dqr_ref[2 * half + 1].astype(F32)
            dq_ref[:, 512 + half * LANES:512 + (half + 1) * LANES] = _rope_t(quad, cos_t, sa_t, sb_t).astype(BF16)
        dcq, dgq = _rms_bwd(cq_ref[...], gq_ref[...], _dot_nt(dq_ref[...], wuq_ref[...]))
        _accumulate(dgq_ref, dgq, first)
        dkv_ref[:, 0:512] = dkn_ref[...]
        dkv_ref[:, 512:1024] = dv_ref[...]
        dckv, dgkv = _rms_bwd(ckv_ref[...], gkv_ref[...], _dot_nt(dkv_ref[...], wukv_ref[...]))
        _accumulate(dgkv_ref, dgkv, first)
        pairs_sum = (dkr_ref[0].astype(F32) + dkr_ref[1].astype(F32)) + (dkr_ref[2].astype(F32) + dkr_ref[3].astype(F32))
        g = _rope_t(pairs_sum, cos_t, sa_t, sb_t)
        g = g + pltpu.roll(g, 96, 1) + pltpu.roll(g, 64, 1) + pltpu.roll(g, 32, 1)
        dproj_ref[:, 0:256] = dcq.astype(BF16)
        dproj_ref[:, 256:384] = dckv.astype(BF16)
        dproj_ref[:, 384:512] = jnp.where(lane < MLA_ROPE, g, 0.0).astype(BF16)
        dproj_ref[:, 512:1024] = dqs_ref[...]
        dproj_ref[:, 1024:1536] = dks_ref[...]
        dproj_ref[:, 1536:2048] = dvs_ref[...]
        dxn, dgm = _rms_bwd(x_ref[...], gm_ref[...], _dot_nt(dproj_ref[...], wa_ref[...]))
        dx_ref[...] = dh1_ref[...] + dxn
        _accumulate(dgm_ref, dgm, first)

    quad_spec = pl.BlockSpec((4, tm, LANES), lambda r: (0, r, 0))
    outs = [jax.ShapeDtypeStruct((s, d), F32), jax.ShapeDtypeStruct((s, 2048), BF16), jax.ShapeDtypeStruct((s, 768), BF16),
            jax.ShapeDtypeStruct((s, 1024), BF16), jax.ShapeDtypeStruct((1, 256), F32), jax.ShapeDtypeStruct((1, 128), F32),
            jax.ShapeDtypeStruct((1, d), F32)]
    return pl.pallas_call(
        body, name="proj_in_bwd", grid=(s // tm,), out_shape=outs,
        in_specs=[_row_spec(tm, 512), quad_spec, _row_spec(tm, 512), _row_spec(tm, 512), quad_spec,
                  _row_spec(tm, 512), _row_spec(tm, 512), _row_spec(tm, 512), _row_spec(tm, 256), _row_spec(tm, 128),
                  _row_spec(tm, d), _row_spec(tm, d), _row_spec(tm, LANES), _row_spec(tm, LANES), _row_spec(tm, LANES),
                  _full_spec((1, 256)), _full_spec((1, 128)), _full_spec((1, d)),
                  _full_spec(w_uq.shape), _full_spec(w_ukv.shape), _full_spec(w_a.shape)],
        out_specs=[_row_spec(tm, d), _row_spec(tm, 2048), _row_spec(tm, 768), _row_spec(tm, 1024),
                   _full_spec((1, 256)), _full_spec((1, 128)), _full_spec((1, d))],
        compiler_params=_cparams("arbitrary"),
    )(dqn, dqr, dkn, dv, dkr, dq_sb, dk_sb, dv_sb, cq, ckv, x, dh1, cos, sin_a, sin_b, g_q, g_kv, g_mix,
      w_uq, w_ukv, w_a)


ANY = pl.BlockSpec(memory_space=pl.ANY)


def _place():
    return lax.axis_index("x"), lax.axis_index("y"), lax.axis_index("c")


def _all_gather(shards, name):
    n = len(shards)

    def body(*refs):
        ins, outs = refs[:n], refs[n:2 * n]
        send_sems, recv_sems, local_sems = refs[2 * n:]
        x, y, c = _place()
        me, sibling = (x, y, c), (x, y, 1 - c)
        chips = [(1 - x, y), (x, 1 - y), (1 - x, 1 - y)]

        def slot(a, px, py, pc):
            return outs[a].at[4 * px + 2 * py + pc]

        def copy(a, k, block, to, src=None):
            return pltpu.make_async_remote_copy(
                src_ref=slot(a, *block) if src is None else src, dst_ref=slot(a, *block),
                send_sem=send_sems.at[a, k], recv_sem=recv_sems.at[a, k], device_id=to, device_id_type=MESH)

        mine, first, passed = [], [], []
        for a in range(n):
            own = pltpu.make_async_copy(ins[a], slot(a, *me), local_sems.at[a])
            own.start()
            mine.append(own)
            cps = [copy(a, 0, me, sibling, src=ins[a])]
            cps += [copy(a, 1 + j, me, (*chip, c), src=ins[a]) for j, chip in enumerate(chips)]
            for cp in cps:
                cp.start()
            first += cps
        for a in range(n):
            for j, chip in enumerate(chips):
                copy(a, 1 + j, (*chip, c), me).wait_recv()
                fwd = copy(a, 4 + j, (*chip, c), sibling)
                fwd.start()
                passed.append(fwd)
        for a in range(n):
            copy(a, 0, sibling, me).wait_recv()
            for j, chip in enumerate(chips):
                copy(a, 4 + j, (*chip, 1 - c), me).wait_recv()
        for cp in first + passed:
            cp.wait_send()
        for own in mine:
            own.wait()

    return pl.pallas_call(
        body, name=name,
        out_shape=[jax.ShapeDtypeStruct((N_DEV,) + v.shape, v.dtype) for v in shards],
        in_specs=[ANY] * n, out_specs=[ANY] * n,
        scratch_shapes=[pltpu.SemaphoreType.DMA((n, 7)), pltpu.SemaphoreType.DMA((n, 7)), pltpu.SemaphoreType.DMA((n,))],
    )(*shards)


class _Exchange:
    def __init__(self, gather, ins, outs, send_sems, recv_sems, local_sems):
        self.gather, self.ins, self.outs = gather, ins, outs
        self.sems = (send_sems, recv_sems, local_sems)
        x, y, c = _place()
        self.me = 4 * x + 2 * y + c
        self.peers = []
        for k in range(1, N_DEV):
            px = 1 - x if k & 4 else x
            py = 1 - y if k & 2 else y
            pc = 1 - c if k & 1 else c
            self.peers.append(((px, py, pc), 4 * px + 2 * py + pc))

    def _remote(self, a, k, landing):
        send_sems, recv_sems, _ = self.sems
        where, number = self.peers[k]
        src = self.ins[a] if self.gather else self.ins[a].at[number]
        return pltpu.make_async_remote_copy(
            src_ref=src, dst_ref=self.outs[a].at[landing], send_sem=send_sems.at[a, k], recv_sem=recv_sems.at[a, k],
            device_id=where, device_id_type=MESH)

    def _local(self, a):
        src = self.ins[a] if self.gather else self.ins[a].at[self.me]
        return pltpu.make_async_copy(src, self.outs[a].at[self.me], self.sems[2].at[a])

    def start(self):
        for a in range(len(self.ins)):
            self._local(a).start()
            for k in range(N_DEV - 1):
                self._remote(a, k, self.me).start()

    def finish(self):
        for a in range(len(self.ins)):
            for k in range(N_DEV - 1):
                self._remote(a, k, self.peers[k][1]).wait_recv()
            for k in range(N_DEV - 1):
                self._remote(a, k, self.me).wait_send()
            self._local(a).wait()


def _exchange_shapes(gather, arrays):
    out_shape = [jax.ShapeDtypeStruct(((N_DEV,) + v.shape) if gather else v.shape, v.dtype) for v in arrays]
    n = len(arrays)
    sems = [pltpu.SemaphoreType.DMA((n, N_DEV - 1)), pltpu.SemaphoreType.DMA((n, N_DEV - 1)), pltpu.SemaphoreType.DMA((n,))]
    return out_shape, sems


def _exchange(gathers, scatters, name):
    ng, ns = len(gathers), len(scatters)
    n = ng + ns

    def body(*refs):
        ins, outs, sems = refs[:n], refs[n:2 * n], refs[2 * n:]
        both = [_Exchange(True, ins[:ng], outs[:ng], *sems[:3]), _Exchange(False, ins[ng:], outs[ng:], *sems[3:])]
        for ex in both:
            ex.start()
        for ex in both:
            ex.finish()

    g_shapes, g_sems = _exchange_shapes(True, gathers)
    s_shapes, s_sems = _exchange_shapes(False, scatters)
    res = pl.pallas_call(body, name=name, out_shape=g_shapes + s_shapes, in_specs=[ANY] * n, out_specs=[ANY] * n,
                         scratch_shapes=g_sems + s_sems)(*gathers, *scatters)
    return res[:ng], res[ng:]


def _grad_row_tile(rows):
    return _largest_tile_rows(rows, 256)


def _largest_tile_rows(rows, cap):
    for cand in range(cap, 0, -8):
        if rows % cand == 0:
            return cand
    return rows


def _adamw_math(w, g, m, v):
    m_new = ADAM_B1 * m + (1.0 - ADAM_B1) * g
    v_new = ADAM_B2 * v + (1.0 - ADAM_B2) * (g * g)
    m_hat = m_new / (1.0 - ADAM_B1 ** ADAM_STEP)
    v_hat = v_new / (1.0 - ADAM_B2 ** ADAM_STEP)
    delta = -ADAM_LR * (m_hat / (jnp.sqrt(v_hat) + ADAM_EPS) + ADAM_WD * w)
    return delta, m_new, v_new


def _adamw(slots, w, m, v, name):
    k, r, cdim = slots.shape
    tr = _grad_row_tile(r)

    def body(s_ref, w_ref, m_ref, v_ref, g_ref, d_ref, mo_ref, vo_ref):
        g = s_ref[0].astype(F32)
        for q in range(1, k):
            g = g + s_ref[q].astype(F32)
        g_ref[...] = g
        d_ref[...], mo_ref[...], vo_ref[...] = _adamw_math(w_ref[...], g, m_ref[...], v_ref[...])

    blk = pl.BlockSpec((tr, cdim), lambda i: (i, 0))
    return pl.pallas_call(
        body, name=name, grid=(r // tr,), out_shape=[jax.ShapeDtypeStruct((r, cdim), F32)] * 4,
        in_specs=[pl.BlockSpec((k, tr, cdim), lambda i: (0, i, 0)), blk, blk, blk], out_specs=[blk] * 4,
        compiler_params=_cparams("arbitrary"),
    )(slots, w, m, v)


def _stack_cols(g):
    n, r, c = g.shape
    return g.transpose(1, 0, 2).reshape(r, n * c)


def _split_cols(w):
    r, nc = w.shape
    return w.reshape(r, N_DEV, nc // N_DEV).transpose(1, 0, 2)


def _rope_tables(positions):
    inv_freq = ROPE_THETA ** (-jnp.arange(0, MLA_ROPE, 2, dtype=F32) / MLA_ROPE)
    ang = positions.astype(F32).reshape(-1, 1) * inv_freq[None, :]
    cos, sin, zero = jnp.cos(ang), jnp.sin(ang), jnp.zeros_like(ang)
    reps = LANES // MLA_ROPE
    return (jnp.tile(jnp.concatenate([cos, cos], axis=1), (1, reps)),
            jnp.tile(jnp.concatenate([-sin, zero], axis=1), (1, reps)),
            jnp.tile(jnp.concatenate([zero, sin], axis=1), (1, reps)))


def _local_step(x, positions, loss_target, gains, g_in, g_uq, g_ukv, late_shards):
    norm_mix, q_norm, kv_norm, out_mla, out_sb, norm_ffn, norm_final = gains
    d = x.shape[1]
    w_in = _stack_cols(g_in)
    w_a = jnp.concatenate([w_in[:, :416], jnp.zeros((d, 96), BF16), w_in[:, 416:]], axis=1)
    w_uq = jnp.concatenate([g_uq[:, :, :MLA_NOPE].transpose(1, 0, 2).reshape(Q_LORA, -1),
                            g_uq[:, :, MLA_NOPE:].transpose(1, 0, 2).reshape(Q_LORA, -1)], axis=1)
    w_ukv = jnp.concatenate([g_ukv[:, :, :MLA_NOPE].transpose(1, 0, 2).reshape(KV_LORA, -1),
                             g_ukv[:, :, MLA_NOPE:].transpose(1, 0, 2).reshape(KV_LORA, -1)], axis=1)
    cos, sin_a, sin_b = _rope_tables(positions)

    u, cq, ckv, cqn, ckvn, qn, qr, kv, kr, qkv_sb = _proj_in_fwd(x, norm_mix, w_a, q_norm, w_uq, kv_norm, w_ukv, cos, sin_a, sin_b)
    o_mla, lse, (g_o, g_gate, g_up, g_down) = _mla_fwd(qn, qr, kv, kr, late_shards)
    w_o = g_o.reshape(-1, d)
    w_gate_t, w_up_t = g_gate.reshape(-1, d), g_up.reshape(-1, d)
    w_down = g_down.reshape(-1, d)
    o_sb, tot, swept = _sb_fwd(qkv_sb)
    merged, h1, f = _attn_out_fwd(o_mla, o_sb, out_mla, out_sb, w_o, x, norm_ffn)
    gate, up, h2 = _ffn_fwd(f, h1, w_gate_t, w_up_t, w_down)
    loss, dh2, dh2b, dg_final = _final_loss(h2, loss_target, norm_final.reshape(1, d))

    dgate, dup, act, dh1, dh1b, dg_ffn = _ffn_bwd(dh2, dh2b, gate, up, h1, norm_ffn, w_down, w_gate_t, w_up_t)
    dw_down = _tn_matmul(act, dh2b, "dw_down")
    dw_gate_t = _tn_matmul(dgate, f, "dw_gate")
    dw_up_t = _tn_matmul(dup, f, "dw_up")
    do_mla, do_sb, dg_mla, dg_sb = _attn_out_bwd(dh1b, w_o, o_mla, o_sb, out_mla, out_sb)
    dw_o = _tn_matmul(merged, dh1b, "dw_o")
    dq_sb, dk_sb, dv_sb = _sb_bwd(qkv_sb, do_sb, tot, swept)
    early = [g.reshape(N_DEV, -1, d) for g in (dw_o, dw_gate_t, dw_up_t, dw_down)]
    (dqn, dqr, dkn, dv, dkr), landed = _mla_bwd(qn, qr, kv, kr, do_mla, o_mla, lse, early)
    landed = [landed[0], landed[1].transpose(0, 2, 1), landed[2].transpose(0, 2, 1), landed[3]]
    dx, dproj, dq, dkv, dg_q, dg_kv, dg_mix = _proj_in_bwd(
        dqn, dqr, dkn, dv, dkr, dq_sb, dk_sb, dv_sb, cq, ckv, x, dh1, cos, sin_a, sin_b,
        q_norm, kv_norm, norm_mix, w_uq, w_ukv, w_a)
    dw_a = _tn_matmul(u, dproj, "dw_in")
    dw_uq = _tn_matmul(cqn, dq, "dw_uq")
    dw_ukv = _tn_matmul(ckvn, dkv, "dw_ukv")

    p_in = _split_cols(jnp.concatenate([dw_a[:, :416], dw_a[:, 512:]], axis=1))
    p_uq = jnp.concatenate([dw_uq[:, :512].reshape(Q_LORA, MLA_HEADS, MLA_NOPE),
                            dw_uq[:, 512:].reshape(Q_LORA, MLA_HEADS, MLA_ROPE)], axis=2).transpose(1, 0, 2)
    p_ukv = jnp.concatenate([dw_ukv[:, :512].reshape(KV_LORA, MLA_HEADS, MLA_NOPE),
                             dw_ukv[:, 512:].reshape(KV_LORA, MLA_HEADS, HEAD_DIM)], axis=2).transpose(1, 0, 2)
    late = [p_in, p_uq, p_ukv]
    gain_grads = [dg_mix, dg_q, dg_kv, dg_mla, dg_sb, dg_ffn, dg_final]
    return loss, dx, list(landed), late, gain_grads


def kernel(x, positions, norm_mix, w_in, q_latent_norm, w_uq, kv_latent_norm, w_ukv, out_norm_mla, out_norm_sb, w_o, norm_ffn, w_gate, w_up, w_down, norm_final, loss_target, m_norm_mix, m_w_in, m_q_latent_norm, m_w_uq, m_kv_latent_norm, m_w_ukv, m_out_norm_mla, m_out_norm_sb, m_w_o, m_norm_ffn, m_w_gate, m_w_up, m_w_down, m_norm_final, v_norm_mix, v_w_in, v_q_latent_norm, v_w_uq, v_kv_latent_norm, v_w_ukv, v_out_norm_mla, v_out_norm_sb, v_w_o, v_norm_ffn, v_w_gate, v_w_up, v_w_down, v_norm_final):
    mats = [w_in, w_uq, w_ukv, w_o, w_gate, w_up, w_down]
    mat_m = [m_w_in, m_w_uq, m_w_ukv, m_w_o, m_w_gate, m_w_up, m_w_down]
    mat_v = [v_w_in, v_w_uq, v_w_ukv, v_w_o, v_w_gate, v_w_up, v_w_down]
    mat_names = ["w_in", "w_uq", "w_ukv", "w_o", "w_gate", "w_up", "w_down"]
    gains = [norm_mix, q_latent_norm, kv_latent_norm, out_norm_mla, out_norm_sb, norm_ffn, norm_final]
    gain_m = [m_norm_mix, m_q_latent_norm, m_kv_latent_norm, m_out_norm_mla, m_out_norm_sb, m_norm_ffn, m_norm_final]
    gain_v = [v_norm_mix, v_q_latent_norm, v_kv_latent_norm, v_out_norm_mla, v_out_norm_sb, v_norm_ffn, v_norm_final]

    shards = [w[0].astype(BF16) for w in mats]
    shards[4], shards[5] = shards[4].T, shards[5].T
    g_in, g_uq, g_ukv = _all_gather(shards[:3], "weight_all_gather")

    gains2d = [g.reshape(1, -1) for g in gains]
    loss_part, dx, landed, late, gain_grads = _local_step(
        x[0], positions[0], loss_target[0], gains2d, g_in, g_uq, g_ukv, shards[3:])

    sizes = [g.size for g in gains]
    used = sum(sizes) + LANES
    rows = -(-used // (8 * LANES)) * 8

    def pack(vals, tail):
        flat = jnp.concatenate([v.reshape(-1) for v in vals] + [tail])
        return jnp.pad(flat, (0, rows * LANES - flat.size)).reshape(rows, LANES)

    (small,), scattered = _exchange([pack(gain_grads, loss_part.reshape(-1))], late, "grad_exchange")

    mat_out = [_adamw(sl, w[0], m[0], v[0], "adamw_" + nm)
               for sl, w, m, v, nm in zip(list(scattered) + landed, mats, mat_m, mat_v, mat_names)]
    zeros_tail = jnp.zeros((LANES,), F32)
    g_s, d_s, m_s, v_s = _adamw(small, pack(gains, zeros_tail), pack(gain_m, zeros_tail), pack(gain_v, zeros_tail), "adamw_gains")

    def unpack(packed):
        flat = packed.reshape(-1)
        outs, off = [], 0
        for g, n in zip(gains, sizes):
            outs.append(flat[off:off + n].reshape(g.shape))
            off += n
        return outs

    loss = g_s.reshape(-1)[sum(sizes)]

    order = ["norm_mix", "w_in", "q_latent_norm", "w_uq", "kv_latent_norm", "w_ukv", "out_norm_mla", "out_norm_sb",
             "w_o", "norm_ffn", "w_gate", "w_up", "w_down", "norm_final"]
    gain_names = ["norm_mix", "q_latent_norm", "kv_latent_norm", "out_norm_mla", "out_norm_sb", "norm_ffn", "norm_final"]
    result = [loss, dx[None]]
    for kind in range(4):
        small_parts = dict(zip(gain_names, unpack([g_s, d_s, m_s, v_s][kind])))
        mat_parts = {nm: out[kind][None] for nm, out in zip(mat_names, mat_out)}
        result += [small_parts[nm] if nm in small_parts else mat_parts[nm] for nm in order]
    return tuple(result)
```

```python
import math

import jax
import jax.numpy as jnp
from jax import lax
from jax.experimental import pallas as pl
from jax.experimental.pallas import tpu as pltpu

F32 = jnp.float32
BF16 = jnp.bfloat16
MESH = pl.DeviceIdType.MESH

EPS = 1e-6
ROPE_THETA = 10000.0
MLA_HEADS = 8
MLA_NOPE = 64
MLA_ROPE = 32
SB_HEADS = 8
HEAD_DIM = 64
Q_LORA = 256
KV_LORA = 128
MLA_SCALE = 1.0 / math.sqrt(MLA_NOPE + MLA_ROPE)
SB_SCALE = 1.0 / math.sqrt(HEAD_DIM)
LOG2E = math.log2(math.e)
SB_DEAD = -160.0
N_DEV = 8

ADAM_LR = 0.001
ADAM_B1 = 0.9
ADAM_B2 = 0.999
ADAM_EPS = 1e-08
ADAM_WD = 0.01
ADAM_STEP = 10

LANES = 128
ATT_TILE = 512
SB_TILE = 512
TRI = 256
ROW_TILE = 512
STREAM_ROW_TILE = 1024
FFN_BWD_ROW_TILE = 512
PROJ_BWD_ROW_TILE = 512
TN_BLOCK = 256
TN_RESIDENT_BYTES = 16 * 1024 * 1024
VMEM_LIMIT = 56 * 1024 * 1024
NEG = -1e30


def _cparams(*sem):
    return pltpu.CompilerParams(dimension_semantics=sem, vmem_limit_bytes=VMEM_LIMIT)


def _dot(a, b):
    return jnp.dot(a, b, preferred_element_type=F32)


def _dot_nt(a, b):
    return lax.dot_general(a, b, (((1,), (1,)), ((), ())), preferred_element_type=F32)


def _dot_tn(a, b):
    return lax.dot_general(a, b, (((0,), (0,)), ((), ())), preferred_element_type=F32)


def _rms(x, g):
    r = lax.rsqrt(jnp.mean(x * x, axis=-1, keepdims=True) + EPS)
    return x * r * g


def _rms_bwd(x, g, dy):
    r = lax.rsqrt(jnp.mean(x * x, axis=-1, keepdims=True) + EPS)
    n = x * r
    dn = dy * g
    dx = r * (dn - n * jnp.mean(dn * n, axis=-1, keepdims=True))
    return dx, jnp.sum(dy * n, axis=0, keepdims=True)


def _rope(x, cos, sin_a, sin_b):
    return x * cos + pltpu.roll(x, 112, 1) * sin_a + pltpu.roll(x, 16, 1) * sin_b


def _rope_t(g, cos, sin_a, sin_b):
    return g * cos + pltpu.roll(g * sin_a, 16, 1) + pltpu.roll(g * sin_b, 112, 1)


def _row_spec(tm, width):
    return pl.BlockSpec((tm, width), lambda r: (r, 0))


def _full_spec(shape):
    return pl.BlockSpec(shape, lambda *_: (0,) * len(shape))


def _accumulate(ref, val, first):
    @pl.when(first)
    def _():
        ref[...] = val

    @pl.when(jnp.logical_not(first))
    def _():
        ref[...] += val


def _proj_in_fwd(x, g_mix, w_a, g_q, w_uq, g_kv, w_ukv, cos, sin_a, sin_b):
    s, d = x.shape
    tm = min(ROW_TILE, s)

    def body(x_ref, gm_ref, wa_ref, gq_ref, wuq_ref, gkv_ref, wukv_ref, cos_ref, sa_ref, sb_ref,
             u_ref, cq_ref, ckv_ref, cqn_ref, ckvn_ref, qn_ref, qr_ref, kv_ref, kr_ref, sbq_ref):
        u = _rms(x_ref[...], gm_ref[...]).astype(BF16)
        u_ref[...] = u
        cq = _dot(u, wa_ref[:, 0:256])
        ckv = _dot(u, wa_ref[:, 256:384])
        kr = _dot(u, wa_ref[:, 384:512])
        cq_ref[...] = cq
        ckv_ref[...] = ckv
        cqn = _rms(cq, gq_ref[...]).astype(BF16)
        ckvn = _rms(ckv, gkv_ref[...]).astype(BF16)
        cqn_ref[...] = cqn
        ckvn_ref[...] = ckvn
        cos_t, sa_t, sb_t = cos_ref[...], sa_ref[...], sb_ref[...]
        qn_ref[...] = (_dot(cqn, wuq_ref[:, 0:512]) * MLA_SCALE).astype(BF16)
        for half in range(2):
            lo = 512 + half * LANES
            qr = _dot(cqn, wuq_ref[:, lo:lo + LANES])
            qr_ref[:, half * LANES:(half + 1) * LANES] = (_rope(qr, cos_t, sa_t, sb_t) * MLA_SCALE).astype(BF16)
        kv_ref[...] = _dot(ckvn, wukv_ref[...]).astype(BF16)
        krt = kr + pltpu.roll(kr, 32, 1) + pltpu.roll(kr, 64, 1) + pltpu.roll(kr, 96, 1)
        kr_ref[...] = _rope(krt, cos_t, sa_t, sb_t).astype(BF16)
        sbq_ref[:, 0:512] = (_dot(u, wa_ref[:, 512:1024]) * (SB_SCALE * LOG2E)).astype(BF16)
        sbq_ref[:, 512:1536] = _dot(u, wa_ref[:, 1024:2048]).astype(BF16)

    outs = [
        jax.ShapeDtypeStruct((s, d), BF16),
        jax.ShapeDtypeStruct((s, 256), F32),
        jax.ShapeDtypeStruct((s, 128), F32),
        jax.ShapeDtypeStruct((s, 256), BF16),
        jax.ShapeDtypeStruct((s, 128), BF16),
        jax.ShapeDtypeStruct((s, 512), BF16),
        jax.ShapeDtypeStruct((s, 256), BF16),
        jax.ShapeDtypeStruct((s, 1024), BF16),
        jax.ShapeDtypeStruct((s, 128), BF16),
        jax.ShapeDtypeStruct((s, 1536), BF16),
    ]
    return pl.pallas_call(
        body, name="proj_in_fwd", grid=(s // tm,), out_shape=outs,
        in_specs=[_row_spec(tm, d), _full_spec(g_mix.shape), _full_spec(w_a.shape), _full_spec(g_q.shape),
                  _full_spec(w_uq.shape), _full_spec(g_kv.shape), _full_spec(w_ukv.shape),
                  _row_spec(tm, LANES), _row_spec(tm, LANES), _row_spec(tm, LANES)],
        out_specs=[_row_spec(tm, o.shape[1]) for o in outs],
        compiler_params=_cparams("arbitrary"),
    )(x, g_mix, w_a, g_q, w_uq, g_kv, w_ukv, cos, sin_a, sin_b)


def _attn_out_fwd(o_mla, o_sb, g_mla, g_sb, w_o, x, g_ffn):
    s, d = x.shape
    tm = min(STREAM_ROW_TILE, s)

    def body(oa_ref, ob_ref, ga_ref, gb_ref, wo_ref, x_ref, gf_ref, merged_ref, h1_ref, f_ref):
        na = _rms(oa_ref[...], ga_ref[...]).astype(BF16)
        nb = _rms(ob_ref[...], gb_ref[...]).astype(BF16)
        merged_ref[:, 0:512] = na
        merged_ref[:, 512:1024] = nb
        h1 = x_ref[...] + _dot(na, wo_ref[0:512, :]) + _dot(nb, wo_ref[512:1024, :])
        h1_ref[...] = h1
        f_ref[...] = _rms(h1, gf_ref[...]).astype(BF16)

    outs = [jax.ShapeDtypeStruct((s, d), BF16), jax.ShapeDtypeStruct((s, d), F32), jax.ShapeDtypeStruct((s, d), BF16)]
    return pl.pallas_call(
        body, name="attn_out_fwd", grid=(s // tm,), out_shape=outs,
        in_specs=[_row_spec(tm, 512), _row_spec(tm, 512), _full_spec(g_mla.shape), _full_spec(g_sb.shape),
                  _full_spec(w_o.shape), _row_spec(tm, d), _full_spec(g_ffn.shape)],
        out_specs=[_row_spec(tm, d)] * 3,
        compiler_params=_cparams("arbitrary"),
    )(o_mla, o_sb, g_mla, g_sb, w_o, x, g_ffn)


def _ffn_tile(d_ff):
    return d_ff // 2 if (d_ff // 2) % LANES == 0 else d_ff


def _ffn_fwd(f, h1, w_gate_t, w_up_t, w_down):
    s, d = h1.shape
    d_ff = w_gate_t.shape[0]
    tm = min(ROW_TILE, s)
    tf = _ffn_tile(d_ff)

    def body(f_ref, h1_ref, wgt_ref, wut_ref, wd_ref, gate_ref, up_ref, h2_ref):
        j = pl.program_id(1)
        fb = f_ref[...]
        gate = _dot_nt(fb, wgt_ref[...])
        up = _dot_nt(fb, wut_ref[...])
        gate_ref[...] = gate.astype(BF16)
        up_ref[...] = up.astype(BF16)
        act = (gate * jax.nn.sigmoid(gate) * up).astype(BF16)
        part = _dot(act, wd_ref[...])

        @pl.when(j == 0)
        def _():
            h2_ref[...] = h1_ref[...] + part

        @pl.when(j != 0)
        def _():
            h2_ref[...] += part

    outs = [jax.ShapeDtypeStruct((s, d_ff), BF16), jax.ShapeDtypeStruct((s, d_ff), BF16), jax.ShapeDtypeStruct((s, d), F32)]
    return pl.pallas_call(
        body, name="ffn_fwd", grid=(s // tm, d_ff // tf), out_shape=outs,
        in_specs=[pl.BlockSpec((tm, d), lambda r, j: (r, 0)), pl.BlockSpec((tm, d), lambda r, j: (r, 0)),
                  pl.BlockSpec((tf, d), lambda r, j: (j, 0)), pl.BlockSpec((tf, d), lambda r, j: (j, 0)),
                  pl.BlockSpec((tf, d), lambda r, j: (j, 0))],
        out_specs=[pl.BlockSpec((tm, tf), lambda r, j: (r, j)), pl.BlockSpec((tm, tf), lambda r, j: (r, j)),
                   pl.BlockSpec((tm, d), lambda r, j: (r, 0))],
        compiler_params=_cparams("arbitrary", "arbitrary"),
    )(f, h1, w_gate_t, w_up_t, w_down)


def _final_loss(h2, target, g_final):
    s, d = h2.shape
    tm = min(STREAM_ROW_TILE, s)

    def body(h2_ref, t_ref, g_ref, loss_ref, dh2_ref, dh2b_ref, dg_ref):
        first = pl.program_id(0) == 0
        h2v = h2_ref[...]
        g = g_ref[...]
        diff = _rms(h2v, g) - t_ref[...]
        part = 0.5 * jnp.sum(jnp.mean(diff * diff, axis=-1, keepdims=True), axis=0, keepdims=True)
        _accumulate(loss_ref, jnp.broadcast_to(part, loss_ref.shape), first)
        dx, dg = _rms_bwd(h2v, g, diff * (1.0 / d))
        dh2_ref[...] = dx
        dh2b_ref[...] = dx.astype(BF16)
        _accumulate(dg_ref, dg, first)

    outs = [jax.ShapeDtypeStruct((1, LANES), F32), jax.ShapeDtypeStruct((s, d), F32), jax.ShapeDtypeStruct((s, d), BF16),
            jax.ShapeDtypeStruct((1, d), F32)]
    return pl.pallas_call(
        body, name="final_loss", grid=(s // tm,), out_shape=outs,
        in_specs=[_row_spec(tm, d), _row_spec(tm, d), _full_spec((1, d))],
        out_specs=[_full_spec((1, LANES)), _row_spec(tm, d), _row_spec(tm, d), _full_spec((1, d))],
        compiler_params=_cparams("arbitrary"),
    )(h2, target, g_final)


def _tile_iotas(t):
    return lax.broadcasted_iota(jnp.int32, (t, t), 0), lax.broadcasted_iota(jnp.int32, (t, t), 1)


def _mla_fwd(qn, qr, kv, kr, shards):
    s = qn.shape[0]
    t = min(ATT_TILE, s)
    pairs = MLA_HEADS // 2
    nq = s // t
    n = len(shards)

    def body(*refs):
        qn_ref, qr_ref, kn_ref, v_ref, kr_ref = refs[:5]
        o_ref, lse_ref = refs[5 + n:7 + n]
        qcat_ref, m_ref, l_ref, acc_ref = refs[7 + 2 * n:11 + 2 * n]
        hp, i = pl.program_id(0), pl.program_id(1)
        ride = _Exchange(True, refs[5:5 + n], refs[7 + n:7 + 2 * n], *refs[11 + 2 * n:])

        @pl.when((hp == 0) & (i == 0))
        def _():
            ride.start()

        lane = lax.broadcasted_iota(jnp.int32, (1, LANES), 1)
        row, col = _tile_iotas(t)
        causal = col <= row
        q_pair, q_quad = qn_ref[...], qr_ref[...]
        zero = jnp.zeros_like(q_pair)
        for hh in range(2):
            in_head = (lane // HEAD_DIM) == hh
            in_rope = (lane // MLA_ROPE) == (hp % 2) * 2 + hh
            qcat_ref[hh * t:(hh + 1) * t, 0:LANES] = jnp.where(in_head, q_pair, zero)
            qcat_ref[hh * t:(hh + 1) * t, LANES:2 * LANES] = jnp.where(in_rope, q_quad, zero)
        m_ref[...] = jnp.full_like(m_ref, NEG)
        l_ref[...] = jnp.zeros_like(l_ref)
        acc_ref[...] = jnp.zeros_like(acc_ref)

        def tile(j, width, masked):
            rows = pl.ds(pl.multiple_of(j * t, t), width * t)
            kcat = jnp.concatenate([kn_ref[rows, :], kr_ref[rows, :]], axis=1)
            v_ones = jnp.concatenate([v_ref[rows, :], jnp.ones((width * t, LANES), BF16)], axis=1)
            scores = [_dot_nt(qcat_ref[hh * t:(hh + 1) * t, :], kcat) for hh in range(2)]
            for hh in range(2):
                half = slice(hh * t, (hh + 1) * t)
                sc = jnp.where(causal, scores[hh], NEG) if masked else scores[hh]
                m = m_ref[half, :]
                m_new = jnp.maximum(m, jnp.max(sc, axis=-1, keepdims=True))
                alpha = jnp.exp(m - m_new)
                p = jnp.exp(sc - jnp.concatenate([m_new] * (width * t // LANES), axis=1))
                pv = _dot(p.astype(BF16), v_ones)
                l_ref[half, :] = alpha * l_ref[half, :] + pv[:, LANES:]
                acc_ref[half, :] = alpha * acc_ref[half, :] + pv[:, :LANES]
                m_ref[half, :] = m_new

        tile(i, 1, True)

        def step(n, carry):
            tile(4 * n, 4, False)
            return carry

        lax.fori_loop(0, i // 4, step, 0)

        @pl.when(i % 4 >= 2)
        def _():
            tile((i // 4) * 4, 2, False)

        @pl.when(i % 2 == 1)
        def _():
            tile(i - 1, 1, False)

        first = (lane // HEAD_DIM) == 0
        o = acc_ref[...] / l_ref[...]
        lse = m_ref[...] + jnp.log(l_ref[...])
        o_ref[...] = jnp.where(first, o[0:t], o[t:2 * t])
        lse_ref[...] = jnp.where(first, lse[0:t], lse[t:2 * t])

        @pl.when((hp == pairs - 1) & (i == nq - 1))
        def _():
            ride.finish()

    gathered_shapes, sems = _exchange_shapes(True, shards)
    outs = [jax.ShapeDtypeStruct((s, 512), F32), jax.ShapeDtypeStruct((pairs, s, LANES), F32)] + gathered_shapes
    res = pl.pallas_call(
        body, name="mla_fwd", grid=(pairs, nq), out_shape=outs,
        in_specs=[pl.BlockSpec((t, LANES), lambda hp, i: (i, hp)), pl.BlockSpec((t, LANES), lambda hp, i: (i, hp // 2)),
                  pl.BlockSpec((s, LANES), lambda hp, i: (0, hp)), pl.BlockSpec((s, LANES), lambda hp, i: (0, 4 + hp)),
                  pl.BlockSpec((s, LANES), lambda hp, i: (0, 0))] + [ANY] * n,
        out_specs=[pl.BlockSpec((t, LANES), lambda hp, i: (i, hp)), pl.BlockSpec((None, t, LANES), lambda hp, i: (hp, i, 0))]
        + [ANY] * n,
        scratch_shapes=[pltpu.VMEM((2 * t, 2 * LANES), BF16), pltpu.VMEM((2 * t, LANES), F32), pltpu.VMEM((2 * t, LANES), F32),
                        pltpu.VMEM((2 * t, LANES), F32)] + sems,
        compiler_params=_cparams("arbitrary", "arbitrary"),
    )(qn, qr, kv, kv, kr, *shards)
    return res[0], res[1], res[2:]


HEADS = (0, 1)


def _sb_logs(z2, strict, masked):
    log_b = jnp.minimum(z2, 0.0) - jnp.log2(1.0 + jnp.exp2(-jnp.abs(z2)))
    log_1m = log_b - z2
    if masked:
        log_1m = jnp.where(strict, log_1m, 0.0)
    return log_1m, log_b


def _block_totals(x):
    t, w = x.shape
    nb = max(w // TRI, 1)
    bw = w // nb
    blocks = [x[:, b * bw:(b + 1) * bw] for b in range(nb)]
    totals = [jnp.broadcast_to(jnp.sum(blk, axis=-1, keepdims=True), (t, LANES)) for blk in blocks]
    whole = totals[0]
    for tot in totals[1:]:
        whole = whole + tot
    return blocks, totals, whole


def _running_sums(blocks, totals, tri, carry, suffix):
    nb = len(blocks)
    reps = blocks[0].shape[1] // LANES
    outs = [None] * nb
    run = carry
    for b in (range(nb - 1, -1, -1) if suffix else range(nb)):
        outs[b] = _dot(blocks[b].astype(BF16), tri) + jnp.concatenate([run] * reps, axis=1)
        run = run + totals[b]
    return outs[0] if nb == 1 else jnp.concatenate(outs, axis=1)


def _tri(t, rel):
    n = min(TRI, t)
    row, col = _tile_iotas(n)
    return rel(row, col).astype(BF16)


def _sweep_width(t):
    return t // 2 if t // 2 >= TRI else t


def _sb_fwd(qkv):
    s = qkv.shape[0]
    t = min(SB_TILE, s)
    sw = _sweep_width(t)
    pairs = SB_HEADS // 2

    def body(q_ref, k_ref, v_ref, o_ref, tot_ref, cnt_ref, qm_ref, right_ref, acc_ref):
        i = pl.program_id(1)
        lane = lax.broadcasted_iota(jnp.int32, (1, LANES), 1)
        row, col = _tile_iotas(t)
        strict = col < row
        t_suffix = _tri(t, lambda r, c: r > c)
        q_pair = q_ref[...]
        for hh in range(2):
            qm_ref[hh] = jnp.where((lane // HEAD_DIM) == hh, q_pair, jnp.zeros_like(q_pair))
        right_ref[...] = jnp.zeros_like(right_ref)
        acc_ref[...] = jnp.zeros_like(acc_ref)

        def tile(start, width, masked):
            rows = pl.ds(pl.multiple_of(start, width), width)
            k, v = k_ref[rows, :], v_ref[rows, :]
            for hh in HEADS:
                log_1m, log_b = _sb_logs(_dot_nt(qm_ref[hh], k), strict, masked)
                blocks, totals, whole = _block_totals(log_1m)
                a = jnp.exp2(log_b + _running_sums(blocks, totals, t_suffix, right_ref[hh], True))
                if masked:
                    a = jnp.where(strict, a, 0.0)
                right_ref[hh] += whole
                acc_ref[hh] += _dot(a.astype(BF16), v)

        tile(i * t, t, True)

        def alive(n):
            return (n < i * (t // sw)) & (jnp.max(right_ref[...]) > SB_DEAD)

        def step(n):
            tile((i * (t // sw) - 1 - n) * sw, sw, False)
            return n + 1

        swept = lax.while_loop(alive, step, jnp.int32(0))
        cnt_ref[...] = jnp.full(cnt_ref.shape, swept.astype(F32))
        first = (lane // HEAD_DIM) == 0
        o_ref[...] = jnp.where(first, acc_ref[0], acc_ref[1])
        tot_ref[...] = jnp.where(first, right_ref[0], right_ref[1])

    outs = [jax.ShapeDtypeStruct((s, 512), F32), jax.ShapeDtypeStruct((pairs, s, LANES), F32),
            jax.ShapeDtypeStruct((pairs, s // t, 8, LANES), F32)]
    return pl.pallas_call(
        body, name="sb_fwd", grid=(pairs, s // t), out_shape=outs,
        in_specs=[pl.BlockSpec((t, LANES), lambda hp, i: (i, hp)), pl.BlockSpec((s, LANES), lambda hp, i: (0, 4 + hp)),
                  pl.BlockSpec((s, LANES), lambda hp, i: (0, 8 + hp))],
        out_specs=[pl.BlockSpec((t, LANES), lambda hp, i: (i, hp)), pl.BlockSpec((None, t, LANES), lambda hp, i: (hp, i, 0)),
                   pl.BlockSpec((None, None, 8, LANES), lambda hp, i: (hp, i, 0, 0))],
        scratch_shapes=[pltpu.VMEM((2, t, LANES), BF16), pltpu.VMEM((2, t, LANES), F32), pltpu.VMEM((2, t, LANES), F32)],
        compiler_params=_cparams("arbitrary", "arbitrary"),
    )(qkv, qkv, qkv)


def _sb_bwd(qkv, do, tot, cnt):
    s = qkv.shape[0]
    t = min(SB_TILE, s)
    sw = _sweep_width(t)
    pairs = SB_HEADS // 2

    def body(q_ref, k_ref, v_ref, do_ref, tot_ref, cnt_ref, dq_ref, dk_ref, dv_ref,
             qm_ref, dob_ref, total_s, left_l, left_g, dq_s, dk_s, dv_s):
        i = pl.program_id(1)

        @pl.when(i == 0)
        def _():
            dk_s[...] = jnp.zeros_like(dk_s)
            dv_s[...] = jnp.zeros_like(dv_s)

        lane = lax.broadcasted_iota(jnp.int32, (1, LANES), 1)
        row, col = _tile_iotas(t)
        strict = col < row
        t_suffix = _tri(t, lambda r, c: r > c)
        t_excl = _tri(t, lambda r, c: r < c)
        q_pair, do_pair, tot_pair = q_ref[...], do_ref[...], tot_ref[...]
        for hh in range(2):
            in_head = (lane // HEAD_DIM) == hh
            qm_ref[hh] = jnp.where(in_head, q_pair, jnp.zeros_like(q_pair))
            dob_ref[hh] = jnp.where(in_head, do_pair, 0.0).astype(BF16)
            total_s[hh] = jnp.broadcast_to(
                jnp.sum(jnp.where(lane == hh * HEAD_DIM, tot_pair, 0.0), axis=-1, keepdims=True), (t, LANES))
        left_l[...] = jnp.zeros_like(left_l)
        left_g[...] = jnp.zeros_like(left_g)
        dq_s[...] = jnp.zeros_like(dq_s)

        def tile(start, width, masked):
            rows = pl.ds(pl.multiple_of(start, width), width)
            k, v = k_ref[rows, :], v_ref[rows, :]
            z2 = [_dot_nt(qm_ref[hh], k) for hh in HEADS]
            d_a = [_dot_nt(dob_ref[hh], v) for hh in HEADS]
            for hh in HEADS:
                qm, dob = qm_ref[hh], dob_ref[hh]
                log_1m, log_b = _sb_logs(z2[hh], strict, masked)
                blocks, totals, whole = _block_totals(log_1m)
                done = left_l[hh] + whole
                left_l[hh] = done
                a = jnp.exp2(log_b + _running_sums(blocks, totals, t_suffix, total_s[hh] - done, True))
                if masked:
                    a = jnp.where(strict, a, 0.0)
                g = a * d_a[hh]
                blocks, totals, whole = _block_totals(g)
                before = _running_sums(blocks, totals, t_excl, left_g[hh], False)
                left_g[hh] += whole
                dz = g - jnp.exp2(log_b) * (g + before)
                if masked:
                    dz = jnp.where(strict, dz, 0.0)
                dzb = dz.astype(BF16)
                dq_s[hh] += _dot(dzb, k)
                dk_s[rows, :] += _dot_tn(dzb, qm)
                dv_s[rows, :] += _dot_tn(a.astype(BF16), dob)

        def step(h, carry):
            tile(h * sw, sw, False)
            return carry

        swept = jnp.max(cnt_ref[...]).astype(jnp.int32)
        lax.fori_loop(i * (t // sw) - swept, i * (t // sw), step, 0)
        tile(i * t, t, True)
        dq_ref[...] = (jnp.where((lane // HEAD_DIM) == 0, dq_s[0], dq_s[1]) * SB_SCALE).astype(BF16)

        @pl.when(i == s // t - 1)
        def _():
            dk_ref[...] = (dk_s[...] * (1.0 / LOG2E)).astype(BF16)
            dv_ref[...] = dv_s[...].astype(BF16)

    outs = [jax.ShapeDtypeStruct((s, 512), BF16)] * 3
    return pl.pallas_call(
        body, name="sb_bwd", grid=(pairs, s // t), out_shape=outs,
        in_specs=[pl.BlockSpec((t, LANES), lambda hp, i: (i, hp)), pl.BlockSpec((s, LANES), lambda hp, i: (0, 4 + hp)),
                  pl.BlockSpec((s, LANES), lambda hp, i: (0, 8 + hp)), pl.BlockSpec((t, LANES), lambda hp, i: (i, hp)),
                  pl.BlockSpec((None, t, LANES), lambda hp, i: (hp, i, 0)),
                  pl.BlockSpec((None, None, 8, LANES), lambda hp, i: (hp, i, 0, 0))],
        out_specs=[pl.BlockSpec((t, LANES), lambda hp, i: (i, hp)), pl.BlockSpec((s, LANES), lambda hp, i: (0, hp)),
                   pl.BlockSpec((s, LANES), lambda hp, i: (0, hp))],
        scratch_shapes=[pltpu.VMEM((2, t, LANES), BF16), pltpu.VMEM((2, t, LANES), BF16)]
        + [pltpu.VMEM((2, t, LANES), F32)] * 4 + [pltpu.VMEM((s, LANES), F32)] * 2,
        compiler_params=_cparams("arbitrary", "arbitrary"),
    )(qkv, qkv, qkv, do, tot, cnt)


def _mla_bwd(qn, qr, kv, kr, do, o, lse, parts):
    s = qn.shape[0]
    t = min(ATT_TILE, s)
    pairs = MLA_HEADS // 2
    nq = s // t
    n = len(parts)

    def body(*refs):
        qn_ref, qr_ref, kn_ref, v_ref, kr_ref, do_ref, o_ref, lse_ref = refs[:8]
        dqn_ref, dqr_ref, dkn_ref, dv_ref, dkr_ref = refs[8 + n:13 + n]
        qcat_ref, dob_ref, lse_s, delta_s, dq_s, dkn_s, dv_s, dkr_s = refs[13 + 2 * n:21 + 2 * n]
        hp, i = pl.program_id(0), pl.program_id(1)
        ride = _Exchange(False, refs[8:8 + n], refs[13 + n:13 + 2 * n], *refs[21 + 2 * n:])

        @pl.when((hp == 0) & (i == 0))
        def _():
            ride.start()

        @pl.when(i == 0)
        def _():
            dkn_s[...] = jnp.zeros_like(dkn_s)
            dv_s[...] = jnp.zeros_like(dv_s)
            dkr_s[...] = jnp.zeros_like(dkr_s)

        lane = lax.broadcasted_iota(jnp.int32, (1, LANES), 1)
        row, col = _tile_iotas(t)
        causal = col <= row
        q_pair, q_quad, do_pair, lse_pair = qn_ref[...], qr_ref[...], do_ref[...], lse_ref[...]
        do_o = do_pair * o_ref[...]
        zero = jnp.zeros_like(q_pair)
        ropes = []
        for hh in range(2):
            in_head = (lane // HEAD_DIM) == hh
            in_rope = (lane // MLA_ROPE) == (hp % 2) * 2 + hh
            ropes.append(in_rope)
            qcat_ref[hh, :, 0:LANES] = jnp.where(in_head, q_pair, zero)
            qcat_ref[hh, :, LANES:2 * LANES] = jnp.where(in_rope, q_quad, zero)
            dob_ref[hh] = jnp.where(in_head, do_pair, 0.0).astype(BF16)
            delta_s[hh] = jnp.broadcast_to(jnp.sum(jnp.where(in_head, do_o, 0.0), axis=-1, keepdims=True), (t, LANES))
            lse_s[hh] = jnp.broadcast_to(
                jnp.sum(jnp.where(lane == hh * HEAD_DIM, lse_pair, 0.0), axis=-1, keepdims=True), (t, LANES))
        dq_s[...] = jnp.zeros_like(dq_s)
        reps = t // LANES

        def tile(j, width, masked):
            rows = pl.ds(pl.multiple_of(j * t, t), width * t)
            kcat = jnp.concatenate([kn_ref[rows, :], kr_ref[rows, :]], axis=1)
            v = v_ref[rows, :]
            sc = [_dot_nt(qcat_ref[hh], kcat) for hh in HEADS]
            dp = [_dot_nt(dob_ref[hh], v) for hh in HEADS]
            p = [jnp.exp(sc[hh] - jnp.concatenate([lse_s[hh]] * (width * reps), axis=1)) for hh in HEADS]
            if masked:
                p = [jnp.where(causal, p[hh], 0.0) for hh in HEADS]
            ds = [(p[hh] * (dp[hh] - jnp.concatenate([delta_s[hh]] * (width * reps), axis=1))).astype(BF16) for hh in HEADS]
            for hh in HEADS:
                dq_s[hh] += _dot(ds[hh], kcat)
            dkcat = _dot_tn(ds[0], qcat_ref[0]) + _dot_tn(ds[1], qcat_ref[1])
            dkn_s[rows, :] += dkcat[:, 0:LANES]
            dkr_s[rows, :] += dkcat[:, LANES:2 * LANES]
            dv_s[rows, :] += _dot_tn(p[0].astype(BF16), dob_ref[0]) + _dot_tn(p[1].astype(BF16), dob_ref[1])

        def step(n, carry):
            tile(4 * n, 4, False)
            return carry

        lax.fori_loop(0, i // 4, step, 0)

        @pl.when(i % 4 >= 2)
        def _():
            tile((i // 4) * 4, 2, False)

        @pl.when(i % 2 == 1)
        def _():
            tile(i - 1, 1, False)

        tile(i, 1, True)
        dqn_ref[...] = (jnp.where((lane // HEAD_DIM) == 0, dq_s[0, :, 0:LANES], dq_s[1, :, 0:LANES]) * MLA_SCALE).astype(BF16)
        dqr_ref[...] = ((jnp.where(ropes[0], dq_s[0, :, LANES:2 * LANES], 0.0)
                         + jnp.where(ropes[1], dq_s[1, :, LANES:2 * LANES], 0.0)) * MLA_SCALE).astype(BF16)

        @pl.when(i == nq - 1)
        def _():
            dkn_ref[...] = dkn_s[...].astype(BF16)
            dv_ref[...] = dv_s[...].astype(BF16)
            dkr_ref[...] = dkr_s[...].astype(BF16)

        @pl.when((hp == pairs - 1) & (i == nq - 1))
        def _():
            ride.finish()

    pair_block = pl.BlockSpec((t, LANES), lambda hp, i: (i, hp))
    once = pl.Buffered(1)
    landed_shapes, sems = _exchange_shapes(False, parts)
    outs = [jax.ShapeDtypeStruct((s, 512), BF16), jax.ShapeDtypeStruct((pairs, s, LANES), BF16),
            jax.ShapeDtypeStruct((s, 512), BF16), jax.ShapeDtypeStruct((s, 512), BF16),
            jax.ShapeDtypeStruct((pairs, s, LANES), BF16)] + landed_shapes
    res = pl.pallas_call(
        body, name="mla_bwd", grid=(pairs, nq), out_shape=outs,
        in_specs=[pair_block, pl.BlockSpec((t, LANES), lambda hp, i: (i, hp // 2)),
                  pl.BlockSpec((s, LANES), lambda hp, i: (0, hp), pipeline_mode=once),
                  pl.BlockSpec((s, LANES), lambda hp, i: (0, 4 + hp), pipeline_mode=once),
                  pl.BlockSpec((s, LANES), lambda hp, i: (0, 0), pipeline_mode=once), pair_block, pair_block,
                  pl.BlockSpec((None, t, LANES), lambda hp, i: (hp, i, 0))] + [ANY] * n,
        out_specs=[pair_block, pl.BlockSpec((None, t, LANES), lambda hp, i: (hp, i, 0)),
                   pl.BlockSpec((s, LANES), lambda hp, i: (0, hp), pipeline_mode=once),
                   pl.BlockSpec((s, LANES), lambda hp, i: (0, hp), pipeline_mode=once),
                   pl.BlockSpec((None, s, LANES), lambda hp, i: (hp, 0, 0), pipeline_mode=once)] + [ANY] * n,
        scratch_shapes=[pltpu.VMEM((2, t, 2 * LANES), BF16), pltpu.VMEM((2, t, LANES), BF16), pltpu.VMEM((2, t, LANES), F32),
                        pltpu.VMEM((2, t, LANES), F32), pltpu.VMEM((2, t, 2 * LANES), F32)]
        + [pltpu.VMEM((s, LANES), F32)] * 3 + sems,
        compiler_params=_cparams("arbitrary", "arbitrary"),
    )(qn, qr, kv, kv, kr, do, o, lse, *parts)
    return res[:5], res[5:]


def _ffn_bwd(dh2, dh2b, gate, up, h1, g_ffn, w_down, w_gate_t, w_up_t):
    s, d = h1.shape
    d_ff = gate.shape[1]
    tm = min(FFN_BWD_ROW_TILE, s)
    tf = _ffn_tile(d_ff)

    def act_body(dh2b_ref, gate_ref, up_ref, wd_ref, dgate_ref, dup_ref, act_ref):
        dact = _dot_nt(dh2b_ref[...], wd_ref[...])
        gate_v = gate_ref[...].astype(F32)
        up_v = up_ref[...].astype(F32)
        sig = jax.nn.sigmoid(gate_v)
        silu = gate_v * sig
        dup_ref[...] = (dact * silu).astype(BF16)
        dgate_ref[...] = ((dact * up_v) * (sig * (1.0 + gate_v - silu))).astype(BF16)
        act_ref[...] = (silu * up_v).astype(BF16)

    ta = min(STREAM_ROW_TILE, s)
    ff = pl.BlockSpec((ta, tf), lambda j, r: (r, j))
    dgate, dup, act = pl.pallas_call(
        act_body, name="ffn_bwd_act", grid=(d_ff // tf, s // ta), out_shape=[jax.ShapeDtypeStruct((s, d_ff), BF16)] * 3,
        in_specs=[pl.BlockSpec((ta, d), lambda j, r: (r, 0)), ff, ff, pl.BlockSpec((tf, d), lambda j, r: (j, 0))],
        out_specs=[ff, ff, ff],
        compiler_params=_cparams("arbitrary", "arbitrary"),
    )(dh2b, gate, up, w_down)

    def df_body(dgate_ref, dup_ref, dh2_ref, h1_ref, g_ref, wgt_ref, wut_ref, dh1_ref, dh1b_ref, dg_ref):
        df = _dot(dgate_ref[...], wgt_ref[...]) + _dot(dup_ref[...], wut_ref[...])
        dx, dg = _rms_bwd(h1_ref[...], g_ref[...], df)
        dh1 = dh2_ref[...] + dx
        dh1_ref[...] = dh1
        dh1b_ref[...] = dh1.astype(BF16)
        _accumulate(dg_ref, dg, pl.program_id(0) == 0)

    outs = [jax.ShapeDtypeStruct((s, d), F32), jax.ShapeDtypeStruct((s, d), BF16), jax.ShapeDtypeStruct((1, d), F32)]
    dh1, dh1b, dg = pl.pallas_call(
        df_body, name="ffn_bwd_df", grid=(s // tm,), out_shape=outs,
        in_specs=[_row_spec(tm, d_ff), _row_spec(tm, d_ff), _row_spec(tm, d), _row_spec(tm, d), _full_spec((1, d)),
                  pl.BlockSpec(w_gate_t.shape, lambda r: (0, 0), pipeline_mode=pl.Buffered(1)),
                  pl.BlockSpec(w_up_t.shape, lambda r: (0, 0), pipeline_mode=pl.Buffered(1))],
        out_specs=[_row_spec(tm, d), _row_spec(tm, d), _full_spec((1, d))],
        compiler_params=_cparams("arbitrary"),
    )(dgate, dup, dh2, h1, g_ffn, w_gate_t, w_up_t)
    return dgate, dup, act, dh1, dh1b, dg


def _tn_matmul(a, b, name):
    assert a.dtype == BF16 and b.dtype == BF16
    s, m = a.shape
    n = b.shape[1]
    if s * m * 2 <= TN_RESIDENT_BYTES:
        tm, tn = m, min(n, TN_BLOCK)
    else:
        tm, tn = TN_BLOCK, n

    def body(a_ref, b_ref, o_ref):
        o_ref[...] = _dot_tn(a_ref[...], b_ref[...]).astype(BF16)

    return pl.pallas_call(
        body, name=name, grid=(m // tm, n // tn), out_shape=jax.ShapeDtypeStruct((m, n), BF16),
        in_specs=[pl.BlockSpec((s, tm), lambda i, j: (0, i)), pl.BlockSpec((s, tn), lambda i, j: (0, j))],
        out_specs=pl.BlockSpec((tm, tn), lambda i, j: (i, j)),
        compiler_params=_cparams("arbitrary", "arbitrary"),
    )(a, b)


def _attn_out_bwd(dh1, w_o, o_mla, o_sb, g_mla, g_sb):
    s, d = dh1.shape
    tm = min(STREAM_ROW_TILE, s)

    def body(dh1_ref, wo_ref, oa_ref, ob_ref, ga_ref, gb_ref, doa_ref, dob_ref, dga_ref, dgb_ref):
        first = pl.program_id(0) == 0
        dh1b = dh1_ref[...]
        dxa, dga = _rms_bwd(oa_ref[...], ga_ref[...], _dot_nt(dh1b, wo_ref[0:512, :]))
        dxb, dgb = _rms_bwd(ob_ref[...], gb_ref[...], _dot_nt(dh1b, wo_ref[512:1024, :]))
        doa_ref[...] = dxa
        dob_ref[...] = dxb
        _accumulate(dga_ref, dga, first)
        _accumulate(dgb_ref, dgb, first)

    outs = [jax.ShapeDtypeStruct((s, 512), F32)] * 2 + [jax.ShapeDtypeStruct((1, 512), F32)] * 2
    return pl.pallas_call(
        body, name="attn_out_bwd", grid=(s // tm,), out_shape=outs,
        in_specs=[_row_spec(tm, d), _full_spec(w_o.shape), _row_spec(tm, 512), _row_spec(tm, 512),
                  _full_spec((1, 512)), _full_spec((1, 512))],
        out_specs=[_row_spec(tm, 512), _row_spec(tm, 512), _full_spec((1, 512)), _full_spec((1, 512))],
        compiler_params=_cparams("arbitrary"),
    )(dh1, w_o, o_mla, o_sb, g_mla, g_sb)


def _proj_in_bwd(dqn, dqr, dkn, dv, dkr, dq_sb, dk_sb, dv_sb, cq, ckv, x, dh1, cos, sin_a, sin_b,
                 g_q, g_kv, g_mix, w_uq, w_ukv, w_a):
    s, d = x.shape
    tm = min(PROJ_BWD_ROW_TILE, s)

    def body(dqn_ref, dqr_ref, dkn_ref, dv_ref, dkr_ref, dqs_ref, dks_ref, dvs_ref, cq_ref, ckv_ref, x_ref, dh1_ref,
             cos_ref, sa_ref, sb_ref, gq_ref, gkv_ref, gm_ref, wuq_ref, wukv_ref, wa_ref,
             dx_ref, dproj_ref, dq_ref, dkv_ref, dgq_ref, dgkv_ref, dgm_ref):
        first = pl.program_id(0) == 0
        lane = lax.broadcasted_iota(jnp.int32, (1, LANES), 1)
        cos_t, sa_t, sb_t = cos_ref[...], sa_ref[...], sb_ref[...]
        dq_ref[:, 0:512] = dqn_ref[...]
        for half in range(2):
            quad = dqr_ref[2 * half].astype(F32) + dqr_ref[2 * half + 1].astype(F32)
            dq_ref[:, 512 + half * LANES:512 + (half + 1) * LANES] = _rope_t(quad, cos_t, sa_t, sb_t).astype(BF16)
        dcq, dgq = _rms_bwd(cq_ref[...], gq_ref[...], _dot_nt(dq_ref[...], wuq_ref[...]))
        _accumulate(dgq_ref, dgq, first)
        dkv_ref[:, 0:512] = dkn_ref[...]
        dkv_ref[:, 512:1024] = dv_ref[...]
        dckv, dgkv = _rms_bwd(ckv_ref[...], gkv_ref[...], _dot_nt(dkv_ref[...], wukv_ref[...]))
        _accumulate(dgkv_ref, dgkv, first)
        pairs_sum = (dkr_ref[0].astype(F32) + dkr_ref[1].astype(F32)) + (dkr_ref[2].astype(F32) + dkr_ref[3].astype(F32))
        g = _rope_t(pairs_sum, cos_t, sa_t, sb_t)
        g = g + pltpu.roll(g, 96, 1) + pltpu.roll(g, 64, 1) + pltpu.roll(g, 32, 1)
        dproj_ref[:, 0:256] = dcq.astype(BF16)
        dproj_ref[:, 256:384] = dckv.astype(BF16)
        dproj_ref[:, 384:512] = jnp.where(lane < MLA_ROPE, g, 0.0).astype(BF16)
        dproj_ref[:, 512:1024] = dqs_ref[...]
        dproj_ref[:, 1024:1536] = dks_ref[...]
        dproj_ref[:, 1536:2048] = dvs_ref[...]
        dxn, dgm = _rms_bwd(x_ref[...], gm_ref[...], _dot_nt(dproj_ref[...], wa_ref[...]))
        dx_ref[...] = dh1_ref[...] + dxn
        _accumulate(dgm_ref, dgm, first)

    quad_spec = pl.BlockSpec((4, tm, LANES), lambda r: (0, r, 0))
    outs = [jax.ShapeDtypeStruct((s, d), F32), jax.ShapeDtypeStruct((s, 2048), BF16), jax.ShapeDtypeStruct((s, 768), BF16),
            jax.ShapeDtypeStruct((s, 1024), BF16), jax.ShapeDtypeStruct((1, 256), F32), jax.ShapeDtypeStruct((1, 128), F32),
            jax.ShapeDtypeStruct((1, d), F32)]
    return pl.pallas_call(
        body, name="proj_in_bwd", grid=(s // tm,), out_shape=outs,
        in_specs=[_row_spec(tm, 512), quad_spec, _row_spec(tm, 512), _row_spec(tm, 512), quad_spec,
                  _row_spec(tm, 512), _row_spec(tm, 512), _row_spec(tm, 512), _row_spec(tm, 256), _row_spec(tm, 128),
                  _row_spec(tm, d), _row_spec(tm, d), _row_spec(tm, LANES), _row_spec(tm, LANES), _row_spec(tm, LANES),
                  _full_spec((1, 256)), _full_spec((1, 128)), _full_spec((1, d)),
                  _full_spec(w_uq.shape), _full_spec(w_ukv.shape), _full_spec(w_a.shape)],
        out_specs=[_row_spec(tm, d), _row_spec(tm, 2048), _row_spec(tm, 768), _row_spec(tm, 1024),
                   _full_spec((1, 256)), _full_spec((1, 128)), _full_spec((1, d))],
        compiler_params=_cparams("arbitrary"),
    )(dqn, dqr, dkn, dv, dkr, dq_sb, dk_sb, dv_sb, cq, ckv, x, dh1, cos, sin_a, sin_b, g_q, g_kv, g_mix,
      w_uq, w_ukv, w_a)


ANY = pl.BlockSpec(memory_space=pl.ANY)


def _place():
    return lax.axis_index("x"), lax.axis_index("y"), lax.axis_index("c")


def _all_gather(shards, name):
    n = len(shards)

    def body(*refs):
        ins, outs = refs[:n], refs[n:2 * n]
        send_sems, recv_sems, local_sems = refs[2 * n:]
        x, y, c = _place()
        me, sibling = (x, y, c), (x, y, 1 - c)
        chips = [(1 - x, y), (x, 1 - y), (1 - x, 1 - y)]

        def slot(a, px, py, pc):
            return outs[a].at[4 * px + 2 * py + pc]

        def copy(a, k, block, to, src=None):
            return pltpu.make_async_remote_copy(
                src_ref=slot(a, *block) if src is None else src, dst_ref=slot(a, *block),
                send_sem=send_sems.at[a, k], recv_sem=recv_sems.at[a, k], device_id=to, device_id_type=MESH)

        mine, first, passed = [], [], []
        for a in range(n):
            own = pltpu.make_async_copy(ins[a], slot(a, *me), local_sems.at[a])
            own.start()
            mine.append(own)
            cps = [copy(a, 0, me, sibling, src=ins[a])]
            cps += [copy(a, 1 + j, me, (*chip, c), src=ins[a]) for j, chip in enumerate(chips)]
            for cp in cps:
                cp.start()
            first += cps
        for a in range(n):
            for j, chip in enumerate(chips):
                copy(a, 1 + j, (*chip, c), me).wait_recv()
                fwd = copy(a, 4 + j, (*chip, c), sibling)
                fwd.start()
                passed.append(fwd)
        for a in range(n):
            copy(a, 0, sibling, me).wait_recv()
            for j, chip in enumerate(chips):
                copy(a, 4 + j, (*chip, 1 - c), me).wait_recv()
        for cp in first + passed:
            cp.wait_send()
        for own in mine:
            own.wait()

    return pl.pallas_call(
        body, name=name,
        out_shape=[jax.ShapeDtypeStruct((N_DEV,) + v.shape, v.dtype) for v in shards],
        in_specs=[ANY] * n, out_specs=[ANY] * n,
        scratch_shapes=[pltpu.SemaphoreType.DMA((n, 7)), pltpu.SemaphoreType.DMA((n, 7)), pltpu.SemaphoreType.DMA((n,))],
    )(*shards)


class _Exchange:
    def __init__(self, gather, ins, outs, send_sems, recv_sems, local_sems):
        self.gather, self.ins, self.outs = gather, ins, outs
        self.sems = (send_sems, recv_sems, local_sems)
        x, y, c = _place()
        self.me = 4 * x + 2 * y + c
        self.peers = []
        for k in range(1, N_DEV):
            px = 1 - x if k & 4 else x
            py = 1 - y if k & 2 else y
            pc = 1 - c if k & 1 else c
            self.peers.append(((px, py, pc), 4 * px + 2 * py + pc))

    def _remote(self, a, k, landing):
        send_sems, recv_sems, _ = self.sems
        where, number = self.peers[k]
        src = self.ins[a] if self.gather else self.ins[a].at[number]
        return pltpu.make_async_remote_copy(
            src_ref=src, dst_ref=self.outs[a].at[landing], send_sem=send_sems.at[a, k], recv_sem=recv_sems.at[a, k],
            device_id=where, device_id_type=MESH)

    def _local(self, a):
        src = self.ins[a] if self.gather else self.ins[a].at[self.me]
        return pltpu.make_async_copy(src, self.outs[a].at[self.me], self.sems[2].at[a])

    def start(self):
        for a in range(len(self.ins)):
            self._local(a).start()
            for k in range(N_DEV - 1):
                self._remote(a, k, self.me).start()

    def finish(self):
        for a in range(len(self.ins)):
            for k in range(N_DEV - 1):
                self._remote(a, k, self.peers[k][1]).wait_recv()
            for k in range(N_DEV - 1):
                self._remote(a, k, self.me).wait_send()
            self._local(a).wait()


def _exchange_shapes(gather, arrays):
    out_shape = [jax.ShapeDtypeStruct(((N_DEV,) + v.shape) if gather else v.shape, v.dtype) for v in arrays]
    n = len(arrays)
    sems = [pltpu.SemaphoreType.DMA((n, N_DEV - 1)), pltpu.SemaphoreType.DMA((n, N_DEV - 1)), pltpu.SemaphoreType.DMA((n,))]
    return out_shape, sems


def _exchange(gathers, scatters, name):
    ng, ns = len(gathers), len(scatters)
    n = ng + ns

    def body(*refs):
        ins, outs, sems = refs[:n], refs[n:2 * n], refs[2 * n:]
        both = [_Exchange(True, ins[:ng], outs[:ng], *sems[:3]), _Exchange(False, ins[ng:], outs[ng:], *sems[3:])]
        for ex in both:
            ex.start()
        for ex in both:
            ex.finish()

    g_shapes, g_sems = _exchange_shapes(True, gathers)
    s_shapes, s_sems = _exchange_shapes(False, scatters)
    res = pl.pallas_call(body, name=name, out_shape=g_shapes + s_shapes, in_specs=[ANY] * n, out_specs=[ANY] * n,
                         scratch_shapes=g_sems + s_sems)(*gathers, *scatters)
    return res[:ng], res[ng:]


def _grad_row_tile(rows):
    return _largest_tile_rows(rows, 256)


def _largest_tile_rows(rows, cap):
    for cand in range(cap, 0, -8):
        if rows % cand == 0:
            return cand
    return rows


def _adamw_math(w, g, m, v):
    m_new = ADAM_B1 * m + (1.0 - ADAM_B1) * g
    v_new = ADAM_B2 * v + (1.0 - ADAM_B2) * (g * g)
    m_hat = m_new / (1.0 - ADAM_B1 ** ADAM_STEP)
    v_hat = v_new / (1.0 - ADAM_B2 ** ADAM_STEP)
    delta = -ADAM_LR * (m_hat / (jnp.sqrt(v_hat) + ADAM_EPS) + ADAM_WD * w)
    return delta, m_new, v_new


def _adamw(slots, w, m, v, name):
    k, r, cdim = slots.shape
    tr = _grad_row_tile(r)

    def body(s_ref, w_ref, m_ref, v_ref, g_ref, d_ref, mo_ref, vo_ref):
        g = s_ref[0].astype(F32)
        for q in range(1, k):
            g = g + s_ref[q].astype(F32)
        g_ref[...] = g
        d_ref[...], mo_ref[...], vo_ref[...] = _adamw_math(w_ref[...], g, m_ref[...], v_ref[...])

    blk = pl.BlockSpec((tr, cdim), lambda i: (i, 0))
    return pl.pallas_call(
        body, name=name, grid=(r // tr,), out_shape=[jax.ShapeDtypeStruct((r, cdim), F32)] * 4,
        in_specs=[pl.BlockSpec((k, tr, cdim), lambda i: (0, i, 0)), blk, blk, blk], out_specs=[blk] * 4,
        compiler_params=_cparams("arbitrary"),
    )(slots, w, m, v)


def _stack_cols(g):
    n, r, c = g.shape
    return g.transpose(1, 0, 2).reshape(r, n * c)


def _split_cols(w):
    r, nc = w.shape
    return w.reshape(r, N_DEV, nc // N_DEV).transpose(1, 0, 2)


def _rope_tables(positions):
    inv_freq = ROPE_THETA ** (-jnp.arange(0, MLA_ROPE, 2, dtype=F32) / MLA_ROPE)
    ang = positions.astype(F32).reshape(-1, 1) * inv_freq[None, :]
    cos, sin, zero = jnp.cos(ang), jnp.sin(ang), jnp.zeros_like(ang)
    reps = LANES // MLA_ROPE
    return (jnp.tile(jnp.concatenate([cos, cos], axis=1), (1, reps)),
            jnp.tile(jnp.concatenate([-sin, zero], axis=1), (1, reps)),
            jnp.tile(jnp.concatenate([zero, sin], axis=1), (1, reps)))


def _local_step(x, positions, loss_target, gains, g_in, g_uq, g_ukv, late_shards):
    norm_mix, q_norm, kv_norm, out_mla, out_sb, norm_ffn, norm_final = gains
    d = x.shape[1]
    w_in = _stack_cols(g_in)
    w_a = jnp.concatenate([w_in[:, :416], jnp.zeros((d, 96), BF16), w_in[:, 416:]], axis=1)
    w_uq = jnp.concatenate([g_uq[:, :, :MLA_NOPE].transpose(1, 0, 2).reshape(Q_LORA, -1),
                            g_uq[:, :, MLA_NOPE:].transpose(1, 0, 2).reshape(Q_LORA, -1)], axis=1)
    w_ukv = jnp.concatenate([g_ukv[:, :, :MLA_NOPE].transpose(1, 0, 2).reshape(KV_LORA, -1),
                             g_ukv[:, :, MLA_NOPE:].transpose(1, 0, 2).reshape(KV_LORA, -1)], axis=1)
    cos, sin_a, sin_b = _rope_tables(positions)

    u, cq, ckv, cqn, ckvn, qn, qr, kv, kr, qkv_sb = _proj_in_fwd(x, norm_mix, w_a, q_norm, w_uq, kv_norm, w_ukv, cos, sin_a, sin_b)
    o_mla, lse, (g_o, g_gate, g_up, g_down) = _mla_fwd(qn, qr, kv, kr, late_shards)
    w_o = g_o.reshape(-1, d)
    w_gate_t, w_up_t = g_gate.reshape(-1, d), g_up.reshape(-1, d)
    w_down = g_down.reshape(-1, d)
    o_sb, tot, swept = _sb_fwd(qkv_sb)
    merged, h1, f = _attn_out_fwd(o_mla, o_sb, out_mla, out_sb, w_o, x, norm_ffn)
    gate, up, h2 = _ffn_fwd(f, h1, w_gate_t, w_up_t, w_down)
    loss, dh2, dh2b, dg_final = _final_loss(h2, loss_target, norm_final.reshape(1, d))

    dgate, dup, act, dh1, dh1b, dg_ffn = _ffn_bwd(dh2, dh2b, gate, up, h1, norm_ffn, w_down, w_gate_t, w_up_t)
    dw_down = _tn_matmul(act, dh2b, "dw_down")
    dw_gate_t = _tn_matmul(dgate, f, "dw_gate")
    dw_up_t = _tn_matmul(dup, f, "dw_up")
    do_mla, do_sb, dg_mla, dg_sb = _attn_out_bwd(dh1b, w_o, o_mla, o_sb, out_mla, out_sb)
    dw_o = _tn_matmul(merged, dh1b, "dw_o")
    dq_sb, dk_sb, dv_sb = _sb_bwd(qkv_sb, do_sb, tot, swept)
    early = [g.reshape(N_DEV, -1, d) for g in (dw_o, dw_gate_t, dw_up_t, dw_down)]
    (dqn, dqr, dkn, dv, dkr), landed = _mla_bwd(qn, qr, kv, kr, do_mla, o_mla, lse, early)
    landed = [landed[0], landed[1].transpose(0, 2, 1), landed[2].transpose(0, 2, 1), landed[3]]
    dx, dproj, dq, dkv, dg_q, dg_kv, dg_mix = _proj_in_bwd(
        dqn, dqr, dkn, dv, dkr, dq_sb, dk_sb, dv_sb, cq, ckv, x, dh1, cos, sin_a, sin_b,
        q_norm, kv_norm, norm_mix, w_uq, w_ukv, w_a)
    dw_a = _tn_matmul(u, dproj, "dw_in")
    dw_uq = _tn_matmul(cqn, dq, "dw_uq")
    dw_ukv = _tn_matmul(ckvn, dkv, "dw_ukv")

    p_in = _split_cols(jnp.concatenate([dw_a[:, :416], dw_a[:, 512:]], axis=1))
    p_uq = jnp.concatenate([dw_uq[:, :512].reshape(Q_LORA, MLA_HEADS, MLA_NOPE),
                            dw_uq[:, 512:].reshape(Q_LORA, MLA_HEADS, MLA_ROPE)], axis=2).transpose(1, 0, 2)
    p_ukv = jnp.concatenate([dw_ukv[:, :512].reshape(KV_LORA, MLA_HEADS, MLA_NOPE),
                             dw_ukv[:, 512:].reshape(KV_LORA, MLA_HEADS, HEAD_DIM)], axis=2).transpose(1, 0, 2)
    late = [p_in, p_uq, p_ukv]
    gain_grads = [dg_mix, dg_q, dg_kv, dg_mla, dg_sb, dg_ffn, dg_final]
    return loss, dx, list(landed), late, gain_grads


def kernel(x, positions, norm_mix, w_in, q_latent_norm, w_uq, kv_latent_norm, w_ukv, out_norm_mla, out_norm_sb, w_o, norm_ffn, w_gate, w_up, w_down, norm_final, loss_target, m_norm_mix, m_w_in, m_q_latent_norm, m_w_uq, m_kv_latent_norm, m_w_ukv, m_out_norm_mla, m_out_norm_sb, m_w_o, m_norm_ffn, m_w_gate, m_w_up, m_w_down, m_norm_final, v_norm_mix, v_w_in, v_q_latent_norm, v_w_uq, v_kv_latent_norm, v_w_ukv, v_out_norm_mla, v_out_norm_sb, v_w_o, v_norm_ffn, v_w_gate, v_w_up, v_w_down, v_norm_final):
    mats = [w_in, w_uq, w_ukv, w_o, w_gate, w_up, w_down]
    mat_m = [m_w_in, m_w_uq, m_w_ukv, m_w_o, m_w_gate, m_w_up, m_w_down]
    mat_v = [v_w_in, v_w_uq, v_w_ukv, v_w_o, v_w_gate, v_w_up, v_w_down]
    mat_names = ["w_in", "w_uq", "w_ukv", "w_o", "w_gate", "w_up", "w_down"]
    gains = [norm_mix, q_latent_norm, kv_latent_norm, out_norm_mla, out_norm_sb, norm_ffn, norm_final]
    gain_m = [m_norm_mix, m_q_latent_norm, m_kv_latent_norm, m_out_norm_mla, m_out_norm_sb, m_norm_ffn, m_norm_final]
    gain_v = [v_norm_mix, v_q_latent_norm, v_kv_latent_norm, v_out_norm_mla, v_out_norm_sb, v_norm_ffn, v_norm_final]

    shards = [w[0].astype(BF16) for w in mats]
    shards[4], shards[5] = shards[4].T, shards[5].T
    g_in, g_uq, g_ukv = _all_gather(shards[:3], "weight_all_gather")

    gains2d = [g.reshape(1, -1) for g in gains]
    loss_part, dx, landed, late, gain_grads = _local_step(
        x[0], positions[0], loss_target[0], gains2d, g_in, g_uq, g_ukv, shards[3:])

    sizes = [g.size for g in gains]
    used = sum(sizes) + LANES
    rows = -(-used // (8 * LANES)) * 8

    def pack(vals, tail):
        flat = jnp.concatenate([v.reshape(-1) for v in vals] + [tail])
        return jnp.pad(flat, (0, rows * LANES - flat.size)).reshape(rows, LANES)

    (small,), scattered = _exchange([pack(gain_grads, loss_part.reshape(-1))], late, "grad_exchange")

    mat_out = [_adamw(sl, w[0], m[0], v[0], "adamw_" + nm)
               for sl, w, m, v, nm in zip(list(scattered) + landed, mats, mat_m, mat_v, mat_names)]
    zeros_tail = jnp.zeros((LANES,), F32)
    g_s, d_s, m_s, v_s = _adamw(small, pack(gains, zeros_tail), pack(gain_m, zeros_tail), pack(gain_v, zeros_tail), "adamw_gains")

    def unpack(packed):
        flat = packed.reshape(-1)
        outs, off = [], 0
        for g, n in zip(gains, sizes):
            outs.append(flat[off:off + n].reshape(g.shape))
            off += n
        return outs

    loss = g_s.reshape(-1)[sum(sizes)]

    order = ["norm_mix", "w_in", "q_latent_norm", "w_uq", "kv_latent_norm", "w_ukv", "out_norm_mla", "out_norm_sb",
             "w_o", "norm_ffn", "w_gate", "w_up", "w_down", "norm_final"]
    gain_names = ["norm_mix", "q_latent_norm", "kv_latent_norm", "out_norm_mla", "out_norm_sb", "norm_ffn", "norm_final"]
    result = [loss, dx[None]]
    for kind in range(4):
        small_parts = dict(zip(gain_names, unpack([g_s, d_s, m_s, v_s][kind])))
        mat_parts = {nm: out[kind][None] for nm, out in zip(mat_names, mat_out)}
        result += [small_parts[nm] if nm in small_parts else mat_parts[nm] for nm in order]
    return tuple(result)
```

```python
import math

import jax
import jax.numpy as jnp
from jax import lax
from jax.experimental import pallas as pl
from jax.experimental.pallas import tpu as pltpu

F32 = jnp.float32
BF16 = jnp.bfloat16
MESH = pl.DeviceIdType.MESH

EPS = 1e-6
ROPE_THETA = 10000.0
MLA_HEADS = 8
MLA_NOPE = 64
MLA_ROPE = 32
SB_HEADS = 8
HEAD_DIM = 64
Q_LORA = 256
KV_LORA = 128
MLA_SCALE = 1.0 / math.sqrt(MLA_NOPE + MLA_ROPE)
SB_SCALE = 1.0 / math.sqrt(HEAD_DIM)
LOG2E = math.log2(math.e)
SB_DEAD = -160.0
N_DEV = 8

ADAM_LR = 0.001
ADAM_B1 = 0.9
ADAM_B2 = 0.999
ADAM_EPS = 1e-08
ADAM_WD = 0.01
ADAM_STEP = 10

LANES = 128
ATT_TILE = 512
SB_TILE = 512
TRI = 256
ROW_TILE = 512
STREAM_ROW_TILE = 1024
FFN_BWD_ROW_TILE = 512
PROJ_BWD_ROW_TILE = 512
TN_BLOCK = 256
TN_RESIDENT_BYTES = 16 * 1024 * 1024
VMEM_LIMIT = 56 * 1024 * 1024
NEG = -1e30


def _cparams(*sem):
    return pltpu.CompilerParams(dimension_semantics=sem, vmem_limit_bytes=VMEM_LIMIT)


def _dot(a, b):
    return jnp.dot(a, b, preferred_element_type=F32)


def _dot_nt(a, b):
    return lax.dot_general(a, b, (((1,), (1,)), ((), ())), preferred_element_type=F32)


def _dot_tn(a, b):
    return lax.dot_general(a, b, (((0,), (0,)), ((), ())), preferred_element_type=F32)


def _rms(x, g):
    r = lax.rsqrt(jnp.mean(x * x, axis=-1, keepdims=True) + EPS)
    return x * r * g


def _rms_bwd(x, g, dy):
    r = lax.rsqrt(jnp.mean(x * x, axis=-1, keepdims=True) + EPS)
    n = x * r
    dn = dy * g
    dx = r * (dn - n * jnp.mean(dn * n, axis=-1, keepdims=True))
    return dx, jnp.sum(dy * n, axis=0, keepdims=True)


def _rope(x, cos, sin_a, sin_b):
    return x * cos + pltpu.roll(x, 112, 1) * sin_a + pltpu.roll(x, 16, 1) * sin_b


def _rope_t(g, cos, sin_a, sin_b):
    return g * cos + pltpu.roll(g * sin_a, 16, 1) + pltpu.roll(g * sin_b, 112, 1)


def _row_spec(tm, width):
    return pl.BlockSpec((tm, width), lambda r: (r, 0))


def _full_spec(shape):
    return pl.BlockSpec(shape, lambda *_: (0,) * len(shape))


def _accumulate(ref, val, first):
    @pl.when(first)
    def _():
        ref[...] = val

    @pl.when(jnp.logical_not(first))
    def _():
        ref[...] += val


def _proj_in_fwd(x, g_mix, w_a, g_q, w_uq, g_kv, w_ukv, cos, sin_a, sin_b):
    s, d = x.shape
    tm = min(ROW_TILE, s)

    def body(x_ref, gm_ref, wa_ref, gq_ref, wuq_ref, gkv_ref, wukv_ref, cos_ref, sa_ref, sb_ref,
             u_ref, cq_ref, ckv_ref, cqn_ref, ckvn_ref, qn_ref, qr_ref, kv_ref, kr_ref, sbq_ref):
        u = _rms(x_ref[...], gm_ref[...]).astype(BF16)
        u_ref[...] = u
        cq = _dot(u, wa_ref[:, 0:256])
        ckv = _dot(u, wa_ref[:, 256:384])
        kr = _dot(u, wa_ref[:, 384:512])
        cq_ref[...] = cq
        ckv_ref[...] = ckv
        cqn = _rms(cq, gq_ref[...]).astype(BF16)
        ckvn = _rms(ckv, gkv_ref[...]).astype(BF16)
        cqn_ref[...] = cqn
        ckvn_ref[...] = ckvn
        cos_t, sa_t, sb_t = cos_ref[...], sa_ref[...], sb_ref[...]
        qn_ref[...] = (_dot(cqn, wuq_ref[:, 0:512]) * MLA_SCALE).astype(BF16)
        for half in range(2):
            lo = 512 + half * LANES
            qr = _dot(cqn, wuq_ref[:, lo:lo + LANES])
            qr_ref[:, half * LANES:(half + 1) * LANES] = (_rope(qr, cos_t, sa_t, sb_t) * MLA_SCALE).astype(BF16)
        kv_ref[...] = _dot(ckvn, wukv_ref[...]).astype(BF16)
        krt = kr + pltpu.roll(kr, 32, 1) + pltpu.roll(kr, 64, 1) + pltpu.roll(kr, 96, 1)
        kr_ref[...] = _rope(krt, cos_t, sa_t, sb_t).astype(BF16)
        sbq_ref[:, 0:512] = (_dot(u, wa_ref[:, 512:1024]) * (SB_SCALE * LOG2E)).astype(BF16)
        sbq_ref[:, 512:1536] = _dot(u, wa_ref[:, 1024:2048]).astype(BF16)

    outs = [
        jax.ShapeDtypeStruct((s, d), BF16),
        jax.ShapeDtypeStruct((s, 256), F32),
        jax.ShapeDtypeStruct((s, 128), F32),
        jax.ShapeDtypeStruct((s, 256), BF16),
        jax.ShapeDtypeStruct((s, 128), BF16),
        jax.ShapeDtypeStruct((s, 512), BF16),
        jax.ShapeDtypeStruct((s, 256), BF16),
        jax.ShapeDtypeStruct((s, 1024), BF16),
        jax.ShapeDtypeStruct((s, 128), BF16),
        jax.ShapeDtypeStruct((s, 1536), BF16),
    ]
    return pl.pallas_call(
        body, name="proj_in_fwd", grid=(s // tm,), out_shape=outs,
        in_specs=[_row_spec(tm, d), _full_spec(g_mix.shape), _full_spec(w_a.shape), _full_spec(g_q.shape),
                  _full_spec(w_uq.shape), _full_spec(g_kv.shape), _full_spec(w_ukv.shape),
                  _row_spec(tm, LANES), _row_spec(tm, LANES), _row_spec(tm, LANES)],
        out_specs=[_row_spec(tm, o.shape[1]) for o in outs],
        compiler_params=_cparams("arbitrary"),
    )(x, g_mix, w_a, g_q, w_uq, g_kv, w_ukv, cos, sin_a, sin_b)


def _attn_out_fwd(o_mla, o_sb, g_mla, g_sb, w_o, x, g_ffn):
    s, d = x.shape
    tm = min(STREAM_ROW_TILE, s)

    def body(oa_ref, ob_ref, ga_ref, gb_ref, wo_ref, x_ref, gf_ref, merged_ref, h1_ref, f_ref):
        na = _rms(oa_ref[...], ga_ref[...]).astype(BF16)
        nb = _rms(ob_ref[...], gb_ref[...]).astype(BF16)
        merged_ref[:, 0:512] = na
        merged_ref[:, 512:1024] = nb
        h1 = x_ref[...] + _dot(na, wo_ref[0:512, :]) + _dot(nb, wo_ref[512:1024, :])
        h1_ref[...] = h1
        f_ref[...] = _rms(h1, gf_ref[...]).astype(BF16)

    outs = [jax.ShapeDtypeStruct((s, d), BF16), jax.ShapeDtypeStruct((s, d), F32), jax.ShapeDtypeStruct((s, d), BF16)]
    return pl.pallas_call(
        body, name="attn_out_fwd", grid=(s // tm,), out_shape=outs,
        in_specs=[_row_spec(tm, 512), _row_spec(tm, 512), _full_spec(g_mla.shape), _full_spec(g_sb.shape),
                  _full_spec(w_o.shape), _row_spec(tm, d), _full_spec(g_ffn.shape)],
        out_specs=[_row_spec(tm, d)] * 3,
        compiler_params=_cparams("arbitrary"),
    )(o_mla, o_sb, g_mla, g_sb, w_o, x, g_ffn)


def _ffn_tile(d_ff):
    return d_ff // 2 if (d_ff // 2) % LANES == 0 else d_ff


def _ffn_fwd(f, h1, w_gate_t, w_up_t, w_down):
    s, d = h1.shape
    d_ff = w_gate_t.shape[0]
    tm = min(ROW_TILE, s)
    tf = _ffn_tile(d_ff)

    def body(f_ref, h1_ref, wgt_ref, wut_ref, wd_ref, gate_ref, up_ref, h2_ref):
        j = pl.program_id(1)
        fb = f_ref[...]
        gate = _dot_nt(fb, wgt_ref[...])
        up = _dot_nt(fb, wut_ref[...])
        gate_ref[...] = gate.astype(BF16)
        up_ref[...] = up.astype(BF16)
        act = (gate * jax.nn.sigmoid(gate) * up).astype(BF16)
        part = _dot(act, wd_ref[...])

        @pl.when(j == 0)
        def _():
            h2_ref[...] = h1_ref[...] + part

        @pl.when(j != 0)
        def _():
            h2_ref[...] += part

    outs = [jax.ShapeDtypeStruct((s, d_ff), BF16), jax.ShapeDtypeStruct((s, d_ff), BF16), jax.ShapeDtypeStruct((s, d), F32)]
    return pl.pallas_call(
        body, name="ffn_fwd", grid=(s // tm, d_ff // tf), out_shape=outs,
        in_specs=[pl.BlockSpec((tm, d), lambda r, j: (r, 0)), pl.BlockSpec((tm, d), lambda r, j: (r, 0)),
                  pl.BlockSpec((tf, d), lambda r, j: (j, 0)), pl.BlockSpec((tf, d), lambda r, j: (j, 0)),
                  pl.BlockSpec((tf, d), lambda r, j: (j, 0))],
        out_specs=[pl.BlockSpec((tm, tf), lambda r, j: (r, j)), pl.BlockSpec((tm, tf), lambda r, j: (r, j)),
                   pl.BlockSpec((tm, d), lambda r, j: (r, 0))],
        compiler_params=_cparams("arbitrary", "arbitrary"),
    )(f, h1, w_gate_t, w_up_t, w_down)


def _final_loss(h2, target, g_final):
    s, d = h2.shape
    tm = min(STREAM_ROW_TILE, s)

    def body(h2_ref, t_ref, g_ref, loss_ref, dh2_ref, dh2b_ref, dg_ref):
        first = pl.program_id(0) == 0
        h2v = h2_ref[...]
        g = g_ref[...]
        diff = _rms(h2v, g) - t_ref[...]
        part = 0.5 * jnp.sum(jnp.mean(diff * diff, axis=-1, keepdims=True), axis=0, keepdims=True)
        _accumulate(loss_ref, jnp.broadcast_to(part, loss_ref.shape), first)
        dx, dg = _rms_bwd(h2v, g, diff * (1.0 / d))
        dh2_ref[...] = dx
        dh2b_ref[...] = dx.astype(BF16)
        _accumulate(dg_ref, dg, first)

    outs = [jax.ShapeDtypeStruct((1, LANES), F32), jax.ShapeDtypeStruct((s, d), F32), jax.ShapeDtypeStruct((s, d), BF16),
            jax.ShapeDtypeStruct((1, d), F32)]
    return pl.pallas_call(
        body, name="final_loss", grid=(s // tm,), out_shape=outs,
        in_specs=[_row_spec(tm, d), _row_spec(tm, d), _full_spec((1, d))],
        out_specs=[_full_spec((1, LANES)), _row_spec(tm, d), _row_spec(tm, d), _full_spec((1, d))],
        compiler_params=_cparams("arbitrary"),
    )(h2, target, g_final)


def _tile_iotas(t):
    return lax.broadcasted_iota(jnp.int32, (t, t), 0), lax.broadcasted_iota(jnp.int32, (t, t), 1)


def _mla_fwd(qn, qr, kv, kr, shards):
    s = qn.shape[0]
    t = min(ATT_TILE, s)
    pairs = MLA_HEADS // 2
    nq = s // t
    n = len(shards)

    def body(*refs):
        qn_ref, qr_ref, kn_ref, v_ref, kr_ref = refs[:5]
        o_ref, lse_ref = refs[5 + n:7 + n]
        qcat_ref, m_ref, l_ref, acc_ref = refs[7 + 2 * n:11 + 2 * n]
        hp, i = pl.program_id(0), pl.program_id(1)
        ride = _Exchange(True, refs[5:5 + n], refs[7 + n:7 + 2 * n], *refs[11 + 2 * n:])

        @pl.when((hp == 0) & (i == 0))
        def _():
            ride.start()

        lane = lax.broadcasted_iota(jnp.int32, (1, LANES), 1)
        row, col = _tile_iotas(t)
        causal = col <= row
        q_pair, q_quad = qn_ref[...], qr_ref[...]
        zero = jnp.zeros_like(q_pair)
        for hh in range(2):
            in_head = (lane // HEAD_DIM) == hh
            in_rope = (lane // MLA_ROPE) == (hp % 2) * 2 + hh
            qcat_ref[hh * t:(hh + 1) * t, 0:LANES] = jnp.where(in_head, q_pair, zero)
            qcat_ref[hh * t:(hh + 1) * t, LANES:2 * LANES] = jnp.where(in_rope, q_quad, zero)
        m_ref[...] = jnp.full_like(m_ref, NEG)
        l_ref[...] = jnp.zeros_like(l_ref)
        acc_ref[...] = jnp.zeros_like(acc_ref)

        def tile(j, width, masked):
            rows = pl.ds(pl.multiple_of(j * t, t), width * t)
            kcat = jnp.concatenate([kn_ref[rows, :], kr_ref[rows, :]], axis=1)
            v_ones = jnp.concatenate([v_ref[rows, :], jnp.ones((width * t, LANES), BF16)], axis=1)
            scores = [_dot_nt(qcat_ref[hh * t:(hh + 1) * t, :], kcat) for hh in range(2)]
            for hh in range(2):
                half = slice(hh * t, (hh + 1) * t)
                sc = jnp.where(causal, scores[hh], NEG) if masked else scores[hh]
                m = m_ref[half, :]
                m_new = jnp.maximum(m, jnp.max(sc, axis=-1, keepdims=True))
                alpha = jnp.exp(m - m_new)
                p = jnp.exp(sc - jnp.concatenate([m_new] * (width * t // LANES), axis=1))
                pv = _dot(p.astype(BF16), v_ones)
                l_ref[half, :] = alpha * l_ref[half, :] + pv[:, LANES:]
                acc_ref[half, :] = alpha * acc_ref[half, :] + pv[:, :LANES]
                m_ref[half, :] = m_new

        tile(i, 1, True)

        def step(n, carry):
            tile(4 * n, 4, False)
            return carry

        lax.fori_loop(0, i // 4, step, 0)

        @pl.when(i % 4 >= 2)
        def _():
            tile((i // 4) * 4, 2, False)

        @pl.when(i % 2 == 1)
        def _():
            tile(i - 1, 1, False)

        first = (lane // HEAD_DIM) == 0
        o = acc_ref[...] / l_ref[...]
        lse = m_ref[...] + jnp.log(l_ref[...])
        o_ref[...] = jnp.where(first, o[0:t], o[t:2 * t])
        lse_ref[...] = jnp.where(first, lse[0:t], lse[t:2 * t])

        @pl.when((hp == pairs - 1) & (i == nq - 1))
        def _():
            ride.finish()

    gathered_shapes, sems = _exchange_shapes(True, shards)
    outs = [jax.ShapeDtypeStruct((s, 512), F32), jax.ShapeDtypeStruct((pairs, s, LANES), F32)] + gathered_shapes
    res = pl.pallas_call(
        body, name="mla_fwd", grid=(pairs, nq), out_shape=outs,
        in_specs=[pl.BlockSpec((t, LANES), lambda hp, i: (i, hp)), pl.BlockSpec((t, LANES), lambda hp, i: (i, hp // 2)),
                  pl.BlockSpec((s, LANES), lambda hp, i: (0, hp)), pl.BlockSpec((s, LANES), lambda hp, i: (0, 4 + hp)),
                  pl.BlockSpec((s, LANES), lambda hp, i: (0, 0))] + [ANY] * n,
        out_specs=[pl.BlockSpec((t, LANES), lambda hp, i: (i, hp)), pl.BlockSpec((None, t, LANES), lambda hp, i: (hp, i, 0))]
        + [ANY] * n,
        scratch_shapes=[pltpu.VMEM((2 * t, 2 * LANES), BF16), pltpu.VMEM((2 * t, LANES), F32), pltpu.VMEM((2 * t, LANES), F32),
                        pltpu.VMEM((2 * t, LANES), F32)] + sems,
        compiler_params=_cparams("arbitrary", "arbitrary"),
    )(qn, qr, kv, kv, kr, *shards)
    return res[0], res[1], res[2:]


HEADS = (0, 1)


def _sb_logs(z2, strict, masked):
    log_b = jnp.minimum(z2, 0.0) - jnp.log2(1.0 + jnp.exp2(-jnp.abs(z2)))
    log_1m = log_b - z2
    if masked:
        log_1m = jnp.where(strict, log_1m, 0.0)
    return log_1m, log_b


def _block_totals(x):
    t, w = x.shape
    nb = max(w // TRI, 1)
    bw = w // nb
    blocks = [x[:, b * bw:(b + 1) * bw] for b in range(nb)]
    totals = [jnp.broadcast_to(jnp.sum(blk, axis=-1, keepdims=True), (t, LANES)) for blk in blocks]
    whole = totals[0]
    for tot in totals[1:]:
        whole = whole + tot
    return blocks, totals, whole


def _running_sums(blocks, totals, tri, carry, suffix):
    nb = len(blocks)
    reps = blocks[0].shape[1] // LANES
    outs = [None] * nb
    run = carry
    for b in (range(nb - 1, -1, -1) if suffix else range(nb)):
        outs[b] = _dot(blocks[b].astype(BF16), tri) + jnp.concatenate([run] * reps, axis=1)
        run = run + totals[b]
    return outs[0] if nb == 1 else jnp.concatenate(outs, axis=1)


def _tri(t, rel):
    n = min(TRI, t)
    row, col = _tile_iotas(n)
    return rel(row, col).astype(BF16)


def _sweep_width(t):
    return t // 2 if t // 2 >= TRI else t


def _sb_fwd(qkv):
    s = qkv.shape[0]
    t = min(SB_TILE, s)
    sw = _sweep_width(t)
    pairs = SB_HEADS // 2

    def body(q_ref, k_ref, v_ref, o_ref, tot_ref, cnt_ref, qm_ref, right_ref, acc_ref):
        i = pl.program_id(1)
        lane = lax.broadcasted_iota(jnp.int32, (1, LANES), 1)
        row, col = _tile_iotas(t)
        strict = col < row
        t_suffix = _tri(t, lambda r, c: r > c)
        q_pair = q_ref[...]
        for hh in range(2):
            qm_ref[hh] = jnp.where((lane // HEAD_DIM) == hh, q_pair, jnp.zeros_like(q_pair))
        right_ref[...] = jnp.zeros_like(right_ref)
        acc_ref[...] = jnp.zeros_like(acc_ref)

        def tile(start, width, masked):
            rows = pl.ds(pl.multiple_of(start, width), width)
            k, v = k_ref[rows, :], v_ref[rows, :]
            for hh in HEADS:
                log_1m, log_b = _sb_logs(_dot_nt(qm_ref[hh], k), strict, masked)
                blocks, totals, whole = _block_totals(log_1m)
                a = jnp.exp2(log_b + _running_sums(blocks, totals, t_suffix, right_ref[hh], True))
                if masked:
                    a = jnp.where(strict, a, 0.0)
                right_ref[hh] += whole
                acc_ref[hh] += _dot(a.astype(BF16), v)

        tile(i * t, t, True)

        def alive(n):
            return (n < i * (t // sw)) & (jnp.max(right_ref[...]) > SB_DEAD)

        def step(n):
            tile((i * (t // sw) - 1 - n) * sw, sw, False)
            return n + 1

        swept = lax.while_loop(alive, step, jnp.int32(0))
        cnt_ref[...] = jnp.full(cnt_ref.shape, swept.astype(F32))
        first = (lane // HEAD_DIM) == 0
        o_ref[...] = jnp.where(first, acc_ref[0], acc_ref[1])
        tot_ref[...] = jnp.where(first, right_ref[0], right_ref[1])

    outs = [jax.ShapeDtypeStruct((s, 512), F32), jax.ShapeDtypeStruct((pairs, s, LANES), F32),
            jax.ShapeDtypeStruct((pairs, s // t, 8, LANES), F32)]
    return pl.pallas_call(
        body, name="sb_fwd", grid=(pairs, s // t), out_shape=outs,
        in_specs=[pl.BlockSpec((t, LANES), lambda hp, i: (i, hp)), pl.BlockSpec((s, LANES), lambda hp, i: (0, 4 + hp)),
                  pl.BlockSpec((s, LANES), lambda hp, i: (0, 8 + hp))],
        out_specs=[pl.BlockSpec((t, LANES), lambda hp, i: (i, hp)), pl.BlockSpec((None, t, LANES), lambda hp, i: (hp, i, 0)),
                   pl.BlockSpec((None, None, 8, LANES), lambda hp, i: (hp, i, 0, 0))],
        scratch_shapes=[pltpu.VMEM((2, t, LANES), BF16), pltpu.VMEM((2, t, LANES), F32), pltpu.VMEM((2, t, LANES), F32)],
        compiler_params=_cparams("arbitrary", "arbitrary"),
    )(qkv, qkv, qkv)


def _sb_bwd(qkv, do, tot, cnt):
    s = qkv.shape[0]
    t = min(SB_TILE, s)
    sw = _sweep_width(t)
    pairs = SB_HEADS // 2

    def body(q_ref, k_ref, v_ref, do_ref, tot_ref, cnt_ref, dq_ref, dk_ref, dv_ref,
             qm_ref, dob_ref, total_s, left_l, left_g, dq_s, dk_s, dv_s):
        i = pl.program_id(1)

        @pl.when(i == 0)
        def _():
            dk_s[...] = jnp.zeros_like(dk_s)
            dv_s[...] = jnp.zeros_like(dv_s)

        lane = lax.broadcasted_iota(jnp.int32, (1, LANES), 1)
        row, col = _tile_iotas(t)
        strict = col < row
        t_suffix = _tri(t, lambda r, c: r > c)
        t_excl = _tri(t, lambda r, c: r < c)
        q_pair, do_pair, tot_pair = q_ref[...], do_ref[...], tot_ref[...]
        for hh in range(2):
            in_head = (lane // HEAD_DIM) == hh
            qm_ref[hh] = jnp.where(in_head, q_pair, jnp.zeros_like(q_pair))
            dob_ref[hh] = jnp.where(in_head, do_pair, 0.0).astype(BF16)
            total_s[hh] = jnp.broadcast_to(
                jnp.sum(jnp.where(lane == hh * HEAD_DIM, tot_pair, 0.0), axis=-1, keepdims=True), (t, LANES))
        left_l[...] = jnp.zeros_like(left_l)
        left_g[...] = jnp.zeros_like(left_g)
        dq_s[...] = jnp.zeros_like(dq_s)

        def tile(start, width, masked):
            rows = pl.ds(pl.multiple_of(start, width), width)
            k, v = k_ref[rows, :], v_ref[rows, :]
            z2 = [_dot_nt(qm_ref[hh], k) for hh in HEADS]
            d_a = [_dot_nt(dob_ref[hh], v) for hh in HEADS]
            for hh in HEADS:
                qm, dob = qm_ref[hh], dob_ref[hh]
                log_1m, log_b = _sb_logs(z2[hh], strict, masked)
                blocks, totals, whole = _block_totals(log_1m)
                done = left_l[hh] + whole
                left_l[hh] = done
                a = jnp.exp2(log_b + _running_sums(blocks, totals, t_suffix, total_s[hh] - done, True))
                if masked:
                    a = jnp.where(strict, a, 0.0)
                g = a * d_a[hh]
                blocks, totals, whole = _block_totals(g)
                before = _running_sums(blocks, totals, t_excl, left_g[hh], False)
                left_g[hh] += whole
                dz = g - jnp.exp2(log_b) * (g + before)
                if masked:
                    dz = jnp.where(strict, dz, 0.0)
                dzb = dz.astype(BF16)
                dq_s[hh] += _dot(dzb, k)
                dk_s[rows, :] += _dot_tn(dzb, qm)
                dv_s[rows, :] += _dot_tn(a.astype(BF16), dob)

        def step(h, carry):
            tile(h * sw, sw, False)
            return carry

        swept = jnp.max(cnt_ref[...]).astype(jnp.int32)
        lax.fori_loop(i * (t // sw) - swept, i * (t // sw), step, 0)
        tile(i * t, t, True)
        dq_ref[...] = (jnp.where((lane // HEAD_DIM) == 0, dq_s[0], dq_s[1]) * SB_SCALE).astype(BF16)

        @pl.when(i == s // t - 1)
        def _():
            dk_ref[...] = (dk_s[...] * (1.0 / LOG2E)).astype(BF16)
            dv_ref[...] = dv_s[...].astype(BF16)

    outs = [jax.ShapeDtypeStruct((s, 512), BF16)] * 3
    return pl.pallas_call(
        body, name="sb_bwd", grid=(pairs, s // t), out_shape=outs,
        in_specs=[pl.BlockSpec((t, LANES), lambda hp, i: (i, hp)), pl.BlockSpec((s, LANES), lambda hp, i: (0, 4 + hp)),
                  pl.BlockSpec((s, LANES), lambda hp, i: (0, 8 + hp)), pl.BlockSpec((t, LANES), lambda hp, i: (i, hp)),
                  pl.BlockSpec((None, t, LANES), lambda hp, i: (hp, i, 0)),
                  pl.BlockSpec((None, None, 8, LANES), lambda hp, i: (hp, i, 0, 0))],
        out_specs=[pl.BlockSpec((t, LANES), lambda hp, i: (i, hp)), pl.BlockSpec((s, LANES), lambda hp, i: (0, hp)),
                   pl.BlockSpec((s, LANES), lambda hp, i: (0, hp))],
        scratch_shapes=[pltpu.VMEM((2, t, LANES), BF16), pltpu.VMEM((2, t, LANES), BF16)]
        + [pltpu.VMEM((2, t, LANES), F32)] * 4 + [pltpu.VMEM((s, LANES), F32)] * 2,
        compiler_params=_cparams("arbitrary", "arbitrary"),
    )(qkv, qkv, qkv, do, tot, cnt)


def _mla_bwd(qn, qr, kv, kr, do, o, lse, parts):
    s = qn.shape[0]
    t = min(ATT_TILE, s)
    pairs = MLA_HEADS // 2
    nq = s // t
    n = len(parts)

    def body(*refs):
        qn_ref, qr_ref, kn_ref, v_ref, kr_ref, do_ref, o_ref, lse_ref = refs[:8]
        dqn_ref, dqr_ref, dkn_ref, dv_ref, dkr_ref = refs[8 + n:13 + n]
        qcat_ref, dob_ref, lse_s, delta_s, dq_s, dkn_s, dv_s, dkr_s = refs[13 + 2 * n:21 + 2 * n]
        hp, i = pl.program_id(0), pl.program_id(1)
        ride = _Exchange(False, refs[8:8 + n], refs[13 + n:13 + 2 * n], *refs[21 + 2 * n:])

        @pl.when((hp == 0) & (i == 0))
        def _():
            ride.start()

        @pl.when(i == 0)
        def _():
            dkn_s[...] = jnp.zeros_like(dkn_s)
            dv_s[...] = jnp.zeros_like(dv_s)
            dkr_s[...] = jnp.zeros_like(dkr_s)

        lane = lax.broadcasted_iota(jnp.int32, (1, LANES), 1)
        row, col = _tile_iotas(t)
        causal = col <= row
        q_pair, q_quad, do_pair, lse_pair = qn_ref[...], qr_ref[...], do_ref[...], lse_ref[...]
        do_o = do_pair * o_ref[...]
        zero = jnp.zeros_like(q_pair)
        ropes = []
        for hh in range(2):
            in_head = (lane // HEAD_DIM) == hh
            in_rope = (lane // MLA_ROPE) == (hp % 2) * 2 + hh
            ropes.append(in_rope)
            qcat_ref[hh, :, 0:LANES] = jnp.where(in_head, q_pair, zero)
            qcat_ref[hh, :, LANES:2 * LANES] = jnp.where(in_rope, q_quad, zero)
            dob_ref[hh] = jnp.where(in_head, do_pair, 0.0).astype(BF16)
            delta_s[hh] = jnp.broadcast_to(jnp.sum(jnp.where(in_head, do_o, 0.0), axis=-1, keepdims=True), (t, LANES))
            lse_s[hh] = jnp.broadcast_to(
                jnp.sum(jnp.where(lane == hh * HEAD_DIM, lse_pair, 0.0), axis=-1, keepdims=True), (t, LANES))
        dq_s[...] = jnp.zeros_like(dq_s)
        reps = t // LANES

        def tile(j, width, masked):
            rows = pl.ds(pl.multiple_of(j * t, t), width * t)
            kcat = jnp.concatenate([kn_ref[rows, :], kr_ref[rows, :]], axis=1)
            v = v_ref[rows, :]
            sc = [_dot_nt(qcat_ref[hh], kcat) for hh in HEADS]
            dp = [_dot_nt(dob_ref[hh], v) for hh in HEADS]
            p = [jnp.exp(sc[hh] - jnp.concatenate([lse_s[hh]] * (width * reps), axis=1)) for hh in HEADS]
            if masked:
                p = [jnp.where(causal, p[hh], 0.0) for hh in HEADS]
            ds = [(p[hh] * (dp[hh] - jnp.concatenate([delta_s[hh]] * (width * reps), axis=1))).astype(BF16) for hh in HEADS]
            for hh in HEADS:
                dq_s[hh] += _dot(ds[hh], kcat)
            dkcat = _dot_tn(ds[0], qcat_ref[0]) + _dot_tn(ds[1], qcat_ref[1])
            dkn_s[rows, :] += dkcat[:, 0:LANES]
            dkr_s[rows, :] += dkcat[:, LANES:2 * LANES]
            dv_s[rows, :] += _dot_tn(p[0].astype(BF16), dob_ref[0]) + _dot_tn(p[1].astype(BF16), dob_ref[1])

        def step(n, carry):
            tile(4 * n, 4, False)
            return carry

        lax.fori_loop(0, i // 4, step, 0)

        @pl.when(i % 4 >= 2)
        def _():
            tile((i // 4) * 4, 2, False)

        @pl.when(i % 2 == 1)
        def _():
            tile(i - 1, 1, False)

        tile(i, 1, True)
        dqn_ref[...] = (jnp.where((lane // HEAD_DIM) == 0, dq_s[0, :, 0:LANES], dq_s[1, :, 0:LANES]) * MLA_SCALE).astype(BF16)
        dqr_ref[...] = ((jnp.where(ropes[0], dq_s[0, :, LANES:2 * LANES], 0.0)
                         + jnp.where(ropes[1], dq_s[1, :, LANES:2 * LANES], 0.0)) * MLA_SCALE).astype(BF16)

        @pl.when(i == nq - 1)
        def _():
            dkn_ref[...] = dkn_s[...].astype(BF16)
            dv_ref[...] = dv_s[...].astype(BF16)
            dkr_ref[...] = dkr_s[...].astype(BF16)

        @pl.when((hp == pairs - 1) & (i == nq - 1))
        def _():
            ride.finish()

    pair_block = pl.BlockSpec((t, LANES), lambda hp, i: (i, hp))
    once = pl.Buffered(1)
    landed_shapes, sems = _exchange_shapes(False, parts)
    outs = [jax.ShapeDtypeStruct((s, 512), BF16), jax.ShapeDtypeStruct((pairs, s, LANES), BF16),
            jax.ShapeDtypeStruct((s, 512), BF16), jax.ShapeDtypeStruct((s, 512), BF16),
            jax.ShapeDtypeStruct((pairs, s, LANES), BF16)] + landed_shapes
    res = pl.pallas_call(
        body, name="mla_bwd", grid=(pairs, nq), out_shape=outs,
        in_specs=[pair_block, pl.BlockSpec((t, LANES), lambda hp, i: (i, hp // 2)),
                  pl.BlockSpec((s, LANES), lambda hp, i: (0, hp), pipeline_mode=once),
                  pl.BlockSpec((s, LANES), lambda hp, i: (0, 4 + hp), pipeline_mode=once),
                  pl.BlockSpec((s, LANES), lambda hp, i: (0, 0), pipeline_mode=once), pair_block, pair_block,
                  pl.BlockSpec((None, t, LANES), lambda hp, i: (hp, i, 0))] + [ANY] * n,
        out_specs=[pair_block, pl.BlockSpec((None, t, LANES), lambda hp, i: (hp, i, 0)),
                   pl.BlockSpec((s, LANES), lambda hp, i: (0, hp), pipeline_mode=once),
                   pl.BlockSpec((s, LANES), lambda hp, i: (0, hp), pipeline_mode=once),
                   pl.BlockSpec((None, s, LANES), lambda hp, i: (hp, 0, 0), pipeline_mode=once)] + [ANY] * n,
        scratch_shapes=[pltpu.VMEM((2, t, 2 * LANES), BF16), pltpu.VMEM((2, t, LANES), BF16), pltpu.VMEM((2, t, LANES), F32),
                        pltpu.VMEM((2, t, LANES), F32), pltpu.VMEM((2, t, 2 * LANES), F32)]
        + [pltpu.VMEM((s, LANES), F32)] * 3 + sems,
        compiler_params=_cparams("arbitrary", "arbitrary"),
    )(qn, qr, kv, kv, kr, do, o, lse, *parts)
    return res[:5], res[5:]


def _ffn_bwd(dh2, dh2b, gate, up, h1, g_ffn, w_down, w_gate_t, w_up_t):
    s, d = h1.shape
    d_ff = gate.shape[1]
    tm = min(FFN_BWD_ROW_TILE, s)
    tf = _ffn_tile(d_ff)

    def act_body(dh2b_ref, gate_ref, up_ref, wd_ref, dgate_ref, dup_ref, act_ref):
        dact = _dot_nt(dh2b_ref[...], wd_ref[...])
        gate_v = gate_ref[...].astype(F32)
        up_v = up_ref[...].astype(F32)
        sig = jax.nn.sigmoid(gate_v)
        silu = gate_v * sig
        dup_ref[...] = (dact * silu).astype(BF16)
        dgate_ref[...] = ((dact * up_v) * (sig * (1.0 + gate_v - silu))).astype(BF16)
        act_ref[...] = (silu * up_v).astype(BF16)

    ta = min(STREAM_ROW_TILE, s)
    ff = pl.BlockSpec((ta, tf), lambda j, r: (r, j))
    dgate, dup, act = pl.pallas_call(
        act_body, name="ffn_bwd_act", grid=(d_ff // tf, s // ta), out_shape=[jax.ShapeDtypeStruct((s, d_ff), BF16)] * 3,
        in_specs=[pl.BlockSpec((ta, d), lambda j, r: (r, 0)), ff, ff, pl.BlockSpec((tf, d), lambda j, r: (j, 0))],
        out_specs=[ff, ff, ff],
        compiler_params=_cparams("arbitrary", "arbitrary"),
    )(dh2b, gate, up, w_down)

    def df_body(dgate_ref, dup_ref, dh2_ref, h1_ref, g_ref, wgt_ref, wut_ref, dh1_ref, dh1b_ref, dg_ref):
        df = _dot(dgate_ref[...], wgt_ref[...]) + _dot(dup_ref[...], wut_ref[...])
        dx, dg = _rms_bwd(h1_ref[...], g_ref[...], df)
        dh1 = dh2_ref[...] + dx
        dh1_ref[...] = dh1
        dh1b_ref[...] = dh1.astype(BF16)
        _accumulate(dg_ref, dg, pl.program_id(0) == 0)

    outs = [jax.ShapeDtypeStruct((s, d), F32), jax.ShapeDtypeStruct((s, d), BF16), jax.ShapeDtypeStruct((1, d), F32)]
    dh1, dh1b, dg = pl.pallas_call(
        df_body, name="ffn_bwd_df", grid=(s // tm,), out_shape=outs,
        in_specs=[_row_spec(tm, d_ff), _row_spec(tm, d_ff), _row_spec(tm, d), _row_spec(tm, d), _full_spec((1, d)),
                  pl.BlockSpec(w_gate_t.shape, lambda r: (0, 0), pipeline_mode=pl.Buffered(1)),
                  pl.BlockSpec(w_up_t.shape, lambda r: (0, 0), pipeline_mode=pl.Buffered(1))],
        out_specs=[_row_spec(tm, d), _row_spec(tm, d), _full_spec((1, d))],
        compiler_params=_cparams("arbitrary"),
    )(dgate, dup, dh2, h1, g_ffn, w_gate_t, w_up_t)
    return dgate, dup, act, dh1, dh1b, dg


def _tn_matmul(a, b, name):
    assert a.dtype == BF16 and b.dtype == BF16
    s, m = a.shape
    n = b.shape[1]
    if s * m * 2 <= TN_RESIDENT_BYTES:
        tm, tn = m, min(n, TN_BLOCK)
    else:
        tm, tn = TN_BLOCK, n

    def body(a_ref, b_ref, o_ref):
        o_ref[...] = _dot_tn(a_ref[...], b_ref[...]).astype(BF16)

    return pl.pallas_call(
        body, name=name, grid=(m // tm, n // tn), out_shape=jax.ShapeDtypeStruct((m, n), BF16),
        in_specs=[pl.BlockSpec((s, tm), lambda i, j: (0, i)), pl.BlockSpec((s, tn), lambda i, j: (0, j))],
        out_specs=pl.BlockSpec((tm, tn), lambda i, j: (i, j)),
        compiler_params=_cparams("arbitrary", "arbitrary"),
    )(a, b)


def _tn_matmul_pair(a1, a2, b, name):
    assert a1.shape == a2.shape and a1.dtype == a2.dtype == b.dtype == BF16
    s, m = a1.shape
    n = b.shape[1]

    def body(a1_ref, a2_ref, b_ref, o1_ref, o2_ref):
        bv = b_ref[...]
        o1_ref[...] = _dot_tn(a1_ref[...], bv).astype(BF16)
        o2_ref[...] = _dot_tn(a2_ref[...], bv).astype(BF16)

    a_spec = pl.BlockSpec((s, TN_BLOCK), lambda i: (0, i))
    o_spec = pl.BlockSpec((TN_BLOCK, n), lambda i: (i, 0))
    return pl.pallas_call(
        body, name=name, grid=(m // TN_BLOCK,), out_shape=[jax.ShapeDtypeStruct((m, n), BF16)] * 2,
        in_specs=[a_spec, a_spec, pl.BlockSpec((s, n), lambda i: (0, 0), pipeline_mode=pl.Buffered(1))],
        out_specs=[o_spec, o_spec],
        compiler_params=_cparams("arbitrary"),
    )(a1, a2, b)


def _attn_out_bwd(dh1, w_o, o_mla, o_sb, g_mla, g_sb):
    s, d = dh1.shape
    tm = min(STREAM_ROW_TILE, s)

    def body(dh1_ref, wo_ref, oa_ref, ob_ref, ga_ref, gb_ref, doa_ref, dob_ref, dga_ref, dgb_ref):
        first = pl.program_id(0) == 0
        dh1b = dh1_ref[...]
        dxa, dga = _rms_bwd(oa_ref[...], ga_ref[...], _dot_nt(dh1b, wo_ref[0:512, :]))
        dxb, dgb = _rms_bwd(ob_ref[...], gb_ref[...], _dot_nt(dh1b, wo_ref[512:1024, :]))
        doa_ref[...] = dxa
        dob_ref[...] = dxb
        _accumulate(dga_ref, dga, first)
        _accumulate(dgb_ref, dgb, first)

    outs = [jax.ShapeDtypeStruct((s, 512), F32)] * 2 + [jax.ShapeDtypeStruct((1, 512), F32)] * 2
    return pl.pallas_call(
        body, name="attn_out_bwd", grid=(s // tm,), out_shape=outs,
        in_specs=[_row_spec(tm, d), _full_spec(w_o.shape), _row_spec(tm, 512), _row_spec(tm, 512),
                  _full_spec((1, 512)), _full_spec((1, 512))],
        out_specs=[_row_spec(tm, 512), _row_spec(tm, 512), _full_spec((1, 512)), _full_spec((1, 512))],
        compiler_params=_cparams("arbitrary"),
    )(dh1, w_o, o_mla, o_sb, g_mla, g_sb)


def _proj_in_bwd(dqn, dqr, dkn, dv, dkr, dq_sb, dk_sb, dv_sb, cq, ckv, x, dh1, cos, sin_a, sin_b,
                 g_q, g_kv, g_mix, w_uq, w_ukv, w_a):
    s, d = x.shape
    tm = min(PROJ_BWD_ROW_TILE, s)

    def body(dqn_ref, dqr_ref, dkn_ref, dv_ref, dkr_ref, dqs_ref, dks_ref, dvs_ref, cq_ref, ckv_ref, x_ref, dh1_ref,
             cos_ref, sa_ref, sb_ref, gq_ref, gkv_ref, gm_ref, wuq_ref, wukv_ref, wa_ref,
             dx_ref, dproj_ref, dq_ref, dkv_ref, dgq_ref, dgkv_ref, dgm_ref):
        first = pl.program_id(0) == 0
        lane = lax.broadcasted_iota(jnp.int32, (1, LANES), 1)
        cos_t, sa_t, sb_t = cos_ref[...], sa_ref[...], sb_ref[...]
        dq_ref[:, 0:512] = dqn_ref[...]
        for half in range(2):
            quad = dqr_ref[2 * half].astype(F32) + dqr_ref[2 * half + 1].astype(F32)
            dq_ref[:, 512 + half * LANES:512 + (half + 1) * LANES] = _rope_t(quad, cos_t, sa_t, sb_t).astype(BF16)
        dcq, dgq = _rms_bwd(cq_ref[...], gq_ref[...], _dot_nt(dq_ref[...], wuq_ref[...]))
        _accumulate(dgq_ref, dgq, first)
        dkv_ref[:, 0:512] = dkn_ref[...]
        dkv_ref[:, 512:1024] = dv_ref[...]
        dckv, dgkv = _rms_bwd(ckv_ref[...], gkv_ref[...], _dot_nt(dkv_ref[...], wukv_ref[...]))
        _accumulate(dgkv_ref, dgkv, first)
        pairs_sum = (dkr_ref[0].astype(F32) + dkr_ref[1].astype(F32)) + (dkr_ref[2].astype(F32) + dkr_ref[3].astype(F32))
        g = _rope_t(pairs_sum, cos_t, sa_t, sb_t)
        g = g + pltpu.roll(g, 96, 1) + pltpu.roll(g, 64, 1) + pltpu.roll(g, 32, 1)
        dproj_ref[:, 0:256] = dcq.astype(BF16)
        dproj_ref[:, 256:384] = dckv.astype(BF16)
        dproj_ref[:, 384:512] = jnp.where(lane < MLA_ROPE, g, 0.0).astype(BF16)
        dproj_ref[:, 512:1024] = dqs_ref[...]
        dproj_ref[:, 1024:1536] = dks_ref[...]
        dproj_ref[:, 1536:2048] = dvs_ref[...]
        dxn, dgm = _rms_bwd(x_ref[...], gm_ref[...], _dot_nt(dproj_ref[...], wa_ref[...]))
        dx_ref[...] = dh1_ref[...] + dxn
        _accumulate(dgm_ref, dgm, first)

    quad_spec = pl.BlockSpec((4, tm, LANES), lambda r: (0, r, 0))
    outs = [jax.ShapeDtypeStruct((s, d), F32), jax.ShapeDtypeStruct((s, 2048), BF16), jax.ShapeDtypeStruct((s, 768), BF16),
            jax.ShapeDtypeStruct((s, 1024), BF16), jax.ShapeDtypeStruct((1, 256), F32), jax.ShapeDtypeStruct((1, 128), F32),
            jax.ShapeDtypeStruct((1, d), F32)]
    return pl.pallas_call(
        body, name="proj_in_bwd", grid=(s // tm,), out_shape=outs,
        in_specs=[_row_spec(tm, 512), quad_spec, _row_spec(tm, 512), _row_spec(tm, 512), quad_spec,
                  _row_spec(tm, 512), _row_spec(tm, 512), _row_spec(tm, 512), _row_spec(tm, 256), _row_spec(tm, 128),
                  _row_spec(tm, d), _row_spec(tm, d), _row_spec(tm, LANES), _row_spec(tm, LANES), _row_spec(tm, LANES),
                  _full_spec((1, 256)), _full_spec((1, 128)), _full_spec((1, d)),
                  _full_spec(w_uq.shape), _full_spec(w_ukv.shape), _full_spec(w_a.shape)],
        out_specs=[_row_spec(tm, d), _row_spec(tm, 2048), _row_spec(tm, 768), _row_spec(tm, 1024),
                   _full_spec((1, 256)), _full_spec((1, 128)), _full_spec((1, d))],
        compiler_params=_cparams("arbitrary"),
    )(dqn, dqr, dkn, dv, dkr, dq_sb, dk_sb, dv_sb, cq, ckv, x, dh1, cos, sin_a, sin_b, g_q, g_kv, g_mix,
      w_uq, w_ukv, w_a)


ANY = pl.BlockSpec(memory_space=pl.ANY)


def _place():
    return lax.axis_index("x"), lax.axis_index("y"), lax.axis_index("c")


def _all_gather(shards, name):
    n = len(shards)

    def body(*refs):
        ins, outs = refs[:n], refs[n:2 * n]
        send_sems, recv_sems, local_sems = refs[2 * n:]
        x, y, c = _place()
        me, sibling = (x, y, c), (x, y, 1 - c)
        chips = [(1 - x, y), (x, 1 - y), (1 - x, 1 - y)]

        def slot(a, px, py, pc):
            return outs[a].at[4 * px + 2 * py + pc]

        def copy(a, k, block, to, src=None):
            return pltpu.make_async_remote_copy(
                src_ref=slot(a, *block) if src is None else src, dst_ref=slot(a, *block),
                send_sem=send_sems.at[a, k], recv_sem=recv_sems.at[a, k], device_id=to, device_id_type=MESH)

        mine, first, passed = [], [], []
        for a in range(n):
            own = pltpu.make_async_copy(ins[a], slot(a, *me), local_sems.at[a])
            own.start()
            mine.append(own)
            cps = [copy(a, 0, me, sibling, src=ins[a])]
            cps += [copy(a, 1 + j, me, (*chip, c), src=ins[a]) for j, chip in enumerate(chips)]
            for cp in cps:
                cp.start()
            first += cps
        for a in range(n):
            for j, chip in enumerate(chips):
                copy(a, 1 + j, (*chip, c), me).wait_recv()
                fwd = copy(a, 4 + j, (*chip, c), sibling)
                fwd.start()
                passed.append(fwd)
        for a in range(n):
            copy(a, 0, sibling, me).wait_recv()
            for j, chip in enumerate(chips):
                copy(a, 4 + j, (*chip, 1 - c), me).wait_recv()
        for cp in first + passed:
            cp.wait_send()
        for own in mine:
            own.wait()

    return pl.pallas_call(
        body, name=name,
        out_shape=[jax.ShapeDtypeStruct((N_DEV,) + v.shape, v.dtype) for v in shards],
        in_specs=[ANY] * n, out_specs=[ANY] * n,
        scratch_shapes=[pltpu.SemaphoreType.DMA((n, 7)), pltpu.SemaphoreType.DMA((n, 7)), pltpu.SemaphoreType.DMA((n,))],
    )(*shards)


class _Exchange:
    def __init__(self, gather, ins, outs, send_sems, recv_sems, local_sems):
        self.gather, self.ins, self.outs = gather, ins, outs
        self.sems = (send_sems, recv_sems, local_sems)
        x, y, c = _place()
        self.me = 4 * x + 2 * y + c
        self.peers = []
        for k in range(1, N_DEV):
            px = 1 - x if k & 4 else x
            py = 1 - y if k & 2 else y
            pc = 1 - c if k & 1 else c
            self.peers.append(((px, py, pc), 4 * px + 2 * py + pc))

    def _remote(self, a, k, landing):
        send_sems, recv_sems, _ = self.sems
        where, number = self.peers[k]
        src = self.ins[a] if self.gather else self.ins[a].at[number]
        return pltpu.make_async_remote_copy(
            src_ref=src, dst_ref=self.outs[a].at[landing], send_sem=send_sems.at[a, k], recv_sem=recv_sems.at[a, k],
            device_id=where, device_id_type=MESH)

    def _local(self, a):
        src = self.ins[a] if self.gather else self.ins[a].at[self.me]
        return pltpu.make_async_copy(src, self.outs[a].at[self.me], self.sems[2].at[a])

    def start(self):
        for a in range(len(self.ins)):
            self._local(a).start()
            for k in range(N_DEV - 1):
                self._remote(a, k, self.me).start()

    def finish(self):
        for a in range(len(self.ins)):
            for k in range(N_DEV - 1):
                self._remote(a, k, self.peers[k][1]).wait_recv()
            for k in range(N_DEV - 1):
                self._remote(a, k, self.me).wait_send()
            self._local(a).wait()


def _exchange_shapes(gather, arrays):
    out_shape = [jax.ShapeDtypeStruct(((N_DEV,) + v.shape) if gather else v.shape, v.dtype) for v in arrays]
    n = len(arrays)
    sems = [pltpu.SemaphoreType.DMA((n, N_DEV - 1)), pltpu.SemaphoreType.DMA((n, N_DEV - 1)), pltpu.SemaphoreType.DMA((n,))]
    return out_shape, sems


def _exchange(gathers, scatters, name):
    ng, ns = len(gathers), len(scatters)
    n = ng + ns

    def body(*refs):
        ins, outs, sems = refs[:n], refs[n:2 * n], refs[2 * n:]
        both = [_Exchange(True, ins[:ng], outs[:ng], *sems[:3]), _Exchange(False, ins[ng:], outs[ng:], *sems[3:])]
        for ex in both:
            ex.start()
        for ex in both:
            ex.finish()

    g_shapes, g_sems = _exchange_shapes(True, gathers)
    s_shapes, s_sems = _exchange_shapes(False, scatters)
    res = pl.pallas_call(body, name=name, out_shape=g_shapes + s_shapes, in_specs=[ANY] * n, out_specs=[ANY] * n,
                         scratch_shapes=g_sems + s_sems)(*gathers, *scatters)
    return res[:ng], res[ng:]


def _grad_row_tile(rows):
    return _largest_tile_rows(rows, 256)


def _largest_tile_rows(rows, cap):
    for cand in range(cap, 0, -8):
        if rows % cand == 0:
            return cand
    return rows


def _adamw_math(w, g, m, v):
    m_new = ADAM_B1 * m + (1.0 - ADAM_B1) * g
    v_new = ADAM_B2 * v + (1.0 - ADAM_B2) * (g * g)
    m_hat = m_new / (1.0 - ADAM_B1 ** ADAM_STEP)
    v_hat = v_new / (1.0 - ADAM_B2 ** ADAM_STEP)
    delta = -ADAM_LR * (m_hat / (jnp.sqrt(v_hat) + ADAM_EPS) + ADAM_WD * w)
    return delta, m_new, v_new


def _adamw(slots, w, m, v, name):
    k, r, cdim = slots.shape
    tr = _grad_row_tile(r)

    def body(s_ref, w_ref, m_ref, v_ref, g_ref, d_ref, mo_ref, vo_ref):
        g = s_ref[0].astype(F32)
        for q in range(1, k):
            g = g + s_ref[q].astype(F32)
        g_ref[...] = g
        d_ref[...], mo_ref[...], vo_ref[...] = _adamw_math(w_ref[...], g, m_ref[...], v_ref[...])

    blk = pl.BlockSpec((tr, cdim), lambda i: (i, 0))
    return pl.pallas_call(
        body, name=name, grid=(r // tr,), out_shape=[jax.ShapeDtypeStruct((r, cdim), F32)] * 4,
        in_specs=[pl.BlockSpec((k, tr, cdim), lambda i: (0, i, 0)), blk, blk, blk], out_specs=[blk] * 4,
        compiler_params=_cparams("arbitrary"),
    )(slots, w, m, v)


def _stack_cols(g):
    n, r, c = g.shape
    return g.transpose(1, 0, 2).reshape(r, n * c)


def _split_cols(w):
    r, nc = w.shape
    return w.reshape(r, N_DEV, nc // N_DEV).transpose(1, 0, 2)


def _rope_tables(positions):
    inv_freq = ROPE_THETA ** (-jnp.arange(0, MLA_ROPE, 2, dtype=F32) / MLA_ROPE)
    ang = positions.astype(F32).reshape(-1, 1) * inv_freq[None, :]
    cos, sin, zero = jnp.cos(ang), jnp.sin(ang), jnp.zeros_like(ang)
    reps = LANES // MLA_ROPE
    return (jnp.tile(jnp.concatenate([cos, cos], axis=1), (1, reps)),
            jnp.tile(jnp.concatenate([-sin, zero], axis=1), (1, reps)),
            jnp.tile(jnp.concatenate([zero, sin], axis=1), (1, reps)))


def _local_step(x, positions, loss_target, gains, g_in, g_uq, g_ukv, late_shards):
    norm_mix, q_norm, kv_norm, out_mla, out_sb, norm_ffn, norm_final = gains
    d = x.shape[1]
    w_in = _stack_cols(g_in)
    w_a = jnp.concatenate([w_in[:, :416], jnp.zeros((d, 96), BF16), w_in[:, 416:]], axis=1)
    w_uq = jnp.concatenate([g_uq[:, :, :MLA_NOPE].transpose(1, 0, 2).reshape(Q_LORA, -1),
                            g_uq[:, :, MLA_NOPE:].transpose(1, 0, 2).reshape(Q_LORA, -1)], axis=1)
    w_ukv = jnp.concatenate([g_ukv[:, :, :MLA_NOPE].transpose(1, 0, 2).reshape(KV_LORA, -1),
                             g_ukv[:, :, MLA_NOPE:].transpose(1, 0, 2).reshape(KV_LORA, -1)], axis=1)
    cos, sin_a, sin_b = _rope_tables(positions)

    u, cq, ckv, cqn, ckvn, qn, qr, kv, kr, qkv_sb = _proj_in_fwd(x, norm_mix, w_a, q_norm, w_uq, kv_norm, w_ukv, cos, sin_a, sin_b)
    o_mla, lse, (g_o, g_gate, g_up, g_down) = _mla_fwd(qn, qr, kv, kr, late_shards)
    w_o = g_o.reshape(-1, d)
    w_gate_t, w_up_t = g_gate.reshape(-1, d), g_up.reshape(-1, d)
    w_down = g_down.reshape(-1, d)
    o_sb, tot, swept = _sb_fwd(qkv_sb)
    merged, h1, f = _attn_out_fwd(o_mla, o_sb, out_mla, out_sb, w_o, x, norm_ffn)
    gate, up, h2 = _ffn_fwd(f, h1, w_gate_t, w_up_t, w_down)
    loss, dh2, dh2b, dg_final = _final_loss(h2, loss_target, norm_final.reshape(1, d))

    dgate, dup, act, dh1, dh1b, dg_ffn = _ffn_bwd(dh2, dh2b, gate, up, h1, norm_ffn, w_down, w_gate_t, w_up_t)
    dw_down = _tn_matmul(act, dh2b, "dw_down")
    dw_gate_t, dw_up_t = _tn_matmul_pair(dgate, dup, f, "dw_gate_up")
    do_mla, do_sb, dg_mla, dg_sb = _attn_out_bwd(dh1b, w_o, o_mla, o_sb, out_mla, out_sb)
    dw_o = _tn_matmul(merged, dh1b, "dw_o")
    dq_sb, dk_sb, dv_sb = _sb_bwd(qkv_sb, do_sb, tot, swept)
    early = [g.reshape(N_DEV, -1, d) for g in (dw_o, dw_gate_t, dw_up_t, dw_down)]
    (dqn, dqr, dkn, dv, dkr), landed = _mla_bwd(qn, qr, kv, kr, do_mla, o_mla, lse, early)
    landed = [landed[0], landed[1].transpose(0, 2, 1), landed[2].transpose(0, 2, 1), landed[3]]
    dx, dproj, dq, dkv, dg_q, dg_kv, dg_mix = _proj_in_bwd(
        dqn, dqr, dkn, dv, dkr, dq_sb, dk_sb, dv_sb, cq, ckv, x, dh1, cos, sin_a, sin_b,
        q_norm, kv_norm, norm_mix, w_uq, w_ukv, w_a)
    dw_a = _tn_matmul(u, dproj, "dw_in")
    dw_uq = _tn_matmul(cqn, dq, "dw_uq")
    dw_ukv = _tn_matmul(ckvn, dkv, "dw_ukv")

    p_in = _split_cols(jnp.concatenate([dw_a[:, :416], dw_a[:, 512:]], axis=1))
    p_uq = jnp.concatenate([dw_uq[:, :512].reshape(Q_LORA, MLA_HEADS, MLA_NOPE),
                            dw_uq[:, 512:].reshape(Q_LORA, MLA_HEADS, MLA_ROPE)], axis=2).transpose(1, 0, 2)
    p_ukv = jnp.concatenate([dw_ukv[:, :512].reshape(KV_LORA, MLA_HEADS, MLA_NOPE),
                             dw_ukv[:, 512:].reshape(KV_LORA, MLA_HEADS, HEAD_DIM)], axis=2).transpose(1, 0, 2)
    late = [p_in, p_uq, p_ukv]
    gain_grads = [dg_mix, dg_q, dg_kv, dg_mla, dg_sb, dg_ffn, dg_final]
    return loss, dx, list(landed), late, gain_grads


def kernel(x, positions, norm_mix, w_in, q_latent_norm, w_uq, kv_latent_norm, w_ukv, out_norm_mla, out_norm_sb, w_o, norm_ffn, w_gate, w_up, w_down, norm_final, loss_target, m_norm_mix, m_w_in, m_q_latent_norm, m_w_uq, m_kv_latent_norm, m_w_ukv, m_out_norm_mla, m_out_norm_sb, m_w_o, m_norm_ffn, m_w_gate, m_w_up, m_w_down, m_norm_final, v_norm_mix, v_w_in, v_q_latent_norm, v_w_uq, v_kv_latent_norm, v_w_ukv, v_out_norm_mla, v_out_norm_sb, v_w_o, v_norm_ffn, v_w_gate, v_w_up, v_w_down, v_norm_final):
    mats = [w_in, w_uq, w_ukv, w_o, w_gate, w_up, w_down]
    mat_m = [m_w_in, m_w_uq, m_w_ukv, m_w_o, m_w_gate, m_w_up, m_w_down]
    mat_v = [v_w_in, v_w_uq, v_w_ukv, v_w_o, v_w_gate, v_w_up, v_w_down]
    mat_names = ["w_in", "w_uq", "w_ukv", "w_o", "w_gate", "w_up", "w_down"]
    gains = [norm_mix, q_latent_norm, kv_latent_norm, out_norm_mla, out_norm_sb, norm_ffn, norm_final]
    gain_m = [m_norm_mix, m_q_latent_norm, m_kv_latent_norm, m_out_norm_mla, m_out_norm_sb, m_norm_ffn, m_norm_final]
    gain_v = [v_norm_mix, v_q_latent_norm, v_kv_latent_norm, v_out_norm_mla, v_out_norm_sb, v_norm_ffn, v_norm_final]

    shards = [w[0].astype(BF16) for w in mats]
    shards[4], shards[5] = shards[4].T, shards[5].T
    g_in, g_uq, g_ukv = _all_gather(shards[:3], "weight_all_gather")

    gains2d = [g.reshape(1, -1) for g in gains]
    loss_part, dx, landed, late, gain_grads = _local_step(
        x[0], positions[0], loss_target[0], gains2d, g_in, g_uq, g_ukv, shards[3:])

    sizes = [g.size for g in gains]
    used = sum(sizes) + LANES
    rows = -(-used // (8 * LANES)) * 8

    def pack(vals, tail):
        flat = jnp.concatenate([v.reshape(-1) for v in vals] + [tail])
        return jnp.pad(flat, (0, rows * LANES - flat.size)).reshape(rows, LANES)

    (small,), scattered = _exchange([pack(gain_grads, loss_part.reshape(-1))], late, "grad_exchange")

    mat_out = [_adamw(sl, w[0], m[0], v[0], "adamw_" + nm)
               for sl, w, m, v, nm in zip(list(scattered) + landed, mats, mat_m, mat_v, mat_names)]
    zeros_tail = jnp.zeros((LANES,), F32)
    g_s, d_s, m_s, v_s = _adamw(small, pack(gains, zeros_tail), pack(gain_m, zeros_tail), pack(gain_v, zeros_tail), "adamw_gains")

    def unpack(packed):
        flat = packed.reshape(-1)
        outs, off = [], 0
        for g, n in zip(gains, sizes):
            outs.append(flat[off:off + n].reshape(g.shape))
            off += n
        return outs

    loss = g_s.reshape(-1)[sum(sizes)]

    order = ["norm_mix", "w_in", "q_latent_norm", "w_uq", "kv_latent_norm", "w_ukv", "out_norm_mla", "out_norm_sb",
             "w_o", "norm_ffn", "w_gate", "w_up", "w_down", "norm_final"]
    gain_names = ["norm_mix", "q_latent_norm", "kv_latent_norm", "out_norm_mla", "out_norm_sb", "norm_ffn", "norm_final"]
    result = [loss, dx[None]]
    for kind in range(4):
        small_parts = dict(zip(gain_names, unpack([g_s, d_s, m_s, v_s][kind])))
        mat_parts = {nm: out[kind][None] for nm, out in zip(mat_names, mat_out)}
        result += [small_parts[nm] if nm in small_parts else mat_parts[nm] for nm in order]
    return tuple(result)
```

```python
import math

import jax
import jax.numpy as jnp
from jax import lax
from jax.experimental import pallas as pl
from jax.experimental.pallas import tpu as pltpu

F32 = jnp.float32
BF16 = jnp.bfloat16
MESH = pl.DeviceIdType.MESH

EPS = 1e-6
ROPE_THETA = 10000.0
MLA_HEADS = 8
MLA_NOPE = 64
MLA_ROPE = 32
SB_HEADS = 8
HEAD_DIM = 64
Q_LORA = 256
KV_LORA = 128
MLA_SCALE = 1.0 / math.sqrt(MLA_NOPE + MLA_ROPE)
SB_SCALE = 1.0 / math.sqrt(HEAD_DIM)
LOG2E = math.log2(math.e)
SB_DEAD = -160.0
N_DEV = 8

ADAM_LR = 0.001
ADAM_B1 = 0.9
ADAM_B2 = 0.999
ADAM_EPS = 1e-08
ADAM_WD = 0.01
ADAM_STEP = 10

LANES = 128
ATT_TILE = 512
SB_TILE = 512
TRI = 256
ROW_TILE = 512
STREAM_ROW_TILE = 1024
FFN_BWD_ROW_TILE = 512
PROJ_BWD_ROW_TILE = 512
TN_BLOCK = 256
TN_RESIDENT_BYTES = 16 * 1024 * 1024
VMEM_LIMIT = 56 * 1024 * 1024
NEG = -1e30


def _cparams(*sem):
    return pltpu.CompilerParams(dimension_semantics=sem, vmem_limit_bytes=VMEM_LIMIT)


def _dot(a, b):
    return jnp.dot(a, b, preferred_element_type=F32)


def _dot_nt(a, b):
    return lax.dot_general(a, b, (((1,), (1,)), ((), ())), preferred_element_type=F32)


def _dot_tn(a, b):
    return lax.dot_general(a, b, (((0,), (0,)), ((), ())), preferred_element_type=F32)


def _rms(x, g):
    r = lax.rsqrt(jnp.mean(x * x, axis=-1, keepdims=True) + EPS)
    return x * r * g


def _rms_bwd(x, g, dy):
    r = lax.rsqrt(jnp.mean(x * x, axis=-1, keepdims=True) + EPS)
    n = x * r
    dn = dy * g
    dx = r * (dn - n * jnp.mean(dn * n, axis=-1, keepdims=True))
    return dx, jnp.sum(dy * n, axis=0, keepdims=True)


def _rope(x, cos, sin_a, sin_b):
    return x * cos + pltpu.roll(x, 112, 1) * sin_a + pltpu.roll(x, 16, 1) * sin_b


def _rope_t(g, cos, sin_a, sin_b):
    return g * cos + pltpu.roll(g * sin_a, 16, 1) + pltpu.roll(g * sin_b, 112, 1)


def _row_spec(tm, width):
    return pl.BlockSpec((tm, width), lambda r: (r, 0))


def _full_spec(shape):
    return pl.BlockSpec(shape, lambda *_: (0,) * len(shape))


def _accumulate(ref, val, first):
    @pl.when(first)
    def _():
        ref[...] = val

    @pl.when(jnp.logical_not(first))
    def _():
        ref[...] += val


def _proj_in_fwd(x, g_mix, w_a, g_q, w_uq, g_kv, w_ukv, cos, sin_a, sin_b):
    s, d = x.shape
    tm = min(ROW_TILE, s)

    def body(x_ref, gm_ref, wa_ref, gq_ref, wuq_ref, gkv_ref, wukv_ref, cos_ref, sa_ref, sb_ref,
             u_ref, cq_ref, ckv_ref, cqn_ref, ckvn_ref, qn_ref, qr_ref, kv_ref, kr_ref, sbq_ref):
        u = _rms(x_ref[...], gm_ref[...]).astype(BF16)
        u_ref[...] = u
        cq = _dot(u, wa_ref[:, 0:256])
        ckv = _dot(u, wa_ref[:, 256:384])
        kr = _dot(u, wa_ref[:, 384:512])
        cq_ref[...] = cq
        ckv_ref[...] = ckv
        cqn = _rms(cq, gq_ref[...]).astype(BF16)
        ckvn = _rms(ckv, gkv_ref[...]).astype(BF16)
        cqn_ref[...] = cqn
        ckvn_ref[...] = ckvn
        cos_t, sa_t, sb_t = cos_ref[...], sa_ref[...], sb_ref[...]
        qn_ref[...] = (_dot(cqn, wuq_ref[:, 0:512]) * MLA_SCALE).astype(BF16)
        for half in range(2):
            lo = 512 + half * LANES
            qr = _dot(cqn, wuq_ref[:, lo:lo + LANES])
            qr_ref[:, half * LANES:(half + 1) * LANES] = (_rope(qr, cos_t, sa_t, sb_t) * MLA_SCALE).astype(BF16)
        kv_ref[...] = _dot(ckvn, wukv_ref[...]).astype(BF16)
        krt = kr + pltpu.roll(kr, 32, 1) + pltpu.roll(kr, 64, 1) + pltpu.roll(kr, 96, 1)
        kr_ref[...] = _rope(krt, cos_t, sa_t, sb_t).astype(BF16)
        sbq_ref[:, 0:512] = (_dot(u, wa_ref[:, 512:1024]) * (SB_SCALE * LOG2E)).astype(BF16)
        sbq_ref[:, 512:1536] = _dot(u, wa_ref[:, 1024:2048]).astype(BF16)

    outs = [
        jax.ShapeDtypeStruct((s, d), BF16),
        jax.ShapeDtypeStruct((s, 256), F32),
        jax.ShapeDtypeStruct((s, 128), F32),
        jax.ShapeDtypeStruct((s, 256), BF16),
        jax.ShapeDtypeStruct((s, 128), BF16),
        jax.ShapeDtypeStruct((s, 512), BF16),
        jax.ShapeDtypeStruct((s, 256), BF16),
        jax.ShapeDtypeStruct((s, 1024), BF16),
        jax.ShapeDtypeStruct((s, 128), BF16),
        jax.ShapeDtypeStruct((s, 1536), BF16),
    ]
    return pl.pallas_call(
        body, name="proj_in_fwd", grid=(s // tm,), out_shape=outs,
        in_specs=[_row_spec(tm, d), _full_spec(g_mix.shape), _full_spec(w_a.shape), _full_spec(g_q.shape),
                  _full_spec(w_uq.shape), _full_spec(g_kv.shape), _full_spec(w_ukv.shape),
                  _row_spec(tm, LANES), _row_spec(tm, LANES), _row_spec(tm, LANES)],
        out_specs=[_row_spec(tm, o.shape[1]) for o in outs],
        compiler_params=_cparams("arbitrary"),
    )(x, g_mix, w_a, g_q, w_uq, g_kv, w_ukv, cos, sin_a, sin_b)


def _attn_out_fwd(o_mla, o_sb, g_mla, g_sb, w_o, x, g_ffn):
    s, d = x.shape
    tm = min(STREAM_ROW_TILE, s)

    def body(oa_ref, ob_ref, ga_ref, gb_ref, wo_ref, x_ref, gf_ref, merged_ref, h1_ref, f_ref):
        na = _rms(oa_ref[...], ga_ref[...]).astype(BF16)
        nb = _rms(ob_ref[...], gb_ref[...]).astype(BF16)
        merged_ref[:, 0:512] = na
        merged_ref[:, 512:1024] = nb
        h1 = x_ref[...] + _dot(na, wo_ref[0:512, :]) + _dot(nb, wo_ref[512:1024, :])
        h1_ref[...] = h1
        f_ref[...] = _rms(h1, gf_ref[...]).astype(BF16)

    outs = [jax.ShapeDtypeStruct((s, d), BF16), jax.ShapeDtypeStruct((s, d), F32), jax.ShapeDtypeStruct((s, d), BF16)]
    return pl.pallas_call(
        body, name="attn_out_fwd", grid=(s // tm,), out_shape=outs,
        in_specs=[_row_spec(tm, 512), _row_spec(tm, 512), _full_spec(g_mla.shape), _full_spec(g_sb.shape),
                  _full_spec(w_o.shape), _row_spec(tm, d), _full_spec(g_ffn.shape)],
        out_specs=[_row_spec(tm, d)] * 3,
        compiler_params=_cparams("arbitrary"),
    )(o_mla, o_sb, g_mla, g_sb, w_o, x, g_ffn)


def _ffn_tile(d_ff):
    return d_ff // 2 if (d_ff // 2) % LANES == 0 else d_ff


def _ffn_fwd(f, h1, w_gate_t, w_up_t, w_down):
    s, d = h1.shape
    d_ff = w_gate_t.shape[0]
    tm = min(ROW_TILE, s)
    tf = _ffn_tile(d_ff)

    def body(f_ref, h1_ref, wgt_ref, wut_ref, wd_ref, gate_ref, up_ref, h2_ref):
        j = pl.program_id(1)
        fb = f_ref[...]
        gate = _dot_nt(fb, wgt_ref[...])
        up = _dot_nt(fb, wut_ref[...])
        gate_ref[...] = gate.astype(BF16)
        up_ref[...] = up.astype(BF16)
        act = (gate * jax.nn.sigmoid(gate) * up).astype(BF16)
        part = _dot(act, wd_ref[...])

        @pl.when(j == 0)
        def _():
            h2_ref[...] = h1_ref[...] + part

        @pl.when(j != 0)
        def _():
            h2_ref[...] += part

    outs = [jax.ShapeDtypeStruct((s, d_ff), BF16), jax.ShapeDtypeStruct((s, d_ff), BF16), jax.ShapeDtypeStruct((s, d), F32)]
    return pl.pallas_call(
        body, name="ffn_fwd", grid=(s // tm, d_ff // tf), out_shape=outs,
        in_specs=[pl.BlockSpec((tm, d), lambda r, j: (r, 0)), pl.BlockSpec((tm, d), lambda r, j: (r, 0)),
                  pl.BlockSpec((tf, d), lambda r, j: (j, 0)), pl.BlockSpec((tf, d), lambda r, j: (j, 0)),
                  pl.BlockSpec((tf, d), lambda r, j: (j, 0))],
        out_specs=[pl.BlockSpec((tm, tf), lambda r, j: (r, j)), pl.BlockSpec((tm, tf), lambda r, j: (r, j)),
                   pl.BlockSpec((tm, d), lambda r, j: (r, 0))],
        compiler_params=_cparams("arbitrary", "arbitrary"),
    )(f, h1, w_gate_t, w_up_t, w_down)


def _final_loss(h2, target, g_final):
    s, d = h2.shape
    tm = min(STREAM_ROW_TILE, s)

    def body(h2_ref, t_ref, g_ref, loss_ref, dh2_ref, dh2b_ref, dg_ref):
        first = pl.program_id(0) == 0
        h2v = h2_ref[...]
        g = g_ref[...]
        diff = _rms(h2v, g) - t_ref[...]
        part = 0.5 * jnp.sum(jnp.mean(diff * diff, axis=-1, keepdims=True), axis=0, keepdims=True)
        _accumulate(loss_ref, jnp.broadcast_to(part, loss_ref.shape), first)
        dx, dg = _rms_bwd(h2v, g, diff * (1.0 / d))
        dh2_ref[...] = dx
        dh2b_ref[...] = dx.astype(BF16)
        _accumulate(dg_ref, dg, first)

    outs = [jax.ShapeDtypeStruct((1, LANES), F32), jax.ShapeDtypeStruct((s, d), F32), jax.ShapeDtypeStruct((s, d), BF16),
            jax.ShapeDtypeStruct((1, d), F32)]
    return pl.pallas_call(
        body, name="final_loss", grid=(s // tm,), out_shape=outs,
        in_specs=[_row_spec(tm, d), _row_spec(tm, d), _full_spec((1, d))],
        out_specs=[_full_spec((1, LANES)), _row_spec(tm, d), _row_spec(tm, d), _full_spec((1, d))],
        compiler_params=_cparams("arbitrary"),
    )(h2, target, g_final)


def _tile_iotas(t):
    return lax.broadcasted_iota(jnp.int32, (t, t), 0), lax.broadcasted_iota(jnp.int32, (t, t), 1)


def _mla_fwd(qn, qr, kv, kr, shards):
    s = qn.shape[0]
    t = min(ATT_TILE, s)
    pairs = MLA_HEADS // 2
    nq = s // t
    n = len(shards)

    def body(*refs):
        qn_ref, qr_ref, kn_ref, v_ref, kr_ref = refs[:5]
        o_ref, lse_ref = refs[5 + n:7 + n]
        qcat_ref, m_ref, l_ref, acc_ref = refs[7 + 2 * n:11 + 2 * n]
        hp, i = pl.program_id(0), pl.program_id(1)
        ride = _Exchange(True, refs[5:5 + n], refs[7 + n:7 + 2 * n], *refs[11 + 2 * n:])

        @pl.when((hp == 0) & (i == 0))
        def _():
            ride.start()

        lane = lax.broadcasted_iota(jnp.int32, (1, LANES), 1)
        row, col = _tile_iotas(t)
        causal = col <= row
        q_pair, q_quad = qn_ref[...], qr_ref[...]
        zero = jnp.zeros_like(q_pair)
        for hh in range(2):
            in_head = (lane // HEAD_DIM) == hh
            in_rope = (lane // MLA_ROPE) == (hp % 2) * 2 + hh
            qcat_ref[hh * t:(hh + 1) * t, 0:LANES] = jnp.where(in_head, q_pair, zero)
            qcat_ref[hh * t:(hh + 1) * t, LANES:2 * LANES] = jnp.where(in_rope, q_quad, zero)
        m_ref[...] = jnp.full_like(m_ref, NEG)
        l_ref[...] = jnp.zeros_like(l_ref)
        acc_ref[...] = jnp.zeros_like(acc_ref)

        def tile(j, width, masked):
            rows = pl.ds(pl.multiple_of(j * t, t), width * t)
            kcat = jnp.concatenate([kn_ref[rows, :], kr_ref[rows, :]], axis=1)
            v_ones = jnp.concatenate([v_ref[rows, :], jnp.ones((width * t, LANES), BF16)], axis=1)
            scores = [_dot_nt(qcat_ref[hh * t:(hh + 1) * t, :], kcat) for hh in range(2)]
            for hh in range(2):
                half = slice(hh * t, (hh + 1) * t)
                sc = jnp.where(causal, scores[hh], NEG) if masked else scores[hh]
                m = m_ref[half, :]
                m_new = jnp.maximum(m, jnp.max(sc, axis=-1, keepdims=True))
                alpha = jnp.exp(m - m_new)
                p = jnp.exp(sc - jnp.concatenate([m_new] * (width * t // LANES), axis=1))
                pv = _dot(p.astype(BF16), v_ones)
                l_ref[half, :] = alpha * l_ref[half, :] + pv[:, LANES:]
                acc_ref[half, :] = alpha * acc_ref[half, :] + pv[:, :LANES]
                m_ref[half, :] = m_new

        tile(i, 1, True)

        def step(n, carry):
            tile(4 * n, 4, False)
            return carry

        lax.fori_loop(0, i // 4, step, 0)

        @pl.when(i % 4 >= 2)
        def _():
            tile((i // 4) * 4, 2, False)

        @pl.when(i % 2 == 1)
        def _():
            tile(i - 1, 1, False)

        first = (lane // HEAD_DIM) == 0
        o = acc_ref[...] / l_ref[...]
        lse = m_ref[...] + jnp.log(l_ref[...])
        o_ref[...] = jnp.where(first, o[0:t], o[t:2 * t])
        lse_ref[...] = jnp.where(first, lse[0:t], lse[t:2 * t])

        @pl.when((hp == pairs - 1) & (i == nq - 1))
        def _():
            ride.finish()

    gathered_shapes, sems = _exchange_shapes(True, shards)
    outs = [jax.ShapeDtypeStruct((s, 512), F32), jax.ShapeDtypeStruct((pairs, s, LANES), F32)] + gathered_shapes
    res = pl.pallas_call(
        body, name="mla_fwd", grid=(pairs, nq), out_shape=outs,
        in_specs=[pl.BlockSpec((t, LANES), lambda hp, i: (i, hp)), pl.BlockSpec((t, LANES), lambda hp, i: (i, hp // 2)),
                  pl.BlockSpec((s, LANES), lambda hp, i: (0, hp)), pl.BlockSpec((s, LANES), lambda hp, i: (0, 4 + hp)),
                  pl.BlockSpec((s, LANES), lambda hp, i: (0, 0))] + [ANY] * n,
        out_specs=[pl.BlockSpec((t, LANES), lambda hp, i: (i, hp)), pl.BlockSpec((None, t, LANES), lambda hp, i: (hp, i, 0))]
        + [ANY] * n,
        scratch_shapes=[pltpu.VMEM((2 * t, 2 * LANES), BF16), pltpu.VMEM((2 * t, LANES), F32), pltpu.VMEM((2 * t, LANES), F32),
                        pltpu.VMEM((2 * t, LANES), F32)] + sems,
        compiler_params=_cparams("arbitrary", "arbitrary"),
    )(qn, qr, kv, kv, kr, *shards)
    return res[0], res[1], res[2:]


HEADS = (0, 1)


def _sb_logs(z2, strict, masked):
    log_b = jnp.minimum(z2, 0.0) - jnp.log2(1.0 + jnp.exp2(-jnp.abs(z2)))
    log_1m = log_b - z2
    if masked:
        log_1m = jnp.where(strict, log_1m, 0.0)
    return log_1m, log_b


def _block_totals(x):
    t, w = x.shape
    nb = max(w // TRI, 1)
    bw = w // nb
    blocks = [x[:, b * bw:(b + 1) * bw] for b in range(nb)]
    totals = [jnp.broadcast_to(jnp.sum(blk, axis=-1, keepdims=True), (t, LANES)) for blk in blocks]
    whole = totals[0]
    for tot in totals[1:]:
        whole = whole + tot
    return blocks, totals, whole


def _running_sums(blocks, totals, tri, carry, suffix):
    nb = len(blocks)
    reps = blocks[0].shape[1] // LANES
    outs = [None] * nb
    run = carry
    for b in (range(nb - 1, -1, -1) if suffix else range(nb)):
        outs[b] = _dot(blocks[b].astype(BF16), tri) + jnp.concatenate([run] * reps, axis=1)
        run = run + totals[b]
    return outs[0] if nb == 1 else jnp.concatenate(outs, axis=1)


def _tri(t, rel):
    n = min(TRI, t)
    row, col = _tile_iotas(n)
    return rel(row, col).astype(BF16)


def _sweep_width(t):
    return t // 2 if t // 2 >= TRI else t


def _sb_fwd(qkv):
    s = qkv.shape[0]
    t = min(SB_TILE, s)
    sw = _sweep_width(t)
    pairs = SB_HEADS // 2

    def body(q_ref, k_ref, v_ref, o_ref, tot_ref, cnt_ref, qm_ref, right_ref, acc_ref):
        i = pl.program_id(1)
        lane = lax.broadcasted_iota(jnp.int32, (1, LANES), 1)
        row, col = _tile_iotas(t)
        strict = col < row
        t_suffix = _tri(t, lambda r, c: r > c)
        q_pair = q_ref[...]
        for hh in range(2):
            qm_ref[hh] = jnp.where((lane // HEAD_DIM) == hh, q_pair, jnp.zeros_like(q_pair))
        right_ref[...] = jnp.zeros_like(right_ref)
        acc_ref[...] = jnp.zeros_like(acc_ref)

        def tile(start, width, masked):
            rows = pl.ds(pl.multiple_of(start, width), width)
            k, v = k_ref[rows, :], v_ref[rows, :]
            for hh in HEADS:
                log_1m, log_b = _sb_logs(_dot_nt(qm_ref[hh], k), strict, masked)
                blocks, totals, whole = _block_totals(log_1m)
                a = jnp.exp2(log_b + _running_sums(blocks, totals, t_suffix, right_ref[hh], True))
                if masked:
                    a = jnp.where(strict, a, 0.0)
                right_ref[hh] += whole
                acc_ref[hh] += _dot(a.astype(BF16), v)

        tile(i * t, t, True)

        def alive(n):
            return (n < i * (t // sw)) & (jnp.max(right_ref[...]) > SB_DEAD)

        def step(n):
            tile((i * (t // sw) - 1 - n) * sw, sw, False)
            return n + 1

        swept = lax.while_loop(alive, step, jnp.int32(0))
        cnt_ref[...] = jnp.full(cnt_ref.shape, swept.astype(F32))
        first = (lane // HEAD_DIM) == 0
        o_ref[...] = jnp.where(first, acc_ref[0], acc_ref[1])
        tot_ref[...] = jnp.where(first, right_ref[0], right_ref[1])

    outs = [jax.ShapeDtypeStruct((s, 512), F32), jax.ShapeDtypeStruct((pairs, s, LANES), F32),
            jax.ShapeDtypeStruct((pairs, s // t, 8, LANES), F32)]
    return pl.pallas_call(
        body, name="sb_fwd", grid=(pairs, s // t), out_shape=outs,
        in_specs=[pl.BlockSpec((t, LANES), lambda hp, i: (i, hp)), pl.BlockSpec((s, LANES), lambda hp, i: (0, 4 + hp)),
                  pl.BlockSpec((s, LANES), lambda hp, i: (0, 8 + hp))],
        out_specs=[pl.BlockSpec((t, LANES), lambda hp, i: (i, hp)), pl.BlockSpec((None, t, LANES), lambda hp, i: (hp, i, 0)),
                   pl.BlockSpec((None, None, 8, LANES), lambda hp, i: (hp, i, 0, 0))],
        scratch_shapes=[pltpu.VMEM((2, t, LANES), BF16), pltpu.VMEM((2, t, LANES), F32), pltpu.VMEM((2, t, LANES), F32)],
        compiler_params=_cparams("arbitrary", "arbitrary"),
    )(qkv, qkv, qkv)


def _sb_bwd(qkv, do, tot, cnt):
    s = qkv.shape[0]
    t = min(SB_TILE, s)
    sw = _sweep_width(t)
    pairs = SB_HEADS // 2

    def body(q_ref, k_ref, v_ref, do_ref, tot_ref, cnt_ref, dq_ref, dk_ref, dv_ref,
             qm_ref, dob_ref, total_s, left_l, left_g, dq_s, dk_s, dv_s):
        i = pl.program_id(1)

        @pl.when(i == 0)
        def _():
            dk_s[...] = jnp.zeros_like(dk_s)
            dv_s[...] = jnp.zeros_like(dv_s)

        lane = lax.broadcasted_iota(jnp.int32, (1, LANES), 1)
        row, col = _tile_iotas(t)
        strict = col < row
        t_suffix = _tri(t, lambda r, c: r > c)
        t_excl = _tri(t, lambda r, c: r < c)
        q_pair, do_pair, tot_pair = q_ref[...], do_ref[...], tot_ref[...]
        for hh in range(2):
            in_head = (lane // HEAD_DIM) == hh
            qm_ref[hh] = jnp.where(in_head, q_pair, jnp.zeros_like(q_pair))
            dob_ref[hh] = jnp.where(in_head, do_pair, 0.0).astype(BF16)
            total_s[hh] = jnp.broadcast_to(
                jnp.sum(jnp.where(lane == hh * HEAD_DIM, tot_pair, 0.0), axis=-1, keepdims=True), (t, LANES))
        left_l[...] = jnp.zeros_like(left_l)
        left_g[...] = jnp.zeros_like(left_g)
        dq_s[...] = jnp.zeros_like(dq_s)

        def tile(start, width, masked):
            rows = pl.ds(pl.multiple_of(start, width), width)
            k, v = k_ref[rows, :], v_ref[rows, :]
            z2 = [_dot_nt(qm_ref[hh], k) for hh in HEADS]
            d_a = [_dot_nt(dob_ref[hh], v) for hh in HEADS]
            for hh in HEADS:
                qm, dob = qm_ref[hh], dob_ref[hh]
                log_1m, log_b = _sb_logs(z2[hh], strict, masked)
                blocks, totals, whole = _block_totals(log_1m)
                done = left_l[hh] + whole
                left_l[hh] = done
                a = jnp.exp2(log_b + _running_sums(blocks, totals, t_suffix, total_s[hh] - done, True))
                if masked:
                    a = jnp.where(strict, a, 0.0)
                g = a * d_a[hh]
                blocks, totals, whole = _block_totals(g)
                before = _running_sums(blocks, totals, t_excl, left_g[hh], False)
                left_g[hh] += whole
                dz = g - jnp.exp2(log_b) * (g + before)
                if masked:
                    dz = jnp.where(strict, dz, 0.0)
                dzb = dz.astype(BF16)
                dq_s[hh] += _dot(dzb, k)
                dk_s[rows, :] += _dot_tn(dzb, qm)
                dv_s[rows, :] += _dot_tn(a.astype(BF16), dob)

        def step(h, carry):
            tile(h * sw, sw, False)
            return carry

        swept = jnp.max(cnt_ref[...]).astype(jnp.int32)
        lax.fori_loop(i * (t // sw) - swept, i * (t // sw), step, 0)
        tile(i * t, t, True)
        dq_ref[...] = (jnp.where((lane // HEAD_DIM) == 0, dq_s[0], dq_s[1]) * SB_SCALE).astype(BF16)

        @pl.when(i == s // t - 1)
        def _():
            dk_ref[...] = (dk_s[...] * (1.0 / LOG2E)).astype(BF16)
            dv_ref[...] = dv_s[...].astype(BF16)

    outs = [jax.ShapeDtypeStruct((s, 512), BF16)] * 3
    return pl.pallas_call(
        body, name="sb_bwd", grid=(pairs, s // t), out_shape=outs,
        in_specs=[pl.BlockSpec((t, LANES), lambda hp, i: (i, hp)), pl.BlockSpec((s, LANES), lambda hp, i: (0, 4 + hp)),
                  pl.BlockSpec((s, LANES), lambda hp, i: (0, 8 + hp)), pl.BlockSpec((t, LANES), lambda hp, i: (i, hp)),
                  pl.BlockSpec((None, t, LANES), lambda hp, i: (hp, i, 0)),
                  pl.BlockSpec((None, None, 8, LANES), lambda hp, i: (hp, i, 0, 0))],
        out_specs=[pl.BlockSpec((t, LANES), lambda hp, i: (i, hp)), pl.BlockSpec((s, LANES), lambda hp, i: (0, hp)),
                   pl.BlockSpec((s, LANES), lambda hp, i: (0, hp))],
        scratch_shapes=[pltpu.VMEM((2, t, LANES), BF16), pltpu.VMEM((2, t, LANES), BF16)]
        + [pltpu.VMEM((2, t, LANES), F32)] * 4 + [pltpu.VMEM((s, LANES), F32)] * 2,
        compiler_params=_cparams("arbitrary", "arbitrary"),
    )(qkv, qkv, qkv, do, tot, cnt)


def _mla_bwd(qn, qr, kv, kr, do, o, lse, parts):
    s = qn.shape[0]
    t = min(ATT_TILE, s)
    pairs = MLA_HEADS // 2
    nq = s // t
    n = len(parts)

    def body(*refs):
        qn_ref, qr_ref, kn_ref, v_ref, kr_ref, do_ref, o_ref, lse_ref = refs[:8]
        dqn_ref, dqr_ref, dkn_ref, dv_ref, dkr_ref = refs[8 + n:13 + n]
        qcat_ref, dob_ref, lse_s, delta_s, dq_s, dkn_s, dv_s, dkr_s = refs[13 + 2 * n:21 + 2 * n]
        hp, i = pl.program_id(0), pl.program_id(1)
        ride = _Exchange(False, refs[8:8 + n], refs[13 + n:13 + 2 * n], *refs[21 + 2 * n:])

        @pl.when((hp == 0) & (i == 0))
        def _():
            ride.start()

        @pl.when(i == 0)
        def _():
            dkn_s[...] = jnp.zeros_like(dkn_s)
            dv_s[...] = jnp.zeros_like(dv_s)
            dkr_s[...] = jnp.zeros_like(dkr_s)

        lane = lax.broadcasted_iota(jnp.int32, (1, LANES), 1)
        row, col = _tile_iotas(t)
        causal = col <= row
        q_pair, q_quad, do_pair, lse_pair = qn_ref[...], qr_ref[...], do_ref[...], lse_ref[...]
        do_o = do_pair * o_ref[...]
        zero = jnp.zeros_like(q_pair)
        ropes = []
        for hh in range(2):
            in_head = (lane // HEAD_DIM) == hh
            in_rope = (lane // MLA_ROPE) == (hp % 2) * 2 + hh
            ropes.append(in_rope)
            qcat_ref[hh, :, 0:LANES] = jnp.where(in_head, q_pair, zero)
            qcat_ref[hh, :, LANES:2 * LANES] = jnp.where(in_rope, q_quad, zero)
            dob_ref[hh] = jnp.where(in_head, do_pair, 0.0).astype(BF16)
            delta_s[hh] = jnp.broadcast_to(jnp.sum(jnp.where(in_head, do_o, 0.0), axis=-1, keepdims=True), (t, LANES))
            lse_s[hh] = jnp.broadcast_to(
                jnp.sum(jnp.where(lane == hh * HEAD_DIM, lse_pair, 0.0), axis=-1, keepdims=True), (t, LANES))
        dq_s[...] = jnp.zeros_like(dq_s)
        reps = t // LANES

        def tile(j, width, masked):
            rows = pl.ds(pl.multiple_of(j * t, t), width * t)
            kcat = jnp.concatenate([kn_ref[rows, :], kr_ref[rows, :]], axis=1)
            v = v_ref[rows, :]
            sc = [_dot_nt(qcat_ref[hh], kcat) for hh in HEADS]
            dp = [_dot_nt(dob_ref[hh], v) for hh in HEADS]
            p = [jnp.exp(sc[hh] - jnp.concatenate([lse_s[hh]] * (width * reps), axis=1)) for hh in HEADS]
            if masked:
                p = [jnp.where(causal, p[hh], 0.0) for hh in HEADS]
            ds = [(p[hh] * (dp[hh] - jnp.concatenate([delta_s[hh]] * (width * reps), axis=1))).astype(BF16) for hh in HEADS]
            for hh in HEADS:
                dq_s[hh] += _dot(ds[hh], kcat)
            dkcat = _dot_tn(ds[0], qcat_ref[0]) + _dot_tn(ds[1], qcat_ref[1])
            dkn_s[rows, :] += dkcat[:, 0:LANES]
            dkr_s[rows, :] += dkcat[:, LANES:2 * LANES]
            dv_s[rows, :] += _dot_tn(p[0].astype(BF16), dob_ref[0]) + _dot_tn(p[1].astype(BF16), dob_ref[1])

        def step(n, carry):
            tile(4 * n, 4, False)
            return carry

        lax.fori_loop(0, i // 4, step, 0)

        @pl.when(i % 4 >= 2)
        def _():
            tile((i // 4) * 4, 2, False)

        @pl.when(i % 2 == 1)
        def _():
            tile(i - 1, 1, False)

        tile(i, 1, True)
        dqn_ref[...] = (jnp.where((lane // HEAD_DIM) == 0, dq_s[0, :, 0:LANES], dq_s[1, :, 0:LANES]) * MLA_SCALE).astype(BF16)
        dqr_ref[...] = ((jnp.where(ropes[0], dq_s[0, :, LANES:2 * LANES], 0.0)
                         + jnp.where(ropes[1], dq_s[1, :, LANES:2 * LANES], 0.0)) * MLA_SCALE).astype(BF16)

        @pl.when(i == nq - 1)
        def _():
            dkn_ref[...] = dkn_s[...].astype(BF16)
            dv_ref[...] = dv_s[...].astype(BF16)
            dkr_ref[...] = dkr_s[...].astype(BF16)

        @pl.when((hp == pairs - 1) & (i == nq - 1))
        def _():
            ride.finish()

    pair_block = pl.BlockSpec((t, LANES), lambda hp, i: (i, hp))
    once = pl.Buffered(1)
    landed_shapes, sems = _exchange_shapes(False, parts)
    outs = [jax.ShapeDtypeStruct((s, 512), BF16), jax.ShapeDtypeStruct((pairs, s, LANES), BF16),
            jax.ShapeDtypeStruct((s, 512), BF16), jax.ShapeDtypeStruct((s, 512), BF16),
            jax.ShapeDtypeStruct((pairs, s, LANES), BF16)] + landed_shapes
    res = pl.pallas_call(
        body, name="mla_bwd", grid=(pairs, nq), out_shape=outs,
        in_specs=[pair_block, pl.BlockSpec((t, LANES), lambda hp, i: (i, hp // 2)),
                  pl.BlockSpec((s, LANES), lambda hp, i: (0, hp), pipeline_mode=once),
                  pl.BlockSpec((s, LANES), lambda hp, i: (0, 4 + hp), pipeline_mode=once),
                  pl.BlockSpec((s, LANES), lambda hp, i: (0, 0), pipeline_mode=once), pair_block, pair_block,
                  pl.BlockSpec((None, t, LANES), lambda hp, i: (hp, i, 0))] + [ANY] * n,
        out_specs=[pair_block, pl.BlockSpec((None, t, LANES), lambda hp, i: (hp, i, 0)),
                   pl.BlockSpec((s, LANES), lambda hp, i: (0, hp), pipeline_mode=once),
                   pl.BlockSpec((s, LANES), lambda hp, i: (0, hp), pipeline_mode=once),
                   pl.BlockSpec((None, s, LANES), lambda hp, i: (hp, 0, 0), pipeline_mode=once)] + [ANY] * n,
        scratch_shapes=[pltpu.VMEM((2, t, 2 * LANES), BF16), pltpu.VMEM((2, t, LANES), BF16), pltpu.VMEM((2, t, LANES), F32),
                        pltpu.VMEM((2, t, LANES), F32), pltpu.VMEM((2, t, 2 * LANES), F32)]
        + [pltpu.VMEM((s, LANES), F32)] * 3 + sems,
        compiler_params=_cparams("arbitrary", "arbitrary"),
    )(qn, qr, kv, kv, kr, do, o, lse, *parts)
    return res[:5], res[5:]


def _ffn_bwd(dh2, dh2b, gate, up, h1, g_ffn, w_down, w_gate_t, w_up_t):
    s, d = h1.shape
    d_ff = gate.shape[1]
    tm = min(FFN_BWD_ROW_TILE, s)
    tf = _ffn_tile(d_ff)

    def act_body(dh2b_ref, gate_ref, up_ref, wd_ref, dgate_ref, dup_ref, act_ref):
        dact = _dot_nt(dh2b_ref[...], wd_ref[...])
        gate_v = gate_ref[...].astype(F32)
        up_v = up_ref[...].astype(F32)
        sig = jax.nn.sigmoid(gate_v)
        silu = gate_v * sig
        dup_ref[...] = (dact * silu).astype(BF16)
        dgate_ref[...] = ((dact * up_v) * (sig * (1.0 + gate_v - silu))).astype(BF16)
        act_ref[...] = (silu * up_v).astype(BF16)

    ta = min(STREAM_ROW_TILE, s)
    ff = pl.BlockSpec((ta, tf), lambda j, r: (r, j))
    dgate, dup, act = pl.pallas_call(
        act_body, name="ffn_bwd_act", grid=(d_ff // tf, s // ta), out_shape=[jax.ShapeDtypeStruct((s, d_ff), BF16)] * 3,
        in_specs=[pl.BlockSpec((ta, d), lambda j, r: (r, 0)), ff, ff, pl.BlockSpec((tf, d), lambda j, r: (j, 0))],
        out_specs=[ff, ff, ff],
        compiler_params=_cparams("arbitrary", "arbitrary"),
    )(dh2b, gate, up, w_down)

    def df_body(dgate_ref, dup_ref, dh2_ref, h1_ref, g_ref, wgt_ref, wut_ref, dh1_ref, dh1b_ref, dg_ref):
        df = _dot(dgate_ref[...], wgt_ref[...]) + _dot(dup_ref[...], wut_ref[...])
        dx, dg = _rms_bwd(h1_ref[...], g_ref[...], df)
        dh1 = dh2_ref[...] + dx
        dh1_ref[...] = dh1
        dh1b_ref[...] = dh1.astype(BF16)
        _accumulate(dg_ref, dg, pl.program_id(0) == 0)

    outs = [jax.ShapeDtypeStruct((s, d), F32), jax.ShapeDtypeStruct((s, d), BF16), jax.ShapeDtypeStruct((1, d), F32)]
    dh1, dh1b, dg = pl.pallas_call(
        df_body, name="ffn_bwd_df", grid=(s // tm,), out_shape=outs,
        in_specs=[_row_spec(tm, d_ff), _row_spec(tm, d_ff), _row_spec(tm, d), _row_spec(tm, d), _full_spec((1, d)),
                  pl.BlockSpec(w_gate_t.shape, lambda r: (0, 0), pipeline_mode=pl.Buffered(1)),
                  pl.BlockSpec(w_up_t.shape, lambda r: (0, 0), pipeline_mode=pl.Buffered(1))],
        out_specs=[_row_spec(tm, d), _row_spec(tm, d), _full_spec((1, d))],
        compiler_params=_cparams("arbitrary"),
    )(dgate, dup, dh2, h1, g_ffn, w_gate_t, w_up_t)
    return dgate, dup, act, dh1, dh1b, dg


def _tn_matmul(a, b, name):
    assert a.dtype == BF16 and b.dtype == BF16
    s, m = a.shape
    n = b.shape[1]
    def stream(width):
        return 2 * TN_BLOCK if width % (2 * TN_BLOCK) == 0 else min(width, TN_BLOCK)

    once, twice = pl.Buffered(1), pl.Buffered(2)
    if s * m * 2 <= TN_RESIDENT_BYTES:
        tm, tn, modes = m, stream(n), (once, twice)
    else:
        tm, tn, modes = stream(m), n, (twice, once)

    def body(a_ref, b_ref, o_ref):
        o_ref[...] = _dot_tn(a_ref[...], b_ref[...]).astype(BF16)

    return pl.pallas_call(
        body, name=name, grid=(m // tm, n // tn), out_shape=jax.ShapeDtypeStruct((m, n), BF16),
        in_specs=[pl.BlockSpec((s, tm), lambda i, j: (0, i), pipeline_mode=modes[0]),
                  pl.BlockSpec((s, tn), lambda i, j: (0, j), pipeline_mode=modes[1])],
        out_specs=pl.BlockSpec((tm, tn), lambda i, j: (i, j)),
        compiler_params=_cparams("arbitrary", "arbitrary"),
    )(a, b)


def _tn_matmul_pair(a1, a2, b, name):
    assert a1.shape == a2.shape and a1.dtype == a2.dtype == b.dtype == BF16
    s, m = a1.shape
    n = b.shape[1]

    def body(a1_ref, a2_ref, b_ref, o1_ref, o2_ref):
        bv = b_ref[...]
        o1_ref[...] = _dot_tn(a1_ref[...], bv).astype(BF16)
        o2_ref[...] = _dot_tn(a2_ref[...], bv).astype(BF16)

    a_spec = pl.BlockSpec((s, TN_BLOCK), lambda i: (0, i))
    o_spec = pl.BlockSpec((TN_BLOCK, n), lambda i: (i, 0))
    return pl.pallas_call(
        body, name=name, grid=(m // TN_BLOCK,), out_shape=[jax.ShapeDtypeStruct((m, n), BF16)] * 2,
        in_specs=[a_spec, a_spec, pl.BlockSpec((s, n), lambda i: (0, 0), pipeline_mode=pl.Buffered(1))],
        out_specs=[o_spec, o_spec],
        compiler_params=_cparams("arbitrary"),
    )(a1, a2, b)


def _attn_out_bwd(dh1, w_o, o_mla, o_sb, g_mla, g_sb):
    s, d = dh1.shape
    tm = min(STREAM_ROW_TILE, s)

    def body(dh1_ref, wo_ref, oa_ref, ob_ref, ga_ref, gb_ref, doa_ref, dob_ref, dga_ref, dgb_ref):
        first = pl.program_id(0) == 0
        dh1b = dh1_ref[...]
        dxa, dga = _rms_bwd(oa_ref[...], ga_ref[...], _dot_nt(dh1b, wo_ref[0:512, :]))
        dxb, dgb = _rms_bwd(ob_ref[...], gb_ref[...], _dot_nt(dh1b, wo_ref[512:1024, :]))
        doa_ref[...] = dxa
        dob_ref[...] = dxb
        _accumulate(dga_ref, dga, first)
        _accumulate(dgb_ref, dgb, first)

    outs = [jax.ShapeDtypeStruct((s, 512), F32)] * 2 + [jax.ShapeDtypeStruct((1, 512), F32)] * 2
    return pl.pallas_call(
        body, name="attn_out_bwd", grid=(s // tm,), out_shape=outs,
        in_specs=[_row_spec(tm, d), _full_spec(w_o.shape), _row_spec(tm, 512), _row_spec(tm, 512),
                  _full_spec((1, 512)), _full_spec((1, 512))],
        out_specs=[_row_spec(tm, 512), _row_spec(tm, 512), _full_spec((1, 512)), _full_spec((1, 512))],
        compiler_params=_cparams("arbitrary"),
    )(dh1, w_o, o_mla, o_sb, g_mla, g_sb)


def _proj_in_bwd(dqn, dqr, dkn, dv, dkr, dq_sb, dk_sb, dv_sb, cq, ckv, x, dh1, cos, sin_a, sin_b,
                 g_q, g_kv, g_mix, w_uq, w_ukv, w_a):
    s, d = x.shape
    tm = min(PROJ_BWD_ROW_TILE, s)

    def body(dqn_ref, dqr_ref, dkn_ref, dv_ref, dkr_ref, dqs_ref, dks_ref, dvs_ref, cq_ref, ckv_ref, x_ref, dh1_ref,
             cos_ref, sa_ref, sb_ref, gq_ref, gkv_ref, gm_ref, wuq_ref, wukv_ref, wa_ref,
             dx_ref, dproj_ref, dq_ref, dkv_ref, dgq_ref, dgkv_ref, dgm_ref):
        first = pl.program_id(0) == 0
        lane = lax.broadcasted_iota(jnp.int32, (1, LANES), 1)
        cos_t, sa_t, sb_t = cos_ref[...], sa_ref[...], sb_ref[...]
        dq_ref[:, 0:512] = dqn_ref[...]
        for half in range(2):
            quad = dqr_ref[2 * half].astype(F32) + dqr_ref[2 * half + 1].astype(F32)
            dq_ref[:, 512 + half * LANES:512 + (half + 1) * LANES] = _rope_t(quad, cos_t, sa_t, sb_t).astype(BF16)
        dcq, dgq = _rms_bwd(cq_ref[...], gq_ref[...], _dot_nt(dq_ref[...], wuq_ref[...]))
        _accumulate(dgq_ref, dgq, first)
        dkv_ref[:, 0:512] = dkn_ref[...]
        dkv_ref[:, 512:1024] = dv_ref[...]
        dckv, dgkv = _rms_bwd(ckv_ref[...], gkv_ref[...], _dot_nt(dkv_ref[...], wukv_ref[...]))
        _accumulate(dgkv_ref, dgkv, first)
        pairs_sum = (dkr_ref[0].astype(F32) + dkr_ref[1].astype(F32)) + (dkr_ref[2].astype(F32) + dkr_ref[3].astype(F32))
        g = _rope_t(pairs_sum, cos_t, sa_t, sb_t)
        g = g + pltpu.roll(g, 96, 1) + pltpu.roll(g, 64, 1) + pltpu.roll(g, 32, 1)
        dproj_ref[:, 0:256] = dcq.astype(BF16)
        dproj_ref[:, 256:384] = dckv.astype(BF16)
        dproj_ref[:, 384:512] = jnp.where(lane < MLA_ROPE, g, 0.0).astype(BF16)
        dproj_ref[:, 512:1024] = dqs_ref[...]
        dproj_ref[:, 1024:1536] = dks_ref[...]
        dproj_ref[:, 1536:2048] = dvs_ref[...]
        dxn, dgm = _rms_bwd(x_ref[...], gm_ref[...], _dot_nt(dproj_ref[...], wa_ref[...]))
        dx_ref[...] = dh1_ref[...] + dxn
        _accumulate(dgm_ref, dgm, first)

    quad_spec = pl.BlockSpec((4, tm, LANES), lambda r: (0, r, 0))
    outs = [jax.ShapeDtypeStruct((s, d), F32), jax.ShapeDtypeStruct((s, 2048), BF16), jax.ShapeDtypeStruct((s, 768), BF16),
            jax.ShapeDtypeStruct((s, 1024), BF16), jax.ShapeDtypeStruct((1, 256), F32), jax.ShapeDtypeStruct((1, 128), F32),
            jax.ShapeDtypeStruct((1, d), F32)]
    return pl.pallas_call(
        body, name="proj_in_bwd", grid=(s // tm,), out_shape=outs,
        in_specs=[_row_spec(tm, 512), quad_spec, _row_spec(tm, 512), _row_spec(tm, 512), quad_spec,
                  _row_spec(tm, 512), _row_spec(tm, 512), _row_spec(tm, 512), _row_spec(tm, 256), _row_spec(tm, 128),
                  _row_spec(tm, d), _row_spec(tm, d), _row_spec(tm, LANES), _row_spec(tm, LANES), _row_spec(tm, LANES),
                  _full_spec((1, 256)), _full_spec((1, 128)), _full_spec((1, d)),
                  _full_spec(w_uq.shape), _full_spec(w_ukv.shape), _full_spec(w_a.shape)],
        out_specs=[_row_spec(tm, d), _row_spec(tm, 2048), _row_spec(tm, 768), _row_spec(tm, 1024),
                   _full_spec((1, 256)), _full_spec((1, 128)), _full_spec((1, d))],
        compiler_params=_cparams("arbitrary"),
    )(dqn, dqr, dkn, dv, dkr, dq_sb, dk_sb, dv_sb, cq, ckv, x, dh1, cos, sin_a, sin_b, g_q, g_kv, g_mix,
      w_uq, w_ukv, w_a)


ANY = pl.BlockSpec(memory_space=pl.ANY)


def _place():
    return lax.axis_index("x"), lax.axis_index("y"), lax.axis_index("c")


def _all_gather(shards, name):
    n = len(shards)

    def body(*refs):
        ins, outs = refs[:n], refs[n:2 * n]
        send_sems, recv_sems, local_sems = refs[2 * n:]
        x, y, c = _place()
        me, sibling = (x, y, c), (x, y, 1 - c)
        chips = [(1 - x, y), (x, 1 - y), (1 - x, 1 - y)]

        def slot(a, px, py, pc):
            return outs[a].at[4 * px + 2 * py + pc]

        def copy(a, k, block, to, src=None):
            return pltpu.make_async_remote_copy(
                src_ref=slot(a, *block) if src is None else src, dst_ref=slot(a, *block),
                send_sem=send_sems.at[a, k], recv_sem=recv_sems.at[a, k], device_id=to, device_id_type=MESH)

        mine, first, passed = [], [], []
        for a in range(n):
            own = pltpu.make_async_copy(ins[a], slot(a, *me), local_sems.at[a])
            own.start()
            mine.append(own)
            cps = [copy(a, 0, me, sibling, src=ins[a])]
            cps += [copy(a, 1 + j, me, (*chip, c), src=ins[a]) for j, chip in enumerate(chips)]
            for cp in cps:
                cp.start()
            first += cps
        for a in range(n):
            for j, chip in enumerate(chips):
                copy(a, 1 + j, (*chip, c), me).wait_recv()
                fwd = copy(a, 4 + j, (*chip, c), sibling)
                fwd.start()
                passed.append(fwd)
        for a in range(n):
            copy(a, 0, sibling, me).wait_recv()
            for j, chip in enumerate(chips):
                copy(a, 4 + j, (*chip, 1 - c), me).wait_recv()
        for cp in first + passed:
            cp.wait_send()
        for own in mine:
            own.wait()

    return pl.pallas_call(
        body, name=name,
        out_shape=[jax.ShapeDtypeStruct((N_DEV,) + v.shape, v.dtype) for v in shards],
        in_specs=[ANY] * n, out_specs=[ANY] * n,
        scratch_shapes=[pltpu.SemaphoreType.DMA((n, 7)), pltpu.SemaphoreType.DMA((n, 7)), pltpu.SemaphoreType.DMA((n,))],
    )(*shards)


class _Exchange:
    def __init__(self, gather, ins, outs, send_sems, recv_sems, local_sems):
        self.gather, self.ins, self.outs = gather, ins, outs
        self.sems = (send_sems, recv_sems, local_sems)
        x, y, c = _place()
        self.me = 4 * x + 2 * y + c
        self.peers = []
        for k in range(1, N_DEV):
            px = 1 - x if k & 4 else x
            py = 1 - y if k & 2 else y
            pc = 1 - c if k & 1 else c
            self.peers.append(((px, py, pc), 4 * px + 2 * py + pc))

    def _remote(self, a, k, landing):
        send_sems, recv_sems, _ = self.sems
        where, number = self.peers[k]
        src = self.ins[a] if self.gather else self.ins[a].at[number]
        return pltpu.make_async_remote_copy(
            src_ref=src, dst_ref=self.outs[a].at[landing], send_sem=send_sems.at[a, k], recv_sem=recv_sems.at[a, k],
            device_id=where, device_id_type=MESH)

    def _local(self, a):
        src = self.ins[a] if self.gather else self.ins[a].at[self.me]
        return pltpu.make_async_copy(src, self.outs[a].at[self.me], self.sems[2].at[a])

    def start(self):
        for a in range(len(self.ins)):
            self._local(a).start()
            for k in range(N_DEV - 1):
                self._remote(a, k, self.me).start()

    def finish(self):
        for a in range(len(self.ins)):
            for k in range(N_DEV - 1):
                self._remote(a, k, self.peers[k][1]).wait_recv()
            for k in range(N_DEV - 1):
                self._remote(a, k, self.me).wait_send()
            self._local(a).wait()


def _exchange_shapes(gather, arrays):
    out_shape = [jax.ShapeDtypeStruct(((N_DEV,) + v.shape) if gather else v.shape, v.dtype) for v in arrays]
    n = len(arrays)
    sems = [pltpu.SemaphoreType.DMA((n, N_DEV - 1)), pltpu.SemaphoreType.DMA((n, N_DEV - 1)), pltpu.SemaphoreType.DMA((n,))]
    return out_shape, sems


def _exchange(gathers, scatters, name):
    ng, ns = len(gathers), len(scatters)
    n = ng + ns

    def body(*refs):
        ins, outs, sems = refs[:n], refs[n:2 * n], refs[2 * n:]
        both = [_Exchange(True, ins[:ng], outs[:ng], *sems[:3]), _Exchange(False, ins[ng:], outs[ng:], *sems[3:])]
        for ex in both:
            ex.start()
        for ex in both:
            ex.finish()

    g_shapes, g_sems = _exchange_shapes(True, gathers)
    s_shapes, s_sems = _exchange_shapes(False, scatters)
    res = pl.pallas_call(body, name=name, out_shape=g_shapes + s_shapes, in_specs=[ANY] * n, out_specs=[ANY] * n,
                         scratch_shapes=g_sems + s_sems)(*gathers, *scatters)
    return res[:ng], res[ng:]


def _grad_row_tile(rows):
    return _largest_tile_rows(rows, 256)


def _largest_tile_rows(rows, cap):
    for cand in range(cap, 0, -8):
        if rows % cand == 0:
            return cand
    return rows


def _adamw_math(w, g, m, v):
    m_new = ADAM_B1 * m + (1.0 - ADAM_B1) * g
    v_new = ADAM_B2 * v + (1.0 - ADAM_B2) * (g * g)
    m_hat = m_new / (1.0 - ADAM_B1 ** ADAM_STEP)
    v_hat = v_new / (1.0 - ADAM_B2 ** ADAM_STEP)
    delta = -ADAM_LR * (m_hat / (jnp.sqrt(v_hat) + ADAM_EPS) + ADAM_WD * w)
    return delta, m_new, v_new


def _adamw(slots, w, m, v, name):
    k, r, cdim = slots.shape
    tr = _grad_row_tile(r)

    def body(s_ref, w_ref, m_ref, v_ref, g_ref, d_ref, mo_ref, vo_ref):
        g = s_ref[0].astype(F32)
        for q in range(1, k):
            g = g + s_ref[q].astype(F32)
        g_ref[...] = g
        d_ref[...], mo_ref[...], vo_ref[...] = _adamw_math(w_ref[...], g, m_ref[...], v_ref[...])

    blk = pl.BlockSpec((tr, cdim), lambda i: (i, 0))
    return pl.pallas_call(
        body, name=name, grid=(r // tr,), out_shape=[jax.ShapeDtypeStruct((r, cdim), F32)] * 4,
        in_specs=[pl.BlockSpec((k, tr, cdim), lambda i: (0, i, 0)), blk, blk, blk], out_specs=[blk] * 4,
        compiler_params=_cparams("arbitrary"),
    )(slots, w, m, v)


def _stack_cols(g):
    n, r, c = g.shape
    return g.transpose(1, 0, 2).reshape(r, n * c)


def _split_cols(w):
    r, nc = w.shape
    return w.reshape(r, N_DEV, nc // N_DEV).transpose(1, 0, 2)


def _rope_tables(positions):
    inv_freq = ROPE_THETA ** (-jnp.arange(0, MLA_ROPE, 2, dtype=F32) / MLA_ROPE)
    ang = positions.astype(F32).reshape(-1, 1) * inv_freq[None, :]
    cos, sin, zero = jnp.cos(ang), jnp.sin(ang), jnp.zeros_like(ang)
    reps = LANES // MLA_ROPE
    return (jnp.tile(jnp.concatenate([cos, cos], axis=1), (1, reps)),
            jnp.tile(jnp.concatenate([-sin, zero], axis=1), (1, reps)),
            jnp.tile(jnp.concatenate([zero, sin], axis=1), (1, reps)))


def _local_step(x, positions, loss_target, gains, g_in, g_uq, g_ukv, late_shards):
    norm_mix, q_norm, kv_norm, out_mla, out_sb, norm_ffn, norm_final = gains
    d = x.shape[1]
    w_in = _stack_cols(g_in)
    w_a = jnp.concatenate([w_in[:, :416], jnp.zeros((d, 96), BF16), w_in[:, 416:]], axis=1)
    w_uq = jnp.concatenate([g_uq[:, :, :MLA_NOPE].transpose(1, 0, 2).reshape(Q_LORA, -1),
                            g_uq[:, :, MLA_NOPE:].transpose(1, 0, 2).reshape(Q_LORA, -1)], axis=1)
    w_ukv = jnp.concatenate([g_ukv[:, :, :MLA_NOPE].transpose(1, 0, 2).reshape(KV_LORA, -1),
                             g_ukv[:, :, MLA_NOPE:].transpose(1, 0, 2).reshape(KV_LORA, -1)], axis=1)
    cos, sin_a, sin_b = _rope_tables(positions)

    u, cq, ckv, cqn, ckvn, qn, qr, kv, kr, qkv_sb = _proj_in_fwd(x, norm_mix, w_a, q_norm, w_uq, kv_norm, w_ukv, cos, sin_a, sin_b)
    o_mla, lse, (g_o, g_gate, g_up, g_down) = _mla_fwd(qn, qr, kv, kr, late_shards)
    w_o = g_o.reshape(-1, d)
    w_gate_t, w_up_t = g_gate.reshape(-1, d), g_up.reshape(-1, d)
    w_down = g_down.reshape(-1, d)
    o_sb, tot, swept = _sb_fwd(qkv_sb)
    merged, h1, f = _attn_out_fwd(o_mla, o_sb, out_mla, out_sb, w_o, x, norm_ffn)
    gate, up, h2 = _ffn_fwd(f, h1, w_gate_t, w_up_t, w_down)
    loss, dh2, dh2b, dg_final = _final_loss(h2, loss_target, norm_final.reshape(1, d))

    dgate, dup, act, dh1, dh1b, dg_ffn = _ffn_bwd(dh2, dh2b, gate, up, h1, norm_ffn, w_down, w_gate_t, w_up_t)
    dw_down = _tn_matmul(act, dh2b, "dw_down")
    dw_gate_t, dw_up_t = _tn_matmul_pair(dgate, dup, f, "dw_gate_up")
    do_mla, do_sb, dg_mla, dg_sb = _attn_out_bwd(dh1b, w_o, o_mla, o_sb, out_mla, out_sb)
    dw_o = _tn_matmul(merged, dh1b, "dw_o")
    dq_sb, dk_sb, dv_sb = _sb_bwd(qkv_sb, do_sb, tot, swept)
    early = [g.reshape(N_DEV, -1, d) for g in (dw_o, dw_gate_t, dw_up_t, dw_down)]
    (dqn, dqr, dkn, dv, dkr), landed = _mla_bwd(qn, qr, kv, kr, do_mla, o_mla, lse, early)
    landed = [landed[0], landed[1].transpose(0, 2, 1), landed[2].transpose(0, 2, 1), landed[3]]
    dx, dproj, dq, dkv, dg_q, dg_kv, dg_mix = _proj_in_bwd(
        dqn, dqr, dkn, dv, dkr, dq_sb, dk_sb, dv_sb, cq, ckv, x, dh1, cos, sin_a, sin_b,
        q_norm, kv_norm, norm_mix, w_uq, w_ukv, w_a)
    dw_a = _tn_matmul(u, dproj, "dw_in")
    dw_uq = _tn_matmul(cqn, dq, "dw_uq")
    dw_ukv = _tn_matmul(ckvn, dkv, "dw_ukv")

    p_in = _split_cols(jnp.concatenate([dw_a[:, :416], dw_a[:, 512:]], axis=1))
    p_uq = jnp.concatenate([dw_uq[:, :512].reshape(Q_LORA, MLA_HEADS, MLA_NOPE),
                            dw_uq[:, 512:].reshape(Q_LORA, MLA_HEADS, MLA_ROPE)], axis=2).transpose(1, 0, 2)
    p_ukv = jnp.concatenate([dw_ukv[:, :512].reshape(KV_LORA, MLA_HEADS, MLA_NOPE),
                             dw_ukv[:, 512:].reshape(KV_LORA, MLA_HEADS, HEAD_DIM)], axis=2).transpose(1, 0, 2)
    late = [p_in, p_uq, p_ukv]
    gain_grads = [dg_mix, dg_q, dg_kv, dg_mla, dg_sb, dg_ffn, dg_final]
    return loss, dx, list(landed), late, gain_grads


def kernel(x, positions, norm_mix, w_in, q_latent_norm, w_uq, kv_latent_norm, w_ukv, out_norm_mla, out_norm_sb, w_o, norm_ffn, w_gate, w_up, w_down, norm_final, loss_target, m_norm_mix, m_w_in, m_q_latent_norm, m_w_uq, m_kv_latent_norm, m_w_ukv, m_out_norm_mla, m_out_norm_sb, m_w_o, m_norm_ffn, m_w_gate, m_w_up, m_w_down, m_norm_final, v_norm_mix, v_w_in, v_q_latent_norm, v_w_uq, v_kv_latent_norm, v_w_ukv, v_out_norm_mla, v_out_norm_sb, v_w_o, v_norm_ffn, v_w_gate, v_w_up, v_w_down, v_norm_final):
    mats = [w_in, w_uq, w_ukv, w_o, w_gate, w_up, w_down]
    mat_m = [m_w_in, m_w_uq, m_w_ukv, m_w_o, m_w_gate, m_w_up, m_w_down]
    mat_v = [v_w_in, v_w_uq, v_w_ukv, v_w_o, v_w_gate, v_w_up, v_w_down]
    mat_names = ["w_in", "w_uq", "w_ukv", "w_o", "w_gate", "w_up", "w_down"]
    gains = [norm_mix, q_latent_norm, kv_latent_norm, out_norm_mla, out_norm_sb, norm_ffn, norm_final]
    gain_m = [m_norm_mix, m_q_latent_norm, m_kv_latent_norm, m_out_norm_mla, m_out_norm_sb, m_norm_ffn, m_norm_final]
    gain_v = [v_norm_mix, v_q_latent_norm, v_kv_latent_norm, v_out_norm_mla, v_out_norm_sb, v_norm_ffn, v_norm_final]

    shards = [w[0].astype(BF16) for w in mats]
    shards[4], shards[5] = shards[4].T, shards[5].T
    g_in, g_uq, g_ukv = _all_gather(shards[:3], "weight_all_gather")

    gains2d = [g.reshape(1, -1) for g in gains]
    loss_part, dx, landed, late, gain_grads = _local_step(
        x[0], positions[0], loss_target[0], gains2d, g_in, g_uq, g_ukv, shards[3:])

    sizes = [g.size for g in gains]
    used = sum(sizes) + LANES
    rows = -(-used // (8 * LANES)) * 8

    def pack(vals, tail):
        flat = jnp.concatenate([v.reshape(-1) for v in vals] + [tail])
        return jnp.pad(flat, (0, rows * LANES - flat.size)).reshape(rows, LANES)

    (small,), scattered = _exchange([pack(gain_grads, loss_part.reshape(-1))], late, "grad_exchange")

    mat_out = [_adamw(sl, w[0], m[0], v[0], "adamw_" + nm)
               for sl, w, m, v, nm in zip(list(scattered) + landed, mats, mat_m, mat_v, mat_names)]
    zeros_tail = jnp.zeros((LANES,), F32)
    g_s, d_s, m_s, v_s = _adamw(small, pack(gains, zeros_tail), pack(gain_m, zeros_tail), pack(gain_v, zeros_tail), "adamw_gains")

    def unpack(packed):
        flat = packed.reshape(-1)
        outs, off = [], 0
        for g, n in zip(gains, sizes):
            outs.append(flat[off:off + n].reshape(g.shape))
            off += n
        return outs

    loss = g_s.reshape(-1)[sum(sizes)]

    order = ["norm_mix", "w_in", "q_latent_norm", "w_uq", "kv_latent_norm", "w_ukv", "out_norm_mla", "out_norm_sb",
             "w_o", "norm_ffn", "w_gate", "w_up", "w_down", "norm_final"]
    gain_names = ["norm_mix", "q_latent_norm", "kv_latent_norm", "out_norm_mla", "out_norm_sb", "norm_ffn", "norm_final"]
    result = [loss, dx[None]]
    for kind in range(4):
        small_parts = dict(zip(gain_names, unpack([g_s, d_s, m_s, v_s][kind])))
        mat_parts = {nm: out[kind][None] for nm, out in zip(mat_names, mat_out)}
        result += [small_parts[nm] if nm in small_parts else mat_parts[nm] for nm in order]
    return tuple(result)
```
